```python
import math
import jax, jax.numpy as jnp
from jax import lax
import numpy as np

D_MODEL = 1024
BATCH = 32
SEQ = 256
DEPTH = 2
DEC_BATCH = 2
DEC_SEQ = 1024
PAST_LEN = 256

GRID_W = 64
HEAD_DIM = 64
ATT_WIDTH = D_MODEL // 2
SGU_WIDTH = D_MODEL // 4
LRU_WIDTH = D_MODEL // 4
MIX_WIDTH = ATT_WIDTH + SGU_WIDTH + LRU_WIDTH
N_ATT_HEADS = ATT_WIDTH // HEAD_DIM
ATT_HALF = HEAD_DIM // 2
ROPE_FREQS = ATT_HALF // 4
ROPE_THETA = 10000.0
Q_BLOCK = 128
CHUNK = 128
N_SGU_GROUPS = 4
SGU_GROUP = SGU_WIDTH // N_SGU_GROUPS
N_LRU_BLOCKS = 4
LRU_BLOCK = LRU_WIDTH // N_LRU_BLOCKS
CONV_W = 4
LRU_C = 8.0
IN_SPLITS = [ATT_WIDTH, 2 * ATT_WIDTH, 3 * ATT_WIDTH, 3 * ATT_WIDTH + SGU_WIDTH,
             3 * ATT_WIDTH + 2 * SGU_WIDTH, 3 * ATT_WIDTH + 2 * SGU_WIDTH + LRU_WIDTH]
IN_COLS = 3 * ATT_WIDTH + 2 * SGU_WIDTH + 2 * LRU_WIDTH
N_EXPERTS = 32
TOP_K = 4
D_EXPERT = D_MODEL
EXPERT_BLOCK = 128
SWIGLU_LIMIT = 7.0
SWIGLU_ALPHA = 1.702
DN_ALPHA = (2 * DEPTH) ** 0.25
DN_BETA = (8 * DEPTH) ** -0.25
EPS = 1e-5

kernel_name = 'hybrid_diffusion_prefix_step'

F32 = jnp.float32


def _layer_norm(x, g, b):
    xf = x.astype(F32)
    mu = jnp.mean(xf, axis=-1, keepdims=True)
    var = jnp.mean(jnp.square(xf - mu), axis=-1, keepdims=True)
    return ((xf - mu) * lax.rsqrt(var + EPS) * g + b).astype(x.dtype)


def _rms_norm(x, g):
    xf = x.astype(F32)
    return (xf * lax.rsqrt(jnp.mean(xf * xf, axis=-1, keepdims=True) + EPS) * g).astype(x.dtype)


def _grid_angles(n_tok):
    n_rows = n_tok // GRID_W
    row = jnp.repeat(jnp.arange(n_rows), GRID_W).astype(F32)
    col = jnp.tile(jnp.arange(GRID_W), n_rows).astype(F32)
    inv = ROPE_THETA ** (-jnp.arange(ROPE_FREQS, dtype=F32) / ROPE_FREQS)
    return row[:, None] * inv, col[:, None] * inv


def _rotate(x, ang):
    c = jnp.cos(ang)[:, None, None, :]
    s = jnp.sin(ang)[:, None, None, :]
    x1, x2 = jnp.split(x, 2, axis=-1)
    return jnp.concatenate([x1 * c - x2 * s, x1 * s + x2 * c], axis=-1).astype(x.dtype)


def _axial_rope(x):
    ang_r, ang_c = _grid_angles(x.shape[1])
    xr, xc = jnp.split(x, 2, axis=-1)
    return jnp.concatenate([_rotate(xr, ang_r), _rotate(xc, ang_c)], axis=-1)


def _diff_attention(q, k, v, lam):
    B, H, Lq = q.shape[:3]
    nb = Lq // Q_BLOCK
    qb = jnp.moveaxis(q.reshape(B, H, nb, Q_BLOCK, 2, ATT_HALF), 2, 0)
    scale = ATT_HALF ** -0.5

    def one_block(qi):
        s = jnp.einsum('bhqcd,bhkcd->cbhqk', qi, k).astype(F32) * scale
        p = jax.nn.softmax(s, axis=-1)
        w = p[0] - lam * p[1]
        return jnp.einsum('bhqk,bhkd->bhqd', w.astype(v.dtype), v)

    o = lax.map(one_block, qb)
    return jnp.moveaxis(o, 0, 2).reshape(B, H, Lq, HEAD_DIM)


def _chunk_sgu(u, v, g, b, w_s, b_s):
    B, L, _ = u.shape
    vn = _layer_norm(v, g, b).reshape(B, L // CHUNK, CHUNK, N_SGU_GROUPS, SGU_GROUP)
    s = jnp.einsum('gpq,bnqgc->bnpgc', w_s, vn) + b_s.T[:, :, None]
    return u * s.reshape(B, L, SGU_WIDTH)


def _centred_conv(x, w, b):
    L = x.shape[1]
    left = CONV_W // 2
    xp = jnp.pad(x, ((0, 0), (left, CONV_W - 1 - left), (0, 0)))
    return sum(xp[:, j:j + L] * w[j] for j in range(CONV_W)) + b


def _linear_scan(a, b, h0):
    def comb(l, r):
        return l[0] * r[0], r[0] * l[1] + r[1]
    A, Bc = lax.associative_scan(comb, (a, b), axis=1)
    return A * h0[:, None, :] + Bc


def _rg_lru(x, gate, conv_w, conv_b, w_rg, b_rg, lru_log, h0):
    B, L, _ = x.shape
    xc = _centred_conv(x, conv_w, conv_b)
    xg = xc.reshape(B, L, N_LRU_BLOCKS, LRU_BLOCK)
    pre = jnp.einsum('blgi,dkgio->dkblgo', xg, w_rg).reshape(2, 2, B, L, LRU_WIDTH)
    gates = jax.nn.sigmoid((pre + b_rg[:, :, None, None, :]).astype(F32))
    r, i = gates[:, 0], gates[:, 1]
    log_a = -LRU_C * r * jax.nn.softplus(-lru_log.astype(F32))[:, None, None, :]
    a = jnp.exp(log_a)
    bx = jnp.sqrt(-jnp.expm1(2.0 * log_a)) * i * xc.astype(F32)[None]
    h0f = h0.astype(F32)
    hf = _linear_scan(a[0], bx[0], h0f[:, 0])
    hb = jnp.flip(_linear_scan(jnp.flip(a[1], 1), jnp.flip(bx[1], 1), h0f[:, 1]), 1)
    y = (hf + hb).astype(x.dtype) * jax.nn.gelu(gate)
    return y, jnp.stack([hf[:, -1], hb[:, 0]], axis=1)


def _moe(x, w_router, b_router, w_gu, b_gu, w_down, b_down):
    N, D = x.shape
    logits = (x @ w_router + b_router).astype(F32)
    top_val, top_idx = lax.top_k(logits, TOP_K)
    gates = jax.nn.softmax(top_val, axis=-1).astype(x.dtype)
    n_assign = N * TOP_K
    flat_e = top_idx.reshape(-1)
    order = jnp.argsort(flat_e)
    se = flat_e[order]
    st = (order // TOP_K).astype(jnp.int32)
    sg = gates.reshape(-1)[order]
    counts = jnp.zeros((N_EXPERTS,), jnp.int32).at[flat_e].add(1)
    starts = jnp.cumsum(counts) - counts
    padded = (counts + EXPERT_BLOCK - 1) // EXPERT_BLOCK * EXPERT_BLOCK
    pends = jnp.cumsum(padded)
    pstarts = pends - padded
    dest = pstarts[se] + jnp.arange(n_assign, dtype=jnp.int32) - starts[se]
    n_blocks = -(-n_assign // EXPERT_BLOCK) + N_EXPERTS
    n_slots = n_blocks * EXPERT_BLOCK
    slot_tok = jnp.full((n_slots,), N, jnp.int32).at[dest].set(st)
    slot_gate = jnp.zeros((n_slots,), x.dtype).at[dest].set(sg)
    block_start = jnp.arange(n_blocks, dtype=jnp.int32) * EXPERT_BLOCK
    block_e = jnp.minimum(jnp.sum(block_start[:, None] >= pends[None, :], axis=1), N_EXPERTS - 1)
    x_pad = jnp.concatenate([x, jnp.zeros((1, D), x.dtype)], axis=0)

    def expert_block(args):
        tok, e = args
        h = x_pad[tok] @ w_gu[e] + b_gu[e]
        g, u = jnp.split(h, 2, axis=-1)
        g = jnp.minimum(g, SWIGLU_LIMIT)
        u = jnp.clip(u, -SWIGLU_LIMIT, SWIGLU_LIMIT)
        return ((u + 1) * (g * jax.nn.sigmoid(SWIGLU_ALPHA * g))) @ w_down[e] + b_down[e]

    out = lax.map(expert_block, (slot_tok.reshape(n_blocks, EXPERT_BLOCK), block_e))
    out = out.reshape(n_slots, D) * slot_gate[:, None]
    return jnp.zeros((N + 1, D), x.dtype).at[slot_tok].add(out)[:N]


def _layer(x, cvec, lp, li, ctx):
    B, L, _ = x.shape
    lam_init = 0.8 - 0.6 * math.exp(-0.3 * li)
    mod = (jax.nn.silu(cvec) @ lp['w_mod'] + lp['b_mod'])[:, None, :]
    sh1, sc1, g1, sh2, sc2, g2 = jnp.split(mod, 6, axis=-1)
    h = x * (1 + sc1) + sh1
    q, k, v, su, sv, rx, rg = jnp.split(h @ lp['w_in'], IN_SPLITS, axis=-1)
    q = q.reshape(B, L, N_ATT_HEADS, 2, ATT_HALF)
    k = k.reshape(B, L, N_ATT_HEADS, 2, ATT_HALF)
    v = v.reshape(B, L, N_ATT_HEADS, HEAD_DIM).transpose(0, 2, 1, 3)
    if ctx is not None:
        q = _axial_rope(q)
        k = _axial_rope(k)
    q = q.transpose(0, 2, 1, 3, 4)
    k = k.transpose(0, 2, 1, 3, 4)
    if ctx is None:
        k_all, v_all = k, v
        h0 = jnp.zeros((B, 2, LRU_WIDTH), F32)
    else:
        ck, cv, h0 = ctx
        P = ck.shape[2]
        k_all = jnp.concatenate([ck.reshape(B, N_ATT_HEADS, P, 2, ATT_HALF).astype(k.dtype), k], axis=2)
        v_all = jnp.concatenate([cv.astype(v.dtype), v], axis=2)
    lm = lp['lam'].astype(F32)
    lam = jnp.exp(jnp.sum(lm[0] * lm[1])) - jnp.exp(jnp.sum(lm[2] * lm[3])) + lam_init
    att = _diff_attention(q, k_all, v_all, lam)
    att = (_rms_norm(att, lp['subln_g']) * (1 - lam_init)).transpose(0, 2, 1, 3).reshape(B, L, ATT_WIDTH)
    sgu = _chunk_sgu(su, sv, lp['sgu_ln_g'], lp['sgu_ln_b'], lp['w_spatial'], lp['b_spatial'])
    lru, h_last = _rg_lru(rx, rg, lp['conv_w'], lp['conv_b'], lp['w_rg'], lp['b_rg'], lp['lru_log'], h0)
    mix = jnp.concatenate([att, sgu.astype(att.dtype), lru.astype(att.dtype)], axis=-1) @ lp['w_out']
    x = _layer_norm(DN_ALPHA * x + g1 * mix, lp['ln_g'][0], lp['ln_b'][0])
    h = x * (1 + sc2) + sh2
    ffn = _moe(h.reshape(B * L, D_MODEL), lp['w_router'], lp['b_router'], lp['w_gu'], lp['b_gu'],
               lp['w_down'], lp['b_down']).reshape(B, L, D_MODEL)
    x = _layer_norm(DN_ALPHA * x + g2 * ffn, lp['ln_g'][1], lp['ln_b'][1])
    if ctx is None:
        return x, (k.reshape(B, N_ATT_HEADS, L, HEAD_DIM), v, h_last)
    return x, None


def setup_inputs(seed: int = 0) -> dict:
    key = jax.random.key(seed)
    ks = jax.random.split(key, 40)
    D = D_MODEL

    def nrm(k, shape, s):
        return jax.random.normal(k, shape, F32) * s

    a_pow = jax.random.uniform(ks[20], (DEPTH, 2, LRU_WIDTH), F32, 0.9, 0.999)
    a0 = a_pow ** (1.0 / LRU_C)
    return {
        'x_prompt': nrm(ks[0], (BATCH, SEQ, D), 1.0),
        'x_sample': nrm(ks[1], (DEC_BATCH, DEC_SEQ, D), 1.0),
        'cache_k': nrm(ks[2], (DEC_BATCH, DEPTH, N_ATT_HEADS, PAST_LEN, HEAD_DIM), 1.0),
        'cache_v': nrm(ks[3], (DEC_BATCH, DEPTH, N_ATT_HEADS, PAST_LEN, HEAD_DIM), 1.0),
        'state_lru': nrm(ks[4], (DEC_BATCH, DEPTH, 2, LRU_WIDTH), 0.5),
        'c': nrm(ks[5], (DEC_BATCH, D), 1.0),
        'c_ctx': nrm(ks[6], (D,), 1.0),
        'w_mod': nrm(ks[7], (DEPTH, D, 6 * D), 0.5 * D ** -0.5),
        'b_mod': nrm(ks[8], (DEPTH, 6 * D), 0.02),
        'w_in': nrm(ks[9], (DEPTH, D, IN_COLS), D ** -0.5),
        'lam': nrm(ks[10], (DEPTH, 4, ATT_HALF), 0.1),
        'subln_g': 1.0 + nrm(ks[11], (DEPTH, HEAD_DIM), 0.02),
        'sgu_ln_g': 1.0 + nrm(ks[12], (DEPTH, SGU_WIDTH), 0.02),
        'sgu_ln_b': nrm(ks[13], (DEPTH, SGU_WIDTH), 0.02),
        'w_spatial': nrm(ks[14], (DEPTH, N_SGU_GROUPS, CHUNK, CHUNK), CHUNK ** -0.5),
        'b_spatial': 1.0 + nrm(ks[15], (DEPTH, N_SGU_GROUPS, CHUNK), 0.02),
        'conv_w': nrm(ks[16], (DEPTH, CONV_W, LRU_WIDTH), CONV_W ** -0.5),
        'conv_b': nrm(ks[17], (DEPTH, LRU_WIDTH), 0.02),
        'w_rg': nrm(ks[18], (DEPTH, 2, 2, N_LRU_BLOCKS, LRU_BLOCK, LRU_BLOCK), LRU_BLOCK ** -0.5),
        'b_rg': nrm(ks[19], (DEPTH, 2, 2, LRU_WIDTH), 0.02),
        'lru_log': jnp.log(a0) - jnp.log1p(-a0),
        'w_out': nrm(ks[21], (DEPTH, MIX_WIDTH, D), MIX_WIDTH ** -0.5 * DN_BETA),
        'ln_g': 1.0 + nrm(ks[22], (DEPTH, 2, D), 0.02),
        'ln_b': nrm(ks[23], (DEPTH, 2, D), 0.02),
        'w_router': nrm(ks[24], (DEPTH, D, N_EXPERTS), D ** -0.5),
        'b_router': nrm(ks[25], (DEPTH, N_EXPERTS), 0.01),
        'w_gu': nrm(ks[26], (DEPTH, N_EXPERTS, D, 2 * D_EXPERT), D ** -0.5),
        'b_gu': nrm(ks[27], (DEPTH, N_EXPERTS, 2 * D_EXPERT), 0.02),
        'w_down': nrm(ks[28], (DEPTH, N_EXPERTS, D_EXPERT, D), D_EXPERT ** -0.5 * DN_BETA),
        'b_down': nrm(ks[29], (DEPTH, N_EXPERTS, D), 0.02),
    }


def reference(x_prompt, x_sample, cache_k, cache_v, state_lru, c, c_ctx, w_mod, b_mod, w_in, lam,
              subln_g, sgu_ln_g, sgu_ln_b, w_spatial, b_spatial, conv_w, conv_b, w_rg, b_rg, lru_log,
              w_out, ln_g, ln_b, w_router, b_router, w_gu, b_gu, w_down, b_down):
    yp = x_prompt
    ys = x_sample
    new_k, new_v, new_s = [], [], []
    for li in range(DEPTH):
        lp = {'w_mod': w_mod[li], 'b_mod': b_mod[li], 'w_in': w_in[li], 'lam': lam[li],
              'subln_g': subln_g[li], 'sgu_ln_g': sgu_ln_g[li], 'sgu_ln_b': sgu_ln_b[li],
              'w_spatial': w_spatial[li], 'b_spatial': b_spatial[li], 'conv_w': conv_w[li],
              'conv_b': conv_b[li], 'w_rg': w_rg[li], 'b_rg': b_rg[li], 'lru_log': lru_log[li],
              'w_out': w_out[li], 'ln_g': ln_g[li], 'ln_b': ln_b[li], 'w_router': w_router[li],
              'b_router': b_router[li], 'w_gu': w_gu[li], 'b_gu': b_gu[li],
              'w_down': w_down[li], 'b_down': b_down[li]}
        yp, (kc, vc, hc) = _layer(yp, c_ctx[None, :], lp, li, None)
        new_k.append(kc)
        new_v.append(vc)
        new_s.append(hc)
        ys, _ = _layer(ys, c, lp, li, (cache_k[:, li], cache_v[:, li], state_lru[:, li]))
    return (yp, ys, jnp.stack(new_k, axis=1), jnp.stack(new_v, axis=1), jnp.stack(new_s, axis=1))
```

```python
import functools
import math

import numpy as np
import jax
import jax.numpy as jnp
from jax import lax
from jax.experimental import pallas as pl
from jax.experimental.pallas import tpu as pltpu

F32 = jnp.float32
BF16 = jnp.bfloat16

D_MODEL = 1024
BATCH = 32
SEQ = 256
DEPTH = 2
DEC_BATCH = 2
DEC_SEQ = 1024
PAST_LEN = 256
GRID_W = 64
HEAD_DIM = 64
ATT_WIDTH = D_MODEL // 2
SGU_WIDTH = D_MODEL // 4
LRU_WIDTH = D_MODEL // 4
N_ATT_HEADS = ATT_WIDTH // HEAD_DIM
ATT_HALF = HEAD_DIM // 2
ROPE_FREQS = ATT_HALF // 4
ROPE_THETA = 10000.0
CHUNK = 128
N_SGU_GROUPS = 4
SGU_GROUP = SGU_WIDTH // N_SGU_GROUPS
N_LRU_BLOCKS = 4
LRU_BLOCK = LRU_WIDTH // N_LRU_BLOCKS
CONV_W = 4
LRU_C = 8.0
IN_COLS = 3 * ATT_WIDTH + 2 * SGU_WIDTH + 2 * LRU_WIDTH
N_EXPERTS = 32
TOP_K = 4
D_EXPERT = D_MODEL
SWIGLU_LIMIT = 7.0
SWIGLU_ALPHA = 1.702
DN_ALPHA = (2 * DEPTH) ** 0.25
EPS = 1e-5

N_CTX = BATCH * SEQ
N_DEC = DEC_BATCH * DEC_SEQ
N_TOK = N_CTX + N_DEC

LANES = 128
SUBLANES = 8
ROW_SLABS = D_MODEL // LANES

TM = 512
CTX_TILES = N_CTX // TM
TILES_PER_DEC = DEC_SEQ // TM
N_TILES = N_TOK // TM
MOD_TN = 512
BM = 256
NB = N_TOK * TOP_K // BM + N_EXPERTS
N_SLOTS = NB * BM
TC = 256
GATHER_UNROLL = 8
VMEM_LIMIT = 56 * 1024 * 1024


def _cparams(n_axes):
    return pltpu.CompilerParams(
        dimension_semantics=("arbitrary",) * n_axes,
        vmem_limit_bytes=VMEM_LIMIT)


def _mod_row(i):
    return jnp.where(i < CTX_TILES, 0, 1 + (i - CTX_TILES) // TILES_PER_DEC)


def _layer_norm_rows(z, g, b):
    mu = jnp.mean(z, axis=-1, keepdims=True)
    zc = z - mu
    var = jnp.mean(zc * zc, axis=-1, keepdims=True)
    return zc * lax.rsqrt(var + EPS) * g + b


def _mod_kernel(cvec_ref, w_ref, b_ref, o_ref):
    cv = cvec_ref[...]
    s = cv * jax.nn.sigmoid(cv)
    s_t = s.T
    w = w_ref[0]
    rows = [jnp.sum(s_t[:, r:r + 1] * w, axis=0, keepdims=True) for r in range(1 + DEC_BATCH)]
    rows.append(jnp.zeros((SUBLANES - 1 - DEC_BATCH, MOD_TN), F32))
    o_ref[0] = jnp.concatenate(rows, axis=0) + b_ref[0]


def _modulation(cvec8, w_mod, b_mod):
    n_out = w_mod.shape[-1]
    return pl.pallas_call(
        _mod_kernel,
        grid=(DEPTH, n_out // MOD_TN),
        in_specs=[
            pl.BlockSpec((SUBLANES, D_MODEL), lambda l, j: (0, 0)),
            pl.BlockSpec((1, D_MODEL, MOD_TN), lambda l, j: (l, 0, j)),
            pl.BlockSpec((1, 1, MOD_TN), lambda l, j: (l, 0, j)),
        ],
        out_specs=pl.BlockSpec((1, SUBLANES, MOD_TN), lambda l, j: (l, 0, j)),
        out_shape=jax.ShapeDtypeStruct((DEPTH, SUBLANES, n_out), F32),
        compiler_params=_cparams(2),
        name="modulation",
    )(cvec8, w_mod, b_mod.reshape(DEPTH, 1, n_out))


def _inproj_kernel(x_ref, mod_ref, w_ref, o_ref, wbf_ref):
    i = pl.program_id(0)

    @pl.when(i == 0)
    def _():
        wbf_ref[...] = w_ref[...].astype(BF16)

    m = mod_ref[pl.ds(_mod_row(i), 1), :]
    sh1 = m[:, 0:D_MODEL]
    sc1 = m[:, D_MODEL:2 * D_MODEL]
    h = (x_ref[...] * (1.0 + sc1) + sh1).astype(BF16)
    o_ref[...] = jnp.dot(h, wbf_ref[...], preferred_element_type=F32)


def _input_projection(x, mod, w_in, li):
    return pl.pallas_call(
        _inproj_kernel,
        grid=(N_TILES,),
        in_specs=[
            pl.BlockSpec((TM, D_MODEL), lambda i: (i, 0)),
            pl.BlockSpec((None, SUBLANES, 6 * D_MODEL), lambda i: (li, 0, 0)),
            pl.BlockSpec((None, D_MODEL, IN_COLS), lambda i: (li, 0, 0), pipeline_mode=pl.Buffered(1)),
        ],
        out_specs=pl.BlockSpec((TM, IN_COLS), lambda i: (i, 0)),
        out_shape=jax.ShapeDtypeStruct((N_TOK, IN_COLS), F32),
        scratch_shapes=[pltpu.VMEM((D_MODEL, IN_COLS), BF16)],
        compiler_params=_cparams(1),
        name="input_projection",
    )(x, mod, w_in)


def _lambda(lam_ref, lam_init):
    lm = lam_ref[...]
    a = jnp.sum(lm[0:1] * lm[1:2], axis=-1, keepdims=True)
    b = jnp.sum(lm[2:3] * lm[3:4], axis=-1, keepdims=True)
    return jnp.exp(a) - jnp.exp(b) + lam_init


def _softmax_rows(s):
    e = jnp.exp(s - jnp.max(s, axis=-1, keepdims=True))
    return e / jnp.sum(e, axis=-1, keepdims=True)


def _sub_ln(o, g, lam_init):
    ms = jnp.mean(o * o, axis=-1, keepdims=True)
    return o * lax.rsqrt(ms + EPS) * g * (1.0 - lam_init)


_NT = (((1,), (1,)), ((), ()))
_ATT_SCALE = ATT_HALF ** -0.5


def _ctx_attn_kernel(lam_ref, g_ref, q_ref, k_ref, v_ref, att_ref, ck_ref, cv_ref, *, lam_init):
    lam = _lambda(lam_ref, lam_init)
    k = k_ref[...]
    v = v_ref[...]
    qb = q_ref[...].astype(BF16)
    kb = k.astype(BF16)
    vb = v.astype(BF16)
    outs = []
    for h in range(N_ATT_HEADS):
        lo = h * HEAD_DIM
        ps = []
        for c in range(2):
            a = lo + c * ATT_HALF
            s = lax.dot_general(qb[:, a:a + ATT_HALF], kb[:, a:a + ATT_HALF], _NT,
                                preferred_element_type=F32) * _ATT_SCALE
            ps.append(_softmax_rows(s))
        w = ps[0] - lam * ps[1]
        o = jnp.dot(w.astype(BF16), vb[:, lo:lo + HEAD_DIM], preferred_element_type=F32)
        outs.append(_sub_ln(o, g_ref[...], lam_init))
        ck_ref[0, h] = k[:, lo:lo + HEAD_DIM]
        cv_ref[0, h] = v[:, lo:lo + HEAD_DIM]
    att_ref[...] = jnp.concatenate(outs, axis=-1)


def _context_attention(proj, lam, subln_g, lam_init):
    kv_shape = jax.ShapeDtypeStruct((BATCH, N_ATT_HEADS, SEQ, HEAD_DIM), F32)
    kv_spec = pl.BlockSpec((1, N_ATT_HEADS, SEQ, HEAD_DIM), lambda b: (b, 0, 0, 0))
    return pl.pallas_call(
        functools.partial(_ctx_attn_kernel, lam_init=lam_init),
        grid=(BATCH,),
        in_specs=[
            pl.BlockSpec((4, ATT_HALF), lambda b: (0, 0)),
            pl.BlockSpec((1, HEAD_DIM), lambda b: (0, 0)),
            pl.BlockSpec((SEQ, ATT_WIDTH), lambda b: (b, 0)),
            pl.BlockSpec((SEQ, ATT_WIDTH), lambda b: (b, 1)),
            pl.BlockSpec((SEQ, ATT_WIDTH), lambda b: (b, 2)),
        ],
        out_specs=[pl.BlockSpec((SEQ, ATT_WIDTH), lambda b: (b, 0)), kv_spec, kv_spec],
        out_shape=[jax.ShapeDtypeStruct((N_CTX, ATT_WIDTH), F32), kv_shape, kv_shape],
        compiler_params=_cparams(1),
        name="context_attention",
    )(lam, subln_g.reshape(1, HEAD_DIM), proj, proj, proj)


def _rope_tables():
    t = np.arange(DEC_SEQ)
    pos = np.stack([t // GRID_W, t % GRID_W], axis=1).astype(np.float32)
    inv = (np.float32(ROPE_THETA) ** (-np.arange(ROPE_FREQS, dtype=np.float32) / np.float32(ROPE_FREQS)))
    j = np.arange(HEAD_DIM)
    d = j % ATT_HALF
    axis = d // (2 * ROPE_FREQS)
    u = d % (2 * ROPE_FREQS)
    ang = pos[:, axis] * inv[u % ROPE_FREQS][None, :].astype(np.float32)
    cos = np.cos(ang).astype(np.float32)
    sin = np.sin(ang).astype(np.float32)
    first = (u < ROPE_FREQS)[None, :]
    s_next = np.where(first, -sin, 0.0).astype(np.float32)
    s_prev = np.where(first, 0.0, sin).astype(np.float32)
    tile = lambda a: jnp.asarray(np.tile(a, (1, N_ATT_HEADS)))
    return tile(cos), tile(s_next), tile(s_prev)


def _rotate(x, cos, s_next, s_prev):
    width = x.shape[-1]
    return (x * cos + pltpu.roll(x, width - ROPE_FREQS, axis=1) * s_next
            + pltpu.roll(x, ROPE_FREQS, axis=1) * s_prev)


QB = 256
Q_STEPS = DEC_SEQ // QB


def _dec_attn_kernel(lam_ref, g_ref, q_ref, k_ref, v_ref, ck_ref, cv_ref,
                     cq_ref, snq_ref, spq_ref, ck_tab, snk_tab, spk_tab,
                     att_ref, krot_ref, *, lam_init):
    j = pl.program_id(1)

    @pl.when(j == 0)
    def _():
        krot_ref[...] = _rotate(k_ref[...], ck_tab[...], snk_tab[...], spk_tab[...]).astype(BF16)

    lam = _lambda(lam_ref, lam_init)
    qb = _rotate(q_ref[...], cq_ref[...], snq_ref[...], spq_ref[...]).astype(BF16)
    kb = krot_ref[...]
    vb = v_ref[...].astype(BF16)
    outs = []
    for h in range(N_ATT_HEADS):
        lo = h * HEAD_DIM
        pk = ck_ref[0, 0, h].astype(BF16)
        pv = cv_ref[0, 0, h].astype(BF16)
        ps = []
        for c in range(2):
            a = lo + c * ATT_HALF
            qs = qb[:, a:a + ATT_HALF]
            s_past = lax.dot_general(qs, pk[:, c * ATT_HALF:(c + 1) * ATT_HALF], _NT,
                                     preferred_element_type=F32)
            s_new = lax.dot_general(qs, kb[:, a:a + ATT_HALF], _NT, preferred_element_type=F32)
            ps.append(_softmax_rows(jnp.concatenate([s_past, s_new], axis=-1) * _ATT_SCALE))
        w = (ps[0] - lam * ps[1]).astype(BF16)
        o = (jnp.dot(w[:, :PAST_LEN], pv, preferred_element_type=F32)
             + jnp.dot(w[:, PAST_LEN:], vb[:, lo:lo + HEAD_DIM], preferred_element_type=F32))
        outs.append(_sub_ln(o, g_ref[...], lam_init))
    att_ref[...] = jnp.concatenate(outs, axis=-1)


def _denoise_attention(proj, cache_k, cache_v, lam, subln_g, li, lam_init, tables):
    cos, s_next, s_prev = tables
    row0 = N_CTX // QB
    seq0 = N_CTX // DEC_SEQ
    q_tab = pl.BlockSpec((QB, ATT_WIDTH), lambda b, j: (j, 0))
    k_tab = pl.BlockSpec((DEC_SEQ, ATT_WIDTH), lambda b, j: (0, 0))
    cache_spec = pl.BlockSpec((1, 1, N_ATT_HEADS, PAST_LEN, HEAD_DIM), lambda b, j: (b, li, 0, 0, 0))
    return pl.pallas_call(
        functools.partial(_dec_attn_kernel, lam_init=lam_init),
        grid=(DEC_BATCH, Q_STEPS),
        in_specs=[
            pl.BlockSpec((4, ATT_HALF), lambda b, j: (0, 0)),
            pl.BlockSpec((1, HEAD_DIM), lambda b, j: (0, 0)),
            pl.BlockSpec((QB, ATT_WIDTH), lambda b, j: (row0 + b * Q_STEPS + j, 0)),
            pl.BlockSpec((DEC_SEQ, ATT_WIDTH), lambda b, j: (seq0 + b, 1)),
            pl.BlockSpec((DEC_SEQ, ATT_WIDTH), lambda b, j: (seq0 + b, 2)),
            cache_spec, cache_spec,
            q_tab, q_tab, q_tab, k_tab, k_tab, k_tab,
        ],
        out_specs=pl.BlockSpec((QB, ATT_WIDTH), lambda b, j: (b * Q_STEPS + j, 0)),
        out_shape=jax.ShapeDtypeStruct((N_DEC, ATT_WIDTH), F32),
        scratch_shapes=[pltpu.VMEM((DEC_SEQ, ATT_WIDTH), BF16)],
        compiler_params=_cparams(2),
        name="denoise_attention",
    )(lam, subln_g.reshape(1, HEAD_DIM), proj, proj, proj, cache_k, cache_v,
      cos, s_next, s_prev, cos, s_next, s_prev)


def _softplus(z):
    return jnp.maximum(z, 0.0) + jnp.log1p(jnp.exp(-jnp.abs(z)))


def _mixer_kernel(su_ref, sv_ref, rx_ref, rg_ref, h0_ref, lng_ref, lnb_ref, ws_ref, bs_ref,
                  cw_ref, cb_ref, wrg_ref, brg_ref, lrulog_ref,
                  sgu_ref, lru_ref, hlast_ref, a_scr, b_scr, h_scr, *, seq_len):
    vn = _layer_norm_rows(sv_ref[...], lng_ref[...], lnb_ref[...])
    lane_group = lax.broadcasted_iota(jnp.int32, (CHUNK, SGU_WIDTH), 1) // SGU_GROUP
    for n in range(seq_len // CHUNK):
        rows = slice(n * CHUNK, (n + 1) * CHUNK)
        vc = vn[rows].astype(BF16)
        s = jnp.zeros((CHUNK, SGU_WIDTH), F32)
        for g in range(N_SGU_GROUPS):
            sg = jnp.dot(ws_ref[g].astype(BF16), vc, preferred_element_type=F32)
            s = jnp.where(lane_group == g, sg, s)
        sgu_ref[rows, :] = su_ref[rows, :] * (s + bs_ref[...])

    x = rx_ref[...]
    row = lax.broadcasted_iota(jnp.int32, (seq_len, LRU_WIDTH), 0)

    def shifted(val, d, fill):
        rolled = pltpu.roll(val, d % seq_len, axis=0)
        inside = (row >= d) if d > 0 else (row < seq_len + d)
        return jnp.where(inside, rolled, fill)

    left = CONV_W // 2
    xc = cb_ref[...] + x * cw_ref[left:left + 1, :]
    for tap in range(CONV_W):
        if tap != left:
            xc = xc + shifted(x, left - tap, 0.0) * cw_ref[tap:tap + 1, :]
    gates = jax.nn.sigmoid(jnp.dot(xc.astype(BF16), wrg_ref[...].astype(BF16),
                                   preferred_element_type=F32) + brg_ref[...])
    in_chunk = row % SUBLANES
    n_chunks = seq_len // SUBLANES
    h0 = h0_ref[0, 0]
    lasts = []
    for direction in range(2):
        reverse = direction == 1
        base = direction * 2 * LRU_WIDTH
        r = gates[:, base:base + LRU_WIDTH]
        gi = gates[:, base + LRU_WIDTH:base + 2 * LRU_WIDTH]
        log_a = -LRU_C * r * _softplus(-lrulog_ref[direction:direction + 1, :])
        a = jnp.exp(log_a)
        b = jnp.sqrt(-jnp.tanh(log_a) * (a * a + 1.0)) * gi * xc
        for d in (1, 2, 4):
            if reverse:
                inside = in_chunk < SUBLANES - d
                a_n = jnp.where(inside, pltpu.roll(a, seq_len - d, axis=0), 1.0)
                b_n = jnp.where(inside, pltpu.roll(b, seq_len - d, axis=0), 0.0)
            else:
                inside = in_chunk >= d
                a_n = jnp.where(inside, pltpu.roll(a, d, axis=0), 1.0)
                b_n = jnp.where(inside, pltpu.roll(b, d, axis=0), 0.0)
            b = a * b_n + b
            a = a * a_n
        a_scr[...] = a
        b_scr[...] = b

        def chunk_step(c, carry, reverse=reverse):
            cc = n_chunks - 1 - c if reverse else c
            off = pl.multiple_of(cc * SUBLANES, SUBLANES)
            hc = a_scr[pl.ds(off, SUBLANES), :] * carry + b_scr[pl.ds(off, SUBLANES), :]
            if reverse:
                h_scr[pl.ds(off, SUBLANES), :] = h_scr[pl.ds(off, SUBLANES), :] + hc
                return hc[0:1, :]
            h_scr[pl.ds(off, SUBLANES), :] = hc
            return hc[SUBLANES - 1:SUBLANES, :]

        lasts.append(lax.fori_loop(0, n_chunks, chunk_step, h0[direction:direction + 1, :]))
    lru_ref[...] = h_scr[...] * jax.nn.gelu(rg_ref[...])
    hlast_ref[0] = jnp.concatenate(lasts, axis=0)


def _mixers(proj, h0, h0_layer, lp, seq_len, n_seq, row_block0):
    col0 = 3 * ATT_WIDTH // SGU_WIDTH
    col = lambda c: pl.BlockSpec((seq_len, SGU_WIDTH), lambda b: (row_block0 + b, col0 + c))
    full = lambda shape: pl.BlockSpec(shape, lambda b: (0,) * len(shape))
    out_rows = pl.BlockSpec((seq_len, SGU_WIDTH), lambda b: (b, 0))
    return pl.pallas_call(
        functools.partial(_mixer_kernel, seq_len=seq_len),
        grid=(n_seq,),
        in_specs=[
            col(0), col(1), col(2), col(3),
            pl.BlockSpec((1, 1, 2, LRU_WIDTH), lambda b: (b, h0_layer, 0, 0)),
            full((1, SGU_WIDTH)), full((1, SGU_WIDTH)),
            full((N_SGU_GROUPS, CHUNK, CHUNK)), full((CHUNK, SGU_WIDTH)),
            full((CONV_W, LRU_WIDTH)), full((1, LRU_WIDTH)),
            full((LRU_WIDTH, 4 * LRU_WIDTH)), full((1, 4 * LRU_WIDTH)),
            full((2, LRU_WIDTH)),
        ],
        out_specs=[out_rows, out_rows, pl.BlockSpec((1, 2, LRU_WIDTH), lambda b: (b, 0, 0))],
        out_shape=[
            jax.ShapeDtypeStruct((n_seq * seq_len, SGU_WIDTH), F32),
            jax.ShapeDtypeStruct((n_seq * seq_len, LRU_WIDTH), F32),
            jax.ShapeDtypeStruct((n_seq, 2, LRU_WIDTH), F32),
        ],
        scratch_shapes=[pltpu.VMEM((seq_len, LRU_WIDTH), F32)] * 3,
        compiler_params=_cparams(1),
        name="mixers_%d" % seq_len,
    )(proj, proj, proj, proj, h0, lp["sgu_ln_g"], lp["sgu_ln_b"], lp["w_spatial"], lp["b_spatial_full"],
      lp["conv_w"], lp["conv_b"], lp["w_rg_full"], lp["b_rg_full"], lp["lru_log"])


def _split_bf16(v):
    hi = v.astype(BF16)
    lo = (v - hi.astype(F32)).astype(BF16)
    return hi, lo


def _outproj_kernel(x_ref, att_ref, sgu_ref, lru_ref, mod_ref, wout_ref, lng_ref, lnb_ref, wr_ref, br_ref,
                    x1_ref, h2_ref, ri_ref, rf_ref, cnt_ref, wbf_ref, carry_ref):
    i = pl.program_id(0)

    @pl.when(i == 0)
    def _():
        wbf_ref[...] = wout_ref[...].astype(BF16)
        carry_ref[...] = jnp.zeros_like(carry_ref)

    m = mod_ref[pl.ds(_mod_row(i), 1), :]
    g1 = m[:, 2 * D_MODEL:3 * D_MODEL]
    sh2 = m[:, 3 * D_MODEL:4 * D_MODEL]
    sc2 = m[:, 4 * D_MODEL:5 * D_MODEL]
    a0, a1 = ATT_WIDTH, ATT_WIDTH + SGU_WIDTH
    mix = (jnp.dot(att_ref[...].astype(BF16), wbf_ref[0:a0, :], preferred_element_type=F32)
           + jnp.dot(sgu_ref[...].astype(BF16), wbf_ref[a0:a1, :], preferred_element_type=F32)
           + jnp.dot(lru_ref[...].astype(BF16), wbf_ref[a1:, :], preferred_element_type=F32))
    x1 = _layer_norm_rows(DN_ALPHA * x_ref[...] + g1 * mix, lng_ref[...], lnb_ref[...])
    x1_ref[...] = x1
    h2 = x1 * (1.0 + sc2) + sh2
    for s in range(ROW_SLABS):
        h2_ref[:, s, :] = h2[:, s * LANES:(s + 1) * LANES]

    h_hi, h_lo = _split_bf16(h2)
    w_hi, w_lo = _split_bf16(wr_ref[...])
    logits = (jnp.dot(h_hi, w_hi, preferred_element_type=F32)
              + jnp.dot(h_lo, w_hi, preferred_element_type=F32)
              + jnp.dot(h_hi, w_lo, preferred_element_type=F32)) + br_ref[...]
    lane = lax.broadcasted_iota(jnp.int32, (TM, LANES), 1)
    lane_f = lane.astype(F32)
    neg_inf = jnp.float32(-jnp.inf)
    work = jnp.where(lane < N_EXPERTS, logits, neg_inf)
    vals, idxs = [], []
    for _ in range(TOP_K):
        top = jnp.max(work, axis=-1, keepdims=True)
        idx = jnp.min(jnp.where(work == top, lane_f, float(LANES)), axis=-1, keepdims=True)
        vals.append(top)
        idxs.append(idx)
        work = jnp.where(lane_f == idx, neg_inf, work)
    exps = [jnp.exp(v - vals[0]) for v in vals]
    denom = exps[0] + exps[1] + exps[2] + exps[3]
    onehot = jnp.zeros((TM, LANES), F32)
    for idx in idxs:
        onehot = onehot + (lane_f == idx).astype(F32)
    r_i = lax.broadcasted_iota(jnp.int32, (TM, TM), 0)
    c_i = lax.broadcasted_iota(jnp.int32, (TM, TM), 1)
    tri = (r_i > c_i).astype(F32).astype(BF16)
    before = jnp.dot(tri, onehot.astype(BF16), preferred_element_type=F32) + carry_ref[0:1, :]
    ri = jnp.zeros((TM, LANES), F32)
    rf = jnp.zeros((TM, LANES), F32)
    for k in range(TOP_K):
        rank = jnp.sum(jnp.where(lane_f == idxs[k], before, 0.0), axis=-1, keepdims=True)
        ri = jnp.where(lane == k, idxs[k], ri)
        ri = jnp.where(lane == TOP_K + k, rank, ri)
        rf = jnp.where(lane == k, exps[k] / denom, rf)
    ri_ref[...] = ri.astype(jnp.int32)
    rf_ref[...] = rf
    total = carry_ref[0:1, :] + jnp.sum(onehot, axis=0, keepdims=True)
    carry_ref[...] = jnp.broadcast_to(total, carry_ref.shape)
    cnt_ref[...] = jnp.broadcast_to(total, cnt_ref.shape).astype(jnp.int32)


def _output_projection(x, att, sgu, lru, mod, lp, li):
    rows = lambda w: pl.BlockSpec((TM, w), lambda i: (i, 0))
    full = lambda shape: pl.BlockSpec(shape, lambda i: (0,) * len(shape))
    return pl.pallas_call(
        _outproj_kernel,
        grid=(N_TILES,),
        in_specs=[
            rows(D_MODEL), rows(ATT_WIDTH), rows(SGU_WIDTH), rows(LRU_WIDTH),
            pl.BlockSpec((None, SUBLANES, 6 * D_MODEL), lambda i: (li, 0, 0)),
            pl.BlockSpec((None, D_MODEL, D_MODEL), lambda i: (li, 0, 0), pipeline_mode=pl.Buffered(1)),
            full((1, D_MODEL)), full((1, D_MODEL)),
            full((D_MODEL, LANES)), full((1, LANES)),
        ],
        out_specs=[
            rows(D_MODEL),
            pl.BlockSpec((TM, ROW_SLABS, LANES), lambda i: (i, 0, 0)),
            rows(LANES), rows(LANES),
            pl.BlockSpec((SUBLANES, LANES), lambda i: (0, 0)),
        ],
        out_shape=[
            jax.ShapeDtypeStruct((N_TOK, D_MODEL), F32),
            jax.ShapeDtypeStruct((N_TOK, ROW_SLABS, LANES), F32),
            jax.ShapeDtypeStruct((N_TOK, LANES), jnp.int32),
            jax.ShapeDtypeStruct((N_TOK, LANES), F32),
            jax.ShapeDtypeStruct((SUBLANES, LANES), jnp.int32),
        ],
        scratch_shapes=[pltpu.VMEM((D_MODEL, D_MODEL), BF16), pltpu.VMEM((SUBLANES, LANES), F32)],
        compiler_params=_cparams(1),
        name="output_projection",
    )(x, att, sgu, lru, mod, lp["w_out_all"], lp["ln_g1"], lp["ln_b1"], lp["w_router_pad"], lp["b_router_pad"])


def _gather_rows_start(idx_ref, idx_base, src_hbm, buf, buf_row0, n_rows, sem):
    def chunk(c, carry):
        for u in range(GATHER_UNROLL):
            r = c * GATHER_UNROLL + u
            src_row = idx_ref[idx_base + r]
            dst = pl.multiple_of((buf_row0 + r) * ROW_SLABS, ROW_SLABS)
            pltpu.make_async_copy(src_hbm.at[src_row], buf.at[pl.ds(dst, ROW_SLABS), :], sem).start()
        return carry
    lax.fori_loop(0, n_rows // GATHER_UNROLL, chunk, 0)


def _gather_rows_wait(buf, buf_row0, n_rows, sem):
    dst = pl.multiple_of(buf_row0 * ROW_SLABS, ROW_SLABS)
    region = buf.at[pl.ds(dst, n_rows * ROW_SLABS), :]
    pltpu.make_async_copy(region, region, sem).wait()


def _slab_column(buf, row0, n_rows, s):
    return buf[pl.ds(row0 * ROW_SLABS + s, n_rows, stride=ROW_SLABS), :]


def _moe_kernel(be_ref, nv_ref, tok_ref, h2_hbm, wgu_ref, bgu_ref, wdn_ref, bdn_ref, y_ref,
                xbuf, sem, wgu_bf, wdn_bf):
    i = pl.program_id(0)
    n_valid = nv_ref[0]
    slot = i % 2

    @pl.when(i == 0)
    def _():
        _gather_rows_start(tok_ref, 0, h2_hbm, xbuf, 0, BM, sem.at[0])

    @pl.when(i + 1 < n_valid)
    def _():
        _gather_rows_start(tok_ref, (i + 1) * BM, h2_hbm, xbuf, (1 - slot) * BM, BM, sem.at[1 - slot])

    @pl.when(i < n_valid)
    def _():
        new_expert = jnp.logical_or(i == 0, be_ref[i] != be_ref[jnp.maximum(i - 1, 0)])

        @pl.when(new_expert)
        def _():
            wgu_bf[...] = wgu_ref[...].astype(BF16)
            wdn_bf[...] = wdn_ref[...].astype(BF16)

        _gather_rows_wait(xbuf, slot * BM, BM, sem.at[slot])
        x = jnp.concatenate([_slab_column(xbuf, slot * BM, BM, s) for s in range(ROW_SLABS)],
                            axis=-1).astype(BF16)
        h = jnp.dot(x, wgu_bf[...], preferred_element_type=F32) + bgu_ref[...]
        g = jnp.minimum(h[:, :D_EXPERT], SWIGLU_LIMIT)
        u = jnp.clip(h[:, D_EXPERT:], -SWIGLU_LIMIT, SWIGLU_LIMIT)
        act = ((u + 1.0) * (g * jax.nn.sigmoid(SWIGLU_ALPHA * g))).astype(BF16)
        y = jnp.dot(act, wdn_bf[...], preferred_element_type=F32) + bdn_ref[...]
        for s in range(ROW_SLABS):
            y_ref[:, s, :] = y[:, s * LANES:(s + 1) * LANES]

    @pl.when(i >= n_valid)
    def _():
        y_ref[...] = jnp.zeros_like(y_ref)


def _moe_blocks(block_expert, n_valid, slot_tok, h2, w_gu, b_gu, w_down, b_down, li):
    grid_spec = pltpu.PrefetchScalarGridSpec(
        num_scalar_prefetch=3,
        grid=(NB,),
        in_specs=[
            pl.BlockSpec(memory_space=pl.ANY),
            pl.BlockSpec((None, None, D_MODEL, 2 * D_EXPERT), lambda i, be, nv, tok: (li, be[i], 0, 0)),
            pl.BlockSpec((None, None, 1, 2 * D_EXPERT), lambda i, be, nv, tok: (li, be[i], 0, 0)),
            pl.BlockSpec((None, None, D_EXPERT, D_MODEL), lambda i, be, nv, tok: (li, be[i], 0, 0)),
            pl.BlockSpec((None, None, 1, D_MODEL), lambda i, be, nv, tok: (li, be[i], 0, 0)),
        ],
        out_specs=pl.BlockSpec((BM, ROW_SLABS, LANES), lambda i, be, nv, tok: (i, 0, 0)),
        scratch_shapes=[
            pltpu.VMEM((2 * BM * ROW_SLABS, LANES), F32),
            pltpu.SemaphoreType.DMA((2,)),
            pltpu.VMEM((D_MODEL, 2 * D_EXPERT), BF16),
            pltpu.VMEM((D_EXPERT, D_MODEL), BF16),
        ],
    )
    return pl.pallas_call(
        _moe_kernel,
        grid_spec=grid_spec,
        out_shape=jax.ShapeDtypeStruct((N_SLOTS, ROW_SLABS, LANES), F32),
        compiler_params=_cparams(1),
        name="moe_experts",
    )(block_expert, n_valid, slot_tok, h2, w_gu,
      b_gu.reshape(DEPTH, N_EXPERTS, 1, 2 * D_EXPERT), w_down, b_down.reshape(DEPTH, N_EXPERTS, 1, D_MODEL))


def _combine_kernel(dest_ref, y_hbm, x1_ref, gate_ref, mod_ref, lng_ref, lnb_ref, o_ref, ybuf, sem):
    i = pl.program_id(0)
    n_steps = pl.num_programs(0)
    slot = i % 2
    rows = TC * TOP_K

    @pl.when(i == 0)
    def _():
        _gather_rows_start(dest_ref, 0, y_hbm, ybuf, 0, rows, sem.at[0])

    @pl.when(i + 1 < n_steps)
    def _():
        _gather_rows_start(dest_ref, (i + 1) * rows, y_hbm, ybuf, (1 - slot) * rows, rows, sem.at[1 - slot])

    _gather_rows_wait(ybuf, slot * rows, rows, sem.at[slot])
    gates = gate_ref[...]
    cols = []
    for s in range(ROW_SLABS):
        acc = jnp.zeros((TC, LANES), F32)
        for k in range(TOP_K):
            part = ybuf[pl.ds((slot * rows + k) * ROW_SLABS + s, TC, stride=TOP_K * ROW_SLABS), :]
            acc = acc + gates[:, k:k + 1] * part
        cols.append(acc)
    ffn = jnp.concatenate(cols, axis=-1)
    m = mod_ref[pl.ds(_mod_row(i * TC // TM), 1), :]
    g2 = m[:, 5 * D_MODEL:6 * D_MODEL]
    o_ref[...] = _layer_norm_rows(DN_ALPHA * x1_ref[...] + g2 * ffn, lng_ref[...], lnb_ref[...])


def _combine(dest, y, x1, gates, mod, ln_g, ln_b, li):
    grid_spec = pltpu.PrefetchScalarGridSpec(
        num_scalar_prefetch=1,
        grid=(N_TOK // TC,),
        in_specs=[
            pl.BlockSpec(memory_space=pl.ANY),
            pl.BlockSpec((TC, D_MODEL), lambda i, d: (i, 0)),
            pl.BlockSpec((TC, LANES), lambda i, d: (i, 0)),
            pl.BlockSpec((None, SUBLANES, 6 * D_MODEL), lambda i, d: (li, 0, 0)),
            pl.BlockSpec((1, D_MODEL), lambda i, d: (0, 0)),
            pl.BlockSpec((1, D_MODEL), lambda i, d: (0, 0)),
        ],
        out_specs=pl.BlockSpec((TC, D_MODEL), lambda i, d: (i, 0)),
        scratch_shapes=[
            pltpu.VMEM((2 * TC * TOP_K * ROW_SLABS, LANES), F32),
            pltpu.SemaphoreType.DMA((2,)),
        ],
    )
    return pl.pallas_call(
        _combine_kernel,
        grid_spec=grid_spec,
        out_shape=jax.ShapeDtypeStruct((N_TOK, D_MODEL), F32),
        compiler_params=_cparams(1),
        name="moe_combine",
    )(dest, y, x1, gates, mod, ln_g, ln_b)


def _routing_tables(route_i, counts):
    expert = route_i[:, 0:TOP_K]
    rank = route_i[:, TOP_K:2 * TOP_K]
    cnt = counts[0, :N_EXPERTS]
    padded = (cnt + BM - 1) // BM * BM
    pends = jnp.cumsum(padded)
    pstarts = pends - padded
    dest = (pstarts[expert] + rank).reshape(-1).astype(jnp.int32)
    tok = jnp.repeat(jnp.arange(N_TOK, dtype=jnp.int32), TOP_K)
    slot_tok = jnp.zeros((N_SLOTS,), jnp.int32).at[dest].set(tok)
    n_valid = (pends[-1] // BM).astype(jnp.int32)
    block_start = jnp.arange(NB, dtype=jnp.int32) * BM
    block_e = jnp.minimum(jnp.sum(block_start[:, None] >= pends[None, :], axis=1), N_EXPERTS - 1)
    last_e = block_e[jnp.maximum(n_valid - 1, 0)]
    block_e = jnp.where(jnp.arange(NB) < n_valid, block_e, last_e).astype(jnp.int32)
    return dest, slot_tok, block_e, n_valid.reshape(1)


def _layer_params(p, li):
    eye = jnp.eye(N_LRU_BLOCKS, dtype=F32)
    w_rg_full = jnp.einsum("dkgio,gh->gidkho", p["w_rg"][li], eye).reshape(LRU_WIDTH, 4 * LRU_WIDTH)
    pad = LANES - N_EXPERTS
    return {
        "sgu_ln_g": p["sgu_ln_g"][li].reshape(1, SGU_WIDTH),
        "sgu_ln_b": p["sgu_ln_b"][li].reshape(1, SGU_WIDTH),
        "w_spatial": p["w_spatial"][li],
        "b_spatial_full": jnp.repeat(p["b_spatial"][li].T, SGU_GROUP, axis=1),
        "conv_w": p["conv_w"][li],
        "conv_b": p["conv_b"][li].reshape(1, LRU_WIDTH),
        "w_rg_full": w_rg_full,
        "b_rg_full": p["b_rg"][li].reshape(1, 4 * LRU_WIDTH),
        "lru_log": p["lru_log"][li],
        "w_out_all": p["w_out"],
        "ln_g1": p["ln_g"][li, 0].reshape(1, D_MODEL),
        "ln_b1": p["ln_b"][li, 0].reshape(1, D_MODEL),
        "ln_g2": p["ln_g"][li, 1].reshape(1, D_MODEL),
        "ln_b2": p["ln_b"][li, 1].reshape(1, D_MODEL),
        "w_router_pad": jnp.pad(p["w_router"][li], ((0, 0), (0, pad))),
        "b_router_pad": jnp.pad(p["b_router"][li], (0, pad)).reshape(1, LANES),
    }


def kernel(x_prompt, x_sample, cache_k, cache_v, state_lru, c, c_ctx, w_mod, b_mod, w_in, lam, subln_g, sgu_ln_g, sgu_ln_b, w_spatial, b_spatial, conv_w, conv_b, w_rg, b_rg, lru_log, w_out, ln_g, ln_b, w_router, b_router, w_gu, b_gu, w_down, b_down):
    p = dict(sgu_ln_g=sgu_ln_g, sgu_ln_b=sgu_ln_b, w_spatial=w_spatial, b_spatial=b_spatial, conv_w=conv_w,
             conv_b=conv_b, w_rg=w_rg, b_rg=b_rg, lru_log=lru_log, w_out=w_out, ln_g=ln_g, ln_b=ln_b,
             w_router=w_router, b_router=b_router)
    cvec8 = jnp.concatenate([c_ctx[None, :], c, jnp.zeros((SUBLANES - 1 - DEC_BATCH, D_MODEL), F32)], axis=0)
    mod = _modulation(cvec8, w_mod, b_mod)
    x = jnp.concatenate([x_prompt.reshape(N_CTX, D_MODEL), x_sample.reshape(N_DEC, D_MODEL)], axis=0)
    tables = _rope_tables()
    zero_state = jnp.zeros((BATCH, 1, 2, LRU_WIDTH), F32)
    new_k, new_v, new_s = [], [], []
    for li in range(DEPTH):
        lp = _layer_params(p, li)
        lam_init = 0.8 - 0.6 * math.exp(-0.3 * li)
        proj = _input_projection(x, mod, w_in, li)
        att_ctx, kc, vc = _context_attention(proj, lam[li], subln_g[li], lam_init)
        att_dec = _denoise_attention(proj, cache_k, cache_v, lam[li], subln_g[li], li, lam_init, tables)
        sgu_ctx, lru_ctx, h_ctx = _mixers(proj, zero_state, 0, lp, SEQ, BATCH, 0)
        sgu_dec, lru_dec, _ = _mixers(proj, state_lru, li, lp, DEC_SEQ, DEC_BATCH, N_CTX // DEC_SEQ)
        att = jnp.concatenate([att_ctx, att_dec], axis=0)
        sgu = jnp.concatenate([sgu_ctx, sgu_dec], axis=0)
        lru = jnp.concatenate([lru_ctx, lru_dec], axis=0)
        x1, h2, route_i, route_f, counts = _output_projection(x, att, sgu, lru, mod, lp, li)
        dest, slot_tok, block_e, n_valid = _routing_tables(route_i, counts)
        y = _moe_blocks(block_e, n_valid, slot_tok, h2, w_gu, b_gu, w_down, b_down, li)
        x = _combine(dest, y, x1, route_f, mod, lp["ln_g2"], lp["ln_b2"], li)
        new_k.append(kc)
        new_v.append(vc)
        new_s.append(h_ctx)
    y_prompt = x[:N_CTX].reshape(BATCH, SEQ, D_MODEL)
    y_sample = x[N_CTX:].reshape(DEC_BATCH, DEC_SEQ, D_MODEL)
    return (y_prompt, y_sample, jnp.stack(new_k, axis=1), jnp.stack(new_v, axis=1), jnp.stack(new_s, axis=1))
```

```python
import functools
import math

import numpy as np
import jax
import jax.numpy as jnp
from jax import lax
from jax.experimental import pallas as pl
from jax.experimental.pallas import tpu as pltpu

F32 = jnp.float32
BF16 = jnp.bfloat16
I32 = jnp.int32

D_MODEL = 1024
BATCH = 32
SEQ = 256
DEPTH = 2
DEC_BATCH = 2
DEC_SEQ = 1024
PAST_LEN = 256
GRID_W = 64
HEAD_DIM = 64
ATT_WIDTH = D_MODEL // 2
SGU_WIDTH = D_MODEL // 4
LRU_WIDTH = D_MODEL // 4
N_ATT_HEADS = ATT_WIDTH // HEAD_DIM
ATT_HALF = HEAD_DIM // 2
ROPE_FREQS = ATT_HALF // 4
ROPE_THETA = 10000.0
CHUNK = 128
N_SGU_GROUPS = 4
SGU_GROUP = SGU_WIDTH // N_SGU_GROUPS
N_LRU_BLOCKS = 4
LRU_BLOCK = LRU_WIDTH // N_LRU_BLOCKS
CONV_W = 4
LRU_C = 8.0
IN_COLS = 3 * ATT_WIDTH + 2 * SGU_WIDTH + 2 * LRU_WIDTH
N_EXPERTS = 32
TOP_K = 4
D_EXPERT = D_MODEL
SWIGLU_LIMIT = 7.0
SWIGLU_ALPHA = 1.702
DN_ALPHA = (2 * DEPTH) ** 0.25
EPS = 1e-5

N_CTX = BATCH * SEQ
N_DEC = DEC_BATCH * DEC_SEQ
N_TOK = N_CTX + N_DEC

LANES = 128
SUBLANES = 8
ROW_SLABS = D_MODEL // LANES

TM = 512
CTX_TILES = N_CTX // TM
TILES_PER_DEC = DEC_SEQ // TM
N_TILES = N_TOK // TM
MOD_TN = 512
RT = 256
N_RT = N_TOK // RT
RT_PER_TM = TM // RT
RT_ROWS = RT * TOP_K
BM = 256
NB = N_TOK * TOP_K // BM + N_EXPERTS
N_SLOTS = NB * BM
RUN_BITS = tuple(1 << b for b in range(RT.bit_length() - 1, -1, -1))
PAD_BITS = tuple(1 << b for b in range(BM.bit_length() - 2, -1, -1))
VMEM_LIMIT = 56 * 1024 * 1024


def _cparams(n_axes):
    return pltpu.CompilerParams(
        dimension_semantics=("arbitrary",) * n_axes,
        vmem_limit_bytes=VMEM_LIMIT)


def _mod_row(i):
    return jnp.where(i < CTX_TILES, 0, 1 + (i - CTX_TILES) // TILES_PER_DEC)


def _layer_norm_rows(z, g, b):
    mu = jnp.mean(z, axis=-1, keepdims=True)
    zc = z - mu
    var = jnp.mean(zc * zc, axis=-1, keepdims=True)
    return zc * lax.rsqrt(var + EPS) * g + b


def _pair_specs(tile, width, n_ctx_tiles):
    ctx = pl.BlockSpec((tile, width), lambda i, *_: (jnp.minimum(i, n_ctx_tiles - 1), 0))
    dec = pl.BlockSpec((tile, width), lambda i, *_: (jnp.maximum(i - n_ctx_tiles, 0), 0))
    return [ctx, dec]


def _pair_read(i, n_ctx_tiles, ctx_ref, dec_ref):
    return jnp.where(i < n_ctx_tiles, ctx_ref[...], dec_ref[...])


def _mod_kernel(cvec_ref, w_ref, b_ref, o_ref):
    cv = cvec_ref[...]
    s = cv * jax.nn.sigmoid(cv)
    s_t = s.T
    w = w_ref[0]
    rows = [jnp.sum(s_t[:, r:r + 1] * w, axis=0, keepdims=True) for r in range(1 + DEC_BATCH)]
    rows.append(jnp.zeros((SUBLANES - 1 - DEC_BATCH, MOD_TN), F32))
    o_ref[0] = jnp.concatenate(rows, axis=0) + b_ref[0]


def _modulation(cvec8, w_mod, b_mod):
    n_out = w_mod.shape[-1]
    return pl.pallas_call(
        _mod_kernel,
        grid=(DEPTH, n_out // MOD_TN),
        in_specs=[
            pl.BlockSpec((SUBLANES, D_MODEL), lambda l, j: (0, 0)),
            pl.BlockSpec((1, D_MODEL, MOD_TN), lambda l, j: (l, 0, j)),
            pl.BlockSpec((1, 1, MOD_TN), lambda l, j: (l, 0, j)),
        ],
        out_specs=pl.BlockSpec((1, SUBLANES, MOD_TN), lambda l, j: (l, 0, j)),
        out_shape=jax.ShapeDtypeStruct((DEPTH, SUBLANES, n_out), F32),
        compiler_params=_cparams(2),
        name="modulation",
    )(cvec8, w_mod, b_mod.reshape(DEPTH, 1, n_out))


def _inproj_kernel(xc_ref, xd_ref, mod_ref, w_ref, o_ref, wbf_ref):
    i = pl.program_id(0)

    @pl.when(i == 0)
    def _():
        wbf_ref[...] = w_ref[...].astype(BF16)

    m = mod_ref[pl.ds(_mod_row(i), 1), :]
    sh1 = m[:, 0:D_MODEL]
    sc1 = m[:, D_MODEL:2 * D_MODEL]
    x = _pair_read(i, CTX_TILES, xc_ref, xd_ref)
    h = (x * (1.0 + sc1) + sh1).astype(BF16)
    o_ref[...] = jnp.dot(h, wbf_ref[...], preferred_element_type=F32)


def _input_projection(x_ctx, x_dec, mod, w_in, li):
    return pl.pallas_call(
        _inproj_kernel,
        grid=(N_TILES,),
        in_specs=_pair_specs(TM, D_MODEL, CTX_TILES) + [
            pl.BlockSpec((None, SUBLANES, 6 * D_MODEL), lambda i: (li, 0, 0)),
            pl.BlockSpec((None, D_MODEL, IN_COLS), lambda i: (li, 0, 0), pipeline_mode=pl.Buffered(1)),
        ],
        out_specs=pl.BlockSpec((TM, IN_COLS), lambda i: (i, 0)),
        out_shape=jax.ShapeDtypeStruct((N_TOK, IN_COLS), F32),
        scratch_shapes=[pltpu.VMEM((D_MODEL, IN_COLS), BF16)],
        compiler_params=_cparams(1),
        name="input_projection",
    )(x_ctx, x_dec, mod, w_in)


def _lambda(lam_ref, lam_init):
    lm = lam_ref[...]
    a = jnp.sum(lm[0:1] * lm[1:2], axis=-1, keepdims=True)
    b = jnp.sum(lm[2:3] * lm[3:4], axis=-1, keepdims=True)
    return jnp.exp(a) - jnp.exp(b) + lam_init


def _softmax_rows(s):
    e = jnp.exp(s - jnp.max(s, axis=-1, keepdims=True))
    return e / jnp.sum(e, axis=-1, keepdims=True)


def _sub_ln(o, g, lam_init):
    ms = jnp.mean(o * o, axis=-1, keepdims=True)
    return o * lax.rsqrt(ms + EPS) * g * (1.0 - lam_init)


_NT = (((1,), (1,)), ((), ()))
_TN = (((0,), (0,)), ((), ()))
_ATT_SCALE = ATT_HALF ** -0.5


def _ctx_attn_kernel(lam_ref, g_ref, q_ref, k_ref, v_ref, att_ref, ck_ref, cv_ref, *, lam_init):
    lam = _lambda(lam_ref, lam_init)
    k = k_ref[...]
    v = v_ref[...]
    qb = q_ref[...].astype(BF16)
    kb = k.astype(BF16)
    vb = v.astype(BF16)
    outs = []
    for h in range(N_ATT_HEADS):
        lo = h * HEAD_DIM
        ps = []
        for c in range(2):
            a = lo + c * ATT_HALF
            s = lax.dot_general(qb[:, a:a + ATT_HALF], kb[:, a:a + ATT_HALF], _NT,
                                preferred_element_type=F32) * _ATT_SCALE
            ps.append(_softmax_rows(s))
        w = ps[0] - lam * ps[1]
        o = jnp.dot(w.astype(BF16), vb[:, lo:lo + HEAD_DIM], preferred_element_type=F32)
        outs.append(_sub_ln(o, g_ref[...], lam_init))
        ck_ref[0, h] = k[:, lo:lo + HEAD_DIM]
        cv_ref[0, h] = v[:, lo:lo + HEAD_DIM]
    att_ref[...] = jnp.concatenate(outs, axis=-1)


def _context_attention(proj, lam, subln_g, lam_init):
    kv_shape = jax.ShapeDtypeStruct((BATCH, N_ATT_HEADS, SEQ, HEAD_DIM), F32)
    kv_spec = pl.BlockSpec((1, N_ATT_HEADS, SEQ, HEAD_DIM), lambda b: (b, 0, 0, 0))
    return pl.pallas_call(
        functools.partial(_ctx_attn_kernel, lam_init=lam_init),
        grid=(BATCH,),
        in_specs=[
            pl.BlockSpec((4, ATT_HALF), lambda b: (0, 0)),
            pl.BlockSpec((1, HEAD_DIM), lambda b: (0, 0)),
            pl.BlockSpec((SEQ, ATT_WIDTH), lambda b: (b, 0)),
            pl.BlockSpec((SEQ, ATT_WIDTH), lambda b: (b, 1)),
            pl.BlockSpec((SEQ, ATT_WIDTH), lambda b: (b, 2)),
        ],
        out_specs=[pl.BlockSpec((SEQ, ATT_WIDTH), lambda b: (b, 0)), kv_spec, kv_spec],
        out_shape=[jax.ShapeDtypeStruct((N_CTX, ATT_WIDTH), F32), kv_shape, kv_shape],
        compiler_params=_cparams(1),
        name="context_attention",
    )(lam, subln_g.reshape(1, HEAD_DIM), proj, proj, proj)


def _rope_tables():
    t = np.arange(DEC_SEQ)
    pos = np.stack([t // GRID_W, t % GRID_W], axis=1).astype(np.float32)
    inv = (np.float32(ROPE_THETA) ** (-np.arange(ROPE_FREQS, dtype=np.float32) / np.float32(ROPE_FREQS)))
    j = np.arange(HEAD_DIM)
    d = j % ATT_HALF
    axis = d // (2 * ROPE_FREQS)
    u = d % (2 * ROPE_FREQS)
    ang = pos[:, axis] * inv[u % ROPE_FREQS][None, :].astype(np.float32)
    cos = np.cos(ang).astype(np.float32)
    sin = np.sin(ang).astype(np.float32)
    first = (u < ROPE_FREQS)[None, :]
    s_next = np.where(first, -sin, 0.0).astype(np.float32)
    s_prev = np.where(first, 0.0, sin).astype(np.float32)
    tile = lambda a: jnp.asarray(np.tile(a, (1, N_ATT_HEADS)))
    return tile(cos), tile(s_next), tile(s_prev)


def _rotate(x, cos, s_next, s_prev):
    width = x.shape[-1]
    return (x * cos + pltpu.roll(x, width - ROPE_FREQS, axis=1) * s_next
            + pltpu.roll(x, ROPE_FREQS, axis=1) * s_prev)


QB = 256
Q_STEPS = DEC_SEQ // QB


def _dec_attn_kernel(lam_ref, g_ref, q_ref, k_ref, v_ref, ck_ref, cv_ref,
                     cq_ref, snq_ref, spq_ref, ck_tab, snk_tab, spk_tab,
                     att_ref, krot_ref, *, lam_init):
    j = pl.program_id(1)

    @pl.when(j == 0)
    def _():
        krot_ref[...] = _rotate(k_ref[...], ck_tab[...], snk_tab[...], spk_tab[...]).astype(BF16)

    lam = _lambda(lam_ref, lam_init)
    qb = _rotate(q_ref[...], cq_ref[...], snq_ref[...], spq_ref[...]).astype(BF16)
    kb = krot_ref[...]
    vb = v_ref[...].astype(BF16)
    outs = []
    for h in range(N_ATT_HEADS):
        lo = h * HEAD_DIM
        pk = ck_ref[0, 0, h].astype(BF16)
        pv = cv_ref[0, 0, h].astype(BF16)
        ps = []
        for c in range(2):
            a = lo + c * ATT_HALF
            qs = qb[:, a:a + ATT_HALF]
            s_past = lax.dot_general(qs, pk[:, c * ATT_HALF:(c + 1) * ATT_HALF], _NT,
                                     preferred_element_type=F32)
            s_new = lax.dot_general(qs, kb[:, a:a + ATT_HALF], _NT, preferred_element_type=F32)
            ps.append(_softmax_rows(jnp.concatenate([s_past, s_new], axis=-1) * _ATT_SCALE))
        w = (ps[0] - lam * ps[1]).astype(BF16)
        o = (jnp.dot(w[:, :PAST_LEN], pv, preferred_element_type=F32)
             + jnp.dot(w[:, PAST_LEN:], vb[:, lo:lo + HEAD_DIM], preferred_element_type=F32))
        outs.append(_sub_ln(o, g_ref[...], lam_init))
    att_ref[...] = jnp.concatenate(outs, axis=-1)


def _denoise_attention(proj, cache_k, cache_v, lam, subln_g, li, lam_init, tables):
    cos, s_next, s_prev = tables
    row0 = N_CTX // QB
    seq0 = N_CTX // DEC_SEQ
    q_tab = pl.BlockSpec((QB, ATT_WIDTH), lambda b, j: (j, 0))
    k_tab = pl.BlockSpec((DEC_SEQ, ATT_WIDTH), lambda b, j: (0, 0))
    cache_spec = pl.BlockSpec((1, 1, N_ATT_HEADS, PAST_LEN, HEAD_DIM), lambda b, j: (b, li, 0, 0, 0))
    return pl.pallas_call(
        functools.partial(_dec_attn_kernel, lam_init=lam_init),
        grid=(DEC_BATCH, Q_STEPS),
        in_specs=[
            pl.BlockSpec((4, ATT_HALF), lambda b, j: (0, 0)),
            pl.BlockSpec((1, HEAD_DIM), lambda b, j: (0, 0)),
            pl.BlockSpec((QB, ATT_WIDTH), lambda b, j: (row0 + b * Q_STEPS + j, 0)),
            pl.BlockSpec((DEC_SEQ, ATT_WIDTH), lambda b, j: (seq0 + b, 1)),
            pl.BlockSpec((DEC_SEQ, ATT_WIDTH), lambda b, j: (seq0 + b, 2)),
            cache_spec, cache_spec,
            q_tab, q_tab, q_tab, k_tab, k_tab, k_tab,
        ],
        out_specs=pl.BlockSpec((QB, ATT_WIDTH), lambda b, j: (b * Q_STEPS + j, 0)),
        out_shape=jax.ShapeDtypeStruct((N_DEC, ATT_WIDTH), F32),
        scratch_shapes=[pltpu.VMEM((DEC_SEQ, ATT_WIDTH), BF16)],
        compiler_params=_cparams(2),
        name="denoise_attention",
    )(lam, subln_g.reshape(1, HEAD_DIM), proj, proj, proj, cache_k, cache_v,
      cos, s_next, s_prev, cos, s_next, s_prev)


def _softplus(z):
    return jnp.maximum(z, 0.0) + jnp.log1p(jnp.exp(-jnp.abs(z)))


def _mixer_kernel(su_ref, sv_ref, rx_ref, rg_ref, h0_ref, lng_ref, lnb_ref, ws_ref, bs_ref,
                  cw_ref, cb_ref, wrg_ref, brg_ref, lrulog_ref,
                  sgu_ref, lru_ref, hlast_ref, a_scr, b_scr, h_scr, *, seq_len):
    vn = _layer_norm_rows(sv_ref[...], lng_ref[...], lnb_ref[...])
    lane_group = lax.broadcasted_iota(I32, (CHUNK, SGU_WIDTH), 1) // SGU_GROUP
    for n in range(seq_len // CHUNK):
        rows = slice(n * CHUNK, (n + 1) * CHUNK)
        vc = vn[rows].astype(BF16)
        s = jnp.zeros((CHUNK, SGU_WIDTH), F32)
        for g in range(N_SGU_GROUPS):
            sg = jnp.dot(ws_ref[g].astype(BF16), vc, preferred_element_type=F32)
            s = jnp.where(lane_group == g, sg, s)
        sgu_ref[rows, :] = su_ref[rows, :] * (s + bs_ref[...])

    x = rx_ref[...]
    row = lax.broadcasted_iota(I32, (seq_len, LRU_WIDTH), 0)

    def shifted(val, d, fill):
        rolled = pltpu.roll(val, d % seq_len, axis=0)
        inside = (row >= d) if d > 0 else (row < seq_len + d)
        return jnp.where(inside, rolled, fill)

    left = CONV_W // 2
    xc = cb_ref[...] + x * cw_ref[left:left + 1, :]
    for tap in range(CONV_W):
        if tap != left:
            xc = xc + shifted(x, left - tap, 0.0) * cw_ref[tap:tap + 1, :]
    gates = jax.nn.sigmoid(jnp.dot(xc.astype(BF16), wrg_ref[...].astype(BF16),
                                   preferred_element_type=F32) + brg_ref[...])
    in_chunk = row % SUBLANES
    n_chunks = seq_len // SUBLANES
    h0 = h0_ref[0, 0]
    lasts = []
    for direction in range(2):
        reverse = direction == 1
        base = direction * 2 * LRU_WIDTH
        r = gates[:, base:base + LRU_WIDTH]
        gi = gates[:, base + LRU_WIDTH:base + 2 * LRU_WIDTH]
        log_a = -LRU_C * r * _softplus(-lrulog_ref[direction:direction + 1, :])
        a = jnp.exp(log_a)
        b = jnp.sqrt(-jnp.tanh(log_a) * (a * a + 1.0)) * gi * xc
        for d in (1, 2, 4):
            if reverse:
                inside = in_chunk < SUBLANES - d
                a_n = jnp.where(inside, pltpu.roll(a, seq_len - d, axis=0), 1.0)
                b_n = jnp.where(inside, pltpu.roll(b, seq_len - d, axis=0), 0.0)
            else:
                inside = in_chunk >= d
                a_n = jnp.where(inside, pltpu.roll(a, d, axis=0), 1.0)
                b_n = jnp.where(inside, pltpu.roll(b, d, axis=0), 0.0)
            b = a * b_n + b
            a = a * a_n
        a_scr[...] = a
        b_scr[...] = b

        def chunk_step(c, carry, reverse=reverse):
            cc = n_chunks - 1 - c if reverse else c
            off = pl.multiple_of(cc * SUBLANES, SUBLANES)
            hc = a_scr[pl.ds(off, SUBLANES), :] * carry + b_scr[pl.ds(off, SUBLANES), :]
            if reverse:
                h_scr[pl.ds(off, SUBLANES), :] = h_scr[pl.ds(off, SUBLANES), :] + hc
                return hc[0:1, :]
            h_scr[pl.ds(off, SUBLANES), :] = hc
            return hc[SUBLANES - 1:SUBLANES, :]

        lasts.append(lax.fori_loop(0, n_chunks, chunk_step, h0[direction:direction + 1, :]))
    lru_ref[...] = h_scr[...] * jax.nn.gelu(rg_ref[...])
    hlast_ref[0] = jnp.concatenate(lasts, axis=0)


def _mixers(proj, h0, h0_layer, lp, seq_len, n_seq, row_block0):
    col0 = 3 * ATT_WIDTH // SGU_WIDTH
    col = lambda c: pl.BlockSpec((seq_len, SGU_WIDTH), lambda b: (row_block0 + b, col0 + c))
    full = lambda shape: pl.BlockSpec(shape, lambda b: (0,) * len(shape))
    out_rows = pl.BlockSpec((seq_len, SGU_WIDTH), lambda b: (b, 0))
    return pl.pallas_call(
        functools.partial(_mixer_kernel, seq_len=seq_len),
        grid=(n_seq,),
        in_specs=[
            col(0), col(1), col(2), col(3),
            pl.BlockSpec((1, 1, 2, LRU_WIDTH), lambda b: (b, h0_layer, 0, 0)),
            full((1, SGU_WIDTH)), full((1, SGU_WIDTH)),
            full((N_SGU_GROUPS, CHUNK, CHUNK)), full((CHUNK, SGU_WIDTH)),
            full((CONV_W, LRU_WIDTH)), full((1, LRU_WIDTH)),
            full((LRU_WIDTH, 4 * LRU_WIDTH)), full((1, 4 * LRU_WIDTH)),
            full((2, LRU_WIDTH)),
        ],
        out_specs=[out_rows, out_rows, pl.BlockSpec((1, 2, LRU_WIDTH), lambda b: (b, 0, 0))],
        out_shape=[
            jax.ShapeDtypeStruct((n_seq * seq_len, SGU_WIDTH), F32),
            jax.ShapeDtypeStruct((n_seq * seq_len, LRU_WIDTH), F32),
            jax.ShapeDtypeStruct((n_seq, 2, LRU_WIDTH), F32),
        ],
        scratch_shapes=[pltpu.VMEM((seq_len, LRU_WIDTH), F32)] * 3,
        compiler_params=_cparams(1),
        name="mixers_%d" % seq_len,
    )(proj, proj, proj, proj, h0, lp["sgu_ln_g"], lp["sgu_ln_b"], lp["w_spatial"], lp["b_spatial_full"],
      lp["conv_w"], lp["conv_b"], lp["w_rg_full"], lp["b_rg_full"], lp["lru_log"])


def _split_bf16(v):
    hi = v.astype(BF16)
    lo = (v - hi.astype(F32)).astype(BF16)
    return hi, lo


def _outproj_kernel(xc_ref, xd_ref, ac_ref, ad_ref, sc_ref, sd_ref, lc_ref, ld_ref,
                    mod_ref, wout_ref, lng_ref, lnb_ref, wr_ref, br_ref,
                    x1_ref, h2_ref, ri_ref, rf_ref, cnt_ref, wbf_ref):
    i = pl.program_id(0)

    @pl.when(i == 0)
    def _():
        wbf_ref[...] = wout_ref[...].astype(BF16)

    m = mod_ref[pl.ds(_mod_row(i), 1), :]
    g1 = m[:, 2 * D_MODEL:3 * D_MODEL]
    sh2 = m[:, 3 * D_MODEL:4 * D_MODEL]
    sc2 = m[:, 4 * D_MODEL:5 * D_MODEL]
    a0, a1 = ATT_WIDTH, ATT_WIDTH + SGU_WIDTH
    x = _pair_read(i, CTX_TILES, xc_ref, xd_ref)
    att = _pair_read(i, CTX_TILES, ac_ref, ad_ref).astype(BF16)
    sgu = _pair_read(i, CTX_TILES, sc_ref, sd_ref).astype(BF16)
    lru = _pair_read(i, CTX_TILES, lc_ref, ld_ref).astype(BF16)
    mix = (jnp.dot(att, wbf_ref[0:a0, :], preferred_element_type=F32)
           + jnp.dot(sgu, wbf_ref[a0:a1, :], preferred_element_type=F32)
           + jnp.dot(lru, wbf_ref[a1:, :], preferred_element_type=F32))
    x1 = _layer_norm_rows(DN_ALPHA * x + g1 * mix, lng_ref[...], lnb_ref[...])
    x1_ref[...] = x1
    h2 = x1 * (1.0 + sc2) + sh2
    h2_ref[...] = h2.astype(BF16)

    h_hi, h_lo = _split_bf16(h2)
    w_hi, w_lo = _split_bf16(wr_ref[...])
    logits = (jnp.dot(h_hi, w_hi, preferred_element_type=F32)
              + jnp.dot(h_lo, w_hi, preferred_element_type=F32)
              + jnp.dot(h_hi, w_lo, preferred_element_type=F32)) + br_ref[...]
    lane = lax.broadcasted_iota(I32, (TM, LANES), 1)
    lane_f = lane.astype(F32)
    neg_inf = jnp.float32(-jnp.inf)
    work = jnp.where(lane < N_EXPERTS, logits, neg_inf)
    vals, idxs = [], []
    for _ in range(TOP_K):
        top = jnp.max(work, axis=-1, keepdims=True)
        idx = jnp.min(jnp.where(work == top, lane_f, float(LANES)), axis=-1, keepdims=True)
        vals.append(top)
        idxs.append(idx)
        work = jnp.where(lane_f == idx, neg_inf, work)
    exps = [jnp.exp(v - vals[0]) for v in vals]
    denom = exps[0] + exps[1] + exps[2] + exps[3]
    onehot = jnp.zeros((TM, LANES), F32)
    for idx in idxs:
        onehot = onehot + (lane_f == idx).astype(F32)
    r_i = lax.broadcasted_iota(I32, (RT, RT), 0)
    c_i = lax.broadcasted_iota(I32, (RT, RT), 1)
    tri = (r_i > c_i).astype(F32).astype(BF16)
    before = jnp.concatenate(
        [jnp.dot(tri, onehot[t * RT:(t + 1) * RT].astype(BF16), preferred_element_type=F32)
         for t in range(RT_PER_TM)], axis=0)
    ri = jnp.zeros((TM, LANES), F32)
    rf = jnp.zeros((TM, LANES), F32)
    for k in range(TOP_K):
        rank = jnp.sum(jnp.where(lane_f == idxs[k], before, 0.0), axis=-1, keepdims=True)
        ri = jnp.where(lane == k, idxs[k], ri)
        ri = jnp.where(lane == TOP_K + k, rank, ri)
        rf = jnp.where(lane == k, exps[k] / denom, rf)
    ri_ref[...] = ri.astype(I32)
    rf_ref[...] = rf
    for t in range(RT_PER_TM):
        total = jnp.sum(onehot[t * RT:(t + 1) * RT], axis=0, keepdims=True)
        cnt_ref[t] = jnp.broadcast_to(total, (SUBLANES, LANES)).astype(I32)


def _output_projection(x_pair, att_pair, sgu_pair, lru_pair, mod, lp, li):
    rows = lambda w: pl.BlockSpec((TM, w), lambda i: (i, 0))
    full = lambda shape: pl.BlockSpec(shape, lambda i: (0,) * len(shape))
    return pl.pallas_call(
        _outproj_kernel,
        grid=(N_TILES,),
        in_specs=(_pair_specs(TM, D_MODEL, CTX_TILES) + _pair_specs(TM, ATT_WIDTH, CTX_TILES)
                  + _pair_specs(TM, SGU_WIDTH, CTX_TILES) + _pair_specs(TM, LRU_WIDTH, CTX_TILES) + [
            pl.BlockSpec((None, SUBLANES, 6 * D_MODEL), lambda i: (li, 0, 0)),
            pl.BlockSpec((None, D_MODEL, D_MODEL), lambda i: (li, 0, 0), pipeline_mode=pl.Buffered(1)),
            full((1, D_MODEL)), full((1, D_MODEL)),
            full((D_MODEL, LANES)), full((1, LANES)),
        ]),
        out_specs=[
            rows(D_MODEL), rows(D_MODEL), rows(LANES), rows(LANES),
            pl.BlockSpec((RT_PER_TM, SUBLANES, LANES), lambda i: (i, 0, 0)),
        ],
        out_shape=[
            jax.ShapeDtypeStruct((N_TOK, D_MODEL), F32),
            jax.ShapeDtypeStruct((N_TOK, D_MODEL), BF16),
            jax.ShapeDtypeStruct((N_TOK, LANES), I32),
            jax.ShapeDtypeStruct((N_TOK, LANES), F32),
            jax.ShapeDtypeStruct((N_RT, SUBLANES, LANES), I32),
        ],
        scratch_shapes=[pltpu.VMEM((D_MODEL, D_MODEL), BF16)],
        compiler_params=_cparams(1),
        name="output_projection",
    )(*x_pair, *att_pair, *sgu_pair, *lru_pair, mod, lp["w_out_all"], lp["ln_g1"], lp["ln_b1"],
      lp["w_router_pad"], lp["b_router_pad"])


def _row_ds(row, n_rows):
    return pl.ds(pl.multiple_of(row * ROW_SLABS, ROW_SLABS), n_rows * ROW_SLABS)


def _for_each_run_piece(length, bits, fn):
    for bit in bits:
        done = length & (-2 * bit)
        @pl.when((length & bit) != 0)
        def _(done=done, bit=bit):
            fn(done, bit)


def _tile_run_copies(tile, slot, start_ref, len_ref, off_ref, sorted_hbm, buf, sem, to_sorted):
    def per_expert(e, carry):
        t = tile * N_EXPERTS + e
        start, length, off = start_ref[t], len_ref[t], off_ref[t]

        def piece(done, bit):
            packed = buf.at[_row_ds(slot * RT_ROWS + off + done, bit), :]
            srt = sorted_hbm.at[_row_ds(start + done, bit), :]
            if to_sorted:
                pltpu.make_async_copy(packed, srt, sem).start()
            else:
                pltpu.make_async_copy(srt, packed, sem).start()

        _for_each_run_piece(length, RUN_BITS, piece)
        return carry
    lax.fori_loop(0, N_EXPERTS, per_expert, 0)


def _tile_runs_wait(slot, buf, sem):
    region = buf.at[_row_ds(slot * RT_ROWS, RT_ROWS), :]
    pltpu.make_async_copy(region, region, sem).wait()


def _packed_positions(ri, off_row):
    lane = lax.broadcasted_iota(I32, (RT, LANES), 1)
    pos = []
    for k in range(TOP_K):
        e_k = ri[:, k:k + 1]
        base = jnp.sum(jnp.where(lane == e_k, off_row, 0.0), axis=-1, keepdims=True)
        pos.append(base + ri[:, TOP_K + k:TOP_K + k + 1].astype(F32))
    return pos


def _slab_columns(buf, row0, n_rows):
    return jnp.concatenate(
        [buf[pl.ds(row0 * ROW_SLABS + s, n_rows, stride=ROW_SLABS), :] for s in range(ROW_SLABS)], axis=-1)


def _dispatch_kernel(start_ref, len_ref, off_ref, pstart_ref, plen_ref, nv_ref,
                     h2_ref, ri_ref, offrow_ref, xs_hbm, xc, zbuf, sem, zsem):
    i = pl.program_id(0)
    n_steps = pl.num_programs(0)
    slot = i % 2

    @pl.when(i == 0)
    def _():
        zbuf[...] = jnp.zeros_like(zbuf)

        def zero_fill(wait):
            def go(cp):
                cp.wait() if wait else cp.start()

            def per_expert(e, carry):
                def piece(done, bit):
                    go(pltpu.make_async_copy(zbuf.at[pl.ds(0, bit * ROW_SLABS), :],
                                             xs_hbm.at[_row_ds(pstart_ref[e] + done, bit), :], zsem))
                _for_each_run_piece(plen_ref[e], PAD_BITS, piece)
                return carry
            lax.fori_loop(0, N_EXPERTS, per_expert, 0)

            def per_block(b, carry):
                go(pltpu.make_async_copy(zbuf, xs_hbm.at[_row_ds(b * BM, BM), :], zsem))
                return carry
            lax.fori_loop(nv_ref[0], NB, per_block, 0)

        zero_fill(False)
        zero_fill(True)

    @pl.when(i >= 2)
    def _():
        _tile_runs_wait(slot, xc, sem.at[slot])

    pos = _packed_positions(ri_ref[...], offrow_ref[0, 0:1, :])
    col = lax.broadcasted_iota(I32, (RT, RT_ROWS), 1).astype(F32)
    sel = jnp.zeros((RT, RT_ROWS), F32)
    for p in pos:
        sel = sel + (col == p).astype(F32)
    packed = lax.dot_general(sel.astype(BF16), h2_ref[...], _TN, preferred_element_type=F32)
    for s in range(ROW_SLABS):
        xc[pl.ds(slot * RT_ROWS * ROW_SLABS + s, RT_ROWS, stride=ROW_SLABS), :] = (
            packed[:, s * LANES:(s + 1) * LANES])
    _tile_run_copies(i, slot, start_ref, len_ref, off_ref, xs_hbm, xc, sem.at[slot], to_sorted=True)

    @pl.when(i == n_steps - 1)
    def _():
        _tile_runs_wait(1 - slot, xc, sem.at[1 - slot])
        _tile_runs_wait(slot, xc, sem.at[slot])


def _dispatch(tables, h2, route_i):
    grid_spec = pltpu.PrefetchScalarGridSpec(
        num_scalar_prefetch=6,
        grid=(N_RT,),
        in_specs=[
            pl.BlockSpec((RT, D_MODEL), lambda i, *_: (i, 0)),
            pl.BlockSpec((RT, LANES), lambda i, *_: (i, 0)),
            pl.BlockSpec((1, SUBLANES, LANES), lambda i, *_: (i, 0, 0)),
        ],
        out_specs=pl.BlockSpec(memory_space=pl.ANY),
        scratch_shapes=[
            pltpu.VMEM((2 * RT_ROWS * ROW_SLABS, LANES), F32),
            pltpu.VMEM((BM * ROW_SLABS, LANES), F32),
            pltpu.SemaphoreType.DMA((2,)),
            pltpu.SemaphoreType.DMA(()),
        ],
    )
    return pl.pallas_call(
        _dispatch_kernel,
        grid_spec=grid_spec,
        out_shape=jax.ShapeDtypeStruct((N_SLOTS * ROW_SLABS, LANES), F32),
        compiler_params=_cparams(1),
        name="moe_dispatch",
    )(tables["start"], tables["len"], tables["off"], tables["pad_start"], tables["pad_len"], tables["n_valid"],
      h2, route_i, tables["off_rows"])


def _moe_kernel(be_ref, nv_ref, x_ref, wgu_ref, bgu_ref, wdn_ref, bdn_ref, y_ref, wgu_bf, wdn_bf):
    i = pl.program_id(0)
    n_valid = nv_ref[0]

    @pl.when(i < n_valid)
    def _():
        new_expert = jnp.logical_or(i == 0, be_ref[i] != be_ref[jnp.maximum(i - 1, 0)])

        @pl.when(new_expert)
        def _():
            wgu_bf[...] = wgu_ref[...].astype(BF16)
            wdn_bf[...] = wdn_ref[...].astype(BF16)

        x = _slab_columns(x_ref, 0, BM).astype(BF16)
        h = jnp.dot(x, wgu_bf[...], preferred_element_type=F32) + bgu_ref[...]
        g = jnp.minimum(h[:, :D_EXPERT], SWIGLU_LIMIT)
        u = jnp.clip(h[:, D_EXPERT:], -SWIGLU_LIMIT, SWIGLU_LIMIT)
        act = ((u + 1.0) * (g * jax.nn.sigmoid(SWIGLU_ALPHA * g))).astype(BF16)
        y = jnp.dot(act, wdn_bf[...], preferred_element_type=F32) + bdn_ref[...]
        for s in range(ROW_SLABS):
            y_ref[pl.ds(s, BM, stride=ROW_SLABS), :] = y[:, s * LANES:(s + 1) * LANES]

    @pl.when(i >= n_valid)
    def _():
        y_ref[...] = jnp.zeros_like(y_ref)


def _moe_blocks(tables, xs, w_gu, b_gu, w_down, b_down, li):
    wmap = lambda i, be, nv: (li, be[i], 0, 0)
    grid_spec = pltpu.PrefetchScalarGridSpec(
        num_scalar_prefetch=2,
        grid=(NB,),
        in_specs=[
            pl.BlockSpec((BM * ROW_SLABS, LANES), lambda i, be, nv: (jnp.minimum(i, nv[0] - 1), 0)),
            pl.BlockSpec((None, None, D_MODEL, 2 * D_EXPERT), wmap),
            pl.BlockSpec((None, None, 1, 2 * D_EXPERT), wmap),
            pl.BlockSpec((None, None, D_EXPERT, D_MODEL), wmap),
            pl.BlockSpec((None, None, 1, D_MODEL), wmap),
        ],
        out_specs=pl.BlockSpec((BM * ROW_SLABS, LANES), lambda i, be, nv: (i, 0)),
        scratch_shapes=[
            pltpu.VMEM((D_MODEL, 2 * D_EXPERT), BF16),
            pltpu.VMEM((D_EXPERT, D_MODEL), BF16),
        ],
    )
    return pl.pallas_call(
        _moe_kernel,
        grid_spec=grid_spec,
        out_shape=jax.ShapeDtypeStruct((N_SLOTS * ROW_SLABS, LANES), F32),
        compiler_params=_cparams(1),
        name="moe_experts",
    )(tables["block_expert"], tables["n_valid"], xs, w_gu,
      b_gu.reshape(DEPTH, N_EXPERTS, 1, 2 * D_EXPERT), w_down, b_down.reshape(DEPTH, N_EXPERTS, 1, D_MODEL))


def _combine_kernel(start_ref, len_ref, off_ref, y_hbm, x1_ref, ri_ref, rf_ref, offrow_ref, mod_ref,
                    lng_ref, lnb_ref, oc_ref, od_ref, ybuf, sem):
    i = pl.program_id(0)
    n_steps = pl.num_programs(0)
    slot = i % 2
    fetch = functools.partial(_tile_run_copies, start_ref=start_ref, len_ref=len_ref, off_ref=off_ref,
                              sorted_hbm=y_hbm, buf=ybuf, to_sorted=False)

    @pl.when(i == 0)
    def _():
        fetch(0, 0, sem=sem.at[0])

    @pl.when(i + 1 < n_steps)
    def _():
        fetch(i + 1, 1 - slot, sem=sem.at[1 - slot])

    _tile_runs_wait(slot, ybuf, sem.at[slot])
    pos = _packed_positions(ri_ref[...], offrow_ref[0, 0:1, :])
    gates = rf_ref[...]
    col = lax.broadcasted_iota(I32, (RT, RT_ROWS), 1).astype(F32)
    mix = jnp.zeros((RT, RT_ROWS), F32)
    for k in range(TOP_K):
        mix = mix + jnp.where(col == pos[k], gates[:, k:k + 1], 0.0)
    rows = _slab_columns(ybuf, slot * RT_ROWS, RT_ROWS).astype(BF16)
    ffn = jnp.dot(mix.astype(BF16), rows, preferred_element_type=F32)
    m = mod_ref[pl.ds(_mod_row(i // RT_PER_TM), 1), :]
    g2 = m[:, 5 * D_MODEL:6 * D_MODEL]
    out = _layer_norm_rows(DN_ALPHA * x1_ref[...] + g2 * ffn, lng_ref[...], lnb_ref[...])

    @pl.when(i < N_CTX // RT)
    def _():
        oc_ref[...] = out

    @pl.when(i >= N_CTX // RT)
    def _():
        od_ref[...] = out


def _combine(tables, y, x1, route_i, route_f, mod, ln_g, ln_b, li):
    n_ctx_tiles = N_CTX // RT
    grid_spec = pltpu.PrefetchScalarGridSpec(
        num_scalar_prefetch=3,
        grid=(N_RT,),
        in_specs=[
            pl.BlockSpec(memory_space=pl.ANY),
            pl.BlockSpec((RT, D_MODEL), lambda i, *_: (i, 0)),
            pl.BlockSpec((RT, LANES), lambda i, *_: (i, 0)),
            pl.BlockSpec((RT, LANES), lambda i, *_: (i, 0)),
            pl.BlockSpec((1, SUBLANES, LANES), lambda i, *_: (i, 0, 0)),
            pl.BlockSpec((None, SUBLANES, 6 * D_MODEL), lambda i, *_: (li, 0, 0)),
            pl.BlockSpec((1, D_MODEL), lambda i, *_: (0, 0)),
            pl.BlockSpec((1, D_MODEL), lambda i, *_: (0, 0)),
        ],
        out_specs=_pair_specs(RT, D_MODEL, n_ctx_tiles),
        scratch_shapes=[
            pltpu.VMEM((2 * RT_ROWS * ROW_SLABS, LANES), F32),
            pltpu.SemaphoreType.DMA((2,)),
        ],
    )
    return pl.pallas_call(
        _combine_kernel,
        grid_spec=grid_spec,
        out_shape=[jax.ShapeDtypeStruct((N_CTX, D_MODEL), F32), jax.ShapeDtypeStruct((N_DEC, D_MODEL), F32)],
        compiler_params=_cparams(1),
        name="moe_combine",
    )(tables["start"], tables["len"], tables["off"], y, x1, route_i, route_f, tables["off_rows"], mod, ln_g, ln_b)


def _routing_tables(tile_cnt):
    cnt = tile_cnt[:, 0, :N_EXPERTS]
    totals = jnp.sum(cnt, axis=0)
    padded = (totals + BM - 1) // BM * BM
    pends = jnp.cumsum(padded)
    pstarts = pends - padded
    start = pstarts[None, :] + jnp.cumsum(cnt, axis=0) - cnt
    off = jnp.cumsum(cnt, axis=1) - cnt
    n_valid = (pends[-1] // BM).astype(I32)
    block_start = jnp.arange(NB, dtype=I32) * BM
    block_e = jnp.minimum(jnp.sum(block_start[:, None] >= pends[None, :], axis=1), N_EXPERTS - 1)
    last_e = block_e[jnp.maximum(n_valid - 1, 0)]
    block_e = jnp.where(jnp.arange(NB) < n_valid, block_e, last_e)
    off_rows = jnp.pad(off.astype(F32), ((0, 0), (0, LANES - N_EXPERTS)))
    return {
        "start": start.reshape(-1).astype(I32),
        "len": cnt.reshape(-1).astype(I32),
        "off": off.reshape(-1).astype(I32),
        "pad_start": (pstarts + totals).astype(I32),
        "pad_len": (padded - totals).astype(I32),
        "n_valid": n_valid.reshape(1),
        "block_expert": block_e.astype(I32),
        "off_rows": jnp.broadcast_to(off_rows[:, None, :], (N_RT, SUBLANES, LANES)),
    }


def _layer_params(p, li):
    eye = jnp.eye(N_LRU_BLOCKS, dtype=F32)
    w_rg_full = jnp.einsum("dkgio,gh->gidkho", p["w_rg"][li], eye).reshape(LRU_WIDTH, 4 * LRU_WIDTH)
    pad = LANES - N_EXPERTS
    return {
        "sgu_ln_g": p["sgu_ln_g"][li].reshape(1, SGU_WIDTH),
        "sgu_ln_b": p["sgu_ln_b"][li].reshape(1, SGU_WIDTH),
        "w_spatial": p["w_spatial"][li],
        "b_spatial_full": jnp.repeat(p["b_spatial"][li].T, SGU_GROUP, axis=1),
        "conv_w": p["conv_w"][li],
        "conv_b": p["conv_b"][li].reshape(1, LRU_WIDTH),
        "w_rg_full": w_rg_full,
        "b_rg_full": p["b_rg"][li].reshape(1, 4 * LRU_WIDTH),
        "lru_log": p["lru_log"][li],
        "w_out_all": p["w_out"],
        "ln_g1": p["ln_g"][li, 0].reshape(1, D_MODEL),
        "ln_b1": p["ln_b"][li, 0].reshape(1, D_MODEL),
        "ln_g2": p["ln_g"][li, 1].reshape(1, D_MODEL),
        "ln_b2": p["ln_b"][li, 1].reshape(1, D_MODEL),
        "w_router_pad": jnp.pad(p["w_router"][li], ((0, 0), (0, pad))),
        "b_router_pad": jnp.pad(p["b_router"][li], (0, pad)).reshape(1, LANES),
    }


def kernel(x_prompt, x_sample, cache_k, cache_v, state_lru, c, c_ctx, w_mod, b_mod, w_in, lam, subln_g, sgu_ln_g, sgu_ln_b, w_spatial, b_spatial, conv_w, conv_b, w_rg, b_rg, lru_log, w_out, ln_g, ln_b, w_router, b_router, w_gu, b_gu, w_down, b_down):
    p = dict(sgu_ln_g=sgu_ln_g, sgu_ln_b=sgu_ln_b, w_spatial=w_spatial, b_spatial=b_spatial, conv_w=conv_w,
             conv_b=conv_b, w_rg=w_rg, b_rg=b_rg, lru_log=lru_log, w_out=w_out, ln_g=ln_g, ln_b=ln_b,
             w_router=w_router, b_router=b_router)
    cvec8 = jnp.concatenate([c_ctx[None, :], c, jnp.zeros((SUBLANES - 1 - DEC_BATCH, D_MODEL), F32)], axis=0)
    mod = _modulation(cvec8, w_mod, b_mod)
    x_pair = (x_prompt.reshape(N_CTX, D_MODEL), x_sample.reshape(N_DEC, D_MODEL))
    tables = _rope_tables()
    zero_state = jnp.zeros((BATCH, 1, 2, LRU_WIDTH), F32)
    new_k, new_v, new_s = [], [], []
    for li in range(DEPTH):
        lp = _layer_params(p, li)
        lam_init = 0.8 - 0.6 * math.exp(-0.3 * li)
        proj = _input_projection(*x_pair, mod, w_in, li)
        att_ctx, kc, vc = _context_attention(proj, lam[li], subln_g[li], lam_init)
        att_dec = _denoise_attention(proj, cache_k, cache_v, lam[li], subln_g[li], li, lam_init, tables)
        sgu_ctx, lru_ctx, h_ctx = _mixers(proj, zero_state, 0, lp, SEQ, BATCH, 0)
        sgu_dec, lru_dec, _ = _mixers(proj, state_lru, li, lp, DEC_SEQ, DEC_BATCH, N_CTX // DEC_SEQ)
        x1, h2, route_i, route_f, tile_cnt = _output_projection(
            x_pair, (att_ctx, att_dec), (sgu_ctx, sgu_dec), (lru_ctx, lru_dec), mod, lp, li)
        rt = _routing_tables(tile_cnt)
        xs = _dispatch(rt, h2, route_i)
        y = _moe_blocks(rt, xs, w_gu, b_gu, w_down, b_down, li)
        x_pair = _combine(rt, y, x1, route_i, route_f, mod, lp["ln_g2"], lp["ln_b2"], li)
        new_k.append(kc)
        new_v.append(vc)
        new_s.append(h_ctx)
    y_prompt = x_pair[0].reshape(BATCH, SEQ, D_MODEL)
    y_sample = x_pair[1].reshape(DEC_BATCH, DEC_SEQ, D_MODEL)
    return (y_prompt, y_sample, jnp.stack(new_k, axis=1), jnp.stack(new_v, axis=1), jnp.stack(new_s, axis=1))
```

```python
import functools
import math

import numpy as np
import jax
import jax.numpy as jnp
from jax import lax
from jax.experimental import pallas as pl
from jax.experimental.pallas import tpu as pltpu

F32 = jnp.float32
BF16 = jnp.bfloat16
I32 = jnp.int32

D_MODEL = 1024
BATCH = 32
SEQ = 256
DEPTH = 2
DEC_BATCH = 2
DEC_SEQ = 1024
PAST_LEN = 256
GRID_W = 64
HEAD_DIM = 64
ATT_WIDTH = D_MODEL // 2
SGU_WIDTH = D_MODEL // 4
LRU_WIDTH = D_MODEL // 4
N_ATT_HEADS = ATT_WIDTH // HEAD_DIM
ATT_HALF = HEAD_DIM // 2
ROPE_FREQS = ATT_HALF // 4
ROPE_THETA = 10000.0
CHUNK = 128
N_SGU_GROUPS = 4
SGU_GROUP = SGU_WIDTH // N_SGU_GROUPS
N_LRU_BLOCKS = 4
LRU_BLOCK = LRU_WIDTH // N_LRU_BLOCKS
CONV_W = 4
LRU_C = 8.0
IN_COLS = 3 * ATT_WIDTH + 2 * SGU_WIDTH + 2 * LRU_WIDTH
N_EXPERTS = 32
TOP_K = 4
D_EXPERT = D_MODEL
SWIGLU_LIMIT = 7.0
SWIGLU_ALPHA = 1.702
DN_ALPHA = (2 * DEPTH) ** 0.25
EPS = 1e-5

N_CTX = BATCH * SEQ
N_DEC = DEC_BATCH * DEC_SEQ
N_TOK = N_CTX + N_DEC

LANES = 128
SUBLANES = 8
ROW_SLABS = D_MODEL // LANES

TM = 512
CTX_TILES = N_CTX // TM
TILES_PER_DEC = DEC_SEQ // TM
N_TILES = N_TOK // TM
MOD_TN = 512
RT = 256
N_RT = N_TOK // RT
RT_PER_TM = TM // RT
RT_ROWS = RT * TOP_K
BM = 256
NB = N_TOK * TOP_K // BM + N_EXPERTS
N_SLOTS = NB * BM
RUN_BITS = tuple(1 << b for b in range(RT.bit_length() - 1, -1, -1))
PAD_BITS = tuple(1 << b for b in range(BM.bit_length() - 2, -1, -1))
VMEM_LIMIT = 56 * 1024 * 1024


def _cparams(n_axes):
    return pltpu.CompilerParams(
        dimension_semantics=("arbitrary",) * n_axes,
        vmem_limit_bytes=VMEM_LIMIT)


def _mod_row(i):
    return jnp.where(i < CTX_TILES, 0, 1 + (i - CTX_TILES) // TILES_PER_DEC)


def _layer_norm_rows(z, g, b):
    mu = jnp.mean(z, axis=-1, keepdims=True)
    zc = z - mu
    var = jnp.mean(zc * zc, axis=-1, keepdims=True)
    return zc * lax.rsqrt(var + EPS) * g + b


def _pair_specs(tile, width, n_ctx_tiles):
    ctx = pl.BlockSpec((tile, width), lambda i, *_: (jnp.minimum(i, n_ctx_tiles - 1), 0))
    dec = pl.BlockSpec((tile, width), lambda i, *_: (jnp.maximum(i - n_ctx_tiles, 0), 0))
    return [ctx, dec]


def _pair_read(i, n_ctx_tiles, ctx_ref, dec_ref):
    return jnp.where(i < n_ctx_tiles, ctx_ref[...], dec_ref[...])


def _mod_kernel(cvec_ref, w_ref, b_ref, o_ref):
    cv = cvec_ref[...]
    s = cv * jax.nn.sigmoid(cv)
    s_t = s.T
    w = w_ref[0]
    rows = [jnp.sum(s_t[:, r:r + 1] * w, axis=0, keepdims=True) for r in range(1 + DEC_BATCH)]
    rows.append(jnp.zeros((SUBLANES - 1 - DEC_BATCH, MOD_TN), F32))
    o_ref[0] = jnp.concatenate(rows, axis=0) + b_ref[0]


def _modulation(cvec8, w_mod, b_mod):
    n_out = w_mod.shape[-1]
    return pl.pallas_call(
        _mod_kernel,
        grid=(DEPTH, n_out // MOD_TN),
        in_specs=[
            pl.BlockSpec((SUBLANES, D_MODEL), lambda l, j: (0, 0)),
            pl.BlockSpec((1, D_MODEL, MOD_TN), lambda l, j: (l, 0, j)),
            pl.BlockSpec((1, 1, MOD_TN), lambda l, j: (l, 0, j)),
        ],
        out_specs=pl.BlockSpec((1, SUBLANES, MOD_TN), lambda l, j: (l, 0, j)),
        out_shape=jax.ShapeDtypeStruct((DEPTH, SUBLANES, n_out), F32),
        compiler_params=_cparams(2),
        name="modulation",
    )(cvec8, w_mod, b_mod.reshape(DEPTH, 1, n_out))


def _inproj_kernel(xc_ref, xd_ref, mod_ref, w_ref, o_ref, wbf_ref):
    i = pl.program_id(0)

    @pl.when(i == 0)
    def _():
        wbf_ref[...] = w_ref[...].astype(BF16)

    m = mod_ref[pl.ds(_mod_row(i), 1), :]
    sh1 = m[:, 0:D_MODEL]
    sc1 = m[:, D_MODEL:2 * D_MODEL]
    x = _pair_read(i, CTX_TILES, xc_ref, xd_ref)
    h = (x * (1.0 + sc1) + sh1).astype(BF16)
    o_ref[...] = jnp.dot(h, wbf_ref[...], preferred_element_type=F32)


def _input_projection(x_ctx, x_dec, mod, w_in, li):
    return pl.pallas_call(
        _inproj_kernel,
        grid=(N_TILES,),
        in_specs=_pair_specs(TM, D_MODEL, CTX_TILES) + [
            pl.BlockSpec((None, SUBLANES, 6 * D_MODEL), lambda i: (li, 0, 0)),
            pl.BlockSpec((None, D_MODEL, IN_COLS), lambda i: (li, 0, 0), pipeline_mode=pl.Buffered(1)),
        ],
        out_specs=pl.BlockSpec((TM, IN_COLS), lambda i: (i, 0)),
        out_shape=jax.ShapeDtypeStruct((N_TOK, IN_COLS), F32),
        scratch_shapes=[pltpu.VMEM((D_MODEL, IN_COLS), BF16)],
        compiler_params=_cparams(1),
        name="input_projection",
    )(x_ctx, x_dec, mod, w_in)


def _lambda(lam_ref, lam_init):
    lm = lam_ref[...]
    a = jnp.sum(lm[0:1] * lm[1:2], axis=-1, keepdims=True)
    b = jnp.sum(lm[2:3] * lm[3:4], axis=-1, keepdims=True)
    return jnp.exp(a) - jnp.exp(b) + lam_init


_NT = (((1,), (1,)), ((), ()))
_TN = (((0,), (0,)), ((), ()))
_ATT_SCALE = ATT_HALF ** -0.5


def _head_attention(q_parts, k_segs, v_segs, lam, g_col, lam_init):
    v_aug = [jnp.concatenate([v, jnp.ones_like(v)], axis=-1) for v in v_segs]
    normed = []
    for c in range(2):
        s_t = [lax.dot_general(k, q_parts[c], _NT, preferred_element_type=F32) * _ATT_SCALE
               for k in k_segs[c]]
        top = functools.reduce(jnp.maximum, [jnp.max(s, axis=0, keepdims=True) for s in s_t])
        acc = None
        for s, v in zip(s_t, v_aug):
            e = jnp.exp(s - top).astype(BF16)
            part = lax.dot_general(v, e, _TN, preferred_element_type=F32)
            acc = part if acc is None else acc + part
        normed.append(acc[:HEAD_DIM] * (1.0 / acc[HEAD_DIM:HEAD_DIM + 1]))
    o_t = normed[0] - lam * normed[1]
    ms = jnp.mean(o_t * o_t, axis=0, keepdims=True)
    return o_t * lax.rsqrt(ms + EPS) * g_col * (1.0 - lam_init)


def _ctx_attn_kernel(lam_ref, g_ref, q_ref, k_ref, v_ref, *rest, lam_init, stacked):
    if stacked:
        pk_ref, pv_ref, att_ref, ck_ref, cv_ref = rest
        ck_ref[0, 0] = pk_ref[0]
        cv_ref[0, 0] = pv_ref[0]
        put_k = lambda h, val: ck_ref.__setitem__((0, 1, h), val)
        put_v = lambda h, val: cv_ref.__setitem__((0, 1, h), val)
    else:
        att_ref, ck_ref, cv_ref = rest
        put_k = lambda h, val: ck_ref.__setitem__((0, h), val)
        put_v = lambda h, val: cv_ref.__setitem__((0, h), val)
    lam = _lambda(lam_ref, lam_init)
    k = k_ref[...]
    v = v_ref[...]
    qb = q_ref[...].astype(BF16)
    kb = k.astype(BF16)
    vb = v.astype(BF16)
    outs = []
    for h in range(N_ATT_HEADS):
        lo = h * HEAD_DIM
        cols = [slice(lo + c * ATT_HALF, lo + (c + 1) * ATT_HALF) for c in range(2)]
        outs.append(_head_attention([qb[:, cs] for cs in cols], [[kb[:, cs]] for cs in cols],
                                    [vb[:, lo:lo + HEAD_DIM]], lam, g_ref[...], lam_init))
        put_k(h, k[:, lo:lo + HEAD_DIM])
        put_v(h, v[:, lo:lo + HEAD_DIM])
    att_ref[...] = jnp.concatenate(outs, axis=0).T


def _context_attention(proj, lam, subln_g, lam_init, prev_kv=None):
    stacked = prev_kv is not None
    per_layer = pl.BlockSpec((1, N_ATT_HEADS, SEQ, HEAD_DIM), lambda b: (b, 0, 0, 0))
    if stacked:
        kv_shape = jax.ShapeDtypeStruct((BATCH, DEPTH, N_ATT_HEADS, SEQ, HEAD_DIM), F32)
        kv_spec = pl.BlockSpec((1, DEPTH, N_ATT_HEADS, SEQ, HEAD_DIM), lambda b: (b, 0, 0, 0, 0))
    else:
        kv_shape = jax.ShapeDtypeStruct((BATCH, N_ATT_HEADS, SEQ, HEAD_DIM), F32)
        kv_spec = per_layer
    return pl.pallas_call(
        functools.partial(_ctx_attn_kernel, lam_init=lam_init, stacked=stacked),
        grid=(BATCH,),
        in_specs=[
            pl.BlockSpec((4, ATT_HALF), lambda b: (0, 0)),
            pl.BlockSpec((HEAD_DIM, 1), lambda b: (0, 0)),
            pl.BlockSpec((SEQ, ATT_WIDTH), lambda b: (b, 0)),
            pl.BlockSpec((SEQ, ATT_WIDTH), lambda b: (b, 1)),
            pl.BlockSpec((SEQ, ATT_WIDTH), lambda b: (b, 2)),
        ] + ([per_layer, per_layer] if stacked else []),
        out_specs=[pl.BlockSpec((SEQ, ATT_WIDTH), lambda b: (b, 0)), kv_spec, kv_spec],
        out_shape=[jax.ShapeDtypeStruct((N_CTX, ATT_WIDTH), F32), kv_shape, kv_shape],
        compiler_params=_cparams(1),
        name="context_attention",
    )(lam, subln_g.reshape(HEAD_DIM, 1), proj, proj, proj, *(prev_kv if stacked else ()))


def _rope_tables():
    t = np.arange(DEC_SEQ)
    pos = np.stack([t // GRID_W, t % GRID_W], axis=1).astype(np.float32)
    inv = (np.float32(ROPE_THETA) ** (-np.arange(ROPE_FREQS, dtype=np.float32) / np.float32(ROPE_FREQS)))
    j = np.arange(HEAD_DIM)
    d = j % ATT_HALF
    axis = d // (2 * ROPE_FREQS)
    u = d % (2 * ROPE_FREQS)
    ang = pos[:, axis] * inv[u % ROPE_FREQS][None, :].astype(np.float32)
    cos = np.cos(ang).astype(np.float32)
    sin = np.sin(ang).astype(np.float32)
    first = (u < ROPE_FREQS)[None, :]
    s_next = np.where(first, -sin, 0.0).astype(np.float32)
    s_prev = np.where(first, 0.0, sin).astype(np.float32)
    tile = lambda a: jnp.asarray(np.tile(a, (1, N_ATT_HEADS)))
    return tile(cos), tile(s_next), tile(s_prev)


def _rotate(x, cos, s_next, s_prev):
    width = x.shape[-1]
    return (x * cos + pltpu.roll(x, width - ROPE_FREQS, axis=1) * s_next
            + pltpu.roll(x, ROPE_FREQS, axis=1) * s_prev)


QB = 256
Q_STEPS = DEC_SEQ // QB


def _dec_attn_kernel(lam_ref, g_ref, q_ref, k_ref, v_ref, ck_ref, cv_ref,
                     cq_ref, snq_ref, spq_ref, ck_tab, snk_tab, spk_tab,
                     att_ref, krot_ref, *, lam_init):
    j = pl.program_id(1)

    @pl.when(j == 0)
    def _():
        krot_ref[...] = _rotate(k_ref[...], ck_tab[...], snk_tab[...], spk_tab[...]).astype(BF16)

    lam = _lambda(lam_ref, lam_init)
    qb = _rotate(q_ref[...], cq_ref[...], snq_ref[...], spq_ref[...]).astype(BF16)
    kb = krot_ref[...]
    vb = v_ref[...].astype(BF16)
    outs = []
    for h in range(N_ATT_HEADS):
        lo = h * HEAD_DIM
        pk = ck_ref[0, 0, h].astype(BF16)
        pv = cv_ref[0, 0, h].astype(BF16)
        cols = [slice(lo + c * ATT_HALF, lo + (c + 1) * ATT_HALF) for c in range(2)]
        k_segs = [[pk[:, c * ATT_HALF:(c + 1) * ATT_HALF], kb[:, cols[c]]] for c in range(2)]
        outs.append(_head_attention([qb[:, cs] for cs in cols], k_segs, [pv, vb[:, lo:lo + HEAD_DIM]],
                                    lam, g_ref[...], lam_init))
    att_ref[...] = jnp.concatenate(outs, axis=0).T


def _denoise_attention(proj, cache_k, cache_v, lam, subln_g, li, lam_init, tables):
    cos, s_next, s_prev = tables
    row0 = N_CTX // QB
    seq0 = N_CTX // DEC_SEQ
    q_tab = pl.BlockSpec((QB, ATT_WIDTH), lambda b, j: (j, 0))
    k_tab = pl.BlockSpec((DEC_SEQ, ATT_WIDTH), lambda b, j: (0, 0))
    cache_spec = pl.BlockSpec((1, 1, N_ATT_HEADS, PAST_LEN, HEAD_DIM), lambda b, j: (b, li, 0, 0, 0))
    return pl.pallas_call(
        functools.partial(_dec_attn_kernel, lam_init=lam_init),
        grid=(DEC_BATCH, Q_STEPS),
        in_specs=[
            pl.BlockSpec((4, ATT_HALF), lambda b, j: (0, 0)),
            pl.BlockSpec((HEAD_DIM, 1), lambda b, j: (0, 0)),
            pl.BlockSpec((QB, ATT_WIDTH), lambda b, j: (row0 + b * Q_STEPS + j, 0)),
            pl.BlockSpec((DEC_SEQ, ATT_WIDTH), lambda b, j: (seq0 + b, 1)),
            pl.BlockSpec((DEC_SEQ, ATT_WIDTH), lambda b, j: (seq0 + b, 2)),
            cache_spec, cache_spec,
            q_tab, q_tab, q_tab, k_tab, k_tab, k_tab,
        ],
        out_specs=pl.BlockSpec((QB, ATT_WIDTH), lambda b, j: (b * Q_STEPS + j, 0)),
        out_shape=jax.ShapeDtypeStruct((N_DEC, ATT_WIDTH), F32),
        scratch_shapes=[pltpu.VMEM((DEC_SEQ, ATT_WIDTH), BF16)],
        compiler_params=_cparams(2),
        name="denoise_attention",
    )(lam, subln_g.reshape(HEAD_DIM, 1), proj, proj, proj, cache_k, cache_v,
      cos, s_next, s_prev, cos, s_next, s_prev)


def _softplus(z):
    return jnp.maximum(z, 0.0) + jnp.log1p(jnp.exp(-jnp.abs(z)))


def _mixer_kernel(su_ref, sv_ref, rx_ref, rg_ref, h0_ref, lng_ref, lnb_ref, ws_ref, bs_ref,
                  cw_ref, cb_ref, wrg_ref, brg_ref, lrulog_ref,
                  sgu_ref, lru_ref, hlast_ref, a_scr, b_scr, h_scr, *, seq_len):
    vn = _layer_norm_rows(sv_ref[...], lng_ref[...], lnb_ref[...])
    lane_group = lax.broadcasted_iota(I32, (CHUNK, SGU_WIDTH), 1) // SGU_GROUP
    for n in range(seq_len // CHUNK):
        rows = slice(n * CHUNK, (n + 1) * CHUNK)
        vc = vn[rows].astype(BF16)
        s = jnp.zeros((CHUNK, SGU_WIDTH), F32)
        for g in range(N_SGU_GROUPS):
            sg = jnp.dot(ws_ref[g].astype(BF16), vc, preferred_element_type=F32)
            s = jnp.where(lane_group == g, sg, s)
        sgu_ref[rows, :] = su_ref[rows, :] * (s + bs_ref[...])

    x = rx_ref[...]
    row = lax.broadcasted_iota(I32, (seq_len, LRU_WIDTH), 0)

    def shifted(val, d, fill):
        rolled = pltpu.roll(val, d % seq_len, axis=0)
        inside = (row >= d) if d > 0 else (row < seq_len + d)
        return jnp.where(inside, rolled, fill)

    left = CONV_W // 2
    xc = cb_ref[...] + x * cw_ref[left:left + 1, :]
    for tap in range(CONV_W):
        if tap != left:
            xc = xc + shifted(x, left - tap, 0.0) * cw_ref[tap:tap + 1, :]
    gates = jax.nn.sigmoid(jnp.dot(xc.astype(BF16), wrg_ref[...].astype(BF16),
                                   preferred_element_type=F32) + brg_ref[...])
    in_chunk = row % SUBLANES
    n_chunks = seq_len // SUBLANES
    h0 = h0_ref[0, 0]
    lasts = []
    for direction in range(2):
        reverse = direction == 1
        base = direction * 2 * LRU_WIDTH
        r = gates[:, base:base + LRU_WIDTH]
        gi = gates[:, base + LRU_WIDTH:base + 2 * LRU_WIDTH]
        log_a = -LRU_C * r * _softplus(-lrulog_ref[direction:direction + 1, :])
        a = jnp.exp(log_a)
        b = jnp.sqrt(-jnp.tanh(log_a) * (a * a + 1.0)) * gi * xc
        for d in (1, 2, 4):
            if reverse:
                inside = in_chunk < SUBLANES - d
                a_n = jnp.where(inside, pltpu.roll(a, seq_len - d, axis=0), 1.0)
                b_n = jnp.where(inside, pltpu.roll(b, seq_len - d, axis=0), 0.0)
            else:
                inside = in_chunk >= d
                a_n = jnp.where(inside, pltpu.roll(a, d, axis=0), 1.0)
                b_n = jnp.where(inside, pltpu.roll(b, d, axis=0), 0.0)
            b = a * b_n + b
            a = a * a_n
        a_scr[...] = a
        b_scr[...] = b

        def chunk_step(c, carry, reverse=reverse):
            cc = n_chunks - 1 - c if reverse else c
            off = pl.multiple_of(cc * SUBLANES, SUBLANES)
            hc = a_scr[pl.ds(off, SUBLANES), :] * carry + b_scr[pl.ds(off, SUBLANES), :]
            if reverse:
                h_scr[pl.ds(off, SUBLANES), :] = h_scr[pl.ds(off, SUBLANES), :] + hc
                return hc[0:1, :]
            h_scr[pl.ds(off, SUBLANES), :] = hc
            return hc[SUBLANES - 1:SUBLANES, :]

        lasts.append(lax.fori_loop(0, n_chunks, chunk_step, h0[direction:direction + 1, :]))
    lru_ref[...] = h_scr[...] * jax.nn.gelu(rg_ref[...])
    hlast_ref[0] = jnp.concatenate(lasts, axis=0)


def _mixers(proj, h0, h0_layer, lp, seq_len, n_seq, row_block0):
    col0 = 3 * ATT_WIDTH // SGU_WIDTH
    col = lambda c: pl.BlockSpec((seq_len, SGU_WIDTH), lambda b: (row_block0 + b, col0 + c))
    full = lambda shape: pl.BlockSpec(shape, lambda b: (0,) * len(shape))
    out_rows = pl.BlockSpec((seq_len, SGU_WIDTH), lambda b: (b, 0))
    return pl.pallas_call(
        functools.partial(_mixer_kernel, seq_len=seq_len),
        grid=(n_seq,),
        in_specs=[
            col(0), col(1), col(2), col(3),
            pl.BlockSpec((1, 1, 2, LRU_WIDTH), lambda b: (b, h0_layer, 0, 0)),
            full((1, SGU_WIDTH)), full((1, SGU_WIDTH)),
            full((N_SGU_GROUPS, CHUNK, CHUNK)), full((CHUNK, SGU_WIDTH)),
            full((CONV_W, LRU_WIDTH)), full((1, LRU_WIDTH)),
            full((LRU_WIDTH, 4 * LRU_WIDTH)), full((1, 4 * LRU_WIDTH)),
            full((2, LRU_WIDTH)),
        ],
        out_specs=[out_rows, out_rows, pl.BlockSpec((1, 2, LRU_WIDTH), lambda b: (b, 0, 0))],
        out_shape=[
            jax.ShapeDtypeStruct((n_seq * seq_len, SGU_WIDTH), F32),
            jax.ShapeDtypeStruct((n_seq * seq_len, LRU_WIDTH), F32),
            jax.ShapeDtypeStruct((n_seq, 2, LRU_WIDTH), F32),
        ],
        scratch_shapes=[pltpu.VMEM((seq_len, LRU_WIDTH), F32)] * 3,
        compiler_params=_cparams(1),
        name="mixers_%d" % seq_len,
    )(proj, proj, proj, proj, h0, lp["sgu_ln_g"], lp["sgu_ln_b"], lp["w_spatial"], lp["b_spatial_full"],
      lp["conv_w"], lp["conv_b"], lp["w_rg_full"], lp["b_rg_full"], lp["lru_log"])


def _split_bf16(v):
    hi = v.astype(BF16)
    lo = (v - hi.astype(F32)).astype(BF16)
    return hi, lo


def _outproj_kernel(xc_ref, xd_ref, ac_ref, ad_ref, sc_ref, sd_ref, lc_ref, ld_ref,
                    mod_ref, wout_ref, lng_ref, lnb_ref, wr_ref, br_ref,
                    x1_ref, h2_ref, ri_ref, rf_ref, cnt_ref, wbf_ref):
    i = pl.program_id(0)

    @pl.when(i == 0)
    def _():
        wbf_ref[...] = wout_ref[...].astype(BF16)

    m = mod_ref[pl.ds(_mod_row(i), 1), :]
    g1 = m[:, 2 * D_MODEL:3 * D_MODEL]
    sh2 = m[:, 3 * D_MODEL:4 * D_MODEL]
    sc2 = m[:, 4 * D_MODEL:5 * D_MODEL]
    a0, a1 = ATT_WIDTH, ATT_WIDTH + SGU_WIDTH
    x = _pair_read(i, CTX_TILES, xc_ref, xd_ref)
    att = _pair_read(i, CTX_TILES, ac_ref, ad_ref).astype(BF16)
    sgu = _pair_read(i, CTX_TILES, sc_ref, sd_ref).astype(BF16)
    lru = _pair_read(i, CTX_TILES, lc_ref, ld_ref).astype(BF16)
    mix = (jnp.dot(att, wbf_ref[0:a0, :], preferred_element_type=F32)
           + jnp.dot(sgu, wbf_ref[a0:a1, :], preferred_element_type=F32)
           + jnp.dot(lru, wbf_ref[a1:, :], preferred_element_type=F32))
    x1 = _layer_norm_rows(DN_ALPHA * x + g1 * mix, lng_ref[...], lnb_ref[...])
    x1_ref[...] = x1
    h2 = x1 * (1.0 + sc2) + sh2
    h2_ref[...] = h2.astype(BF16)

    h_hi, h_lo = _split_bf16(h2)
    w_hi, w_lo = _split_bf16(wr_ref[...])
    logits = (jnp.dot(h_hi, w_hi, preferred_element_type=F32)
              + jnp.dot(h_lo, w_hi, preferred_element_type=F32)
              + jnp.dot(h_hi, w_lo, preferred_element_type=F32)) + br_ref[...]
    lane = lax.broadcasted_iota(I32, (TM, LANES), 1)
    lane_f = lane.astype(F32)
    neg_inf = jnp.float32(-jnp.inf)
    work = jnp.where(lane < N_EXPERTS, logits, neg_inf)
    vals, idxs = [], []
    for _ in range(TOP_K):
        top = jnp.max(work, axis=-1, keepdims=True)
        idx = jnp.min(jnp.where(work == top, lane_f, float(LANES)), axis=-1, keepdims=True)
        vals.append(top)
        idxs.append(idx)
        work = jnp.where(lane_f == idx, neg_inf, work)
    exps = [jnp.exp(v - vals[0]) for v in vals]
    denom = exps[0] + exps[1] + exps[2] + exps[3]
    onehot = jnp.zeros((TM, LANES), F32)
    for idx in idxs:
        onehot = onehot + (lane_f == idx).astype(F32)
    r_i = lax.broadcasted_iota(I32, (RT, RT), 0)
    c_i = lax.broadcasted_iota(I32, (RT, RT), 1)
    tri = (r_i > c_i).astype(F32).astype(BF16)
    before = jnp.concatenate(
        [jnp.dot(tri, onehot[t * RT:(t + 1) * RT].astype(BF16), preferred_element_type=F32)
         for t in range(RT_PER_TM)], axis=0)
    ri = jnp.zeros((TM, LANES), F32)
    rf = jnp.zeros((TM, LANES), F32)
    for k in range(TOP_K):
        rank = jnp.sum(jnp.where(lane_f == idxs[k], before, 0.0), axis=-1, keepdims=True)
        ri = jnp.where(lane == k, idxs[k], ri)
        ri = jnp.where(lane == TOP_K + k, rank, ri)
        rf = jnp.where(lane == k, exps[k] / denom, rf)
    ri_ref[...] = ri.astype(I32)
    rf_ref[...] = rf
    for t in range(RT_PER_TM):
        total = jnp.sum(onehot[t * RT:(t + 1) * RT], axis=0, keepdims=True)
        cnt_ref[t] = jnp.broadcast_to(total, (SUBLANES, LANES)).astype(I32)


def _output_projection(x_pair, att_pair, sgu_pair, lru_pair, mod, lp, li):
    rows = lambda w: pl.BlockSpec((TM, w), lambda i: (i, 0))
    full = lambda shape: pl.BlockSpec(shape, lambda i: (0,) * len(shape))
    return pl.pallas_call(
        _outproj_kernel,
        grid=(N_TILES,),
        in_specs=(_pair_specs(TM, D_MODEL, CTX_TILES) + _pair_specs(TM, ATT_WIDTH, CTX_TILES)
                  + _pair_specs(TM, SGU_WIDTH, CTX_TILES) + _pair_specs(TM, LRU_WIDTH, CTX_TILES) + [
            pl.BlockSpec((None, SUBLANES, 6 * D_MODEL), lambda i: (li, 0, 0)),
            pl.BlockSpec((None, D_MODEL, D_MODEL), lambda i: (li, 0, 0), pipeline_mode=pl.Buffered(1)),
            full((1, D_MODEL)), full((1, D_MODEL)),
            full((D_MODEL, LANES)), full((1, LANES)),
        ]),
        out_specs=[
            rows(D_MODEL), rows(D_MODEL), rows(LANES), rows(LANES),
            pl.BlockSpec((RT_PER_TM, SUBLANES, LANES), lambda i: (i, 0, 0)),
        ],
        out_shape=[
            jax.ShapeDtypeStruct((N_TOK, D_MODEL), F32),
            jax.ShapeDtypeStruct((N_TOK, D_MODEL), BF16),
            jax.ShapeDtypeStruct((N_TOK, LANES), I32),
            jax.ShapeDtypeStruct((N_TOK, LANES), F32),
            jax.ShapeDtypeStruct((N_RT, SUBLANES, LANES), I32),
        ],
        scratch_shapes=[pltpu.VMEM((D_MODEL, D_MODEL), BF16)],
        compiler_params=_cparams(1),
        name="output_projection",
    )(*x_pair, *att_pair, *sgu_pair, *lru_pair, mod, lp["w_out_all"], lp["ln_g1"], lp["ln_b1"],
      lp["w_router_pad"], lp["b_router_pad"])


def _row_ds(row, n_rows):
    return pl.ds(pl.multiple_of(row * ROW_SLABS, ROW_SLABS), n_rows * ROW_SLABS)


def _for_each_run_piece(length, bits, fn):
    for bit in bits:
        done = length & (-2 * bit)
        @pl.when((length & bit) != 0)
        def _(done=done, bit=bit):
            fn(done, bit)


def _tile_run_copies(tile, slot, start_ref, len_ref, off_ref, sorted_hbm, buf, sem, to_sorted):
    for e in range(N_EXPERTS):
        t = tile * N_EXPERTS + e
        start, length, off = start_ref[t], len_ref[t], off_ref[t]

        def piece(done, bit, start=start, off=off):
            packed = buf.at[_row_ds(slot * RT_ROWS + off + done, bit), :]
            srt = sorted_hbm.at[_row_ds(start + done, bit), :]
            if to_sorted:
                pltpu.make_async_copy(packed, srt, sem).start()
            else:
                pltpu.make_async_copy(srt, packed, sem).start()

        _for_each_run_piece(length, RUN_BITS, piece)


def _tile_runs_wait(slot, buf, sem):
    region = buf.at[_row_ds(slot * RT_ROWS, RT_ROWS), :]
    pltpu.make_async_copy(region, region, sem).wait()


def _packed_positions(ri, off_row):
    lane = lax.broadcasted_iota(I32, (RT, LANES), 1)
    pos = []
    for k in range(TOP_K):
        e_k = ri[:, k:k + 1]
        base = jnp.sum(jnp.where(lane == e_k, off_row, 0.0), axis=-1, keepdims=True)
        pos.append(base + ri[:, TOP_K + k:TOP_K + k + 1].astype(F32))
    return pos


def _slab_columns(buf, row0, n_rows):
    return jnp.concatenate(
        [buf[pl.ds(row0 * ROW_SLABS + s, n_rows, stride=ROW_SLABS), :] for s in range(ROW_SLABS)], axis=-1)


def _dispatch_kernel(start_ref, len_ref, off_ref, pstart_ref, plen_ref, nv_ref,
                     h2_ref, ri_ref, offrow_ref, xs_hbm, xc, zbuf, sem, zsem):
    i = pl.program_id(0)
    n_steps = pl.num_programs(0)
    slot = i % 2

    @pl.when(i == 0)
    def _():
        zbuf[...] = jnp.zeros_like(zbuf)

        def zero_fill(wait):
            def go(cp):
                cp.wait() if wait else cp.start()

            def per_expert(e, carry):
                def piece(done, bit):
                    go(pltpu.make_async_copy(zbuf.at[pl.ds(0, bit * ROW_SLABS), :],
                                             xs_hbm.at[_row_ds(pstart_ref[e] + done, bit), :], zsem))
                _for_each_run_piece(plen_ref[e], PAD_BITS, piece)
                return carry
            lax.fori_loop(0, N_EXPERTS, per_expert, 0)

            def per_block(b, carry):
                go(pltpu.make_async_copy(zbuf, xs_hbm.at[_row_ds(b * BM, BM), :], zsem))
                return carry
            lax.fori_loop(nv_ref[0], NB, per_block, 0)

        zero_fill(False)
        zero_fill(True)

    @pl.when(i >= 2)
    def _():
        _tile_runs_wait(slot, xc, sem.at[slot])

    pos = _packed_positions(ri_ref[...], offrow_ref[0, 0:1, :])
    col = lax.broadcasted_iota(I32, (RT, RT_ROWS), 1).astype(F32)
    sel = jnp.zeros((RT, RT_ROWS), F32)
    for p in pos:
        sel = sel + (col == p).astype(F32)
    packed = lax.dot_general(sel.astype(BF16), h2_ref[...], _TN, preferred_element_type=F32)
    for s in range(ROW_SLABS):
        xc[pl.ds(slot * RT_ROWS * ROW_SLABS + s, RT_ROWS, stride=ROW_SLABS), :] = (
            packed[:, s * LANES:(s + 1) * LANES])
    _tile_run_copies(i, slot, start_ref, len_ref, off_ref, xs_hbm, xc, sem.at[slot], to_sorted=True)

    @pl.when(i == n_steps - 1)
    def _():
        _tile_runs_wait(1 - slot, xc, sem.at[1 - slot])
        _tile_runs_wait(slot, xc, sem.at[slot])


def _dispatch(tables, h2, route_i):
    grid_spec = pltpu.PrefetchScalarGridSpec(
        num_scalar_prefetch=6,
        grid=(N_RT,),
        in_specs=[
            pl.BlockSpec((RT, D_MODEL), lambda i, *_: (i, 0)),
            pl.BlockSpec((RT, LANES), lambda i, *_: (i, 0)),
            pl.BlockSpec((1, SUBLANES, LANES), lambda i, *_: (i, 0, 0)),
        ],
        out_specs=pl.BlockSpec(memory_space=pl.ANY),
        scratch_shapes=[
            pltpu.VMEM((2 * RT_ROWS * ROW_SLABS, LANES), F32),
            pltpu.VMEM((BM * ROW_SLABS, LANES), F32),
            pltpu.SemaphoreType.DMA((2,)),
            pltpu.SemaphoreType.DMA(()),
        ],
    )
    return pl.pallas_call(
        _dispatch_kernel,
        grid_spec=grid_spec,
        out_shape=jax.ShapeDtypeStruct((N_SLOTS * ROW_SLABS, LANES), F32),
        compiler_params=_cparams(1),
        name="moe_dispatch",
    )(tables["start"], tables["len"], tables["off"], tables["pad_start"], tables["pad_len"], tables["n_valid"],
      h2, route_i, tables["off_rows"])


def _moe_kernel(be_ref, nxt_ref, nv_ref, x_ref, wgu_hbm, bgu_ref, wdn_hbm, bdn_ref, y_ref,
                gu_stage, dn_stage, wgu_bf, wdn_bf, sem, *, li):
    i = pl.program_id(0)
    n_valid = nv_ref[0]

    def weight_copies(e):
        return (pltpu.make_async_copy(wgu_hbm.at[li, e], gu_stage, sem.at[0]),
                pltpu.make_async_copy(wdn_hbm.at[li, e], dn_stage, sem.at[1]))

    def fetch(e):
        for cp in weight_copies(e):
            cp.start()

    @pl.when(i < n_valid)
    def _():
        e = be_ref[i]
        new_expert = jnp.logical_or(i == 0, e != be_ref[jnp.maximum(i - 1, 0)])

        @pl.when(i == 0)
        def _():
            fetch(e)

        @pl.when(new_expert)
        def _():
            for cp in weight_copies(e):
                cp.wait()
            wgu_bf[...] = gu_stage[...].astype(BF16)
            wdn_bf[...] = dn_stage[...].astype(BF16)

            @pl.when(nxt_ref[i] >= 0)
            def _():
                fetch(nxt_ref[i])

        x = _slab_columns(x_ref, 0, BM).astype(BF16)
        h = jnp.dot(x, wgu_bf[...], preferred_element_type=F32) + bgu_ref[...]
        g = jnp.minimum(h[:, :D_EXPERT], SWIGLU_LIMIT)
        u = jnp.clip(h[:, D_EXPERT:], -SWIGLU_LIMIT, SWIGLU_LIMIT)
        act = ((u + 1.0) * (g * jax.nn.sigmoid(SWIGLU_ALPHA * g))).astype(BF16)
        y = jnp.dot(act, wdn_bf[...], preferred_element_type=F32) + bdn_ref[...]
        for s in range(ROW_SLABS):
            y_ref[pl.ds(s, BM, stride=ROW_SLABS), :] = y[:, s * LANES:(s + 1) * LANES]

    @pl.when(i >= n_valid)
    def _():
        y_ref[...] = jnp.zeros_like(y_ref)


def _moe_blocks(tables, xs, w_gu, b_gu, w_down, b_down, li):
    bmap = lambda i, be, nxt, nv: (li, be[i], 0, 0)
    grid_spec = pltpu.PrefetchScalarGridSpec(
        num_scalar_prefetch=3,
        grid=(NB,),
        in_specs=[
            pl.BlockSpec((BM * ROW_SLABS, LANES), lambda i, be, nxt, nv: (jnp.minimum(i, nv[0] - 1), 0)),
            pl.BlockSpec(memory_space=pl.ANY),
            pl.BlockSpec((None, None, 1, 2 * D_EXPERT), bmap),
            pl.BlockSpec(memory_space=pl.ANY),
            pl.BlockSpec((None, None, 1, D_MODEL), bmap),
        ],
        out_specs=pl.BlockSpec((BM * ROW_SLABS, LANES), lambda i, be, nxt, nv: (i, 0)),
        scratch_shapes=[
            pltpu.VMEM((D_MODEL, 2 * D_EXPERT), F32),
            pltpu.VMEM((D_EXPERT, D_MODEL), F32),
            pltpu.VMEM((D_MODEL, 2 * D_EXPERT), BF16),
            pltpu.VMEM((D_EXPERT, D_MODEL), BF16),
            pltpu.SemaphoreType.DMA((2,)),
        ],
    )
    return pl.pallas_call(
        functools.partial(_moe_kernel, li=li),
        grid_spec=grid_spec,
        out_shape=jax.ShapeDtypeStruct((N_SLOTS * ROW_SLABS, LANES), F32),
        compiler_params=_cparams(1),
        name="moe_experts",
    )(tables["block_expert"], tables["next_expert"], tables["n_valid"], xs, w_gu,
      b_gu.reshape(DEPTH, N_EXPERTS, 1, 2 * D_EXPERT), w_down, b_down.reshape(DEPTH, N_EXPERTS, 1, D_MODEL))


def _combine_kernel(start_ref, len_ref, off_ref, y_hbm, x1_ref, ri_ref, rf_ref, offrow_ref, mod_ref,
                    lng_ref, lnb_ref, oc_ref, od_ref, ybuf, sem):
    i = pl.program_id(0)
    n_steps = pl.num_programs(0)
    slot = i % 2
    fetch = functools.partial(_tile_run_copies, start_ref=start_ref, len_ref=len_ref, off_ref=off_ref,
                              sorted_hbm=y_hbm, buf=ybuf, to_sorted=False)

    @pl.when(i == 0)
    def _():
        fetch(0, 0, sem=sem.at[0])

    @pl.when(i + 1 < n_steps)
    def _():
        fetch(i + 1, 1 - slot, sem=sem.at[1 - slot])

    _tile_runs_wait(slot, ybuf, sem.at[slot])
    pos = _packed_positions(ri_ref[...], offrow_ref[0, 0:1, :])
    gates = rf_ref[...]
    col = lax.broadcasted_iota(I32, (RT, RT_ROWS), 1).astype(F32)
    mix = jnp.zeros((RT, RT_ROWS), F32)
    for k in range(TOP_K):
        mix = mix + jnp.where(col == pos[k], gates[:, k:k + 1], 0.0)
    rows = _slab_columns(ybuf, slot * RT_ROWS, RT_ROWS).astype(BF16)
    ffn = jnp.dot(mix.astype(BF16), rows, preferred_element_type=F32)
    m = mod_ref[pl.ds(_mod_row(i // RT_PER_TM), 1), :]
    g2 = m[:, 5 * D_MODEL:6 * D_MODEL]
    out = _layer_norm_rows(DN_ALPHA * x1_ref[...] + g2 * ffn, lng_ref[...], lnb_ref[...])

    @pl.when(i < N_CTX // RT)
    def _():
        oc_ref[...] = out

    @pl.when(i >= N_CTX // RT)
    def _():
        od_ref[...] = out


def _combine(tables, y, x1, route_i, route_f, mod, ln_g, ln_b, li):
    n_ctx_tiles = N_CTX // RT
    grid_spec = pltpu.PrefetchScalarGridSpec(
        num_scalar_prefetch=3,
        grid=(N_RT,),
        in_specs=[
            pl.BlockSpec(memory_space=pl.ANY),
            pl.BlockSpec((RT, D_MODEL), lambda i, *_: (i, 0)),
            pl.BlockSpec((RT, LANES), lambda i, *_: (i, 0)),
            pl.BlockSpec((RT, LANES), lambda i, *_: (i, 0)),
            pl.BlockSpec((1, SUBLANES, LANES), lambda i, *_: (i, 0, 0)),
            pl.BlockSpec((None, SUBLANES, 6 * D_MODEL), lambda i, *_: (li, 0, 0)),
            pl.BlockSpec((1, D_MODEL), lambda i, *_: (0, 0)),
            pl.BlockSpec((1, D_MODEL), lambda i, *_: (0, 0)),
        ],
        out_specs=_pair_specs(RT, D_MODEL, n_ctx_tiles),
        scratch_shapes=[
            pltpu.VMEM((2 * RT_ROWS * ROW_SLABS, LANES), F32),
            pltpu.SemaphoreType.DMA((2,)),
        ],
    )
    return pl.pallas_call(
        _combine_kernel,
        grid_spec=grid_spec,
        out_shape=[jax.ShapeDtypeStruct((N_CTX, D_MODEL), F32), jax.ShapeDtypeStruct((N_DEC, D_MODEL), F32)],
        compiler_params=_cparams(1),
        name="moe_combine",
    )(tables["start"], tables["len"], tables["off"], y, x1, route_i, route_f, tables["off_rows"], mod, ln_g, ln_b)


def _routing_tables(tile_cnt):
    cnt = tile_cnt[:, 0, :N_EXPERTS]
    totals = jnp.sum(cnt, axis=0)
    padded = (totals + BM - 1) // BM * BM
    pends = jnp.cumsum(padded)
    pstarts = pends - padded
    start = pstarts[None, :] + jnp.cumsum(cnt, axis=0) - cnt
    off = jnp.cumsum(cnt, axis=1) - cnt
    n_valid = (pends[-1] // BM).astype(I32)
    block_start = jnp.arange(NB, dtype=I32) * BM
    block_e = jnp.minimum(jnp.sum(block_start[:, None] >= pends[None, :], axis=1), N_EXPERTS - 1)
    last_e = block_e[jnp.maximum(n_valid - 1, 0)]
    block_e = jnp.where(jnp.arange(NB) < n_valid, block_e, last_e)
    run_end = jnp.sum(block_e[None, :] <= block_e[:, None], axis=1)
    next_e = jnp.where(run_end < n_valid, block_e[jnp.minimum(run_end, NB - 1)], -1)
    off_rows = jnp.pad(off.astype(F32), ((0, 0), (0, LANES - N_EXPERTS)))
    return {
        "start": start.reshape(-1).astype(I32),
        "len": cnt.reshape(-1).astype(I32),
        "off": off.reshape(-1).astype(I32),
        "pad_start": (pstarts + totals).astype(I32),
        "pad_len": (padded - totals).astype(I32),
        "n_valid": n_valid.reshape(1),
        "block_expert": block_e.astype(I32),
        "next_expert": next_e.astype(I32),
        "off_rows": jnp.broadcast_to(off_rows[:, None, :], (N_RT, SUBLANES, LANES)),
    }


def _layer_params(p, li):
    eye = jnp.eye(N_LRU_BLOCKS, dtype=F32)
    w_rg_full = jnp.einsum("dkgio,gh->gidkho", p["w_rg"][li], eye).reshape(LRU_WIDTH, 4 * LRU_WIDTH)
    pad = LANES - N_EXPERTS
    return {
        "sgu_ln_g": p["sgu_ln_g"][li].reshape(1, SGU_WIDTH),
        "sgu_ln_b": p["sgu_ln_b"][li].reshape(1, SGU_WIDTH),
        "w_spatial": p["w_spatial"][li],
        "b_spatial_full": jnp.repeat(p["b_spatial"][li].T, SGU_GROUP, axis=1),
        "conv_w": p["conv_w"][li],
        "conv_b": p["conv_b"][li].reshape(1, LRU_WIDTH),
        "w_rg_full": w_rg_full,
        "b_rg_full": p["b_rg"][li].reshape(1, 4 * LRU_WIDTH),
        "lru_log": p["lru_log"][li],
        "w_out_all": p["w_out"],
        "ln_g1": p["ln_g"][li, 0].reshape(1, D_MODEL),
        "ln_b1": p["ln_b"][li, 0].reshape(1, D_MODEL),
        "ln_g2": p["ln_g"][li, 1].reshape(1, D_MODEL),
        "ln_b2": p["ln_b"][li, 1].reshape(1, D_MODEL),
        "w_router_pad": jnp.pad(p["w_router"][li], ((0, 0), (0, pad))),
        "b_router_pad": jnp.pad(p["b_router"][li], (0, pad)).reshape(1, LANES),
    }


def kernel(x_prompt, x_sample, cache_k, cache_v, state_lru, c, c_ctx, w_mod, b_mod, w_in, lam, subln_g, sgu_ln_g, sgu_ln_b, w_spatial, b_spatial, conv_w, conv_b, w_rg, b_rg, lru_log, w_out, ln_g, ln_b, w_router, b_router, w_gu, b_gu, w_down, b_down):
    p = dict(sgu_ln_g=sgu_ln_g, sgu_ln_b=sgu_ln_b, w_spatial=w_spatial, b_spatial=b_spatial, conv_w=conv_w,
             conv_b=conv_b, w_rg=w_rg, b_rg=b_rg, lru_log=lru_log, w_out=w_out, ln_g=ln_g, ln_b=ln_b,
             w_router=w_router, b_router=b_router)
    cvec8 = jnp.concatenate([c_ctx[None, :], c, jnp.zeros((SUBLANES - 1 - DEC_BATCH, D_MODEL), F32)], axis=0)
    mod = _modulation(cvec8, w_mod, b_mod)
    x_pair = (x_prompt.reshape(N_CTX, D_MODEL), x_sample.reshape(N_DEC, D_MODEL))
    tables = _rope_tables()
    zero_state = jnp.zeros((BATCH, 1, 2, LRU_WIDTH), F32)
    prev_kv = None
    new_s = []
    for li in range(DEPTH):
        lp = _layer_params(p, li)
        lam_init = 0.8 - 0.6 * math.exp(-0.3 * li)
        proj = _input_projection(*x_pair, mod, w_in, li)
        att_ctx, kc, vc = _context_attention(proj, lam[li], subln_g[li], lam_init, prev_kv)
        prev_kv = (kc, vc)
        att_dec = _denoise_attention(proj, cache_k, cache_v, lam[li], subln_g[li], li, lam_init, tables)
        sgu_ctx, lru_ctx, h_ctx = _mixers(proj, zero_state, 0, lp, SEQ, BATCH, 0)
        sgu_dec, lru_dec, _ = _mixers(proj, state_lru, li, lp, DEC_SEQ, DEC_BATCH, N_CTX // DEC_SEQ)
        x1, h2, route_i, route_f, tile_cnt = _output_projection(
            x_pair, (att_ctx, att_dec), (sgu_ctx, sgu_dec), (lru_ctx, lru_dec), mod, lp, li)
        rt = _routing_tables(tile_cnt)
        xs = _dispatch(rt, h2, route_i)
        y = _moe_blocks(rt, xs, w_gu, b_gu, w_down, b_down, li)
        x_pair = _combine(rt, y, x1, route_i, route_f, mod, lp["ln_g2"], lp["ln_b2"], li)
        new_s.append(h_ctx)
    y_prompt = x_pair[0].reshape(BATCH, SEQ, D_MODEL)
    y_sample = x_pair[1].reshape(DEC_BATCH, DEC_SEQ, D_MODEL)
    return (y_prompt, y_sample, prev_kv[0], prev_kv[1], jnp.stack(new_s, axis=1))
```

```python
import functools
import math

import numpy as np
import jax
import jax.numpy as jnp
from jax import lax
from jax.experimental import pallas as pl
from jax.experimental.pallas import tpu as pltpu

F32 = jnp.float32
BF16 = jnp.bfloat16
I32 = jnp.int32

D_MODEL = 1024
BATCH = 32
SEQ = 256
DEPTH = 2
DEC_BATCH = 2
DEC_SEQ = 1024
PAST_LEN = 256
GRID_W = 64
HEAD_DIM = 64
ATT_WIDTH = D_MODEL // 2
SGU_WIDTH = D_MODEL // 4
LRU_WIDTH = D_MODEL // 4
N_ATT_HEADS = ATT_WIDTH // HEAD_DIM
ATT_HALF = HEAD_DIM // 2
ROPE_FREQS = ATT_HALF // 4
ROPE_THETA = 10000.0
CHUNK = 128
N_SGU_GROUPS = 4
SGU_GROUP = SGU_WIDTH // N_SGU_GROUPS
N_LRU_BLOCKS = 4
LRU_BLOCK = LRU_WIDTH // N_LRU_BLOCKS
CONV_W = 4
LRU_C = 8.0
IN_COLS = 3 * ATT_WIDTH + 2 * SGU_WIDTH + 2 * LRU_WIDTH
N_EXPERTS = 32
TOP_K = 4
D_EXPERT = D_MODEL
SWIGLU_LIMIT = 7.0
SWIGLU_ALPHA = 1.702
DN_ALPHA = (2 * DEPTH) ** 0.25
EPS = 1e-5

N_CTX = BATCH * SEQ
N_DEC = DEC_BATCH * DEC_SEQ
N_TOK = N_CTX + N_DEC

LANES = 128
SUBLANES = 8
ROW_SLABS = D_MODEL // LANES

TM = 512
CTX_TILES = N_CTX // TM
TILES_PER_DEC = DEC_SEQ // TM
N_TILES = N_TOK // TM
MOD_TN = 512
RT = 256
N_RT = N_TOK // RT
RT_PER_TM = TM // RT
RT_ROWS = RT * TOP_K
BM = 256
NB = N_TOK * TOP_K // BM + N_EXPERTS
HID_CHUNK = 1024
N_SLOTS = NB * BM
RUN_BITS = tuple(1 << b for b in range(RT.bit_length() - 1, -1, -1))
PAD_BITS = tuple(1 << b for b in range(BM.bit_length() - 2, -1, -1))
VMEM_LIMIT = 56 * 1024 * 1024


def _cparams(n_axes):
    return pltpu.CompilerParams(
        dimension_semantics=("arbitrary",) * n_axes,
        vmem_limit_bytes=VMEM_LIMIT)


def _mod_row(i):
    return jnp.where(i < CTX_TILES, 0, 1 + (i - CTX_TILES) // TILES_PER_DEC)


def _layer_norm_rows(z, g, b):
    mu = jnp.mean(z, axis=-1, keepdims=True)
    zc = z - mu
    var = jnp.mean(zc * zc, axis=-1, keepdims=True)
    return zc * lax.rsqrt(var + EPS) * g + b


def _pair_specs(tile, width, n_ctx_tiles):
    ctx = pl.BlockSpec((tile, width), lambda i, *_: (jnp.minimum(i, n_ctx_tiles - 1), 0))
    dec = pl.BlockSpec((tile, width), lambda i, *_: (jnp.maximum(i - n_ctx_tiles, 0), 0))
    return [ctx, dec]


def _pair_read(i, n_ctx_tiles, ctx_ref, dec_ref):
    return jnp.where(i < n_ctx_tiles, ctx_ref[...], dec_ref[...])


def _mod_kernel(cvec_ref, w_ref, b_ref, o_ref):
    cv = cvec_ref[...]
    s = cv * jax.nn.sigmoid(cv)
    s_t = s.T
    w = w_ref[0]
    rows = [jnp.sum(s_t[:, r:r + 1] * w, axis=0, keepdims=True) for r in range(1 + DEC_BATCH)]
    rows.append(jnp.zeros((SUBLANES - 1 - DEC_BATCH, MOD_TN), F32))
    o_ref[0] = jnp.concatenate(rows, axis=0) + b_ref[0]


def _modulation(cvec8, w_mod, b_mod):
    n_out = w_mod.shape[-1]
    return pl.pallas_call(
        _mod_kernel,
        grid=(DEPTH, n_out // MOD_TN),
        in_specs=[
            pl.BlockSpec((SUBLANES, D_MODEL), lambda l, j: (0, 0)),
            pl.BlockSpec((1, D_MODEL, MOD_TN), lambda l, j: (l, 0, j)),
            pl.BlockSpec((1, 1, MOD_TN), lambda l, j: (l, 0, j)),
        ],
        out_specs=pl.BlockSpec((1, SUBLANES, MOD_TN), lambda l, j: (l, 0, j)),
        out_shape=jax.ShapeDtypeStruct((DEPTH, SUBLANES, n_out), F32),
        compiler_params=_cparams(2),
        name="modulation",
    )(cvec8, w_mod, b_mod.reshape(DEPTH, 1, n_out))


def _inproj_kernel(xc_ref, xd_ref, mod_ref, w_ref, o_ref, wbf_ref):
    i = pl.program_id(0)

    @pl.when(i == 0)
    def _():
        wbf_ref[...] = w_ref[...].astype(BF16)

    m = mod_ref[pl.ds(_mod_row(i), 1), :]
    sh1 = m[:, 0:D_MODEL]
    sc1 = m[:, D_MODEL:2 * D_MODEL]
    x = _pair_read(i, CTX_TILES, xc_ref, xd_ref)
    h = (x * (1.0 + sc1) + sh1).astype(BF16)
    o_ref[...] = jnp.dot(h, wbf_ref[...], preferred_element_type=F32)


def _input_projection(x_ctx, x_dec, mod, w_in, li):
    return pl.pallas_call(
        _inproj_kernel,
        grid=(N_TILES,),
        in_specs=_pair_specs(TM, D_MODEL, CTX_TILES) + [
            pl.BlockSpec((None, SUBLANES, 6 * D_MODEL), lambda i: (li, 0, 0)),
            pl.BlockSpec((None, D_MODEL, IN_COLS), lambda i: (li, 0, 0), pipeline_mode=pl.Buffered(1)),
        ],
        out_specs=pl.BlockSpec((TM, IN_COLS), lambda i: (i, 0)),
        out_shape=jax.ShapeDtypeStruct((N_TOK, IN_COLS), F32),
        scratch_shapes=[pltpu.VMEM((D_MODEL, IN_COLS), BF16)],
        compiler_params=_cparams(1),
        name="input_projection",
    )(x_ctx, x_dec, mod, w_in)


def _lambda(lam_ref, lam_init):
    lm = lam_ref[...]
    a = jnp.sum(lm[0:1] * lm[1:2], axis=-1, keepdims=True)
    b = jnp.sum(lm[2:3] * lm[3:4], axis=-1, keepdims=True)
    return jnp.exp(a) - jnp.exp(b) + lam_init


_NT = (((1,), (1,)), ((), ()))
_TN = (((0,), (0,)), ((), ()))
_ATT_SCALE = ATT_HALF ** -0.5
_Q_SCALE = _ATT_SCALE * math.log2(math.e)
KEY_CHUNK = 128


def _attention_heads(q_parts, k_segs, values, s_scr, e_scr, lam, g_col, lam_init):
    n_heads = len(q_parts)
    for h in range(n_heads):
        for c in range(2):
            row = 0
            for k in k_segs[h][c]:
                s_scr[2 * h + c, row:row + k.shape[0], :] = lax.dot_general(
                    k, q_parts[h][c], _NT, preferred_element_type=F32)
                row += k.shape[0]
    n_keys, n_q = s_scr.shape[1:]
    chunks = [slice(r, r + KEY_CHUNK) for r in range(0, n_keys, KEY_CHUNK)]
    for n in range(2 * n_heads):
        tops = [jnp.max(s_scr[n, rows, :].reshape(KEY_CHUNK // SUBLANES, SUBLANES, n_q), axis=0)
                for rows in chunks]
        top = jnp.max(functools.reduce(jnp.maximum, tops), axis=0, keepdims=True)
        for rows in chunks:
            e_scr[n, rows, :] = jnp.exp2(s_scr[n, rows, :] - top).astype(BF16)
    outs = []
    for h in range(n_heads):
        v_aug = jnp.concatenate([values[h], jnp.ones_like(values[h])], axis=-1)
        normed = []
        for c in range(2):
            acc = lax.dot_general(v_aug, e_scr[2 * h + c], _TN, preferred_element_type=F32)
            normed.append(acc[:HEAD_DIM] * (1.0 / acc[HEAD_DIM:HEAD_DIM + 1]))
        o_t = normed[0] - lam * normed[1]
        ms = jnp.mean(o_t * o_t, axis=0, keepdims=True)
        outs.append(o_t * lax.rsqrt(ms + EPS) * g_col * (1.0 - lam_init))
    return outs


def _head_cols(h):
    lo = h * HEAD_DIM
    return [slice(lo + c * ATT_HALF, lo + (c + 1) * ATT_HALF) for c in range(2)]


def _ctx_attn_kernel(lam_ref, g_ref, q_ref, k_ref, v_ref, *rest, lam_init, stacked):
    *rest, s_scr, e_scr = rest
    if stacked:
        pk_ref, pv_ref, att_ref, ck_ref, cv_ref = rest
        ck_ref[0, 0] = pk_ref[0]
        cv_ref[0, 0] = pv_ref[0]
        put_k = lambda h, val: ck_ref.__setitem__((0, 1, h), val)
        put_v = lambda h, val: cv_ref.__setitem__((0, 1, h), val)
    else:
        att_ref, ck_ref, cv_ref = rest
        put_k = lambda h, val: ck_ref.__setitem__((0, h), val)
        put_v = lambda h, val: cv_ref.__setitem__((0, h), val)
    lam = _lambda(lam_ref, lam_init)
    k = k_ref[...]
    v = v_ref[...]
    qb = (q_ref[...] * _Q_SCALE).astype(BF16)
    kb = k.astype(BF16)
    vb = v.astype(BF16)
    heads = range(N_ATT_HEADS)
    outs = _attention_heads([[qb[:, cs] for cs in _head_cols(h)] for h in heads],
                            [[[kb[:, cs]] for cs in _head_cols(h)] for h in heads],
                            [vb[:, h * HEAD_DIM:(h + 1) * HEAD_DIM] for h in heads],
                            s_scr, e_scr, lam, g_ref[...], lam_init)
    for h in heads:
        put_k(h, k[:, h * HEAD_DIM:(h + 1) * HEAD_DIM])
        put_v(h, v[:, h * HEAD_DIM:(h + 1) * HEAD_DIM])
    att_ref[...] = jnp.concatenate(outs, axis=0).T


def _context_attention(proj, lam, subln_g, lam_init, prev_kv=None):
    stacked = prev_kv is not None
    per_layer = pl.BlockSpec((1, N_ATT_HEADS, SEQ, HEAD_DIM), lambda b: (b, 0, 0, 0))
    if stacked:
        kv_shape = jax.ShapeDtypeStruct((BATCH, DEPTH, N_ATT_HEADS, SEQ, HEAD_DIM), F32)
        kv_spec = pl.BlockSpec((1, DEPTH, N_ATT_HEADS, SEQ, HEAD_DIM), lambda b: (b, 0, 0, 0, 0))
    else:
        kv_shape = jax.ShapeDtypeStruct((BATCH, N_ATT_HEADS, SEQ, HEAD_DIM), F32)
        kv_spec = per_layer
    return pl.pallas_call(
        functools.partial(_ctx_attn_kernel, lam_init=lam_init, stacked=stacked),
        grid=(BATCH,),
        in_specs=[
            pl.BlockSpec((4, ATT_HALF), lambda b: (0, 0)),
            pl.BlockSpec((HEAD_DIM, 1), lambda b: (0, 0)),
            pl.BlockSpec((SEQ, ATT_WIDTH), lambda b: (b, 0)),
            pl.BlockSpec((SEQ, ATT_WIDTH), lambda b: (b, 1)),
            pl.BlockSpec((SEQ, ATT_WIDTH), lambda b: (b, 2)),
        ] + ([per_layer, per_layer] if stacked else []),
        out_specs=[pl.BlockSpec((SEQ, ATT_WIDTH), lambda b: (b, 0)), kv_spec, kv_spec],
        out_shape=[jax.ShapeDtypeStruct((N_CTX, ATT_WIDTH), F32), kv_shape, kv_shape],
        scratch_shapes=[pltpu.VMEM((2 * N_ATT_HEADS, SEQ, SEQ), F32),
                        pltpu.VMEM((2 * N_ATT_HEADS, SEQ, SEQ), BF16)],
        compiler_params=_cparams(1),
        name="context_attention",
    )(lam, subln_g.reshape(HEAD_DIM, 1), proj, proj, proj, *(prev_kv if stacked else ()))


def _rope_tables():
    t = np.arange(DEC_SEQ)
    pos = np.stack([t // GRID_W, t % GRID_W], axis=1).astype(np.float32)
    inv = (np.float32(ROPE_THETA) ** (-np.arange(ROPE_FREQS, dtype=np.float32) / np.float32(ROPE_FREQS)))
    j = np.arange(HEAD_DIM)
    d = j % ATT_HALF
    axis = d // (2 * ROPE_FREQS)
    u = d % (2 * ROPE_FREQS)
    ang = pos[:, axis] * inv[u % ROPE_FREQS][None, :].astype(np.float32)
    cos = np.cos(ang).astype(np.float32)
    sin = np.sin(ang).astype(np.float32)
    first = (u < ROPE_FREQS)[None, :]
    s_next = np.where(first, -sin, 0.0).astype(np.float32)
    s_prev = np.where(first, 0.0, sin).astype(np.float32)
    tile = lambda a: jnp.asarray(np.tile(a, (1, N_ATT_HEADS)))
    return tile(cos), tile(s_next), tile(s_prev)


def _rotate(x, cos, s_next, s_prev):
    width = x.shape[-1]
    return (x * cos + pltpu.roll(x, width - ROPE_FREQS, axis=1) * s_next
            + pltpu.roll(x, ROPE_FREQS, axis=1) * s_prev)


QB = 256
Q_STEPS = DEC_SEQ // QB
DEC_HEAD_GROUP = 2


def _dec_attn_kernel(lam_ref, g_ref, q_ref, k_ref, v_ref, ck_ref, cv_ref,
                     cq_ref, snq_ref, spq_ref, ck_tab, snk_tab, spk_tab,
                     att_ref, krot_ref, s_scr, e_scr, *, lam_init):
    j = pl.program_id(1)

    @pl.when(j == 0)
    def _():
        krot_ref[...] = _rotate(k_ref[...], ck_tab[...], snk_tab[...], spk_tab[...]).astype(BF16)

    lam = _lambda(lam_ref, lam_init)
    qb = (_rotate(q_ref[...], cq_ref[...], snq_ref[...], spq_ref[...]) * _Q_SCALE).astype(BF16)
    kb = krot_ref[...]
    vb = v_ref[...].astype(BF16)
    outs = []
    for h0 in range(0, N_ATT_HEADS, DEC_HEAD_GROUP):
        heads = range(h0, h0 + DEC_HEAD_GROUP)
        past_k = [ck_ref[0, 0, h].astype(BF16) for h in heads]
        k_segs = [[[pk[:, c * ATT_HALF:(c + 1) * ATT_HALF], kb[:, _head_cols(h)[c]]] for c in range(2)]
                  for h, pk in zip(heads, past_k)]
        values = [jnp.concatenate([cv_ref[0, 0, h].astype(BF16), vb[:, h * HEAD_DIM:(h + 1) * HEAD_DIM]], axis=0)
                  for h in heads]
        outs += _attention_heads([[qb[:, cs] for cs in _head_cols(h)] for h in heads], k_segs, values,
                                 s_scr, e_scr, lam, g_ref[...], lam_init)
    att_ref[...] = jnp.concatenate(outs, axis=0).T


def _denoise_attention(proj, cache_k, cache_v, lam, subln_g, li, lam_init, tables):
    cos, s_next, s_prev = tables
    row0 = N_CTX // QB
    seq0 = N_CTX // DEC_SEQ
    q_tab = pl.BlockSpec((QB, ATT_WIDTH), lambda b, j: (j, 0))
    k_tab = pl.BlockSpec((DEC_SEQ, ATT_WIDTH), lambda b, j: (0, 0))
    cache_spec = pl.BlockSpec((1, 1, N_ATT_HEADS, PAST_LEN, HEAD_DIM), lambda b, j: (b, li, 0, 0, 0))
    return pl.pallas_call(
        functools.partial(_dec_attn_kernel, lam_init=lam_init),
        grid=(DEC_BATCH, Q_STEPS),
        in_specs=[
            pl.BlockSpec((4, ATT_HALF), lambda b, j: (0, 0)),
            pl.BlockSpec((HEAD_DIM, 1), lambda b, j: (0, 0)),
            pl.BlockSpec((QB, ATT_WIDTH), lambda b, j: (row0 + b * Q_STEPS + j, 0)),
            pl.BlockSpec((DEC_SEQ, ATT_WIDTH), lambda b, j: (seq0 + b, 1)),
            pl.BlockSpec((DEC_SEQ, ATT_WIDTH), lambda b, j: (seq0 + b, 2)),
            cache_spec, cache_spec,
            q_tab, q_tab, q_tab, k_tab, k_tab, k_tab,
        ],
        out_specs=pl.BlockSpec((QB, ATT_WIDTH), lambda b, j: (b * Q_STEPS + j, 0)),
        out_shape=jax.ShapeDtypeStruct((N_DEC, ATT_WIDTH), F32),
        scratch_shapes=[pltpu.VMEM((DEC_SEQ, ATT_WIDTH), BF16),
                        pltpu.VMEM((2 * DEC_HEAD_GROUP, PAST_LEN + DEC_SEQ, QB), F32),
                        pltpu.VMEM((2 * DEC_HEAD_GROUP, PAST_LEN + DEC_SEQ, QB), BF16)],
        compiler_params=_cparams(2),
        name="denoise_attention",
    )(lam, subln_g.reshape(HEAD_DIM, 1), proj, proj, proj, cache_k, cache_v,
      cos, s_next, s_prev, cos, s_next, s_prev)


def _softplus(z):
    return jnp.maximum(z, 0.0) + jnp.log1p(jnp.exp(-jnp.abs(z)))


def _mixer_kernel(su_ref, sv_ref, rx_ref, rg_ref, h0_ref, lng_ref, lnb_ref, ws_ref, bs_ref,
                  cw_ref, cb_ref, wrg_ref, brg_ref, lrulog_ref,
                  sgu_ref, lru_ref, hlast_ref, a_scr, b_scr, h_scr, *, seq_len):
    vn = _layer_norm_rows(sv_ref[...], lng_ref[...], lnb_ref[...])
    lane_group = lax.broadcasted_iota(I32, (CHUNK, SGU_WIDTH), 1) // SGU_GROUP
    for n in range(seq_len // CHUNK):
        rows = slice(n * CHUNK, (n + 1) * CHUNK)
        vc = vn[rows].astype(BF16)
        s = jnp.zeros((CHUNK, SGU_WIDTH), F32)
        for g in range(N_SGU_GROUPS):
            sg = jnp.dot(ws_ref[g].astype(BF16), vc, preferred_element_type=F32)
            s = jnp.where(lane_group == g, sg, s)
        sgu_ref[rows, :] = su_ref[rows, :] * (s + bs_ref[...])

    x = rx_ref[...]
    row = lax.broadcasted_iota(I32, (seq_len, LRU_WIDTH), 0)

    def shifted(val, d, fill):
        rolled = pltpu.roll(val, d % seq_len, axis=0)
        inside = (row >= d) if d > 0 else (row < seq_len + d)
        return jnp.where(inside, rolled, fill)

    left = CONV_W // 2
    xc = cb_ref[...] + x * cw_ref[left:left + 1, :]
    for tap in range(CONV_W):
        if tap != left:
            xc = xc + shifted(x, left - tap, 0.0) * cw_ref[tap:tap + 1, :]
    gates = jax.nn.sigmoid(jnp.dot(xc.astype(BF16), wrg_ref[...].astype(BF16),
                                   preferred_element_type=F32) + brg_ref[...])
    in_chunk = row % SUBLANES
    n_chunks = seq_len // SUBLANES
    h0 = h0_ref[0, 0]
    lasts = []
    for direction in range(2):
        reverse = direction == 1
        base = direction * 2 * LRU_WIDTH
        r = gates[:, base:base + LRU_WIDTH]
        gi = gates[:, base + LRU_WIDTH:base + 2 * LRU_WIDTH]
        log_a = -LRU_C * r * _softplus(-lrulog_ref[direction:direction + 1, :])
        a = jnp.exp(log_a)
        b = jnp.sqrt(-jnp.tanh(log_a) * (a * a + 1.0)) * gi * xc
        for d in (1, 2, 4):
            if reverse:
                inside = in_chunk < SUBLANES - d
                a_n = jnp.where(inside, pltpu.roll(a, seq_len - d, axis=0), 1.0)
                b_n = jnp.where(inside, pltpu.roll(b, seq_len - d, axis=0), 0.0)
            else:
                inside = in_chunk >= d
                a_n = jnp.where(inside, pltpu.roll(a, d, axis=0), 1.0)
                b_n = jnp.where(inside, pltpu.roll(b, d, axis=0), 0.0)
            b = a * b_n + b
            a = a * a_n
        a_scr[...] = a
        b_scr[...] = b

        def chunk_step(c, carry, reverse=reverse):
            cc = n_chunks - 1 - c if reverse else c
            off = pl.multiple_of(cc * SUBLANES, SUBLANES)
            hc = a_scr[pl.ds(off, SUBLANES), :] * carry + b_scr[pl.ds(off, SUBLANES), :]
            if reverse:
                h_scr[pl.ds(off, SUBLANES), :] = h_scr[pl.ds(off, SUBLANES), :] + hc
                return hc[0:1, :]
            h_scr[pl.ds(off, SUBLANES), :] = hc
            return hc[SUBLANES - 1:SUBLANES, :]

        lasts.append(lax.fori_loop(0, n_chunks, chunk_step, h0[direction:direction + 1, :]))
    lru_ref[...] = h_scr[...] * jax.nn.gelu(rg_ref[...])
    hlast_ref[0] = jnp.concatenate(lasts, axis=0)


def _mixers(proj, h0, h0_layer, lp, seq_len, n_seq, row_block0):
    col0 = 3 * ATT_WIDTH // SGU_WIDTH
    col = lambda c: pl.BlockSpec((seq_len, SGU_WIDTH), lambda b: (row_block0 + b, col0 + c))
    full = lambda shape: pl.BlockSpec(shape, lambda b: (0,) * len(shape))
    out_rows = pl.BlockSpec((seq_len, SGU_WIDTH), lambda b: (b, 0))
    return pl.pallas_call(
        functools.partial(_mixer_kernel, seq_len=seq_len),
        grid=(n_seq,),
        in_specs=[
            col(0), col(1), col(2), col(3),
            pl.BlockSpec((1, 1, 2, LRU_WIDTH), lambda b: (b, h0_layer, 0, 0)),
            full((1, SGU_WIDTH)), full((1, SGU_WIDTH)),
            full((N_SGU_GROUPS, CHUNK, CHUNK)), full((CHUNK, SGU_WIDTH)),
            full((CONV_W, LRU_WIDTH)), full((1, LRU_WIDTH)),
            full((LRU_WIDTH, 4 * LRU_WIDTH)), full((1, 4 * LRU_WIDTH)),
            full((2, LRU_WIDTH)),
        ],
        out_specs=[out_rows, out_rows, pl.BlockSpec((1, 2, LRU_WIDTH), lambda b: (b, 0, 0))],
        out_shape=[
            jax.ShapeDtypeStruct((n_seq * seq_len, SGU_WIDTH), F32),
            jax.ShapeDtypeStruct((n_seq * seq_len, LRU_WIDTH), F32),
            jax.ShapeDtypeStruct((n_seq, 2, LRU_WIDTH), F32),
        ],
        scratch_shapes=[pltpu.VMEM((seq_len, LRU_WIDTH), F32)] * 3,
        compiler_params=_cparams(1),
        name="mixers_%d" % seq_len,
    )(proj, proj, proj, proj, h0, lp["sgu_ln_g"], lp["sgu_ln_b"], lp["w_spatial"], lp["b_spatial_full"],
      lp["conv_w"], lp["conv_b"], lp["w_rg_full"], lp["b_rg_full"], lp["lru_log"])


def _split_bf16(v):
    hi = v.astype(BF16)
    lo = (v - hi.astype(F32)).astype(BF16)
    return hi, lo


def _outproj_kernel(xc_ref, xd_ref, ac_ref, ad_ref, sc_ref, sd_ref, lc_ref, ld_ref,
                    mod_ref, wout_ref, lng_ref, lnb_ref, wr_ref, br_ref,
                    x1_ref, h2_ref, ri_ref, rf_ref, cnt_ref, wbf_ref):
    i = pl.program_id(0)

    @pl.when(i == 0)
    def _():
        wbf_ref[...] = wout_ref[...].astype(BF16)

    m = mod_ref[pl.ds(_mod_row(i), 1), :]
    g1 = m[:, 2 * D_MODEL:3 * D_MODEL]
    sh2 = m[:, 3 * D_MODEL:4 * D_MODEL]
    sc2 = m[:, 4 * D_MODEL:5 * D_MODEL]
    a0, a1 = ATT_WIDTH, ATT_WIDTH + SGU_WIDTH
    x = _pair_read(i, CTX_TILES, xc_ref, xd_ref)
    att = _pair_read(i, CTX_TILES, ac_ref, ad_ref).astype(BF16)
    sgu = _pair_read(i, CTX_TILES, sc_ref, sd_ref).astype(BF16)
    lru = _pair_read(i, CTX_TILES, lc_ref, ld_ref).astype(BF16)
    mix = (jnp.dot(att, wbf_ref[0:a0, :], preferred_element_type=F32)
           + jnp.dot(sgu, wbf_ref[a0:a1, :], preferred_element_type=F32)
           + jnp.dot(lru, wbf_ref[a1:, :], preferred_element_type=F32))
    x1 = _layer_norm_rows(DN_ALPHA * x + g1 * mix, lng_ref[...], lnb_ref[...])
    x1_ref[...] = x1
    h2 = x1 * (1.0 + sc2) + sh2
    h2_ref[...] = h2.astype(BF16)

    h_hi, h_lo = _split_bf16(h2)
    w_hi, w_lo = _split_bf16(wr_ref[...])
    logits = (jnp.dot(h_hi, w_hi, preferred_element_type=F32)
              + jnp.dot(h_lo, w_hi, preferred_element_type=F32)
              + jnp.dot(h_hi, w_lo, preferred_element_type=F32)) + br_ref[...]
    lane = lax.broadcasted_iota(I32, (TM, LANES), 1)
    lane_f = lane.astype(F32)
    neg_inf = jnp.float32(-jnp.inf)
    work = jnp.where(lane < N_EXPERTS, logits, neg_inf)
    vals, idxs = [], []
    for _ in range(TOP_K):
        top = jnp.max(work, axis=-1, keepdims=True)
        idx = jnp.min(jnp.where(work == top, lane_f, float(LANES)), axis=-1, keepdims=True)
        vals.append(top)
        idxs.append(idx)
        work = jnp.where(lane_f == idx, neg_inf, work)
    exps = [jnp.exp(v - vals[0]) for v in vals]
    denom = exps[0] + exps[1] + exps[2] + exps[3]
    onehot = jnp.zeros((TM, LANES), F32)
    for idx in idxs:
        onehot = onehot + (lane_f == idx).astype(F32)
    r_i = lax.broadcasted_iota(I32, (RT, RT), 0)
    c_i = lax.broadcasted_iota(I32, (RT, RT), 1)
    tri = (r_i > c_i).astype(F32).astype(BF16)
    before = jnp.concatenate(
        [jnp.dot(tri, onehot[t * RT:(t + 1) * RT].astype(BF16), preferred_element_type=F32)
         for t in range(RT_PER_TM)], axis=0)
    ri = jnp.zeros((TM, LANES), F32)
    rf = jnp.zeros((TM, LANES), F32)
    for k in range(TOP_K):
        rank = jnp.sum(jnp.where(lane_f == idxs[k], before, 0.0), axis=-1, keepdims=True)
        ri = jnp.where(lane == k, idxs[k], ri)
        ri = jnp.where(lane == TOP_K + k, rank, ri)
        rf = jnp.where(lane == k, exps[k] / denom, rf)
    ri_ref[...] = ri.astype(I32)
    rf_ref[...] = rf
    for t in range(RT_PER_TM):
        total = jnp.sum(onehot[t * RT:(t + 1) * RT], axis=0, keepdims=True)
        cnt_ref[t] = jnp.broadcast_to(total, (SUBLANES, LANES)).astype(I32)


def _output_projection(x_pair, att_pair, sgu_pair, lru_pair, mod, lp, li):
    rows = lambda w: pl.BlockSpec((TM, w), lambda i: (i, 0))
    full = lambda shape: pl.BlockSpec(shape, lambda i: (0,) * len(shape))
    return pl.pallas_call(
        _outproj_kernel,
        grid=(N_TILES,),
        in_specs=(_pair_specs(TM, D_MODEL, CTX_TILES) + _pair_specs(TM, ATT_WIDTH, CTX_TILES)
                  + _pair_specs(TM, SGU_WIDTH, CTX_TILES) + _pair_specs(TM, LRU_WIDTH, CTX_TILES) + [
            pl.BlockSpec((None, SUBLANES, 6 * D_MODEL), lambda i: (li, 0, 0)),
            pl.BlockSpec((None, D_MODEL, D_MODEL), lambda i: (li, 0, 0), pipeline_mode=pl.Buffered(1)),
            full((1, D_MODEL)), full((1, D_MODEL)),
            full((D_MODEL, LANES)), full((1, LANES)),
        ]),
        out_specs=[
            rows(D_MODEL), rows(D_MODEL), rows(LANES), rows(LANES),
            pl.BlockSpec((RT_PER_TM, SUBLANES, LANES), lambda i: (i, 0, 0)),
        ],
        out_shape=[
            jax.ShapeDtypeStruct((N_TOK, D_MODEL), F32),
            jax.ShapeDtypeStruct((N_TOK, D_MODEL), BF16),
            jax.ShapeDtypeStruct((N_TOK, LANES), I32),
            jax.ShapeDtypeStruct((N_TOK, LANES), F32),
            jax.ShapeDtypeStruct((N_RT, SUBLANES, LANES), I32),
        ],
        scratch_shapes=[pltpu.VMEM((D_MODEL, D_MODEL), BF16)],
        compiler_params=_cparams(1),
        name="output_projection",
    )(*x_pair, *att_pair, *sgu_pair, *lru_pair, mod, lp["w_out_all"], lp["ln_g1"], lp["ln_b1"],
      lp["w_router_pad"], lp["b_router_pad"])


def _row_ds(row, n_rows):
    return pl.ds(pl.multiple_of(row * ROW_SLABS, ROW_SLABS), n_rows * ROW_SLABS)


def _for_each_run_piece(length, bits, fn):
    for bit in bits:
        done = length & (-2 * bit)
        @pl.when((length & bit) != 0)
        def _(done=done, bit=bit):
            fn(done, bit)


def _tile_run_copies(tile, slot, start_ref, len_ref, off_ref, sorted_hbm, buf, sem, to_sorted):
    for e in range(N_EXPERTS):
        t = tile * N_EXPERTS + e
        start, length, off = start_ref[t], len_ref[t], off_ref[t]

        def piece(done, bit, start=start, off=off):
            packed = buf.at[_row_ds(slot * RT_ROWS + off + done, bit), :]
            srt = sorted_hbm.at[_row_ds(start + done, bit), :]
            if to_sorted:
                pltpu.make_async_copy(packed, srt, sem).start()
            else:
                pltpu.make_async_copy(srt, packed, sem).start()

        _for_each_run_piece(length, RUN_BITS, piece)


def _tile_runs_wait(slot, buf, sem):
    region = buf.at[_row_ds(slot * RT_ROWS, RT_ROWS), :]
    pltpu.make_async_copy(region, region, sem).wait()


def _packed_positions(ri, off_row):
    lane = lax.broadcasted_iota(I32, (RT, LANES), 1)
    pos = []
    for k in range(TOP_K):
        e_k = ri[:, k:k + 1]
        base = jnp.sum(jnp.where(lane == e_k, off_row, 0.0), axis=-1, keepdims=True)
        pos.append(base + ri[:, TOP_K + k:TOP_K + k + 1].astype(F32))
    return pos


def _slab_columns(buf, row0, n_rows):
    return jnp.concatenate(
        [buf[pl.ds(row0 * ROW_SLABS + s, n_rows, stride=ROW_SLABS), :] for s in range(ROW_SLABS)], axis=-1)


def _dispatch_kernel(start_ref, len_ref, off_ref, pstart_ref, plen_ref, nv_ref,
                     h2_ref, ri_ref, offrow_ref, xs_hbm, xc, zbuf, sem, zsem):
    i = pl.program_id(0)
    n_steps = pl.num_programs(0)
    slot = i % 2

    @pl.when(i == 0)
    def _():
        zbuf[...] = jnp.zeros_like(zbuf)

        def zero_fill(wait):
            def go(cp):
                cp.wait() if wait else cp.start()

            def per_expert(e, carry):
                def piece(done, bit):
                    go(pltpu.make_async_copy(zbuf.at[pl.ds(0, bit * ROW_SLABS), :],
                                             xs_hbm.at[_row_ds(pstart_ref[e] + done, bit), :], zsem))
                _for_each_run_piece(plen_ref[e], PAD_BITS, piece)
                return carry
            lax.fori_loop(0, N_EXPERTS, per_expert, 0)

            def per_block(b, carry):
                go(pltpu.make_async_copy(zbuf, xs_hbm.at[_row_ds(b * BM, BM), :], zsem))
                return carry
            lax.fori_loop(nv_ref[0], NB, per_block, 0)

        zero_fill(False)
        zero_fill(True)

    @pl.when(i >= 2)
    def _():
        _tile_runs_wait(slot, xc, sem.at[slot])

    pos = _packed_positions(ri_ref[...], offrow_ref[0, 0:1, :])
    col = lax.broadcasted_iota(I32, (RT, RT_ROWS), 1).astype(F32)
    sel = jnp.zeros((RT, RT_ROWS), F32)
    for p in pos:
        sel = sel + (col == p).astype(F32)
    packed = lax.dot_general(sel.astype(BF16), h2_ref[...], _TN, preferred_element_type=F32)
    for s in range(ROW_SLABS):
        xc[pl.ds(slot * RT_ROWS * ROW_SLABS + s, RT_ROWS, stride=ROW_SLABS), :] = (
            packed[:, s * LANES:(s + 1) * LANES])
    _tile_run_copies(i, slot, start_ref, len_ref, off_ref, xs_hbm, xc, sem.at[slot], to_sorted=True)

    @pl.when(i == n_steps - 1)
    def _():
        _tile_runs_wait(1 - slot, xc, sem.at[1 - slot])
        _tile_runs_wait(slot, xc, sem.at[slot])


def _dispatch(tables, h2, route_i):
    grid_spec = pltpu.PrefetchScalarGridSpec(
        num_scalar_prefetch=6,
        grid=(N_RT,),
        in_specs=[
            pl.BlockSpec((RT, D_MODEL), lambda i, *_: (i, 0)),
            pl.BlockSpec((RT, LANES), lambda i, *_: (i, 0)),
            pl.BlockSpec((1, SUBLANES, LANES), lambda i, *_: (i, 0, 0)),
        ],
        out_specs=pl.BlockSpec(memory_space=pl.ANY),
        scratch_shapes=[
            pltpu.VMEM((2 * RT_ROWS * ROW_SLABS, LANES), F32),
            pltpu.VMEM((BM * ROW_SLABS, LANES), F32),
            pltpu.SemaphoreType.DMA((2,)),
            pltpu.SemaphoreType.DMA(()),
        ],
    )
    return pl.pallas_call(
        _dispatch_kernel,
        grid_spec=grid_spec,
        out_shape=jax.ShapeDtypeStruct((N_SLOTS * ROW_SLABS, LANES), F32),
        compiler_params=_cparams(1),
        name="moe_dispatch",
    )(tables["start"], tables["len"], tables["off"], tables["pad_start"], tables["pad_len"], tables["n_valid"],
      h2, route_i, tables["off_rows"])


def _moe_kernel(be_ref, nxt_ref, nv_ref, x_ref, wgu_hbm, bgu_ref, wdn_hbm, bdn_ref, y_ref,
                gu_stage, dn_stage, wgu_bf, wdn_bf, sem, *, li):
    i = pl.program_id(0)
    n_valid = nv_ref[0]

    def weight_copies(e):
        return (pltpu.make_async_copy(wgu_hbm.at[li, e], gu_stage, sem.at[0]),
                pltpu.make_async_copy(wdn_hbm.at[li, e], dn_stage, sem.at[1]))

    def fetch(e):
        for cp in weight_copies(e):
            cp.start()

    @pl.when(i < n_valid)
    def _():
        e = be_ref[i]
        new_expert = jnp.logical_or(i == 0, e != be_ref[jnp.maximum(i - 1, 0)])

        @pl.when(i == 0)
        def _():
            fetch(e)

        @pl.when(new_expert)
        def _():
            for cp in weight_copies(e):
                cp.wait()
            wgu_bf[...] = gu_stage[...].astype(BF16)
            wdn_bf[...] = dn_stage[...].astype(BF16)

            @pl.when(nxt_ref[i] >= 0)
            def _():
                fetch(nxt_ref[i])

        x = _slab_columns(x_ref, 0, BM).astype(BF16)
        y = None
        for j in range(D_EXPERT // HID_CHUNK):
            gc = slice(j * HID_CHUNK, (j + 1) * HID_CHUNK)
            uc = slice(D_EXPERT + j * HID_CHUNK, D_EXPERT + (j + 1) * HID_CHUNK)
            g = jnp.dot(x, wgu_bf[:, gc], preferred_element_type=F32) + bgu_ref[:, gc]
            u = jnp.dot(x, wgu_bf[:, uc], preferred_element_type=F32) + bgu_ref[:, uc]
            g = jnp.minimum(g, SWIGLU_LIMIT)
            u = jnp.clip(u, -SWIGLU_LIMIT, SWIGLU_LIMIT)
            act = ((u + 1.0) * (0.5 * g * (1.0 + jnp.tanh((0.5 * SWIGLU_ALPHA) * g)))).astype(BF16)
            part = jnp.dot(act, wdn_bf[gc, :], preferred_element_type=F32)
            y = part if y is None else y + part
        y = y + bdn_ref[...]
        for s in range(ROW_SLABS):
            y_ref[pl.ds(s, BM, stride=ROW_SLABS), :] = y[:, s * LANES:(s + 1) * LANES]

    @pl.when(i >= n_valid)
    def _():
        y_ref[...] = jnp.zeros_like(y_ref)


def _moe_blocks(tables, xs, w_gu, b_gu, w_down, b_down, li):
    bmap = lambda i, be, nxt, nv: (li, be[i], 0, 0)
    grid_spec = pltpu.PrefetchScalarGridSpec(
        num_scalar_prefetch=3,
        grid=(NB,),
        in_specs=[
            pl.BlockSpec((BM * ROW_SLABS, LANES), lambda i, be, nxt, nv: (jnp.minimum(i, nv[0] - 1), 0)),
            pl.BlockSpec(memory_space=pl.ANY),
            pl.BlockSpec((None, None, 1, 2 * D_EXPERT), bmap),
            pl.BlockSpec(memory_space=pl.ANY),
            pl.BlockSpec((None, None, 1, D_MODEL), bmap),
        ],
        out_specs=pl.BlockSpec((BM * ROW_SLABS, LANES), lambda i, be, nxt, nv: (i, 0)),
        scratch_shapes=[
            pltpu.VMEM((D_MODEL, 2 * D_EXPERT), F32),
            pltpu.VMEM((D_EXPERT, D_MODEL), F32),
            pltpu.VMEM((D_MODEL, 2 * D_EXPERT), BF16),
            pltpu.VMEM((D_EXPERT, D_MODEL), BF16),
            pltpu.SemaphoreType.DMA((2,)),
        ],
    )
    return pl.pallas_call(
        functools.partial(_moe_kernel, li=li),
        grid_spec=grid_spec,
        out_shape=jax.ShapeDtypeStruct((N_SLOTS * ROW_SLABS, LANES), F32),
        compiler_params=_cparams(1),
        name="moe_experts",
    )(tables["block_expert"], tables["next_expert"], tables["n_valid"], xs, w_gu,
      b_gu.reshape(DEPTH, N_EXPERTS, 1, 2 * D_EXPERT), w_down, b_down.reshape(DEPTH, N_EXPERTS, 1, D_MODEL))


def _combine_kernel(start_ref, len_ref, off_ref, y_hbm, x1_ref, ri_ref, rf_ref, offrow_ref, mod_ref,
                    lng_ref, lnb_ref, oc_ref, od_ref, ybuf, sem):
    i = pl.program_id(0)
    n_steps = pl.num_programs(0)
    slot = i % 2
    fetch = functools.partial(_tile_run_copies, start_ref=start_ref, len_ref=len_ref, off_ref=off_ref,
                              sorted_hbm=y_hbm, buf=ybuf, to_sorted=False)

    @pl.when(i == 0)
    def _():
        fetch(0, 0, sem=sem.at[0])

    @pl.when(i + 1 < n_steps)
    def _():
        fetch(i + 1, 1 - slot, sem=sem.at[1 - slot])

    _tile_runs_wait(slot, ybuf, sem.at[slot])
    pos = _packed_positions(ri_ref[...], offrow_ref[0, 0:1, :])
    gates = rf_ref[...]
    col = lax.broadcasted_iota(I32, (RT, RT_ROWS), 1).astype(F32)
    mix = jnp.zeros((RT, RT_ROWS), F32)
    for k in range(TOP_K):
        mix = mix + jnp.where(col == pos[k], gates[:, k:k + 1], 0.0)
    rows = _slab_columns(ybuf, slot * RT_ROWS, RT_ROWS).astype(BF16)
    ffn = jnp.dot(mix.astype(BF16), rows, preferred_element_type=F32)
    m = mod_ref[pl.ds(_mod_row(i // RT_PER_TM), 1), :]
    g2 = m[:, 5 * D_MODEL:6 * D_MODEL]
    out = _layer_norm_rows(DN_ALPHA * x1_ref[...] + g2 * ffn, lng_ref[...], lnb_ref[...])

    @pl.when(i < N_CTX // RT)
    def _():
        oc_ref[...] = out

    @pl.when(i >= N_CTX // RT)
    def _():
        od_ref[...] = out


def _combine(tables, y, x1, route_i, route_f, mod, ln_g, ln_b, li):
    n_ctx_tiles = N_CTX // RT
    grid_spec = pltpu.PrefetchScalarGridSpec(
        num_scalar_prefetch=3,
        grid=(N_RT,),
        in_specs=[
            pl.BlockSpec(memory_space=pl.ANY),
            pl.BlockSpec((RT, D_MODEL), lambda i, *_: (i, 0)),
            pl.BlockSpec((RT, LANES), lambda i, *_: (i, 0)),
            pl.BlockSpec((RT, LANES), lambda i, *_: (i, 0)),
            pl.BlockSpec((1, SUBLANES, LANES), lambda i, *_: (i, 0, 0)),
            pl.BlockSpec((None, SUBLANES, 6 * D_MODEL), lambda i, *_: (li, 0, 0)),
            pl.BlockSpec((1, D_MODEL), lambda i, *_: (0, 0)),
            pl.BlockSpec((1, D_MODEL), lambda i, *_: (0, 0)),
        ],
        out_specs=_pair_specs(RT, D_MODEL, n_ctx_tiles),
        scratch_shapes=[
            pltpu.VMEM((2 * RT_ROWS * ROW_SLABS, LANES), F32),
            pltpu.SemaphoreType.DMA((2,)),
        ],
    )
    return pl.pallas_call(
        _combine_kernel,
        grid_spec=grid_spec,
        out_shape=[jax.ShapeDtypeStruct((N_CTX, D_MODEL), F32), jax.ShapeDtypeStruct((N_DEC, D_MODEL), F32)],
        compiler_params=_cparams(1),
        name="moe_combine",
    )(tables["start"], tables["len"], tables["off"], y, x1, route_i, route_f, tables["off_rows"], mod, ln_g, ln_b)


def _routing_tables(tile_cnt):
    cnt = tile_cnt[:, 0, :N_EXPERTS]
    totals = jnp.sum(cnt, axis=0)
    padded = (totals + BM - 1) // BM * BM
    pends = jnp.cumsum(padded)
    pstarts = pends - padded
    start = pstarts[None, :] + jnp.cumsum(cnt, axis=0) - cnt
    off = jnp.cumsum(cnt, axis=1) - cnt
    n_valid = (pends[-1] // BM).astype(I32)
    block_start = jnp.arange(NB, dtype=I32) * BM
    block_e = jnp.minimum(jnp.sum(block_start[:, None] >= pends[None, :], axis=1), N_EXPERTS - 1)
    last_e = block_e[jnp.maximum(n_valid - 1, 0)]
    block_e = jnp.where(jnp.arange(NB) < n_valid, block_e, last_e)
    run_end = jnp.sum(block_e[None, :] <= block_e[:, None], axis=1)
    next_e = jnp.where(run_end < n_valid, block_e[jnp.minimum(run_end, NB - 1)], -1)
    off_rows = jnp.pad(off.astype(F32), ((0, 0), (0, LANES - N_EXPERTS)))
    return {
        "start": start.reshape(-1).astype(I32),
        "len": cnt.reshape(-1).astype(I32),
        "off": off.reshape(-1).astype(I32),
        "pad_start": (pstarts + totals).astype(I32),
        "pad_len": (padded - totals).astype(I32),
        "n_valid": n_valid.reshape(1),
        "block_expert": block_e.astype(I32),
        "next_expert": next_e.astype(I32),
        "off_rows": jnp.broadcast_to(off_rows[:, None, :], (N_RT, SUBLANES, LANES)),
    }


def _layer_params(p, li):
    eye = jnp.eye(N_LRU_BLOCKS, dtype=F32)
    w_rg_full = jnp.einsum("dkgio,gh->gidkho", p["w_rg"][li], eye).reshape(LRU_WIDTH, 4 * LRU_WIDTH)
    pad = LANES - N_EXPERTS
    return {
        "sgu_ln_g": p["sgu_ln_g"][li].reshape(1, SGU_WIDTH),
        "sgu_ln_b": p["sgu_ln_b"][li].reshape(1, SGU_WIDTH),
        "w_spatial": p["w_spatial"][li],
        "b_spatial_full": jnp.repeat(p["b_spatial"][li].T, SGU_GROUP, axis=1),
        "conv_w": p["conv_w"][li],
        "conv_b": p["conv_b"][li].reshape(1, LRU_WIDTH),
        "w_rg_full": w_rg_full,
        "b_rg_full": p["b_rg"][li].reshape(1, 4 * LRU_WIDTH),
        "lru_log": p["lru_log"][li],
        "w_out_all": p["w_out"],
        "ln_g1": p["ln_g"][li, 0].reshape(1, D_MODEL),
        "ln_b1": p["ln_b"][li, 0].reshape(1, D_MODEL),
        "ln_g2": p["ln_g"][li, 1].reshape(1, D_MODEL),
        "ln_b2": p["ln_b"][li, 1].reshape(1, D_MODEL),
        "w_router_pad": jnp.pad(p["w_router"][li], ((0, 0), (0, pad))),
        "b_router_pad": jnp.pad(p["b_router"][li], (0, pad)).reshape(1, LANES),
    }


def kernel(x_prompt, x_sample, cache_k, cache_v, state_lru, c, c_ctx, w_mod, b_mod, w_in, lam, subln_g, sgu_ln_g, sgu_ln_b, w_spatial, b_spatial, conv_w, conv_b, w_rg, b_rg, lru_log, w_out, ln_g, ln_b, w_router, b_router, w_gu, b_gu, w_down, b_down):
    p = dict(sgu_ln_g=sgu_ln_g, sgu_ln_b=sgu_ln_b, w_spatial=w_spatial, b_spatial=b_spatial, conv_w=conv_w,
             conv_b=conv_b, w_rg=w_rg, b_rg=b_rg, lru_log=lru_log, w_out=w_out, ln_g=ln_g, ln_b=ln_b,
             w_router=w_router, b_router=b_router)
    cvec8 = jnp.concatenate([c_ctx[None, :], c, jnp.zeros((SUBLANES - 1 - DEC_BATCH, D_MODEL), F32)], axis=0)
    mod = _modulation(cvec8, w_mod, b_mod)
    x_pair = (x_prompt.reshape(N_CTX, D_MODEL), x_sample.reshape(N_DEC, D_MODEL))
    tables = _rope_tables()
    zero_state = jnp.zeros((BATCH, 1, 2, LRU_WIDTH), F32)
    prev_kv = None
    new_s = []
    for li in range(DEPTH):
        lp = _layer_params(p, li)
        lam_init = 0.8 - 0.6 * math.exp(-0.3 * li)
        proj = _input_projection(*x_pair, mod, w_in, li)
        att_ctx, kc, vc = _context_attention(proj, lam[li], subln_g[li], lam_init, prev_kv)
        prev_kv = (kc, vc)
        att_dec = _denoise_attention(proj, cache_k, cache_v, lam[li], subln_g[li], li, lam_init, tables)
        sgu_ctx, lru_ctx, h_ctx = _mixers(proj, zero_state, 0, lp, SEQ, BATCH, 0)
        sgu_dec, lru_dec, _ = _mixers(proj, state_lru, li, lp, DEC_SEQ, DEC_BATCH, N_CTX // DEC_SEQ)
        x1, h2, route_i, route_f, tile_cnt = _output_projection(
            x_pair, (att_ctx, att_dec), (sgu_ctx, sgu_dec), (lru_ctx, lru_dec), mod, lp, li)
        rt = _routing_tables(tile_cnt)
        xs = _dispatch(rt, h2, route_i)
        y = _moe_blocks(rt, xs, w_gu, b_gu, w_down, b_down, li)
        x_pair = _combine(rt, y, x1, route_i, route_f, mod, lp["ln_g2"], lp["ln_b2"], li)
        new_s.append(h_ctx)
    y_prompt = x_pair[0].reshape(BATCH, SEQ, D_MODEL)
    y_sample = x_pair[1].reshape(DEC_BATCH, DEC_SEQ, D_MODEL)
    return (y_prompt, y_sample, prev_kv[0], prev_kv[1], jnp.stack(new_s, axis=1))
```

```python
import functools
import math

import numpy as np
import jax
import jax.numpy as jnp
from jax import lax
from jax.experimental import pallas as pl
from jax.experimental.pallas import tpu as pltpu

F32 = jnp.float32
BF16 = jnp.bfloat16
I32 = jnp.int32

D_MODEL = 1024
BATCH = 32
SEQ = 256
DEPTH = 2
DEC_BATCH = 2
DEC_SEQ = 1024
PAST_LEN = 256
GRID_W = 64
HEAD_DIM = 64
ATT_WIDTH = D_MODEL // 2
SGU_WIDTH = D_MODEL // 4
LRU_WIDTH = D_MODEL // 4
N_ATT_HEADS = ATT_WIDTH // HEAD_DIM
ATT_HALF = HEAD_DIM // 2
ROPE_FREQS = ATT_HALF // 4
ROPE_THETA = 10000.0
CHUNK = 128
N_SGU_GROUPS = 4
SGU_GROUP = SGU_WIDTH // N_SGU_GROUPS
N_LRU_BLOCKS = 4
LRU_BLOCK = LRU_WIDTH // N_LRU_BLOCKS
CONV_W = 4
LRU_C = 8.0
IN_COLS = 3 * ATT_WIDTH + 2 * SGU_WIDTH + 2 * LRU_WIDTH
N_EXPERTS = 32
TOP_K = 4
D_EXPERT = D_MODEL
SWIGLU_LIMIT = 7.0
SWIGLU_ALPHA = 1.702
DN_ALPHA = (2 * DEPTH) ** 0.25
EPS = 1e-5

N_CTX = BATCH * SEQ
N_DEC = DEC_BATCH * DEC_SEQ
N_TOK = N_CTX + N_DEC

LANES = 128
SUBLANES = 8
ROW_SLABS = D_MODEL // LANES

TM = 512
CTX_TILES = N_CTX // TM
TILES_PER_DEC = DEC_SEQ // TM
N_TILES = N_TOK // TM
MOD_TN = 512
RT = 256
N_RT = N_TOK // RT
RT_PER_TM = TM // RT
RT_ROWS = RT * TOP_K
BM = 256
NB = N_TOK * TOP_K // BM + N_EXPERTS
N_SLOTS = NB * BM
RUN_BITS = tuple(1 << b for b in range(RT.bit_length() - 1, -1, -1))
RARE_RUN = 2 * RT * TOP_K // N_EXPERTS
PAD_BITS = tuple(1 << b for b in range(BM.bit_length() - 2, -1, -1))
VMEM_LIMIT = 56 * 1024 * 1024


def _cparams(n_axes):
    return pltpu.CompilerParams(
        dimension_semantics=("arbitrary",) * n_axes,
        vmem_limit_bytes=VMEM_LIMIT)


def _mod_row(i):
    return jnp.where(i < CTX_TILES, 0, 1 + (i - CTX_TILES) // TILES_PER_DEC)


def _layer_norm_rows(z, g, b):
    mu = jnp.mean(z, axis=-1, keepdims=True)
    zc = z - mu
    var = jnp.mean(zc * zc, axis=-1, keepdims=True)
    return zc * lax.rsqrt(var + EPS) * g + b


def _pair_specs(tile, width, n_ctx_tiles):
    ctx = pl.BlockSpec((tile, width), lambda i, *_: (jnp.minimum(i, n_ctx_tiles - 1), 0))
    dec = pl.BlockSpec((tile, width), lambda i, *_: (jnp.maximum(i - n_ctx_tiles, 0), 0))
    return [ctx, dec]


def _pair_read(i, n_ctx_tiles, ctx_ref, dec_ref):
    return jnp.where(i < n_ctx_tiles, ctx_ref[...], dec_ref[...])


def _mod_kernel(cvec_ref, w_ref, b_ref, o_ref):
    cv = cvec_ref[...]
    s = cv * jax.nn.sigmoid(cv)
    s_t = s.T
    w = w_ref[0]
    rows = [jnp.sum(s_t[:, r:r + 1] * w, axis=0, keepdims=True) for r in range(1 + DEC_BATCH)]
    rows.append(jnp.zeros((SUBLANES - 1 - DEC_BATCH, MOD_TN), F32))
    o_ref[0] = jnp.concatenate(rows, axis=0) + b_ref[0]


def _modulation(cvec8, w_mod, b_mod):
    n_out = w_mod.shape[-1]
    return pl.pallas_call(
        _mod_kernel,
        grid=(DEPTH, n_out // MOD_TN),
        in_specs=[
            pl.BlockSpec((SUBLANES, D_MODEL), lambda l, j: (0, 0)),
            pl.BlockSpec((1, D_MODEL, MOD_TN), lambda l, j: (l, 0, j)),
            pl.BlockSpec((1, 1, MOD_TN), lambda l, j: (l, 0, j)),
        ],
        out_specs=pl.BlockSpec((1, SUBLANES, MOD_TN), lambda l, j: (l, 0, j)),
        out_shape=jax.ShapeDtypeStruct((DEPTH, SUBLANES, n_out), F32),
        compiler_params=_cparams(2),
        name="modulation",
    )(cvec8, w_mod, b_mod.reshape(DEPTH, 1, n_out))


def _inproj_kernel(xc_ref, xd_ref, mod_ref, w_ref, o_ref, wbf_ref):
    i = pl.program_id(0)

    @pl.when(i == 0)
    def _():
        wbf_ref[...] = w_ref[...].astype(BF16)

    m = mod_ref[pl.ds(_mod_row(i), 1), :]
    sh1 = m[:, 0:D_MODEL]
    sc1 = m[:, D_MODEL:2 * D_MODEL]
    x = _pair_read(i, CTX_TILES, xc_ref, xd_ref)
    h = (x * (1.0 + sc1) + sh1).astype(BF16)
    o_ref[...] = jnp.dot(h, wbf_ref[...], preferred_element_type=F32)


def _input_projection(x_ctx, x_dec, mod, w_in, li):
    return pl.pallas_call(
        _inproj_kernel,
        grid=(N_TILES,),
        in_specs=_pair_specs(TM, D_MODEL, CTX_TILES) + [
            pl.BlockSpec((None, SUBLANES, 6 * D_MODEL), lambda i: (li, 0, 0)),
            pl.BlockSpec((None, D_MODEL, IN_COLS), lambda i: (li, 0, 0), pipeline_mode=pl.Buffered(1)),
        ],
        out_specs=pl.BlockSpec((TM, IN_COLS), lambda i: (i, 0)),
        out_shape=jax.ShapeDtypeStruct((N_TOK, IN_COLS), F32),
        scratch_shapes=[pltpu.VMEM((D_MODEL, IN_COLS), BF16)],
        compiler_params=_cparams(1),
        name="input_projection",
    )(x_ctx, x_dec, mod, w_in)


def _lambda(lam_ref, lam_init):
    lm = lam_ref[...]
    a = jnp.sum(lm[0:1] * lm[1:2], axis=-1, keepdims=True)
    b = jnp.sum(lm[2:3] * lm[3:4], axis=-1, keepdims=True)
    return jnp.exp(a) - jnp.exp(b) + lam_init


_NT = (((1,), (1,)), ((), ()))
_TN = (((0,), (0,)), ((), ()))
_ATT_SCALE = ATT_HALF ** -0.5
_Q_SCALE = _ATT_SCALE * math.log2(math.e)
KEY_CHUNK = 128


def _attention_heads(q_parts, k_segs, values, s_scr, e_scr, lam, g_col, lam_init):
    n_heads = len(q_parts)
    for h in range(n_heads):
        for c in range(2):
            row = 0
            for k in k_segs[h][c]:
                s_scr[2 * h + c, row:row + k.shape[0], :] = lax.dot_general(
                    k, q_parts[h][c], _NT, preferred_element_type=F32)
                row += k.shape[0]
    n_keys, n_q = s_scr.shape[1:]
    chunks = [slice(r, r + KEY_CHUNK) for r in range(0, n_keys, KEY_CHUNK)]
    for n in range(2 * n_heads):
        tops = [jnp.max(s_scr[n, rows, :].reshape(KEY_CHUNK // SUBLANES, SUBLANES, n_q), axis=0)
                for rows in chunks]
        top = jnp.max(functools.reduce(jnp.maximum, tops), axis=0, keepdims=True)
        for rows in chunks:
            e_scr[n, rows, :] = jnp.exp2(s_scr[n, rows, :] - top).astype(BF16)
    outs = []
    for h in range(n_heads):
        v_aug = jnp.concatenate([values[h], jnp.ones_like(values[h])], axis=-1)
        normed = []
        for c in range(2):
            acc = lax.dot_general(v_aug, e_scr[2 * h + c], _TN, preferred_element_type=F32)
            normed.append(acc[:HEAD_DIM] * (1.0 / acc[HEAD_DIM:HEAD_DIM + 1]))
        o_t = normed[0] - lam * normed[1]
        ms = jnp.mean(o_t * o_t, axis=0, keepdims=True)
        outs.append(o_t * lax.rsqrt(ms + EPS) * g_col * (1.0 - lam_init))
    return outs


def _head_cols(h):
    lo = h * HEAD_DIM
    return [slice(lo + c * ATT_HALF, lo + (c + 1) * ATT_HALF) for c in range(2)]


def _ctx_attn_kernel(lam_ref, g_ref, q_ref, k_ref, v_ref, *rest, lam_init, stacked):
    *rest, s_scr, e_scr = rest
    if stacked:
        pk_ref, pv_ref, att_ref, ck_ref, cv_ref = rest
        ck_ref[0, 0] = pk_ref[0]
        cv_ref[0, 0] = pv_ref[0]
        put_k = lambda h, val: ck_ref.__setitem__((0, 1, h), val)
        put_v = lambda h, val: cv_ref.__setitem__((0, 1, h), val)
    else:
        att_ref, ck_ref, cv_ref = rest
        put_k = lambda h, val: ck_ref.__setitem__((0, h), val)
        put_v = lambda h, val: cv_ref.__setitem__((0, h), val)
    lam = _lambda(lam_ref, lam_init)
    k = k_ref[...]
    v = v_ref[...]
    qb = (q_ref[...] * _Q_SCALE).astype(BF16)
    kb = k.astype(BF16)
    vb = v.astype(BF16)
    heads = range(N_ATT_HEADS)
    outs = _attention_heads([[qb[:, cs] for cs in _head_cols(h)] for h in heads],
                            [[[kb[:, cs]] for cs in _head_cols(h)] for h in heads],
                            [vb[:, h * HEAD_DIM:(h + 1) * HEAD_DIM] for h in heads],
                            s_scr, e_scr, lam, g_ref[...], lam_init)
    for h in heads:
        put_k(h, k[:, h * HEAD_DIM:(h + 1) * HEAD_DIM])
        put_v(h, v[:, h * HEAD_DIM:(h + 1) * HEAD_DIM])
    att_ref[...] = jnp.concatenate(outs, axis=0).T


def _context_attention(proj, lam, subln_g, lam_init, prev_kv=None):
    stacked = prev_kv is not None
    per_layer = pl.BlockSpec((1, N_ATT_HEADS, SEQ, HEAD_DIM), lambda b: (b, 0, 0, 0))
    if stacked:
        kv_shape = jax.ShapeDtypeStruct((BATCH, DEPTH, N_ATT_HEADS, SEQ, HEAD_DIM), F32)
        kv_spec = pl.BlockSpec((1, DEPTH, N_ATT_HEADS, SEQ, HEAD_DIM), lambda b: (b, 0, 0, 0, 0))
    else:
        kv_shape = jax.ShapeDtypeStruct((BATCH, N_ATT_HEADS, SEQ, HEAD_DIM), F32)
        kv_spec = per_layer
    return pl.pallas_call(
        functools.partial(_ctx_attn_kernel, lam_init=lam_init, stacked=stacked),
        grid=(BATCH,),
        in_specs=[
            pl.BlockSpec((4, ATT_HALF), lambda b: (0, 0)),
            pl.BlockSpec((HEAD_DIM, 1), lambda b: (0, 0)),
            pl.BlockSpec((SEQ, ATT_WIDTH), lambda b: (b, 0)),
            pl.BlockSpec((SEQ, ATT_WIDTH), lambda b: (b, 1)),
            pl.BlockSpec((SEQ, ATT_WIDTH), lambda b: (b, 2)),
        ] + ([per_layer, per_layer] if stacked else []),
        out_specs=[pl.BlockSpec((SEQ, ATT_WIDTH), lambda b: (b, 0)), kv_spec, kv_spec],
        out_shape=[jax.ShapeDtypeStruct((N_CTX, ATT_WIDTH), F32), kv_shape, kv_shape],
        scratch_shapes=[pltpu.VMEM((2 * N_ATT_HEADS, SEQ, SEQ), F32),
                        pltpu.VMEM((2 * N_ATT_HEADS, SEQ, SEQ), BF16)],
        compiler_params=_cparams(1),
        name="context_attention",
    )(lam, subln_g.reshape(HEAD_DIM, 1), proj, proj, proj, *(prev_kv if stacked else ()))


def _rope_tables():
    t = np.arange(DEC_SEQ)
    pos = np.stack([t // GRID_W, t % GRID_W], axis=1).astype(np.float32)
    inv = (np.float32(ROPE_THETA) ** (-np.arange(ROPE_FREQS, dtype=np.float32) / np.float32(ROPE_FREQS)))
    j = np.arange(HEAD_DIM)
    d = j % ATT_HALF
    axis = d // (2 * ROPE_FREQS)
    u = d % (2 * ROPE_FREQS)
    ang = pos[:, axis] * inv[u % ROPE_FREQS][None, :].astype(np.float32)
    cos = np.cos(ang).astype(np.float32)
    sin = np.sin(ang).astype(np.float32)
    first = (u < ROPE_FREQS)[None, :]
    s_next = np.where(first, -sin, 0.0).astype(np.float32)
    s_prev = np.where(first, 0.0, sin).astype(np.float32)
    tile = lambda a: jnp.asarray(np.tile(a, (1, N_ATT_HEADS)))
    return tile(cos), tile(s_next), tile(s_prev)


def _rotate(x, cos, s_next, s_prev):
    width = x.shape[-1]
    return (x * cos + pltpu.roll(x, width - ROPE_FREQS, axis=1) * s_next
            + pltpu.roll(x, ROPE_FREQS, axis=1) * s_prev)


QB = 256
Q_STEPS = DEC_SEQ // QB
DEC_HEAD_GROUP = 2


def _dec_attn_kernel(lam_ref, g_ref, q_ref, k_ref, v_ref, ck_ref, cv_ref,
                     cq_ref, snq_ref, spq_ref, ck_tab, snk_tab, spk_tab,
                     att_ref, krot_ref, s_scr, e_scr, *, lam_init):
    j = pl.program_id(1)

    @pl.when(j == 0)
    def _():
        krot_ref[...] = _rotate(k_ref[...], ck_tab[...], snk_tab[...], spk_tab[...]).astype(BF16)

    lam = _lambda(lam_ref, lam_init)
    qb = (_rotate(q_ref[...], cq_ref[...], snq_ref[...], spq_ref[...]) * _Q_SCALE).astype(BF16)
    kb = krot_ref[...]
    vb = v_ref[...].astype(BF16)
    outs = []
    for h0 in range(0, N_ATT_HEADS, DEC_HEAD_GROUP):
        heads = range(h0, h0 + DEC_HEAD_GROUP)
        past_k = [ck_ref[0, 0, h].astype(BF16) for h in heads]
        k_segs = [[[pk[:, c * ATT_HALF:(c + 1) * ATT_HALF], kb[:, _head_cols(h)[c]]] for c in range(2)]
                  for h, pk in zip(heads, past_k)]
        values = [jnp.concatenate([cv_ref[0, 0, h].astype(BF16), vb[:, h * HEAD_DIM:(h + 1) * HEAD_DIM]], axis=0)
                  for h in heads]
        outs += _attention_heads([[qb[:, cs] for cs in _head_cols(h)] for h in heads], k_segs, values,
                                 s_scr, e_scr, lam, g_ref[...], lam_init)
    att_ref[...] = jnp.concatenate(outs, axis=0).T


def _denoise_attention(proj, cache_k, cache_v, lam, subln_g, li, lam_init, tables):
    cos, s_next, s_prev = tables
    row0 = N_CTX // QB
    seq0 = N_CTX // DEC_SEQ
    q_tab = pl.BlockSpec((QB, ATT_WIDTH), lambda b, j: (j, 0))
    k_tab = pl.BlockSpec((DEC_SEQ, ATT_WIDTH), lambda b, j: (0, 0))
    cache_spec = pl.BlockSpec((1, 1, N_ATT_HEADS, PAST_LEN, HEAD_DIM), lambda b, j: (b, li, 0, 0, 0))
    return pl.pallas_call(
        functools.partial(_dec_attn_kernel, lam_init=lam_init),
        grid=(DEC_BATCH, Q_STEPS),
        in_specs=[
            pl.BlockSpec((4, ATT_HALF), lambda b, j: (0, 0)),
            pl.BlockSpec((HEAD_DIM, 1), lambda b, j: (0, 0)),
            pl.BlockSpec((QB, ATT_WIDTH), lambda b, j: (row0 + b * Q_STEPS + j, 0)),
            pl.BlockSpec((DEC_SEQ, ATT_WIDTH), lambda b, j: (seq0 + b, 1)),
            pl.BlockSpec((DEC_SEQ, ATT_WIDTH), lambda b, j: (seq0 + b, 2)),
            cache_spec, cache_spec,
            q_tab, q_tab, q_tab, k_tab, k_tab, k_tab,
        ],
        out_specs=pl.BlockSpec((QB, ATT_WIDTH), lambda b, j: (b * Q_STEPS + j, 0)),
        out_shape=jax.ShapeDtypeStruct((N_DEC, ATT_WIDTH), F32),
        scratch_shapes=[pltpu.VMEM((DEC_SEQ, ATT_WIDTH), BF16),
                        pltpu.VMEM((2 * DEC_HEAD_GROUP, PAST_LEN + DEC_SEQ, QB), F32),
                        pltpu.VMEM((2 * DEC_HEAD_GROUP, PAST_LEN + DEC_SEQ, QB), BF16)],
        compiler_params=_cparams(2),
        name="denoise_attention",
    )(lam, subln_g.reshape(HEAD_DIM, 1), proj, proj, proj, cache_k, cache_v,
      cos, s_next, s_prev, cos, s_next, s_prev)


def _softplus(z):
    return jnp.maximum(z, 0.0) + jnp.log1p(jnp.exp(-jnp.abs(z)))


def _mixer_kernel(su_ref, sv_ref, rx_ref, rg_ref, h0_ref, lng_ref, lnb_ref, ws_ref, bs_ref,
                  cw_ref, cb_ref, wrg_ref, brg_ref, lrulog_ref,
                  sgu_ref, lru_ref, hlast_ref, a_scr, b_scr, h_scr, *, seq_len):
    vn = _layer_norm_rows(sv_ref[...], lng_ref[...], lnb_ref[...])
    lane_group = lax.broadcasted_iota(I32, (CHUNK, SGU_WIDTH), 1) // SGU_GROUP
    for n in range(seq_len // CHUNK):
        rows = slice(n * CHUNK, (n + 1) * CHUNK)
        vc = vn[rows].astype(BF16)
        s = jnp.zeros((CHUNK, SGU_WIDTH), F32)
        for g in range(N_SGU_GROUPS):
            sg = jnp.dot(ws_ref[g].astype(BF16), vc, preferred_element_type=F32)
            s = jnp.where(lane_group == g, sg, s)
        sgu_ref[rows, :] = su_ref[rows, :] * (s + bs_ref[...])

    x = rx_ref[...]
    row = lax.broadcasted_iota(I32, (seq_len, LRU_WIDTH), 0)

    def shifted(val, d, fill):
        rolled = pltpu.roll(val, d % seq_len, axis=0)
        inside = (row >= d) if d > 0 else (row < seq_len + d)
        return jnp.where(inside, rolled, fill)

    left = CONV_W // 2
    xc = cb_ref[...] + x * cw_ref[left:left + 1, :]
    for tap in range(CONV_W):
        if tap != left:
            xc = xc + shifted(x, left - tap, 0.0) * cw_ref[tap:tap + 1, :]
    gates = jax.nn.sigmoid(jnp.dot(xc.astype(BF16), wrg_ref[...].astype(BF16),
                                   preferred_element_type=F32) + brg_ref[...])
    in_chunk = row % SUBLANES
    n_chunks = seq_len // SUBLANES
    h0 = h0_ref[0, 0]
    lasts = []
    for direction in range(2):
        reverse = direction == 1
        base = direction * 2 * LRU_WIDTH
        r = gates[:, base:base + LRU_WIDTH]
        gi = gates[:, base + LRU_WIDTH:base + 2 * LRU_WIDTH]
        log_a = -LRU_C * r * _softplus(-lrulog_ref[direction:direction + 1, :])
        a = jnp.exp(log_a)
        b = jnp.sqrt(-jnp.tanh(log_a) * (a * a + 1.0)) * gi * xc
        for d in (1, 2, 4):
            if reverse:
                inside = in_chunk < SUBLANES - d
                a_n = jnp.where(inside, pltpu.roll(a, seq_len - d, axis=0), 1.0)
                b_n = jnp.where(inside, pltpu.roll(b, seq_len - d, axis=0), 0.0)
            else:
                inside = in_chunk >= d
                a_n = jnp.where(inside, pltpu.roll(a, d, axis=0), 1.0)
                b_n = jnp.where(inside, pltpu.roll(b, d, axis=0), 0.0)
            b = a * b_n + b
            a = a * a_n
        a_scr[...] = a
        b_scr[...] = b

        def chunk_step(c, carry, reverse=reverse):
            cc = n_chunks - 1 - c if reverse else c
            off = pl.multiple_of(cc * SUBLANES, SUBLANES)
            hc = a_scr[pl.ds(off, SUBLANES), :] * carry + b_scr[pl.ds(off, SUBLANES), :]
            if reverse:
                h_scr[pl.ds(off, SUBLANES), :] = h_scr[pl.ds(off, SUBLANES), :] + hc
                return hc[0:1, :]
            h_scr[pl.ds(off, SUBLANES), :] = hc
            return hc[SUBLANES - 1:SUBLANES, :]

        lasts.append(lax.fori_loop(0, n_chunks, chunk_step, h0[direction:direction + 1, :]))
    lru_ref[...] = h_scr[...] * jax.nn.gelu(rg_ref[...])
    hlast_ref[0] = jnp.concatenate(lasts, axis=0)


def _mixers(proj, h0, h0_layer, lp, seq_len, n_seq, row_block0):
    col0 = 3 * ATT_WIDTH // SGU_WIDTH
    col = lambda c: pl.BlockSpec((seq_len, SGU_WIDTH), lambda b: (row_block0 + b, col0 + c))
    full = lambda shape: pl.BlockSpec(shape, lambda b: (0,) * len(shape))
    out_rows = pl.BlockSpec((seq_len, SGU_WIDTH), lambda b: (b, 0))
    return pl.pallas_call(
        functools.partial(_mixer_kernel, seq_len=seq_len),
        grid=(n_seq,),
        in_specs=[
            col(0), col(1), col(2), col(3),
            pl.BlockSpec((1, 1, 2, LRU_WIDTH), lambda b: (b, h0_layer, 0, 0)),
            full((1, SGU_WIDTH)), full((1, SGU_WIDTH)),
            full((N_SGU_GROUPS, CHUNK, CHUNK)), full((CHUNK, SGU_WIDTH)),
            full((CONV_W, LRU_WIDTH)), full((1, LRU_WIDTH)),
            full((LRU_WIDTH, 4 * LRU_WIDTH)), full((1, 4 * LRU_WIDTH)),
            full((2, LRU_WIDTH)),
        ],
        out_specs=[out_rows, out_rows, pl.BlockSpec((1, 2, LRU_WIDTH), lambda b: (b, 0, 0))],
        out_shape=[
            jax.ShapeDtypeStruct((n_seq * seq_len, SGU_WIDTH), F32),
            jax.ShapeDtypeStruct((n_seq * seq_len, LRU_WIDTH), F32),
            jax.ShapeDtypeStruct((n_seq, 2, LRU_WIDTH), F32),
        ],
        scratch_shapes=[pltpu.VMEM((seq_len, LRU_WIDTH), F32)] * 3,
        compiler_params=_cparams(1),
        name="mixers_%d" % seq_len,
    )(proj, proj, proj, proj, h0, lp["sgu_ln_g"], lp["sgu_ln_b"], lp["w_spatial"], lp["b_spatial_full"],
      lp["conv_w"], lp["conv_b"], lp["w_rg_full"], lp["b_rg_full"], lp["lru_log"])


def _split_bf16(v):
    hi = v.astype(BF16)
    lo = (v - hi.astype(F32)).astype(BF16)
    return hi, lo


def _outproj_kernel(xc_ref, xd_ref, ac_ref, ad_ref, sc_ref, sd_ref, lc_ref, ld_ref,
                    mod_ref, wout_ref, lng_ref, lnb_ref, wr_ref, br_ref,
                    x1_ref, h2_ref, ri_ref, rf_ref, cnt_ref, wbf_ref):
    i = pl.program_id(0)

    @pl.when(i == 0)
    def _():
        wbf_ref[...] = wout_ref[...].astype(BF16)

    m = mod_ref[pl.ds(_mod_row(i), 1), :]
    g1 = m[:, 2 * D_MODEL:3 * D_MODEL]
    sh2 = m[:, 3 * D_MODEL:4 * D_MODEL]
    sc2 = m[:, 4 * D_MODEL:5 * D_MODEL]
    a0, a1 = ATT_WIDTH, ATT_WIDTH + SGU_WIDTH
    x = _pair_read(i, CTX_TILES, xc_ref, xd_ref)
    att = _pair_read(i, CTX_TILES, ac_ref, ad_ref).astype(BF16)
    sgu = _pair_read(i, CTX_TILES, sc_ref, sd_ref).astype(BF16)
    lru = _pair_read(i, CTX_TILES, lc_ref, ld_ref).astype(BF16)
    mix = (jnp.dot(att, wbf_ref[0:a0, :], preferred_element_type=F32)
           + jnp.dot(sgu, wbf_ref[a0:a1, :], preferred_element_type=F32)
           + jnp.dot(lru, wbf_ref[a1:, :], preferred_element_type=F32))
    x1 = _layer_norm_rows(DN_ALPHA * x + g1 * mix, lng_ref[...], lnb_ref[...])
    x1_ref[...] = x1
    h2 = x1 * (1.0 + sc2) + sh2
    h2_ref[...] = h2.astype(BF16)

    h_hi, h_lo = _split_bf16(h2)
    w_hi, w_lo = _split_bf16(wr_ref[...])
    logits = (jnp.dot(h_hi, w_hi, preferred_element_type=F32)
              + jnp.dot(h_lo, w_hi, preferred_element_type=F32)
              + jnp.dot(h_hi, w_lo, preferred_element_type=F32)) + br_ref[...]
    lane = lax.broadcasted_iota(I32, (TM, LANES), 1)
    lane_f = lane.astype(F32)
    neg_inf = jnp.float32(-jnp.inf)
    work = jnp.where(lane < N_EXPERTS, logits, neg_inf)
    vals, idxs = [], []
    for _ in range(TOP_K):
        top = jnp.max(work, axis=-1, keepdims=True)
        idx = jnp.min(jnp.where(work == top, lane_f, float(LANES)), axis=-1, keepdims=True)
        vals.append(top)
        idxs.append(idx)
        work = jnp.where(lane_f == idx, neg_inf, work)
    exps = [jnp.exp(v - vals[0]) for v in vals]
    denom = exps[0] + exps[1] + exps[2] + exps[3]
    onehot = jnp.zeros((TM, LANES), F32)
    for idx in idxs:
        onehot = onehot + (lane_f == idx).astype(F32)
    r_i = lax.broadcasted_iota(I32, (RT, RT), 0)
    c_i = lax.broadcasted_iota(I32, (RT, RT), 1)
    tri = (r_i > c_i).astype(F32).astype(BF16)
    before = jnp.concatenate(
        [jnp.dot(tri, onehot[t * RT:(t + 1) * RT].astype(BF16), preferred_element_type=F32)
         for t in range(RT_PER_TM)], axis=0)
    ri = jnp.zeros((TM, LANES), F32)
    rf = jnp.zeros((TM, LANES), F32)
    for k in range(TOP_K):
        rank = jnp.sum(jnp.where(lane_f == idxs[k], before, 0.0), axis=-1, keepdims=True)
        ri = jnp.where(lane == k, idxs[k], ri)
        ri = jnp.where(lane == TOP_K + k, rank, ri)
        rf = jnp.where(lane == k, exps[k] / denom, rf)
    ri_ref[...] = ri.astype(I32)
    rf_ref[...] = rf
    for t in range(RT_PER_TM):
        total = jnp.sum(onehot[t * RT:(t + 1) * RT], axis=0, keepdims=True)
        cnt_ref[t] = jnp.broadcast_to(total, (SUBLANES, LANES)).astype(I32)


def _output_projection(x_pair, att_pair, sgu_pair, lru_pair, mod, lp, li):
    rows = lambda w: pl.BlockSpec((TM, w), lambda i: (i, 0))
    full = lambda shape: pl.BlockSpec(shape, lambda i: (0,) * len(shape))
    return pl.pallas_call(
        _outproj_kernel,
        grid=(N_TILES,),
        in_specs=(_pair_specs(TM, D_MODEL, CTX_TILES) + _pair_specs(TM, ATT_WIDTH, CTX_TILES)
                  + _pair_specs(TM, SGU_WIDTH, CTX_TILES) + _pair_specs(TM, LRU_WIDTH, CTX_TILES) + [
            pl.BlockSpec((None, SUBLANES, 6 * D_MODEL), lambda i: (li, 0, 0)),
            pl.BlockSpec((None, D_MODEL, D_MODEL), lambda i: (li, 0, 0), pipeline_mode=pl.Buffered(1)),
            full((1, D_MODEL)), full((1, D_MODEL)),
            full((D_MODEL, LANES)), full((1, LANES)),
        ]),
        out_specs=[
            rows(D_MODEL), rows(D_MODEL), rows(LANES), rows(LANES),
            pl.BlockSpec((RT_PER_TM, SUBLANES, LANES), lambda i: (i, 0, 0)),
        ],
        out_shape=[
            jax.ShapeDtypeStruct((N_TOK, D_MODEL), F32),
            jax.ShapeDtypeStruct((N_TOK, D_MODEL), BF16),
            jax.ShapeDtypeStruct((N_TOK, LANES), I32),
            jax.ShapeDtypeStruct((N_TOK, LANES), F32),
            jax.ShapeDtypeStruct((N_RT, SUBLANES, LANES), I32),
        ],
        scratch_shapes=[pltpu.VMEM((D_MODEL, D_MODEL), BF16)],
        compiler_params=_cparams(1),
        name="output_projection",
    )(*x_pair, *att_pair, *sgu_pair, *lru_pair, mod, lp["w_out_all"], lp["ln_g1"], lp["ln_b1"],
      lp["w_router_pad"], lp["b_router_pad"])


def _row_ds(row, n_rows):
    return pl.ds(pl.multiple_of(row * ROW_SLABS, ROW_SLABS), n_rows * ROW_SLABS)


def _for_each_run_piece(length, bits, fn, rare_from=None):
    def pieces(some_bits):
        for bit in some_bits:
            done = length & (-2 * bit)
            @pl.when((length & bit) != 0)
            def _(done=done, bit=bit):
                fn(done, bit)

    rare = [b for b in bits if rare_from is not None and b >= rare_from]
    if rare:
        @pl.when(length >= rare_from)
        def _():
            pieces(rare)
    pieces([b for b in bits if b not in rare])


def _tile_run_copies(tile, slot, start_ref, len_ref, off_ref, sorted_hbm, buf, sem, to_sorted):
    for e in range(N_EXPERTS):
        t = tile * N_EXPERTS + e
        start, length, off = start_ref[t], len_ref[t], off_ref[t]

        def piece(done, bit, start=start, off=off):
            packed = buf.at[_row_ds(slot * RT_ROWS + off + done, bit), :]
            srt = sorted_hbm.at[_row_ds(start + done, bit), :]
            if to_sorted:
                pltpu.make_async_copy(packed, srt, sem).start()
            else:
                pltpu.make_async_copy(srt, packed, sem).start()

        _for_each_run_piece(length, RUN_BITS, piece, rare_from=RARE_RUN)


def _tile_runs_wait(slot, buf, sem):
    region = buf.at[_row_ds(slot * RT_ROWS, RT_ROWS), :]
    pltpu.make_async_copy(region, region, sem).wait()


def _packed_positions(ri, off_row):
    lane = lax.broadcasted_iota(I32, (RT, LANES), 1)
    pos = []
    for k in range(TOP_K):
        e_k = ri[:, k:k + 1]
        base = jnp.sum(jnp.where(lane == e_k, off_row, 0.0), axis=-1, keepdims=True)
        pos.append(base + ri[:, TOP_K + k:TOP_K + k + 1].astype(F32))
    return pos


def _slab_columns(buf, row0, n_rows):
    return jnp.concatenate(
        [buf[pl.ds(row0 * ROW_SLABS + s, n_rows, stride=ROW_SLABS), :] for s in range(ROW_SLABS)], axis=-1)


def _dispatch_kernel(start_ref, len_ref, off_ref, pstart_ref, plen_ref, nv_ref,
                     h2_ref, ri_ref, offrow_ref, xs_hbm, xc, zbuf, sem, zsem):
    i = pl.program_id(0)
    n_steps = pl.num_programs(0)
    slot = i % 2

    @pl.when(i == 0)
    def _():
        zbuf[...] = jnp.zeros_like(zbuf)

        def zero_fill(wait):
            def go(cp):
                cp.wait() if wait else cp.start()

            def per_expert(e, carry):
                def piece(done, bit):
                    go(pltpu.make_async_copy(zbuf.at[pl.ds(0, bit * ROW_SLABS), :],
                                             xs_hbm.at[_row_ds(pstart_ref[e] + done, bit), :], zsem))
                _for_each_run_piece(plen_ref[e], PAD_BITS, piece)
                return carry
            lax.fori_loop(0, N_EXPERTS, per_expert, 0)

            def per_block(b, carry):
                go(pltpu.make_async_copy(zbuf, xs_hbm.at[_row_ds(b * BM, BM), :], zsem))
                return carry
            lax.fori_loop(nv_ref[0], NB, per_block, 0)

        zero_fill(False)
        zero_fill(True)

    @pl.when(i >= 2)
    def _():
        _tile_runs_wait(slot, xc, sem.at[slot])

    pos = _packed_positions(ri_ref[...], offrow_ref[0, 0:1, :])
    col = lax.broadcasted_iota(I32, (RT, RT_ROWS), 1).astype(F32)
    sel = jnp.zeros((RT, RT_ROWS), F32)
    for p in pos:
        sel = sel + (col == p).astype(F32)
    packed = lax.dot_general(sel.astype(BF16), h2_ref[...], _TN, preferred_element_type=F32)
    for s in range(ROW_SLABS):
        xc[pl.ds(slot * RT_ROWS * ROW_SLABS + s, RT_ROWS, stride=ROW_SLABS), :] = (
            packed[:, s * LANES:(s + 1) * LANES])
    _tile_run_copies(i, slot, start_ref, len_ref, off_ref, xs_hbm, xc, sem.at[slot], to_sorted=True)

    @pl.when(i == n_steps - 1)
    def _():
        _tile_runs_wait(1 - slot, xc, sem.at[1 - slot])
        _tile_runs_wait(slot, xc, sem.at[slot])


def _dispatch(tables, h2, route_i):
    grid_spec = pltpu.PrefetchScalarGridSpec(
        num_scalar_prefetch=6,
        grid=(N_RT,),
        in_specs=[
            pl.BlockSpec((RT, D_MODEL), lambda i, *_: (i, 0)),
            pl.BlockSpec((RT, LANES), lambda i, *_: (i, 0)),
            pl.BlockSpec((1, SUBLANES, LANES), lambda i, *_: (i, 0, 0)),
        ],
        out_specs=pl.BlockSpec(memory_space=pl.ANY),
        scratch_shapes=[
            pltpu.VMEM((2 * RT_ROWS * ROW_SLABS, LANES), F32),
            pltpu.VMEM((BM * ROW_SLABS, LANES), F32),
            pltpu.SemaphoreType.DMA((2,)),
            pltpu.SemaphoreType.DMA(()),
        ],
    )
    return pl.pallas_call(
        _dispatch_kernel,
        grid_spec=grid_spec,
        out_shape=jax.ShapeDtypeStruct((N_SLOTS * ROW_SLABS, LANES), F32),
        compiler_params=_cparams(1),
        name="moe_dispatch",
    )(tables["start"], tables["len"], tables["off"], tables["pad_start"], tables["pad_len"], tables["n_valid"],
      h2, route_i, tables["off_rows"])


def _moe_kernel(be_ref, nxt_ref, nv_ref, xs_hbm, wgu_hbm, bgu_ref, wdn_hbm, bdn_ref, y_hbm,
                xbuf, ybuf, gu_stage, dn_stage, wgu_bf, wdn_bf, wsem, xsem, ysem, *, li):
    n_valid = nv_ref[0]

    def weight_copies(e):
        return (pltpu.make_async_copy(wgu_hbm.at[li, e], gu_stage, wsem.at[0]),
                pltpu.make_async_copy(wdn_hbm.at[li, e], dn_stage, wsem.at[1]))

    def fetch(e):
        for cp in weight_copies(e):
            cp.start()

    def x_copy(b, slot):
        return pltpu.make_async_copy(xs_hbm.at[_row_ds(b * BM, BM), :], xbuf.at[slot], xsem.at[slot])

    def y_copy(b, slot):
        return pltpu.make_async_copy(ybuf.at[slot], y_hbm.at[_row_ds(b * BM, BM), :], ysem.at[slot])

    fetch(be_ref[0])
    x_copy(0, 0).start()

    def block(b, carry):
        slot = b % 2
        e = be_ref[b]
        new_expert = jnp.logical_or(b == 0, e != be_ref[jnp.maximum(b - 1, 0)])

        @pl.when(b + 1 < n_valid)
        def _():
            x_copy(b + 1, 1 - slot).start()

        @pl.when(new_expert)
        def _():
            for cp in weight_copies(e):
                cp.wait()
            wgu_bf[...] = gu_stage[...].astype(BF16)
            wdn_bf[...] = dn_stage[...].astype(BF16)

            @pl.when(nxt_ref[b] >= 0)
            def _():
                fetch(nxt_ref[b])

        x_copy(b, slot).wait()

        @pl.when(b >= 2)
        def _():
            y_copy(b - 2, slot).wait()

        x = _slab_columns(xbuf.at[slot], 0, BM).astype(BF16)
        bgu = bgu_ref[e]
        g = jnp.dot(x, wgu_bf[:, :D_EXPERT], preferred_element_type=F32) + bgu[:, :D_EXPERT]
        u = jnp.dot(x, wgu_bf[:, D_EXPERT:], preferred_element_type=F32) + bgu[:, D_EXPERT:]
        g = jnp.minimum(g, SWIGLU_LIMIT)
        u = jnp.clip(u, -SWIGLU_LIMIT, SWIGLU_LIMIT)
        act = ((u + 1.0) * (0.5 * g * (1.0 + jnp.tanh((0.5 * SWIGLU_ALPHA) * g)))).astype(BF16)
        y = jnp.dot(act, wdn_bf[...], preferred_element_type=F32) + bdn_ref[e]
        out = ybuf.at[slot]
        for s in range(ROW_SLABS):
            out[pl.ds(s, BM, stride=ROW_SLABS), :] = y[:, s * LANES:(s + 1) * LANES]
        y_copy(b, slot).start()
        return carry

    lax.fori_loop(0, n_valid, block, 0)

    @pl.when(n_valid >= 2)
    def _():
        y_copy(n_valid - 2, n_valid % 2).wait()
    y_copy(n_valid - 1, (n_valid - 1) % 2).wait()

    ybuf[0] = jnp.zeros((BM * ROW_SLABS, LANES), F32)

    def zero_blocks(wait):
        def one(b, carry):
            cp = y_copy(b, 0)
            cp.wait() if wait else cp.start()
            return carry
        lax.fori_loop(n_valid, NB, one, 0)

    zero_blocks(False)
    zero_blocks(True)


def _moe_blocks(tables, xs, w_gu, b_gu, w_down, b_down, li):
    grid_spec = pltpu.PrefetchScalarGridSpec(
        num_scalar_prefetch=3,
        grid=(1,),
        in_specs=[
            pl.BlockSpec(memory_space=pl.ANY),
            pl.BlockSpec(memory_space=pl.ANY),
            pl.BlockSpec((None, N_EXPERTS, 1, 2 * D_EXPERT), lambda i, *_: (li, 0, 0, 0)),
            pl.BlockSpec(memory_space=pl.ANY),
            pl.BlockSpec((None, N_EXPERTS, 1, D_MODEL), lambda i, *_: (li, 0, 0, 0)),
        ],
        out_specs=pl.BlockSpec(memory_space=pl.ANY),
        scratch_shapes=[
            pltpu.VMEM((2, BM * ROW_SLABS, LANES), F32),
            pltpu.VMEM((2, BM * ROW_SLABS, LANES), F32),
            pltpu.VMEM((D_MODEL, 2 * D_EXPERT), F32),
            pltpu.VMEM((D_EXPERT, D_MODEL), F32),
            pltpu.VMEM((D_MODEL, 2 * D_EXPERT), BF16),
            pltpu.VMEM((D_EXPERT, D_MODEL), BF16),
            pltpu.SemaphoreType.DMA((2,)),
            pltpu.SemaphoreType.DMA((2,)),
            pltpu.SemaphoreType.DMA((2,)),
        ],
    )
    return pl.pallas_call(
        functools.partial(_moe_kernel, li=li),
        grid_spec=grid_spec,
        out_shape=jax.ShapeDtypeStruct((N_SLOTS * ROW_SLABS, LANES), F32),
        compiler_params=_cparams(1),
        name="moe_experts",
    )(tables["block_expert"], tables["next_expert"], tables["n_valid"], xs, w_gu,
      b_gu.reshape(DEPTH, N_EXPERTS, 1, 2 * D_EXPERT), w_down, b_down.reshape(DEPTH, N_EXPERTS, 1, D_MODEL))


def _combine_kernel(start_ref, len_ref, off_ref, y_hbm, x1_ref, ri_ref, rf_ref, offrow_ref, mod_ref,
                    lng_ref, lnb_ref, oc_ref, od_ref, ybuf, sem):
    i = pl.program_id(0)
    n_steps = pl.num_programs(0)
    slot = i % 2
    fetch = functools.partial(_tile_run_copies, start_ref=start_ref, len_ref=len_ref, off_ref=off_ref,
                              sorted_hbm=y_hbm, buf=ybuf, to_sorted=False)

    @pl.when(i == 0)
    def _():
        fetch(0, 0, sem=sem.at[0])

    @pl.when(i + 1 < n_steps)
    def _():
        fetch(i + 1, 1 - slot, sem=sem.at[1 - slot])

    _tile_runs_wait(slot, ybuf, sem.at[slot])
    pos = _packed_positions(ri_ref[...], offrow_ref[0, 0:1, :])
    gates = rf_ref[...]
    col = lax.broadcasted_iota(I32, (RT, RT_ROWS), 1).astype(F32)
    mix = jnp.zeros((RT, RT_ROWS), F32)
    for k in range(TOP_K):
        mix = mix + jnp.where(col == pos[k], gates[:, k:k + 1], 0.0)
    rows = _slab_columns(ybuf, slot * RT_ROWS, RT_ROWS).astype(BF16)
    ffn = jnp.dot(mix.astype(BF16), rows, preferred_element_type=F32)
    m = mod_ref[pl.ds(_mod_row(i // RT_PER_TM), 1), :]
    g2 = m[:, 5 * D_MODEL:6 * D_MODEL]
    out = _layer_norm_rows(DN_ALPHA * x1_ref[...] + g2 * ffn, lng_ref[...], lnb_ref[...])

    @pl.when(i < N_CTX // RT)
    def _():
        oc_ref[...] = out

    @pl.when(i >= N_CTX // RT)
    def _():
        od_ref[...] = out


def _combine(tables, y, x1, route_i, route_f, mod, ln_g, ln_b, li):
    n_ctx_tiles = N_CTX // RT
    grid_spec = pltpu.PrefetchScalarGridSpec(
        num_scalar_prefetch=3,
        grid=(N_RT,),
        in_specs=[
            pl.BlockSpec(memory_space=pl.ANY),
            pl.BlockSpec((RT, D_MODEL), lambda i, *_: (i, 0)),
            pl.BlockSpec((RT, LANES), lambda i, *_: (i, 0)),
            pl.BlockSpec((RT, LANES), lambda i, *_: (i, 0)),
            pl.BlockSpec((1, SUBLANES, LANES), lambda i, *_: (i, 0, 0)),
            pl.BlockSpec((None, SUBLANES, 6 * D_MODEL), lambda i, *_: (li, 0, 0)),
            pl.BlockSpec((1, D_MODEL), lambda i, *_: (0, 0)),
            pl.BlockSpec((1, D_MODEL), lambda i, *_: (0, 0)),
        ],
        out_specs=_pair_specs(RT, D_MODEL, n_ctx_tiles),
        scratch_shapes=[
            pltpu.VMEM((2 * RT_ROWS * ROW_SLABS, LANES), F32),
            pltpu.SemaphoreType.DMA((2,)),
        ],
    )
    return pl.pallas_call(
        _combine_kernel,
        grid_spec=grid_spec,
        out_shape=[jax.ShapeDtypeStruct((N_CTX, D_MODEL), F32), jax.ShapeDtypeStruct((N_DEC, D_MODEL), F32)],
        compiler_params=_cparams(1),
        name="moe_combine",
    )(tables["start"], tables["len"], tables["off"], y, x1, route_i, route_f, tables["off_rows"], mod, ln_g, ln_b)


def _routing_tables(tile_cnt):
    cnt = tile_cnt[:, 0, :N_EXPERTS]
    totals = jnp.sum(cnt, axis=0)
    padded = (totals + BM - 1) // BM * BM
    pends = jnp.cumsum(padded)
    pstarts = pends - padded
    start = pstarts[None, :] + jnp.cumsum(cnt, axis=0) - cnt
    off = jnp.cumsum(cnt, axis=1) - cnt
    n_valid = (pends[-1] // BM).astype(I32)
    block_start = jnp.arange(NB, dtype=I32) * BM
    block_e = jnp.minimum(jnp.sum(block_start[:, None] >= pends[None, :], axis=1), N_EXPERTS - 1)
    last_e = block_e[jnp.maximum(n_valid - 1, 0)]
    block_e = jnp.where(jnp.arange(NB) < n_valid, block_e, last_e)
    run_end = jnp.sum(block_e[None, :] <= block_e[:, None], axis=1)
    next_e = jnp.where(run_end < n_valid, block_e[jnp.minimum(run_end, NB - 1)], -1)
    off_rows = jnp.pad(off.astype(F32), ((0, 0), (0, LANES - N_EXPERTS)))
    return {
        "start": start.reshape(-1).astype(I32),
        "len": cnt.reshape(-1).astype(I32),
        "off": off.reshape(-1).astype(I32),
        "pad_start": (pstarts + totals).astype(I32),
        "pad_len": (padded - totals).astype(I32),
        "n_valid": n_valid.reshape(1),
        "block_expert": block_e.astype(I32),
        "next_expert": next_e.astype(I32),
        "off_rows": jnp.broadcast_to(off_rows[:, None, :], (N_RT, SUBLANES, LANES)),
    }


def _layer_params(p, li):
    eye = jnp.eye(N_LRU_BLOCKS, dtype=F32)
    w_rg_full = jnp.einsum("dkgio,gh->gidkho", p["w_rg"][li], eye).reshape(LRU_WIDTH, 4 * LRU_WIDTH)
    pad = LANES - N_EXPERTS
    return {
        "sgu_ln_g": p["sgu_ln_g"][li].reshape(1, SGU_WIDTH),
        "sgu_ln_b": p["sgu_ln_b"][li].reshape(1, SGU_WIDTH),
        "w_spatial": p["w_spatial"][li],
        "b_spatial_full": jnp.repeat(p["b_spatial"][li].T, SGU_GROUP, axis=1),
        "conv_w": p["conv_w"][li],
        "conv_b": p["conv_b"][li].reshape(1, LRU_WIDTH),
        "w_rg_full": w_rg_full,
        "b_rg_full": p["b_rg"][li].reshape(1, 4 * LRU_WIDTH),
        "lru_log": p["lru_log"][li],
        "w_out_all": p["w_out"],
        "ln_g1": p["ln_g"][li, 0].reshape(1, D_MODEL),
        "ln_b1": p["ln_b"][li, 0].reshape(1, D_MODEL),
        "ln_g2": p["ln_g"][li, 1].reshape(1, D_MODEL),
        "ln_b2": p["ln_b"][li, 1].reshape(1, D_MODEL),
        "w_router_pad": jnp.pad(p["w_router"][li], ((0, 0), (0, pad))),
        "b_router_pad": jnp.pad(p["b_router"][li], (0, pad)).reshape(1, LANES),
    }


def kernel(x_prompt, x_sample, cache_k, cache_v, state_lru, c, c_ctx, w_mod, b_mod, w_in, lam, subln_g, sgu_ln_g, sgu_ln_b, w_spatial, b_spatial, conv_w, conv_b, w_rg, b_rg, lru_log, w_out, ln_g, ln_b, w_router, b_router, w_gu, b_gu, w_down, b_down):
    p = dict(sgu_ln_g=sgu_ln_g, sgu_ln_b=sgu_ln_b, w_spatial=w_spatial, b_spatial=b_spatial, conv_w=conv_w,
             conv_b=conv_b, w_rg=w_rg, b_rg=b_rg, lru_log=lru_log, w_out=w_out, ln_g=ln_g, ln_b=ln_b,
             w_router=w_router, b_router=b_router)
    cvec8 = jnp.concatenate([c_ctx[None, :], c, jnp.zeros((SUBLANES - 1 - DEC_BATCH, D_MODEL), F32)], axis=0)
    mod = _modulation(cvec8, w_mod, b_mod)
    x_pair = (x_prompt.reshape(N_CTX, D_MODEL), x_sample.reshape(N_DEC, D_MODEL))
    tables = _rope_tables()
    zero_state = jnp.zeros((BATCH, 1, 2, LRU_WIDTH), F32)
    prev_kv = None
    new_s = []
    for li in range(DEPTH):
        lp = _layer_params(p, li)
        lam_init = 0.8 - 0.6 * math.exp(-0.3 * li)
        proj = _input_projection(*x_pair, mod, w_in, li)
        att_ctx, kc, vc = _context_attention(proj, lam[li], subln_g[li], lam_init, prev_kv)
        prev_kv = (kc, vc)
        att_dec = _denoise_attention(proj, cache_k, cache_v, lam[li], subln_g[li], li, lam_init, tables)
        sgu_ctx, lru_ctx, h_ctx = _mixers(proj, zero_state, 0, lp, SEQ, BATCH, 0)
        sgu_dec, lru_dec, _ = _mixers(proj, state_lru, li, lp, DEC_SEQ, DEC_BATCH, N_CTX // DEC_SEQ)
        x1, h2, route_i, route_f, tile_cnt = _output_projection(
            x_pair, (att_ctx, att_dec), (sgu_ctx, sgu_dec), (lru_ctx, lru_dec), mod, lp, li)
        rt = _routing_tables(tile_cnt)
        xs = _dispatch(rt, h2, route_i)
        y = _moe_blocks(rt, xs, w_gu, b_gu, w_down, b_down, li)
        x_pair = _combine(rt, y, x1, route_i, route_f, mod, lp["ln_g2"], lp["ln_b2"], li)
        new_s.append(h_ctx)
    y_prompt = x_pair[0].reshape(BATCH, SEQ, D_MODEL)
    y_sample = x_pair[1].reshape(DEC_BATCH, DEC_SEQ, D_MODEL)
    return (y_prompt, y_sample, prev_kv[0], prev_kv[1], jnp.stack(new_s, axis=1))
```

```python
import functools
import math

import numpy as np
import jax
import jax.numpy as jnp
from jax import lax
from jax.experimental import pallas as pl
from jax.experimental.pallas import tpu as pltpu

F32 = jnp.float32
BF16 = jnp.bfloat16
I32 = jnp.int32

D_MODEL = 1024
BATCH = 32
SEQ = 256
DEPTH = 2
DEC_BATCH = 2
DEC_SEQ = 1024
PAST_LEN = 256
GRID_W = 64
HEAD_DIM = 64
ATT_WIDTH = D_MODEL // 2
SGU_WIDTH = D_MODEL // 4
LRU_WIDTH = D_MODEL // 4
N_ATT_HEADS = ATT_WIDTH // HEAD_DIM
ATT_HALF = HEAD_DIM // 2
ROPE_FREQS = ATT_HALF // 4
ROPE_THETA = 10000.0
CHUNK = 128
N_SGU_GROUPS = 4
SGU_GROUP = SGU_WIDTH // N_SGU_GROUPS
N_LRU_BLOCKS = 4
LRU_BLOCK = LRU_WIDTH // N_LRU_BLOCKS
CONV_W = 4
LRU_C = 8.0
IN_COLS = 3 * ATT_WIDTH + 2 * SGU_WIDTH + 2 * LRU_WIDTH
N_EXPERTS = 32
TOP_K = 4
D_EXPERT = D_MODEL
SWIGLU_LIMIT = 7.0
SWIGLU_ALPHA = 1.702
DN_ALPHA = (2 * DEPTH) ** 0.25
EPS = 1e-5

N_CTX = BATCH * SEQ
N_DEC = DEC_BATCH * DEC_SEQ
N_TOK = N_CTX + N_DEC

LANES = 128
SUBLANES = 8
ROW_SLABS = D_MODEL // LANES

TM = 512
CTX_TILES = N_CTX // TM
TILES_PER_DEC = DEC_SEQ // TM
N_TILES = N_TOK // TM
MOD_TN = 512
RT = 256
N_RT = N_TOK // RT
RT_PER_TM = TM // RT
RT_ROWS = RT * TOP_K
RT_PER_STEP = 4
FETCH_AHEAD = 2
BM = 256
NB = N_TOK * TOP_K // BM + N_EXPERTS
WEIGHT_PIECES = 4
N_SLOTS = NB * BM
RUN_BITS = tuple(1 << b for b in range(RT.bit_length() - 1, -1, -1))
PAD_BITS = tuple(1 << b for b in range(BM.bit_length() - 2, -1, -1))
VMEM_LIMIT = 56 * 1024 * 1024


def _cparams(n_axes):
    return pltpu.CompilerParams(
        dimension_semantics=("arbitrary",) * n_axes,
        vmem_limit_bytes=VMEM_LIMIT)


def _mod_row(i):
    return jnp.where(i < CTX_TILES, 0, 1 + (i - CTX_TILES) // TILES_PER_DEC)


def _layer_norm_rows(z, g, b):
    mu = jnp.mean(z, axis=-1, keepdims=True)
    zc = z - mu
    var = jnp.mean(zc * zc, axis=-1, keepdims=True)
    return zc * lax.rsqrt(var + EPS) * g + b


def _pair_specs(tile, width, n_ctx_tiles):
    ctx = pl.BlockSpec((tile, width), lambda i, *_: (jnp.minimum(i, n_ctx_tiles - 1), 0))
    dec = pl.BlockSpec((tile, width), lambda i, *_: (jnp.maximum(i - n_ctx_tiles, 0), 0))
    return [ctx, dec]


def _pair_read(i, n_ctx_tiles, ctx_ref, dec_ref):
    return jnp.where(i < n_ctx_tiles, ctx_ref[...], dec_ref[...])


def _mod_kernel(cvec_ref, w_ref, b_ref, o_ref):
    cv = cvec_ref[...]
    s = cv * jax.nn.sigmoid(cv)
    s_t = s.T
    w = w_ref[0]
    rows = [jnp.sum(s_t[:, r:r + 1] * w, axis=0, keepdims=True) for r in range(1 + DEC_BATCH)]
    rows.append(jnp.zeros((SUBLANES - 1 - DEC_BATCH, MOD_TN), F32))
    o_ref[0] = jnp.concatenate(rows, axis=0) + b_ref[0]


def _modulation(cvec8, w_mod, b_mod):
    n_out = w_mod.shape[-1]
    return pl.pallas_call(
        _mod_kernel,
        grid=(DEPTH, n_out // MOD_TN),
        in_specs=[
            pl.BlockSpec((SUBLANES, D_MODEL), lambda l, j: (0, 0)),
            pl.BlockSpec((1, D_MODEL, MOD_TN), lambda l, j: (l, 0, j)),
            pl.BlockSpec((1, 1, MOD_TN), lambda l, j: (l, 0, j)),
        ],
        out_specs=pl.BlockSpec((1, SUBLANES, MOD_TN), lambda l, j: (l, 0, j)),
        out_shape=jax.ShapeDtypeStruct((DEPTH, SUBLANES, n_out), F32),
        compiler_params=_cparams(2),
        name="modulation",
    )(cvec8, w_mod, b_mod.reshape(DEPTH, 1, n_out))


def _inproj_kernel(xc_ref, xd_ref, mod_ref, w_ref, o_ref, wbf_ref):
    i = pl.program_id(0)

    @pl.when(i == 0)
    def _():
        wbf_ref[...] = w_ref[...].astype(BF16)

    m = mod_ref[pl.ds(_mod_row(i), 1), :]
    sh1 = m[:, 0:D_MODEL]
    sc1 = m[:, D_MODEL:2 * D_MODEL]
    x = _pair_read(i, CTX_TILES, xc_ref, xd_ref)
    h = (x * (1.0 + sc1) + sh1).astype(BF16)
    o_ref[...] = jnp.dot(h, wbf_ref[...], preferred_element_type=F32)


def _input_projection(x_ctx, x_dec, mod, w_in, li):
    return pl.pallas_call(
        _inproj_kernel,
        grid=(N_TILES,),
        in_specs=_pair_specs(TM, D_MODEL, CTX_TILES) + [
            pl.BlockSpec((None, SUBLANES, 6 * D_MODEL), lambda i: (li, 0, 0)),
            pl.BlockSpec((None, D_MODEL, IN_COLS), lambda i: (li, 0, 0), pipeline_mode=pl.Buffered(1)),
        ],
        out_specs=pl.BlockSpec((TM, IN_COLS), lambda i: (i, 0)),
        out_shape=jax.ShapeDtypeStruct((N_TOK, IN_COLS), F32),
        scratch_shapes=[pltpu.VMEM((D_MODEL, IN_COLS), BF16)],
        compiler_params=_cparams(1),
        name="input_projection",
    )(x_ctx, x_dec, mod, w_in)


def _lambda(lam_ref, lam_init):
    lm = lam_ref[...]
    a = jnp.sum(lm[0:1] * lm[1:2], axis=-1, keepdims=True)
    b = jnp.sum(lm[2:3] * lm[3:4], axis=-1, keepdims=True)
    return jnp.exp(a) - jnp.exp(b) + lam_init


_NT = (((1,), (1,)), ((), ()))
_TN = (((0,), (0,)), ((), ()))
_ATT_SCALE = ATT_HALF ** -0.5
_Q_SCALE = _ATT_SCALE * math.log2(math.e)
KEY_CHUNK = 128


def _attention_heads(q_parts, k_segs, values, s_scr, e_scr, lam, g_col, lam_init):
    n_heads = len(q_parts)
    for h in range(n_heads):
        for c in range(2):
            row = 0
            for k in k_segs[h][c]:
                s_scr[2 * h + c, row:row + k.shape[0], :] = lax.dot_general(
                    k, q_parts[h][c], _NT, preferred_element_type=F32)
                row += k.shape[0]
    n_keys, n_q = s_scr.shape[1:]
    chunks = [slice(r, r + KEY_CHUNK) for r in range(0, n_keys, KEY_CHUNK)]
    for n in range(2 * n_heads):
        tops = [jnp.max(s_scr[n, rows, :].reshape(KEY_CHUNK // SUBLANES, SUBLANES, n_q), axis=0)
                for rows in chunks]
        top = jnp.max(functools.reduce(jnp.maximum, tops), axis=0, keepdims=True)
        for rows in chunks:
            e_scr[n, rows, :] = jnp.exp2(s_scr[n, rows, :] - top).astype(BF16)
    outs = []
    for h in range(n_heads):
        v_aug = jnp.concatenate([values[h], jnp.ones_like(values[h])], axis=-1)
        normed = []
        for c in range(2):
            acc = lax.dot_general(v_aug, e_scr[2 * h + c], _TN, preferred_element_type=F32)
            normed.append(acc[:HEAD_DIM] * (1.0 / acc[HEAD_DIM:HEAD_DIM + 1]))
        o_t = normed[0] - lam * normed[1]
        ms = jnp.mean(o_t * o_t, axis=0, keepdims=True)
        outs.append(o_t * lax.rsqrt(ms + EPS) * g_col * (1.0 - lam_init))
    return outs


def _head_cols(h):
    lo = h * HEAD_DIM
    return [slice(lo + c * ATT_HALF, lo + (c + 1) * ATT_HALF) for c in range(2)]


def _ctx_attn_kernel(lam_ref, g_ref, q_ref, k_ref, v_ref, *rest, lam_init, stacked):
    *rest, s_scr, e_scr = rest
    if stacked:
        pk_ref, pv_ref, att_ref, ck_ref, cv_ref = rest
        ck_ref[0, 0] = pk_ref[0]
        cv_ref[0, 0] = pv_ref[0]
        put_k = lambda h, val: ck_ref.__setitem__((0, 1, h), val)
        put_v = lambda h, val: cv_ref.__setitem__((0, 1, h), val)
    else:
        att_ref, ck_ref, cv_ref = rest
        put_k = lambda h, val: ck_ref.__setitem__((0, h), val)
        put_v = lambda h, val: cv_ref.__setitem__((0, h), val)
    lam = _lambda(lam_ref, lam_init)
    k = k_ref[...]
    v = v_ref[...]
    qb = (q_ref[...] * _Q_SCALE).astype(BF16)
    kb = k.astype(BF16)
    vb = v.astype(BF16)
    heads = range(N_ATT_HEADS)
    outs = _attention_heads([[qb[:, cs] for cs in _head_cols(h)] for h in heads],
                            [[[kb[:, cs]] for cs in _head_cols(h)] for h in heads],
                            [vb[:, h * HEAD_DIM:(h + 1) * HEAD_DIM] for h in heads],
                            s_scr, e_scr, lam, g_ref[...], lam_init)
    for h in heads:
        put_k(h, k[:, h * HEAD_DIM:(h + 1) * HEAD_DIM])
        put_v(h, v[:, h * HEAD_DIM:(h + 1) * HEAD_DIM])
    att_ref[...] = jnp.concatenate(outs, axis=0).T


def _context_attention(proj, lam, subln_g, lam_init, prev_kv=None):
    stacked = prev_kv is not None
    per_layer = pl.BlockSpec((1, N_ATT_HEADS, SEQ, HEAD_DIM), lambda b: (b, 0, 0, 0))
    if stacked:
        kv_shape = jax.ShapeDtypeStruct((BATCH, DEPTH, N_ATT_HEADS, SEQ, HEAD_DIM), F32)
        kv_spec = pl.BlockSpec((1, DEPTH, N_ATT_HEADS, SEQ, HEAD_DIM), lambda b: (b, 0, 0, 0, 0))
    else:
        kv_shape = jax.ShapeDtypeStruct((BATCH, N_ATT_HEADS, SEQ, HEAD_DIM), F32)
        kv_spec = per_layer
    return pl.pallas_call(
        functools.partial(_ctx_attn_kernel, lam_init=lam_init, stacked=stacked),
        grid=(BATCH,),
        in_specs=[
            pl.BlockSpec((4, ATT_HALF), lambda b: (0, 0)),
            pl.BlockSpec((HEAD_DIM, 1), lambda b: (0, 0)),
            pl.BlockSpec((SEQ, ATT_WIDTH), lambda b: (b, 0)),
            pl.BlockSpec((SEQ, ATT_WIDTH), lambda b: (b, 1)),
            pl.BlockSpec((SEQ, ATT_WIDTH), lambda b: (b, 2)),
        ] + ([per_layer, per_layer] if stacked else []),
        out_specs=[pl.BlockSpec((SEQ, ATT_WIDTH), lambda b: (b, 0)), kv_spec, kv_spec],
        out_shape=[jax.ShapeDtypeStruct((N_CTX, ATT_WIDTH), F32), kv_shape, kv_shape],
        scratch_shapes=[pltpu.VMEM((2 * N_ATT_HEADS, SEQ, SEQ), F32),
                        pltpu.VMEM((2 * N_ATT_HEADS, SEQ, SEQ), BF16)],
        compiler_params=_cparams(1),
        name="context_attention",
    )(lam, subln_g.reshape(HEAD_DIM, 1), proj, proj, proj, *(prev_kv if stacked else ()))


def _rope_tables():
    t = np.arange(DEC_SEQ)
    pos = np.stack([t // GRID_W, t % GRID_W], axis=1).astype(np.float32)
    inv = (np.float32(ROPE_THETA) ** (-np.arange(ROPE_FREQS, dtype=np.float32) / np.float32(ROPE_FREQS)))
    j = np.arange(HEAD_DIM)
    d = j % ATT_HALF
    axis = d // (2 * ROPE_FREQS)
    u = d % (2 * ROPE_FREQS)
    ang = pos[:, axis] * inv[u % ROPE_FREQS][None, :].astype(np.float32)
    cos = np.cos(ang).astype(np.float32)
    sin = np.sin(ang).astype(np.float32)
    first = (u < ROPE_FREQS)[None, :]
    s_next = np.where(first, -sin, 0.0).astype(np.float32)
    s_prev = np.where(first, 0.0, sin).astype(np.float32)
    tile = lambda a: jnp.asarray(np.tile(a, (1, N_ATT_HEADS)))
    return tile(cos), tile(s_next), tile(s_prev)


def _rotate(x, cos, s_next, s_prev):
    width = x.shape[-1]
    return (x * cos + pltpu.roll(x, width - ROPE_FREQS, axis=1) * s_next
            + pltpu.roll(x, ROPE_FREQS, axis=1) * s_prev)


QB = 256
Q_STEPS = DEC_SEQ // QB
DEC_HEAD_GROUP = 2


def _dec_attn_kernel(lam_ref, g_ref, q_ref, k_ref, v_ref, ck_ref, cv_ref,
                     cq_ref, snq_ref, spq_ref, ck_tab, snk_tab, spk_tab,
                     att_ref, krot_ref, s_scr, e_scr, *, lam_init):
    j = pl.program_id(1)

    @pl.when(j == 0)
    def _():
        krot_ref[...] = _rotate(k_ref[...], ck_tab[...], snk_tab[...], spk_tab[...]).astype(BF16)

    lam = _lambda(lam_ref, lam_init)
    qb = (_rotate(q_ref[...], cq_ref[...], snq_ref[...], spq_ref[...]) * _Q_SCALE).astype(BF16)
    kb = krot_ref[...]
    vb = v_ref[...].astype(BF16)
    outs = []
    for h0 in range(0, N_ATT_HEADS, DEC_HEAD_GROUP):
        heads = range(h0, h0 + DEC_HEAD_GROUP)
        past_k = [ck_ref[0, 0, h].astype(BF16) for h in heads]
        k_segs = [[[pk[:, c * ATT_HALF:(c + 1) * ATT_HALF], kb[:, _head_cols(h)[c]]] for c in range(2)]
                  for h, pk in zip(heads, past_k)]
        values = [jnp.concatenate([cv_ref[0, 0, h].astype(BF16), vb[:, h * HEAD_DIM:(h + 1) * HEAD_DIM]], axis=0)
                  for h in heads]
        outs += _attention_heads([[qb[:, cs] for cs in _head_cols(h)] for h in heads], k_segs, values,
                                 s_scr, e_scr, lam, g_ref[...], lam_init)
    att_ref[...] = jnp.concatenate(outs, axis=0).T


def _denoise_attention(proj, cache_k, cache_v, lam, subln_g, li, lam_init, tables):
    cos, s_next, s_prev = tables
    row0 = N_CTX // QB
    seq0 = N_CTX // DEC_SEQ
    q_tab = pl.BlockSpec((QB, ATT_WIDTH), lambda b, j: (j, 0))
    k_tab = pl.BlockSpec((DEC_SEQ, ATT_WIDTH), lambda b, j: (0, 0))
    cache_spec = pl.BlockSpec((1, 1, N_ATT_HEADS, PAST_LEN, HEAD_DIM), lambda b, j: (b, li, 0, 0, 0))
    return pl.pallas_call(
        functools.partial(_dec_attn_kernel, lam_init=lam_init),
        grid=(DEC_BATCH, Q_STEPS),
        in_specs=[
            pl.BlockSpec((4, ATT_HALF), lambda b, j: (0, 0)),
            pl.BlockSpec((HEAD_DIM, 1), lambda b, j: (0, 0)),
            pl.BlockSpec((QB, ATT_WIDTH), lambda b, j: (row0 + b * Q_STEPS + j, 0)),
            pl.BlockSpec((DEC_SEQ, ATT_WIDTH), lambda b, j: (seq0 + b, 1)),
            pl.BlockSpec((DEC_SEQ, ATT_WIDTH), lambda b, j: (seq0 + b, 2)),
            cache_spec, cache_spec,
            q_tab, q_tab, q_tab, k_tab, k_tab, k_tab,
        ],
        out_specs=pl.BlockSpec((QB, ATT_WIDTH), lambda b, j: (b * Q_STEPS + j, 0)),
        out_shape=jax.ShapeDtypeStruct((N_DEC, ATT_WIDTH), F32),
        scratch_shapes=[pltpu.VMEM((DEC_SEQ, ATT_WIDTH), BF16),
                        pltpu.VMEM((2 * DEC_HEAD_GROUP, PAST_LEN + DEC_SEQ, QB), F32),
                        pltpu.VMEM((2 * DEC_HEAD_GROUP, PAST_LEN + DEC_SEQ, QB), BF16)],
        compiler_params=_cparams(2),
        name="denoise_attention",
    )(lam, subln_g.reshape(HEAD_DIM, 1), proj, proj, proj, cache_k, cache_v,
      cos, s_next, s_prev, cos, s_next, s_prev)


def _softplus(z):
    return jnp.maximum(z, 0.0) + jnp.log1p(jnp.exp(-jnp.abs(z)))


def _mixer_kernel(su_ref, sv_ref, rx_ref, rg_ref, h0_ref, lng_ref, lnb_ref, ws_ref, bs_ref,
                  cw_ref, cb_ref, wrg_ref, brg_ref, lrulog_ref,
                  sgu_ref, lru_ref, hlast_ref, a_scr, b_scr, h_scr, *, seq_len):
    vn = _layer_norm_rows(sv_ref[...], lng_ref[...], lnb_ref[...])
    lane_group = lax.broadcasted_iota(I32, (CHUNK, SGU_WIDTH), 1) // SGU_GROUP
    for n in range(seq_len // CHUNK):
        rows = slice(n * CHUNK, (n + 1) * CHUNK)
        vc = vn[rows].astype(BF16)
        s = jnp.zeros((CHUNK, SGU_WIDTH), F32)
        for g in range(N_SGU_GROUPS):
            sg = jnp.dot(ws_ref[g].astype(BF16), vc, preferred_element_type=F32)
            s = jnp.where(lane_group == g, sg, s)
        sgu_ref[rows, :] = su_ref[rows, :] * (s + bs_ref[...])

    x = rx_ref[...]
    row = lax.broadcasted_iota(I32, (seq_len, LRU_WIDTH), 0)

    def shifted(val, d, fill):
        rolled = pltpu.roll(val, d % seq_len, axis=0)
        inside = (row >= d) if d > 0 else (row < seq_len + d)
        return jnp.where(inside, rolled, fill)

    left = CONV_W // 2
    xc = cb_ref[...] + x * cw_ref[left:left + 1, :]
    for tap in range(CONV_W):
        if tap != left:
            xc = xc + shifted(x, left - tap, 0.0) * cw_ref[tap:tap + 1, :]
    gates = jax.nn.sigmoid(jnp.dot(xc.astype(BF16), wrg_ref[...].astype(BF16),
                                   preferred_element_type=F32) + brg_ref[...])
    in_chunk = row % SUBLANES
    n_chunks = seq_len // SUBLANES
    h0 = h0_ref[0, 0]
    lasts = []
    for direction in range(2):
        reverse = direction == 1
        base = direction * 2 * LRU_WIDTH
        r = gates[:, base:base + LRU_WIDTH]
        gi = gates[:, base + LRU_WIDTH:base + 2 * LRU_WIDTH]
        log_a = -LRU_C * r * _softplus(-lrulog_ref[direction:direction + 1, :])
        a = jnp.exp(log_a)
        b = jnp.sqrt(-jnp.tanh(log_a) * (a * a + 1.0)) * gi * xc
        for d in (1, 2, 4):
            if reverse:
                inside = in_chunk < SUBLANES - d
                a_n = jnp.where(inside, pltpu.roll(a, seq_len - d, axis=0), 1.0)
                b_n = jnp.where(inside, pltpu.roll(b, seq_len - d, axis=0), 0.0)
            else:
                inside = in_chunk >= d
                a_n = jnp.where(inside, pltpu.roll(a, d, axis=0), 1.0)
                b_n = jnp.where(inside, pltpu.roll(b, d, axis=0), 0.0)
            b = a * b_n + b
            a = a * a_n
        a_scr[...] = a
        b_scr[...] = b

        def chunk_step(c, carry, reverse=reverse):
            cc = n_chunks - 1 - c if reverse else c
            off = pl.multiple_of(cc * SUBLANES, SUBLANES)
            hc = a_scr[pl.ds(off, SUBLANES), :] * carry + b_scr[pl.ds(off, SUBLANES), :]
            if reverse:
                h_scr[pl.ds(off, SUBLANES), :] = h_scr[pl.ds(off, SUBLANES), :] + hc
                return hc[0:1, :]
            h_scr[pl.ds(off, SUBLANES), :] = hc
            return hc[SUBLANES - 1:SUBLANES, :]

        lasts.append(lax.fori_loop(0, n_chunks, chunk_step, h0[direction:direction + 1, :]))
    lru_ref[...] = h_scr[...] * jax.nn.gelu(rg_ref[...])
    hlast_ref[0] = jnp.concatenate(lasts, axis=0)


def _mixers(proj, h0, h0_layer, lp, seq_len, n_seq, row_block0):
    col0 = 3 * ATT_WIDTH // SGU_WIDTH
    col = lambda c: pl.BlockSpec((seq_len, SGU_WIDTH), lambda b: (row_block0 + b, col0 + c))
    full = lambda shape: pl.BlockSpec(shape, lambda b: (0,) * len(shape))
    out_rows = pl.BlockSpec((seq_len, SGU_WIDTH), lambda b: (b, 0))
    return pl.pallas_call(
        functools.partial(_mixer_kernel, seq_len=seq_len),
        grid=(n_seq,),
        in_specs=[
            col(0), col(1), col(2), col(3),
            pl.BlockSpec((1, 1, 2, LRU_WIDTH), lambda b: (b, h0_layer, 0, 0)),
            full((1, SGU_WIDTH)), full((1, SGU_WIDTH)),
            full((N_SGU_GROUPS, CHUNK, CHUNK)), full((CHUNK, SGU_WIDTH)),
            full((CONV_W, LRU_WIDTH)), full((1, LRU_WIDTH)),
            full((LRU_WIDTH, 4 * LRU_WIDTH)), full((1, 4 * LRU_WIDTH)),
            full((2, LRU_WIDTH)),
        ],
        out_specs=[out_rows, out_rows, pl.BlockSpec((1, 2, LRU_WIDTH), lambda b: (b, 0, 0))],
        out_shape=[
            jax.ShapeDtypeStruct((n_seq * seq_len, SGU_WIDTH), F32),
            jax.ShapeDtypeStruct((n_seq * seq_len, LRU_WIDTH), F32),
            jax.ShapeDtypeStruct((n_seq, 2, LRU_WIDTH), F32),
        ],
        scratch_shapes=[pltpu.VMEM((seq_len, LRU_WIDTH), F32)] * 3,
        compiler_params=_cparams(1),
        name="mixers_%d" % seq_len,
    )(proj, proj, proj, proj, h0, lp["sgu_ln_g"], lp["sgu_ln_b"], lp["w_spatial"], lp["b_spatial_full"],
      lp["conv_w"], lp["conv_b"], lp["w_rg_full"], lp["b_rg_full"], lp["lru_log"])


def _split_bf16(v):
    hi = v.astype(BF16)
    lo = (v - hi.astype(F32)).astype(BF16)
    return hi, lo


def _outproj_kernel(xc_ref, xd_ref, ac_ref, ad_ref, sc_ref, sd_ref, lc_ref, ld_ref,
                    mod_ref, wout_ref, lng_ref, lnb_ref, wr_ref, br_ref,
                    x1_ref, h2_ref, ri_ref, rf_ref, cnt_ref, wbf_ref):
    i = pl.program_id(0)

    @pl.when(i == 0)
    def _():
        wbf_ref[...] = wout_ref[...].astype(BF16)

    m = mod_ref[pl.ds(_mod_row(i), 1), :]
    g1 = m[:, 2 * D_MODEL:3 * D_MODEL]
    sh2 = m[:, 3 * D_MODEL:4 * D_MODEL]
    sc2 = m[:, 4 * D_MODEL:5 * D_MODEL]
    a0, a1 = ATT_WIDTH, ATT_WIDTH + SGU_WIDTH
    x = _pair_read(i, CTX_TILES, xc_ref, xd_ref)
    att = _pair_read(i, CTX_TILES, ac_ref, ad_ref).astype(BF16)
    sgu = _pair_read(i, CTX_TILES, sc_ref, sd_ref).astype(BF16)
    lru = _pair_read(i, CTX_TILES, lc_ref, ld_ref).astype(BF16)
    mix = (jnp.dot(att, wbf_ref[0:a0, :], preferred_element_type=F32)
           + jnp.dot(sgu, wbf_ref[a0:a1, :], preferred_element_type=F32)
           + jnp.dot(lru, wbf_ref[a1:, :], preferred_element_type=F32))
    x1 = _layer_norm_rows(DN_ALPHA * x + g1 * mix, lng_ref[...], lnb_ref[...])
    x1_ref[...] = x1
    h2 = x1 * (1.0 + sc2) + sh2
    h2_ref[...] = h2.astype(BF16)

    h_hi, h_lo = _split_bf16(h2)
    w_hi, w_lo = _split_bf16(wr_ref[...])
    logits = (jnp.dot(h_hi, w_hi, preferred_element_type=F32)
              + jnp.dot(h_lo, w_hi, preferred_element_type=F32)
              + jnp.dot(h_hi, w_lo, preferred_element_type=F32)) + br_ref[...]
    lane = lax.broadcasted_iota(I32, (TM, LANES), 1)
    lane_f = lane.astype(F32)
    neg_inf = jnp.float32(-jnp.inf)
    work = jnp.where(lane < N_EXPERTS, logits, neg_inf)
    vals, idxs = [], []
    for _ in range(TOP_K):
        top = jnp.max(work, axis=-1, keepdims=True)
        idx = jnp.min(jnp.where(work == top, lane_f, float(LANES)), axis=-1, keepdims=True)
        vals.append(top)
        idxs.append(idx)
        work = jnp.where(lane_f == idx, neg_inf, work)
    exps = [jnp.exp(v - vals[0]) for v in vals]
    denom = exps[0] + exps[1] + exps[2] + exps[3]
    onehot = jnp.zeros((TM, LANES), F32)
    for idx in idxs:
        onehot = onehot + (lane_f == idx).astype(F32)
    r_i = lax.broadcasted_iota(I32, (RT, RT), 0)
    c_i = lax.broadcasted_iota(I32, (RT, RT), 1)
    tri = (r_i > c_i).astype(F32).astype(BF16)
    before = jnp.concatenate(
        [jnp.dot(tri, onehot[t * RT:(t + 1) * RT].astype(BF16), preferred_element_type=F32)
         for t in range(RT_PER_TM)], axis=0)
    ri = jnp.zeros((TM, LANES), F32)
    rf = jnp.zeros((TM, LANES), F32)
    for k in range(TOP_K):
        rank = jnp.sum(jnp.where(lane_f == idxs[k], before, 0.0), axis=-1, keepdims=True)
        ri = jnp.where(lane == k, idxs[k], ri)
        ri = jnp.where(lane == TOP_K + k, rank, ri)
        rf = jnp.where(lane == k, exps[k] / denom, rf)
    ri_ref[...] = ri.astype(I32)
    rf_ref[...] = rf
    for t in range(RT_PER_TM):
        total = jnp.sum(onehot[t * RT:(t + 1) * RT], axis=0, keepdims=True)
        cnt_ref[t] = jnp.broadcast_to(total, (SUBLANES, LANES)).astype(I32)


def _output_projection(x_pair, att_pair, sgu_pair, lru_pair, mod, lp, li):
    rows = lambda w: pl.BlockSpec((TM, w), lambda i: (i, 0))
    full = lambda shape: pl.BlockSpec(shape, lambda i: (0,) * len(shape))
    return pl.pallas_call(
        _outproj_kernel,
        grid=(N_TILES,),
        in_specs=(_pair_specs(TM, D_MODEL, CTX_TILES) + _pair_specs(TM, ATT_WIDTH, CTX_TILES)
                  + _pair_specs(TM, SGU_WIDTH, CTX_TILES) + _pair_specs(TM, LRU_WIDTH, CTX_TILES) + [
            pl.BlockSpec((None, SUBLANES, 6 * D_MODEL), lambda i: (li, 0, 0)),
            pl.BlockSpec((None, D_MODEL, D_MODEL), lambda i: (li, 0, 0), pipeline_mode=pl.Buffered(1)),
            full((1, D_MODEL)), full((1, D_MODEL)),
            full((D_MODEL, LANES)), full((1, LANES)),
        ]),
        out_specs=[
            rows(D_MODEL), rows(D_MODEL), rows(LANES), rows(LANES),
            pl.BlockSpec((RT_PER_TM, SUBLANES, LANES), lambda i: (i, 0, 0)),
        ],
        out_shape=[
            jax.ShapeDtypeStruct((N_TOK, D_MODEL), F32),
            jax.ShapeDtypeStruct((N_TOK, D_MODEL), BF16),
            jax.ShapeDtypeStruct((N_TOK, LANES), I32),
            jax.ShapeDtypeStruct((N_TOK, LANES), F32),
            jax.ShapeDtypeStruct((N_RT, SUBLANES, LANES), I32),
        ],
        scratch_shapes=[pltpu.VMEM((D_MODEL, D_MODEL), BF16)],
        compiler_params=_cparams(1),
        name="output_projection",
    )(*x_pair, *att_pair, *sgu_pair, *lru_pair, mod, lp["w_out_all"], lp["ln_g1"], lp["ln_b1"],
      lp["w_router_pad"], lp["b_router_pad"])


def _row_ds(row, n_rows):
    return pl.ds(pl.multiple_of(row * ROW_SLABS, ROW_SLABS), n_rows * ROW_SLABS)


def _for_each_run_piece(length, bits, fn):
    for bit in bits:
        done = length & (-2 * bit)
        @pl.when((length & bit) != 0)
        def _(done=done, bit=bit):
            fn(done, bit)


def _tile_run_copies(tile, start_ref, len_ref, off_ref, sorted_hbm, buf, sem, to_sorted, live=None):
    for e in range(N_EXPERTS):
        t = tile * N_EXPERTS + e
        start, length, off = start_ref[t], len_ref[t], off_ref[t]
        if live is not None:
            length = jnp.where(live, length, 0)

        def piece(done, bit, start=start, off=off):
            packed = buf.at[_row_ds(off + done, bit), :]
            srt = sorted_hbm.at[_row_ds(start + done, bit), :]
            if to_sorted:
                pltpu.make_async_copy(packed, srt, sem).start()
            else:
                pltpu.make_async_copy(srt, packed, sem).start()

        _for_each_run_piece(length, RUN_BITS, piece)


def _tile_runs_wait(buf, sem):
    pltpu.make_async_copy(buf, buf, sem).wait()


def _packed_positions(ri, off_row):
    lane = lax.broadcasted_iota(I32, (RT, LANES), 1)
    pos = []
    for k in range(TOP_K):
        e_k = ri[:, k:k + 1]
        base = jnp.sum(jnp.where(lane == e_k, off_row, 0.0), axis=-1, keepdims=True)
        pos.append(base + ri[:, TOP_K + k:TOP_K + k + 1].astype(F32))
    return pos


def _slab_columns(buf, row0, n_rows):
    return jnp.concatenate(
        [buf[pl.ds(row0 * ROW_SLABS + s, n_rows, stride=ROW_SLABS), :] for s in range(ROW_SLABS)], axis=-1)


def _dispatch_kernel(start_ref, len_ref, off_ref, pstart_ref, plen_ref, nv_ref,
                     h2_ref, ri_ref, offrow_ref, xs_hbm, *scratch):
    bufs, (zbuf, sem, zsem) = scratch[:RT_PER_STEP], scratch[RT_PER_STEP:]
    i = pl.program_id(0)
    n_steps = pl.num_programs(0)

    @pl.when(i == 0)
    def _():
        zbuf[...] = jnp.zeros_like(zbuf)

        def zero_fill(wait):
            def go(cp):
                cp.wait() if wait else cp.start()

            def per_expert(e, carry):
                def piece(done, bit):
                    go(pltpu.make_async_copy(zbuf.at[pl.ds(0, bit * ROW_SLABS), :],
                                             xs_hbm.at[_row_ds(pstart_ref[e] + done, bit), :], zsem))
                _for_each_run_piece(plen_ref[e], PAD_BITS, piece)
                return carry
            lax.fori_loop(0, N_EXPERTS, per_expert, 0)

            def per_block(b, carry):
                go(pltpu.make_async_copy(zbuf, xs_hbm.at[_row_ds(b * BM, BM), :], zsem))
                return carry
            lax.fori_loop(nv_ref[0], NB, per_block, 0)

        zero_fill(False)
        zero_fill(True)

    def send(tile, t, live):
        _tile_run_copies(tile, start_ref, len_ref, off_ref, xs_hbm, bufs[t], sem.at[t], to_sorted=True, live=live)

    col = lax.broadcasted_iota(I32, (RT, RT_ROWS), 1).astype(F32)
    for t in range(RT_PER_STEP):
        buf = bufs[t]
        rows = slice(t * RT, (t + 1) * RT)
        tile = i * RT_PER_STEP + t

        @pl.when(i >= 1)
        def _(buf=buf, t=t):
            _tile_runs_wait(buf, sem.at[t])

        send(jnp.maximum(tile - 1, 0), (t - 1) % RT_PER_STEP, tile >= 1)
        pos = _packed_positions(ri_ref[rows, :], offrow_ref[t, 0:1, :])
        sel = jnp.zeros((RT, RT_ROWS), F32)
        for p in pos:
            sel = sel + (col == p).astype(F32)
        packed = lax.dot_general(sel.astype(BF16), h2_ref[rows, :], _TN, preferred_element_type=F32)
        for s in range(ROW_SLABS):
            buf[pl.ds(s, RT_ROWS, stride=ROW_SLABS), :] = packed[:, s * LANES:(s + 1) * LANES]

    @pl.when(i == n_steps - 1)
    def _():
        send(N_RT - 1, RT_PER_STEP - 1, True)
        for t in range(RT_PER_STEP):
            _tile_runs_wait(bufs[t], sem.at[t])


def _dispatch(tables, h2, route_i):
    grid_spec = pltpu.PrefetchScalarGridSpec(
        num_scalar_prefetch=6,
        grid=(N_RT // RT_PER_STEP,),
        in_specs=[
            pl.BlockSpec((RT_PER_STEP * RT, D_MODEL), lambda i, *_: (i, 0)),
            pl.BlockSpec((RT_PER_STEP * RT, LANES), lambda i, *_: (i, 0)),
            pl.BlockSpec((RT_PER_STEP, SUBLANES, LANES), lambda i, *_: (i, 0, 0)),
        ],
        out_specs=pl.BlockSpec(memory_space=pl.ANY),
        scratch_shapes=[pltpu.VMEM((RT_ROWS * ROW_SLABS, LANES), F32)] * RT_PER_STEP + [
            pltpu.VMEM((BM * ROW_SLABS, LANES), F32),
            pltpu.SemaphoreType.DMA((RT_PER_STEP,)),
            pltpu.SemaphoreType.DMA(()),
        ],
    )
    return pl.pallas_call(
        _dispatch_kernel,
        grid_spec=grid_spec,
        out_shape=jax.ShapeDtypeStruct((N_SLOTS * ROW_SLABS, LANES), F32),
        compiler_params=_cparams(1),
        name="moe_dispatch",
    )(tables["start"], tables["len"], tables["off"], tables["pad_start"], tables["pad_len"], tables["n_valid"],
      h2, route_i, tables["off_rows"])


def _moe_kernel(be_ref, eo_ref, nv_ref, xs_hbm, wgu_hbm, bgu_ref, wdn_hbm, bdn_ref, y_hbm,
                xbuf, ybuf, gu_stage, dn_stage, wgu_bf, wdn_bf, wsem, xsem, ysem, *, li):
    n_valid = nv_ref[0]
    rows_per_piece = D_MODEL // WEIGHT_PIECES

    def weight_copies(e, stage):
        cps = []
        for p in range(WEIGHT_PIECES):
            band = pl.ds(p * rows_per_piece, rows_per_piece)
            cps.append(pltpu.make_async_copy(wgu_hbm.at[li, e, band, :], gu_stage.at[stage, band, :], wsem.at[stage]))
            cps.append(pltpu.make_async_copy(wdn_hbm.at[li, e, band, :], dn_stage.at[stage, band, :], wsem.at[stage]))
        return cps

    def fetch(k):
        @pl.when(eo_ref[k] >= 0)
        def _():
            for cp in weight_copies(eo_ref[k], k % 2):
                cp.start()

    def x_copy(b, slot):
        return pltpu.make_async_copy(xs_hbm.at[_row_ds(b * BM, BM), :], xbuf.at[slot], xsem.at[slot])

    def y_copy(b, slot):
        return pltpu.make_async_copy(ybuf.at[slot], y_hbm.at[_row_ds(b * BM, BM), :], ysem.at[slot])

    fetch(0)
    fetch(1)
    x_copy(0, 0).start()

    def block(b, k):
        slot = b % 2
        e = be_ref[b]
        new_expert = jnp.logical_or(b == 0, e != be_ref[jnp.maximum(b - 1, 0)])

        @pl.when(b + 1 < n_valid)
        def _():
            x_copy(b + 1, 1 - slot).start()

        @pl.when(new_expert)
        def _():
            stage = k % 2
            for cp in weight_copies(e, stage):
                cp.wait()
            for st in range(2):
                @pl.when(stage == st)
                def _(st=st):
                    wgu_bf[...] = gu_stage[st].astype(BF16)
                    wdn_bf[...] = dn_stage[st].astype(BF16)
            fetch(k + 2)

        x_copy(b, slot).wait()

        @pl.when(b >= 2)
        def _():
            y_copy(b - 2, slot).wait()

        x = _slab_columns(xbuf.at[slot], 0, BM).astype(BF16)
        bgu = bgu_ref[e]
        g = jnp.dot(x, wgu_bf[:, :D_EXPERT], preferred_element_type=F32) + bgu[:, :D_EXPERT]
        u = jnp.dot(x, wgu_bf[:, D_EXPERT:], preferred_element_type=F32) + bgu[:, D_EXPERT:]
        g = jnp.minimum(g, SWIGLU_LIMIT)
        u = jnp.clip(u, -SWIGLU_LIMIT, SWIGLU_LIMIT)
        act = ((u + 1.0) * (0.5 * g * (1.0 + jnp.tanh((0.5 * SWIGLU_ALPHA) * g)))).astype(BF16)
        y = jnp.dot(act, wdn_bf[...], preferred_element_type=F32) + bdn_ref[e]
        out = ybuf.at[slot]
        for s in range(ROW_SLABS):
            out[pl.ds(s, BM, stride=ROW_SLABS), :] = y[:, s * LANES:(s + 1) * LANES]
        y_copy(b, slot).start()
        return k + new_expert.astype(I32)

    lax.fori_loop(0, n_valid, block, jnp.int32(0))

    @pl.when(n_valid >= 2)
    def _():
        y_copy(n_valid - 2, n_valid % 2).wait()
    y_copy(n_valid - 1, (n_valid - 1) % 2).wait()

    ybuf[0] = jnp.zeros((BM * ROW_SLABS, LANES), F32)

    def zero_blocks(wait):
        def one(b, carry):
            cp = y_copy(b, 0)
            cp.wait() if wait else cp.start()
            return carry
        lax.fori_loop(n_valid, NB, one, 0)

    zero_blocks(False)
    zero_blocks(True)


def _moe_blocks(tables, xs, w_gu, b_gu, w_down, b_down, li):
    grid_spec = pltpu.PrefetchScalarGridSpec(
        num_scalar_prefetch=3,
        grid=(1,),
        in_specs=[
            pl.BlockSpec(memory_space=pl.ANY),
            pl.BlockSpec(memory_space=pl.ANY),
            pl.BlockSpec((None, N_EXPERTS, 1, 2 * D_EXPERT), lambda i, *_: (li, 0, 0, 0)),
            pl.BlockSpec(memory_space=pl.ANY),
            pl.BlockSpec((None, N_EXPERTS, 1, D_MODEL), lambda i, *_: (li, 0, 0, 0)),
        ],
        out_specs=pl.BlockSpec(memory_space=pl.ANY),
        scratch_shapes=[
            pltpu.VMEM((2, BM * ROW_SLABS, LANES), F32),
            pltpu.VMEM((2, BM * ROW_SLABS, LANES), F32),
            pltpu.VMEM((2, D_MODEL, 2 * D_EXPERT), F32),
            pltpu.VMEM((2, D_EXPERT, D_MODEL), F32),
            pltpu.VMEM((D_MODEL, 2 * D_EXPERT), BF16),
            pltpu.VMEM((D_EXPERT, D_MODEL), BF16),
            pltpu.SemaphoreType.DMA((2,)),
            pltpu.SemaphoreType.DMA((2,)),
            pltpu.SemaphoreType.DMA((2,)),
        ],
    )
    return pl.pallas_call(
        functools.partial(_moe_kernel, li=li),
        grid_spec=grid_spec,
        out_shape=jax.ShapeDtypeStruct((N_SLOTS * ROW_SLABS, LANES), F32),
        compiler_params=_cparams(1),
        name="moe_experts",
    )(tables["block_expert"], tables["expert_order"], tables["n_valid"], xs, w_gu,
      b_gu.reshape(DEPTH, N_EXPERTS, 1, 2 * D_EXPERT), w_down, b_down.reshape(DEPTH, N_EXPERTS, 1, D_MODEL))


def _combine_kernel(start_ref, len_ref, off_ref, y_hbm, x1_ref, ri_ref, rf_ref, offrow_ref, mod_ref,
                    lng_ref, lnb_ref, oc_ref, od_ref, *scratch):
    bufs, sem = scratch[:RT_PER_STEP], scratch[RT_PER_STEP]
    i = pl.program_id(0)
    n_steps = pl.num_programs(0)

    def fetch(tile, t):
        _tile_run_copies(tile, start_ref, len_ref, off_ref, y_hbm, bufs[t], sem.at[t], to_sorted=False)

    @pl.when(i == 0)
    def _():
        for t in range(FETCH_AHEAD):
            fetch(t, t)

    m = mod_ref[pl.ds(_mod_row(i * RT_PER_STEP // RT_PER_TM), 1), :]
    g2 = m[:, 5 * D_MODEL:6 * D_MODEL]
    col = lax.broadcasted_iota(I32, (RT, RT_ROWS), 1).astype(F32)
    outs = []
    for t in range(RT_PER_STEP):
        rows = slice(t * RT, (t + 1) * RT)
        fetch(jnp.minimum(i * RT_PER_STEP + t + FETCH_AHEAD, N_RT - 1), (t + FETCH_AHEAD) % RT_PER_STEP)
        _tile_runs_wait(bufs[t], sem.at[t])
        pos = _packed_positions(ri_ref[rows, :], offrow_ref[t, 0:1, :])
        gates = rf_ref[rows, :]
        mix = jnp.zeros((RT, RT_ROWS), F32)
        for k in range(TOP_K):
            mix = mix + jnp.where(col == pos[k], gates[:, k:k + 1], 0.0)
        ffn = jnp.dot(mix.astype(BF16), _slab_columns(bufs[t], 0, RT_ROWS).astype(BF16),
                      preferred_element_type=F32)
        outs.append(_layer_norm_rows(DN_ALPHA * x1_ref[rows, :] + g2 * ffn, lng_ref[...], lnb_ref[...]))
    out = jnp.concatenate(outs, axis=0)
    n_ctx_steps = N_CTX // (RT * RT_PER_STEP)

    @pl.when(i < n_ctx_steps)
    def _():
        oc_ref[...] = out

    @pl.when(i >= n_ctx_steps)
    def _():
        od_ref[...] = out

    @pl.when(i == n_steps - 1)
    def _():
        for t in range(FETCH_AHEAD):
            _tile_runs_wait(bufs[t], sem.at[t])


def _combine(tables, y, x1, route_i, route_f, mod, ln_g, ln_b, li):
    rows = RT * RT_PER_STEP
    grid_spec = pltpu.PrefetchScalarGridSpec(
        num_scalar_prefetch=3,
        grid=(N_RT // RT_PER_STEP,),
        in_specs=[
            pl.BlockSpec(memory_space=pl.ANY),
            pl.BlockSpec((rows, D_MODEL), lambda i, *_: (i, 0)),
            pl.BlockSpec((rows, LANES), lambda i, *_: (i, 0)),
            pl.BlockSpec((rows, LANES), lambda i, *_: (i, 0)),
            pl.BlockSpec((RT_PER_STEP, SUBLANES, LANES), lambda i, *_: (i, 0, 0)),
            pl.BlockSpec((None, SUBLANES, 6 * D_MODEL), lambda i, *_: (li, 0, 0)),
            pl.BlockSpec((1, D_MODEL), lambda i, *_: (0, 0)),
            pl.BlockSpec((1, D_MODEL), lambda i, *_: (0, 0)),
        ],
        out_specs=_pair_specs(rows, D_MODEL, N_CTX // rows),
        scratch_shapes=[pltpu.VMEM((RT_ROWS * ROW_SLABS, LANES), F32)] * RT_PER_STEP + [
            pltpu.SemaphoreType.DMA((RT_PER_STEP,)),
        ],
    )
    return pl.pallas_call(
        _combine_kernel,
        grid_spec=grid_spec,
        out_shape=[jax.ShapeDtypeStruct((N_CTX, D_MODEL), F32), jax.ShapeDtypeStruct((N_DEC, D_MODEL), F32)],
        compiler_params=_cparams(1),
        name="moe_combine",
    )(tables["start"], tables["len"], tables["off"], y, x1, route_i, route_f, tables["off_rows"], mod, ln_g, ln_b)


def _routing_tables(tile_cnt):
    cnt = tile_cnt[:, 0, :N_EXPERTS]
    totals = jnp.sum(cnt, axis=0)
    padded = (totals + BM - 1) // BM * BM
    pends = jnp.cumsum(padded)
    pstarts = pends - padded
    start = pstarts[None, :] + jnp.cumsum(cnt, axis=0) - cnt
    off = jnp.cumsum(cnt, axis=1) - cnt
    n_valid = (pends[-1] // BM).astype(I32)
    block_start = jnp.arange(NB, dtype=I32) * BM
    block_e = jnp.minimum(jnp.sum(block_start[:, None] >= pends[None, :], axis=1), N_EXPERTS - 1)
    last_e = block_e[jnp.maximum(n_valid - 1, 0)]
    block_e = jnp.where(jnp.arange(NB) < n_valid, block_e, last_e)
    ids = jnp.arange(N_EXPERTS, dtype=I32)
    order = jnp.sort(jnp.where(totals > 0, ids, N_EXPERTS))
    order = jnp.concatenate([jnp.where(order < N_EXPERTS, order, -1), jnp.full((2,), -1, I32)])
    off_rows = jnp.pad(off.astype(F32), ((0, 0), (0, LANES - N_EXPERTS)))
    return {
        "start": start.reshape(-1).astype(I32),
        "len": cnt.reshape(-1).astype(I32),
        "off": off.reshape(-1).astype(I32),
        "pad_start": (pstarts + totals).astype(I32),
        "pad_len": (padded - totals).astype(I32),
        "n_valid": n_valid.reshape(1),
        "block_expert": block_e.astype(I32),
        "expert_order": order.astype(I32),
        "off_rows": jnp.broadcast_to(off_rows[:, None, :], (N_RT, SUBLANES, LANES)),
    }


def _layer_params(p, li):
    eye = jnp.eye(N_LRU_BLOCKS, dtype=F32)
    w_rg_full = jnp.einsum("dkgio,gh->gidkho", p["w_rg"][li], eye).reshape(LRU_WIDTH, 4 * LRU_WIDTH)
    pad = LANES - N_EXPERTS
    return {
        "sgu_ln_g": p["sgu_ln_g"][li].reshape(1, SGU_WIDTH),
        "sgu_ln_b": p["sgu_ln_b"][li].reshape(1, SGU_WIDTH),
        "w_spatial": p["w_spatial"][li],
        "b_spatial_full": jnp.repeat(p["b_spatial"][li].T, SGU_GROUP, axis=1),
        "conv_w": p["conv_w"][li],
        "conv_b": p["conv_b"][li].reshape(1, LRU_WIDTH),
        "w_rg_full": w_rg_full,
        "b_rg_full": p["b_rg"][li].reshape(1, 4 * LRU_WIDTH),
        "lru_log": p["lru_log"][li],
        "w_out_all": p["w_out"],
        "ln_g1": p["ln_g"][li, 0].reshape(1, D_MODEL),
        "ln_b1": p["ln_b"][li, 0].reshape(1, D_MODEL),
        "ln_g2": p["ln_g"][li, 1].reshape(1, D_MODEL),
        "ln_b2": p["ln_b"][li, 1].reshape(1, D_MODEL),
        "w_router_pad": jnp.pad(p["w_router"][li], ((0, 0), (0, pad))),
        "b_router_pad": jnp.pad(p["b_router"][li], (0, pad)).reshape(1, LANES),
    }


def kernel(x_prompt, x_sample, cache_k, cache_v, state_lru, c, c_ctx, w_mod, b_mod, w_in, lam, subln_g, sgu_ln_g, sgu_ln_b, w_spatial, b_spatial, conv_w, conv_b, w_rg, b_rg, lru_log, w_out, ln_g, ln_b, w_router, b_router, w_gu, b_gu, w_down, b_down):
    p = dict(sgu_ln_g=sgu_ln_g, sgu_ln_b=sgu_ln_b, w_spatial=w_spatial, b_spatial=b_spatial, conv_w=conv_w,
             conv_b=conv_b, w_rg=w_rg, b_rg=b_rg, lru_log=lru_log, w_out=w_out, ln_g=ln_g, ln_b=ln_b,
             w_router=w_router, b_router=b_router)
    cvec8 = jnp.concatenate([c_ctx[None, :], c, jnp.zeros((SUBLANES - 1 - DEC_BATCH, D_MODEL), F32)], axis=0)
    mod = _modulation(cvec8, w_mod, b_mod)
    x_pair = (x_prompt.reshape(N_CTX, D_MODEL), x_sample.reshape(N_DEC, D_MODEL))
    tables = _rope_tables()
    zero_state = jnp.zeros((BATCH, 1, 2, LRU_WIDTH), F32)
    prev_kv = None
    new_s = []
    for li in range(DEPTH):
        lp = _layer_params(p, li)
        lam_init = 0.8 - 0.6 * math.exp(-0.3 * li)
        proj = _input_projection(*x_pair, mod, w_in, li)
        att_ctx, kc, vc = _context_attention(proj, lam[li], subln_g[li], lam_init, prev_kv)
        prev_kv = (kc, vc)
        att_dec = _denoise_attention(proj, cache_k, cache_v, lam[li], subln_g[li], li, lam_init, tables)
        sgu_ctx, lru_ctx, h_ctx = _mixers(proj, zero_state, 0, lp, SEQ, BATCH, 0)
        sgu_dec, lru_dec, _ = _mixers(proj, state_lru, li, lp, DEC_SEQ, DEC_BATCH, N_CTX // DEC_SEQ)
        x1, h2, route_i, route_f, tile_cnt = _output_projection(
            x_pair, (att_ctx, att_dec), (sgu_ctx, sgu_dec), (lru_ctx, lru_dec), mod, lp, li)
        rt = _routing_tables(tile_cnt)
        xs = _dispatch(rt, h2, route_i)
        y = _moe_blocks(rt, xs, w_gu, b_gu, w_down, b_down, li)
        x_pair = _combine(rt, y, x1, route_i, route_f, mod, lp["ln_g2"], lp["ln_b2"], li)
        new_s.append(h_ctx)
    y_prompt = x_pair[0].reshape(BATCH, SEQ, D_MODEL)
    y_sample = x_pair[1].reshape(DEC_BATCH, DEC_SEQ, D_MODEL)
    return (y_prompt, y_sample, prev_kv[0], prev_kv[1], jnp.stack(new_s, axis=1))
```

```python
import functools
import math

import numpy as np
import jax
import jax.numpy as jnp
from jax import lax
from jax.experimental import pallas as pl
from jax.experimental.pallas import tpu as pltpu

F32 = jnp.float32
BF16 = jnp.bfloat16
I32 = jnp.int32

D_MODEL = 1024
BATCH = 32
SEQ = 256
DEPTH = 2
DEC_BATCH = 2
DEC_SEQ = 1024
PAST_LEN = 256
GRID_W = 64
HEAD_DIM = 64
ATT_WIDTH = D_MODEL // 2
SGU_WIDTH = D_MODEL // 4
LRU_WIDTH = D_MODEL // 4
N_ATT_HEADS = ATT_WIDTH // HEAD_DIM
ATT_HALF = HEAD_DIM // 2
ROPE_FREQS = ATT_HALF // 4
ROPE_THETA = 10000.0
CHUNK = 128
N_SGU_GROUPS = 4
SGU_GROUP = SGU_WIDTH // N_SGU_GROUPS
N_LRU_BLOCKS = 4
LRU_BLOCK = LRU_WIDTH // N_LRU_BLOCKS
CONV_W = 4
LRU_C = 8.0
IN_COLS = 3 * ATT_WIDTH + 2 * SGU_WIDTH + 2 * LRU_WIDTH
N_EXPERTS = 32
TOP_K = 4
D_EXPERT = D_MODEL
SWIGLU_LIMIT = 7.0
SWIGLU_ALPHA = 1.702
DN_ALPHA = (2 * DEPTH) ** 0.25
EPS = 1e-5

N_CTX = BATCH * SEQ
N_DEC = DEC_BATCH * DEC_SEQ
N_TOK = N_CTX + N_DEC

LANES = 128
SUBLANES = 8
ROW_SLABS = D_MODEL // LANES

TM = 512
CTX_TILES = N_CTX // TM
TILES_PER_DEC = DEC_SEQ // TM
N_TILES = N_TOK // TM
MOD_TN = 512
RT = 256
N_RT = N_TOK // RT
RT_PER_TM = TM // RT
RT_ROWS = RT * TOP_K
RT_PER_STEP = 4
FETCH_AHEAD = 2
BM = 256
NB = N_TOK * TOP_K // BM + N_EXPERTS
WEIGHT_PIECES = 4
N_SLOTS = NB * BM
RUN_BITS = tuple(1 << b for b in range(RT.bit_length() - 1, -1, -1))
PAD_BITS = tuple(1 << b for b in range(BM.bit_length() - 2, -1, -1))
VMEM_LIMIT = 56 * 1024 * 1024


def _cparams(n_axes):
    return pltpu.CompilerParams(
        dimension_semantics=("arbitrary",) * n_axes,
        vmem_limit_bytes=VMEM_LIMIT)


def _mod_row(i):
    return jnp.where(i < CTX_TILES, 0, 1 + (i - CTX_TILES) // TILES_PER_DEC)


def _layer_norm_rows(z, g, b):
    mu = jnp.mean(z, axis=-1, keepdims=True)
    zc = z - mu
    var = jnp.mean(zc * zc, axis=-1, keepdims=True)
    return zc * lax.rsqrt(var + EPS) * g + b


def _pair_specs(tile, width, n_ctx_tiles):
    ctx = pl.BlockSpec((tile, width), lambda i, *_: (jnp.minimum(i, n_ctx_tiles - 1), 0))
    dec = pl.BlockSpec((tile, width), lambda i, *_: (jnp.maximum(i - n_ctx_tiles, 0), 0))
    return [ctx, dec]


def _pair_read(i, n_ctx_tiles, ctx_ref, dec_ref):
    return jnp.where(i < n_ctx_tiles, ctx_ref[...], dec_ref[...])


def _mod_kernel(cvec_ref, w_ref, b_ref, o_ref):
    cv = cvec_ref[...]
    s = cv * jax.nn.sigmoid(cv)
    s_t = s.T
    w = w_ref[0]
    rows = [jnp.sum(s_t[:, r:r + 1] * w, axis=0, keepdims=True) for r in range(1 + DEC_BATCH)]
    rows.append(jnp.zeros((SUBLANES - 1 - DEC_BATCH, MOD_TN), F32))
    o_ref[0] = jnp.concatenate(rows, axis=0) + b_ref[0]


def _modulation(cvec8, w_mod, b_mod):
    n_out = w_mod.shape[-1]
    return pl.pallas_call(
        _mod_kernel,
        grid=(DEPTH, n_out // MOD_TN),
        in_specs=[
            pl.BlockSpec((SUBLANES, D_MODEL), lambda l, j: (0, 0)),
            pl.BlockSpec((1, D_MODEL, MOD_TN), lambda l, j: (l, 0, j)),
            pl.BlockSpec((1, 1, MOD_TN), lambda l, j: (l, 0, j)),
        ],
        out_specs=pl.BlockSpec((1, SUBLANES, MOD_TN), lambda l, j: (l, 0, j)),
        out_shape=jax.ShapeDtypeStruct((DEPTH, SUBLANES, n_out), F32),
        compiler_params=_cparams(2),
        name="modulation",
    )(cvec8, w_mod, b_mod.reshape(DEPTH, 1, n_out))


def _inproj_kernel(xc_ref, xd_ref, mod_ref, w_ref, o_ref, wbf_ref):
    i = pl.program_id(0)

    @pl.when(i == 0)
    def _():
        wbf_ref[...] = w_ref[...].astype(BF16)

    m = mod_ref[pl.ds(_mod_row(i), 1), :]
    sh1 = m[:, 0:D_MODEL]
    sc1 = m[:, D_MODEL:2 * D_MODEL]
    x = _pair_read(i, CTX_TILES, xc_ref, xd_ref)
    h = (x * (1.0 + sc1) + sh1).astype(BF16)
    o_ref[...] = jnp.dot(h, wbf_ref[...], preferred_element_type=F32)


def _input_projection(x_ctx, x_dec, mod, w_in, li):
    return pl.pallas_call(
        _inproj_kernel,
        grid=(N_TILES,),
        in_specs=_pair_specs(TM, D_MODEL, CTX_TILES) + [
            pl.BlockSpec((None, SUBLANES, 6 * D_MODEL), lambda i: (li, 0, 0)),
            pl.BlockSpec((None, D_MODEL, IN_COLS), lambda i: (li, 0, 0), pipeline_mode=pl.Buffered(1)),
        ],
        out_specs=pl.BlockSpec((TM, IN_COLS), lambda i: (i, 0)),
        out_shape=jax.ShapeDtypeStruct((N_TOK, IN_COLS), F32),
        scratch_shapes=[pltpu.VMEM((D_MODEL, IN_COLS), BF16)],
        compiler_params=_cparams(1),
        name="input_projection",
    )(x_ctx, x_dec, mod, w_in)


def _lambda(lam_ref, lam_init):
    lm = lam_ref[...]
    a = jnp.sum(lm[0:1] * lm[1:2], axis=-1, keepdims=True)
    b = jnp.sum(lm[2:3] * lm[3:4], axis=-1, keepdims=True)
    return jnp.exp(a) - jnp.exp(b) + lam_init


_NT = (((1,), (1,)), ((), ()))
_TN = (((0,), (0,)), ((), ()))
_ATT_SCALE = ATT_HALF ** -0.5
_Q_SCALE = _ATT_SCALE * math.log2(math.e)
KEY_CHUNK = 128


def _attention_heads(q_parts, k_segs, values, s_scr, e_scr, lam, g_col, lam_init):
    n_heads = len(q_parts)
    for h in range(n_heads):
        for c in range(2):
            row = 0
            for k in k_segs[h][c]:
                s_scr[2 * h + c, row:row + k.shape[0], :] = lax.dot_general(
                    k, q_parts[h][c], _NT, preferred_element_type=F32)
                row += k.shape[0]
    n_keys, n_q = s_scr.shape[1:]
    chunks = [slice(r, r + KEY_CHUNK) for r in range(0, n_keys, KEY_CHUNK)]
    for n in range(2 * n_heads):
        tops = [jnp.max(s_scr[n, rows, :].reshape(KEY_CHUNK // SUBLANES, SUBLANES, n_q), axis=0)
                for rows in chunks]
        top = jnp.max(functools.reduce(jnp.maximum, tops), axis=0, keepdims=True)
        for rows in chunks:
            e_scr[n, rows, :] = jnp.exp2(s_scr[n, rows, :] - top).astype(BF16)
    outs = []
    for h in range(n_heads):
        v_aug = jnp.concatenate([values[h], jnp.ones_like(values[h])], axis=-1)
        normed = []
        for c in range(2):
            acc = lax.dot_general(v_aug, e_scr[2 * h + c], _TN, preferred_element_type=F32)
            normed.append(acc[:HEAD_DIM] * (1.0 / acc[HEAD_DIM:HEAD_DIM + 1]))
        o_t = normed[0] - lam * normed[1]
        ms = jnp.mean(o_t * o_t, axis=0, keepdims=True)
        outs.append(o_t * lax.rsqrt(ms + EPS) * g_col * (1.0 - lam_init))
    return outs


def _head_cols(h):
    lo = h * HEAD_DIM
    return [slice(lo + c * ATT_HALF, lo + (c + 1) * ATT_HALF) for c in range(2)]


def _ctx_attn_kernel(lam_ref, g_ref, q_ref, k_ref, v_ref, *rest, lam_init, stacked):
    *rest, s_scr, e_scr = rest
    if stacked:
        pk_ref, pv_ref, att_ref, ck_ref, cv_ref = rest
        ck_ref[0, 0] = pk_ref[0]
        cv_ref[0, 0] = pv_ref[0]
        put_k = lambda h, val: ck_ref.__setitem__((0, 1, h), val)
        put_v = lambda h, val: cv_ref.__setitem__((0, 1, h), val)
    else:
        att_ref, ck_ref, cv_ref = rest
        put_k = lambda h, val: ck_ref.__setitem__((0, h), val)
        put_v = lambda h, val: cv_ref.__setitem__((0, h), val)
    lam = _lambda(lam_ref, lam_init)
    k = k_ref[...]
    v = v_ref[...]
    qb = (q_ref[...] * _Q_SCALE).astype(BF16)
    kb = k.astype(BF16)
    vb = v.astype(BF16)
    heads = range(N_ATT_HEADS)
    outs = _attention_heads([[qb[:, cs] for cs in _head_cols(h)] for h in heads],
                            [[[kb[:, cs]] for cs in _head_cols(h)] for h in heads],
                            [vb[:, h * HEAD_DIM:(h + 1) * HEAD_DIM] for h in heads],
                            s_scr, e_scr, lam, g_ref[...], lam_init)
    for h in heads:
        put_k(h, k[:, h * HEAD_DIM:(h + 1) * HEAD_DIM])
        put_v(h, v[:, h * HEAD_DIM:(h + 1) * HEAD_DIM])
    att_ref[...] = jnp.concatenate(outs, axis=0).T


def _context_attention(proj, lam, subln_g, lam_init, prev_kv=None):
    stacked = prev_kv is not None
    per_layer = pl.BlockSpec((1, N_ATT_HEADS, SEQ, HEAD_DIM), lambda b: (b, 0, 0, 0))
    if stacked:
        kv_shape = jax.ShapeDtypeStruct((BATCH, DEPTH, N_ATT_HEADS, SEQ, HEAD_DIM), F32)
        kv_spec = pl.BlockSpec((1, DEPTH, N_ATT_HEADS, SEQ, HEAD_DIM), lambda b: (b, 0, 0, 0, 0))
    else:
        kv_shape = jax.ShapeDtypeStruct((BATCH, N_ATT_HEADS, SEQ, HEAD_DIM), F32)
        kv_spec = per_layer
    return pl.pallas_call(
        functools.partial(_ctx_attn_kernel, lam_init=lam_init, stacked=stacked),
        grid=(BATCH,),
        in_specs=[
            pl.BlockSpec((4, ATT_HALF), lambda b: (0, 0)),
            pl.BlockSpec((HEAD_DIM, 1), lambda b: (0, 0)),
            pl.BlockSpec((SEQ, ATT_WIDTH), lambda b: (b, 0)),
            pl.BlockSpec((SEQ, ATT_WIDTH), lambda b: (b, 1)),
            pl.BlockSpec((SEQ, ATT_WIDTH), lambda b: (b, 2)),
        ] + ([per_layer, per_layer] if stacked else []),
        out_specs=[pl.BlockSpec((SEQ, ATT_WIDTH), lambda b: (b, 0)), kv_spec, kv_spec],
        out_shape=[jax.ShapeDtypeStruct((N_CTX, ATT_WIDTH), F32), kv_shape, kv_shape],
        scratch_shapes=[pltpu.VMEM((2 * N_ATT_HEADS, SEQ, SEQ), F32),
                        pltpu.VMEM((2 * N_ATT_HEADS, SEQ, SEQ), BF16)],
        compiler_params=_cparams(1),
        name="context_attention",
    )(lam, subln_g.reshape(HEAD_DIM, 1), proj, proj, proj, *(prev_kv if stacked else ()))


def _rope_tables():
    t = np.arange(DEC_SEQ)
    pos = np.stack([t // GRID_W, t % GRID_W], axis=1).astype(np.float32)
    inv = (np.float32(ROPE_THETA) ** (-np.arange(ROPE_FREQS, dtype=np.float32) / np.float32(ROPE_FREQS)))
    j = np.arange(HEAD_DIM)
    d = j % ATT_HALF
    axis = d // (2 * ROPE_FREQS)
    u = d % (2 * ROPE_FREQS)
    ang = pos[:, axis] * inv[u % ROPE_FREQS][None, :].astype(np.float32)
    cos = np.cos(ang).astype(np.float32)
    sin = np.sin(ang).astype(np.float32)
    first = (u < ROPE_FREQS)[None, :]
    s_next = np.where(first, -sin, 0.0).astype(np.float32)
    s_prev = np.where(first, 0.0, sin).astype(np.float32)
    tile = lambda a: jnp.asarray(np.tile(a, (1, N_ATT_HEADS)))
    return tile(cos), tile(s_next), tile(s_prev)


def _rotate(x, cos, s_next, s_prev):
    width = x.shape[-1]
    return (x * cos + pltpu.roll(x, width - ROPE_FREQS, axis=1) * s_next
            + pltpu.roll(x, ROPE_FREQS, axis=1) * s_prev)


QB = 256
Q_STEPS = DEC_SEQ // QB
DEC_HEAD_GROUP = 2


def _dec_attn_kernel(lam_ref, g_ref, q_ref, k_ref, v_ref, ck_ref, cv_ref,
                     cq_ref, snq_ref, spq_ref, ck_tab, snk_tab, spk_tab,
                     att_ref, krot_ref, s_scr, e_scr, *, lam_init):
    j = pl.program_id(1)

    @pl.when(j == 0)
    def _():
        krot_ref[...] = _rotate(k_ref[...], ck_tab[...], snk_tab[...], spk_tab[...]).astype(BF16)

    lam = _lambda(lam_ref, lam_init)
    qb = (_rotate(q_ref[...], cq_ref[...], snq_ref[...], spq_ref[...]) * _Q_SCALE).astype(BF16)
    kb = krot_ref[...]
    vb = v_ref[...].astype(BF16)
    outs = []
    for h0 in range(0, N_ATT_HEADS, DEC_HEAD_GROUP):
        heads = range(h0, h0 + DEC_HEAD_GROUP)
        past_k = [ck_ref[0, 0, h].astype(BF16) for h in heads]
        k_segs = [[[pk[:, c * ATT_HALF:(c + 1) * ATT_HALF], kb[:, _head_cols(h)[c]]] for c in range(2)]
                  for h, pk in zip(heads, past_k)]
        values = [jnp.concatenate([cv_ref[0, 0, h].astype(BF16), vb[:, h * HEAD_DIM:(h + 1) * HEAD_DIM]], axis=0)
                  for h in heads]
        outs += _attention_heads([[qb[:, cs] for cs in _head_cols(h)] for h in heads], k_segs, values,
                                 s_scr, e_scr, lam, g_ref[...], lam_init)
    att_ref[...] = jnp.concatenate(outs, axis=0).T


def _denoise_attention(proj, cache_k, cache_v, lam, subln_g, li, lam_init, tables):
    cos, s_next, s_prev = tables
    row0 = N_CTX // QB
    seq0 = N_CTX // DEC_SEQ
    q_tab = pl.BlockSpec((QB, ATT_WIDTH), lambda b, j: (j, 0))
    k_tab = pl.BlockSpec((DEC_SEQ, ATT_WIDTH), lambda b, j: (0, 0))
    cache_spec = pl.BlockSpec((1, 1, N_ATT_HEADS, PAST_LEN, HEAD_DIM), lambda b, j: (b, li, 0, 0, 0))
    return pl.pallas_call(
        functools.partial(_dec_attn_kernel, lam_init=lam_init),
        grid=(DEC_BATCH, Q_STEPS),
        in_specs=[
            pl.BlockSpec((4, ATT_HALF), lambda b, j: (0, 0)),
            pl.BlockSpec((HEAD_DIM, 1), lambda b, j: (0, 0)),
            pl.BlockSpec((QB, ATT_WIDTH), lambda b, j: (row0 + b * Q_STEPS + j, 0)),
            pl.BlockSpec((DEC_SEQ, ATT_WIDTH), lambda b, j: (seq0 + b, 1)),
            pl.BlockSpec((DEC_SEQ, ATT_WIDTH), lambda b, j: (seq0 + b, 2)),
            cache_spec, cache_spec,
            q_tab, q_tab, q_tab, k_tab, k_tab, k_tab,
        ],
        out_specs=pl.BlockSpec((QB, ATT_WIDTH), lambda b, j: (b * Q_STEPS + j, 0)),
        out_shape=jax.ShapeDtypeStruct((N_DEC, ATT_WIDTH), F32),
        scratch_shapes=[pltpu.VMEM((DEC_SEQ, ATT_WIDTH), BF16),
                        pltpu.VMEM((2 * DEC_HEAD_GROUP, PAST_LEN + DEC_SEQ, QB), F32),
                        pltpu.VMEM((2 * DEC_HEAD_GROUP, PAST_LEN + DEC_SEQ, QB), BF16)],
        compiler_params=_cparams(2),
        name="denoise_attention",
    )(lam, subln_g.reshape(HEAD_DIM, 1), proj, proj, proj, cache_k, cache_v,
      cos, s_next, s_prev, cos, s_next, s_prev)


def _softplus(z):
    return jnp.maximum(z, 0.0) + jnp.log1p(jnp.exp(-jnp.abs(z)))


def _mixer_kernel(su_ref, sv_ref, rx_ref, rg_ref, h0_ref, lng_ref, lnb_ref, ws_ref, bs_ref,
                  cw_ref, cb_ref, wrg_ref, brg_ref, lrulog_ref,
                  sgu_ref, lru_ref, hlast_ref, a_scr, b_scr, h_scr, *, seq_len):
    vn = _layer_norm_rows(sv_ref[...], lng_ref[...], lnb_ref[...])
    lane_group = lax.broadcasted_iota(I32, (CHUNK, SGU_WIDTH), 1) // SGU_GROUP
    for n in range(seq_len // CHUNK):
        rows = slice(n * CHUNK, (n + 1) * CHUNK)
        vc = vn[rows].astype(BF16)
        s = jnp.zeros((CHUNK, SGU_WIDTH), F32)
        for g in range(N_SGU_GROUPS):
            sg = jnp.dot(ws_ref[g].astype(BF16), vc, preferred_element_type=F32)
            s = jnp.where(lane_group == g, sg, s)
        sgu_ref[rows, :] = su_ref[rows, :] * (s + bs_ref[...])

    x = rx_ref[...]
    row = lax.broadcasted_iota(I32, (seq_len, LRU_WIDTH), 0)

    def shifted(val, d, fill):
        rolled = pltpu.roll(val, d % seq_len, axis=0)
        inside = (row >= d) if d > 0 else (row < seq_len + d)
        return jnp.where(inside, rolled, fill)

    left = CONV_W // 2
    xc = cb_ref[...] + x * cw_ref[left:left + 1, :]
    for tap in range(CONV_W):
        if tap != left:
            xc = xc + shifted(x, left - tap, 0.0) * cw_ref[tap:tap + 1, :]
    gates = jax.nn.sigmoid(jnp.dot(xc.astype(BF16), wrg_ref[...].astype(BF16),
                                   preferred_element_type=F32) + brg_ref[...])
    in_chunk = row % SUBLANES
    n_chunks = seq_len // SUBLANES
    h0 = h0_ref[0, 0]
    lasts = []
    for direction in range(2):
        reverse = direction == 1
        base = direction * 2 * LRU_WIDTH
        r = gates[:, base:base + LRU_WIDTH]
        gi = gates[:, base + LRU_WIDTH:base + 2 * LRU_WIDTH]
        log_a = -LRU_C * r * _softplus(-lrulog_ref[direction:direction + 1, :])
        a = jnp.exp(log_a)
        b = jnp.sqrt(-jnp.tanh(log_a) * (a * a + 1.0)) * gi * xc
        for d in (1, 2, 4):
            if reverse:
                inside = in_chunk < SUBLANES - d
                a_n = jnp.where(inside, pltpu.roll(a, seq_len - d, axis=0), 1.0)
                b_n = jnp.where(inside, pltpu.roll(b, seq_len - d, axis=0), 0.0)
            else:
                inside = in_chunk >= d
                a_n = jnp.where(inside, pltpu.roll(a, d, axis=0), 1.0)
                b_n = jnp.where(inside, pltpu.roll(b, d, axis=0), 0.0)
            b = a * b_n + b
            a = a * a_n
        a_scr[...] = a
        b_scr[...] = b

        def chunk_step(c, carry, reverse=reverse):
            cc = n_chunks - 1 - c if reverse else c
            off = pl.multiple_of(cc * SUBLANES, SUBLANES)
            hc = a_scr[pl.ds(off, SUBLANES), :] * carry + b_scr[pl.ds(off, SUBLANES), :]
            if reverse:
                h_scr[pl.ds(off, SUBLANES), :] = h_scr[pl.ds(off, SUBLANES), :] + hc
                return hc[0:1, :]
            h_scr[pl.ds(off, SUBLANES), :] = hc
            return hc[SUBLANES - 1:SUBLANES, :]

        lasts.append(lax.fori_loop(0, n_chunks, chunk_step, h0[direction:direction + 1, :]))
    lru_ref[...] = h_scr[...] * jax.nn.gelu(rg_ref[...])
    hlast_ref[0] = jnp.concatenate(lasts, axis=0)


def _mixers(proj, h0, h0_layer, lp, seq_len, n_seq, row_block0):
    col0 = 3 * ATT_WIDTH // SGU_WIDTH
    col = lambda c: pl.BlockSpec((seq_len, SGU_WIDTH), lambda b: (row_block0 + b, col0 + c))
    full = lambda shape: pl.BlockSpec(shape, lambda b: (0,) * len(shape))
    out_rows = pl.BlockSpec((seq_len, SGU_WIDTH), lambda b: (b, 0))
    return pl.pallas_call(
        functools.partial(_mixer_kernel, seq_len=seq_len),
        grid=(n_seq,),
        in_specs=[
            col(0), col(1), col(2), col(3),
            pl.BlockSpec((1, 1, 2, LRU_WIDTH), lambda b: (b, h0_layer, 0, 0)),
            full((1, SGU_WIDTH)), full((1, SGU_WIDTH)),
            full((N_SGU_GROUPS, CHUNK, CHUNK)), full((CHUNK, SGU_WIDTH)),
            full((CONV_W, LRU_WIDTH)), full((1, LRU_WIDTH)),
            full((LRU_WIDTH, 4 * LRU_WIDTH)), full((1, 4 * LRU_WIDTH)),
            full((2, LRU_WIDTH)),
        ],
        out_specs=[out_rows, out_rows, pl.BlockSpec((1, 2, LRU_WIDTH), lambda b: (b, 0, 0))],
        out_shape=[
            jax.ShapeDtypeStruct((n_seq * seq_len, SGU_WIDTH), F32),
            jax.ShapeDtypeStruct((n_seq * seq_len, LRU_WIDTH), F32),
            jax.ShapeDtypeStruct((n_seq, 2, LRU_WIDTH), F32),
        ],
        scratch_shapes=[pltpu.VMEM((seq_len, LRU_WIDTH), F32)] * 3,
        compiler_params=_cparams(1),
        name="mixers_%d" % seq_len,
    )(proj, proj, proj, proj, h0, lp["sgu_ln_g"], lp["sgu_ln_b"], lp["w_spatial"], lp["b_spatial_full"],
      lp["conv_w"], lp["conv_b"], lp["w_rg_full"], lp["b_rg_full"], lp["lru_log"])


def _outproj_kernel(xc_ref, xd_ref, ac_ref, ad_ref, sc_ref, sd_ref, lc_ref, ld_ref,
                    mod_ref, wout_ref, lng_ref, lnb_ref, wr_ref, br_ref,
                    x1_ref, h2_ref, ri_ref, rf_ref, cnt_ref, wbf_ref):
    i = pl.program_id(0)

    @pl.when(i == 0)
    def _():
        wbf_ref[...] = wout_ref[...].astype(BF16)

    m = mod_ref[pl.ds(_mod_row(i), 1), :]
    g1 = m[:, 2 * D_MODEL:3 * D_MODEL]
    sh2 = m[:, 3 * D_MODEL:4 * D_MODEL]
    sc2 = m[:, 4 * D_MODEL:5 * D_MODEL]
    a0, a1 = ATT_WIDTH, ATT_WIDTH + SGU_WIDTH
    x = _pair_read(i, CTX_TILES, xc_ref, xd_ref)
    att = _pair_read(i, CTX_TILES, ac_ref, ad_ref).astype(BF16)
    sgu = _pair_read(i, CTX_TILES, sc_ref, sd_ref).astype(BF16)
    lru = _pair_read(i, CTX_TILES, lc_ref, ld_ref).astype(BF16)
    mix = (jnp.dot(att, wbf_ref[0:a0, :], preferred_element_type=F32)
           + jnp.dot(sgu, wbf_ref[a0:a1, :], preferred_element_type=F32)
           + jnp.dot(lru, wbf_ref[a1:, :], preferred_element_type=F32))
    x1 = _layer_norm_rows(DN_ALPHA * x + g1 * mix, lng_ref[...], lnb_ref[...])
    x1_ref[...] = x1
    h2 = x1 * (1.0 + sc2) + sh2
    h2_ref[...] = h2.astype(BF16)

    logits = jnp.dot(h2.astype(BF16), wr_ref[...].astype(BF16), preferred_element_type=F32) + br_ref[...]
    lane = lax.broadcasted_iota(I32, (TM, LANES), 1)
    lane_f = lane.astype(F32)
    neg_inf = jnp.float32(-jnp.inf)
    work = jnp.where(lane < N_EXPERTS, logits, neg_inf)
    vals, idxs = [], []
    for _ in range(TOP_K):
        top = jnp.max(work, axis=-1, keepdims=True)
        idx = jnp.min(jnp.where(work == top, lane_f, float(LANES)), axis=-1, keepdims=True)
        vals.append(top)
        idxs.append(idx)
        work = jnp.where(lane_f == idx, neg_inf, work)
    exps = [jnp.exp(v - vals[0]) for v in vals]
    denom = exps[0] + exps[1] + exps[2] + exps[3]
    onehot = jnp.zeros((TM, LANES), F32)
    for idx in idxs:
        onehot = onehot + (lane_f == idx).astype(F32)
    r_i = lax.broadcasted_iota(I32, (RT, RT), 0)
    c_i = lax.broadcasted_iota(I32, (RT, RT), 1)
    tri = (r_i > c_i).astype(F32).astype(BF16)
    before = jnp.concatenate(
        [jnp.dot(tri, onehot[t * RT:(t + 1) * RT].astype(BF16), preferred_element_type=F32)
         for t in range(RT_PER_TM)], axis=0)
    ri = jnp.zeros((TM, LANES), F32)
    rf = jnp.zeros((TM, LANES), F32)
    for k in range(TOP_K):
        rank = jnp.sum(jnp.where(lane_f == idxs[k], before, 0.0), axis=-1, keepdims=True)
        ri = jnp.where(lane == k, idxs[k], ri)
        ri = jnp.where(lane == TOP_K + k, rank, ri)
        rf = jnp.where(lane == k, exps[k] / denom, rf)
    ri_ref[...] = ri.astype(I32)
    rf_ref[...] = rf
    for t in range(RT_PER_TM):
        total = jnp.sum(onehot[t * RT:(t + 1) * RT], axis=0, keepdims=True)
        cnt_ref[t] = jnp.broadcast_to(total, (SUBLANES, LANES)).astype(I32)


def _output_projection(x_pair, att_pair, sgu_pair, lru_pair, mod, lp, li):
    rows = lambda w: pl.BlockSpec((TM, w), lambda i: (i, 0))
    full = lambda shape: pl.BlockSpec(shape, lambda i: (0,) * len(shape))
    return pl.pallas_call(
        _outproj_kernel,
        grid=(N_TILES,),
        in_specs=(_pair_specs(TM, D_MODEL, CTX_TILES) + _pair_specs(TM, ATT_WIDTH, CTX_TILES)
                  + _pair_specs(TM, SGU_WIDTH, CTX_TILES) + _pair_specs(TM, LRU_WIDTH, CTX_TILES) + [
            pl.BlockSpec((None, SUBLANES, 6 * D_MODEL), lambda i: (li, 0, 0)),
            pl.BlockSpec((None, D_MODEL, D_MODEL), lambda i: (li, 0, 0), pipeline_mode=pl.Buffered(1)),
            full((1, D_MODEL)), full((1, D_MODEL)),
            full((D_MODEL, LANES)), full((1, LANES)),
        ]),
        out_specs=[
            rows(D_MODEL), rows(D_MODEL), rows(LANES), rows(LANES),
            pl.BlockSpec((RT_PER_TM, SUBLANES, LANES), lambda i: (i, 0, 0)),
        ],
        out_shape=[
            jax.ShapeDtypeStruct((N_TOK, D_MODEL), F32),
            jax.ShapeDtypeStruct((N_TOK, D_MODEL), BF16),
            jax.ShapeDtypeStruct((N_TOK, LANES), I32),
            jax.ShapeDtypeStruct((N_TOK, LANES), F32),
            jax.ShapeDtypeStruct((N_RT, SUBLANES, LANES), I32),
        ],
        scratch_shapes=[pltpu.VMEM((D_MODEL, D_MODEL), BF16)],
        compiler_params=_cparams(1),
        name="output_projection",
    )(*x_pair, *att_pair, *sgu_pair, *lru_pair, mod, lp["w_out_all"], lp["ln_g1"], lp["ln_b1"],
      lp["w_router_pad"], lp["b_router_pad"])


def _row_ds(row, n_rows):
    return pl.ds(pl.multiple_of(row * ROW_SLABS, ROW_SLABS), n_rows * ROW_SLABS)


def _for_each_run_piece(length, bits, fn):
    for bit in bits:
        done = length & (-2 * bit)
        @pl.when((length & bit) != 0)
        def _(done=done, bit=bit):
            fn(done, bit)


def _tile_run_copies(tile, start_ref, len_ref, off_ref, sorted_hbm, buf, sem, to_sorted, live=None):
    for e in range(N_EXPERTS):
        t = tile * N_EXPERTS + e
        start, length, off = start_ref[t], len_ref[t], off_ref[t]
        if live is not None:
            length = jnp.where(live, length, 0)

        def piece(done, bit, start=start, off=off):
            packed = buf.at[_row_ds(off + done, bit), :]
            srt = sorted_hbm.at[_row_ds(start + done, bit), :]
            if to_sorted:
                pltpu.make_async_copy(packed, srt, sem).start()
            else:
                pltpu.make_async_copy(srt, packed, sem).start()

        _for_each_run_piece(length, RUN_BITS, piece)


def _tile_runs_wait(buf, sem):
    pltpu.make_async_copy(buf, buf, sem).wait()


def _packed_positions(ri, off_row):
    lane = lax.broadcasted_iota(I32, (RT, LANES), 1)
    pos = []
    for k in range(TOP_K):
        e_k = ri[:, k:k + 1]
        base = jnp.sum(jnp.where(lane == e_k, off_row, 0.0), axis=-1, keepdims=True)
        pos.append(base + ri[:, TOP_K + k:TOP_K + k + 1].astype(F32))
    return pos


def _slab_columns(buf, row0, n_rows):
    return jnp.concatenate(
        [buf[pl.ds(row0 * ROW_SLABS + s, n_rows, stride=ROW_SLABS), :] for s in range(ROW_SLABS)], axis=-1)


def _dispatch_kernel(start_ref, len_ref, off_ref, pstart_ref, plen_ref, nv_ref,
                     h2_ref, ri_ref, offrow_ref, xs_hbm, *scratch):
    bufs, (zbuf, sem, zsem) = scratch[:RT_PER_STEP], scratch[RT_PER_STEP:]
    i = pl.program_id(0)
    n_steps = pl.num_programs(0)

    @pl.when(i == 0)
    def _():
        zbuf[...] = jnp.zeros_like(zbuf)

        def zero_fill(wait):
            def go(cp):
                cp.wait() if wait else cp.start()

            def per_expert(e, carry):
                def piece(done, bit):
                    go(pltpu.make_async_copy(zbuf.at[pl.ds(0, bit * ROW_SLABS), :],
                                             xs_hbm.at[_row_ds(pstart_ref[e] + done, bit), :], zsem))
                _for_each_run_piece(plen_ref[e], PAD_BITS, piece)
                return carry
            lax.fori_loop(0, N_EXPERTS, per_expert, 0)

            def per_block(b, carry):
                go(pltpu.make_async_copy(zbuf, xs_hbm.at[_row_ds(b * BM, BM), :], zsem))
                return carry
            lax.fori_loop(nv_ref[0], NB, per_block, 0)

        zero_fill(False)
        zero_fill(True)

    def send(tile, t, live):
        _tile_run_copies(tile, start_ref, len_ref, off_ref, xs_hbm, bufs[t], sem.at[t], to_sorted=True, live=live)

    col = lax.broadcasted_iota(I32, (RT, RT_ROWS), 1).astype(F32)
    for t in range(RT_PER_STEP):
        buf = bufs[t]
        rows = slice(t * RT, (t + 1) * RT)
        tile = i * RT_PER_STEP + t

        @pl.when(i >= 1)
        def _(buf=buf, t=t):
            _tile_runs_wait(buf, sem.at[t])

        send(jnp.maximum(tile - 1, 0), (t - 1) % RT_PER_STEP, tile >= 1)
        pos = _packed_positions(ri_ref[rows, :], offrow_ref[t, 0:1, :])
        sel = jnp.zeros((RT, RT_ROWS), F32)
        for p in pos:
            sel = sel + (col == p).astype(F32)
        packed = lax.dot_general(sel.astype(BF16), h2_ref[rows, :], _TN, preferred_element_type=F32)
        for s in range(ROW_SLABS):
            buf[pl.ds(s, RT_ROWS, stride=ROW_SLABS), :] = packed[:, s * LANES:(s + 1) * LANES]

    @pl.when(i == n_steps - 1)
    def _():
        send(N_RT - 1, RT_PER_STEP - 1, True)
        for t in range(RT_PER_STEP):
            _tile_runs_wait(bufs[t], sem.at[t])


def _dispatch(tables, h2, route_i):
    grid_spec = pltpu.PrefetchScalarGridSpec(
        num_scalar_prefetch=6,
        grid=(N_RT // RT_PER_STEP,),
        in_specs=[
            pl.BlockSpec((RT_PER_STEP * RT, D_MODEL), lambda i, *_: (i, 0)),
            pl.BlockSpec((RT_PER_STEP * RT, LANES), lambda i, *_: (i, 0)),
            pl.BlockSpec((RT_PER_STEP, SUBLANES, LANES), lambda i, *_: (i, 0, 0)),
        ],
        out_specs=pl.BlockSpec(memory_space=pl.ANY),
        scratch_shapes=[pltpu.VMEM((RT_ROWS * ROW_SLABS, LANES), F32)] * RT_PER_STEP + [
            pltpu.VMEM((BM * ROW_SLABS, LANES), F32),
            pltpu.SemaphoreType.DMA((RT_PER_STEP,)),
            pltpu.SemaphoreType.DMA(()),
        ],
    )
    return pl.pallas_call(
        _dispatch_kernel,
        grid_spec=grid_spec,
        out_shape=jax.ShapeDtypeStruct((N_SLOTS * ROW_SLABS, LANES), F32),
        compiler_params=_cparams(1),
        name="moe_dispatch",
    )(tables["start"], tables["len"], tables["off"], tables["pad_start"], tables["pad_len"], tables["n_valid"],
      h2, route_i, tables["off_rows"])


def _moe_kernel(be_ref, eo_ref, nv_ref, xs_hbm, wgu_hbm, bgu_ref, wdn_hbm, bdn_ref, y_hbm,
                xbuf, ybuf, gu_stage, dn_stage, wgu_bf, wdn_bf, wsem, xsem, ysem, *, li):
    n_valid = nv_ref[0]
    rows_per_piece = D_MODEL // WEIGHT_PIECES

    def weight_copies(e, stage):
        cps = []
        for p in range(WEIGHT_PIECES):
            band = pl.ds(p * rows_per_piece, rows_per_piece)
            cps.append(pltpu.make_async_copy(wgu_hbm.at[li, e, band, :], gu_stage.at[stage, band, :], wsem.at[stage]))
            cps.append(pltpu.make_async_copy(wdn_hbm.at[li, e, band, :], dn_stage.at[stage, band, :], wsem.at[stage]))
        return cps

    def fetch(k):
        @pl.when(eo_ref[k] >= 0)
        def _():
            for cp in weight_copies(eo_ref[k], k % 2):
                cp.start(priority=1)

    def x_copy(b, slot):
        return pltpu.make_async_copy(xs_hbm.at[_row_ds(b * BM, BM), :], xbuf.at[slot], xsem.at[slot])

    def y_copy(b, slot):
        return pltpu.make_async_copy(ybuf.at[slot], y_hbm.at[_row_ds(b * BM, BM), :], ysem.at[slot])

    fetch(0)
    fetch(1)
    x_copy(0, 0).start()

    def block(b, k):
        slot = b % 2
        e = be_ref[b]
        new_expert = jnp.logical_or(b == 0, e != be_ref[jnp.maximum(b - 1, 0)])

        @pl.when(b + 1 < n_valid)
        def _():
            x_copy(b + 1, 1 - slot).start()

        @pl.when(new_expert)
        def _():
            stage = k % 2
            for cp in weight_copies(e, stage):
                cp.wait()
            for st in range(2):
                @pl.when(stage == st)
                def _(st=st):
                    wgu_bf[...] = gu_stage[st].astype(BF16)
                    wdn_bf[...] = dn_stage[st].astype(BF16)
            fetch(k + 2)

        x_copy(b, slot).wait()

        @pl.when(b >= 2)
        def _():
            y_copy(b - 2, slot).wait()

        x = _slab_columns(xbuf.at[slot], 0, BM).astype(BF16)
        bgu = bgu_ref[e]
        g = jnp.dot(x, wgu_bf[:, :D_EXPERT], preferred_element_type=F32) + bgu[:, :D_EXPERT]
        u = jnp.dot(x, wgu_bf[:, D_EXPERT:], preferred_element_type=F32) + bgu[:, D_EXPERT:]
        g = jnp.minimum(g, SWIGLU_LIMIT)
        u = jnp.clip(u, -SWIGLU_LIMIT, SWIGLU_LIMIT)
        act = ((u + 1.0) * (0.5 * g * (1.0 + jnp.tanh((0.5 * SWIGLU_ALPHA) * g)))).astype(BF16)
        y = jnp.dot(act, wdn_bf[...], preferred_element_type=F32) + bdn_ref[e]
        out = ybuf.at[slot]
        for s in range(ROW_SLABS):
            out[pl.ds(s, BM, stride=ROW_SLABS), :] = y[:, s * LANES:(s + 1) * LANES]
        y_copy(b, slot).start()
        return k + new_expert.astype(I32)

    lax.fori_loop(0, n_valid, block, jnp.int32(0))

    @pl.when(n_valid >= 2)
    def _():
        y_copy(n_valid - 2, n_valid % 2).wait()
    y_copy(n_valid - 1, (n_valid - 1) % 2).wait()

    ybuf[0] = jnp.zeros((BM * ROW_SLABS, LANES), F32)

    def zero_blocks(wait):
        def one(b, carry):
            cp = y_copy(b, 0)
            cp.wait() if wait else cp.start()
            return carry
        lax.fori_loop(n_valid, NB, one, 0)

    zero_blocks(False)
    zero_blocks(True)


def _moe_blocks(tables, xs, w_gu, b_gu, w_down, b_down, li):
    grid_spec = pltpu.PrefetchScalarGridSpec(
        num_scalar_prefetch=3,
        grid=(1,),
        in_specs=[
            pl.BlockSpec(memory_space=pl.ANY),
            pl.BlockSpec(memory_space=pl.ANY),
            pl.BlockSpec((None, N_EXPERTS, 1, 2 * D_EXPERT), lambda i, *_: (li, 0, 0, 0)),
            pl.BlockSpec(memory_space=pl.ANY),
            pl.BlockSpec((None, N_EXPERTS, 1, D_MODEL), lambda i, *_: (li, 0, 0, 0)),
        ],
        out_specs=pl.BlockSpec(memory_space=pl.ANY),
        scratch_shapes=[
            pltpu.VMEM((2, BM * ROW_SLABS, LANES), F32),
            pltpu.VMEM((2, BM * ROW_SLABS, LANES), F32),
            pltpu.VMEM((2, D_MODEL, 2 * D_EXPERT), F32),
            pltpu.VMEM((2, D_EXPERT, D_MODEL), F32),
            pltpu.VMEM((D_MODEL, 2 * D_EXPERT), BF16),
            pltpu.VMEM((D_EXPERT, D_MODEL), BF16),
            pltpu.SemaphoreType.DMA((2,)),
            pltpu.SemaphoreType.DMA((2,)),
            pltpu.SemaphoreType.DMA((2,)),
        ],
    )
    return pl.pallas_call(
        functools.partial(_moe_kernel, li=li),
        grid_spec=grid_spec,
        out_shape=jax.ShapeDtypeStruct((N_SLOTS * ROW_SLABS, LANES), F32),
        compiler_params=_cparams(1),
        name="moe_experts",
    )(tables["block_expert"], tables["expert_order"], tables["n_valid"], xs, w_gu,
      b_gu.reshape(DEPTH, N_EXPERTS, 1, 2 * D_EXPERT), w_down, b_down.reshape(DEPTH, N_EXPERTS, 1, D_MODEL))


def _combine_kernel(start_ref, len_ref, off_ref, y_hbm, x1_ref, ri_ref, rf_ref, offrow_ref, mod_ref,
                    lng_ref, lnb_ref, oc_ref, od_ref, *scratch):
    bufs, sem = scratch[:RT_PER_STEP], scratch[RT_PER_STEP]
    i = pl.program_id(0)
    n_steps = pl.num_programs(0)

    def fetch(tile, t):
        _tile_run_copies(tile, start_ref, len_ref, off_ref, y_hbm, bufs[t], sem.at[t], to_sorted=False)

    @pl.when(i == 0)
    def _():
        for t in range(FETCH_AHEAD):
            fetch(t, t)

    m = mod_ref[pl.ds(_mod_row(i * RT_PER_STEP // RT_PER_TM), 1), :]
    g2 = m[:, 5 * D_MODEL:6 * D_MODEL]
    col = lax.broadcasted_iota(I32, (RT, RT_ROWS), 1).astype(F32)
    outs = []
    for t in range(RT_PER_STEP):
        rows = slice(t * RT, (t + 1) * RT)
        fetch(jnp.minimum(i * RT_PER_STEP + t + FETCH_AHEAD, N_RT - 1), (t + FETCH_AHEAD) % RT_PER_STEP)
        _tile_runs_wait(bufs[t], sem.at[t])
        pos = _packed_positions(ri_ref[rows, :], offrow_ref[t, 0:1, :])
        gates = rf_ref[rows, :]
        mix = jnp.zeros((RT, RT_ROWS), F32)
        for k in range(TOP_K):
            mix = mix + jnp.where(col == pos[k], gates[:, k:k + 1], 0.0)
        ffn = jnp.dot(mix.astype(BF16), _slab_columns(bufs[t], 0, RT_ROWS).astype(BF16),
                      preferred_element_type=F32)
        outs.append(_layer_norm_rows(DN_ALPHA * x1_ref[rows, :] + g2 * ffn, lng_ref[...], lnb_ref[...]))
    out = jnp.concatenate(outs, axis=0)
    n_ctx_steps = N_CTX // (RT * RT_PER_STEP)

    @pl.when(i < n_ctx_steps)
    def _():
        oc_ref[...] = out

    @pl.when(i >= n_ctx_steps)
    def _():
        od_ref[...] = out

    @pl.when(i == n_steps - 1)
    def _():
        for t in range(FETCH_AHEAD):
            _tile_runs_wait(bufs[t], sem.at[t])


def _combine(tables, y, x1, route_i, route_f, mod, ln_g, ln_b, li):
    rows = RT * RT_PER_STEP
    grid_spec = pltpu.PrefetchScalarGridSpec(
        num_scalar_prefetch=3,
        grid=(N_RT // RT_PER_STEP,),
        in_specs=[
            pl.BlockSpec(memory_space=pl.ANY),
            pl.BlockSpec((rows, D_MODEL), lambda i, *_: (i, 0)),
            pl.BlockSpec((rows, LANES), lambda i, *_: (i, 0)),
            pl.BlockSpec((rows, LANES), lambda i, *_: (i, 0)),
            pl.BlockSpec((RT_PER_STEP, SUBLANES, LANES), lambda i, *_: (i, 0, 0)),
            pl.BlockSpec((None, SUBLANES, 6 * D_MODEL), lambda i, *_: (li, 0, 0)),
            pl.BlockSpec((1, D_MODEL), lambda i, *_: (0, 0)),
            pl.BlockSpec((1, D_MODEL), lambda i, *_: (0, 0)),
        ],
        out_specs=_pair_specs(rows, D_MODEL, N_CTX // rows),
        scratch_shapes=[pltpu.VMEM((RT_ROWS * ROW_SLABS, LANES), F32)] * RT_PER_STEP + [
            pltpu.SemaphoreType.DMA((RT_PER_STEP,)),
        ],
    )
    return pl.pallas_call(
        _combine_kernel,
        grid_spec=grid_spec,
        out_shape=[jax.ShapeDtypeStruct((N_CTX, D_MODEL), F32), jax.ShapeDtypeStruct((N_DEC, D_MODEL), F32)],
        compiler_params=_cparams(1),
        name="moe_combine",
    )(tables["start"], tables["len"], tables["off"], y, x1, route_i, route_f, tables["off_rows"], mod, ln_g, ln_b)


def _routing_tables(tile_cnt):
    cnt = tile_cnt[:, 0, :N_EXPERTS]
    totals = jnp.sum(cnt, axis=0)
    padded = (totals + BM - 1) // BM * BM
    pends = jnp.cumsum(padded)
    pstarts = pends - padded
    start = pstarts[None, :] + jnp.cumsum(cnt, axis=0) - cnt
    off = jnp.cumsum(cnt, axis=1) - cnt
    n_valid = (pends[-1] // BM).astype(I32)
    block_start = jnp.arange(NB, dtype=I32) * BM
    block_e = jnp.minimum(jnp.sum(block_start[:, None] >= pends[None, :], axis=1), N_EXPERTS - 1)
    last_e = block_e[jnp.maximum(n_valid - 1, 0)]
    block_e = jnp.where(jnp.arange(NB) < n_valid, block_e, last_e)
    ids = jnp.arange(N_EXPERTS, dtype=I32)
    order = jnp.sort(jnp.where(totals > 0, ids, N_EXPERTS))
    order = jnp.concatenate([jnp.where(order < N_EXPERTS, order, -1), jnp.full((2,), -1, I32)])
    off_rows = jnp.pad(off.astype(F32), ((0, 0), (0, LANES - N_EXPERTS)))
    return {
        "start": start.reshape(-1).astype(I32),
        "len": cnt.reshape(-1).astype(I32),
        "off": off.reshape(-1).astype(I32),
        "pad_start": (pstarts + totals).astype(I32),
        "pad_len": (padded - totals).astype(I32),
        "n_valid": n_valid.reshape(1),
        "block_expert": block_e.astype(I32),
        "expert_order": order.astype(I32),
        "off_rows": jnp.broadcast_to(off_rows[:, None, :], (N_RT, SUBLANES, LANES)),
    }


def _layer_params(p, li):
    eye = jnp.eye(N_LRU_BLOCKS, dtype=F32)
    w_rg_full = jnp.einsum("dkgio,gh->gidkho", p["w_rg"][li], eye).reshape(LRU_WIDTH, 4 * LRU_WIDTH)
    pad = LANES - N_EXPERTS
    return {
        "sgu_ln_g": p["sgu_ln_g"][li].reshape(1, SGU_WIDTH),
        "sgu_ln_b": p["sgu_ln_b"][li].reshape(1, SGU_WIDTH),
        "w_spatial": p["w_spatial"][li],
        "b_spatial_full": jnp.repeat(p["b_spatial"][li].T, SGU_GROUP, axis=1),
        "conv_w": p["conv_w"][li],
        "conv_b": p["conv_b"][li].reshape(1, LRU_WIDTH),
        "w_rg_full": w_rg_full,
        "b_rg_full": p["b_rg"][li].reshape(1, 4 * LRU_WIDTH),
        "lru_log": p["lru_log"][li],
        "w_out_all": p["w_out"],
        "ln_g1": p["ln_g"][li, 0].reshape(1, D_MODEL),
        "ln_b1": p["ln_b"][li, 0].reshape(1, D_MODEL),
        "ln_g2": p["ln_g"][li, 1].reshape(1, D_MODEL),
        "ln_b2": p["ln_b"][li, 1].reshape(1, D_MODEL),
        "w_router_pad": jnp.pad(p["w_router"][li], ((0, 0), (0, pad))),
        "b_router_pad": jnp.pad(p["b_router"][li], (0, pad)).reshape(1, LANES),
    }


def kernel(x_prompt, x_sample, cache_k, cache_v, state_lru, c, c_ctx, w_mod, b_mod, w_in, lam, subln_g, sgu_ln_g, sgu_ln_b, w_spatial, b_spatial, conv_w, conv_b, w_rg, b_rg, lru_log, w_out, ln_g, ln_b, w_router, b_router, w_gu, b_gu, w_down, b_down):
    p = dict(sgu_ln_g=sgu_ln_g, sgu_ln_b=sgu_ln_b, w_spatial=w_spatial, b_spatial=b_spatial, conv_w=conv_w,
             conv_b=conv_b, w_rg=w_rg, b_rg=b_rg, lru_log=lru_log, w_out=w_out, ln_g=ln_g, ln_b=ln_b,
             w_router=w_router, b_router=b_router)
    cvec8 = jnp.concatenate([c_ctx[None, :], c, jnp.zeros((SUBLANES - 1 - DEC_BATCH, D_MODEL), F32)], axis=0)
    mod = _modulation(cvec8, w_mod, b_mod)
    x_pair = (x_prompt.reshape(N_CTX, D_MODEL), x_sample.reshape(N_DEC, D_MODEL))
    tables = _rope_tables()
    zero_state = jnp.zeros((BATCH, 1, 2, LRU_WIDTH), F32)
    prev_kv = None
    new_s = []
    for li in range(DEPTH):
        lp = _layer_params(p, li)
        lam_init = 0.8 - 0.6 * math.exp(-0.3 * li)
        proj = _input_projection(*x_pair, mod, w_in, li)
        att_ctx, kc, vc = _context_attention(proj, lam[li], subln_g[li], lam_init, prev_kv)
        prev_kv = (kc, vc)
        att_dec = _denoise_attention(proj, cache_k, cache_v, lam[li], subln_g[li], li, lam_init, tables)
        sgu_ctx, lru_ctx, h_ctx = _mixers(proj, zero_state, 0, lp, SEQ, BATCH, 0)
        sgu_dec, lru_dec, _ = _mixers(proj, state_lru, li, lp, DEC_SEQ, DEC_BATCH, N_CTX // DEC_SEQ)
        x1, h2, route_i, route_f, tile_cnt = _output_projection(
            x_pair, (att_ctx, att_dec), (sgu_ctx, sgu_dec), (lru_ctx, lru_dec), mod, lp, li)
        rt = _routing_tables(tile_cnt)
        xs = _dispatch(rt, h2, route_i)
        y = _moe_blocks(rt, xs, w_gu, b_gu, w_down, b_down, li)
        x_pair = _combine(rt, y, x1, route_i, route_f, mod, lp["ln_g2"], lp["ln_b2"], li)
        new_s.append(h_ctx)
    y_prompt = x_pair[0].reshape(BATCH, SEQ, D_MODEL)
    y_sample = x_pair[1].reshape(DEC_BATCH, DEC_SEQ, D_MODEL)
    return (y_prompt, y_sample, prev_kv[0], prev_kv[1], jnp.stack(new_s, axis=1))
```

```python
import functools
import math

import numpy as np
import jax
import jax.numpy as jnp
from jax import lax
from jax.experimental import pallas as pl
from jax.experimental.pallas import tpu as pltpu

F32 = jnp.float32
BF16 = jnp.bfloat16
I32 = jnp.int32

D_MODEL = 1024
BATCH = 32
SEQ = 256
DEPTH = 2
DEC_BATCH = 2
DEC_SEQ = 1024
PAST_LEN = 256
GRID_W = 64
HEAD_DIM = 64
ATT_WIDTH = D_MODEL // 2
SGU_WIDTH = D_MODEL // 4
LRU_WIDTH = D_MODEL // 4
N_ATT_HEADS = ATT_WIDTH // HEAD_DIM
ATT_HALF = HEAD_DIM // 2
ROPE_FREQS = ATT_HALF // 4
ROPE_THETA = 10000.0
CHUNK = 128
N_SGU_GROUPS = 4
SGU_GROUP = SGU_WIDTH // N_SGU_GROUPS
N_LRU_BLOCKS = 4
LRU_BLOCK = LRU_WIDTH // N_LRU_BLOCKS
CONV_W = 4
LRU_C = 8.0
IN_COLS = 3 * ATT_WIDTH + 2 * SGU_WIDTH + 2 * LRU_WIDTH
N_EXPERTS = 32
TOP_K = 4
D_EXPERT = D_MODEL
SWIGLU_LIMIT = 7.0
SWIGLU_ALPHA = 1.702
DN_ALPHA = (2 * DEPTH) ** 0.25
EPS = 1e-5

N_CTX = BATCH * SEQ
N_DEC = DEC_BATCH * DEC_SEQ
N_TOK = N_CTX + N_DEC

LANES = 128
SUBLANES = 8
ROW_SLABS = D_MODEL // LANES

TM = 512
CTX_TILES = N_CTX // TM
TILES_PER_DEC = DEC_SEQ // TM
N_TILES = N_TOK // TM
MOD_TN = 512
RT = 256
N_RT = N_TOK // RT
RT_PER_TM = TM // RT
RT_ROWS = RT * TOP_K
RT_PER_STEP = 4
FETCH_AHEAD = 2
BM = 256
NB = N_TOK * TOP_K // BM + N_EXPERTS
WEIGHT_PIECES = 4
X_BUFFERS = 3
N_SLOTS = NB * BM
RUN_BITS = tuple(1 << b for b in range(RT.bit_length() - 1, -1, -1))
PAD_BITS = tuple(1 << b for b in range(BM.bit_length() - 2, -1, -1))
VMEM_LIMIT = 56 * 1024 * 1024


def _cparams(n_axes):
    return pltpu.CompilerParams(
        dimension_semantics=("arbitrary",) * n_axes,
        vmem_limit_bytes=VMEM_LIMIT)


def _mod_row(i):
    return jnp.where(i < CTX_TILES, 0, 1 + (i - CTX_TILES) // TILES_PER_DEC)


def _layer_norm_rows(z, g, b):
    mu = jnp.mean(z, axis=-1, keepdims=True)
    zc = z - mu
    var = jnp.mean(zc * zc, axis=-1, keepdims=True)
    return zc * lax.rsqrt(var + EPS) * g + b


def _pair_specs(tile, width, n_ctx_tiles):
    ctx = pl.BlockSpec((tile, width), lambda i, *_: (jnp.minimum(i, n_ctx_tiles - 1), 0))
    dec = pl.BlockSpec((tile, width), lambda i, *_: (jnp.maximum(i - n_ctx_tiles, 0), 0))
    return [ctx, dec]


def _pair_read(i, n_ctx_tiles, ctx_ref, dec_ref):
    return jnp.where(i < n_ctx_tiles, ctx_ref[...], dec_ref[...])


def _mod_kernel(cvec_ref, w_ref, b_ref, o_ref):
    cv = cvec_ref[...]
    s = cv * jax.nn.sigmoid(cv)
    s_t = s.T
    w = w_ref[0]
    rows = [jnp.sum(s_t[:, r:r + 1] * w, axis=0, keepdims=True) for r in range(1 + DEC_BATCH)]
    rows.append(jnp.zeros((SUBLANES - 1 - DEC_BATCH, MOD_TN), F32))
    o_ref[0] = jnp.concatenate(rows, axis=0) + b_ref[0]


def _modulation(cvec8, w_mod, b_mod):
    n_out = w_mod.shape[-1]
    return pl.pallas_call(
        _mod_kernel,
        grid=(DEPTH, n_out // MOD_TN),
        in_specs=[
            pl.BlockSpec((SUBLANES, D_MODEL), lambda l, j: (0, 0)),
            pl.BlockSpec((1, D_MODEL, MOD_TN), lambda l, j: (l, 0, j)),
            pl.BlockSpec((1, 1, MOD_TN), lambda l, j: (l, 0, j)),
        ],
        out_specs=pl.BlockSpec((1, SUBLANES, MOD_TN), lambda l, j: (l, 0, j)),
        out_shape=jax.ShapeDtypeStruct((DEPTH, SUBLANES, n_out), F32),
        compiler_params=_cparams(2),
        name="modulation",
    )(cvec8, w_mod, b_mod.reshape(DEPTH, 1, n_out))


def _inproj_kernel(xc_ref, xd_ref, mod_ref, w_ref, o_ref, wbf_ref):
    i = pl.program_id(0)

    @pl.when(i == 0)
    def _():
        wbf_ref[...] = w_ref[...].astype(BF16)

    m = mod_ref[pl.ds(_mod_row(i), 1), :]
    sh1 = m[:, 0:D_MODEL]
    sc1 = m[:, D_MODEL:2 * D_MODEL]
    x = _pair_read(i, CTX_TILES, xc_ref, xd_ref)
    h = (x * (1.0 + sc1) + sh1).astype(BF16)
    o_ref[...] = jnp.dot(h, wbf_ref[...], preferred_element_type=F32)


def _input_projection(x_ctx, x_dec, mod, w_in, li):
    return pl.pallas_call(
        _inproj_kernel,
        grid=(N_TILES,),
        in_specs=_pair_specs(TM, D_MODEL, CTX_TILES) + [
            pl.BlockSpec((None, SUBLANES, 6 * D_MODEL), lambda i: (li, 0, 0)),
            pl.BlockSpec((None, D_MODEL, IN_COLS), lambda i: (li, 0, 0), pipeline_mode=pl.Buffered(1)),
        ],
        out_specs=pl.BlockSpec((TM, IN_COLS), lambda i: (i, 0)),
        out_shape=jax.ShapeDtypeStruct((N_TOK, IN_COLS), F32),
        scratch_shapes=[pltpu.VMEM((D_MODEL, IN_COLS), BF16)],
        compiler_params=_cparams(1),
        name="input_projection",
    )(x_ctx, x_dec, mod, w_in)


def _lambda(lam_ref, lam_init):
    lm = lam_ref[...]
    a = jnp.sum(lm[0:1] * lm[1:2], axis=-1, keepdims=True)
    b = jnp.sum(lm[2:3] * lm[3:4], axis=-1, keepdims=True)
    return jnp.exp(a) - jnp.exp(b) + lam_init


_NT = (((1,), (1,)), ((), ()))
_TN = (((0,), (0,)), ((), ()))
_ATT_SCALE = ATT_HALF ** -0.5
_Q_SCALE = _ATT_SCALE * math.log2(math.e)
KEY_CHUNK = 128


def _attention_heads(q_parts, k_segs, values, s_scr, e_scr, lam, g_col, lam_init):
    n_heads = len(q_parts)
    for h in range(n_heads):
        for c in range(2):
            row = 0
            for k in k_segs[h][c]:
                s_scr[2 * h + c, row:row + k.shape[0], :] = lax.dot_general(
                    k, q_parts[h][c], _NT, preferred_element_type=F32)
                row += k.shape[0]
    n_keys, n_q = s_scr.shape[1:]
    chunks = [slice(r, r + KEY_CHUNK) for r in range(0, n_keys, KEY_CHUNK)]
    for n in range(2 * n_heads):
        tops = [jnp.max(s_scr[n, rows, :].reshape(KEY_CHUNK // SUBLANES, SUBLANES, n_q), axis=0)
                for rows in chunks]
        top = jnp.max(functools.reduce(jnp.maximum, tops), axis=0, keepdims=True)
        for rows in chunks:
            e_scr[n, rows, :] = jnp.exp2(s_scr[n, rows, :] - top).astype(BF16)
    outs = []
    for h in range(n_heads):
        v_aug = jnp.concatenate([values[h], jnp.ones_like(values[h])], axis=-1)
        normed = []
        for c in range(2):
            acc = lax.dot_general(v_aug, e_scr[2 * h + c], _TN, preferred_element_type=F32)
            normed.append(acc[:HEAD_DIM] * (1.0 / acc[HEAD_DIM:HEAD_DIM + 1]))
        o_t = normed[0] - lam * normed[1]
        ms = jnp.mean(o_t * o_t, axis=0, keepdims=True)
        outs.append(o_t * lax.rsqrt(ms + EPS) * g_col * (1.0 - lam_init))
    return outs


def _head_cols(h):
    lo = h * HEAD_DIM
    return [slice(lo + c * ATT_HALF, lo + (c + 1) * ATT_HALF) for c in range(2)]


def _ctx_attn_kernel(lam_ref, g_ref, q_ref, k_ref, v_ref, *rest, lam_init, stacked):
    *rest, s_scr, e_scr = rest
    if stacked:
        pk_ref, pv_ref, att_ref, ck_ref, cv_ref = rest
        ck_ref[0, 0] = pk_ref[0]
        cv_ref[0, 0] = pv_ref[0]
        put_k = lambda h, val: ck_ref.__setitem__((0, 1, h), val)
        put_v = lambda h, val: cv_ref.__setitem__((0, 1, h), val)
    else:
        att_ref, ck_ref, cv_ref = rest
        put_k = lambda h, val: ck_ref.__setitem__((0, h), val)
        put_v = lambda h, val: cv_ref.__setitem__((0, h), val)
    lam = _lambda(lam_ref, lam_init)
    k = k_ref[...]
    v = v_ref[...]
    qb = (q_ref[...] * _Q_SCALE).astype(BF16)
    kb = k.astype(BF16)
    vb = v.astype(BF16)
    heads = range(N_ATT_HEADS)
    outs = _attention_heads([[qb[:, cs] for cs in _head_cols(h)] for h in heads],
                            [[[kb[:, cs]] for cs in _head_cols(h)] for h in heads],
                            [vb[:, h * HEAD_DIM:(h + 1) * HEAD_DIM] for h in heads],
                            s_scr, e_scr, lam, g_ref[...], lam_init)
    for h in heads:
        put_k(h, k[:, h * HEAD_DIM:(h + 1) * HEAD_DIM])
        put_v(h, v[:, h * HEAD_DIM:(h + 1) * HEAD_DIM])
    att_ref[...] = jnp.concatenate(outs, axis=0).T


def _context_attention(proj, lam, subln_g, lam_init, prev_kv=None):
    stacked = prev_kv is not None
    per_layer = pl.BlockSpec((1, N_ATT_HEADS, SEQ, HEAD_DIM), lambda b: (b, 0, 0, 0))
    if stacked:
        kv_shape = jax.ShapeDtypeStruct((BATCH, DEPTH, N_ATT_HEADS, SEQ, HEAD_DIM), F32)
        kv_spec = pl.BlockSpec((1, DEPTH, N_ATT_HEADS, SEQ, HEAD_DIM), lambda b: (b, 0, 0, 0, 0))
    else:
        kv_shape = jax.ShapeDtypeStruct((BATCH, N_ATT_HEADS, SEQ, HEAD_DIM), F32)
        kv_spec = per_layer
    return pl.pallas_call(
        functools.partial(_ctx_attn_kernel, lam_init=lam_init, stacked=stacked),
        grid=(BATCH,),
        in_specs=[
            pl.BlockSpec((4, ATT_HALF), lambda b: (0, 0)),
            pl.BlockSpec((HEAD_DIM, 1), lambda b: (0, 0)),
            pl.BlockSpec((SEQ, ATT_WIDTH), lambda b: (b, 0)),
            pl.BlockSpec((SEQ, ATT_WIDTH), lambda b: (b, 1)),
            pl.BlockSpec((SEQ, ATT_WIDTH), lambda b: (b, 2)),
        ] + ([per_layer, per_layer] if stacked else []),
        out_specs=[pl.BlockSpec((SEQ, ATT_WIDTH), lambda b: (b, 0)), kv_spec, kv_spec],
        out_shape=[jax.ShapeDtypeStruct((N_CTX, ATT_WIDTH), F32), kv_shape, kv_shape],
        scratch_shapes=[pltpu.VMEM((2 * N_ATT_HEADS, SEQ, SEQ), F32),
                        pltpu.VMEM((2 * N_ATT_HEADS, SEQ, SEQ), BF16)],
        compiler_params=_cparams(1),
        name="context_attention",
    )(lam, subln_g.reshape(HEAD_DIM, 1), proj, proj, proj, *(prev_kv if stacked else ()))


def _rope_tables():
    t = np.arange(DEC_SEQ)
    pos = np.stack([t // GRID_W, t % GRID_W], axis=1).astype(np.float32)
    inv = (np.float32(ROPE_THETA) ** (-np.arange(ROPE_FREQS, dtype=np.float32) / np.float32(ROPE_FREQS)))
    j = np.arange(HEAD_DIM)
    d = j % ATT_HALF
    axis = d // (2 * ROPE_FREQS)
    u = d % (2 * ROPE_FREQS)
    ang = pos[:, axis] * inv[u % ROPE_FREQS][None, :].astype(np.float32)
    cos = np.cos(ang).astype(np.float32)
    sin = np.sin(ang).astype(np.float32)
    first = (u < ROPE_FREQS)[None, :]
    s_next = np.where(first, -sin, 0.0).astype(np.float32)
    s_prev = np.where(first, 0.0, sin).astype(np.float32)
    tile = lambda a: jnp.asarray(np.tile(a, (1, N_ATT_HEADS)))
    return tile(cos), tile(s_next), tile(s_prev)


def _rotate(x, cos, s_next, s_prev):
    width = x.shape[-1]
    return (x * cos + pltpu.roll(x, width - ROPE_FREQS, axis=1) * s_next
            + pltpu.roll(x, ROPE_FREQS, axis=1) * s_prev)


QB = 256
Q_STEPS = DEC_SEQ // QB
DEC_HEAD_GROUP = 2


def _dec_attn_kernel(lam_ref, g_ref, q_ref, k_ref, v_ref, ck_ref, cv_ref,
                     cq_ref, snq_ref, spq_ref, ck_tab, snk_tab, spk_tab,
                     att_ref, krot_ref, s_scr, e_scr, *, lam_init):
    j = pl.program_id(1)

    @pl.when(j == 0)
    def _():
        krot_ref[...] = _rotate(k_ref[...], ck_tab[...], snk_tab[...], spk_tab[...]).astype(BF16)

    lam = _lambda(lam_ref, lam_init)
    qb = (_rotate(q_ref[...], cq_ref[...], snq_ref[...], spq_ref[...]) * _Q_SCALE).astype(BF16)
    kb = krot_ref[...]
    vb = v_ref[...].astype(BF16)
    outs = []
    for h0 in range(0, N_ATT_HEADS, DEC_HEAD_GROUP):
        heads = range(h0, h0 + DEC_HEAD_GROUP)
        past_k = [ck_ref[0, 0, h].astype(BF16) for h in heads]
        k_segs = [[[pk[:, c * ATT_HALF:(c + 1) * ATT_HALF], kb[:, _head_cols(h)[c]]] for c in range(2)]
                  for h, pk in zip(heads, past_k)]
        values = [jnp.concatenate([cv_ref[0, 0, h].astype(BF16), vb[:, h * HEAD_DIM:(h + 1) * HEAD_DIM]], axis=0)
                  for h in heads]
        outs += _attention_heads([[qb[:, cs] for cs in _head_cols(h)] for h in heads], k_segs, values,
                                 s_scr, e_scr, lam, g_ref[...], lam_init)
    att_ref[...] = jnp.concatenate(outs, axis=0).T


def _denoise_attention(proj, cache_k, cache_v, lam, subln_g, li, lam_init, tables):
    cos, s_next, s_prev = tables
    row0 = N_CTX // QB
    seq0 = N_CTX // DEC_SEQ
    q_tab = pl.BlockSpec((QB, ATT_WIDTH), lambda b, j: (j, 0))
    k_tab = pl.BlockSpec((DEC_SEQ, ATT_WIDTH), lambda b, j: (0, 0))
    cache_spec = pl.BlockSpec((1, 1, N_ATT_HEADS, PAST_LEN, HEAD_DIM), lambda b, j: (b, li, 0, 0, 0))
    return pl.pallas_call(
        functools.partial(_dec_attn_kernel, lam_init=lam_init),
        grid=(DEC_BATCH, Q_STEPS),
        in_specs=[
            pl.BlockSpec((4, ATT_HALF), lambda b, j: (0, 0)),
            pl.BlockSpec((HEAD_DIM, 1), lambda b, j: (0, 0)),
            pl.BlockSpec((QB, ATT_WIDTH), lambda b, j: (row0 + b * Q_STEPS + j, 0)),
            pl.BlockSpec((DEC_SEQ, ATT_WIDTH), lambda b, j: (seq0 + b, 1)),
            pl.BlockSpec((DEC_SEQ, ATT_WIDTH), lambda b, j: (seq0 + b, 2)),
            cache_spec, cache_spec,
            q_tab, q_tab, q_tab, k_tab, k_tab, k_tab,
        ],
        out_specs=pl.BlockSpec((QB, ATT_WIDTH), lambda b, j: (b * Q_STEPS + j, 0)),
        out_shape=jax.ShapeDtypeStruct((N_DEC, ATT_WIDTH), F32),
        scratch_shapes=[pltpu.VMEM((DEC_SEQ, ATT_WIDTH), BF16),
                        pltpu.VMEM((2 * DEC_HEAD_GROUP, PAST_LEN + DEC_SEQ, QB), F32),
                        pltpu.VMEM((2 * DEC_HEAD_GROUP, PAST_LEN + DEC_SEQ, QB), BF16)],
        compiler_params=_cparams(2),
        name="denoise_attention",
    )(lam, subln_g.reshape(HEAD_DIM, 1), proj, proj, proj, cache_k, cache_v,
      cos, s_next, s_prev, cos, s_next, s_prev)


def _softplus(z):
    return jnp.maximum(z, 0.0) + jnp.log1p(jnp.exp(-jnp.abs(z)))


def _mixer_kernel(su_ref, sv_ref, rx_ref, rg_ref, h0_ref, lng_ref, lnb_ref, ws_ref, bs_ref,
                  cw_ref, cb_ref, wrg_ref, brg_ref, lrulog_ref,
                  sgu_ref, lru_ref, hlast_ref, a_scr, b_scr, h_scr, *, seq_len):
    vn = _layer_norm_rows(sv_ref[...], lng_ref[...], lnb_ref[...])
    lane_group = lax.broadcasted_iota(I32, (CHUNK, SGU_WIDTH), 1) // SGU_GROUP
    for n in range(seq_len // CHUNK):
        rows = slice(n * CHUNK, (n + 1) * CHUNK)
        vc = vn[rows].astype(BF16)
        s = jnp.zeros((CHUNK, SGU_WIDTH), F32)
        for g in range(N_SGU_GROUPS):
            sg = jnp.dot(ws_ref[g].astype(BF16), vc, preferred_element_type=F32)
            s = jnp.where(lane_group == g, sg, s)
        sgu_ref[rows, :] = su_ref[rows, :] * (s + bs_ref[...])

    x = rx_ref[...]
    row = lax.broadcasted_iota(I32, (seq_len, LRU_WIDTH), 0)

    def shifted(val, d, fill):
        rolled = pltpu.roll(val, d % seq_len, axis=0)
        inside = (row >= d) if d > 0 else (row < seq_len + d)
        return jnp.where(inside, rolled, fill)

    left = CONV_W // 2
    xc = cb_ref[...] + x * cw_ref[left:left + 1, :]
    for tap in range(CONV_W):
        if tap != left:
            xc = xc + shifted(x, left - tap, 0.0) * cw_ref[tap:tap + 1, :]
    pre = jnp.dot(xc.astype(BF16), wrg_ref[...].astype(BF16), preferred_element_type=F32) + brg_ref[...]
    gates = 0.5 + 0.5 * jnp.tanh(0.5 * pre)
    in_chunk = row % SUBLANES
    n_chunks = seq_len // SUBLANES
    h0 = h0_ref[0, 0]
    lasts = []
    for direction in range(2):
        reverse = direction == 1
        base = direction * 2 * LRU_WIDTH
        r = gates[:, base:base + LRU_WIDTH]
        gi = gates[:, base + LRU_WIDTH:base + 2 * LRU_WIDTH]
        log_a = -LRU_C * r * _softplus(-lrulog_ref[direction:direction + 1, :])
        a = jnp.exp(log_a)
        b = jnp.sqrt(-jnp.tanh(log_a) * (a * a + 1.0)) * gi * xc
        for d in (1, 2, 4):
            if reverse:
                inside = in_chunk < SUBLANES - d
                a_n = jnp.where(inside, pltpu.roll(a, seq_len - d, axis=0), 1.0)
                b_n = jnp.where(inside, pltpu.roll(b, seq_len - d, axis=0), 0.0)
            else:
                inside = in_chunk >= d
                a_n = jnp.where(inside, pltpu.roll(a, d, axis=0), 1.0)
                b_n = jnp.where(inside, pltpu.roll(b, d, axis=0), 0.0)
            b = a * b_n + b
            a = a * a_n
        a_scr[...] = a
        b_scr[...] = b

        def chunk_step(c, carry, reverse=reverse):
            cc = n_chunks - 1 - c if reverse else c
            off = pl.multiple_of(cc * SUBLANES, SUBLANES)
            hc = a_scr[pl.ds(off, SUBLANES), :] * carry + b_scr[pl.ds(off, SUBLANES), :]
            if reverse:
                h_scr[pl.ds(off, SUBLANES), :] = h_scr[pl.ds(off, SUBLANES), :] + hc
                return hc[0:1, :]
            h_scr[pl.ds(off, SUBLANES), :] = hc
            return hc[SUBLANES - 1:SUBLANES, :]

        lasts.append(lax.fori_loop(0, n_chunks, chunk_step, h0[direction:direction + 1, :]))
    lru_ref[...] = h_scr[...] * jax.nn.gelu(rg_ref[...])
    hlast_ref[0] = jnp.concatenate(lasts, axis=0)


def _mixers(proj, h0, h0_layer, lp, seq_len, n_seq, row_block0):
    col0 = 3 * ATT_WIDTH // SGU_WIDTH
    col = lambda c: pl.BlockSpec((seq_len, SGU_WIDTH), lambda b: (row_block0 + b, col0 + c))
    full = lambda shape: pl.BlockSpec(shape, lambda b: (0,) * len(shape))
    out_rows = pl.BlockSpec((seq_len, SGU_WIDTH), lambda b: (b, 0))
    return pl.pallas_call(
        functools.partial(_mixer_kernel, seq_len=seq_len),
        grid=(n_seq,),
        in_specs=[
            col(0), col(1), col(2), col(3),
            pl.BlockSpec((1, 1, 2, LRU_WIDTH), lambda b: (b, h0_layer, 0, 0)),
            full((1, SGU_WIDTH)), full((1, SGU_WIDTH)),
            full((N_SGU_GROUPS, CHUNK, CHUNK)), full((CHUNK, SGU_WIDTH)),
            full((CONV_W, LRU_WIDTH)), full((1, LRU_WIDTH)),
            full((LRU_WIDTH, 4 * LRU_WIDTH)), full((1, 4 * LRU_WIDTH)),
            full((2, LRU_WIDTH)),
        ],
        out_specs=[out_rows, out_rows, pl.BlockSpec((1, 2, LRU_WIDTH), lambda b: (b, 0, 0))],
        out_shape=[
            jax.ShapeDtypeStruct((n_seq * seq_len, SGU_WIDTH), F32),
            jax.ShapeDtypeStruct((n_seq * seq_len, LRU_WIDTH), F32),
            jax.ShapeDtypeStruct((n_seq, 2, LRU_WIDTH), F32),
        ],
        scratch_shapes=[pltpu.VMEM((seq_len, LRU_WIDTH), F32)] * 3,
        compiler_params=_cparams(1),
        name="mixers_%d" % seq_len,
    )(proj, proj, proj, proj, h0, lp["sgu_ln_g"], lp["sgu_ln_b"], lp["w_spatial"], lp["b_spatial_full"],
      lp["conv_w"], lp["conv_b"], lp["w_rg_full"], lp["b_rg_full"], lp["lru_log"])


def _outproj_kernel(xc_ref, xd_ref, ac_ref, ad_ref, sc_ref, sd_ref, lc_ref, ld_ref,
                    mod_ref, wout_ref, lng_ref, lnb_ref, wr_ref, br_ref,
                    x1_ref, h2_ref, ri_ref, rf_ref, cnt_ref, wbf_ref):
    i = pl.program_id(0)

    @pl.when(i == 0)
    def _():
        wbf_ref[...] = wout_ref[...].astype(BF16)

    m = mod_ref[pl.ds(_mod_row(i), 1), :]
    g1 = m[:, 2 * D_MODEL:3 * D_MODEL]
    sh2 = m[:, 3 * D_MODEL:4 * D_MODEL]
    sc2 = m[:, 4 * D_MODEL:5 * D_MODEL]
    a0, a1 = ATT_WIDTH, ATT_WIDTH + SGU_WIDTH
    x = _pair_read(i, CTX_TILES, xc_ref, xd_ref)
    att = _pair_read(i, CTX_TILES, ac_ref, ad_ref).astype(BF16)
    sgu = _pair_read(i, CTX_TILES, sc_ref, sd_ref).astype(BF16)
    lru = _pair_read(i, CTX_TILES, lc_ref, ld_ref).astype(BF16)
    mix = (jnp.dot(att, wbf_ref[0:a0, :], preferred_element_type=F32)
           + jnp.dot(sgu, wbf_ref[a0:a1, :], preferred_element_type=F32)
           + jnp.dot(lru, wbf_ref[a1:, :], preferred_element_type=F32))
    x1 = _layer_norm_rows(DN_ALPHA * x + g1 * mix, lng_ref[...], lnb_ref[...])
    x1_ref[...] = x1
    h2 = x1 * (1.0 + sc2) + sh2
    h2_ref[...] = h2.astype(BF16)

    logits = jnp.dot(h2.astype(BF16), wr_ref[...].astype(BF16), preferred_element_type=F32) + br_ref[...]
    lane = lax.broadcasted_iota(I32, (TM, LANES), 1)
    lane_f = lane.astype(F32)
    neg_inf = jnp.float32(-jnp.inf)
    work = jnp.where(lane < N_EXPERTS, logits, neg_inf)
    vals, idxs = [], []
    for _ in range(TOP_K):
        top = jnp.max(work, axis=-1, keepdims=True)
        idx = jnp.min(jnp.where(work == top, lane_f, float(LANES)), axis=-1, keepdims=True)
        vals.append(top)
        idxs.append(idx)
        work = jnp.where(lane_f == idx, neg_inf, work)
    exps = [jnp.exp(v - vals[0]) for v in vals]
    denom = exps[0] + exps[1] + exps[2] + exps[3]
    onehot = jnp.zeros((TM, LANES), F32)
    for idx in idxs:
        onehot = onehot + (lane_f == idx).astype(F32)
    r_i = lax.broadcasted_iota(I32, (RT, RT), 0)
    c_i = lax.broadcasted_iota(I32, (RT, RT), 1)
    tri = (r_i > c_i).astype(F32).astype(BF16)
    before = jnp.concatenate(
        [jnp.dot(tri, onehot[t * RT:(t + 1) * RT].astype(BF16), preferred_element_type=F32)
         for t in range(RT_PER_TM)], axis=0)
    ri = jnp.zeros((TM, LANES), F32)
    rf = jnp.zeros((TM, LANES), F32)
    for k in range(TOP_K):
        rank = jnp.sum(jnp.where(lane_f == idxs[k], before, 0.0), axis=-1, keepdims=True)
        ri = jnp.where(lane == k, idxs[k], ri)
        ri = jnp.where(lane == TOP_K + k, rank, ri)
        rf = jnp.where(lane == k, exps[k] / denom, rf)
    ri_ref[...] = ri.astype(I32)
    rf_ref[...] = rf
    for t in range(RT_PER_TM):
        total = jnp.sum(onehot[t * RT:(t + 1) * RT], axis=0, keepdims=True)
        cnt_ref[t] = jnp.broadcast_to(total, (SUBLANES, LANES)).astype(I32)


def _output_projection(x_pair, att_pair, sgu_pair, lru_pair, mod, lp, li):
    rows = lambda w: pl.BlockSpec((TM, w), lambda i: (i, 0))
    full = lambda shape: pl.BlockSpec(shape, lambda i: (0,) * len(shape))
    return pl.pallas_call(
        _outproj_kernel,
        grid=(N_TILES,),
        in_specs=(_pair_specs(TM, D_MODEL, CTX_TILES) + _pair_specs(TM, ATT_WIDTH, CTX_TILES)
                  + _pair_specs(TM, SGU_WIDTH, CTX_TILES) + _pair_specs(TM, LRU_WIDTH, CTX_TILES) + [
            pl.BlockSpec((None, SUBLANES, 6 * D_MODEL), lambda i: (li, 0, 0)),
            pl.BlockSpec((None, D_MODEL, D_MODEL), lambda i: (li, 0, 0), pipeline_mode=pl.Buffered(1)),
            full((1, D_MODEL)), full((1, D_MODEL)),
            full((D_MODEL, LANES)), full((1, LANES)),
        ]),
        out_specs=[
            rows(D_MODEL), rows(D_MODEL), rows(LANES), rows(LANES),
            pl.BlockSpec((RT_PER_TM, SUBLANES, LANES), lambda i: (i, 0, 0)),
        ],
        out_shape=[
            jax.ShapeDtypeStruct((N_TOK, D_MODEL), F32),
            jax.ShapeDtypeStruct((N_TOK, D_MODEL), BF16),
            jax.ShapeDtypeStruct((N_TOK, LANES), I32),
            jax.ShapeDtypeStruct((N_TOK, LANES), F32),
            jax.ShapeDtypeStruct((N_RT, SUBLANES, LANES), I32),
        ],
        scratch_shapes=[pltpu.VMEM((D_MODEL, D_MODEL), BF16)],
        compiler_params=_cparams(1),
        name="output_projection",
    )(*x_pair, *att_pair, *sgu_pair, *lru_pair, mod, lp["w_out_all"], lp["ln_g1"], lp["ln_b1"],
      lp["w_router_pad"], lp["b_router_pad"])


def _row_ds(row, n_rows):
    return pl.ds(pl.multiple_of(row * ROW_SLABS, ROW_SLABS), n_rows * ROW_SLABS)


def _for_each_run_piece(length, bits, fn):
    for bit in bits:
        done = length & (-2 * bit)
        @pl.when((length & bit) != 0)
        def _(done=done, bit=bit):
            fn(done, bit)


def _tile_run_copies(tile, start_ref, len_ref, off_ref, sorted_hbm, buf, sem, to_sorted, live=None):
    for e in range(N_EXPERTS):
        t = tile * N_EXPERTS + e
        start, length, off = start_ref[t], len_ref[t], off_ref[t]
        if live is not None:
            length = jnp.where(live, length, 0)

        def piece(done, bit, start=start, off=off):
            packed = buf.at[_row_ds(off + done, bit), :]
            srt = sorted_hbm.at[_row_ds(start + done, bit), :]
            if to_sorted:
                pltpu.make_async_copy(packed, srt, sem).start()
            else:
                pltpu.make_async_copy(srt, packed, sem).start()

        _for_each_run_piece(length, RUN_BITS, piece)


def _tile_runs_wait(buf, sem):
    pltpu.make_async_copy(buf, buf, sem).wait()


def _packed_positions(ri, off_row):
    lane = lax.broadcasted_iota(I32, (RT, LANES), 1)
    pos = []
    for k in range(TOP_K):
        e_k = ri[:, k:k + 1]
        base = jnp.sum(jnp.where(lane == e_k, off_row, 0.0), axis=-1, keepdims=True)
        pos.append(base + ri[:, TOP_K + k:TOP_K + k + 1].astype(F32))
    return pos


def _slab_columns(buf, row0, n_rows):
    return jnp.concatenate(
        [buf[pl.ds(row0 * ROW_SLABS + s, n_rows, stride=ROW_SLABS), :] for s in range(ROW_SLABS)], axis=-1)


def _dispatch_kernel(start_ref, len_ref, off_ref, pstart_ref, plen_ref, nv_ref,
                     h2_ref, ri_ref, offrow_ref, xs_hbm, *scratch):
    bufs, (zbuf, sem, zsem) = scratch[:RT_PER_STEP], scratch[RT_PER_STEP:]
    i = pl.program_id(0)
    n_steps = pl.num_programs(0)

    @pl.when(i == 0)
    def _():
        zbuf[...] = jnp.zeros_like(zbuf)

        def zero_fill(wait):
            def go(cp):
                cp.wait() if wait else cp.start()

            def per_expert(e, carry):
                def piece(done, bit):
                    go(pltpu.make_async_copy(zbuf.at[pl.ds(0, bit * ROW_SLABS), :],
                                             xs_hbm.at[_row_ds(pstart_ref[e] + done, bit), :], zsem))
                _for_each_run_piece(plen_ref[e], PAD_BITS, piece)
                return carry
            lax.fori_loop(0, N_EXPERTS, per_expert, 0)

            def per_block(b, carry):
                go(pltpu.make_async_copy(zbuf, xs_hbm.at[_row_ds(b * BM, BM), :], zsem))
                return carry
            lax.fori_loop(nv_ref[0], NB, per_block, 0)

        zero_fill(False)
        zero_fill(True)

    def send(tile, t, live):
        _tile_run_copies(tile, start_ref, len_ref, off_ref, xs_hbm, bufs[t], sem.at[t], to_sorted=True, live=live)

    col = lax.broadcasted_iota(I32, (RT, RT_ROWS), 1).astype(F32)
    for t in range(RT_PER_STEP):
        buf = bufs[t]
        rows = slice(t * RT, (t + 1) * RT)
        tile = i * RT_PER_STEP + t

        @pl.when(i >= 1)
        def _(buf=buf, t=t):
            _tile_runs_wait(buf, sem.at[t])

        send(jnp.maximum(tile - 1, 0), (t - 1) % RT_PER_STEP, tile >= 1)
        pos = _packed_positions(ri_ref[rows, :], offrow_ref[t, 0:1, :])
        sel = jnp.zeros((RT, RT_ROWS), F32)
        for p in pos:
            sel = sel + (col == p).astype(F32)
        packed = lax.dot_general(sel.astype(BF16), h2_ref[rows, :], _TN, preferred_element_type=F32)
        for s in range(ROW_SLABS):
            buf[pl.ds(s, RT_ROWS, stride=ROW_SLABS), :] = packed[:, s * LANES:(s + 1) * LANES]

    @pl.when(i == n_steps - 1)
    def _():
        send(N_RT - 1, RT_PER_STEP - 1, True)
        for t in range(RT_PER_STEP):
            _tile_runs_wait(bufs[t], sem.at[t])


def _dispatch(tables, h2, route_i):
    grid_spec = pltpu.PrefetchScalarGridSpec(
        num_scalar_prefetch=6,
        grid=(N_RT // RT_PER_STEP,),
        in_specs=[
            pl.BlockSpec((RT_PER_STEP * RT, D_MODEL), lambda i, *_: (i, 0)),
            pl.BlockSpec((RT_PER_STEP * RT, LANES), lambda i, *_: (i, 0)),
            pl.BlockSpec((RT_PER_STEP, SUBLANES, LANES), lambda i, *_: (i, 0, 0)),
        ],
        out_specs=pl.BlockSpec(memory_space=pl.ANY),
        scratch_shapes=[pltpu.VMEM((RT_ROWS * ROW_SLABS, LANES), F32)] * RT_PER_STEP + [
            pltpu.VMEM((BM * ROW_SLABS, LANES), F32),
            pltpu.SemaphoreType.DMA((RT_PER_STEP,)),
            pltpu.SemaphoreType.DMA(()),
        ],
    )
    return pl.pallas_call(
        _dispatch_kernel,
        grid_spec=grid_spec,
        out_shape=jax.ShapeDtypeStruct((N_SLOTS * ROW_SLABS, LANES), F32),
        compiler_params=_cparams(1),
        name="moe_dispatch",
    )(tables["start"], tables["len"], tables["off"], tables["pad_start"], tables["pad_len"], tables["n_valid"],
      h2, route_i, tables["off_rows"])


def _moe_kernel(be_ref, eo_ref, nv_ref, xs_hbm, wgu_hbm, bgu_ref, wdn_hbm, bdn_ref, y_hbm,
                xbuf, ybuf, gu_stage, dn_stage, wgu_bf, wdn_bf, wsem, xsem, ysem, *, li):
    n_valid = nv_ref[0]
    rows_per_piece = D_MODEL // WEIGHT_PIECES

    def weight_copies(e, stage):
        cps = []
        for p in range(WEIGHT_PIECES):
            band = pl.ds(p * rows_per_piece, rows_per_piece)
            cps.append(pltpu.make_async_copy(wgu_hbm.at[li, e, band, :], gu_stage.at[stage, band, :], wsem.at[stage]))
            cps.append(pltpu.make_async_copy(wdn_hbm.at[li, e, band, :], dn_stage.at[stage, band, :], wsem.at[stage]))
        return cps

    def fetch(k):
        @pl.when(eo_ref[k] >= 0)
        def _():
            for cp in weight_copies(eo_ref[k], k % 2):
                cp.start(priority=1)

    def x_copy(b, slot):
        return pltpu.make_async_copy(xs_hbm.at[_row_ds(b * BM, BM), :], xbuf.at[slot], xsem.at[slot])

    def y_copy(b, slot):
        return pltpu.make_async_copy(ybuf.at[slot], y_hbm.at[_row_ds(b * BM, BM), :], ysem.at[slot])

    fetch(0)
    fetch(1)
    for ahead in range(X_BUFFERS - 1):
        @pl.when(ahead < n_valid)
        def _(ahead=ahead):
            x_copy(ahead, ahead).start()

    def next_slot(s):
        return jnp.where(s == X_BUFFERS - 1, 0, s + 1)

    def block(b, carry):
        k, xslot = carry
        slot = b % 2
        e = be_ref[b]
        new_expert = jnp.logical_or(b == 0, e != be_ref[jnp.maximum(b - 1, 0)])

        @pl.when(b + X_BUFFERS - 1 < n_valid)
        def _():
            ahead_slot = xslot
            for _ in range(X_BUFFERS - 1):
                ahead_slot = next_slot(ahead_slot)
            x_copy(b + X_BUFFERS - 1, ahead_slot).start()

        @pl.when(new_expert)
        def _():
            stage = k % 2
            for cp in weight_copies(e, stage):
                cp.wait()
            for st in range(2):
                @pl.when(stage == st)
                def _(st=st):
                    wgu_bf[...] = gu_stage[st].astype(BF16)
                    wdn_bf[...] = dn_stage[st].astype(BF16)
            fetch(k + 2)

        x_copy(b, xslot).wait()

        @pl.when(b >= 2)
        def _():
            y_copy(b - 2, slot).wait()

        x = _slab_columns(xbuf.at[xslot], 0, BM).astype(BF16)
        bgu = bgu_ref[e]
        g = jnp.dot(x, wgu_bf[:, :D_EXPERT], preferred_element_type=F32) + bgu[:, :D_EXPERT]
        u = jnp.dot(x, wgu_bf[:, D_EXPERT:], preferred_element_type=F32) + bgu[:, D_EXPERT:]
        g = jnp.minimum(g, SWIGLU_LIMIT)
        u = jnp.clip(u, -SWIGLU_LIMIT, SWIGLU_LIMIT)
        act = ((u + 1.0) * (0.5 * g * (1.0 + jnp.tanh((0.5 * SWIGLU_ALPHA) * g)))).astype(BF16)
        y = jnp.dot(act, wdn_bf[...], preferred_element_type=F32) + bdn_ref[e]
        out = ybuf.at[slot]
        for s in range(ROW_SLABS):
            out[pl.ds(s, BM, stride=ROW_SLABS), :] = y[:, s * LANES:(s + 1) * LANES]
        y_copy(b, slot).start(priority=1)
        return k + new_expert.astype(I32), next_slot(xslot)

    lax.fori_loop(0, n_valid, block, (jnp.int32(0), jnp.int32(0)))

    @pl.when(n_valid >= 2)
    def _():
        y_copy(n_valid - 2, n_valid % 2).wait()
    y_copy(n_valid - 1, (n_valid - 1) % 2).wait()

    ybuf[0] = jnp.zeros((BM * ROW_SLABS, LANES), F32)

    def zero_blocks(wait):
        def one(b, carry):
            cp = y_copy(b, 0)
            cp.wait() if wait else cp.start()
            return carry
        lax.fori_loop(n_valid, NB, one, 0)

    zero_blocks(False)
    zero_blocks(True)


def _moe_blocks(tables, xs, w_gu, b_gu, w_down, b_down, li):
    grid_spec = pltpu.PrefetchScalarGridSpec(
        num_scalar_prefetch=3,
        grid=(1,),
        in_specs=[
            pl.BlockSpec(memory_space=pl.ANY),
            pl.BlockSpec(memory_space=pl.ANY),
            pl.BlockSpec((None, N_EXPERTS, 1, 2 * D_EXPERT), lambda i, *_: (li, 0, 0, 0)),
            pl.BlockSpec(memory_space=pl.ANY),
            pl.BlockSpec((None, N_EXPERTS, 1, D_MODEL), lambda i, *_: (li, 0, 0, 0)),
        ],
        out_specs=pl.BlockSpec(memory_space=pl.ANY),
        scratch_shapes=[
            pltpu.VMEM((X_BUFFERS, BM * ROW_SLABS, LANES), F32),
            pltpu.VMEM((2, BM * ROW_SLABS, LANES), F32),
            pltpu.VMEM((2, D_MODEL, 2 * D_EXPERT), F32),
            pltpu.VMEM((2, D_EXPERT, D_MODEL), F32),
            pltpu.VMEM((D_MODEL, 2 * D_EXPERT), BF16),
            pltpu.VMEM((D_EXPERT, D_MODEL), BF16),
            pltpu.SemaphoreType.DMA((2,)),
            pltpu.SemaphoreType.DMA((X_BUFFERS,)),
            pltpu.SemaphoreType.DMA((2,)),
        ],
    )
    return pl.pallas_call(
        functools.partial(_moe_kernel, li=li),
        grid_spec=grid_spec,
        out_shape=jax.ShapeDtypeStruct((N_SLOTS * ROW_SLABS, LANES), F32),
        compiler_params=_cparams(1),
        name="moe_experts",
    )(tables["block_expert"], tables["expert_order"], tables["n_valid"], xs, w_gu,
      b_gu.reshape(DEPTH, N_EXPERTS, 1, 2 * D_EXPERT), w_down, b_down.reshape(DEPTH, N_EXPERTS, 1, D_MODEL))


def _combine_kernel(start_ref, len_ref, off_ref, y_hbm, x1_ref, ri_ref, rf_ref, offrow_ref, mod_ref,
                    lng_ref, lnb_ref, oc_ref, od_ref, *scratch):
    bufs, sem = scratch[:RT_PER_STEP], scratch[RT_PER_STEP]
    i = pl.program_id(0)
    n_steps = pl.num_programs(0)

    def fetch(tile, t):
        _tile_run_copies(tile, start_ref, len_ref, off_ref, y_hbm, bufs[t], sem.at[t], to_sorted=False)

    @pl.when(i == 0)
    def _():
        for t in range(FETCH_AHEAD):
            fetch(t, t)

    m = mod_ref[pl.ds(_mod_row(i * RT_PER_STEP // RT_PER_TM), 1), :]
    g2 = m[:, 5 * D_MODEL:6 * D_MODEL]
    col = lax.broadcasted_iota(I32, (RT, RT_ROWS), 1).astype(F32)
    outs = []
    for t in range(RT_PER_STEP):
        rows = slice(t * RT, (t + 1) * RT)
        fetch(jnp.minimum(i * RT_PER_STEP + t + FETCH_AHEAD, N_RT - 1), (t + FETCH_AHEAD) % RT_PER_STEP)
        _tile_runs_wait(bufs[t], sem.at[t])
        pos = _packed_positions(ri_ref[rows, :], offrow_ref[t, 0:1, :])
        gates = rf_ref[rows, :]
        mix = jnp.zeros((RT, RT_ROWS), F32)
        for k in range(TOP_K):
            mix = mix + jnp.where(col == pos[k], gates[:, k:k + 1], 0.0)
        ffn = jnp.dot(mix.astype(BF16), _slab_columns(bufs[t], 0, RT_ROWS).astype(BF16),
                      preferred_element_type=F32)
        outs.append(_layer_norm_rows(DN_ALPHA * x1_ref[rows, :] + g2 * ffn, lng_ref[...], lnb_ref[...]))
    out = jnp.concatenate(outs, axis=0)
    n_ctx_steps = N_CTX // (RT * RT_PER_STEP)

    @pl.when(i < n_ctx_steps)
    def _():
        oc_ref[...] = out

    @pl.when(i >= n_ctx_steps)
    def _():
        od_ref[...] = out

    @pl.when(i == n_steps - 1)
    def _():
        for t in range(FETCH_AHEAD):
            _tile_runs_wait(bufs[t], sem.at[t])


def _combine(tables, y, x1, route_i, route_f, mod, ln_g, ln_b, li):
    rows = RT * RT_PER_STEP
    grid_spec = pltpu.PrefetchScalarGridSpec(
        num_scalar_prefetch=3,
        grid=(N_RT // RT_PER_STEP,),
        in_specs=[
            pl.BlockSpec(memory_space=pl.ANY),
            pl.BlockSpec((rows, D_MODEL), lambda i, *_: (i, 0)),
            pl.BlockSpec((rows, LANES), lambda i, *_: (i, 0)),
            pl.BlockSpec((rows, LANES), lambda i, *_: (i, 0)),
            pl.BlockSpec((RT_PER_STEP, SUBLANES, LANES), lambda i, *_: (i, 0, 0)),
            pl.BlockSpec((None, SUBLANES, 6 * D_MODEL), lambda i, *_: (li, 0, 0)),
            pl.BlockSpec((1, D_MODEL), lambda i, *_: (0, 0)),
            pl.BlockSpec((1, D_MODEL), lambda i, *_: (0, 0)),
        ],
        out_specs=_pair_specs(rows, D_MODEL, N_CTX // rows),
        scratch_shapes=[pltpu.VMEM((RT_ROWS * ROW_SLABS, LANES), F32)] * RT_PER_STEP + [
            pltpu.SemaphoreType.DMA((RT_PER_STEP,)),
        ],
    )
    return pl.pallas_call(
        _combine_kernel,
        grid_spec=grid_spec,
        out_shape=[jax.ShapeDtypeStruct((N_CTX, D_MODEL), F32), jax.ShapeDtypeStruct((N_DEC, D_MODEL), F32)],
        compiler_params=_cparams(1),
        name="moe_combine",
    )(tables["start"], tables["len"], tables["off"], y, x1, route_i, route_f, tables["off_rows"], mod, ln_g, ln_b)


def _routing_tables(tile_cnt):
    cnt = tile_cnt[:, 0, :N_EXPERTS]
    totals = jnp.sum(cnt, axis=0)
    padded = (totals + BM - 1) // BM * BM
    pends = jnp.cumsum(padded)
    pstarts = pends - padded
    start = pstarts[None, :] + jnp.cumsum(cnt, axis=0) - cnt
    off = jnp.cumsum(cnt, axis=1) - cnt
    n_valid = (pends[-1] // BM).astype(I32)
    block_start = jnp.arange(NB, dtype=I32) * BM
    block_e = jnp.minimum(jnp.sum(block_start[:, None] >= pends[None, :], axis=1), N_EXPERTS - 1)
    last_e = block_e[jnp.maximum(n_valid - 1, 0)]
    block_e = jnp.where(jnp.arange(NB) < n_valid, block_e, last_e)
    ids = jnp.arange(N_EXPERTS, dtype=I32)
    order = jnp.sort(jnp.where(totals > 0, ids, N_EXPERTS))
    order = jnp.concatenate([jnp.where(order < N_EXPERTS, order, -1), jnp.full((2,), -1, I32)])
    off_rows = jnp.pad(off.astype(F32), ((0, 0), (0, LANES - N_EXPERTS)))
    return {
        "start": start.reshape(-1).astype(I32),
        "len": cnt.reshape(-1).astype(I32),
        "off": off.reshape(-1).astype(I32),
        "pad_start": (pstarts + totals).astype(I32),
        "pad_len": (padded - totals).astype(I32),
        "n_valid": n_valid.reshape(1),
        "block_expert": block_e.astype(I32),
        "expert_order": order.astype(I32),
        "off_rows": jnp.broadcast_to(off_rows[:, None, :], (N_RT, SUBLANES, LANES)),
    }


def _layer_params(p, li):
    eye = jnp.eye(N_LRU_BLOCKS, dtype=F32)
    w_rg_full = jnp.einsum("dkgio,gh->gidkho", p["w_rg"][li], eye).reshape(LRU_WIDTH, 4 * LRU_WIDTH)
    pad = LANES - N_EXPERTS
    return {
        "sgu_ln_g": p["sgu_ln_g"][li].reshape(1, SGU_WIDTH),
        "sgu_ln_b": p["sgu_ln_b"][li].reshape(1, SGU_WIDTH),
        "w_spatial": p["w_spatial"][li],
        "b_spatial_full": jnp.repeat(p["b_spatial"][li].T, SGU_GROUP, axis=1),
        "conv_w": p["conv_w"][li],
        "conv_b": p["conv_b"][li].reshape(1, LRU_WIDTH),
        "w_rg_full": w_rg_full,
        "b_rg_full": p["b_rg"][li].reshape(1, 4 * LRU_WIDTH),
        "lru_log": p["lru_log"][li],
        "w_out_all": p["w_out"],
        "ln_g1": p["ln_g"][li, 0].reshape(1, D_MODEL),
        "ln_b1": p["ln_b"][li, 0].reshape(1, D_MODEL),
        "ln_g2": p["ln_g"][li, 1].reshape(1, D_MODEL),
        "ln_b2": p["ln_b"][li, 1].reshape(1, D_MODEL),
        "w_router_pad": jnp.pad(p["w_router"][li], ((0, 0), (0, pad))),
        "b_router_pad": jnp.pad(p["b_router"][li], (0, pad)).reshape(1, LANES),
    }


def kernel(x_prompt, x_sample, cache_k, cache_v, state_lru, c, c_ctx, w_mod, b_mod, w_in, lam, subln_g, sgu_ln_g, sgu_ln_b, w_spatial, b_spatial, conv_w, conv_b, w_rg, b_rg, lru_log, w_out, ln_g, ln_b, w_router, b_router, w_gu, b_gu, w_down, b_down):
    p = dict(sgu_ln_g=sgu_ln_g, sgu_ln_b=sgu_ln_b, w_spatial=w_spatial, b_spatial=b_spatial, conv_w=conv_w,
             conv_b=conv_b, w_rg=w_rg, b_rg=b_rg, lru_log=lru_log, w_out=w_out, ln_g=ln_g, ln_b=ln_b,
             w_router=w_router, b_router=b_router)
    cvec8 = jnp.concatenate([c_ctx[None, :], c, jnp.zeros((SUBLANES - 1 - DEC_BATCH, D_MODEL), F32)], axis=0)
    mod = _modulation(cvec8, w_mod, b_mod)
    x_pair = (x_prompt.reshape(N_CTX, D_MODEL), x_sample.reshape(N_DEC, D_MODEL))
    tables = _rope_tables()
    zero_state = jnp.zeros((BATCH, 1, 2, LRU_WIDTH), F32)
    prev_kv = None
    new_s = []
    for li in range(DEPTH):
        lp = _layer_params(p, li)
        lam_init = 0.8 - 0.6 * math.exp(-0.3 * li)
        proj = _input_projection(*x_pair, mod, w_in, li)
        att_ctx, kc, vc = _context_attention(proj, lam[li], subln_g[li], lam_init, prev_kv)
        prev_kv = (kc, vc)
        att_dec = _denoise_attention(proj, cache_k, cache_v, lam[li], subln_g[li], li, lam_init, tables)
        sgu_ctx, lru_ctx, h_ctx = _mixers(proj, zero_state, 0, lp, SEQ, BATCH, 0)
        sgu_dec, lru_dec, _ = _mixers(proj, state_lru, li, lp, DEC_SEQ, DEC_BATCH, N_CTX // DEC_SEQ)
        x1, h2, route_i, route_f, tile_cnt = _output_projection(
            x_pair, (att_ctx, att_dec), (sgu_ctx, sgu_dec), (lru_ctx, lru_dec), mod, lp, li)
        rt = _routing_tables(tile_cnt)
        xs = _dispatch(rt, h2, route_i)
        y = _moe_blocks(rt, xs, w_gu, b_gu, w_down, b_down, li)
        x_pair = _combine(rt, y, x1, route_i, route_f, mod, lp["ln_g2"], lp["ln_b2"], li)
        new_s.append(h_ctx)
    y_prompt = x_pair[0].reshape(BATCH, SEQ, D_MODEL)
    y_sample = x_pair[1].reshape(DEC_BATCH, DEC_SEQ, D_MODEL)
    return (y_prompt, y_sample, prev_kv[0], prev_kv[1], jnp.stack(new_s, axis=1))
```

```python
import functools
import math

import numpy as np
import jax
import jax.numpy as jnp
from jax import lax
from jax.experimental import pallas as pl
from jax.experimental.pallas import tpu as pltpu

F32 = jnp.float32
BF16 = jnp.bfloat16
I32 = jnp.int32

D_MODEL = 1024
BATCH = 32
SEQ = 256
DEPTH = 2
DEC_BATCH = 2
DEC_SEQ = 1024
PAST_LEN = 256
GRID_W = 64
HEAD_DIM = 64
ATT_WIDTH = D_MODEL // 2
SGU_WIDTH = D_MODEL // 4
LRU_WIDTH = D_MODEL // 4
N_ATT_HEADS = ATT_WIDTH // HEAD_DIM
ATT_HALF = HEAD_DIM // 2
ROPE_FREQS = ATT_HALF // 4
ROPE_THETA = 10000.0
CHUNK = 128
N_SGU_GROUPS = 4
SGU_GROUP = SGU_WIDTH // N_SGU_GROUPS
N_LRU_BLOCKS = 4
LRU_BLOCK = LRU_WIDTH // N_LRU_BLOCKS
CONV_W = 4
LRU_C = 8.0
IN_COLS = 3 * ATT_WIDTH + 2 * SGU_WIDTH + 2 * LRU_WIDTH
N_EXPERTS = 32
TOP_K = 4
D_EXPERT = D_MODEL
SWIGLU_LIMIT = 7.0
SWIGLU_ALPHA = 1.702
DN_ALPHA = (2 * DEPTH) ** 0.25
EPS = 1e-5

N_CTX = BATCH * SEQ
N_DEC = DEC_BATCH * DEC_SEQ
N_TOK = N_CTX + N_DEC

LANES = 128
SUBLANES = 8
ROW_SLABS = D_MODEL // LANES

TM = 512
CTX_TILES = N_CTX // TM
TILES_PER_DEC = DEC_SEQ // TM
N_TILES = N_TOK // TM
MOD_TN = 1024
RT = 256
N_RT = N_TOK // RT
RT_PER_TM = TM // RT
RT_ROWS = RT * TOP_K
RT_PER_STEP = 4
FETCH_AHEAD = 2
BM = 256
NB = N_TOK * TOP_K // BM + N_EXPERTS
WEIGHT_PIECES = 8
X_BUFFERS = 3
N_SLOTS = NB * BM
RUN_BITS = tuple(1 << b for b in range(RT.bit_length() - 1, -1, -1))
PAD_BITS = tuple(1 << b for b in range(BM.bit_length() - 2, -1, -1))
VMEM_LIMIT = 56 * 1024 * 1024


def _cparams(n_axes):
    return pltpu.CompilerParams(
        dimension_semantics=("arbitrary",) * n_axes,
        vmem_limit_bytes=VMEM_LIMIT)


def _mod_row(i):
    return jnp.where(i < CTX_TILES, 0, 1 + (i - CTX_TILES) // TILES_PER_DEC)


def _layer_norm_rows(z, g, b):
    mu = jnp.mean(z, axis=-1, keepdims=True)
    zc = z - mu
    var = jnp.mean(zc * zc, axis=-1, keepdims=True)
    return zc * lax.rsqrt(var + EPS) * g + b


def _pair_specs(tile, width, n_ctx_tiles):
    ctx = pl.BlockSpec((tile, width), lambda i, *_: (jnp.minimum(i, n_ctx_tiles - 1), 0))
    dec = pl.BlockSpec((tile, width), lambda i, *_: (jnp.maximum(i - n_ctx_tiles, 0), 0))
    return [ctx, dec]


def _pair_read(i, n_ctx_tiles, ctx_ref, dec_ref):
    return jnp.where(i < n_ctx_tiles, ctx_ref[...], dec_ref[...])


def _mod_kernel(cvec_ref, w_ref, b_ref, o_ref):
    cv = cvec_ref[...]
    s = cv * jax.nn.sigmoid(cv)
    s_t = s.T
    w = w_ref[0]
    rows = [jnp.sum(s_t[:, r:r + 1] * w, axis=0, keepdims=True) for r in range(1 + DEC_BATCH)]
    rows.append(jnp.zeros((SUBLANES - 1 - DEC_BATCH, MOD_TN), F32))
    o_ref[0] = jnp.concatenate(rows, axis=0) + b_ref[0]


def _modulation(cvec8, w_mod, b_mod):
    n_out = w_mod.shape[-1]
    return pl.pallas_call(
        _mod_kernel,
        grid=(DEPTH, n_out // MOD_TN),
        in_specs=[
            pl.BlockSpec((SUBLANES, D_MODEL), lambda l, j: (0, 0)),
            pl.BlockSpec((1, D_MODEL, MOD_TN), lambda l, j: (l, 0, j)),
            pl.BlockSpec((1, 1, MOD_TN), lambda l, j: (l, 0, j)),
        ],
        out_specs=pl.BlockSpec((1, SUBLANES, MOD_TN), lambda l, j: (l, 0, j)),
        out_shape=jax.ShapeDtypeStruct((DEPTH, SUBLANES, n_out), F32),
        compiler_params=_cparams(2),
        name="modulation",
    )(cvec8, w_mod, b_mod.reshape(DEPTH, 1, n_out))


def _inproj_kernel(xc_ref, xd_ref, mod_ref, w_ref, o_ref, wbf_ref):
    i = pl.program_id(0)

    @pl.when(i == 0)
    def _():
        wbf_ref[...] = w_ref[...].astype(BF16)

    m = mod_ref[pl.ds(_mod_row(i), 1), :]
    sh1 = m[:, 0:D_MODEL]
    sc1 = m[:, D_MODEL:2 * D_MODEL]
    x = _pair_read(i, CTX_TILES, xc_ref, xd_ref)
    h = (x * (1.0 + sc1) + sh1).astype(BF16)
    o_ref[...] = jnp.dot(h, wbf_ref[...], preferred_element_type=F32)


def _input_projection(x_ctx, x_dec, mod, w_in, li):
    return pl.pallas_call(
        _inproj_kernel,
        grid=(N_TILES,),
        in_specs=_pair_specs(TM, D_MODEL, CTX_TILES) + [
            pl.BlockSpec((None, SUBLANES, 6 * D_MODEL), lambda i: (li, 0, 0)),
            pl.BlockSpec((None, D_MODEL, IN_COLS), lambda i: (li, 0, 0), pipeline_mode=pl.Buffered(1)),
        ],
        out_specs=pl.BlockSpec((TM, IN_COLS), lambda i: (i, 0)),
        out_shape=jax.ShapeDtypeStruct((N_TOK, IN_COLS), F32),
        scratch_shapes=[pltpu.VMEM((D_MODEL, IN_COLS), BF16)],
        compiler_params=_cparams(1),
        name="input_projection",
    )(x_ctx, x_dec, mod, w_in)


def _lambda(lam_ref, lam_init):
    lm = lam_ref[...]
    a = jnp.sum(lm[0:1] * lm[1:2], axis=-1, keepdims=True)
    b = jnp.sum(lm[2:3] * lm[3:4], axis=-1, keepdims=True)
    return jnp.exp(a) - jnp.exp(b) + lam_init


_NT = (((1,), (1,)), ((), ()))
_TN = (((0,), (0,)), ((), ()))
_ATT_SCALE = ATT_HALF ** -0.5
_Q_SCALE = _ATT_SCALE * math.log2(math.e)
KEY_CHUNK = 128


def _attention_heads(q_parts, k_segs, values, s_scr, e_scr, lam, g_col, lam_init):
    n_heads = len(q_parts)
    for h in range(n_heads):
        for c in range(2):
            row = 0
            for k in k_segs[h][c]:
                s_scr[2 * h + c, row:row + k.shape[0], :] = lax.dot_general(
                    k, q_parts[h][c], _NT, preferred_element_type=F32)
                row += k.shape[0]
    n_keys, n_q = s_scr.shape[1:]
    chunks = [slice(r, r + KEY_CHUNK) for r in range(0, n_keys, KEY_CHUNK)]
    for n in range(2 * n_heads):
        tops = [jnp.max(s_scr[n, rows, :].reshape(KEY_CHUNK // SUBLANES, SUBLANES, n_q), axis=0)
                for rows in chunks]
        top = jnp.max(functools.reduce(jnp.maximum, tops), axis=0, keepdims=True)
        for rows in chunks:
            e_scr[n, rows, :] = jnp.exp2(s_scr[n, rows, :] - top).astype(BF16)
    outs = []
    for h in range(n_heads):
        v_aug = jnp.concatenate([values[h], jnp.ones_like(values[h])], axis=-1)
        normed = []
        for c in range(2):
            acc = lax.dot_general(v_aug, e_scr[2 * h + c], _TN, preferred_element_type=F32)
            normed.append(acc[:HEAD_DIM] * (1.0 / acc[HEAD_DIM:HEAD_DIM + 1]))
        o_t = normed[0] - lam * normed[1]
        ms = jnp.mean(o_t * o_t, axis=0, keepdims=True)
        outs.append(o_t * lax.rsqrt(ms + EPS) * g_col * (1.0 - lam_init))
    return outs


def _head_cols(h):
    lo = h * HEAD_DIM
    return [slice(lo + c * ATT_HALF, lo + (c + 1) * ATT_HALF) for c in range(2)]


def _ctx_attn_kernel(lam_ref, g_ref, q_ref, k_ref, v_ref, *rest, lam_init, stacked):
    *rest, s_scr, e_scr = rest
    if stacked:
        pk_ref, pv_ref, att_ref, ck_ref, cv_ref = rest
        ck_ref[0, 0] = pk_ref[0]
        cv_ref[0, 0] = pv_ref[0]
        put_k = lambda h, val: ck_ref.__setitem__((0, 1, h), val)
        put_v = lambda h, val: cv_ref.__setitem__((0, 1, h), val)
    else:
        att_ref, ck_ref, cv_ref = rest
        put_k = lambda h, val: ck_ref.__setitem__((0, h), val)
        put_v = lambda h, val: cv_ref.__setitem__((0, h), val)
    lam = _lambda(lam_ref, lam_init)
    k = k_ref[...]
    v = v_ref[...]
    qb = (q_ref[...] * _Q_SCALE).astype(BF16)
    kb = k.astype(BF16)
    vb = v.astype(BF16)
    heads = range(N_ATT_HEADS)
    outs = _attention_heads([[qb[:, cs] for cs in _head_cols(h)] for h in heads],
                            [[[kb[:, cs]] for cs in _head_cols(h)] for h in heads],
                            [vb[:, h * HEAD_DIM:(h + 1) * HEAD_DIM] for h in heads],
                            s_scr, e_scr, lam, g_ref[...], lam_init)
    for h in heads:
        put_k(h, k[:, h * HEAD_DIM:(h + 1) * HEAD_DIM])
        put_v(h, v[:, h * HEAD_DIM:(h + 1) * HEAD_DIM])
    att_ref[...] = jnp.concatenate(outs, axis=0).T


def _context_attention(proj, lam, subln_g, lam_init, prev_kv=None):
    stacked = prev_kv is not None
    per_layer = pl.BlockSpec((1, N_ATT_HEADS, SEQ, HEAD_DIM), lambda b: (b, 0, 0, 0))
    if stacked:
        kv_shape = jax.ShapeDtypeStruct((BATCH, DEPTH, N_ATT_HEADS, SEQ, HEAD_DIM), F32)
        kv_spec = pl.BlockSpec((1, DEPTH, N_ATT_HEADS, SEQ, HEAD_DIM), lambda b: (b, 0, 0, 0, 0))
    else:
        kv_shape = jax.ShapeDtypeStruct((BATCH, N_ATT_HEADS, SEQ, HEAD_DIM), F32)
        kv_spec = per_layer
    return pl.pallas_call(
        functools.partial(_ctx_attn_kernel, lam_init=lam_init, stacked=stacked),
        grid=(BATCH,),
        in_specs=[
            pl.BlockSpec((4, ATT_HALF), lambda b: (0, 0)),
            pl.BlockSpec((HEAD_DIM, 1), lambda b: (0, 0)),
            pl.BlockSpec((SEQ, ATT_WIDTH), lambda b: (b, 0)),
            pl.BlockSpec((SEQ, ATT_WIDTH), lambda b: (b, 1)),
            pl.BlockSpec((SEQ, ATT_WIDTH), lambda b: (b, 2)),
        ] + ([per_layer, per_layer] if stacked else []),
        out_specs=[pl.BlockSpec((SEQ, ATT_WIDTH), lambda b: (b, 0)), kv_spec, kv_spec],
        out_shape=[jax.ShapeDtypeStruct((N_CTX, ATT_WIDTH), F32), kv_shape, kv_shape],
        scratch_shapes=[pltpu.VMEM((2 * N_ATT_HEADS, SEQ, SEQ), F32),
                        pltpu.VMEM((2 * N_ATT_HEADS, SEQ, SEQ), BF16)],
        compiler_params=_cparams(1),
        name="context_attention",
    )(lam, subln_g.reshape(HEAD_DIM, 1), proj, proj, proj, *(prev_kv if stacked else ()))


def _rope_tables():
    t = np.arange(DEC_SEQ)
    pos = np.stack([t // GRID_W, t % GRID_W], axis=1).astype(np.float32)
    inv = (np.float32(ROPE_THETA) ** (-np.arange(ROPE_FREQS, dtype=np.float32) / np.float32(ROPE_FREQS)))
    j = np.arange(HEAD_DIM)
    d = j % ATT_HALF
    axis = d // (2 * ROPE_FREQS)
    u = d % (2 * ROPE_FREQS)
    ang = pos[:, axis] * inv[u % ROPE_FREQS][None, :].astype(np.float32)
    cos = np.cos(ang).astype(np.float32)
    sin = np.sin(ang).astype(np.float32)
    first = (u < ROPE_FREQS)[None, :]
    s_next = np.where(first, -sin, 0.0).astype(np.float32)
    s_prev = np.where(first, 0.0, sin).astype(np.float32)
    tile = lambda a: jnp.asarray(np.tile(a, (1, N_ATT_HEADS)))
    return tile(cos), tile(s_next), tile(s_prev)


def _rotate(x, cos, s_next, s_prev):
    width = x.shape[-1]
    return (x * cos + pltpu.roll(x, width - ROPE_FREQS, axis=1) * s_next
            + pltpu.roll(x, ROPE_FREQS, axis=1) * s_prev)


QB = 256
Q_STEPS = DEC_SEQ // QB
DEC_HEAD_GROUP = 2


def _dec_attn_kernel(lam_ref, g_ref, q_ref, k_ref, v_ref, ck_ref, cv_ref,
                     cq_ref, snq_ref, spq_ref, ck_tab, snk_tab, spk_tab,
                     att_ref, krot_ref, s_scr, e_scr, *, lam_init):
    j = pl.program_id(1)

    @pl.when(j == 0)
    def _():
        krot_ref[...] = _rotate(k_ref[...], ck_tab[...], snk_tab[...], spk_tab[...]).astype(BF16)

    lam = _lambda(lam_ref, lam_init)
    qb = (_rotate(q_ref[...], cq_ref[...], snq_ref[...], spq_ref[...]) * _Q_SCALE).astype(BF16)
    kb = krot_ref[...]
    vb = v_ref[...].astype(BF16)
    outs = []
    for h0 in range(0, N_ATT_HEADS, DEC_HEAD_GROUP):
        heads = range(h0, h0 + DEC_HEAD_GROUP)
        past_k = [ck_ref[0, 0, h].astype(BF16) for h in heads]
        k_segs = [[[pk[:, c * ATT_HALF:(c + 1) * ATT_HALF], kb[:, _head_cols(h)[c]]] for c in range(2)]
                  for h, pk in zip(heads, past_k)]
        values = [jnp.concatenate([cv_ref[0, 0, h].astype(BF16), vb[:, h * HEAD_DIM:(h + 1) * HEAD_DIM]], axis=0)
                  for h in heads]
        outs += _attention_heads([[qb[:, cs] for cs in _head_cols(h)] for h in heads], k_segs, values,
                                 s_scr, e_scr, lam, g_ref[...], lam_init)
    att_ref[...] = jnp.concatenate(outs, axis=0).T


def _denoise_attention(proj, cache_k, cache_v, lam, subln_g, li, lam_init, tables):
    cos, s_next, s_prev = tables
    row0 = N_CTX // QB
    seq0 = N_CTX // DEC_SEQ
    q_tab = pl.BlockSpec((QB, ATT_WIDTH), lambda b, j: (j, 0))
    k_tab = pl.BlockSpec((DEC_SEQ, ATT_WIDTH), lambda b, j: (0, 0))
    cache_spec = pl.BlockSpec((1, 1, N_ATT_HEADS, PAST_LEN, HEAD_DIM), lambda b, j: (b, li, 0, 0, 0))
    return pl.pallas_call(
        functools.partial(_dec_attn_kernel, lam_init=lam_init),
        grid=(DEC_BATCH, Q_STEPS),
        in_specs=[
            pl.BlockSpec((4, ATT_HALF), lambda b, j: (0, 0)),
            pl.BlockSpec((HEAD_DIM, 1), lambda b, j: (0, 0)),
            pl.BlockSpec((QB, ATT_WIDTH), lambda b, j: (row0 + b * Q_STEPS + j, 0)),
            pl.BlockSpec((DEC_SEQ, ATT_WIDTH), lambda b, j: (seq0 + b, 1)),
            pl.BlockSpec((DEC_SEQ, ATT_WIDTH), lambda b, j: (seq0 + b, 2)),
            cache_spec, cache_spec,
            q_tab, q_tab, q_tab, k_tab, k_tab, k_tab,
        ],
        out_specs=pl.BlockSpec((QB, ATT_WIDTH), lambda b, j: (b * Q_STEPS + j, 0)),
        out_shape=jax.ShapeDtypeStruct((N_DEC, ATT_WIDTH), F32),
        scratch_shapes=[pltpu.VMEM((DEC_SEQ, ATT_WIDTH), BF16),
                        pltpu.VMEM((2 * DEC_HEAD_GROUP, PAST_LEN + DEC_SEQ, QB), F32),
                        pltpu.VMEM((2 * DEC_HEAD_GROUP, PAST_LEN + DEC_SEQ, QB), BF16)],
        compiler_params=_cparams(2),
        name="denoise_attention",
    )(lam, subln_g.reshape(HEAD_DIM, 1), proj, proj, proj, cache_k, cache_v,
      cos, s_next, s_prev, cos, s_next, s_prev)


def _softplus(z):
    return jnp.maximum(z, 0.0) + jnp.log1p(jnp.exp(-jnp.abs(z)))


def _mixer_kernel(su_ref, sv_ref, rx_ref, rg_ref, h0_ref, lng_ref, lnb_ref, ws_ref, bs_ref,
                  cw_ref, cb_ref, wrg_ref, brg_ref, lrulog_ref,
                  sgu_ref, lru_ref, hlast_ref, a_scr, b_scr, h_scr, *, seq_len):
    vn = _layer_norm_rows(sv_ref[...], lng_ref[...], lnb_ref[...])
    lane_group = lax.broadcasted_iota(I32, (CHUNK, SGU_WIDTH), 1) // SGU_GROUP
    for n in range(seq_len // CHUNK):
        rows = slice(n * CHUNK, (n + 1) * CHUNK)
        vc = vn[rows].astype(BF16)
        s = jnp.zeros((CHUNK, SGU_WIDTH), F32)
        for g in range(N_SGU_GROUPS):
            sg = jnp.dot(ws_ref[g].astype(BF16), vc, preferred_element_type=F32)
            s = jnp.where(lane_group == g, sg, s)
        sgu_ref[rows, :] = su_ref[rows, :] * (s + bs_ref[...])

    x = rx_ref[...]
    row = lax.broadcasted_iota(I32, (seq_len, LRU_WIDTH), 0)

    def shifted(val, d, fill):
        rolled = pltpu.roll(val, d % seq_len, axis=0)
        inside = (row >= d) if d > 0 else (row < seq_len + d)
        return jnp.where(inside, rolled, fill)

    left = CONV_W // 2
    xc = cb_ref[...] + x * cw_ref[left:left + 1, :]
    for tap in range(CONV_W):
        if tap != left:
            xc = xc + shifted(x, left - tap, 0.0) * cw_ref[tap:tap + 1, :]
    pre = jnp.dot(xc.astype(BF16), wrg_ref[...].astype(BF16), preferred_element_type=F32) + brg_ref[...]
    gates = 0.5 + 0.5 * jnp.tanh(0.5 * pre)
    in_chunk = row % SUBLANES
    n_chunks = seq_len // SUBLANES
    h0 = h0_ref[0, 0]
    lasts = []
    for direction in range(2):
        reverse = direction == 1
        base = direction * 2 * LRU_WIDTH
        r = gates[:, base:base + LRU_WIDTH]
        gi = gates[:, base + LRU_WIDTH:base + 2 * LRU_WIDTH]
        log_a = -LRU_C * r * _softplus(-lrulog_ref[direction:direction + 1, :])
        a = jnp.exp(log_a)
        b = jnp.sqrt(-jnp.tanh(log_a) * (a * a + 1.0)) * gi * xc
        for d in (1, 2, 4):
            if reverse:
                inside = in_chunk < SUBLANES - d
                a_n = jnp.where(inside, pltpu.roll(a, seq_len - d, axis=0), 1.0)
                b_n = jnp.where(inside, pltpu.roll(b, seq_len - d, axis=0), 0.0)
            else:
                inside = in_chunk >= d
                a_n = jnp.where(inside, pltpu.roll(a, d, axis=0), 1.0)
                b_n = jnp.where(inside, pltpu.roll(b, d, axis=0), 0.0)
            b = a * b_n + b
            a = a * a_n
        a_scr[...] = a
        b_scr[...] = b

        def chunk_step(c, carry, reverse=reverse):
            cc = n_chunks - 1 - c if reverse else c
            off = pl.multiple_of(cc * SUBLANES, SUBLANES)
            hc = a_scr[pl.ds(off, SUBLANES), :] * carry + b_scr[pl.ds(off, SUBLANES), :]
            if reverse:
                h_scr[pl.ds(off, SUBLANES), :] = h_scr[pl.ds(off, SUBLANES), :] + hc
                return hc[0:1, :]
            h_scr[pl.ds(off, SUBLANES), :] = hc
            return hc[SUBLANES - 1:SUBLANES, :]

        lasts.append(lax.fori_loop(0, n_chunks, chunk_step, h0[direction:direction + 1, :]))
    lru_ref[...] = h_scr[...] * jax.nn.gelu(rg_ref[...])
    hlast_ref[0] = jnp.concatenate(lasts, axis=0)


def _mixers(proj, h0, h0_layer, lp, seq_len, n_seq, row_block0):
    col0 = 3 * ATT_WIDTH // SGU_WIDTH
    col = lambda c: pl.BlockSpec((seq_len, SGU_WIDTH), lambda b: (row_block0 + b, col0 + c))
    full = lambda shape: pl.BlockSpec(shape, lambda b: (0,) * len(shape))
    out_rows = pl.BlockSpec((seq_len, SGU_WIDTH), lambda b: (b, 0))
    return pl.pallas_call(
        functools.partial(_mixer_kernel, seq_len=seq_len),
        grid=(n_seq,),
        in_specs=[
            col(0), col(1), col(2), col(3),
            pl.BlockSpec((1, 1, 2, LRU_WIDTH), lambda b: (b, h0_layer, 0, 0)),
            full((1, SGU_WIDTH)), full((1, SGU_WIDTH)),
            full((N_SGU_GROUPS, CHUNK, CHUNK)), full((CHUNK, SGU_WIDTH)),
            full((CONV_W, LRU_WIDTH)), full((1, LRU_WIDTH)),
            full((LRU_WIDTH, 4 * LRU_WIDTH)), full((1, 4 * LRU_WIDTH)),
            full((2, LRU_WIDTH)),
        ],
        out_specs=[out_rows, out_rows, pl.BlockSpec((1, 2, LRU_WIDTH), lambda b: (b, 0, 0))],
        out_shape=[
            jax.ShapeDtypeStruct((n_seq * seq_len, SGU_WIDTH), F32),
            jax.ShapeDtypeStruct((n_seq * seq_len, LRU_WIDTH), F32),
            jax.ShapeDtypeStruct((n_seq, 2, LRU_WIDTH), F32),
        ],
        scratch_shapes=[pltpu.VMEM((seq_len, LRU_WIDTH), F32)] * 3,
        compiler_params=_cparams(1),
        name="mixers_%d" % seq_len,
    )(proj, proj, proj, proj, h0, lp["sgu_ln_g"], lp["sgu_ln_b"], lp["w_spatial"], lp["b_spatial_full"],
      lp["conv_w"], lp["conv_b"], lp["w_rg_full"], lp["b_rg_full"], lp["lru_log"])


def _outproj_kernel(xc_ref, xd_ref, ac_ref, ad_ref, sc_ref, sd_ref, lc_ref, ld_ref,
                    mod_ref, wout_ref, lng_ref, lnb_ref, wr_ref, br_ref,
                    x1_ref, h2_ref, ri_ref, rf_ref, cnt_ref, wbf_ref):
    i = pl.program_id(0)

    @pl.when(i == 0)
    def _():
        wbf_ref[...] = wout_ref[...].astype(BF16)

    m = mod_ref[pl.ds(_mod_row(i), 1), :]
    g1 = m[:, 2 * D_MODEL:3 * D_MODEL]
    sh2 = m[:, 3 * D_MODEL:4 * D_MODEL]
    sc2 = m[:, 4 * D_MODEL:5 * D_MODEL]
    a0, a1 = ATT_WIDTH, ATT_WIDTH + SGU_WIDTH
    x = _pair_read(i, CTX_TILES, xc_ref, xd_ref)
    att = _pair_read(i, CTX_TILES, ac_ref, ad_ref).astype(BF16)
    sgu = _pair_read(i, CTX_TILES, sc_ref, sd_ref).astype(BF16)
    lru = _pair_read(i, CTX_TILES, lc_ref, ld_ref).astype(BF16)
    mix = (jnp.dot(att, wbf_ref[0:a0, :], preferred_element_type=F32)
           + jnp.dot(sgu, wbf_ref[a0:a1, :], preferred_element_type=F32)
           + jnp.dot(lru, wbf_ref[a1:, :], preferred_element_type=F32))
    x1 = _layer_norm_rows(DN_ALPHA * x + g1 * mix, lng_ref[...], lnb_ref[...])
    x1_ref[...] = x1
    h2 = x1 * (1.0 + sc2) + sh2
    h2_ref[...] = h2.astype(BF16)

    logits = jnp.dot(h2.astype(BF16), wr_ref[...].astype(BF16), preferred_element_type=F32) + br_ref[...]
    lane = lax.broadcasted_iota(I32, (TM, LANES), 1)
    lane_f = lane.astype(F32)
    neg_inf = jnp.float32(-jnp.inf)
    work = jnp.where(lane < N_EXPERTS, logits, neg_inf)
    vals, idxs = [], []
    for _ in range(TOP_K):
        top = jnp.max(work, axis=-1, keepdims=True)
        idx = jnp.min(jnp.where(work == top, lane_f, float(LANES)), axis=-1, keepdims=True)
        vals.append(top)
        idxs.append(idx)
        work = jnp.where(lane_f == idx, neg_inf, work)
    exps = [jnp.exp(v - vals[0]) for v in vals]
    denom = exps[0] + exps[1] + exps[2] + exps[3]
    onehot = jnp.zeros((TM, LANES), F32)
    for idx in idxs:
        onehot = onehot + (lane_f == idx).astype(F32)
    r_i = lax.broadcasted_iota(I32, (RT, RT), 0)
    c_i = lax.broadcasted_iota(I32, (RT, RT), 1)
    tri = (r_i > c_i).astype(F32).astype(BF16)
    upper = (lax.broadcasted_iota(I32, (LANES, LANES), 0)
             < lax.broadcasted_iota(I32, (LANES, LANES), 1)).astype(F32).astype(BF16)
    packed_pos = []
    for t in range(RT_PER_TM):
        hot = onehot[t * RT:(t + 1) * RT]
        total = jnp.broadcast_to(jnp.sum(hot, axis=0, keepdims=True), (SUBLANES, LANES))
        cnt_ref[t] = total.astype(I32)
        run_start = jnp.dot(total.astype(BF16), upper, preferred_element_type=F32)[0:1, :]
        packed_pos.append(jnp.dot(tri, hot.astype(BF16), preferred_element_type=F32) + run_start)
    packed_pos = jnp.concatenate(packed_pos, axis=0)
    ri = jnp.zeros((TM, LANES), F32)
    rf = jnp.zeros((TM, LANES), F32)
    for k in range(TOP_K):
        pos = jnp.sum(jnp.where(lane_f == idxs[k], packed_pos, 0.0), axis=-1, keepdims=True)
        ri = jnp.where(lane == k, idxs[k], ri)
        ri = jnp.where(lane == TOP_K + k, pos, ri)
        rf = jnp.where(lane == k, exps[k] / denom, rf)
    ri_ref[...] = ri.astype(I32)
    rf_ref[...] = rf


def _output_projection(x_pair, att_pair, sgu_pair, lru_pair, mod, lp, li):
    rows = lambda w: pl.BlockSpec((TM, w), lambda i: (i, 0))
    full = lambda shape: pl.BlockSpec(shape, lambda i: (0,) * len(shape))
    return pl.pallas_call(
        _outproj_kernel,
        grid=(N_TILES,),
        in_specs=(_pair_specs(TM, D_MODEL, CTX_TILES) + _pair_specs(TM, ATT_WIDTH, CTX_TILES)
                  + _pair_specs(TM, SGU_WIDTH, CTX_TILES) + _pair_specs(TM, LRU_WIDTH, CTX_TILES) + [
            pl.BlockSpec((None, SUBLANES, 6 * D_MODEL), lambda i: (li, 0, 0)),
            pl.BlockSpec((None, D_MODEL, D_MODEL), lambda i: (li, 0, 0), pipeline_mode=pl.Buffered(1)),
            full((1, D_MODEL)), full((1, D_MODEL)),
            full((D_MODEL, LANES)), full((1, LANES)),
        ]),
        out_specs=[
            rows(D_MODEL), rows(D_MODEL), rows(LANES), rows(LANES),
            pl.BlockSpec((RT_PER_TM, SUBLANES, LANES), lambda i: (i, 0, 0)),
        ],
        out_shape=[
            jax.ShapeDtypeStruct((N_TOK, D_MODEL), F32),
            jax.ShapeDtypeStruct((N_TOK, D_MODEL), BF16),
            jax.ShapeDtypeStruct((N_TOK, LANES), I32),
            jax.ShapeDtypeStruct((N_TOK, LANES), F32),
            jax.ShapeDtypeStruct((N_RT, SUBLANES, LANES), I32),
        ],
        scratch_shapes=[pltpu.VMEM((D_MODEL, D_MODEL), BF16)],
        compiler_params=_cparams(1),
        name="output_projection",
    )(*x_pair, *att_pair, *sgu_pair, *lru_pair, mod, lp["w_out_all"], lp["ln_g1"], lp["ln_b1"],
      lp["w_router_pad"], lp["b_router_pad"])


def _row_ds(row, n_rows):
    return pl.ds(pl.multiple_of(row * ROW_SLABS, ROW_SLABS), n_rows * ROW_SLABS)


def _for_each_run_piece(length, bits, fn):
    for bit in bits:
        done = length & (-2 * bit)
        @pl.when((length & bit) != 0)
        def _(done=done, bit=bit):
            fn(done, bit)


def _tile_run_copies(tile, start_ref, len_ref, off_ref, sorted_hbm, buf, sem, to_sorted, live=None):
    for e in range(N_EXPERTS):
        t = tile * N_EXPERTS + e
        start, length, off = start_ref[t], len_ref[t], off_ref[t]
        if live is not None:
            length = jnp.where(live, length, 0)

        def piece(done, bit, start=start, off=off):
            packed = buf.at[_row_ds(off + done, bit), :]
            srt = sorted_hbm.at[_row_ds(start + done, bit), :]
            if to_sorted:
                pltpu.make_async_copy(packed, srt, sem).start()
            else:
                pltpu.make_async_copy(srt, packed, sem).start()

        _for_each_run_piece(length, RUN_BITS, piece)


def _tile_runs_wait(buf, sem):
    pltpu.make_async_copy(buf, buf, sem).wait()


def _packed_positions(ri):
    return [ri[:, TOP_K + k:TOP_K + k + 1].astype(F32) for k in range(TOP_K)]


def _slab_columns(buf, row0, n_rows):
    return jnp.concatenate(
        [buf[pl.ds(row0 * ROW_SLABS + s, n_rows, stride=ROW_SLABS), :] for s in range(ROW_SLABS)], axis=-1)


def _dispatch_kernel(start_ref, len_ref, off_ref, pstart_ref, plen_ref, nv_ref,
                     h2_ref, ri_ref, xs_hbm, *scratch):
    bufs, (zbuf, sem, zsem) = scratch[:RT_PER_STEP], scratch[RT_PER_STEP:]
    i = pl.program_id(0)
    n_steps = pl.num_programs(0)

    @pl.when(i == 0)
    def _():
        zbuf[...] = jnp.zeros_like(zbuf)

        def zero_fill(wait):
            def go(cp):
                cp.wait() if wait else cp.start()

            def per_expert(e, carry):
                def piece(done, bit):
                    go(pltpu.make_async_copy(zbuf.at[pl.ds(0, bit * ROW_SLABS), :],
                                             xs_hbm.at[_row_ds(pstart_ref[e] + done, bit), :], zsem))
                _for_each_run_piece(plen_ref[e], PAD_BITS, piece)
                return carry
            lax.fori_loop(0, N_EXPERTS, per_expert, 0)

            def per_block(b, carry):
                go(pltpu.make_async_copy(zbuf, xs_hbm.at[_row_ds(b * BM, BM), :], zsem))
                return carry
            lax.fori_loop(nv_ref[0], NB, per_block, 0)

        zero_fill(False)
        zero_fill(True)

    def send(tile, t, live):
        _tile_run_copies(tile, start_ref, len_ref, off_ref, xs_hbm, bufs[t], sem.at[t], to_sorted=True, live=live)

    col = lax.broadcasted_iota(I32, (RT, RT_ROWS), 1).astype(F32)
    for t in range(RT_PER_STEP):
        buf = bufs[t]
        rows = slice(t * RT, (t + 1) * RT)
        tile = i * RT_PER_STEP + t

        @pl.when(i >= 1)
        def _(buf=buf, t=t):
            _tile_runs_wait(buf, sem.at[t])

        send(jnp.maximum(tile - 1, 0), (t - 1) % RT_PER_STEP, tile >= 1)
        pos = _packed_positions(ri_ref[rows, :])
        sel = jnp.zeros((RT, RT_ROWS), F32)
        for p in pos:
            sel = sel + (col == p).astype(F32)
        packed = lax.dot_general(sel.astype(BF16), h2_ref[rows, :], _TN, preferred_element_type=F32)
        for s in range(ROW_SLABS):
            buf[pl.ds(s, RT_ROWS, stride=ROW_SLABS), :] = packed[:, s * LANES:(s + 1) * LANES]

    @pl.when(i == n_steps - 1)
    def _():
        send(N_RT - 1, RT_PER_STEP - 1, True)
        for t in range(RT_PER_STEP):
            _tile_runs_wait(bufs[t], sem.at[t])


def _dispatch(tables, h2, route_i):
    grid_spec = pltpu.PrefetchScalarGridSpec(
        num_scalar_prefetch=6,
        grid=(N_RT // RT_PER_STEP,),
        in_specs=[
            pl.BlockSpec((RT_PER_STEP * RT, D_MODEL), lambda i, *_: (i, 0)),
            pl.BlockSpec((RT_PER_STEP * RT, LANES), lambda i, *_: (i, 0)),
        ],
        out_specs=pl.BlockSpec(memory_space=pl.ANY),
        scratch_shapes=[pltpu.VMEM((RT_ROWS * ROW_SLABS, LANES), F32)] * RT_PER_STEP + [
            pltpu.VMEM((BM * ROW_SLABS, LANES), F32),
            pltpu.SemaphoreType.DMA((RT_PER_STEP,)),
            pltpu.SemaphoreType.DMA(()),
        ],
    )
    return pl.pallas_call(
        _dispatch_kernel,
        grid_spec=grid_spec,
        out_shape=jax.ShapeDtypeStruct((N_SLOTS * ROW_SLABS, LANES), F32),
        compiler_params=_cparams(1),
        name="moe_dispatch",
    )(tables["start"], tables["len"], tables["off"], tables["pad_start"], tables["pad_len"], tables["n_valid"],
      h2, route_i)


def _moe_kernel(be_ref, eo_ref, nv_ref, xs_hbm, wgu_hbm, bgu_ref, wdn_hbm, bdn_ref, y_hbm,
                xbuf, ybuf, gu_stage, dn_stage, wgu_bf, wdn_bf, wsem, xsem, ysem, *, li):
    n_valid = nv_ref[0]
    rows_per_piece = D_MODEL // WEIGHT_PIECES

    def weight_copies(e, stage):
        cps = []
        for p in range(WEIGHT_PIECES):
            band = pl.ds(p * rows_per_piece, rows_per_piece)
            cps.append(pltpu.make_async_copy(wgu_hbm.at[li, e, band, :], gu_stage.at[stage, band, :], wsem.at[stage]))
            cps.append(pltpu.make_async_copy(wdn_hbm.at[li, e, band, :], dn_stage.at[stage, band, :], wsem.at[stage]))
        return cps

    def fetch(k):
        @pl.when(eo_ref[k] >= 0)
        def _():
            for cp in weight_copies(eo_ref[k], k % 2):
                cp.start(priority=1)

    def x_copy(b, slot):
        return pltpu.make_async_copy(xs_hbm.at[_row_ds(b * BM, BM), :], xbuf.at[slot], xsem.at[slot])

    def y_copy(b, slot):
        return pltpu.make_async_copy(ybuf.at[slot], y_hbm.at[_row_ds(b * BM, BM), :], ysem.at[slot])

    fetch(0)
    fetch(1)
    for ahead in range(X_BUFFERS - 1):
        @pl.when(ahead < n_valid)
        def _(ahead=ahead):
            x_copy(ahead, ahead).start()

    def next_slot(s):
        return jnp.where(s == X_BUFFERS - 1, 0, s + 1)

    def block(b, carry):
        k, xslot = carry
        slot = b % 2
        e = be_ref[b]
        new_expert = jnp.logical_or(b == 0, e != be_ref[jnp.maximum(b - 1, 0)])

        @pl.when(b + X_BUFFERS - 1 < n_valid)
        def _():
            ahead_slot = xslot
            for _ in range(X_BUFFERS - 1):
                ahead_slot = next_slot(ahead_slot)
            x_copy(b + X_BUFFERS - 1, ahead_slot).start()

        @pl.when(new_expert)
        def _():
            stage = k % 2
            for cp in weight_copies(e, stage):
                cp.wait()
            for st in range(2):
                @pl.when(stage == st)
                def _(st=st):
                    wgu_bf[...] = gu_stage[st].astype(BF16)
                    wdn_bf[...] = dn_stage[st].astype(BF16)
            fetch(k + 2)

        x_copy(b, xslot).wait()

        @pl.when(b >= 2)
        def _():
            y_copy(b - 2, slot).wait()

        x = _slab_columns(xbuf.at[xslot], 0, BM).astype(BF16)
        bgu = bgu_ref[e]
        g = jnp.dot(x, wgu_bf[:, :D_EXPERT], preferred_element_type=F32) + bgu[:, :D_EXPERT]
        u = jnp.dot(x, wgu_bf[:, D_EXPERT:], preferred_element_type=F32) + bgu[:, D_EXPERT:]
        g = jnp.minimum(g, SWIGLU_LIMIT)
        u = jnp.clip(u, -SWIGLU_LIMIT, SWIGLU_LIMIT)
        act = ((u + 1.0) * (0.5 * g * (1.0 + jnp.tanh((0.5 * SWIGLU_ALPHA) * g)))).astype(BF16)
        y = jnp.dot(act, wdn_bf[...], preferred_element_type=F32) + bdn_ref[e]
        out = ybuf.at[slot]
        for s in range(ROW_SLABS):
            out[pl.ds(s, BM, stride=ROW_SLABS), :] = y[:, s * LANES:(s + 1) * LANES]
        y_copy(b, slot).start(priority=1)
        return k + new_expert.astype(I32), next_slot(xslot)

    lax.fori_loop(0, n_valid, block, (jnp.int32(0), jnp.int32(0)))

    @pl.when(n_valid >= 2)
    def _():
        y_copy(n_valid - 2, n_valid % 2).wait()
    y_copy(n_valid - 1, (n_valid - 1) % 2).wait()

    ybuf[0] = jnp.zeros((BM * ROW_SLABS, LANES), F32)

    def zero_blocks(wait):
        def one(b, carry):
            cp = y_copy(b, 0)
            cp.wait() if wait else cp.start()
            return carry
        lax.fori_loop(n_valid, NB, one, 0)

    zero_blocks(False)
    zero_blocks(True)


def _moe_blocks(tables, xs, w_gu, b_gu, w_down, b_down, li):
    grid_spec = pltpu.PrefetchScalarGridSpec(
        num_scalar_prefetch=3,
        grid=(1,),
        in_specs=[
            pl.BlockSpec(memory_space=pl.ANY),
            pl.BlockSpec(memory_space=pl.ANY),
            pl.BlockSpec((None, N_EXPERTS, 1, 2 * D_EXPERT), lambda i, *_: (li, 0, 0, 0)),
            pl.BlockSpec(memory_space=pl.ANY),
            pl.BlockSpec((None, N_EXPERTS, 1, D_MODEL), lambda i, *_: (li, 0, 0, 0)),
        ],
        out_specs=pl.BlockSpec(memory_space=pl.ANY),
        scratch_shapes=[
            pltpu.VMEM((X_BUFFERS, BM * ROW_SLABS, LANES), F32),
            pltpu.VMEM((2, BM * ROW_SLABS, LANES), F32),
            pltpu.VMEM((2, D_MODEL, 2 * D_EXPERT), F32),
            pltpu.VMEM((2, D_EXPERT, D_MODEL), F32),
            pltpu.VMEM((D_MODEL, 2 * D_EXPERT), BF16),
            pltpu.VMEM((D_EXPERT, D_MODEL), BF16),
            pltpu.SemaphoreType.DMA((2,)),
            pltpu.SemaphoreType.DMA((X_BUFFERS,)),
            pltpu.SemaphoreType.DMA((2,)),
        ],
    )
    return pl.pallas_call(
        functools.partial(_moe_kernel, li=li),
        grid_spec=grid_spec,
        out_shape=jax.ShapeDtypeStruct((N_SLOTS * ROW_SLABS, LANES), F32),
        compiler_params=_cparams(1),
        name="moe_experts",
    )(tables["block_expert"], tables["expert_order"], tables["n_valid"], xs, w_gu,
      b_gu.reshape(DEPTH, N_EXPERTS, 1, 2 * D_EXPERT), w_down, b_down.reshape(DEPTH, N_EXPERTS, 1, D_MODEL))


def _combine_kernel(start_ref, len_ref, off_ref, y_hbm, x1_ref, ri_ref, rf_ref, mod_ref,
                    lng_ref, lnb_ref, oc_ref, od_ref, *scratch):
    bufs, sem = scratch[:RT_PER_STEP], scratch[RT_PER_STEP]
    i = pl.program_id(0)
    n_steps = pl.num_programs(0)

    def fetch(tile, t):
        _tile_run_copies(tile, start_ref, len_ref, off_ref, y_hbm, bufs[t], sem.at[t], to_sorted=False)

    @pl.when(i == 0)
    def _():
        for t in range(FETCH_AHEAD):
            fetch(t, t)

    m = mod_ref[pl.ds(_mod_row(i * RT_PER_STEP // RT_PER_TM), 1), :]
    g2 = m[:, 5 * D_MODEL:6 * D_MODEL]
    col = lax.broadcasted_iota(I32, (RT, RT_ROWS), 1).astype(F32)
    outs = []
    for t in range(RT_PER_STEP):
        rows = slice(t * RT, (t + 1) * RT)
        fetch(jnp.minimum(i * RT_PER_STEP + t + FETCH_AHEAD, N_RT - 1), (t + FETCH_AHEAD) % RT_PER_STEP)
        _tile_runs_wait(bufs[t], sem.at[t])
        pos = _packed_positions(ri_ref[rows, :])
        gates = rf_ref[rows, :]
        mix = jnp.zeros((RT, RT_ROWS), F32)
        for k in range(TOP_K):
            mix = mix + jnp.where(col == pos[k], gates[:, k:k + 1], 0.0)
        ffn = jnp.dot(mix.astype(BF16), _slab_columns(bufs[t], 0, RT_ROWS).astype(BF16),
                      preferred_element_type=F32)
        outs.append(_layer_norm_rows(DN_ALPHA * x1_ref[rows, :] + g2 * ffn, lng_ref[...], lnb_ref[...]))
    out = jnp.concatenate(outs, axis=0)
    n_ctx_steps = N_CTX // (RT * RT_PER_STEP)

    @pl.when(i < n_ctx_steps)
    def _():
        oc_ref[...] = out

    @pl.when(i >= n_ctx_steps)
    def _():
        od_ref[...] = out

    @pl.when(i == n_steps - 1)
    def _():
        for t in range(FETCH_AHEAD):
            _tile_runs_wait(bufs[t], sem.at[t])


def _combine(tables, y, x1, route_i, route_f, mod, ln_g, ln_b, li):
    rows = RT * RT_PER_STEP
    grid_spec = pltpu.PrefetchScalarGridSpec(
        num_scalar_prefetch=3,
        grid=(N_RT // RT_PER_STEP,),
        in_specs=[
            pl.BlockSpec(memory_space=pl.ANY),
            pl.BlockSpec((rows, D_MODEL), lambda i, *_: (i, 0)),
            pl.BlockSpec((rows, LANES), lambda i, *_: (i, 0)),
            pl.BlockSpec((rows, LANES), lambda i, *_: (i, 0)),
            pl.BlockSpec((None, SUBLANES, 6 * D_MODEL), lambda i, *_: (li, 0, 0)),
            pl.BlockSpec((1, D_MODEL), lambda i, *_: (0, 0)),
            pl.BlockSpec((1, D_MODEL), lambda i, *_: (0, 0)),
        ],
        out_specs=_pair_specs(rows, D_MODEL, N_CTX // rows),
        scratch_shapes=[pltpu.VMEM((RT_ROWS * ROW_SLABS, LANES), F32)] * RT_PER_STEP + [
            pltpu.SemaphoreType.DMA((RT_PER_STEP,)),
        ],
    )
    return pl.pallas_call(
        _combine_kernel,
        grid_spec=grid_spec,
        out_shape=[jax.ShapeDtypeStruct((N_CTX, D_MODEL), F32), jax.ShapeDtypeStruct((N_DEC, D_MODEL), F32)],
        compiler_params=_cparams(1),
        name="moe_combine",
    )(tables["start"], tables["len"], tables["off"], y, x1, route_i, route_f, mod, ln_g, ln_b)


def _routing_tables(tile_cnt):
    cnt = tile_cnt[:, 0, :N_EXPERTS]
    totals = jnp.sum(cnt, axis=0)
    padded = (totals + BM - 1) // BM * BM
    pends = jnp.cumsum(padded)
    pstarts = pends - padded
    start = pstarts[None, :] + jnp.cumsum(cnt, axis=0) - cnt
    off = jnp.cumsum(cnt, axis=1) - cnt
    n_valid = (pends[-1] // BM).astype(I32)
    block_start = jnp.arange(NB, dtype=I32) * BM
    block_e = jnp.minimum(jnp.sum(block_start[:, None] >= pends[None, :], axis=1), N_EXPERTS - 1)
    last_e = block_e[jnp.maximum(n_valid - 1, 0)]
    block_e = jnp.where(jnp.arange(NB) < n_valid, block_e, last_e)
    ids = jnp.arange(N_EXPERTS, dtype=I32)
    order = jnp.sort(jnp.where(totals > 0, ids, N_EXPERTS))
    order = jnp.concatenate([jnp.where(order < N_EXPERTS, order, -1), jnp.full((2,), -1, I32)])
    return {
        "start": start.reshape(-1).astype(I32),
        "len": cnt.reshape(-1).astype(I32),
        "off": off.reshape(-1).astype(I32),
        "pad_start": (pstarts + totals).astype(I32),
        "pad_len": (padded - totals).astype(I32),
        "n_valid": n_valid.reshape(1),
        "block_expert": block_e.astype(I32),
        "expert_order": order.astype(I32),
    }


def _layer_params(p, li):
    eye = jnp.eye(N_LRU_BLOCKS, dtype=F32)
    w_rg_full = jnp.einsum("dkgio,gh->gidkho", p["w_rg"][li], eye).reshape(LRU_WIDTH, 4 * LRU_WIDTH)
    pad = LANES - N_EXPERTS
    return {
        "sgu_ln_g": p["sgu_ln_g"][li].reshape(1, SGU_WIDTH),
        "sgu_ln_b": p["sgu_ln_b"][li].reshape(1, SGU_WIDTH),
        "w_spatial": p["w_spatial"][li],
        "b_spatial_full": jnp.repeat(p["b_spatial"][li].T, SGU_GROUP, axis=1),
        "conv_w": p["conv_w"][li],
        "conv_b": p["conv_b"][li].reshape(1, LRU_WIDTH),
        "w_rg_full": w_rg_full,
        "b_rg_full": p["b_rg"][li].reshape(1, 4 * LRU_WIDTH),
        "lru_log": p["lru_log"][li],
        "w_out_all": p["w_out"],
        "ln_g1": p["ln_g"][li, 0].reshape(1, D_MODEL),
        "ln_b1": p["ln_b"][li, 0].reshape(1, D_MODEL),
        "ln_g2": p["ln_g"][li, 1].reshape(1, D_MODEL),
        "ln_b2": p["ln_b"][li, 1].reshape(1, D_MODEL),
        "w_router_pad": jnp.pad(p["w_router"][li], ((0, 0), (0, pad))),
        "b_router_pad": jnp.pad(p["b_router"][li], (0, pad)).reshape(1, LANES),
    }


def kernel(x_prompt, x_sample, cache_k, cache_v, state_lru, c, c_ctx, w_mod, b_mod, w_in, lam, subln_g, sgu_ln_g, sgu_ln_b, w_spatial, b_spatial, conv_w, conv_b, w_rg, b_rg, lru_log, w_out, ln_g, ln_b, w_router, b_router, w_gu, b_gu, w_down, b_down):
    p = dict(sgu_ln_g=sgu_ln_g, sgu_ln_b=sgu_ln_b, w_spatial=w_spatial, b_spatial=b_spatial, conv_w=conv_w,
             conv_b=conv_b, w_rg=w_rg, b_rg=b_rg, lru_log=lru_log, w_out=w_out, ln_g=ln_g, ln_b=ln_b,
             w_router=w_router, b_router=b_router)
    cvec8 = jnp.concatenate([c_ctx[None, :], c, jnp.zeros((SUBLANES - 1 - DEC_BATCH, D_MODEL), F32)], axis=0)
    mod = _modulation(cvec8, w_mod, b_mod)
    x_pair = (x_prompt.reshape(N_CTX, D_MODEL), x_sample.reshape(N_DEC, D_MODEL))
    tables = _rope_tables()
    zero_state = jnp.zeros((BATCH, 1, 2, LRU_WIDTH), F32)
    prev_kv = None
    new_s = []
    for li in range(DEPTH):
        lp = _layer_params(p, li)
        lam_init = 0.8 - 0.6 * math.exp(-0.3 * li)
        proj = _input_projection(*x_pair, mod, w_in, li)
        att_ctx, kc, vc = _context_attention(proj, lam[li], subln_g[li], lam_init, prev_kv)
        prev_kv = (kc, vc)
        att_dec = _denoise_attention(proj, cache_k, cache_v, lam[li], subln_g[li], li, lam_init, tables)
        sgu_ctx, lru_ctx, h_ctx = _mixers(proj, zero_state, 0, lp, SEQ, BATCH, 0)
        sgu_dec, lru_dec, _ = _mixers(proj, state_lru, li, lp, DEC_SEQ, DEC_BATCH, N_CTX // DEC_SEQ)
        x1, h2, route_i, route_f, tile_cnt = _output_projection(
            x_pair, (att_ctx, att_dec), (sgu_ctx, sgu_dec), (lru_ctx, lru_dec), mod, lp, li)
        rt = _routing_tables(tile_cnt)
        xs = _dispatch(rt, h2, route_i)
        y = _moe_blocks(rt, xs, w_gu, b_gu, w_down, b_down, li)
        x_pair = _combine(rt, y, x1, route_i, route_f, mod, lp["ln_g2"], lp["ln_b2"], li)
        new_s.append(h_ctx)
    y_prompt = x_pair[0].reshape(BATCH, SEQ, D_MODEL)
    y_sample = x_pair[1].reshape(DEC_BATCH, DEC_SEQ, D_MODEL)
    return (y_prompt, y_sample, prev_kv[0], prev_kv[1], jnp.stack(new_s, axis=1))
```

```python
import functools
import math

import numpy as np
import jax
import jax.numpy as jnp
from jax import lax
from jax.experimental import pallas as pl
from jax.experimental.pallas import tpu as pltpu

F32 = jnp.float32
BF16 = jnp.bfloat16
I32 = jnp.int32

D_MODEL = 1024
BATCH = 32
SEQ = 256
DEPTH = 2
DEC_BATCH = 2
DEC_SEQ = 1024
PAST_LEN = 256
GRID_W = 64
HEAD_DIM = 64
ATT_WIDTH = D_MODEL // 2
SGU_WIDTH = D_MODEL // 4
LRU_WIDTH = D_MODEL // 4
N_ATT_HEADS = ATT_WIDTH // HEAD_DIM
ATT_HALF = HEAD_DIM // 2
ROPE_FREQS = ATT_HALF // 4
ROPE_THETA = 10000.0
CHUNK = 128
N_SGU_GROUPS = 4
SGU_GROUP = SGU_WIDTH // N_SGU_GROUPS
N_LRU_BLOCKS = 4
LRU_BLOCK = LRU_WIDTH // N_LRU_BLOCKS
CONV_W = 4
LRU_C = 8.0
IN_COLS = 3 * ATT_WIDTH + 2 * SGU_WIDTH + 2 * LRU_WIDTH
N_EXPERTS = 32
TOP_K = 4
D_EXPERT = D_MODEL
SWIGLU_LIMIT = 7.0
SWIGLU_ALPHA = 1.702
DN_ALPHA = (2 * DEPTH) ** 0.25
EPS = 1e-5

N_CTX = BATCH * SEQ
N_DEC = DEC_BATCH * DEC_SEQ
N_TOK = N_CTX + N_DEC

LANES = 128
SUBLANES = 8
ROW_SLABS = D_MODEL // LANES

TM = 512
CTX_TILES = N_CTX // TM
TILES_PER_DEC = DEC_SEQ // TM
N_TILES = N_TOK // TM
MOD_TN = 1024
RT = 256
N_RT = N_TOK // RT
RT_PER_TM = TM // RT
RT_ROWS = RT * TOP_K
RT_PER_STEP = 4
FETCH_AHEAD = 2
BM = 256
NB = N_TOK * TOP_K // BM + N_EXPERTS
WEIGHT_PIECES = 8
X_BUFFERS = 3
N_SLOTS = NB * BM
RUN_BITS = tuple(1 << b for b in range(RT.bit_length() - 1, -1, -1))
LONG_RUN = 2 * RT_ROWS // N_EXPERTS
PAD_BITS = tuple(1 << b for b in range(BM.bit_length() - 2, -1, -1))
VMEM_LIMIT = 56 * 1024 * 1024


def _cparams(n_axes):
    return pltpu.CompilerParams(
        dimension_semantics=("arbitrary",) * n_axes,
        vmem_limit_bytes=VMEM_LIMIT)


def _mod_row(i):
    return jnp.where(i < CTX_TILES, 0, 1 + (i - CTX_TILES) // TILES_PER_DEC)


def _layer_norm_rows(z, g, b):
    mu = jnp.mean(z, axis=-1, keepdims=True)
    zc = z - mu
    var = jnp.mean(zc * zc, axis=-1, keepdims=True)
    return zc * lax.rsqrt(var + EPS) * g + b


def _pair_specs(tile, width, n_ctx_tiles):
    ctx = pl.BlockSpec((tile, width), lambda i, *_: (jnp.minimum(i, n_ctx_tiles - 1), 0))
    dec = pl.BlockSpec((tile, width), lambda i, *_: (jnp.maximum(i - n_ctx_tiles, 0), 0))
    return [ctx, dec]


def _pair_read(i, n_ctx_tiles, ctx_ref, dec_ref):
    return jnp.where(i < n_ctx_tiles, ctx_ref[...], dec_ref[...])


def _mod_kernel(cvec_ref, w_ref, b_ref, o_ref):
    cv = cvec_ref[...]
    s = cv * jax.nn.sigmoid(cv)
    s_t = s.T
    w = w_ref[0]
    rows = [jnp.sum(s_t[:, r:r + 1] * w, axis=0, keepdims=True) for r in range(1 + DEC_BATCH)]
    rows.append(jnp.zeros((SUBLANES - 1 - DEC_BATCH, MOD_TN), F32))
    o_ref[0] = jnp.concatenate(rows, axis=0) + b_ref[0]


def _modulation(cvec8, w_mod, b_mod):
    n_out = w_mod.shape[-1]
    return pl.pallas_call(
        _mod_kernel,
        grid=(DEPTH, n_out // MOD_TN),
        in_specs=[
            pl.BlockSpec((SUBLANES, D_MODEL), lambda l, j: (0, 0)),
            pl.BlockSpec((1, D_MODEL, MOD_TN), lambda l, j: (l, 0, j)),
            pl.BlockSpec((1, 1, MOD_TN), lambda l, j: (l, 0, j)),
        ],
        out_specs=pl.BlockSpec((1, SUBLANES, MOD_TN), lambda l, j: (l, 0, j)),
        out_shape=jax.ShapeDtypeStruct((DEPTH, SUBLANES, n_out), F32),
        compiler_params=_cparams(2),
        name="modulation",
    )(cvec8, w_mod, b_mod.reshape(DEPTH, 1, n_out))


def _inproj_kernel(xc_ref, xd_ref, mod_ref, w_ref, o_ref, wbf_ref):
    i = pl.program_id(0)

    @pl.when(i == 0)
    def _():
        wbf_ref[...] = w_ref[...].astype(BF16)

    m = mod_ref[pl.ds(_mod_row(i), 1), :]
    sh1 = m[:, 0:D_MODEL]
    sc1 = m[:, D_MODEL:2 * D_MODEL]
    x = _pair_read(i, CTX_TILES, xc_ref, xd_ref)
    h = (x * (1.0 + sc1) + sh1).astype(BF16)
    o_ref[...] = jnp.dot(h, wbf_ref[...], preferred_element_type=F32)


def _input_projection(x_ctx, x_dec, mod, w_in, li):
    return pl.pallas_call(
        _inproj_kernel,
        grid=(N_TILES,),
        in_specs=_pair_specs(TM, D_MODEL, CTX_TILES) + [
            pl.BlockSpec((None, SUBLANES, 6 * D_MODEL), lambda i: (li, 0, 0)),
            pl.BlockSpec((None, D_MODEL, IN_COLS), lambda i: (li, 0, 0), pipeline_mode=pl.Buffered(1)),
        ],
        out_specs=pl.BlockSpec((TM, IN_COLS), lambda i: (i, 0)),
        out_shape=jax.ShapeDtypeStruct((N_TOK, IN_COLS), F32),
        scratch_shapes=[pltpu.VMEM((D_MODEL, IN_COLS), BF16)],
        compiler_params=_cparams(1),
        name="input_projection",
    )(x_ctx, x_dec, mod, w_in)


def _lambda(lam_ref, lam_init):
    lm = lam_ref[...]
    a = jnp.sum(lm[0:1] * lm[1:2], axis=-1, keepdims=True)
    b = jnp.sum(lm[2:3] * lm[3:4], axis=-1, keepdims=True)
    return jnp.exp(a) - jnp.exp(b) + lam_init


_NT = (((1,), (1,)), ((), ()))
_TN = (((0,), (0,)), ((), ()))
_ATT_SCALE = ATT_HALF ** -0.5
_Q_SCALE = _ATT_SCALE * math.log2(math.e)
KEY_CHUNK = 128


def _attention_heads(q_parts, k_segs, values, s_scr, e_scr, lam, g_col, lam_init):
    n_heads = len(q_parts)
    for h in range(n_heads):
        for c in range(2):
            row = 0
            for k in k_segs[h][c]:
                s_scr[2 * h + c, row:row + k.shape[0], :] = lax.dot_general(
                    k, q_parts[h][c], _NT, preferred_element_type=F32)
                row += k.shape[0]
    n_keys, n_q = s_scr.shape[1:]
    chunks = [slice(r, r + KEY_CHUNK) for r in range(0, n_keys, KEY_CHUNK)]
    for n in range(2 * n_heads):
        tops = [jnp.max(s_scr[n, rows, :].reshape(KEY_CHUNK // SUBLANES, SUBLANES, n_q), axis=0)
                for rows in chunks]
        top = jnp.max(functools.reduce(jnp.maximum, tops), axis=0, keepdims=True)
        for rows in chunks:
            e_scr[n, rows, :] = jnp.exp2(s_scr[n, rows, :] - top).astype(BF16)
    outs = []
    for h in range(n_heads):
        v_aug = jnp.concatenate([values[h], jnp.ones_like(values[h])], axis=-1)
        normed = []
        for c in range(2):
            acc = lax.dot_general(v_aug, e_scr[2 * h + c], _TN, preferred_element_type=F32)
            normed.append(acc[:HEAD_DIM] * (1.0 / acc[HEAD_DIM:HEAD_DIM + 1]))
        o_t = normed[0] - lam * normed[1]
        ms = jnp.mean(o_t * o_t, axis=0, keepdims=True)
        outs.append(o_t * lax.rsqrt(ms + EPS) * g_col * (1.0 - lam_init))
    return outs


def _head_cols(h):
    lo = h * HEAD_DIM
    return [slice(lo + c * ATT_HALF, lo + (c + 1) * ATT_HALF) for c in range(2)]


def _ctx_attn_kernel(lam_ref, g_ref, q_ref, k_ref, v_ref, *rest, lam_init, stacked):
    *rest, s_scr, e_scr = rest
    if stacked:
        pk_ref, pv_ref, att_ref, ck_ref, cv_ref = rest
        ck_ref[0, 0] = pk_ref[0]
        cv_ref[0, 0] = pv_ref[0]
        put_k = lambda h, val: ck_ref.__setitem__((0, 1, h), val)
        put_v = lambda h, val: cv_ref.__setitem__((0, 1, h), val)
    else:
        att_ref, ck_ref, cv_ref = rest
        put_k = lambda h, val: ck_ref.__setitem__((0, h), val)
        put_v = lambda h, val: cv_ref.__setitem__((0, h), val)
    lam = _lambda(lam_ref, lam_init)
    k = k_ref[...]
    v = v_ref[...]
    qb = (q_ref[...] * _Q_SCALE).astype(BF16)
    kb = k.astype(BF16)
    vb = v.astype(BF16)
    heads = range(N_ATT_HEADS)
    outs = _attention_heads([[qb[:, cs] for cs in _head_cols(h)] for h in heads],
                            [[[kb[:, cs]] for cs in _head_cols(h)] for h in heads],
                            [vb[:, h * HEAD_DIM:(h + 1) * HEAD_DIM] for h in heads],
                            s_scr, e_scr, lam, g_ref[...], lam_init)
    for h in heads:
        put_k(h, k[:, h * HEAD_DIM:(h + 1) * HEAD_DIM])
        put_v(h, v[:, h * HEAD_DIM:(h + 1) * HEAD_DIM])
    att_ref[...] = jnp.concatenate(outs, axis=0).T


def _context_attention(proj, lam, subln_g, lam_init, prev_kv=None):
    stacked = prev_kv is not None
    per_layer = pl.BlockSpec((1, N_ATT_HEADS, SEQ, HEAD_DIM), lambda b: (b, 0, 0, 0))
    if stacked:
        kv_shape = jax.ShapeDtypeStruct((BATCH, DEPTH, N_ATT_HEADS, SEQ, HEAD_DIM), F32)
        kv_spec = pl.BlockSpec((1, DEPTH, N_ATT_HEADS, SEQ, HEAD_DIM), lambda b: (b, 0, 0, 0, 0))
    else:
        kv_shape = jax.ShapeDtypeStruct((BATCH, N_ATT_HEADS, SEQ, HEAD_DIM), F32)
        kv_spec = per_layer
    return pl.pallas_call(
        functools.partial(_ctx_attn_kernel, lam_init=lam_init, stacked=stacked),
        grid=(BATCH,),
        in_specs=[
            pl.BlockSpec((4, ATT_HALF), lambda b: (0, 0)),
            pl.BlockSpec((HEAD_DIM, 1), lambda b: (0, 0)),
            pl.BlockSpec((SEQ, ATT_WIDTH), lambda b: (b, 0)),
            pl.BlockSpec((SEQ, ATT_WIDTH), lambda b: (b, 1)),
            pl.BlockSpec((SEQ, ATT_WIDTH), lambda b: (b, 2)),
        ] + ([per_layer, per_layer] if stacked else []),
        out_specs=[pl.BlockSpec((SEQ, ATT_WIDTH), lambda b: (b, 0)), kv_spec, kv_spec],
        out_shape=[jax.ShapeDtypeStruct((N_CTX, ATT_WIDTH), F32), kv_shape, kv_shape],
        scratch_shapes=[pltpu.VMEM((2 * N_ATT_HEADS, SEQ, SEQ), F32),
                        pltpu.VMEM((2 * N_ATT_HEADS, SEQ, SEQ), BF16)],
        compiler_params=_cparams(1),
        name="context_attention",
    )(lam, subln_g.reshape(HEAD_DIM, 1), proj, proj, proj, *(prev_kv if stacked else ()))


def _rope_tables():
    t = np.arange(DEC_SEQ)
    pos = np.stack([t // GRID_W, t % GRID_W], axis=1).astype(np.float32)
    inv = (np.float32(ROPE_THETA) ** (-np.arange(ROPE_FREQS, dtype=np.float32) / np.float32(ROPE_FREQS)))
    j = np.arange(HEAD_DIM)
    d = j % ATT_HALF
    axis = d // (2 * ROPE_FREQS)
    u = d % (2 * ROPE_FREQS)
    ang = pos[:, axis] * inv[u % ROPE_FREQS][None, :].astype(np.float32)
    cos = np.cos(ang).astype(np.float32)
    sin = np.sin(ang).astype(np.float32)
    first = (u < ROPE_FREQS)[None, :]
    s_next = np.where(first, -sin, 0.0).astype(np.float32)
    s_prev = np.where(first, 0.0, sin).astype(np.float32)
    tile = lambda a: jnp.asarray(np.tile(a, (1, N_ATT_HEADS)))
    return tile(cos), tile(s_next), tile(s_prev)


def _rotate(x, cos, s_next, s_prev):
    width = x.shape[-1]
    return (x * cos + pltpu.roll(x, width - ROPE_FREQS, axis=1) * s_next
            + pltpu.roll(x, ROPE_FREQS, axis=1) * s_prev)


QB = 256
Q_STEPS = DEC_SEQ // QB
DEC_HEAD_GROUP = 2


def _dec_attn_kernel(lam_ref, g_ref, q_ref, k_ref, v_ref, ck_ref, cv_ref,
                     cq_ref, snq_ref, spq_ref, ck_tab, snk_tab, spk_tab,
                     att_ref, krot_ref, s_scr, e_scr, *, lam_init):
    j = pl.program_id(1)

    @pl.when(j == 0)
    def _():
        krot_ref[...] = _rotate(k_ref[...], ck_tab[...], snk_tab[...], spk_tab[...]).astype(BF16)

    lam = _lambda(lam_ref, lam_init)
    qb = (_rotate(q_ref[...], cq_ref[...], snq_ref[...], spq_ref[...]) * _Q_SCALE).astype(BF16)
    kb = krot_ref[...]
    vb = v_ref[...].astype(BF16)
    outs = []
    for h0 in range(0, N_ATT_HEADS, DEC_HEAD_GROUP):
        heads = range(h0, h0 + DEC_HEAD_GROUP)
        past_k = [ck_ref[0, 0, h].astype(BF16) for h in heads]
        k_segs = [[[pk[:, c * ATT_HALF:(c + 1) * ATT_HALF], kb[:, _head_cols(h)[c]]] for c in range(2)]
                  for h, pk in zip(heads, past_k)]
        values = [jnp.concatenate([cv_ref[0, 0, h].astype(BF16), vb[:, h * HEAD_DIM:(h + 1) * HEAD_DIM]], axis=0)
                  for h in heads]
        outs += _attention_heads([[qb[:, cs] for cs in _head_cols(h)] for h in heads], k_segs, values,
                                 s_scr, e_scr, lam, g_ref[...], lam_init)
    att_ref[...] = jnp.concatenate(outs, axis=0).T


def _denoise_attention(proj, cache_k, cache_v, lam, subln_g, li, lam_init, tables):
    cos, s_next, s_prev = tables
    row0 = N_CTX // QB
    seq0 = N_CTX // DEC_SEQ
    q_tab = pl.BlockSpec((QB, ATT_WIDTH), lambda b, j: (j, 0))
    k_tab = pl.BlockSpec((DEC_SEQ, ATT_WIDTH), lambda b, j: (0, 0))
    cache_spec = pl.BlockSpec((1, 1, N_ATT_HEADS, PAST_LEN, HEAD_DIM), lambda b, j: (b, li, 0, 0, 0))
    return pl.pallas_call(
        functools.partial(_dec_attn_kernel, lam_init=lam_init),
        grid=(DEC_BATCH, Q_STEPS),
        in_specs=[
            pl.BlockSpec((4, ATT_HALF), lambda b, j: (0, 0)),
            pl.BlockSpec((HEAD_DIM, 1), lambda b, j: (0, 0)),
            pl.BlockSpec((QB, ATT_WIDTH), lambda b, j: (row0 + b * Q_STEPS + j, 0)),
            pl.BlockSpec((DEC_SEQ, ATT_WIDTH), lambda b, j: (seq0 + b, 1)),
            pl.BlockSpec((DEC_SEQ, ATT_WIDTH), lambda b, j: (seq0 + b, 2)),
            cache_spec, cache_spec,
            q_tab, q_tab, q_tab, k_tab, k_tab, k_tab,
        ],
        out_specs=pl.BlockSpec((QB, ATT_WIDTH), lambda b, j: (b * Q_STEPS + j, 0)),
        out_shape=jax.ShapeDtypeStruct((N_DEC, ATT_WIDTH), F32),
        scratch_shapes=[pltpu.VMEM((DEC_SEQ, ATT_WIDTH), BF16),
                        pltpu.VMEM((2 * DEC_HEAD_GROUP, PAST_LEN + DEC_SEQ, QB), F32),
                        pltpu.VMEM((2 * DEC_HEAD_GROUP, PAST_LEN + DEC_SEQ, QB), BF16)],
        compiler_params=_cparams(2),
        name="denoise_attention",
    )(lam, subln_g.reshape(HEAD_DIM, 1), proj, proj, proj, cache_k, cache_v,
      cos, s_next, s_prev, cos, s_next, s_prev)


def _softplus(z):
    return jnp.maximum(z, 0.0) + jnp.log1p(jnp.exp(-jnp.abs(z)))


def _mixer_kernel(su_ref, sv_ref, rx_ref, rg_ref, h0_ref, lng_ref, lnb_ref, ws_ref, bs_ref,
                  cw_ref, cb_ref, wrg_ref, brg_ref, lrulog_ref,
                  sgu_ref, lru_ref, hlast_ref, a_scr, b_scr, h_scr, *, seq_len):
    vn = _layer_norm_rows(sv_ref[...], lng_ref[...], lnb_ref[...])
    lane_group = lax.broadcasted_iota(I32, (CHUNK, SGU_WIDTH), 1) // SGU_GROUP
    for n in range(seq_len // CHUNK):
        rows = slice(n * CHUNK, (n + 1) * CHUNK)
        vc = vn[rows].astype(BF16)
        s = jnp.zeros((CHUNK, SGU_WIDTH), F32)
        for g in range(N_SGU_GROUPS):
            sg = jnp.dot(ws_ref[g].astype(BF16), vc, preferred_element_type=F32)
            s = jnp.where(lane_group == g, sg, s)
        sgu_ref[rows, :] = su_ref[rows, :] * (s + bs_ref[...])

    x = rx_ref[...]
    row = lax.broadcasted_iota(I32, (seq_len, LRU_WIDTH), 0)

    def shifted(val, d, fill):
        rolled = pltpu.roll(val, d % seq_len, axis=0)
        inside = (row >= d) if d > 0 else (row < seq_len + d)
        return jnp.where(inside, rolled, fill)

    left = CONV_W // 2
    xc = cb_ref[...] + x * cw_ref[left:left + 1, :]
    for tap in range(CONV_W):
        if tap != left:
            xc = xc + shifted(x, left - tap, 0.0) * cw_ref[tap:tap + 1, :]
    pre = jnp.dot(xc.astype(BF16), wrg_ref[...].astype(BF16), preferred_element_type=F32) + brg_ref[...]
    gates = 0.5 + 0.5 * jnp.tanh(0.5 * pre)
    in_chunk = row % SUBLANES
    n_chunks = seq_len // SUBLANES
    h0 = h0_ref[0, 0]
    lasts = []
    for direction in range(2):
        reverse = direction == 1
        base = direction * 2 * LRU_WIDTH
        r = gates[:, base:base + LRU_WIDTH]
        gi = gates[:, base + LRU_WIDTH:base + 2 * LRU_WIDTH]
        log_a = -LRU_C * r * _softplus(-lrulog_ref[direction:direction + 1, :])
        a = jnp.exp(log_a)
        b = jnp.sqrt(-jnp.tanh(log_a) * (a * a + 1.0)) * gi * xc
        def chunk_roll(val, shift):
            chunks = val.reshape(n_chunks, SUBLANES, LRU_WIDTH)
            return pltpu.roll(chunks, shift, axis=1).reshape(seq_len, LRU_WIDTH)

        for d in (1, 2, 4):
            if reverse:
                inside = in_chunk < SUBLANES - d
                a_n = jnp.where(inside, chunk_roll(a, SUBLANES - d), 1.0)
                b_n = jnp.where(inside, chunk_roll(b, SUBLANES - d), 0.0)
            else:
                inside = in_chunk >= d
                a_n = jnp.where(inside, chunk_roll(a, d), 1.0)
                b_n = jnp.where(inside, chunk_roll(b, d), 0.0)
            b = a * b_n + b
            a = a * a_n
        a_scr[...] = a
        b_scr[...] = b

        def chunk_step(c, carry, reverse=reverse):
            cc = n_chunks - 1 - c if reverse else c
            off = pl.multiple_of(cc * SUBLANES, SUBLANES)
            hc = a_scr[pl.ds(off, SUBLANES), :] * carry + b_scr[pl.ds(off, SUBLANES), :]
            if reverse:
                h_scr[pl.ds(off, SUBLANES), :] = h_scr[pl.ds(off, SUBLANES), :] + hc
                return hc[0:1, :]
            h_scr[pl.ds(off, SUBLANES), :] = hc
            return hc[SUBLANES - 1:SUBLANES, :]

        lasts.append(lax.fori_loop(0, n_chunks, chunk_step, h0[direction:direction + 1, :]))
    lru_ref[...] = h_scr[...] * jax.nn.gelu(rg_ref[...])
    hlast_ref[0] = jnp.concatenate(lasts, axis=0)


def _mixers(proj, h0, h0_layer, lp, seq_len, n_seq, row_block0):
    col0 = 3 * ATT_WIDTH // SGU_WIDTH
    col = lambda c: pl.BlockSpec((seq_len, SGU_WIDTH), lambda b: (row_block0 + b, col0 + c))
    full = lambda shape: pl.BlockSpec(shape, lambda b: (0,) * len(shape))
    out_rows = pl.BlockSpec((seq_len, SGU_WIDTH), lambda b: (b, 0))
    return pl.pallas_call(
        functools.partial(_mixer_kernel, seq_len=seq_len),
        grid=(n_seq,),
        in_specs=[
            col(0), col(1), col(2), col(3),
            pl.BlockSpec((1, 1, 2, LRU_WIDTH), lambda b: (b, h0_layer, 0, 0)),
            full((1, SGU_WIDTH)), full((1, SGU_WIDTH)),
            full((N_SGU_GROUPS, CHUNK, CHUNK)), full((CHUNK, SGU_WIDTH)),
            full((CONV_W, LRU_WIDTH)), full((1, LRU_WIDTH)),
            full((LRU_WIDTH, 4 * LRU_WIDTH)), full((1, 4 * LRU_WIDTH)),
            full((2, LRU_WIDTH)),
        ],
        out_specs=[out_rows, out_rows, pl.BlockSpec((1, 2, LRU_WIDTH), lambda b: (b, 0, 0))],
        out_shape=[
            jax.ShapeDtypeStruct((n_seq * seq_len, SGU_WIDTH), F32),
            jax.ShapeDtypeStruct((n_seq * seq_len, LRU_WIDTH), F32),
            jax.ShapeDtypeStruct((n_seq, 2, LRU_WIDTH), F32),
        ],
        scratch_shapes=[pltpu.VMEM((seq_len, LRU_WIDTH), F32)] * 3,
        compiler_params=_cparams(1),
        name="mixers_%d" % seq_len,
    )(proj, proj, proj, proj, h0, lp["sgu_ln_g"], lp["sgu_ln_b"], lp["w_spatial"], lp["b_spatial_full"],
      lp["conv_w"], lp["conv_b"], lp["w_rg_full"], lp["b_rg_full"], lp["lru_log"])


def _outproj_kernel(xc_ref, xd_ref, ac_ref, ad_ref, sc_ref, sd_ref, lc_ref, ld_ref,
                    mod_ref, wout_ref, lng_ref, lnb_ref, wr_ref, br_ref,
                    x1_ref, h2_ref, ri_ref, rf_ref, cnt_ref, wbf_ref):
    i = pl.program_id(0)

    @pl.when(i == 0)
    def _():
        wbf_ref[...] = wout_ref[...].astype(BF16)

    m = mod_ref[pl.ds(_mod_row(i), 1), :]
    g1 = m[:, 2 * D_MODEL:3 * D_MODEL]
    sh2 = m[:, 3 * D_MODEL:4 * D_MODEL]
    sc2 = m[:, 4 * D_MODEL:5 * D_MODEL]
    a0, a1 = ATT_WIDTH, ATT_WIDTH + SGU_WIDTH
    x = _pair_read(i, CTX_TILES, xc_ref, xd_ref)
    att = _pair_read(i, CTX_TILES, ac_ref, ad_ref).astype(BF16)
    sgu = _pair_read(i, CTX_TILES, sc_ref, sd_ref).astype(BF16)
    lru = _pair_read(i, CTX_TILES, lc_ref, ld_ref).astype(BF16)
    mix = (jnp.dot(att, wbf_ref[0:a0, :], preferred_element_type=F32)
           + jnp.dot(sgu, wbf_ref[a0:a1, :], preferred_element_type=F32)
           + jnp.dot(lru, wbf_ref[a1:, :], preferred_element_type=F32))
    x1 = _layer_norm_rows(DN_ALPHA * x + g1 * mix, lng_ref[...], lnb_ref[...])
    x1_ref[...] = x1
    h2 = x1 * (1.0 + sc2) + sh2
    h2_ref[...] = h2.astype(BF16)

    logits = jnp.dot(h2.astype(BF16), wr_ref[...].astype(BF16), preferred_element_type=F32) + br_ref[...]
    lane = lax.broadcasted_iota(I32, (TM, LANES), 1)
    lane_f = lane.astype(F32)
    neg_inf = jnp.float32(-jnp.inf)
    work = jnp.where(lane < N_EXPERTS, logits, neg_inf)
    vals, idxs = [], []
    for _ in range(TOP_K):
        top = jnp.max(work, axis=-1, keepdims=True)
        idx = jnp.min(jnp.where(work == top, lane_f, float(LANES)), axis=-1, keepdims=True)
        vals.append(top)
        idxs.append(idx)
        work = jnp.where(lane_f == idx, neg_inf, work)
    exps = [jnp.exp(v - vals[0]) for v in vals]
    denom = exps[0] + exps[1] + exps[2] + exps[3]
    onehot = jnp.zeros((TM, LANES), F32)
    for idx in idxs:
        onehot = onehot + (lane_f == idx).astype(F32)
    r_i = lax.broadcasted_iota(I32, (RT, RT), 0)
    c_i = lax.broadcasted_iota(I32, (RT, RT), 1)
    tri = (r_i > c_i).astype(F32).astype(BF16)
    upper = (lax.broadcasted_iota(I32, (LANES, LANES), 0)
             < lax.broadcasted_iota(I32, (LANES, LANES), 1)).astype(F32).astype(BF16)
    packed_pos = []
    for t in range(RT_PER_TM):
        hot = onehot[t * RT:(t + 1) * RT]
        total = jnp.broadcast_to(jnp.sum(hot, axis=0, keepdims=True), (SUBLANES, LANES))
        cnt_ref[t] = total.astype(I32)
        run_start = jnp.dot(total.astype(BF16), upper, preferred_element_type=F32)[0:1, :]
        packed_pos.append(jnp.dot(tri, hot.astype(BF16), preferred_element_type=F32) + run_start)
    packed_pos = jnp.concatenate(packed_pos, axis=0)
    ri = jnp.zeros((TM, LANES), F32)
    rf = jnp.zeros((TM, LANES), F32)
    for k in range(TOP_K):
        pos = jnp.sum(jnp.where(lane_f == idxs[k], packed_pos, 0.0), axis=-1, keepdims=True)
        ri = jnp.where(lane == k, idxs[k], ri)
        ri = jnp.where(lane == TOP_K + k, pos, ri)
        rf = jnp.where(lane == k, exps[k] / denom, rf)
    ri_ref[...] = ri.astype(I32)
    rf_ref[...] = rf


def _output_projection(x_pair, att_pair, sgu_pair, lru_pair, mod, lp, li):
    rows = lambda w: pl.BlockSpec((TM, w), lambda i: (i, 0))
    full = lambda shape: pl.BlockSpec(shape, lambda i: (0,) * len(shape))
    return pl.pallas_call(
        _outproj_kernel,
        grid=(N_TILES,),
        in_specs=(_pair_specs(TM, D_MODEL, CTX_TILES) + _pair_specs(TM, ATT_WIDTH, CTX_TILES)
                  + _pair_specs(TM, SGU_WIDTH, CTX_TILES) + _pair_specs(TM, LRU_WIDTH, CTX_TILES) + [
            pl.BlockSpec((None, SUBLANES, 6 * D_MODEL), lambda i: (li, 0, 0)),
            pl.BlockSpec((None, D_MODEL, D_MODEL), lambda i: (li, 0, 0), pipeline_mode=pl.Buffered(1)),
            full((1, D_MODEL)), full((1, D_MODEL)),
            full((D_MODEL, LANES)), full((1, LANES)),
        ]),
        out_specs=[
            rows(D_MODEL), rows(D_MODEL), rows(LANES), rows(LANES),
            pl.BlockSpec((RT_PER_TM, SUBLANES, LANES), lambda i: (i, 0, 0)),
        ],
        out_shape=[
            jax.ShapeDtypeStruct((N_TOK, D_MODEL), F32),
            jax.ShapeDtypeStruct((N_TOK, D_MODEL), BF16),
            jax.ShapeDtypeStruct((N_TOK, LANES), I32),
            jax.ShapeDtypeStruct((N_TOK, LANES), F32),
            jax.ShapeDtypeStruct((N_RT, SUBLANES, LANES), I32),
        ],
        scratch_shapes=[pltpu.VMEM((D_MODEL, D_MODEL), BF16)],
        compiler_params=_cparams(1),
        name="output_projection",
    )(*x_pair, *att_pair, *sgu_pair, *lru_pair, mod, lp["w_out_all"], lp["ln_g1"], lp["ln_b1"],
      lp["w_router_pad"], lp["b_router_pad"])


def _row_ds(row, n_rows):
    return pl.ds(pl.multiple_of(row * ROW_SLABS, ROW_SLABS), n_rows * ROW_SLABS)


def _for_each_run_piece(length, bits, fn):
    for bit in bits:
        done = length & (-2 * bit)
        @pl.when((length & bit) != 0)
        def _(done=done, bit=bit):
            fn(done, bit)


def _tile_run_copies(tile, start_ref, len_ref, off_ref, sorted_hbm, buf, sem, to_sorted, live=None):
    def runs(bits):
        for e in range(N_EXPERTS):
            t = tile * N_EXPERTS + e
            start, length, off = start_ref[t], len_ref[t], off_ref[t]
            if live is not None:
                length = jnp.where(live, length, 0)

            def piece(done, bit, start=start, off=off):
                packed = buf.at[_row_ds(off + done, bit), :]
                srt = sorted_hbm.at[_row_ds(start + done, bit), :]
                if to_sorted:
                    pltpu.make_async_copy(packed, srt, sem).start()
                else:
                    pltpu.make_async_copy(srt, packed, sem).start()

            _for_each_run_piece(length, bits, piece)

    has_long = len_ref[N_RT * N_EXPERTS + tile] != 0
    if live is not None:
        has_long = jnp.logical_and(has_long, live)

    @pl.when(has_long)
    def _():
        runs([b for b in RUN_BITS if b >= LONG_RUN])

    runs([b for b in RUN_BITS if b < LONG_RUN])


def _tile_runs_wait(buf, sem):
    pltpu.make_async_copy(buf, buf, sem).wait()


def _packed_positions(ri):
    return [ri[:, TOP_K + k:TOP_K + k + 1].astype(F32) for k in range(TOP_K)]


def _slab_columns(buf, row0, n_rows):
    return jnp.concatenate(
        [buf[pl.ds(row0 * ROW_SLABS + s, n_rows, stride=ROW_SLABS), :] for s in range(ROW_SLABS)], axis=-1)


def _dispatch_kernel(start_ref, len_ref, off_ref, pstart_ref, plen_ref, nv_ref,
                     h2_ref, ri_ref, xs_hbm, *scratch):
    bufs, (zbuf, sem, zsem) = scratch[:RT_PER_STEP], scratch[RT_PER_STEP:]
    i = pl.program_id(0)
    n_steps = pl.num_programs(0)

    @pl.when(i == 0)
    def _():
        zbuf[...] = jnp.zeros_like(zbuf)

        def zero_fill(wait):
            def go(cp):
                cp.wait() if wait else cp.start()

            def per_expert(e, carry):
                def piece(done, bit):
                    go(pltpu.make_async_copy(zbuf.at[pl.ds(0, bit * ROW_SLABS), :],
                                             xs_hbm.at[_row_ds(pstart_ref[e] + done, bit), :], zsem))
                _for_each_run_piece(plen_ref[e], PAD_BITS, piece)
                return carry
            lax.fori_loop(0, N_EXPERTS, per_expert, 0)

            def per_block(b, carry):
                go(pltpu.make_async_copy(zbuf, xs_hbm.at[_row_ds(b * BM, BM), :], zsem))
                return carry
            lax.fori_loop(nv_ref[0], NB, per_block, 0)

        zero_fill(False)
        zero_fill(True)

    def send(tile, t, live):
        _tile_run_copies(tile, start_ref, len_ref, off_ref, xs_hbm, bufs[t], sem.at[t], to_sorted=True, live=live)

    col = lax.broadcasted_iota(I32, (RT, RT_ROWS), 1).astype(F32)
    for t in range(RT_PER_STEP):
        buf = bufs[t]
        rows = slice(t * RT, (t + 1) * RT)
        tile = i * RT_PER_STEP + t

        @pl.when(i >= 1)
        def _(buf=buf, t=t):
            _tile_runs_wait(buf, sem.at[t])

        send(jnp.maximum(tile - 1, 0), (t - 1) % RT_PER_STEP, tile >= 1)
        pos = _packed_positions(ri_ref[rows, :])
        sel = jnp.zeros((RT, RT_ROWS), F32)
        for p in pos:
            sel = sel + (col == p).astype(F32)
        packed = lax.dot_general(sel.astype(BF16), h2_ref[rows, :], _TN, preferred_element_type=F32)
        for s in range(ROW_SLABS):
            buf[pl.ds(s, RT_ROWS, stride=ROW_SLABS), :] = packed[:, s * LANES:(s + 1) * LANES]

    @pl.when(i == n_steps - 1)
    def _():
        send(N_RT - 1, RT_PER_STEP - 1, True)
        for t in range(RT_PER_STEP):
            _tile_runs_wait(bufs[t], sem.at[t])


def _dispatch(tables, h2, route_i):
    grid_spec = pltpu.PrefetchScalarGridSpec(
        num_scalar_prefetch=6,
        grid=(N_RT // RT_PER_STEP,),
        in_specs=[
            pl.BlockSpec((RT_PER_STEP * RT, D_MODEL), lambda i, *_: (i, 0)),
            pl.BlockSpec((RT_PER_STEP * RT, LANES), lambda i, *_: (i, 0)),
        ],
        out_specs=pl.BlockSpec(memory_space=pl.ANY),
        scratch_shapes=[pltpu.VMEM((RT_ROWS * ROW_SLABS, LANES), F32)] * RT_PER_STEP + [
            pltpu.VMEM((BM * ROW_SLABS, LANES), F32),
            pltpu.SemaphoreType.DMA((RT_PER_STEP,)),
            pltpu.SemaphoreType.DMA(()),
        ],
    )
    return pl.pallas_call(
        _dispatch_kernel,
        grid_spec=grid_spec,
        out_shape=jax.ShapeDtypeStruct((N_SLOTS * ROW_SLABS, LANES), F32),
        compiler_params=_cparams(1),
        name="moe_dispatch",
    )(tables["start"], tables["len"], tables["off"], tables["pad_start"], tables["pad_len"], tables["n_valid"],
      h2, route_i)


def _moe_kernel(be_ref, eo_ref, nv_ref, xs_hbm, wgu_hbm, bgu_ref, wdn_hbm, bdn_ref, y_hbm,
                xbuf, ybuf, gu_stage, dn_stage, wgu_bf, wdn_bf, wsem, xsem, ysem, *, li):
    n_valid = nv_ref[0]
    rows_per_piece = D_MODEL // WEIGHT_PIECES

    def weight_copies(e, stage):
        cps = []
        for p in range(WEIGHT_PIECES):
            band = pl.ds(p * rows_per_piece, rows_per_piece)
            cps.append(pltpu.make_async_copy(wgu_hbm.at[li, e, band, :], gu_stage.at[stage, band, :], wsem.at[stage]))
            cps.append(pltpu.make_async_copy(wdn_hbm.at[li, e, band, :], dn_stage.at[stage, band, :], wsem.at[stage]))
        return cps

    def fetch(k):
        @pl.when(eo_ref[k] >= 0)
        def _():
            for cp in weight_copies(eo_ref[k], k % 2):
                cp.start(priority=1)

    def x_copy(b, slot):
        return pltpu.make_async_copy(xs_hbm.at[_row_ds(b * BM, BM), :], xbuf.at[slot], xsem.at[slot])

    def y_copy(b, slot):
        return pltpu.make_async_copy(ybuf.at[slot], y_hbm.at[_row_ds(b * BM, BM), :], ysem.at[slot])

    fetch(0)
    fetch(1)
    for ahead in range(X_BUFFERS - 1):
        @pl.when(ahead < n_valid)
        def _(ahead=ahead):
            x_copy(ahead, ahead).start()

    def next_slot(s):
        return jnp.where(s == X_BUFFERS - 1, 0, s + 1)

    def block(b, carry):
        k, xslot = carry
        slot = b % 2
        e = be_ref[b]
        new_expert = jnp.logical_or(b == 0, e != be_ref[jnp.maximum(b - 1, 0)])

        @pl.when(b + X_BUFFERS - 1 < n_valid)
        def _():
            ahead_slot = xslot
            for _ in range(X_BUFFERS - 1):
                ahead_slot = next_slot(ahead_slot)
            x_copy(b + X_BUFFERS - 1, ahead_slot).start()

        @pl.when(new_expert)
        def _():
            stage = k % 2
            for cp in weight_copies(e, stage):
                cp.wait()
            for st in range(2):
                @pl.when(stage == st)
                def _(st=st):
                    wgu_bf[...] = gu_stage[st].astype(BF16)
                    wdn_bf[...] = dn_stage[st].astype(BF16)
            fetch(k + 2)

        x_copy(b, xslot).wait()

        @pl.when(b >= 2)
        def _():
            y_copy(b - 2, slot).wait()

        x = _slab_columns(xbuf.at[xslot], 0, BM).astype(BF16)
        bgu = bgu_ref[e]
        g = jnp.dot(x, wgu_bf[:, :D_EXPERT], preferred_element_type=F32) + bgu[:, :D_EXPERT]
        u = jnp.dot(x, wgu_bf[:, D_EXPERT:], preferred_element_type=F32) + bgu[:, D_EXPERT:]
        g = jnp.minimum(g, SWIGLU_LIMIT)
        u = jnp.clip(u, -SWIGLU_LIMIT, SWIGLU_LIMIT)
        act = ((u + 1.0) * (0.5 * g * (1.0 + jnp.tanh((0.5 * SWIGLU_ALPHA) * g)))).astype(BF16)
        y = jnp.dot(act, wdn_bf[...], preferred_element_type=F32) + bdn_ref[e]
        out = ybuf.at[slot]
        for s in range(ROW_SLABS):
            out[pl.ds(s, BM, stride=ROW_SLABS), :] = y[:, s * LANES:(s + 1) * LANES]
        y_copy(b, slot).start(priority=1)
        return k + new_expert.astype(I32), next_slot(xslot)

    lax.fori_loop(0, n_valid, block, (jnp.int32(0), jnp.int32(0)))

    @pl.when(n_valid >= 2)
    def _():
        y_copy(n_valid - 2, n_valid % 2).wait()
    y_copy(n_valid - 1, (n_valid - 1) % 2).wait()

    ybuf[0] = jnp.zeros((BM * ROW_SLABS, LANES), F32)

    def zero_blocks(wait):
        def one(b, carry):
            cp = y_copy(b, 0)
            cp.wait() if wait else cp.start()
            return carry
        lax.fori_loop(n_valid, NB, one, 0)

    zero_blocks(False)
    zero_blocks(True)


def _moe_blocks(tables, xs, w_gu, b_gu, w_down, b_down, li):
    grid_spec = pltpu.PrefetchScalarGridSpec(
        num_scalar_prefetch=3,
        grid=(1,),
        in_specs=[
            pl.BlockSpec(memory_space=pl.ANY),
            pl.BlockSpec(memory_space=pl.ANY),
            pl.BlockSpec((None, N_EXPERTS, 1, 2 * D_EXPERT), lambda i, *_: (li, 0, 0, 0)),
            pl.BlockSpec(memory_space=pl.ANY),
            pl.BlockSpec((None, N_EXPERTS, 1, D_MODEL), lambda i, *_: (li, 0, 0, 0)),
        ],
        out_specs=pl.BlockSpec(memory_space=pl.ANY),
        scratch_shapes=[
            pltpu.VMEM((X_BUFFERS, BM * ROW_SLABS, LANES), F32),
            pltpu.VMEM((2, BM * ROW_SLABS, LANES), F32),
            pltpu.VMEM((2, D_MODEL, 2 * D_EXPERT), F32),
            pltpu.VMEM((2, D_EXPERT, D_MODEL), F32),
            pltpu.VMEM((D_MODEL, 2 * D_EXPERT), BF16),
            pltpu.VMEM((D_EXPERT, D_MODEL), BF16),
            pltpu.SemaphoreType.DMA((2,)),
            pltpu.SemaphoreType.DMA((X_BUFFERS,)),
            pltpu.SemaphoreType.DMA((2,)),
        ],
    )
    return pl.pallas_call(
        functools.partial(_moe_kernel, li=li),
        grid_spec=grid_spec,
        out_shape=jax.ShapeDtypeStruct((N_SLOTS * ROW_SLABS, LANES), F32),
        compiler_params=_cparams(1),
        name="moe_experts",
    )(tables["block_expert"], tables["expert_order"], tables["n_valid"], xs, w_gu,
      b_gu.reshape(DEPTH, N_EXPERTS, 1, 2 * D_EXPERT), w_down, b_down.reshape(DEPTH, N_EXPERTS, 1, D_MODEL))


def _combine_kernel(start_ref, len_ref, off_ref, y_hbm, x1_ref, ri_ref, rf_ref, mod_ref,
                    lng_ref, lnb_ref, oc_ref, od_ref, *scratch):
    bufs, sem = scratch[:RT_PER_STEP], scratch[RT_PER_STEP]
    i = pl.program_id(0)
    n_steps = pl.num_programs(0)

    def fetch(tile, t):
        _tile_run_copies(tile, start_ref, len_ref, off_ref, y_hbm, bufs[t], sem.at[t], to_sorted=False)

    @pl.when(i == 0)
    def _():
        for t in range(FETCH_AHEAD):
            fetch(t, t)

    m = mod_ref[pl.ds(_mod_row(i * RT_PER_STEP // RT_PER_TM), 1), :]
    g2 = m[:, 5 * D_MODEL:6 * D_MODEL]
    col = lax.broadcasted_iota(I32, (RT, RT_ROWS), 1).astype(F32)
    outs = []
    for t in range(RT_PER_STEP):
        rows = slice(t * RT, (t + 1) * RT)
        fetch(jnp.minimum(i * RT_PER_STEP + t + FETCH_AHEAD, N_RT - 1), (t + FETCH_AHEAD) % RT_PER_STEP)
        _tile_runs_wait(bufs[t], sem.at[t])
        pos = _packed_positions(ri_ref[rows, :])
        gates = rf_ref[rows, :]
        mix = jnp.zeros((RT, RT_ROWS), F32)
        for k in range(TOP_K):
            mix = mix + jnp.where(col == pos[k], gates[:, k:k + 1], 0.0)
        ffn = jnp.dot(mix.astype(BF16), _slab_columns(bufs[t], 0, RT_ROWS).astype(BF16),
                      preferred_element_type=F32)
        outs.append(_layer_norm_rows(DN_ALPHA * x1_ref[rows, :] + g2 * ffn, lng_ref[...], lnb_ref[...]))
    out = jnp.concatenate(outs, axis=0)
    n_ctx_steps = N_CTX // (RT * RT_PER_STEP)

    @pl.when(i < n_ctx_steps)
    def _():
        oc_ref[...] = out

    @pl.when(i >= n_ctx_steps)
    def _():
        od_ref[...] = out

    @pl.when(i == n_steps - 1)
    def _():
        for t in range(FETCH_AHEAD):
            _tile_runs_wait(bufs[t], sem.at[t])


def _combine(tables, y, x1, route_i, route_f, mod, ln_g, ln_b, li):
    rows = RT * RT_PER_STEP
    grid_spec = pltpu.PrefetchScalarGridSpec(
        num_scalar_prefetch=3,
        grid=(N_RT // RT_PER_STEP,),
        in_specs=[
            pl.BlockSpec(memory_space=pl.ANY),
            pl.BlockSpec((rows, D_MODEL), lambda i, *_: (i, 0)),
            pl.BlockSpec((rows, LANES), lambda i, *_: (i, 0)),
            pl.BlockSpec((rows, LANES), lambda i, *_: (i, 0)),
            pl.BlockSpec((None, SUBLANES, 6 * D_MODEL), lambda i, *_: (li, 0, 0)),
            pl.BlockSpec((1, D_MODEL), lambda i, *_: (0, 0)),
            pl.BlockSpec((1, D_MODEL), lambda i, *_: (0, 0)),
        ],
        out_specs=_pair_specs(rows, D_MODEL, N_CTX // rows),
        scratch_shapes=[pltpu.VMEM((RT_ROWS * ROW_SLABS, LANES), F32)] * RT_PER_STEP + [
            pltpu.SemaphoreType.DMA((RT_PER_STEP,)),
        ],
    )
    return pl.pallas_call(
        _combine_kernel,
        grid_spec=grid_spec,
        out_shape=[jax.ShapeDtypeStruct((N_CTX, D_MODEL), F32), jax.ShapeDtypeStruct((N_DEC, D_MODEL), F32)],
        compiler_params=_cparams(1),
        name="moe_combine",
    )(tables["start"], tables["len"], tables["off"], y, x1, route_i, route_f, mod, ln_g, ln_b)


def _routing_tables(tile_cnt):
    cnt = tile_cnt[:, 0, :N_EXPERTS]
    totals = jnp.sum(cnt, axis=0)
    padded = (totals + BM - 1) // BM * BM
    pends = jnp.cumsum(padded)
    pstarts = pends - padded
    start = pstarts[None, :] + jnp.cumsum(cnt, axis=0) - cnt
    off = jnp.cumsum(cnt, axis=1) - cnt
    n_valid = (pends[-1] // BM).astype(I32)
    block_start = jnp.arange(NB, dtype=I32) * BM
    block_e = jnp.minimum(jnp.sum(block_start[:, None] >= pends[None, :], axis=1), N_EXPERTS - 1)
    last_e = block_e[jnp.maximum(n_valid - 1, 0)]
    block_e = jnp.where(jnp.arange(NB) < n_valid, block_e, last_e)
    ids = jnp.arange(N_EXPERTS, dtype=I32)
    order = jnp.sort(jnp.where(totals > 0, ids, N_EXPERTS))
    order = jnp.concatenate([jnp.where(order < N_EXPERTS, order, -1), jnp.full((2,), -1, I32)])
    return {
        "start": start.reshape(-1).astype(I32),
        "len": jnp.concatenate([cnt.reshape(-1), jnp.max(cnt, axis=1) >= LONG_RUN]).astype(I32),
        "off": off.reshape(-1).astype(I32),
        "pad_start": (pstarts + totals).astype(I32),
        "pad_len": (padded - totals).astype(I32),
        "n_valid": n_valid.reshape(1),
        "block_expert": block_e.astype(I32),
        "expert_order": order.astype(I32),
    }


def _layer_params(p, li):
    eye = jnp.eye(N_LRU_BLOCKS, dtype=F32)
    w_rg_full = jnp.einsum("dkgio,gh->gidkho", p["w_rg"][li], eye).reshape(LRU_WIDTH, 4 * LRU_WIDTH)
    pad = LANES - N_EXPERTS
    return {
        "sgu_ln_g": p["sgu_ln_g"][li].reshape(1, SGU_WIDTH),
        "sgu_ln_b": p["sgu_ln_b"][li].reshape(1, SGU_WIDTH),
        "w_spatial": p["w_spatial"][li],
        "b_spatial_full": jnp.repeat(p["b_spatial"][li].T, SGU_GROUP, axis=1),
        "conv_w": p["conv_w"][li],
        "conv_b": p["conv_b"][li].reshape(1, LRU_WIDTH),
        "w_rg_full": w_rg_full,
        "b_rg_full": p["b_rg"][li].reshape(1, 4 * LRU_WIDTH),
        "lru_log": p["lru_log"][li],
        "w_out_all": p["w_out"],
        "ln_g1": p["ln_g"][li, 0].reshape(1, D_MODEL),
        "ln_b1": p["ln_b"][li, 0].reshape(1, D_MODEL),
        "ln_g2": p["ln_g"][li, 1].reshape(1, D_MODEL),
        "ln_b2": p["ln_b"][li, 1].reshape(1, D_MODEL),
        "w_router_pad": jnp.pad(p["w_router"][li], ((0, 0), (0, pad))),
        "b_router_pad": jnp.pad(p["b_router"][li], (0, pad)).reshape(1, LANES),
    }


def kernel(x_prompt, x_sample, cache_k, cache_v, state_lru, c, c_ctx, w_mod, b_mod, w_in, lam, subln_g, sgu_ln_g, sgu_ln_b, w_spatial, b_spatial, conv_w, conv_b, w_rg, b_rg, lru_log, w_out, ln_g, ln_b, w_router, b_router, w_gu, b_gu, w_down, b_down):
    p = dict(sgu_ln_g=sgu_ln_g, sgu_ln_b=sgu_ln_b, w_spatial=w_spatial, b_spatial=b_spatial, conv_w=conv_w,
             conv_b=conv_b, w_rg=w_rg, b_rg=b_rg, lru_log=lru_log, w_out=w_out, ln_g=ln_g, ln_b=ln_b,
             w_router=w_router, b_router=b_router)
    cvec8 = jnp.concatenate([c_ctx[None, :], c, jnp.zeros((SUBLANES - 1 - DEC_BATCH, D_MODEL), F32)], axis=0)
    mod = _modulation(cvec8, w_mod, b_mod)
    x_pair = (x_prompt.reshape(N_CTX, D_MODEL), x_sample.reshape(N_DEC, D_MODEL))
    tables = _rope_tables()
    zero_state = jnp.zeros((BATCH, 1, 2, LRU_WIDTH), F32)
    prev_kv = None
    new_s = []
    for li in range(DEPTH):
        lp = _layer_params(p, li)
        lam_init = 0.8 - 0.6 * math.exp(-0.3 * li)
        proj = _input_projection(*x_pair, mod, w_in, li)
        att_ctx, kc, vc = _context_attention(proj, lam[li], subln_g[li], lam_init, prev_kv)
        prev_kv = (kc, vc)
        att_dec = _denoise_attention(proj, cache_k, cache_v, lam[li], subln_g[li], li, lam_init, tables)
        sgu_ctx, lru_ctx, h_ctx = _mixers(proj, zero_state, 0, lp, SEQ, BATCH, 0)
        sgu_dec, lru_dec, _ = _mixers(proj, state_lru, li, lp, DEC_SEQ, DEC_BATCH, N_CTX // DEC_SEQ)
        x1, h2, route_i, route_f, tile_cnt = _output_projection(
            x_pair, (att_ctx, att_dec), (sgu_ctx, sgu_dec), (lru_ctx, lru_dec), mod, lp, li)
        rt = _routing_tables(tile_cnt)
        xs = _dispatch(rt, h2, route_i)
        y = _moe_blocks(rt, xs, w_gu, b_gu, w_down, b_down, li)
        x_pair = _combine(rt, y, x1, route_i, route_f, mod, lp["ln_g2"], lp["ln_b2"], li)
        new_s.append(h_ctx)
    y_prompt = x_pair[0].reshape(BATCH, SEQ, D_MODEL)
    y_sample = x_pair[1].reshape(DEC_BATCH, DEC_SEQ, D_MODEL)
    return (y_prompt, y_sample, prev_kv[0], prev_kv[1], jnp.stack(new_s, axis=1))
```

```python
import functools
import math

import numpy as np
import jax
import jax.numpy as jnp
from jax import lax
from jax.experimental import pallas as pl
from jax.experimental.pallas import tpu as pltpu

F32 = jnp.float32
BF16 = jnp.bfloat16
I32 = jnp.int32

D_MODEL = 1024
BATCH = 32
SEQ = 256
DEPTH = 2
DEC_BATCH = 2
DEC_SEQ = 1024
PAST_LEN = 256
GRID_W = 64
HEAD_DIM = 64
ATT_WIDTH = D_MODEL // 2
SGU_WIDTH = D_MODEL // 4
LRU_WIDTH = D_MODEL // 4
N_ATT_HEADS = ATT_WIDTH // HEAD_DIM
ATT_HALF = HEAD_DIM // 2
ROPE_FREQS = ATT_HALF // 4
ROPE_THETA = 10000.0
CHUNK = 128
N_SGU_GROUPS = 4
SGU_GROUP = SGU_WIDTH // N_SGU_GROUPS
N_LRU_BLOCKS = 4
LRU_BLOCK = LRU_WIDTH // N_LRU_BLOCKS
CONV_W = 4
LRU_C = 8.0
IN_COLS = 3 * ATT_WIDTH + 2 * SGU_WIDTH + 2 * LRU_WIDTH
N_EXPERTS = 32
TOP_K = 4
D_EXPERT = D_MODEL
SWIGLU_LIMIT = 7.0
SWIGLU_ALPHA = 1.702
DN_ALPHA = (2 * DEPTH) ** 0.25
EPS = 1e-5

N_CTX = BATCH * SEQ
N_DEC = DEC_BATCH * DEC_SEQ
N_TOK = N_CTX + N_DEC

LANES = 128
SUBLANES = 8
ROW_SLABS = D_MODEL // LANES

TM = 512
CTX_TILES = N_CTX // TM
TILES_PER_DEC = DEC_SEQ // TM
N_TILES = N_TOK // TM
MOD_TN = 1024
RT = 256
N_RT = N_TOK // RT
RT_PER_TM = TM // RT
RT_ROWS = RT * TOP_K
RT_PER_STEP = 4
FETCH_AHEAD = 2
BM = 256
NB = N_TOK * TOP_K // BM + N_EXPERTS
WEIGHT_PIECES = 8
X_BUFFERS = 3
N_SLOTS = NB * BM
RUN_BITS = tuple(1 << b for b in range(RT.bit_length() - 1, -1, -1))
PAD_BITS = tuple(1 << b for b in range(BM.bit_length() - 2, -1, -1))
VMEM_LIMIT = 56 * 1024 * 1024


def _cparams(n_axes):
    return pltpu.CompilerParams(
        dimension_semantics=("arbitrary",) * n_axes,
        vmem_limit_bytes=VMEM_LIMIT)


def _mod_row(i):
    return jnp.where(i < CTX_TILES, 0, 1 + (i - CTX_TILES) // TILES_PER_DEC)


def _layer_norm_rows(z, g, b):
    mu = jnp.mean(z, axis=-1, keepdims=True)
    zc = z - mu
    var = jnp.mean(zc * zc, axis=-1, keepdims=True)
    return zc * lax.rsqrt(var + EPS) * g + b


def _pair_specs(tile, width, n_ctx_tiles):
    ctx = pl.BlockSpec((tile, width), lambda i, *_: (jnp.minimum(i, n_ctx_tiles - 1), 0))
    dec = pl.BlockSpec((tile, width), lambda i, *_: (jnp.maximum(i - n_ctx_tiles, 0), 0))
    return [ctx, dec]


def _pair_read(i, n_ctx_tiles, ctx_ref, dec_ref):
    return jnp.where(i < n_ctx_tiles, ctx_ref[...], dec_ref[...])


def _mod_kernel(cvec_ref, w_ref, b_ref, o_ref):
    cv = cvec_ref[...]
    s = cv * jax.nn.sigmoid(cv)
    s_t = s.T
    w = w_ref[0]
    rows = [jnp.sum(s_t[:, r:r + 1] * w, axis=0, keepdims=True) for r in range(1 + DEC_BATCH)]
    rows.append(jnp.zeros((SUBLANES - 1 - DEC_BATCH, MOD_TN), F32))
    o_ref[0] = jnp.concatenate(rows, axis=0) + b_ref[0]


def _modulation(cvec8, w_mod, b_mod):
    n_out = w_mod.shape[-1]
    return pl.pallas_call(
        _mod_kernel,
        grid=(DEPTH, n_out // MOD_TN),
        in_specs=[
            pl.BlockSpec((SUBLANES, D_MODEL), lambda l, j: (0, 0)),
            pl.BlockSpec((1, D_MODEL, MOD_TN), lambda l, j: (l, 0, j)),
            pl.BlockSpec((1, 1, MOD_TN), lambda l, j: (l, 0, j)),
        ],
        out_specs=pl.BlockSpec((1, SUBLANES, MOD_TN), lambda l, j: (l, 0, j)),
        out_shape=jax.ShapeDtypeStruct((DEPTH, SUBLANES, n_out), F32),
        compiler_params=_cparams(2),
        name="modulation",
    )(cvec8, w_mod, b_mod.reshape(DEPTH, 1, n_out))


def _inproj_kernel(xc_ref, xd_ref, mod_ref, w_ref, o_ref, wbf_ref):
    i = pl.program_id(0)

    @pl.when(i == 0)
    def _():
        wbf_ref[...] = w_ref[...].astype(BF16)

    m = mod_ref[pl.ds(_mod_row(i), 1), :]
    sh1 = m[:, 0:D_MODEL]
    sc1 = m[:, D_MODEL:2 * D_MODEL]
    x = _pair_read(i, CTX_TILES, xc_ref, xd_ref)
    h = (x * (1.0 + sc1) + sh1).astype(BF16)
    o_ref[...] = jnp.dot(h, wbf_ref[...], preferred_element_type=F32)


def _input_projection(x_ctx, x_dec, mod, w_in, li):
    return pl.pallas_call(
        _inproj_kernel,
        grid=(N_TILES,),
        in_specs=_pair_specs(TM, D_MODEL, CTX_TILES) + [
            pl.BlockSpec((None, SUBLANES, 6 * D_MODEL), lambda i: (li, 0, 0)),
            pl.BlockSpec((None, D_MODEL, IN_COLS), lambda i: (li, 0, 0), pipeline_mode=pl.Buffered(1)),
        ],
        out_specs=pl.BlockSpec((TM, IN_COLS), lambda i: (i, 0)),
        out_shape=jax.ShapeDtypeStruct((N_TOK, IN_COLS), F32),
        scratch_shapes=[pltpu.VMEM((D_MODEL, IN_COLS), BF16)],
        compiler_params=_cparams(1),
        name="input_projection",
    )(x_ctx, x_dec, mod, w_in)


def _lambda(lam_ref, lam_init):
    lm = lam_ref[...]
    a = jnp.sum(lm[0:1] * lm[1:2], axis=-1, keepdims=True)
    b = jnp.sum(lm[2:3] * lm[3:4], axis=-1, keepdims=True)
    return jnp.exp(a) - jnp.exp(b) + lam_init


_NT = (((1,), (1,)), ((), ()))
_TN = (((0,), (0,)), ((), ()))
_ATT_SCALE = ATT_HALF ** -0.5
_Q_SCALE = _ATT_SCALE * math.log2(math.e)
KEY_CHUNK = 128


def _attention_heads(q_parts, k_segs, values, s_scr, e_scr, lam, g_col, lam_init):
    n_heads = len(q_parts)
    for h in range(n_heads):
        for c in range(2):
            row = 0
            for k in k_segs[h][c]:
                s_scr[2 * h + c, row:row + k.shape[0], :] = lax.dot_general(
                    k, q_parts[h][c], _NT, preferred_element_type=F32)
                row += k.shape[0]
    n_keys, n_q = s_scr.shape[1:]
    chunks = [slice(r, r + KEY_CHUNK) for r in range(0, n_keys, KEY_CHUNK)]
    for n in range(2 * n_heads):
        tops = [jnp.max(s_scr[n, rows, :].reshape(KEY_CHUNK // SUBLANES, SUBLANES, n_q), axis=0)
                for rows in chunks]
        top = jnp.max(functools.reduce(jnp.maximum, tops), axis=0, keepdims=True)
        for rows in chunks:
            e_scr[n, rows, :] = jnp.exp2(s_scr[n, rows, :] - top).astype(BF16)
    outs = []
    for h in range(n_heads):
        v_aug = jnp.concatenate([values[h], jnp.ones_like(values[h])], axis=-1)
        normed = []
        for c in range(2):
            acc = lax.dot_general(v_aug, e_scr[2 * h + c], _TN, preferred_element_type=F32)
            normed.append(acc[:HEAD_DIM] * (1.0 / acc[HEAD_DIM:HEAD_DIM + 1]))
        o_t = normed[0] - lam * normed[1]
        ms = jnp.mean(o_t * o_t, axis=0, keepdims=True)
        outs.append(o_t * lax.rsqrt(ms + EPS) * g_col * (1.0 - lam_init))
    return outs


def _head_cols(h):
    lo = h * HEAD_DIM
    return [slice(lo + c * ATT_HALF, lo + (c + 1) * ATT_HALF) for c in range(2)]


def _ctx_attn_kernel(lam_ref, g_ref, q_ref, k_ref, v_ref, *rest, lam_init, stacked):
    *rest, s_scr, e_scr = rest
    if stacked:
        pk_ref, pv_ref, att_ref, ck_ref, cv_ref = rest
        ck_ref[0, 0] = pk_ref[0]
        cv_ref[0, 0] = pv_ref[0]
        put_k = lambda h, val: ck_ref.__setitem__((0, 1, h), val)
        put_v = lambda h, val: cv_ref.__setitem__((0, 1, h), val)
    else:
        att_ref, ck_ref, cv_ref = rest
        put_k = lambda h, val: ck_ref.__setitem__((0, h), val)
        put_v = lambda h, val: cv_ref.__setitem__((0, h), val)
    lam = _lambda(lam_ref, lam_init)
    k = k_ref[...]
    v = v_ref[...]
    qb = (q_ref[...] * _Q_SCALE).astype(BF16)
    kb = k.astype(BF16)
    vb = v.astype(BF16)
    heads = range(N_ATT_HEADS)
    outs = _attention_heads([[qb[:, cs] for cs in _head_cols(h)] for h in heads],
                            [[[kb[:, cs]] for cs in _head_cols(h)] for h in heads],
                            [vb[:, h * HEAD_DIM:(h + 1) * HEAD_DIM] for h in heads],
                            s_scr, e_scr, lam, g_ref[...], lam_init)
    for h in heads:
        put_k(h, k[:, h * HEAD_DIM:(h + 1) * HEAD_DIM])
        put_v(h, v[:, h * HEAD_DIM:(h + 1) * HEAD_DIM])
    att_ref[...] = jnp.concatenate(outs, axis=0).T


def _context_attention(proj, lam, subln_g, lam_init, prev_kv=None):
    stacked = prev_kv is not None
    per_layer = pl.BlockSpec((1, N_ATT_HEADS, SEQ, HEAD_DIM), lambda b: (b, 0, 0, 0))
    if stacked:
        kv_shape = jax.ShapeDtypeStruct((BATCH, DEPTH, N_ATT_HEADS, SEQ, HEAD_DIM), F32)
        kv_spec = pl.BlockSpec((1, DEPTH, N_ATT_HEADS, SEQ, HEAD_DIM), lambda b: (b, 0, 0, 0, 0))
    else:
        kv_shape = jax.ShapeDtypeStruct((BATCH, N_ATT_HEADS, SEQ, HEAD_DIM), F32)
        kv_spec = per_layer
    return pl.pallas_call(
        functools.partial(_ctx_attn_kernel, lam_init=lam_init, stacked=stacked),
        grid=(BATCH,),
        in_specs=[
            pl.BlockSpec((4, ATT_HALF), lambda b: (0, 0)),
            pl.BlockSpec((HEAD_DIM, 1), lambda b: (0, 0)),
            pl.BlockSpec((SEQ, ATT_WIDTH), lambda b: (b, 0)),
            pl.BlockSpec((SEQ, ATT_WIDTH), lambda b: (b, 1)),
            pl.BlockSpec((SEQ, ATT_WIDTH), lambda b: (b, 2)),
        ] + ([per_layer, per_layer] if stacked else []),
        out_specs=[pl.BlockSpec((SEQ, ATT_WIDTH), lambda b: (b, 0)), kv_spec, kv_spec],
        out_shape=[jax.ShapeDtypeStruct((N_CTX, ATT_WIDTH), F32), kv_shape, kv_shape],
        scratch_shapes=[pltpu.VMEM((2 * N_ATT_HEADS, SEQ, SEQ), F32),
                        pltpu.VMEM((2 * N_ATT_HEADS, SEQ, SEQ), BF16)],
        compiler_params=_cparams(1),
        name="context_attention",
    )(lam, subln_g.reshape(HEAD_DIM, 1), proj, proj, proj, *(prev_kv if stacked else ()))


def _rope_tables():
    t = np.arange(DEC_SEQ)
    pos = np.stack([t // GRID_W, t % GRID_W], axis=1).astype(np.float32)
    inv = (np.float32(ROPE_THETA) ** (-np.arange(ROPE_FREQS, dtype=np.float32) / np.float32(ROPE_FREQS)))
    j = np.arange(HEAD_DIM)
    d = j % ATT_HALF
    axis = d // (2 * ROPE_FREQS)
    u = d % (2 * ROPE_FREQS)
    ang = pos[:, axis] * inv[u % ROPE_FREQS][None, :].astype(np.float32)
    cos = np.cos(ang).astype(np.float32)
    sin = np.sin(ang).astype(np.float32)
    first = (u < ROPE_FREQS)[None, :]
    s_next = np.where(first, -sin, 0.0).astype(np.float32)
    s_prev = np.where(first, 0.0, sin).astype(np.float32)
    tile = lambda a: jnp.asarray(np.tile(a, (1, N_ATT_HEADS)))
    return tile(cos), tile(s_next), tile(s_prev)


def _rotate(x, cos, s_next, s_prev):
    width = x.shape[-1]
    return (x * cos + pltpu.roll(x, width - ROPE_FREQS, axis=1) * s_next
            + pltpu.roll(x, ROPE_FREQS, axis=1) * s_prev)


QB = 256
Q_STEPS = DEC_SEQ // QB
DEC_HEAD_GROUP = 2


def _dec_attn_kernel(lam_ref, g_ref, q_ref, k_ref, v_ref, ck_ref, cv_ref,
                     cq_ref, snq_ref, spq_ref, ck_tab, snk_tab, spk_tab,
                     att_ref, krot_ref, s_scr, e_scr, *, lam_init):
    j = pl.program_id(1)

    @pl.when(j == 0)
    def _():
        krot_ref[...] = _rotate(k_ref[...], ck_tab[...], snk_tab[...], spk_tab[...]).astype(BF16)

    lam = _lambda(lam_ref, lam_init)
    qb = (_rotate(q_ref[...], cq_ref[...], snq_ref[...], spq_ref[...]) * _Q_SCALE).astype(BF16)
    kb = krot_ref[...]
    vb = v_ref[...].astype(BF16)
    outs = []
    for h0 in range(0, N_ATT_HEADS, DEC_HEAD_GROUP):
        heads = range(h0, h0 + DEC_HEAD_GROUP)
        past_k = [ck_ref[0, 0, h].astype(BF16) for h in heads]
        k_segs = [[[pk[:, c * ATT_HALF:(c + 1) * ATT_HALF], kb[:, _head_cols(h)[c]]] for c in range(2)]
                  for h, pk in zip(heads, past_k)]
        values = [jnp.concatenate([cv_ref[0, 0, h].astype(BF16), vb[:, h * HEAD_DIM:(h + 1) * HEAD_DIM]], axis=0)
                  for h in heads]
        outs += _attention_heads([[qb[:, cs] for cs in _head_cols(h)] for h in heads], k_segs, values,
                                 s_scr, e_scr, lam, g_ref[...], lam_init)
    att_ref[...] = jnp.concatenate(outs, axis=0).T


def _denoise_attention(proj, cache_k, cache_v, lam, subln_g, li, lam_init, tables):
    cos, s_next, s_prev = tables
    row0 = N_CTX // QB
    seq0 = N_CTX // DEC_SEQ
    q_tab = pl.BlockSpec((QB, ATT_WIDTH), lambda b, j: (j, 0))
    k_tab = pl.BlockSpec((DEC_SEQ, ATT_WIDTH), lambda b, j: (0, 0))
    cache_spec = pl.BlockSpec((1, 1, N_ATT_HEADS, PAST_LEN, HEAD_DIM), lambda b, j: (b, li, 0, 0, 0))
    return pl.pallas_call(
        functools.partial(_dec_attn_kernel, lam_init=lam_init),
        grid=(DEC_BATCH, Q_STEPS),
        in_specs=[
            pl.BlockSpec((4, ATT_HALF), lambda b, j: (0, 0)),
            pl.BlockSpec((HEAD_DIM, 1), lambda b, j: (0, 0)),
            pl.BlockSpec((QB, ATT_WIDTH), lambda b, j: (row0 + b * Q_STEPS + j, 0)),
            pl.BlockSpec((DEC_SEQ, ATT_WIDTH), lambda b, j: (seq0 + b, 1)),
            pl.BlockSpec((DEC_SEQ, ATT_WIDTH), lambda b, j: (seq0 + b, 2)),
            cache_spec, cache_spec,
            q_tab, q_tab, q_tab, k_tab, k_tab, k_tab,
        ],
        out_specs=pl.BlockSpec((QB, ATT_WIDTH), lambda b, j: (b * Q_STEPS + j, 0)),
        out_shape=jax.ShapeDtypeStruct((N_DEC, ATT_WIDTH), F32),
        scratch_shapes=[pltpu.VMEM((DEC_SEQ, ATT_WIDTH), BF16),
                        pltpu.VMEM((2 * DEC_HEAD_GROUP, PAST_LEN + DEC_SEQ, QB), F32),
                        pltpu.VMEM((2 * DEC_HEAD_GROUP, PAST_LEN + DEC_SEQ, QB), BF16)],
        compiler_params=_cparams(2),
        name="denoise_attention",
    )(lam, subln_g.reshape(HEAD_DIM, 1), proj, proj, proj, cache_k, cache_v,
      cos, s_next, s_prev, cos, s_next, s_prev)


def _softplus(z):
    return jnp.maximum(z, 0.0) + jnp.log1p(jnp.exp(-jnp.abs(z)))


def _mixer_kernel(su_ref, sv_ref, rx_ref, rg_ref, h0_ref, lng_ref, lnb_ref, ws_ref, bs_ref,
                  cw_ref, cb_ref, wrg_ref, brg_ref, lrulog_ref,
                  sgu_ref, lru_ref, hlast_ref, a_scr, b_scr, h_scr, *, seq_len):
    vn = _layer_norm_rows(sv_ref[...], lng_ref[...], lnb_ref[...])
    lane_group = lax.broadcasted_iota(I32, (CHUNK, SGU_WIDTH), 1) // SGU_GROUP
    for n in range(seq_len // CHUNK):
        rows = slice(n * CHUNK, (n + 1) * CHUNK)
        vc = vn[rows].astype(BF16)
        s = jnp.zeros((CHUNK, SGU_WIDTH), F32)
        for g in range(N_SGU_GROUPS):
            sg = jnp.dot(ws_ref[g].astype(BF16), vc, preferred_element_type=F32)
            s = jnp.where(lane_group == g, sg, s)
        sgu_ref[rows, :] = su_ref[rows, :] * (s + bs_ref[...])

    x = rx_ref[...]
    row = lax.broadcasted_iota(I32, (seq_len, LRU_WIDTH), 0)

    def shifted(val, d, fill):
        rolled = pltpu.roll(val, d % seq_len, axis=0)
        inside = (row >= d) if d > 0 else (row < seq_len + d)
        return jnp.where(inside, rolled, fill)

    left = CONV_W // 2
    xc = cb_ref[...] + x * cw_ref[left:left + 1, :]
    for tap in range(CONV_W):
        if tap != left:
            xc = xc + shifted(x, left - tap, 0.0) * cw_ref[tap:tap + 1, :]
    pre = jnp.dot(xc.astype(BF16), wrg_ref[...].astype(BF16), preferred_element_type=F32) + brg_ref[...]
    gates = 0.5 + 0.5 * jnp.tanh(0.5 * pre)
    in_chunk = row % SUBLANES
    n_chunks = seq_len // SUBLANES
    h0 = h0_ref[0, 0]
    lasts = []
    for direction in range(2):
        reverse = direction == 1
        base = direction * 2 * LRU_WIDTH
        r = gates[:, base:base + LRU_WIDTH]
        gi = gates[:, base + LRU_WIDTH:base + 2 * LRU_WIDTH]
        log_a = -LRU_C * r * _softplus(-lrulog_ref[direction:direction + 1, :])
        a = jnp.exp(log_a)
        b = jnp.sqrt(-jnp.tanh(log_a) * (a * a + 1.0)) * gi * xc
        def chunk_roll(val, shift):
            chunks = val.reshape(n_chunks, SUBLANES, LRU_WIDTH)
            return pltpu.roll(chunks, shift, axis=1).reshape(seq_len, LRU_WIDTH)

        for d in (1, 2, 4):
            if reverse:
                inside = in_chunk < SUBLANES - d
                a_n = jnp.where(inside, chunk_roll(a, SUBLANES - d), 1.0)
                b_n = jnp.where(inside, chunk_roll(b, SUBLANES - d), 0.0)
            else:
                inside = in_chunk >= d
                a_n = jnp.where(inside, chunk_roll(a, d), 1.0)
                b_n = jnp.where(inside, chunk_roll(b, d), 0.0)
            b = a * b_n + b
            a = a * a_n
        a_scr[...] = a
        b_scr[...] = b

        def chunk_step(c, carry, reverse=reverse):
            cc = n_chunks - 1 - c if reverse else c
            off = pl.multiple_of(cc * SUBLANES, SUBLANES)
            hc = a_scr[pl.ds(off, SUBLANES), :] * carry + b_scr[pl.ds(off, SUBLANES), :]
            if reverse:
                h_scr[pl.ds(off, SUBLANES), :] = h_scr[pl.ds(off, SUBLANES), :] + hc
                return hc[0:1, :]
            h_scr[pl.ds(off, SUBLANES), :] = hc
            return hc[SUBLANES - 1:SUBLANES, :]

        lasts.append(lax.fori_loop(0, n_chunks, chunk_step, h0[direction:direction + 1, :]))
    lru_ref[...] = h_scr[...] * jax.nn.gelu(rg_ref[...])
    hlast_ref[0] = jnp.concatenate(lasts, axis=0)


def _mixers(proj, h0, h0_layer, lp, seq_len, n_seq, row_block0):
    col0 = 3 * ATT_WIDTH // SGU_WIDTH
    col = lambda c: pl.BlockSpec((seq_len, SGU_WIDTH), lambda b: (row_block0 + b, col0 + c))
    full = lambda shape: pl.BlockSpec(shape, lambda b: (0,) * len(shape))
    out_rows = pl.BlockSpec((seq_len, SGU_WIDTH), lambda b: (b, 0))
    return pl.pallas_call(
        functools.partial(_mixer_kernel, seq_len=seq_len),
        grid=(n_seq,),
        in_specs=[
            col(0), col(1), col(2), col(3),
            pl.BlockSpec((1, 1, 2, LRU_WIDTH), lambda b: (b, h0_layer, 0, 0)),
            full((1, SGU_WIDTH)), full((1, SGU_WIDTH)),
            full((N_SGU_GROUPS, CHUNK, CHUNK)), full((CHUNK, SGU_WIDTH)),
            full((CONV_W, LRU_WIDTH)), full((1, LRU_WIDTH)),
            full((LRU_WIDTH, 4 * LRU_WIDTH)), full((1, 4 * LRU_WIDTH)),
            full((2, LRU_WIDTH)),
        ],
        out_specs=[out_rows, out_rows, pl.BlockSpec((1, 2, LRU_WIDTH), lambda b: (b, 0, 0))],
        out_shape=[
            jax.ShapeDtypeStruct((n_seq * seq_len, SGU_WIDTH), F32),
            jax.ShapeDtypeStruct((n_seq * seq_len, LRU_WIDTH), F32),
            jax.ShapeDtypeStruct((n_seq, 2, LRU_WIDTH), F32),
        ],
        scratch_shapes=[pltpu.VMEM((seq_len, LRU_WIDTH), F32)] * 3,
        compiler_params=_cparams(1),
        name="mixers_%d" % seq_len,
    )(proj, proj, proj, proj, h0, lp["sgu_ln_g"], lp["sgu_ln_b"], lp["w_spatial"], lp["b_spatial_full"],
      lp["conv_w"], lp["conv_b"], lp["w_rg_full"], lp["b_rg_full"], lp["lru_log"])


def _outproj_kernel(xc_ref, xd_ref, ac_ref, ad_ref, sc_ref, sd_ref, lc_ref, ld_ref,
                    mod_ref, wout_ref, lng_ref, lnb_ref, wr_ref, br_ref,
                    x1_ref, h2_ref, ri_ref, rf_ref, cnt_ref, wbf_ref):
    i = pl.program_id(0)

    @pl.when(i == 0)
    def _():
        wbf_ref[...] = wout_ref[...].astype(BF16)

    m = mod_ref[pl.ds(_mod_row(i), 1), :]
    g1 = m[:, 2 * D_MODEL:3 * D_MODEL]
    sh2 = m[:, 3 * D_MODEL:4 * D_MODEL]
    sc2 = m[:, 4 * D_MODEL:5 * D_MODEL]
    a0, a1 = ATT_WIDTH, ATT_WIDTH + SGU_WIDTH
    x = _pair_read(i, CTX_TILES, xc_ref, xd_ref)
    att = _pair_read(i, CTX_TILES, ac_ref, ad_ref).astype(BF16)
    sgu = _pair_read(i, CTX_TILES, sc_ref, sd_ref).astype(BF16)
    lru = _pair_read(i, CTX_TILES, lc_ref, ld_ref).astype(BF16)
    mix = (jnp.dot(att, wbf_ref[0:a0, :], preferred_element_type=F32)
           + jnp.dot(sgu, wbf_ref[a0:a1, :], preferred_element_type=F32)
           + jnp.dot(lru, wbf_ref[a1:, :], preferred_element_type=F32))
    x1 = _layer_norm_rows(DN_ALPHA * x + g1 * mix, lng_ref[...], lnb_ref[...])
    x1_ref[...] = x1
    h2 = x1 * (1.0 + sc2) + sh2
    h2_ref[...] = h2.astype(BF16)

    logits = jnp.dot(h2.astype(BF16), wr_ref[...].astype(BF16), preferred_element_type=F32) + br_ref[...]
    lane = lax.broadcasted_iota(I32, (TM, LANES), 1)
    lane_f = lane.astype(F32)
    neg_inf = jnp.float32(-jnp.inf)
    work = jnp.where(lane < N_EXPERTS, logits, neg_inf)
    vals, idxs = [], []
    for _ in range(TOP_K):
        top = jnp.max(work, axis=-1, keepdims=True)
        idx = jnp.min(jnp.where(work == top, lane_f, float(LANES)), axis=-1, keepdims=True)
        vals.append(top)
        idxs.append(idx)
        work = jnp.where(lane_f == idx, neg_inf, work)
    exps = [jnp.exp(v - vals[0]) for v in vals]
    denom = exps[0] + exps[1] + exps[2] + exps[3]
    onehot = jnp.zeros((TM, LANES), F32)
    for idx in idxs:
        onehot = onehot + (lane_f == idx).astype(F32)
    r_i = lax.broadcasted_iota(I32, (RT, RT), 0)
    c_i = lax.broadcasted_iota(I32, (RT, RT), 1)
    tri = (r_i > c_i).astype(F32).astype(BF16)
    upper = (lax.broadcasted_iota(I32, (LANES, LANES), 0)
             < lax.broadcasted_iota(I32, (LANES, LANES), 1)).astype(F32).astype(BF16)
    packed_pos = []
    for t in range(RT_PER_TM):
        hot = onehot[t * RT:(t + 1) * RT]
        total = jnp.broadcast_to(jnp.sum(hot, axis=0, keepdims=True), (SUBLANES, LANES))
        cnt_ref[t] = total.astype(I32)
        run_start = jnp.dot(total.astype(BF16), upper, preferred_element_type=F32)[0:1, :]
        packed_pos.append(jnp.dot(tri, hot.astype(BF16), preferred_element_type=F32) + run_start)
    packed_pos = jnp.concatenate(packed_pos, axis=0)
    ri = jnp.zeros((TM, LANES), F32)
    rf = jnp.zeros((TM, LANES), F32)
    for k in range(TOP_K):
        pos = jnp.sum(jnp.where(lane_f == idxs[k], packed_pos, 0.0), axis=-1, keepdims=True)
        ri = jnp.where(lane == k, idxs[k], ri)
        ri = jnp.where(lane == TOP_K + k, pos, ri)
        rf = jnp.where(lane == k, exps[k] / denom, rf)
    ri_ref[...] = ri.astype(I32)
    rf_ref[...] = rf


def _output_projection(x_pair, att_pair, sgu_pair, lru_pair, mod, lp, li):
    rows = lambda w: pl.BlockSpec((TM, w), lambda i: (i, 0))
    full = lambda shape: pl.BlockSpec(shape, lambda i: (0,) * len(shape))
    return pl.pallas_call(
        _outproj_kernel,
        grid=(N_TILES,),
        in_specs=(_pair_specs(TM, D_MODEL, CTX_TILES) + _pair_specs(TM, ATT_WIDTH, CTX_TILES)
                  + _pair_specs(TM, SGU_WIDTH, CTX_TILES) + _pair_specs(TM, LRU_WIDTH, CTX_TILES) + [
            pl.BlockSpec((None, SUBLANES, 6 * D_MODEL), lambda i: (li, 0, 0)),
            pl.BlockSpec((None, D_MODEL, D_MODEL), lambda i: (li, 0, 0), pipeline_mode=pl.Buffered(1)),
            full((1, D_MODEL)), full((1, D_MODEL)),
            full((D_MODEL, LANES)), full((1, LANES)),
        ]),
        out_specs=[
            rows(D_MODEL), rows(D_MODEL), rows(LANES), rows(LANES),
            pl.BlockSpec((RT_PER_TM, SUBLANES, LANES), lambda i: (i, 0, 0)),
        ],
        out_shape=[
            jax.ShapeDtypeStruct((N_TOK, D_MODEL), F32),
            jax.ShapeDtypeStruct((N_TOK, D_MODEL), BF16),
            jax.ShapeDtypeStruct((N_TOK, LANES), I32),
            jax.ShapeDtypeStruct((N_TOK, LANES), F32),
            jax.ShapeDtypeStruct((N_RT, SUBLANES, LANES), I32),
        ],
        scratch_shapes=[pltpu.VMEM((D_MODEL, D_MODEL), BF16)],
        compiler_params=_cparams(1),
        name="output_projection",
    )(*x_pair, *att_pair, *sgu_pair, *lru_pair, mod, lp["w_out_all"], lp["ln_g1"], lp["ln_b1"],
      lp["w_router_pad"], lp["b_router_pad"])


def _row_ds(row, n_rows):
    return pl.ds(pl.multiple_of(row * ROW_SLABS, ROW_SLABS), n_rows * ROW_SLABS)


def _for_each_run_piece(length_s, bits, fn):
    for bit in bits:
        done_s = length_s & (-2 * bit * ROW_SLABS)
        @pl.when((length_s & (bit * ROW_SLABS)) != 0)
        def _(done_s=done_s, bit=bit):
            fn(done_s, bit)


def _stored_ds(row_s, n_rows):
    return pl.ds(pl.multiple_of(row_s, ROW_SLABS), n_rows * ROW_SLABS)


def _tile_run_copies(tile, start_ref, len_ref, off_ref, sorted_hbm, buf, sem, to_sorted, live=None):
    for e in range(N_EXPERTS):
        t = tile * N_EXPERTS + e
        start_s, length_s, off_s = start_ref[t], len_ref[t], off_ref[t]
        if live is not None:
            length_s = jnp.where(live, length_s, 0)

        def piece(done_s, bit, start_s=start_s, off_s=off_s):
            packed = buf.at[_stored_ds(off_s + done_s, bit), :]
            srt = sorted_hbm.at[_stored_ds(start_s + done_s, bit), :]
            if to_sorted:
                pltpu.make_async_copy(packed, srt, sem).start()
            else:
                pltpu.make_async_copy(srt, packed, sem).start()

        _for_each_run_piece(length_s, RUN_BITS, piece)


def _tile_runs_wait(buf, sem):
    pltpu.make_async_copy(buf, buf, sem).wait()


def _packed_positions(ri):
    return [ri[:, TOP_K + k:TOP_K + k + 1].astype(F32) for k in range(TOP_K)]


def _slab_columns(buf, row0, n_rows):
    return jnp.concatenate(
        [buf[pl.ds(row0 * ROW_SLABS + s, n_rows, stride=ROW_SLABS), :] for s in range(ROW_SLABS)], axis=-1)


def _dispatch_kernel(start_ref, len_ref, off_ref, pstart_ref, plen_ref, nv_ref,
                     h2_ref, ri_ref, xs_hbm, *scratch):
    bufs, (zbuf, sem, zsem) = scratch[:RT_PER_STEP], scratch[RT_PER_STEP:]
    i = pl.program_id(0)
    n_steps = pl.num_programs(0)

    @pl.when(i == 0)
    def _():
        zbuf[...] = jnp.zeros_like(zbuf)

        def zero_fill(wait):
            def go(cp):
                cp.wait() if wait else cp.start()

            def per_expert(e, carry):
                def piece(done_s, bit):
                    go(pltpu.make_async_copy(zbuf.at[pl.ds(0, bit * ROW_SLABS), :],
                                             xs_hbm.at[_stored_ds(pstart_ref[e] + done_s, bit), :], zsem))
                _for_each_run_piece(plen_ref[e], PAD_BITS, piece)
                return carry
            lax.fori_loop(0, N_EXPERTS, per_expert, 0)

            def per_block(b, carry):
                go(pltpu.make_async_copy(zbuf, xs_hbm.at[_row_ds(b * BM, BM), :], zsem))
                return carry
            lax.fori_loop(nv_ref[0], NB, per_block, 0)

        zero_fill(False)
        zero_fill(True)

    def send(tile, t, live):
        _tile_run_copies(tile, start_ref, len_ref, off_ref, xs_hbm, bufs[t], sem.at[t], to_sorted=True, live=live)

    col = lax.broadcasted_iota(I32, (RT, RT_ROWS), 1).astype(F32)
    for t in range(RT_PER_STEP):
        buf = bufs[t]
        rows = slice(t * RT, (t + 1) * RT)
        tile = i * RT_PER_STEP + t

        @pl.when(i >= 1)
        def _(buf=buf, t=t):
            _tile_runs_wait(buf, sem.at[t])

        send(jnp.maximum(tile - 1, 0), (t - 1) % RT_PER_STEP, tile >= 1)
        pos = _packed_positions(ri_ref[rows, :])
        sel = jnp.zeros((RT, RT_ROWS), F32)
        for p in pos:
            sel = sel + (col == p).astype(F32)
        packed = lax.dot_general(sel.astype(BF16), h2_ref[rows, :], _TN, preferred_element_type=F32)
        for s in range(ROW_SLABS):
            buf[pl.ds(s, RT_ROWS, stride=ROW_SLABS), :] = packed[:, s * LANES:(s + 1) * LANES]

    @pl.when(i == n_steps - 1)
    def _():
        send(N_RT - 1, RT_PER_STEP - 1, True)
        for t in range(RT_PER_STEP):
            _tile_runs_wait(bufs[t], sem.at[t])


def _dispatch(tables, h2, route_i):
    grid_spec = pltpu.PrefetchScalarGridSpec(
        num_scalar_prefetch=6,
        grid=(N_RT // RT_PER_STEP,),
        in_specs=[
            pl.BlockSpec((RT_PER_STEP * RT, D_MODEL), lambda i, *_: (i, 0)),
            pl.BlockSpec((RT_PER_STEP * RT, LANES), lambda i, *_: (i, 0)),
        ],
        out_specs=pl.BlockSpec(memory_space=pl.ANY),
        scratch_shapes=[pltpu.VMEM((RT_ROWS * ROW_SLABS, LANES), F32)] * RT_PER_STEP + [
            pltpu.VMEM((BM * ROW_SLABS, LANES), F32),
            pltpu.SemaphoreType.DMA((RT_PER_STEP,)),
            pltpu.SemaphoreType.DMA(()),
        ],
    )
    return pl.pallas_call(
        _dispatch_kernel,
        grid_spec=grid_spec,
        out_shape=jax.ShapeDtypeStruct((N_SLOTS * ROW_SLABS, LANES), F32),
        compiler_params=_cparams(1),
        name="moe_dispatch",
    )(tables["start"], tables["len"], tables["off"], tables["pad_start"], tables["pad_len"], tables["n_valid"],
      h2, route_i)


def _moe_kernel(be_ref, eo_ref, nv_ref, xs_hbm, wgu_hbm, bgu_ref, wdn_hbm, bdn_ref, y_hbm,
                xbuf, ybuf, gu_stage, dn_stage, wgu_bf, wdn_bf, wsem, xsem, ysem, *, li):
    n_valid = nv_ref[0]
    rows_per_piece = D_MODEL // WEIGHT_PIECES

    def weight_copies(e, stage):
        cps = []
        for p in range(WEIGHT_PIECES):
            band = pl.ds(p * rows_per_piece, rows_per_piece)
            cps.append(pltpu.make_async_copy(wgu_hbm.at[li, e, band, :], gu_stage.at[stage, band, :], wsem.at[stage]))
            cps.append(pltpu.make_async_copy(wdn_hbm.at[li, e, band, :], dn_stage.at[stage, band, :], wsem.at[stage]))
        return cps

    def fetch(k):
        @pl.when(eo_ref[k] >= 0)
        def _():
            for cp in weight_copies(eo_ref[k], k % 2):
                cp.start(priority=1)

    def x_copy(b, slot):
        return pltpu.make_async_copy(xs_hbm.at[_row_ds(b * BM, BM), :], xbuf.at[slot], xsem.at[slot])

    def y_copy(b, slot):
        return pltpu.make_async_copy(ybuf.at[slot], y_hbm.at[_row_ds(b * BM, BM), :], ysem.at[slot])

    fetch(0)
    fetch(1)
    for ahead in range(X_BUFFERS - 1):
        @pl.when(ahead < n_valid)
        def _(ahead=ahead):
            x_copy(ahead, ahead).start()

    def next_slot(s):
        return jnp.where(s == X_BUFFERS - 1, 0, s + 1)

    def block(b, carry):
        k, xslot = carry
        slot = b % 2
        e = be_ref[b]
        new_expert = jnp.logical_or(b == 0, e != be_ref[jnp.maximum(b - 1, 0)])

        @pl.when(b + X_BUFFERS - 1 < n_valid)
        def _():
            ahead_slot = xslot
            for _ in range(X_BUFFERS - 1):
                ahead_slot = next_slot(ahead_slot)
            x_copy(b + X_BUFFERS - 1, ahead_slot).start()

        @pl.when(new_expert)
        def _():
            stage = k % 2
            for cp in weight_copies(e, stage):
                cp.wait()
            for st in range(2):
                @pl.when(stage == st)
                def _(st=st):
                    wgu_bf[...] = gu_stage[st].astype(BF16)
                    wdn_bf[...] = dn_stage[st].astype(BF16)
            fetch(k + 2)

        x_copy(b, xslot).wait()

        @pl.when(b >= 2)
        def _():
            y_copy(b - 2, slot).wait()

        x = _slab_columns(xbuf.at[xslot], 0, BM).astype(BF16)
        bgu = bgu_ref[e]
        g = jnp.dot(x, wgu_bf[:, :D_EXPERT], preferred_element_type=F32) + bgu[:, :D_EXPERT]
        u = jnp.dot(x, wgu_bf[:, D_EXPERT:], preferred_element_type=F32) + bgu[:, D_EXPERT:]
        g = jnp.minimum(g, SWIGLU_LIMIT)
        u = jnp.clip(u, -SWIGLU_LIMIT, SWIGLU_LIMIT)
        act = ((u + 1.0) * (0.5 * g * (1.0 + jnp.tanh((0.5 * SWIGLU_ALPHA) * g)))).astype(BF16)
        y = jnp.dot(act, wdn_bf[...], preferred_element_type=F32) + bdn_ref[e]
        out = ybuf.at[slot]
        for s in range(ROW_SLABS):
            out[pl.ds(s, BM, stride=ROW_SLABS), :] = y[:, s * LANES:(s + 1) * LANES]
        y_copy(b, slot).start(priority=1)
        return k + new_expert.astype(I32), next_slot(xslot)

    lax.fori_loop(0, n_valid, block, (jnp.int32(0), jnp.int32(0)))

    @pl.when(n_valid >= 2)
    def _():
        y_copy(n_valid - 2, n_valid % 2).wait()
    y_copy(n_valid - 1, (n_valid - 1) % 2).wait()

    ybuf[0] = jnp.zeros((BM * ROW_SLABS, LANES), F32)

    def zero_blocks(wait):
        def one(b, carry):
            cp = y_copy(b, 0)
            cp.wait() if wait else cp.start()
            return carry
        lax.fori_loop(n_valid, NB, one, 0)

    zero_blocks(False)
    zero_blocks(True)


def _moe_blocks(tables, xs, w_gu, b_gu, w_down, b_down, li):
    grid_spec = pltpu.PrefetchScalarGridSpec(
        num_scalar_prefetch=3,
        grid=(1,),
        in_specs=[
            pl.BlockSpec(memory_space=pl.ANY),
            pl.BlockSpec(memory_space=pl.ANY),
            pl.BlockSpec((None, N_EXPERTS, 1, 2 * D_EXPERT), lambda i, *_: (li, 0, 0, 0)),
            pl.BlockSpec(memory_space=pl.ANY),
            pl.BlockSpec((None, N_EXPERTS, 1, D_MODEL), lambda i, *_: (li, 0, 0, 0)),
        ],
        out_specs=pl.BlockSpec(memory_space=pl.ANY),
        scratch_shapes=[
            pltpu.VMEM((X_BUFFERS, BM * ROW_SLABS, LANES), F32),
            pltpu.VMEM((2, BM * ROW_SLABS, LANES), F32),
            pltpu.VMEM((2, D_MODEL, 2 * D_EXPERT), F32),
            pltpu.VMEM((2, D_EXPERT, D_MODEL), F32),
            pltpu.VMEM((D_MODEL, 2 * D_EXPERT), BF16),
            pltpu.VMEM((D_EXPERT, D_MODEL), BF16),
            pltpu.SemaphoreType.DMA((2,)),
            pltpu.SemaphoreType.DMA((X_BUFFERS,)),
            pltpu.SemaphoreType.DMA((2,)),
        ],
    )
    return pl.pallas_call(
        functools.partial(_moe_kernel, li=li),
        grid_spec=grid_spec,
        out_shape=jax.ShapeDtypeStruct((N_SLOTS * ROW_SLABS, LANES), F32),
        compiler_params=_cparams(1),
        name="moe_experts",
    )(tables["block_expert"], tables["expert_order"], tables["n_valid"], xs, w_gu,
      b_gu.reshape(DEPTH, N_EXPERTS, 1, 2 * D_EXPERT), w_down, b_down.reshape(DEPTH, N_EXPERTS, 1, D_MODEL))


def _combine_kernel(start_ref, len_ref, off_ref, y_hbm, x1_ref, ri_ref, rf_ref, mod_ref,
                    lng_ref, lnb_ref, oc_ref, od_ref, *scratch):
    bufs, sem = scratch[:RT_PER_STEP], scratch[RT_PER_STEP]
    i = pl.program_id(0)
    n_steps = pl.num_programs(0)

    def fetch(tile, t):
        _tile_run_copies(tile, start_ref, len_ref, off_ref, y_hbm, bufs[t], sem.at[t], to_sorted=False)

    @pl.when(i == 0)
    def _():
        for t in range(FETCH_AHEAD):
            fetch(t, t)

    m = mod_ref[pl.ds(_mod_row(i * RT_PER_STEP // RT_PER_TM), 1), :]
    g2 = m[:, 5 * D_MODEL:6 * D_MODEL]
    col = lax.broadcasted_iota(I32, (RT, RT_ROWS), 1).astype(F32)
    outs = []
    for t in range(RT_PER_STEP):
        rows = slice(t * RT, (t + 1) * RT)
        fetch(jnp.minimum(i * RT_PER_STEP + t + FETCH_AHEAD, N_RT - 1), (t + FETCH_AHEAD) % RT_PER_STEP)
        _tile_runs_wait(bufs[t], sem.at[t])
        pos = _packed_positions(ri_ref[rows, :])
        gates = rf_ref[rows, :]
        mix = jnp.zeros((RT, RT_ROWS), F32)
        for k in range(TOP_K):
            mix = mix + jnp.where(col == pos[k], gates[:, k:k + 1], 0.0)
        ffn = jnp.dot(mix.astype(BF16), _slab_columns(bufs[t], 0, RT_ROWS).astype(BF16),
                      preferred_element_type=F32)
        outs.append(_layer_norm_rows(DN_ALPHA * x1_ref[rows, :] + g2 * ffn, lng_ref[...], lnb_ref[...]))
    out = jnp.concatenate(outs, axis=0)
    n_ctx_steps = N_CTX // (RT * RT_PER_STEP)

    @pl.when(i < n_ctx_steps)
    def _():
        oc_ref[...] = out

    @pl.when(i >= n_ctx_steps)
    def _():
        od_ref[...] = out

    @pl.when(i == n_steps - 1)
    def _():
        for t in range(FETCH_AHEAD):
            _tile_runs_wait(bufs[t], sem.at[t])


def _combine(tables, y, x1, route_i, route_f, mod, ln_g, ln_b, li):
    rows = RT * RT_PER_STEP
    grid_spec = pltpu.PrefetchScalarGridSpec(
        num_scalar_prefetch=3,
        grid=(N_RT // RT_PER_STEP,),
        in_specs=[
            pl.BlockSpec(memory_space=pl.ANY),
            pl.BlockSpec((rows, D_MODEL), lambda i, *_: (i, 0)),
            pl.BlockSpec((rows, LANES), lambda i, *_: (i, 0)),
            pl.BlockSpec((rows, LANES), lambda i, *_: (i, 0)),
            pl.BlockSpec((None, SUBLANES, 6 * D_MODEL), lambda i, *_: (li, 0, 0)),
            pl.BlockSpec((1, D_MODEL), lambda i, *_: (0, 0)),
            pl.BlockSpec((1, D_MODEL), lambda i, *_: (0, 0)),
        ],
        out_specs=_pair_specs(rows, D_MODEL, N_CTX // rows),
        scratch_shapes=[pltpu.VMEM((RT_ROWS * ROW_SLABS, LANES), F32)] * RT_PER_STEP + [
            pltpu.SemaphoreType.DMA((RT_PER_STEP,)),
        ],
    )
    return pl.pallas_call(
        _combine_kernel,
        grid_spec=grid_spec,
        out_shape=[jax.ShapeDtypeStruct((N_CTX, D_MODEL), F32), jax.ShapeDtypeStruct((N_DEC, D_MODEL), F32)],
        compiler_params=_cparams(1),
        name="moe_combine",
    )(tables["start"], tables["len"], tables["off"], y, x1, route_i, route_f, mod, ln_g, ln_b)


def _routing_tables(tile_cnt):
    cnt = tile_cnt[:, 0, :N_EXPERTS]
    totals = jnp.sum(cnt, axis=0)
    padded = (totals + BM - 1) // BM * BM
    pends = jnp.cumsum(padded)
    pstarts = pends - padded
    start = pstarts[None, :] + jnp.cumsum(cnt, axis=0) - cnt
    off = jnp.cumsum(cnt, axis=1) - cnt
    n_valid = (pends[-1] // BM).astype(I32)
    block_start = jnp.arange(NB, dtype=I32) * BM
    block_e = jnp.minimum(jnp.sum(block_start[:, None] >= pends[None, :], axis=1), N_EXPERTS - 1)
    last_e = block_e[jnp.maximum(n_valid - 1, 0)]
    block_e = jnp.where(jnp.arange(NB) < n_valid, block_e, last_e)
    ids = jnp.arange(N_EXPERTS, dtype=I32)
    order = jnp.sort(jnp.where(totals > 0, ids, N_EXPERTS))
    order = jnp.concatenate([jnp.where(order < N_EXPERTS, order, -1), jnp.full((2,), -1, I32)])
    return {
        "start": (start.reshape(-1) * ROW_SLABS).astype(I32),
        "len": (cnt.reshape(-1) * ROW_SLABS).astype(I32),
        "off": (off.reshape(-1) * ROW_SLABS).astype(I32),
        "pad_start": ((pstarts + totals) * ROW_SLABS).astype(I32),
        "pad_len": ((padded - totals) * ROW_SLABS).astype(I32),
        "n_valid": n_valid.reshape(1),
        "block_expert": block_e.astype(I32),
        "expert_order": order.astype(I32),
    }


def _layer_params(p, li):
    eye = jnp.eye(N_LRU_BLOCKS, dtype=F32)
    w_rg_full = jnp.einsum("dkgio,gh->gidkho", p["w_rg"][li], eye).reshape(LRU_WIDTH, 4 * LRU_WIDTH)
    pad = LANES - N_EXPERTS
    return {
        "sgu_ln_g": p["sgu_ln_g"][li].reshape(1, SGU_WIDTH),
        "sgu_ln_b": p["sgu_ln_b"][li].reshape(1, SGU_WIDTH),
        "w_spatial": p["w_spatial"][li],
        "b_spatial_full": jnp.repeat(p["b_spatial"][li].T, SGU_GROUP, axis=1),
        "conv_w": p["conv_w"][li],
        "conv_b": p["conv_b"][li].reshape(1, LRU_WIDTH),
        "w_rg_full": w_rg_full,
        "b_rg_full": p["b_rg"][li].reshape(1, 4 * LRU_WIDTH),
        "lru_log": p["lru_log"][li],
        "w_out_all": p["w_out"],
        "ln_g1": p["ln_g"][li, 0].reshape(1, D_MODEL),
        "ln_b1": p["ln_b"][li, 0].reshape(1, D_MODEL),
        "ln_g2": p["ln_g"][li, 1].reshape(1, D_MODEL),
        "ln_b2": p["ln_b"][li, 1].reshape(1, D_MODEL),
        "w_router_pad": jnp.pad(p["w_router"][li], ((0, 0), (0, pad))),
        "b_router_pad": jnp.pad(p["b_router"][li], (0, pad)).reshape(1, LANES),
    }


def kernel(x_prompt, x_sample, cache_k, cache_v, state_lru, c, c_ctx, w_mod, b_mod, w_in, lam, subln_g, sgu_ln_g, sgu_ln_b, w_spatial, b_spatial, conv_w, conv_b, w_rg, b_rg, lru_log, w_out, ln_g, ln_b, w_router, b_router, w_gu, b_gu, w_down, b_down):
    p = dict(sgu_ln_g=sgu_ln_g, sgu_ln_b=sgu_ln_b, w_spatial=w_spatial, b_spatial=b_spatial, conv_w=conv_w,
             conv_b=conv_b, w_rg=w_rg, b_rg=b_rg, lru_log=lru_log, w_out=w_out, ln_g=ln_g, ln_b=ln_b,
             w_router=w_router, b_router=b_router)
    cvec8 = jnp.concatenate([c_ctx[None, :], c, jnp.zeros((SUBLANES - 1 - DEC_BATCH, D_MODEL), F32)], axis=0)
    mod = _modulation(cvec8, w_mod, b_mod)
    x_pair = (x_prompt.reshape(N_CTX, D_MODEL), x_sample.reshape(N_DEC, D_MODEL))
    tables = _rope_tables()
    zero_state = jnp.zeros((BATCH, 1, 2, LRU_WIDTH), F32)
    prev_kv = None
    new_s = []
    for li in range(DEPTH):
        lp = _layer_params(p, li)
        lam_init = 0.8 - 0.6 * math.exp(-0.3 * li)
        proj = _input_projection(*x_pair, mod, w_in, li)
        att_ctx, kc, vc = _context_attention(proj, lam[li], subln_g[li], lam_init, prev_kv)
        prev_kv = (kc, vc)
        att_dec = _denoise_attention(proj, cache_k, cache_v, lam[li], subln_g[li], li, lam_init, tables)
        sgu_ctx, lru_ctx, h_ctx = _mixers(proj, zero_state, 0, lp, SEQ, BATCH, 0)
        sgu_dec, lru_dec, _ = _mixers(proj, state_lru, li, lp, DEC_SEQ, DEC_BATCH, N_CTX // DEC_SEQ)
        x1, h2, route_i, route_f, tile_cnt = _output_projection(
            x_pair, (att_ctx, att_dec), (sgu_ctx, sgu_dec), (lru_ctx, lru_dec), mod, lp, li)
        rt = _routing_tables(tile_cnt)
        xs = _dispatch(rt, h2, route_i)
        y = _moe_blocks(rt, xs, w_gu, b_gu, w_down, b_down, li)
        x_pair = _combine(rt, y, x1, route_i, route_f, mod, lp["ln_g2"], lp["ln_b2"], li)
        new_s.append(h_ctx)
    y_prompt = x_pair[0].reshape(BATCH, SEQ, D_MODEL)
    y_sample = x_pair[1].reshape(DEC_BATCH, DEC_SEQ, D_MODEL)
    return (y_prompt, y_sample, prev_kv[0], prev_kv[1], jnp.stack(new_s, axis=1))
```

```python
import functools
import math

import numpy as np
import jax
import jax.numpy as jnp
from jax import lax
from jax.experimental import pallas as pl
from jax.experimental.pallas import tpu as pltpu

F32 = jnp.float32
BF16 = jnp.bfloat16
I32 = jnp.int32

D_MODEL = 1024
BATCH = 32
SEQ = 256
DEPTH = 2
DEC_BATCH = 2
DEC_SEQ = 1024
PAST_LEN = 256
GRID_W = 64
HEAD_DIM = 64
ATT_WIDTH = D_MODEL // 2
SGU_WIDTH = D_MODEL // 4
LRU_WIDTH = D_MODEL // 4
N_ATT_HEADS = ATT_WIDTH // HEAD_DIM
ATT_HALF = HEAD_DIM // 2
ROPE_FREQS = ATT_HALF // 4
ROPE_THETA = 10000.0
CHUNK = 128
N_SGU_GROUPS = 4
SGU_GROUP = SGU_WIDTH // N_SGU_GROUPS
N_LRU_BLOCKS = 4
LRU_BLOCK = LRU_WIDTH // N_LRU_BLOCKS
CONV_W = 4
LRU_C = 8.0
IN_COLS = 3 * ATT_WIDTH + 2 * SGU_WIDTH + 2 * LRU_WIDTH
N_EXPERTS = 32
TOP_K = 4
D_EXPERT = D_MODEL
SWIGLU_LIMIT = 7.0
SWIGLU_ALPHA = 1.702
DN_ALPHA = (2 * DEPTH) ** 0.25
EPS = 1e-5

N_CTX = BATCH * SEQ
N_DEC = DEC_BATCH * DEC_SEQ
N_TOK = N_CTX + N_DEC

LANES = 128
SUBLANES = 8
ROW_SLABS = D_MODEL // LANES

TM = 512
CTX_TILES = N_CTX // TM
TILES_PER_DEC = DEC_SEQ // TM
N_TILES = N_TOK // TM
MOD_TN = 1024
RT = 256
N_RT = N_TOK // RT
RT_PER_TM = TM // RT
RT_ROWS = RT * TOP_K
RT_PER_STEP = 4
FETCH_AHEAD = 2
BM = 256
NB = N_TOK * TOP_K // BM + N_EXPERTS
WEIGHT_PIECES = 8
X_BUFFERS = 4
N_SLOTS = NB * BM
RUN_BITS = tuple(1 << b for b in range(RT.bit_length() - 1, -1, -1))
PAD_BITS = tuple(1 << b for b in range(BM.bit_length() - 2, -1, -1))
VMEM_LIMIT = 56 * 1024 * 1024


def _cparams(n_axes):
    return pltpu.CompilerParams(
        dimension_semantics=("arbitrary",) * n_axes,
        vmem_limit_bytes=VMEM_LIMIT)


def _mod_row(i):
    return jnp.where(i < CTX_TILES, 0, 1 + (i - CTX_TILES) // TILES_PER_DEC)


def _layer_norm_rows(z, g, b):
    mu = jnp.mean(z, axis=-1, keepdims=True)
    zc = z - mu
    var = jnp.mean(zc * zc, axis=-1, keepdims=True)
    return zc * lax.rsqrt(var + EPS) * g + b


def _pair_specs(tile, width, n_ctx_tiles):
    ctx = pl.BlockSpec((tile, width), lambda i, *_: (jnp.minimum(i, n_ctx_tiles - 1), 0))
    dec = pl.BlockSpec((tile, width), lambda i, *_: (jnp.maximum(i - n_ctx_tiles, 0), 0))
    return [ctx, dec]


def _pair_read(i, n_ctx_tiles, ctx_ref, dec_ref):
    return jnp.where(i < n_ctx_tiles, ctx_ref[...], dec_ref[...])


def _mod_kernel(cvec_ref, w_ref, b_ref, o_ref):
    cv = cvec_ref[...]
    s = cv * jax.nn.sigmoid(cv)
    s_t = s.T
    w = w_ref[0]
    rows = [jnp.sum(s_t[:, r:r + 1] * w, axis=0, keepdims=True) for r in range(1 + DEC_BATCH)]
    rows.append(jnp.zeros((SUBLANES - 1 - DEC_BATCH, MOD_TN), F32))
    o_ref[0] = jnp.concatenate(rows, axis=0) + b_ref[0]


def _modulation(cvec8, w_mod, b_mod):
    n_out = w_mod.shape[-1]
    return pl.pallas_call(
        _mod_kernel,
        grid=(DEPTH, n_out // MOD_TN),
        in_specs=[
            pl.BlockSpec((SUBLANES, D_MODEL), lambda l, j: (0, 0)),
            pl.BlockSpec((1, D_MODEL, MOD_TN), lambda l, j: (l, 0, j)),
            pl.BlockSpec((1, 1, MOD_TN), lambda l, j: (l, 0, j)),
        ],
        out_specs=pl.BlockSpec((1, SUBLANES, MOD_TN), lambda l, j: (l, 0, j)),
        out_shape=jax.ShapeDtypeStruct((DEPTH, SUBLANES, n_out), F32),
        compiler_params=_cparams(2),
        name="modulation",
    )(cvec8, w_mod, b_mod.reshape(DEPTH, 1, n_out))


def _inproj_kernel(xc_ref, xd_ref, mod_ref, w_ref, o_ref, wbf_ref):
    i = pl.program_id(0)

    @pl.when(i == 0)
    def _():
        wbf_ref[...] = w_ref[...].astype(BF16)

    m = mod_ref[pl.ds(_mod_row(i), 1), :]
    sh1 = m[:, 0:D_MODEL]
    sc1 = m[:, D_MODEL:2 * D_MODEL]
    x = _pair_read(i, CTX_TILES, xc_ref, xd_ref)
    h = (x * (1.0 + sc1) + sh1).astype(BF16)
    o_ref[...] = jnp.dot(h, wbf_ref[...], preferred_element_type=F32)


def _input_projection(x_ctx, x_dec, mod, w_in, li):
    return pl.pallas_call(
        _inproj_kernel,
        grid=(N_TILES,),
        in_specs=_pair_specs(TM, D_MODEL, CTX_TILES) + [
            pl.BlockSpec((None, SUBLANES, 6 * D_MODEL), lambda i: (li, 0, 0)),
            pl.BlockSpec((None, D_MODEL, IN_COLS), lambda i: (li, 0, 0), pipeline_mode=pl.Buffered(1)),
        ],
        out_specs=pl.BlockSpec((TM, IN_COLS), lambda i: (i, 0)),
        out_shape=jax.ShapeDtypeStruct((N_TOK, IN_COLS), F32),
        scratch_shapes=[pltpu.VMEM((D_MODEL, IN_COLS), BF16)],
        compiler_params=_cparams(1),
        name="input_projection",
    )(x_ctx, x_dec, mod, w_in)


def _lambda(lam_ref, lam_init):
    lm = lam_ref[...]
    a = jnp.sum(lm[0:1] * lm[1:2], axis=-1, keepdims=True)
    b = jnp.sum(lm[2:3] * lm[3:4], axis=-1, keepdims=True)
    return jnp.exp(a) - jnp.exp(b) + lam_init


_NT = (((1,), (1,)), ((), ()))
_TN = (((0,), (0,)), ((), ()))
_ATT_SCALE = ATT_HALF ** -0.5
_Q_SCALE = _ATT_SCALE * math.log2(math.e)
KEY_CHUNK = 128


def _attention_heads(q_parts, k_segs, values, s_scr, e_scr, lam, g_col, lam_init):
    n_heads = len(q_parts)
    n_keys, n_q = s_scr.shape[1:]
    tops = []
    for h in range(n_heads):
        for c in range(2):
            row = 0
            seg_tops = []
            for k in k_segs[h][c]:
                s = lax.dot_general(k, q_parts[h][c], _NT, preferred_element_type=F32)
                s_scr[2 * h + c, row:row + k.shape[0], :] = s
                seg_tops.append(jnp.max(s.reshape(k.shape[0] // SUBLANES, SUBLANES, n_q), axis=0))
                row += k.shape[0]
            tops.append(jnp.max(functools.reduce(jnp.maximum, seg_tops), axis=0, keepdims=True))
    chunks = [slice(r, r + KEY_CHUNK) for r in range(0, n_keys, KEY_CHUNK)]
    for n in range(2 * n_heads):
        for rows in chunks:
            e_scr[n, rows, :] = jnp.exp2(s_scr[n, rows, :] - tops[n]).astype(BF16)
    outs = []
    for h in range(n_heads):
        v_aug = jnp.concatenate([values[h], jnp.ones_like(values[h])], axis=-1)
        normed = []
        for c in range(2):
            acc = lax.dot_general(v_aug, e_scr[2 * h + c], _TN, preferred_element_type=F32)
            normed.append(acc[:HEAD_DIM] * (1.0 / acc[HEAD_DIM:HEAD_DIM + 1]))
        o_t = normed[0] - lam * normed[1]
        ms = jnp.mean(o_t * o_t, axis=0, keepdims=True)
        outs.append(o_t * lax.rsqrt(ms + EPS) * g_col * (1.0 - lam_init))
    return outs


def _head_cols(h):
    lo = h * HEAD_DIM
    return [slice(lo + c * ATT_HALF, lo + (c + 1) * ATT_HALF) for c in range(2)]


def _ctx_attn_kernel(lam_ref, g_ref, q_ref, k_ref, v_ref, *rest, lam_init, stacked):
    *rest, s_scr, e_scr = rest
    if stacked:
        pk_ref, pv_ref, att_ref, ck_ref, cv_ref = rest
        ck_ref[0, 0] = pk_ref[0]
        cv_ref[0, 0] = pv_ref[0]
        put_k = lambda h, val: ck_ref.__setitem__((0, 1, h), val)
        put_v = lambda h, val: cv_ref.__setitem__((0, 1, h), val)
    else:
        att_ref, ck_ref, cv_ref = rest
        put_k = lambda h, val: ck_ref.__setitem__((0, h), val)
        put_v = lambda h, val: cv_ref.__setitem__((0, h), val)
    lam = _lambda(lam_ref, lam_init)
    k = k_ref[...]
    v = v_ref[...]
    qb = (q_ref[...] * _Q_SCALE).astype(BF16)
    kb = k.astype(BF16)
    vb = v.astype(BF16)
    heads = range(N_ATT_HEADS)
    outs = _attention_heads([[qb[:, cs] for cs in _head_cols(h)] for h in heads],
                            [[[kb[:, cs]] for cs in _head_cols(h)] for h in heads],
                            [vb[:, h * HEAD_DIM:(h + 1) * HEAD_DIM] for h in heads],
                            s_scr, e_scr, lam, g_ref[...], lam_init)
    for h in heads:
        put_k(h, k[:, h * HEAD_DIM:(h + 1) * HEAD_DIM])
        put_v(h, v[:, h * HEAD_DIM:(h + 1) * HEAD_DIM])
    att_ref[...] = jnp.concatenate(outs, axis=0).T


def _context_attention(proj, lam, subln_g, lam_init, prev_kv=None):
    stacked = prev_kv is not None
    per_layer = pl.BlockSpec((1, N_ATT_HEADS, SEQ, HEAD_DIM), lambda b: (b, 0, 0, 0))
    if stacked:
        kv_shape = jax.ShapeDtypeStruct((BATCH, DEPTH, N_ATT_HEADS, SEQ, HEAD_DIM), F32)
        kv_spec = pl.BlockSpec((1, DEPTH, N_ATT_HEADS, SEQ, HEAD_DIM), lambda b: (b, 0, 0, 0, 0))
    else:
        kv_shape = jax.ShapeDtypeStruct((BATCH, N_ATT_HEADS, SEQ, HEAD_DIM), F32)
        kv_spec = per_layer
    return pl.pallas_call(
        functools.partial(_ctx_attn_kernel, lam_init=lam_init, stacked=stacked),
        grid=(BATCH,),
        in_specs=[
            pl.BlockSpec((4, ATT_HALF), lambda b: (0, 0)),
            pl.BlockSpec((HEAD_DIM, 1), lambda b: (0, 0)),
            pl.BlockSpec((SEQ, ATT_WIDTH), lambda b: (b, 0)),
            pl.BlockSpec((SEQ, ATT_WIDTH), lambda b: (b, 1)),
            pl.BlockSpec((SEQ, ATT_WIDTH), lambda b: (b, 2)),
        ] + ([per_layer, per_layer] if stacked else []),
        out_specs=[pl.BlockSpec((SEQ, ATT_WIDTH), lambda b: (b, 0)), kv_spec, kv_spec],
        out_shape=[jax.ShapeDtypeStruct((N_CTX, ATT_WIDTH), F32), kv_shape, kv_shape],
        scratch_shapes=[pltpu.VMEM((2 * N_ATT_HEADS, SEQ, SEQ), F32),
                        pltpu.VMEM((2 * N_ATT_HEADS, SEQ, SEQ), BF16)],
        compiler_params=_cparams(1),
        name="context_attention",
    )(lam, subln_g.reshape(HEAD_DIM, 1), proj, proj, proj, *(prev_kv if stacked else ()))


def _rope_tables():
    t = np.arange(DEC_SEQ)
    pos = np.stack([t // GRID_W, t % GRID_W], axis=1).astype(np.float32)
    inv = (np.float32(ROPE_THETA) ** (-np.arange(ROPE_FREQS, dtype=np.float32) / np.float32(ROPE_FREQS)))
    j = np.arange(HEAD_DIM)
    d = j % ATT_HALF
    axis = d // (2 * ROPE_FREQS)
    u = d % (2 * ROPE_FREQS)
    ang = pos[:, axis] * inv[u % ROPE_FREQS][None, :].astype(np.float32)
    cos = np.cos(ang).astype(np.float32)
    sin = np.sin(ang).astype(np.float32)
    first = (u < ROPE_FREQS)[None, :]
    s_next = np.where(first, -sin, 0.0).astype(np.float32)
    s_prev = np.where(first, 0.0, sin).astype(np.float32)
    tile = lambda a: jnp.asarray(np.tile(a, (1, N_ATT_HEADS)))
    return tile(cos), tile(s_next), tile(s_prev)


def _rotate(x, cos, s_next, s_prev):
    width = x.shape[-1]
    return (x * cos + pltpu.roll(x, width - ROPE_FREQS, axis=1) * s_next
            + pltpu.roll(x, ROPE_FREQS, axis=1) * s_prev)


QB = 256
Q_STEPS = DEC_SEQ // QB
DEC_HEAD_GROUP = 2


def _dec_attn_kernel(lam_ref, g_ref, q_ref, k_ref, v_ref, ck_ref, cv_ref,
                     cq_ref, snq_ref, spq_ref, ck_tab, snk_tab, spk_tab,
                     att_ref, krot_ref, s_scr, e_scr, *, lam_init):
    j = pl.program_id(1)

    @pl.when(j == 0)
    def _():
        krot_ref[...] = _rotate(k_ref[...], ck_tab[...], snk_tab[...], spk_tab[...]).astype(BF16)

    lam = _lambda(lam_ref, lam_init)
    qb = (_rotate(q_ref[...], cq_ref[...], snq_ref[...], spq_ref[...]) * _Q_SCALE).astype(BF16)
    kb = krot_ref[...]
    vb = v_ref[...].astype(BF16)
    outs = []
    for h0 in range(0, N_ATT_HEADS, DEC_HEAD_GROUP):
        heads = range(h0, h0 + DEC_HEAD_GROUP)
        past_k = [ck_ref[0, 0, h].astype(BF16) for h in heads]
        k_segs = [[[pk[:, c * ATT_HALF:(c + 1) * ATT_HALF], kb[:, _head_cols(h)[c]]] for c in range(2)]
                  for h, pk in zip(heads, past_k)]
        values = [jnp.concatenate([cv_ref[0, 0, h].astype(BF16), vb[:, h * HEAD_DIM:(h + 1) * HEAD_DIM]], axis=0)
                  for h in heads]
        outs += _attention_heads([[qb[:, cs] for cs in _head_cols(h)] for h in heads], k_segs, values,
                                 s_scr, e_scr, lam, g_ref[...], lam_init)
    att_ref[...] = jnp.concatenate(outs, axis=0).T


def _denoise_attention(proj, cache_k, cache_v, lam, subln_g, li, lam_init, tables):
    cos, s_next, s_prev = tables
    row0 = N_CTX // QB
    seq0 = N_CTX // DEC_SEQ
    q_tab = pl.BlockSpec((QB, ATT_WIDTH), lambda b, j: (j, 0))
    k_tab = pl.BlockSpec((DEC_SEQ, ATT_WIDTH), lambda b, j: (0, 0))
    cache_spec = pl.BlockSpec((1, 1, N_ATT_HEADS, PAST_LEN, HEAD_DIM), lambda b, j: (b, li, 0, 0, 0))
    return pl.pallas_call(
        functools.partial(_dec_attn_kernel, lam_init=lam_init),
        grid=(DEC_BATCH, Q_STEPS),
        in_specs=[
            pl.BlockSpec((4, ATT_HALF), lambda b, j: (0, 0)),
            pl.BlockSpec((HEAD_DIM, 1), lambda b, j: (0, 0)),
            pl.BlockSpec((QB, ATT_WIDTH), lambda b, j: (row0 + b * Q_STEPS + j, 0)),
            pl.BlockSpec((DEC_SEQ, ATT_WIDTH), lambda b, j: (seq0 + b, 1)),
            pl.BlockSpec((DEC_SEQ, ATT_WIDTH), lambda b, j: (seq0 + b, 2)),
            cache_spec, cache_spec,
            q_tab, q_tab, q_tab, k_tab, k_tab, k_tab,
        ],
        out_specs=pl.BlockSpec((QB, ATT_WIDTH), lambda b, j: (b * Q_STEPS + j, 0)),
        out_shape=jax.ShapeDtypeStruct((N_DEC, ATT_WIDTH), F32),
        scratch_shapes=[pltpu.VMEM((DEC_SEQ, ATT_WIDTH), BF16),
                        pltpu.VMEM((2 * DEC_HEAD_GROUP, PAST_LEN + DEC_SEQ, QB), F32),
                        pltpu.VMEM((2 * DEC_HEAD_GROUP, PAST_LEN + DEC_SEQ, QB), BF16)],
        compiler_params=_cparams(2),
        name="denoise_attention",
    )(lam, subln_g.reshape(HEAD_DIM, 1), proj, proj, proj, cache_k, cache_v,
      cos, s_next, s_prev, cos, s_next, s_prev)


def _softplus(z):
    return jnp.maximum(z, 0.0) + jnp.log1p(jnp.exp(-jnp.abs(z)))


def _mixer_kernel(su_ref, sv_ref, rx_ref, rg_ref, h0_ref, lng_ref, lnb_ref, ws_ref, bs_ref,
                  cw_ref, cb_ref, wrg_ref, brg_ref, lrulog_ref,
                  sgu_ref, lru_ref, hlast_ref, a_scr, b_scr, h_scr, *, seq_len):
    vn = _layer_norm_rows(sv_ref[...], lng_ref[...], lnb_ref[...])
    lane_group = lax.broadcasted_iota(I32, (CHUNK, SGU_WIDTH), 1) // SGU_GROUP
    for n in range(seq_len // CHUNK):
        rows = slice(n * CHUNK, (n + 1) * CHUNK)
        vc = vn[rows].astype(BF16)
        s = jnp.zeros((CHUNK, SGU_WIDTH), F32)
        for g in range(N_SGU_GROUPS):
            sg = jnp.dot(ws_ref[g].astype(BF16), vc, preferred_element_type=F32)
            s = jnp.where(lane_group == g, sg, s)
        sgu_ref[rows, :] = su_ref[rows, :] * (s + bs_ref[...])

    x = rx_ref[...]
    row = lax.broadcasted_iota(I32, (seq_len, LRU_WIDTH), 0)

    def shifted(val, d, fill):
        rolled = pltpu.roll(val, d % seq_len, axis=0)
        inside = (row >= d) if d > 0 else (row < seq_len + d)
        return jnp.where(inside, rolled, fill)

    left = CONV_W // 2
    xc = cb_ref[...] + x * cw_ref[left:left + 1, :]
    for tap in range(CONV_W):
        if tap != left:
            xc = xc + shifted(x, left - tap, 0.0) * cw_ref[tap:tap + 1, :]
    pre = jnp.dot(xc.astype(BF16), wrg_ref[...].astype(BF16), preferred_element_type=F32) + brg_ref[...]
    gates = 0.5 + 0.5 * jnp.tanh(0.5 * pre)
    in_chunk = row % SUBLANES
    n_chunks = seq_len // SUBLANES
    h0 = h0_ref[0, 0]
    lasts = []
    for direction in range(2):
        reverse = direction == 1
        base = direction * 2 * LRU_WIDTH
        r = gates[:, base:base + LRU_WIDTH]
        gi = gates[:, base + LRU_WIDTH:base + 2 * LRU_WIDTH]
        log_a = -LRU_C * r * _softplus(-lrulog_ref[direction:direction + 1, :])
        a = jnp.exp(log_a)
        b = jnp.sqrt(-jnp.tanh(log_a) * (a * a + 1.0)) * gi * xc
        def chunk_roll(val, shift):
            chunks = val.reshape(n_chunks, SUBLANES, LRU_WIDTH)
            return pltpu.roll(chunks, shift, axis=1).reshape(seq_len, LRU_WIDTH)

        for d in (1, 2, 4):
            if reverse:
                inside = in_chunk < SUBLANES - d
                a_n = jnp.where(inside, chunk_roll(a, SUBLANES - d), 1.0)
                b_n = jnp.where(inside, chunk_roll(b, SUBLANES - d), 0.0)
            else:
                inside = in_chunk >= d
                a_n = jnp.where(inside, chunk_roll(a, d), 1.0)
                b_n = jnp.where(inside, chunk_roll(b, d), 0.0)
            b = a * b_n + b
            a = a * a_n
        a_scr[...] = a
        b_scr[...] = b

        def chunk_step(c, carry, reverse=reverse):
            cc = n_chunks - 1 - c if reverse else c
            off = pl.multiple_of(cc * SUBLANES, SUBLANES)
            hc = a_scr[pl.ds(off, SUBLANES), :] * carry + b_scr[pl.ds(off, SUBLANES), :]
            if reverse:
                h_scr[pl.ds(off, SUBLANES), :] = h_scr[pl.ds(off, SUBLANES), :] + hc
                return hc[0:1, :]
            h_scr[pl.ds(off, SUBLANES), :] = hc
            return hc[SUBLANES - 1:SUBLANES, :]

        lasts.append(lax.fori_loop(0, n_chunks, chunk_step, h0[direction:direction + 1, :]))
    lru_ref[...] = h_scr[...] * jax.nn.gelu(rg_ref[...])
    hlast_ref[0] = jnp.concatenate(lasts, axis=0)


def _mixers(proj, h0, h0_layer, lp, seq_len, n_seq, row_block0):
    col0 = 3 * ATT_WIDTH // SGU_WIDTH
    col = lambda c: pl.BlockSpec((seq_len, SGU_WIDTH), lambda b: (row_block0 + b, col0 + c))
    full = lambda shape: pl.BlockSpec(shape, lambda b: (0,) * len(shape))
    out_rows = pl.BlockSpec((seq_len, SGU_WIDTH), lambda b: (b, 0))
    return pl.pallas_call(
        functools.partial(_mixer_kernel, seq_len=seq_len),
        grid=(n_seq,),
        in_specs=[
            col(0), col(1), col(2), col(3),
            pl.BlockSpec((1, 1, 2, LRU_WIDTH), lambda b: (b, h0_layer, 0, 0)),
            full((1, SGU_WIDTH)), full((1, SGU_WIDTH)),
            full((N_SGU_GROUPS, CHUNK, CHUNK)), full((CHUNK, SGU_WIDTH)),
            full((CONV_W, LRU_WIDTH)), full((1, LRU_WIDTH)),
            full((LRU_WIDTH, 4 * LRU_WIDTH)), full((1, 4 * LRU_WIDTH)),
            full((2, LRU_WIDTH)),
        ],
        out_specs=[out_rows, out_rows, pl.BlockSpec((1, 2, LRU_WIDTH), lambda b: (b, 0, 0))],
        out_shape=[
            jax.ShapeDtypeStruct((n_seq * seq_len, SGU_WIDTH), F32),
            jax.ShapeDtypeStruct((n_seq * seq_len, LRU_WIDTH), F32),
            jax.ShapeDtypeStruct((n_seq, 2, LRU_WIDTH), F32),
        ],
        scratch_shapes=[pltpu.VMEM((seq_len, LRU_WIDTH), F32)] * 3,
        compiler_params=_cparams(1),
        name="mixers_%d" % seq_len,
    )(proj, proj, proj, proj, h0, lp["sgu_ln_g"], lp["sgu_ln_b"], lp["w_spatial"], lp["b_spatial_full"],
      lp["conv_w"], lp["conv_b"], lp["w_rg_full"], lp["b_rg_full"], lp["lru_log"])


def _outproj_kernel(xc_ref, xd_ref, ac_ref, ad_ref, sc_ref, sd_ref, lc_ref, ld_ref,
                    mod_ref, wout_ref, lng_ref, lnb_ref, wr_ref, br_ref,
                    x1_ref, h2_ref, ri_ref, rf_ref, cnt_ref, wbf_ref):
    i = pl.program_id(0)

    @pl.when(i == 0)
    def _():
        wbf_ref[...] = wout_ref[...].astype(BF16)

    m = mod_ref[pl.ds(_mod_row(i), 1), :]
    g1 = m[:, 2 * D_MODEL:3 * D_MODEL]
    sh2 = m[:, 3 * D_MODEL:4 * D_MODEL]
    sc2 = m[:, 4 * D_MODEL:5 * D_MODEL]
    a0, a1 = ATT_WIDTH, ATT_WIDTH + SGU_WIDTH
    x = _pair_read(i, CTX_TILES, xc_ref, xd_ref)
    att = _pair_read(i, CTX_TILES, ac_ref, ad_ref).astype(BF16)
    sgu = _pair_read(i, CTX_TILES, sc_ref, sd_ref).astype(BF16)
    lru = _pair_read(i, CTX_TILES, lc_ref, ld_ref).astype(BF16)
    mix = (jnp.dot(att, wbf_ref[0:a0, :], preferred_element_type=F32)
           + jnp.dot(sgu, wbf_ref[a0:a1, :], preferred_element_type=F32)
           + jnp.dot(lru, wbf_ref[a1:, :], preferred_element_type=F32))
    x1 = _layer_norm_rows(DN_ALPHA * x + g1 * mix, lng_ref[...], lnb_ref[...])
    x1_ref[...] = x1
    h2 = x1 * (1.0 + sc2) + sh2
    h2_ref[...] = h2.astype(BF16)

    logits = jnp.dot(h2.astype(BF16), wr_ref[...].astype(BF16), preferred_element_type=F32) + br_ref[...]
    lane = lax.broadcasted_iota(I32, (TM, LANES), 1)
    lane_f = lane.astype(F32)
    neg_inf = jnp.float32(-jnp.inf)
    work = jnp.where(lane < N_EXPERTS, logits, neg_inf)
    vals, idxs = [], []
    for _ in range(TOP_K):
        top = jnp.max(work, axis=-1, keepdims=True)
        idx = jnp.min(jnp.where(work == top, lane_f, float(LANES)), axis=-1, keepdims=True)
        vals.append(top)
        idxs.append(idx)
        work = jnp.where(lane_f == idx, neg_inf, work)
    exps = [jnp.exp(v - vals[0]) for v in vals]
    denom = exps[0] + exps[1] + exps[2] + exps[3]
    onehot = jnp.zeros((TM, LANES), F32)
    for idx in idxs:
        onehot = onehot + (lane_f == idx).astype(F32)
    r_i = lax.broadcasted_iota(I32, (RT, RT), 0)
    c_i = lax.broadcasted_iota(I32, (RT, RT), 1)
    tri = (r_i > c_i).astype(F32).astype(BF16)
    upper = (lax.broadcasted_iota(I32, (LANES, LANES), 0)
             < lax.broadcasted_iota(I32, (LANES, LANES), 1)).astype(F32).astype(BF16)
    packed_pos = []
    for t in range(RT_PER_TM):
        hot = onehot[t * RT:(t + 1) * RT]
        total = jnp.broadcast_to(jnp.sum(hot, axis=0, keepdims=True), (SUBLANES, LANES))
        cnt_ref[t] = total.astype(I32)
        run_start = jnp.dot(total.astype(BF16), upper, preferred_element_type=F32)[0:1, :]
        packed_pos.append(jnp.dot(tri, hot.astype(BF16), preferred_element_type=F32) + run_start)
    packed_pos = jnp.concatenate(packed_pos, axis=0)
    ri = jnp.zeros((TM, LANES), F32)
    rf = jnp.zeros((TM, LANES), F32)
    for k in range(TOP_K):
        pos = jnp.sum(jnp.where(lane_f == idxs[k], packed_pos, 0.0), axis=-1, keepdims=True)
        ri = jnp.where(lane == k, idxs[k], ri)
        ri = jnp.where(lane == TOP_K + k, pos, ri)
        rf = jnp.where(lane == k, exps[k] / denom, rf)
    ri_ref[...] = ri.astype(I32)
    rf_ref[...] = rf


def _output_projection(x_pair, att_pair, sgu_pair, lru_pair, mod, lp, li):
    rows = lambda w: pl.BlockSpec((TM, w), lambda i: (i, 0))
    full = lambda shape: pl.BlockSpec(shape, lambda i: (0,) * len(shape))
    return pl.pallas_call(
        _outproj_kernel,
        grid=(N_TILES,),
        in_specs=(_pair_specs(TM, D_MODEL, CTX_TILES) + _pair_specs(TM, ATT_WIDTH, CTX_TILES)
                  + _pair_specs(TM, SGU_WIDTH, CTX_TILES) + _pair_specs(TM, LRU_WIDTH, CTX_TILES) + [
            pl.BlockSpec((None, SUBLANES, 6 * D_MODEL), lambda i: (li, 0, 0)),
            pl.BlockSpec((None, D_MODEL, D_MODEL), lambda i: (li, 0, 0), pipeline_mode=pl.Buffered(1)),
            full((1, D_MODEL)), full((1, D_MODEL)),
            full((D_MODEL, LANES)), full((1, LANES)),
        ]),
        out_specs=[
            rows(D_MODEL), rows(D_MODEL), rows(LANES), rows(LANES),
            pl.BlockSpec((RT_PER_TM, SUBLANES, LANES), lambda i: (i, 0, 0)),
        ],
        out_shape=[
            jax.ShapeDtypeStruct((N_TOK, D_MODEL), F32),
            jax.ShapeDtypeStruct((N_TOK, D_MODEL), BF16),
            jax.ShapeDtypeStruct((N_TOK, LANES), I32),
            jax.ShapeDtypeStruct((N_TOK, LANES), F32),
            jax.ShapeDtypeStruct((N_RT, SUBLANES, LANES), I32),
        ],
        scratch_shapes=[pltpu.VMEM((D_MODEL, D_MODEL), BF16)],
        compiler_params=_cparams(1),
        name="output_projection",
    )(*x_pair, *att_pair, *sgu_pair, *lru_pair, mod, lp["w_out_all"], lp["ln_g1"], lp["ln_b1"],
      lp["w_router_pad"], lp["b_router_pad"])


def _row_ds(row, n_rows):
    return pl.ds(pl.multiple_of(row * ROW_SLABS, ROW_SLABS), n_rows * ROW_SLABS)


def _for_each_run_piece(length_s, bits, fn):
    for bit in bits:
        done_s = length_s & (-2 * bit * ROW_SLABS)
        @pl.when((length_s & (bit * ROW_SLABS)) != 0)
        def _(done_s=done_s, bit=bit):
            fn(done_s, bit)


def _stored_ds(row_s, n_rows):
    return pl.ds(pl.multiple_of(row_s, ROW_SLABS), n_rows * ROW_SLABS)


def _tile_run_copies(tile, start_ref, len_ref, off_ref, sorted_hbm, buf, sem, to_sorted, live=None):
    for e in range(N_EXPERTS):
        t = tile * N_EXPERTS + e
        start_s, length_s, off_s = start_ref[t], len_ref[t], off_ref[t]
        if live is not None:
            length_s = jnp.where(live, length_s, 0)

        def piece(done_s, bit, start_s=start_s, off_s=off_s):
            packed = buf.at[_stored_ds(off_s + done_s, bit), :]
            srt = sorted_hbm.at[_stored_ds(start_s + done_s, bit), :]
            if to_sorted:
                pltpu.make_async_copy(packed, srt, sem).start()
            else:
                pltpu.make_async_copy(srt, packed, sem).start()

        _for_each_run_piece(length_s, RUN_BITS, piece)


def _tile_runs_wait(buf, sem):
    pltpu.make_async_copy(buf, buf, sem).wait()


def _packed_positions(ri):
    return [ri[:, TOP_K + k:TOP_K + k + 1].astype(F32) for k in range(TOP_K)]


def _slab_columns(buf, row0, n_rows):
    return jnp.concatenate(
        [buf[pl.ds(row0 * ROW_SLABS + s, n_rows, stride=ROW_SLABS), :] for s in range(ROW_SLABS)], axis=-1)


def _dispatch_kernel(start_ref, len_ref, off_ref, pstart_ref, plen_ref, nv_ref,
                     h2_ref, ri_ref, xs_hbm, *scratch):
    bufs, (zbuf, sem, zsem) = scratch[:RT_PER_STEP], scratch[RT_PER_STEP:]
    i = pl.program_id(0)
    n_steps = pl.num_programs(0)

    @pl.when(i == 0)
    def _():
        zbuf[...] = jnp.zeros_like(zbuf)

        def zero_fill(wait):
            def go(cp):
                cp.wait() if wait else cp.start()

            def per_expert(e, carry):
                def piece(done_s, bit):
                    go(pltpu.make_async_copy(zbuf.at[pl.ds(0, bit * ROW_SLABS), :],
                                             xs_hbm.at[_stored_ds(pstart_ref[e] + done_s, bit), :], zsem))
                _for_each_run_piece(plen_ref[e], PAD_BITS, piece)
                return carry
            lax.fori_loop(0, N_EXPERTS, per_expert, 0)

            def per_block(b, carry):
                go(pltpu.make_async_copy(zbuf, xs_hbm.at[_row_ds(b * BM, BM), :], zsem))
                return carry
            lax.fori_loop(nv_ref[0], NB, per_block, 0)

        zero_fill(False)
        zero_fill(True)

    def send(tile, t, live):
        _tile_run_copies(tile, start_ref, len_ref, off_ref, xs_hbm, bufs[t], sem.at[t], to_sorted=True, live=live)

    col = lax.broadcasted_iota(I32, (RT, RT_ROWS), 1).astype(F32)
    for t in range(RT_PER_STEP):
        buf = bufs[t]
        rows = slice(t * RT, (t + 1) * RT)
        tile = i * RT_PER_STEP + t

        @pl.when(i >= 1)
        def _(buf=buf, t=t):
            _tile_runs_wait(buf, sem.at[t])

        send(jnp.maximum(tile - 1, 0), (t - 1) % RT_PER_STEP, tile >= 1)
        pos = _packed_positions(ri_ref[rows, :])
        sel = jnp.zeros((RT, RT_ROWS), F32)
        for p in pos:
            sel = sel + (col == p).astype(F32)
        packed = lax.dot_general(sel.astype(BF16), h2_ref[rows, :], _TN, preferred_element_type=F32)
        for s in range(ROW_SLABS):
            buf[pl.ds(s, RT_ROWS, stride=ROW_SLABS), :] = packed[:, s * LANES:(s + 1) * LANES]

    @pl.when(i == n_steps - 1)
    def _():
        send(N_RT - 1, RT_PER_STEP - 1, True)
        for t in range(RT_PER_STEP):
            _tile_runs_wait(bufs[t], sem.at[t])


def _dispatch(tables, h2, route_i):
    grid_spec = pltpu.PrefetchScalarGridSpec(
        num_scalar_prefetch=6,
        grid=(N_RT // RT_PER_STEP,),
        in_specs=[
            pl.BlockSpec((RT_PER_STEP * RT, D_MODEL), lambda i, *_: (i, 0)),
            pl.BlockSpec((RT_PER_STEP * RT, LANES), lambda i, *_: (i, 0)),
        ],
        out_specs=pl.BlockSpec(memory_space=pl.ANY),
        scratch_shapes=[pltpu.VMEM((RT_ROWS * ROW_SLABS, LANES), F32)] * RT_PER_STEP + [
            pltpu.VMEM((BM * ROW_SLABS, LANES), F32),
            pltpu.SemaphoreType.DMA((RT_PER_STEP,)),
            pltpu.SemaphoreType.DMA(()),
        ],
    )
    return pl.pallas_call(
        _dispatch_kernel,
        grid_spec=grid_spec,
        out_shape=jax.ShapeDtypeStruct((N_SLOTS * ROW_SLABS, LANES), F32),
        compiler_params=_cparams(1),
        name="moe_dispatch",
    )(tables["start"], tables["len"], tables["off"], tables["pad_start"], tables["pad_len"], tables["n_valid"],
      h2, route_i)


def _moe_kernel(be_ref, eo_ref, nv_ref, xs_hbm, wgu_hbm, bgu_ref, wdn_hbm, bdn_ref, y_hbm,
                xbuf, ybuf, gu_stage, dn_stage, wgu_bf, wdn_bf, wsem, xsem, ysem, *, li):
    n_valid = nv_ref[0]
    rows_per_piece = D_MODEL // WEIGHT_PIECES

    def weight_copies(e, stage):
        cps = []
        for p in range(WEIGHT_PIECES):
            band = pl.ds(p * rows_per_piece, rows_per_piece)
            cps.append(pltpu.make_async_copy(wgu_hbm.at[li, e, band, :], gu_stage.at[stage, band, :], wsem.at[stage]))
            cps.append(pltpu.make_async_copy(wdn_hbm.at[li, e, band, :], dn_stage.at[stage, band, :], wsem.at[stage]))
        return cps

    def fetch(k):
        @pl.when(eo_ref[k] >= 0)
        def _():
            for cp in weight_copies(eo_ref[k], k % 2):
                cp.start(priority=1)

    def x_copy(b, slot):
        return pltpu.make_async_copy(xs_hbm.at[_row_ds(b * BM, BM), :], xbuf.at[slot], xsem.at[slot])

    def y_copy(b, slot):
        return pltpu.make_async_copy(ybuf.at[slot], y_hbm.at[_row_ds(b * BM, BM), :], ysem.at[slot])

    fetch(0)
    fetch(1)
    for ahead in range(X_BUFFERS - 1):
        @pl.when(ahead < n_valid)
        def _(ahead=ahead):
            x_copy(ahead, ahead).start()

    def next_slot(s):
        return jnp.where(s == X_BUFFERS - 1, 0, s + 1)

    def block(b, carry):
        k, xslot = carry
        slot = b % 2
        e = be_ref[b]
        new_expert = jnp.logical_or(b == 0, e != be_ref[jnp.maximum(b - 1, 0)])

        @pl.when(b + X_BUFFERS - 1 < n_valid)
        def _():
            ahead_slot = xslot
            for _ in range(X_BUFFERS - 1):
                ahead_slot = next_slot(ahead_slot)
            x_copy(b + X_BUFFERS - 1, ahead_slot).start()

        @pl.when(new_expert)
        def _():
            stage = k % 2
            for cp in weight_copies(e, stage):
                cp.wait()
            for st in range(2):
                @pl.when(stage == st)
                def _(st=st):
                    wgu_bf[...] = gu_stage[st].astype(BF16)
                    wdn_bf[...] = dn_stage[st].astype(BF16)
            fetch(k + 2)

        x_copy(b, xslot).wait()

        @pl.when(b >= 2)
        def _():
            y_copy(b - 2, slot).wait()

        x = _slab_columns(xbuf.at[xslot], 0, BM).astype(BF16)
        bgu = bgu_ref[e]
        g = jnp.dot(x, wgu_bf[:, :D_EXPERT], preferred_element_type=F32) + bgu[:, :D_EXPERT]
        u = jnp.dot(x, wgu_bf[:, D_EXPERT:], preferred_element_type=F32) + bgu[:, D_EXPERT:]
        g = jnp.minimum(g, SWIGLU_LIMIT)
        u = jnp.clip(u, -SWIGLU_LIMIT, SWIGLU_LIMIT)
        act = ((u + 1.0) * (0.5 * g * (1.0 + jnp.tanh((0.5 * SWIGLU_ALPHA) * g)))).astype(BF16)
        y = jnp.dot(act, wdn_bf[...], preferred_element_type=F32) + bdn_ref[e]
        out = ybuf.at[slot]
        for s in range(ROW_SLABS):
            out[pl.ds(s, BM, stride=ROW_SLABS), :] = y[:, s * LANES:(s + 1) * LANES]
        y_copy(b, slot).start()
        return k + new_expert.astype(I32), next_slot(xslot)

    lax.fori_loop(0, n_valid, block, (jnp.int32(0), jnp.int32(0)))

    @pl.when(n_valid >= 2)
    def _():
        y_copy(n_valid - 2, n_valid % 2).wait()
    y_copy(n_valid - 1, (n_valid - 1) % 2).wait()

    ybuf[0] = jnp.zeros((BM * ROW_SLABS, LANES), F32)

    def zero_blocks(wait):
        def one(b, carry):
            cp = y_copy(b, 0)
            cp.wait() if wait else cp.start()
            return carry
        lax.fori_loop(n_valid, NB, one, 0)

    zero_blocks(False)
    zero_blocks(True)


def _moe_blocks(tables, xs, w_gu, b_gu, w_down, b_down, li):
    grid_spec = pltpu.PrefetchScalarGridSpec(
        num_scalar_prefetch=3,
        grid=(1,),
        in_specs=[
            pl.BlockSpec(memory_space=pl.ANY),
            pl.BlockSpec(memory_space=pl.ANY),
            pl.BlockSpec((None, N_EXPERTS, 1, 2 * D_EXPERT), lambda i, *_: (li, 0, 0, 0)),
            pl.BlockSpec(memory_space=pl.ANY),
            pl.BlockSpec((None, N_EXPERTS, 1, D_MODEL), lambda i, *_: (li, 0, 0, 0)),
        ],
        out_specs=pl.BlockSpec(memory_space=pl.ANY),
        scratch_shapes=[
            pltpu.VMEM((X_BUFFERS, BM * ROW_SLABS, LANES), F32),
            pltpu.VMEM((2, BM * ROW_SLABS, LANES), F32),
            pltpu.VMEM((2, D_MODEL, 2 * D_EXPERT), F32),
            pltpu.VMEM((2, D_EXPERT, D_MODEL), F32),
            pltpu.VMEM((D_MODEL, 2 * D_EXPERT), BF16),
            pltpu.VMEM((D_EXPERT, D_MODEL), BF16),
            pltpu.SemaphoreType.DMA((2,)),
            pltpu.SemaphoreType.DMA((X_BUFFERS,)),
            pltpu.SemaphoreType.DMA((2,)),
        ],
    )
    return pl.pallas_call(
        functools.partial(_moe_kernel, li=li),
        grid_spec=grid_spec,
        out_shape=jax.ShapeDtypeStruct((N_SLOTS * ROW_SLABS, LANES), F32),
        compiler_params=_cparams(1),
        name="moe_experts",
    )(tables["block_expert"], tables["expert_order"], tables["n_valid"], xs, w_gu,
      b_gu.reshape(DEPTH, N_EXPERTS, 1, 2 * D_EXPERT), w_down, b_down.reshape(DEPTH, N_EXPERTS, 1, D_MODEL))


def _combine_kernel(start_ref, len_ref, off_ref, y_hbm, x1_ref, ri_ref, rf_ref, mod_ref,
                    lng_ref, lnb_ref, oc_ref, od_ref, *scratch):
    bufs, sem = scratch[:RT_PER_STEP], scratch[RT_PER_STEP]
    i = pl.program_id(0)
    n_steps = pl.num_programs(0)

    def fetch(tile, t):
        _tile_run_copies(tile, start_ref, len_ref, off_ref, y_hbm, bufs[t], sem.at[t], to_sorted=False)

    @pl.when(i == 0)
    def _():
        for t in range(FETCH_AHEAD):
            fetch(t, t)

    m = mod_ref[pl.ds(_mod_row(i * RT_PER_STEP // RT_PER_TM), 1), :]
    g2 = m[:, 5 * D_MODEL:6 * D_MODEL]
    col = lax.broadcasted_iota(I32, (RT, RT_ROWS), 1).astype(F32)
    outs = []
    for t in range(RT_PER_STEP):
        rows = slice(t * RT, (t + 1) * RT)
        fetch(jnp.minimum(i * RT_PER_STEP + t + FETCH_AHEAD, N_RT - 1), (t + FETCH_AHEAD) % RT_PER_STEP)
        _tile_runs_wait(bufs[t], sem.at[t])
        pos = _packed_positions(ri_ref[rows, :])
        gates = rf_ref[rows, :]
        mix = jnp.zeros((RT, RT_ROWS), F32)
        for k in range(TOP_K):
            mix = mix + jnp.where(col == pos[k], gates[:, k:k + 1], 0.0)
        ffn = jnp.dot(mix.astype(BF16), _slab_columns(bufs[t], 0, RT_ROWS).astype(BF16),
                      preferred_element_type=F32)
        outs.append(_layer_norm_rows(DN_ALPHA * x1_ref[rows, :] + g2 * ffn, lng_ref[...], lnb_ref[...]))
    out = jnp.concatenate(outs, axis=0)
    n_ctx_steps = N_CTX // (RT * RT_PER_STEP)

    @pl.when(i < n_ctx_steps)
    def _():
        oc_ref[...] = out

    @pl.when(i >= n_ctx_steps)
    def _():
        od_ref[...] = out

    @pl.when(i == n_steps - 1)
    def _():
        for t in range(FETCH_AHEAD):
            _tile_runs_wait(bufs[t], sem.at[t])


def _combine(tables, y, x1, route_i, route_f, mod, ln_g, ln_b, li):
    rows = RT * RT_PER_STEP
    grid_spec = pltpu.PrefetchScalarGridSpec(
        num_scalar_prefetch=3,
        grid=(N_RT // RT_PER_STEP,),
        in_specs=[
            pl.BlockSpec(memory_space=pl.ANY),
            pl.BlockSpec((rows, D_MODEL), lambda i, *_: (i, 0)),
            pl.BlockSpec((rows, LANES), lambda i, *_: (i, 0)),
            pl.BlockSpec((rows, LANES), lambda i, *_: (i, 0)),
            pl.BlockSpec((None, SUBLANES, 6 * D_MODEL), lambda i, *_: (li, 0, 0)),
            pl.BlockSpec((1, D_MODEL), lambda i, *_: (0, 0)),
            pl.BlockSpec((1, D_MODEL), lambda i, *_: (0, 0)),
        ],
        out_specs=_pair_specs(rows, D_MODEL, N_CTX // rows),
        scratch_shapes=[pltpu.VMEM((RT_ROWS * ROW_SLABS, LANES), F32)] * RT_PER_STEP + [
            pltpu.SemaphoreType.DMA((RT_PER_STEP,)),
        ],
    )
    return pl.pallas_call(
        _combine_kernel,
        grid_spec=grid_spec,
        out_shape=[jax.ShapeDtypeStruct((N_CTX, D_MODEL), F32), jax.ShapeDtypeStruct((N_DEC, D_MODEL), F32)],
        compiler_params=_cparams(1),
        name="moe_combine",
    )(tables["start"], tables["len"], tables["off"], y, x1, route_i, route_f, mod, ln_g, ln_b)


def _routing_tables(tile_cnt):
    cnt = tile_cnt[:, 0, :N_EXPERTS]
    totals = jnp.sum(cnt, axis=0)
    padded = (totals + BM - 1) // BM * BM
    pends = jnp.cumsum(padded)
    pstarts = pends - padded
    start = pstarts[None, :] + jnp.cumsum(cnt, axis=0) - cnt
    off = jnp.cumsum(cnt, axis=1) - cnt
    n_valid = (pends[-1] // BM).astype(I32)
    block_start = jnp.arange(NB, dtype=I32) * BM
    block_e = jnp.minimum(jnp.sum(block_start[:, None] >= pends[None, :], axis=1), N_EXPERTS - 1)
    last_e = block_e[jnp.maximum(n_valid - 1, 0)]
    block_e = jnp.where(jnp.arange(NB) < n_valid, block_e, last_e)
    ids = jnp.arange(N_EXPERTS, dtype=I32)
    order = jnp.sort(jnp.where(totals > 0, ids, N_EXPERTS))
    order = jnp.concatenate([jnp.where(order < N_EXPERTS, order, -1), jnp.full((2,), -1, I32)])
    return {
        "start": (start.reshape(-1) * ROW_SLABS).astype(I32),
        "len": (cnt.reshape(-1) * ROW_SLABS).astype(I32),
        "off": (off.reshape(-1) * ROW_SLABS).astype(I32),
        "pad_start": ((pstarts + totals) * ROW_SLABS).astype(I32),
        "pad_len": ((padded - totals) * ROW_SLABS).astype(I32),
        "n_valid": n_valid.reshape(1),
        "block_expert": block_e.astype(I32),
        "expert_order": order.astype(I32),
    }


def _layer_params(p, li):
    eye = jnp.eye(N_LRU_BLOCKS, dtype=F32)
    w_rg_full = jnp.einsum("dkgio,gh->gidkho", p["w_rg"][li], eye).reshape(LRU_WIDTH, 4 * LRU_WIDTH)
    pad = LANES - N_EXPERTS
    return {
        "sgu_ln_g": p["sgu_ln_g"][li].reshape(1, SGU_WIDTH),
        "sgu_ln_b": p["sgu_ln_b"][li].reshape(1, SGU_WIDTH),
        "w_spatial": p["w_spatial"][li],
        "b_spatial_full": jnp.repeat(p["b_spatial"][li].T, SGU_GROUP, axis=1),
        "conv_w": p["conv_w"][li],
        "conv_b": p["conv_b"][li].reshape(1, LRU_WIDTH),
        "w_rg_full": w_rg_full,
        "b_rg_full": p["b_rg"][li].reshape(1, 4 * LRU_WIDTH),
        "lru_log": p["lru_log"][li],
        "w_out_all": p["w_out"],
        "ln_g1": p["ln_g"][li, 0].reshape(1, D_MODEL),
        "ln_b1": p["ln_b"][li, 0].reshape(1, D_MODEL),
        "ln_g2": p["ln_g"][li, 1].reshape(1, D_MODEL),
        "ln_b2": p["ln_b"][li, 1].reshape(1, D_MODEL),
        "w_router_pad": jnp.pad(p["w_router"][li], ((0, 0), (0, pad))),
        "b_router_pad": jnp.pad(p["b_router"][li], (0, pad)).reshape(1, LANES),
    }


def kernel(x_prompt, x_sample, cache_k, cache_v, state_lru, c, c_ctx, w_mod, b_mod, w_in, lam, subln_g, sgu_ln_g, sgu_ln_b, w_spatial, b_spatial, conv_w, conv_b, w_rg, b_rg, lru_log, w_out, ln_g, ln_b, w_router, b_router, w_gu, b_gu, w_down, b_down):
    p = dict(sgu_ln_g=sgu_ln_g, sgu_ln_b=sgu_ln_b, w_spatial=w_spatial, b_spatial=b_spatial, conv_w=conv_w,
             conv_b=conv_b, w_rg=w_rg, b_rg=b_rg, lru_log=lru_log, w_out=w_out, ln_g=ln_g, ln_b=ln_b,
             w_router=w_router, b_router=b_router)
    cvec8 = jnp.concatenate([c_ctx[None, :], c, jnp.zeros((SUBLANES - 1 - DEC_BATCH, D_MODEL), F32)], axis=0)
    mod = _modulation(cvec8, w_mod, b_mod)
    x_pair = (x_prompt.reshape(N_CTX, D_MODEL), x_sample.reshape(N_DEC, D_MODEL))
    tables = _rope_tables()
    zero_state = jnp.zeros((BATCH, 1, 2, LRU_WIDTH), F32)
    prev_kv = None
    new_s = []
    for li in range(DEPTH):
        lp = _layer_params(p, li)
        lam_init = 0.8 - 0.6 * math.exp(-0.3 * li)
        proj = _input_projection(*x_pair, mod, w_in, li)
        att_ctx, kc, vc = _context_attention(proj, lam[li], subln_g[li], lam_init, prev_kv)
        prev_kv = (kc, vc)
        att_dec = _denoise_attention(proj, cache_k, cache_v, lam[li], subln_g[li], li, lam_init, tables)
        sgu_ctx, lru_ctx, h_ctx = _mixers(proj, zero_state, 0, lp, SEQ, BATCH, 0)
        sgu_dec, lru_dec, _ = _mixers(proj, state_lru, li, lp, DEC_SEQ, DEC_BATCH, N_CTX // DEC_SEQ)
        x1, h2, route_i, route_f, tile_cnt = _output_projection(
            x_pair, (att_ctx, att_dec), (sgu_ctx, sgu_dec), (lru_ctx, lru_dec), mod, lp, li)
        rt = _routing_tables(tile_cnt)
        xs = _dispatch(rt, h2, route_i)
        y = _moe_blocks(rt, xs, w_gu, b_gu, w_down, b_down, li)
        x_pair = _combine(rt, y, x1, route_i, route_f, mod, lp["ln_g2"], lp["ln_b2"], li)
        new_s.append(h_ctx)
    y_prompt = x_pair[0].reshape(BATCH, SEQ, D_MODEL)
    y_sample = x_pair[1].reshape(DEC_BATCH, DEC_SEQ, D_MODEL)
    return (y_prompt, y_sample, prev_kv[0], prev_kv[1], jnp.stack(new_s, axis=1))
```

```python
import functools
import math

import numpy as np
import jax
import jax.numpy as jnp
from jax import lax
from jax.experimental import pallas as pl
from jax.experimental.pallas import tpu as pltpu

F32 = jnp.float32
BF16 = jnp.bfloat16
I32 = jnp.int32

D_MODEL = 1024
BATCH = 32
SEQ = 256
DEPTH = 2
DEC_BATCH = 2
DEC_SEQ = 1024
PAST_LEN = 256
GRID_W = 64
HEAD_DIM = 64
ATT_WIDTH = D_MODEL // 2
SGU_WIDTH = D_MODEL // 4
LRU_WIDTH = D_MODEL // 4
N_ATT_HEADS = ATT_WIDTH // HEAD_DIM
ATT_HALF = HEAD_DIM // 2
ROPE_FREQS = ATT_HALF // 4
ROPE_THETA = 10000.0
CHUNK = 128
N_SGU_GROUPS = 4
SGU_GROUP = SGU_WIDTH // N_SGU_GROUPS
N_LRU_BLOCKS = 4
LRU_BLOCK = LRU_WIDTH // N_LRU_BLOCKS
CONV_W = 4
LRU_C = 8.0
IN_COLS = 3 * ATT_WIDTH + 2 * SGU_WIDTH + 2 * LRU_WIDTH
N_EXPERTS = 32
TOP_K = 4
D_EXPERT = D_MODEL
SWIGLU_LIMIT = 7.0
SWIGLU_ALPHA = 1.702
DN_ALPHA = (2 * DEPTH) ** 0.25
EPS = 1e-5

N_CTX = BATCH * SEQ
N_DEC = DEC_BATCH * DEC_SEQ
N_TOK = N_CTX + N_DEC

LANES = 128
SUBLANES = 8
ROW_SLABS = D_MODEL // LANES

TM = 512
CTX_TILES = N_CTX // TM
TILES_PER_DEC = DEC_SEQ // TM
N_TILES = N_TOK // TM
MOD_TN = 1024
RT = 256
N_RT = N_TOK // RT
RT_PER_TM = TM // RT
RT_ROWS = RT * TOP_K
RT_PER_STEP = 4
FETCH_AHEAD = 2
BM = 256
NB = N_TOK * TOP_K // BM + N_EXPERTS
WEIGHT_PIECES = 8
X_BUFFERS = 4
N_SLOTS = NB * BM
RUN_BITS = tuple(1 << b for b in range(RT.bit_length() - 1, -1, -1))
PAD_BITS = tuple(1 << b for b in range(BM.bit_length() - 2, -1, -1))
VMEM_LIMIT = 56 * 1024 * 1024


def _cparams(n_axes):
    return pltpu.CompilerParams(
        dimension_semantics=("arbitrary",) * n_axes,
        vmem_limit_bytes=VMEM_LIMIT)


def _mod_row(i):
    return jnp.where(i < CTX_TILES, 0, 1 + (i - CTX_TILES) // TILES_PER_DEC)


def _layer_norm_rows(z, g, b):
    mu = jnp.mean(z, axis=-1, keepdims=True)
    zc = z - mu
    var = jnp.mean(zc * zc, axis=-1, keepdims=True)
    return zc * lax.rsqrt(var + EPS) * g + b


def _pair_specs(tile, width, n_ctx_tiles):
    ctx = pl.BlockSpec((tile, width), lambda i, *_: (jnp.minimum(i, n_ctx_tiles - 1), 0))
    dec = pl.BlockSpec((tile, width), lambda i, *_: (jnp.maximum(i - n_ctx_tiles, 0), 0))
    return [ctx, dec]


def _pair_read(i, n_ctx_tiles, ctx_ref, dec_ref):
    return jnp.where(i < n_ctx_tiles, ctx_ref[...], dec_ref[...])


def _layer_block(li, shape, *lead):
    index = (li,) + tuple(lead) + (0,) * len(shape)
    return pl.BlockSpec((None,) * (1 + len(lead)) + tuple(shape), lambda *_: index)


def _mod_kernel(cvec_ref, w_ref, b_ref, o_ref):
    cv = cvec_ref[...]
    s = cv * jax.nn.sigmoid(cv)
    s_t = s.T
    w = w_ref[0]
    rows = [jnp.sum(s_t[:, r:r + 1] * w, axis=0, keepdims=True) for r in range(1 + DEC_BATCH)]
    rows.append(jnp.zeros((SUBLANES - 1 - DEC_BATCH, MOD_TN), F32))
    o_ref[0] = jnp.concatenate(rows, axis=0) + b_ref[0]


def _modulation(cvec8, w_mod, b_mod):
    n_out = w_mod.shape[-1]
    return pl.pallas_call(
        _mod_kernel,
        grid=(DEPTH, n_out // MOD_TN),
        in_specs=[
            pl.BlockSpec((SUBLANES, D_MODEL), lambda l, j: (0, 0)),
            pl.BlockSpec((1, D_MODEL, MOD_TN), lambda l, j: (l, 0, j)),
            pl.BlockSpec((1, 1, MOD_TN), lambda l, j: (l, 0, j)),
        ],
        out_specs=pl.BlockSpec((1, SUBLANES, MOD_TN), lambda l, j: (l, 0, j)),
        out_shape=jax.ShapeDtypeStruct((DEPTH, SUBLANES, n_out), F32),
        compiler_params=_cparams(2),
        name="modulation",
    )(cvec8, w_mod, b_mod.reshape(DEPTH, 1, n_out))


def _inproj_kernel(xc_ref, xd_ref, mod_ref, w_ref, o_ref, wbf_ref):
    i = pl.program_id(0)

    @pl.when(i == 0)
    def _():
        wbf_ref[...] = w_ref[...].astype(BF16)

    m = mod_ref[pl.ds(_mod_row(i), 1), :]
    sh1 = m[:, 0:D_MODEL]
    sc1 = m[:, D_MODEL:2 * D_MODEL]
    x = _pair_read(i, CTX_TILES, xc_ref, xd_ref)
    h = (x * (1.0 + sc1) + sh1).astype(BF16)
    o_ref[...] = jnp.dot(h, wbf_ref[...], preferred_element_type=F32)


def _input_projection(x_ctx, x_dec, mod, w_in, li):
    return pl.pallas_call(
        _inproj_kernel,
        grid=(N_TILES,),
        in_specs=_pair_specs(TM, D_MODEL, CTX_TILES) + [
            pl.BlockSpec((None, SUBLANES, 6 * D_MODEL), lambda i: (li, 0, 0)),
            pl.BlockSpec((None, D_MODEL, IN_COLS), lambda i: (li, 0, 0), pipeline_mode=pl.Buffered(1)),
        ],
        out_specs=pl.BlockSpec((TM, IN_COLS), lambda i: (i, 0)),
        out_shape=jax.ShapeDtypeStruct((N_TOK, IN_COLS), F32),
        scratch_shapes=[pltpu.VMEM((D_MODEL, IN_COLS), BF16)],
        compiler_params=_cparams(1),
        name="input_projection",
    )(x_ctx, x_dec, mod, w_in)


def _lambda(lam_ref, lam_init):
    lm = lam_ref[...]
    a = jnp.sum(lm[0:1] * lm[1:2], axis=-1, keepdims=True)
    b = jnp.sum(lm[2:3] * lm[3:4], axis=-1, keepdims=True)
    return jnp.exp(a) - jnp.exp(b) + lam_init


_NT = (((1,), (1,)), ((), ()))
_TN = (((0,), (0,)), ((), ()))
_ATT_SCALE = ATT_HALF ** -0.5
_Q_SCALE = _ATT_SCALE * math.log2(math.e)
KEY_CHUNK = 128


def _attention_heads(q_parts, k_segs, values, s_scr, e_scr, lam, g_col, lam_init):
    n_heads = len(q_parts)
    n_keys, n_q = s_scr.shape[1:]
    tops = []
    for h in range(n_heads):
        for c in range(2):
            row = 0
            seg_tops = []
            for k in k_segs[h][c]:
                s = lax.dot_general(k, q_parts[h][c], _NT, preferred_element_type=F32)
                s_scr[2 * h + c, row:row + k.shape[0], :] = s
                seg_tops.append(jnp.max(s.reshape(k.shape[0] // SUBLANES, SUBLANES, n_q), axis=0))
                row += k.shape[0]
            tops.append(jnp.max(functools.reduce(jnp.maximum, seg_tops), axis=0, keepdims=True))
    chunks = [slice(r, r + KEY_CHUNK) for r in range(0, n_keys, KEY_CHUNK)]
    for n in range(2 * n_heads):
        for rows in chunks:
            e_scr[n, rows, :] = jnp.exp2(s_scr[n, rows, :] - tops[n]).astype(BF16)
    outs = []
    for h in range(n_heads):
        v_aug = jnp.concatenate([values[h], jnp.ones_like(values[h])], axis=-1)
        normed = []
        for c in range(2):
            acc = lax.dot_general(v_aug, e_scr[2 * h + c], _TN, preferred_element_type=F32)
            normed.append(acc[:HEAD_DIM] * (1.0 / acc[HEAD_DIM:HEAD_DIM + 1]))
        o_t = normed[0] - lam * normed[1]
        ms = jnp.mean(o_t * o_t, axis=0, keepdims=True)
        outs.append(o_t * lax.rsqrt(ms + EPS) * g_col * (1.0 - lam_init))
    return outs


def _head_cols(h):
    lo = h * HEAD_DIM
    return [slice(lo + c * ATT_HALF, lo + (c + 1) * ATT_HALF) for c in range(2)]


def _ctx_attn_kernel(lam_ref, g_ref, q_ref, k_ref, v_ref, *rest, lam_init, stacked):
    *rest, s_scr, e_scr = rest
    if stacked:
        pk_ref, pv_ref, att_ref, ck_ref, cv_ref = rest
        ck_ref[0, 0] = pk_ref[0]
        cv_ref[0, 0] = pv_ref[0]
        put_k = lambda h, val: ck_ref.__setitem__((0, 1, h), val)
        put_v = lambda h, val: cv_ref.__setitem__((0, 1, h), val)
    else:
        att_ref, ck_ref, cv_ref = rest
        put_k = lambda h, val: ck_ref.__setitem__((0, h), val)
        put_v = lambda h, val: cv_ref.__setitem__((0, h), val)
    lam = _lambda(lam_ref, lam_init)
    k = k_ref[...]
    v = v_ref[...]
    qb = (q_ref[...] * _Q_SCALE).astype(BF16)
    kb = k.astype(BF16)
    vb = v.astype(BF16)
    heads = range(N_ATT_HEADS)
    outs = _attention_heads([[qb[:, cs] for cs in _head_cols(h)] for h in heads],
                            [[[kb[:, cs]] for cs in _head_cols(h)] for h in heads],
                            [vb[:, h * HEAD_DIM:(h + 1) * HEAD_DIM] for h in heads],
                            s_scr, e_scr, lam, g_ref[...], lam_init)
    for h in heads:
        put_k(h, k[:, h * HEAD_DIM:(h + 1) * HEAD_DIM])
        put_v(h, v[:, h * HEAD_DIM:(h + 1) * HEAD_DIM])
    att_ref[...] = jnp.concatenate(outs, axis=0).T


def _context_attention(proj, lam, subln_col, li, lam_init, prev_kv=None):
    stacked = prev_kv is not None
    per_layer = pl.BlockSpec((1, N_ATT_HEADS, SEQ, HEAD_DIM), lambda b: (b, 0, 0, 0))
    if stacked:
        kv_shape = jax.ShapeDtypeStruct((BATCH, DEPTH, N_ATT_HEADS, SEQ, HEAD_DIM), F32)
        kv_spec = pl.BlockSpec((1, DEPTH, N_ATT_HEADS, SEQ, HEAD_DIM), lambda b: (b, 0, 0, 0, 0))
    else:
        kv_shape = jax.ShapeDtypeStruct((BATCH, N_ATT_HEADS, SEQ, HEAD_DIM), F32)
        kv_spec = per_layer
    return pl.pallas_call(
        functools.partial(_ctx_attn_kernel, lam_init=lam_init, stacked=stacked),
        grid=(BATCH,),
        in_specs=[
            _layer_block(li, (4, ATT_HALF)),
            _layer_block(li, (HEAD_DIM, 1)),
            pl.BlockSpec((SEQ, ATT_WIDTH), lambda b: (b, 0)),
            pl.BlockSpec((SEQ, ATT_WIDTH), lambda b: (b, 1)),
            pl.BlockSpec((SEQ, ATT_WIDTH), lambda b: (b, 2)),
        ] + ([per_layer, per_layer] if stacked else []),
        out_specs=[pl.BlockSpec((SEQ, ATT_WIDTH), lambda b: (b, 0)), kv_spec, kv_spec],
        out_shape=[jax.ShapeDtypeStruct((N_CTX, ATT_WIDTH), F32), kv_shape, kv_shape],
        scratch_shapes=[pltpu.VMEM((2 * N_ATT_HEADS, SEQ, SEQ), F32),
                        pltpu.VMEM((2 * N_ATT_HEADS, SEQ, SEQ), BF16)],
        compiler_params=_cparams(1),
        name="context_attention",
    )(lam, subln_col, proj, proj, proj, *(prev_kv if stacked else ()))


def _rope_tables():
    t = np.arange(DEC_SEQ)
    pos = np.stack([t // GRID_W, t % GRID_W], axis=1).astype(np.float32)
    inv = (np.float32(ROPE_THETA) ** (-np.arange(ROPE_FREQS, dtype=np.float32) / np.float32(ROPE_FREQS)))
    j = np.arange(HEAD_DIM)
    d = j % ATT_HALF
    axis = d // (2 * ROPE_FREQS)
    u = d % (2 * ROPE_FREQS)
    ang = pos[:, axis] * inv[u % ROPE_FREQS][None, :].astype(np.float32)
    cos = np.cos(ang).astype(np.float32)
    sin = np.sin(ang).astype(np.float32)
    first = (u < ROPE_FREQS)[None, :]
    s_next = np.where(first, -sin, 0.0).astype(np.float32)
    s_prev = np.where(first, 0.0, sin).astype(np.float32)
    tile = lambda a: jnp.asarray(np.tile(a, (1, N_ATT_HEADS)))
    return tile(cos), tile(s_next), tile(s_prev)


def _rotate(x, cos, s_next, s_prev):
    width = x.shape[-1]
    return (x * cos + pltpu.roll(x, width - ROPE_FREQS, axis=1) * s_next
            + pltpu.roll(x, ROPE_FREQS, axis=1) * s_prev)


QB = 256
Q_STEPS = DEC_SEQ // QB
DEC_HEAD_GROUP = 2


def _dec_attn_kernel(lam_ref, g_ref, q_ref, k_ref, v_ref, ck_ref, cv_ref,
                     cq_ref, snq_ref, spq_ref, ck_tab, snk_tab, spk_tab,
                     att_ref, krot_ref, s_scr, e_scr, *, lam_init):
    j = pl.program_id(1)

    @pl.when(j == 0)
    def _():
        krot_ref[...] = _rotate(k_ref[...], ck_tab[...], snk_tab[...], spk_tab[...]).astype(BF16)

    lam = _lambda(lam_ref, lam_init)
    qb = (_rotate(q_ref[...], cq_ref[...], snq_ref[...], spq_ref[...]) * _Q_SCALE).astype(BF16)
    kb = krot_ref[...]
    vb = v_ref[...].astype(BF16)
    outs = []
    for h0 in range(0, N_ATT_HEADS, DEC_HEAD_GROUP):
        heads = range(h0, h0 + DEC_HEAD_GROUP)
        past_k = [ck_ref[0, 0, h].astype(BF16) for h in heads]
        k_segs = [[[pk[:, c * ATT_HALF:(c + 1) * ATT_HALF], kb[:, _head_cols(h)[c]]] for c in range(2)]
                  for h, pk in zip(heads, past_k)]
        values = [jnp.concatenate([cv_ref[0, 0, h].astype(BF16), vb[:, h * HEAD_DIM:(h + 1) * HEAD_DIM]], axis=0)
                  for h in heads]
        outs += _attention_heads([[qb[:, cs] for cs in _head_cols(h)] for h in heads], k_segs, values,
                                 s_scr, e_scr, lam, g_ref[...], lam_init)
    att_ref[...] = jnp.concatenate(outs, axis=0).T


def _denoise_attention(proj, cache_k, cache_v, lam, subln_col, li, lam_init, tables):
    cos, s_next, s_prev = tables
    row0 = N_CTX // QB
    seq0 = N_CTX // DEC_SEQ
    q_tab = pl.BlockSpec((QB, ATT_WIDTH), lambda b, j: (j, 0))
    k_tab = pl.BlockSpec((DEC_SEQ, ATT_WIDTH), lambda b, j: (0, 0))
    cache_spec = pl.BlockSpec((1, 1, N_ATT_HEADS, PAST_LEN, HEAD_DIM), lambda b, j: (b, li, 0, 0, 0))
    return pl.pallas_call(
        functools.partial(_dec_attn_kernel, lam_init=lam_init),
        grid=(DEC_BATCH, Q_STEPS),
        in_specs=[
            _layer_block(li, (4, ATT_HALF)),
            _layer_block(li, (HEAD_DIM, 1)),
            pl.BlockSpec((QB, ATT_WIDTH), lambda b, j: (row0 + b * Q_STEPS + j, 0)),
            pl.BlockSpec((DEC_SEQ, ATT_WIDTH), lambda b, j: (seq0 + b, 1)),
            pl.BlockSpec((DEC_SEQ, ATT_WIDTH), lambda b, j: (seq0 + b, 2)),
            cache_spec, cache_spec,
            q_tab, q_tab, q_tab, k_tab, k_tab, k_tab,
        ],
        out_specs=pl.BlockSpec((QB, ATT_WIDTH), lambda b, j: (b * Q_STEPS + j, 0)),
        out_shape=jax.ShapeDtypeStruct((N_DEC, ATT_WIDTH), F32),
        scratch_shapes=[pltpu.VMEM((DEC_SEQ, ATT_WIDTH), BF16),
                        pltpu.VMEM((2 * DEC_HEAD_GROUP, PAST_LEN + DEC_SEQ, QB), F32),
                        pltpu.VMEM((2 * DEC_HEAD_GROUP, PAST_LEN + DEC_SEQ, QB), BF16)],
        compiler_params=_cparams(2),
        name="denoise_attention",
    )(lam, subln_col, proj, proj, proj, cache_k, cache_v,
      cos, s_next, s_prev, cos, s_next, s_prev)


def _softplus(z):
    return jnp.maximum(z, 0.0) + jnp.log1p(jnp.exp(-jnp.abs(z)))


def _mixer_kernel(su_ref, sv_ref, rx_ref, rg_ref, h0_ref, lng_ref, lnb_ref, ws_ref, bs_ref,
                  cw_ref, cb_ref, wrg_ref, brg_ref, lrulog_ref,
                  sgu_ref, lru_ref, hlast_ref, a_scr, b_scr, h_scr, *, seq_len):
    vn = _layer_norm_rows(sv_ref[...], lng_ref[...], lnb_ref[...])
    lane_group = lax.broadcasted_iota(I32, (CHUNK, SGU_WIDTH), 1) // SGU_GROUP
    for n in range(seq_len // CHUNK):
        rows = slice(n * CHUNK, (n + 1) * CHUNK)
        vc = vn[rows].astype(BF16)
        s = jnp.zeros((CHUNK, SGU_WIDTH), F32)
        for g in range(N_SGU_GROUPS):
            sg = jnp.dot(ws_ref[g].astype(BF16), vc, preferred_element_type=F32)
            s = jnp.where(lane_group == g, sg, s)
        sgu_ref[rows, :] = su_ref[rows, :] * (s + bs_ref[...])

    x = rx_ref[...]
    row = lax.broadcasted_iota(I32, (seq_len, LRU_WIDTH), 0)

    def shifted(val, d, fill):
        rolled = pltpu.roll(val, d % seq_len, axis=0)
        inside = (row >= d) if d > 0 else (row < seq_len + d)
        return jnp.where(inside, rolled, fill)

    left = CONV_W // 2
    xc = cb_ref[...] + x * cw_ref[left:left + 1, :]
    for tap in range(CONV_W):
        if tap != left:
            xc = xc + shifted(x, left - tap, 0.0) * cw_ref[tap:tap + 1, :]
    pre = jnp.dot(xc.astype(BF16), wrg_ref[...].astype(BF16), preferred_element_type=F32) + brg_ref[...]
    gates = 0.5 + 0.5 * jnp.tanh(0.5 * pre)
    in_chunk = row % SUBLANES
    n_chunks = seq_len // SUBLANES
    h0 = h0_ref[0, 0]
    lasts = []
    for direction in range(2):
        reverse = direction == 1
        base = direction * 2 * LRU_WIDTH
        r = gates[:, base:base + LRU_WIDTH]
        gi = gates[:, base + LRU_WIDTH:base + 2 * LRU_WIDTH]
        log_a = -LRU_C * r * _softplus(-lrulog_ref[direction:direction + 1, :])
        a = jnp.exp(log_a)
        b = jnp.sqrt(-jnp.tanh(log_a) * (a * a + 1.0)) * gi * xc
        def chunk_roll(val, shift):
            chunks = val.reshape(n_chunks, SUBLANES, LRU_WIDTH)
            return pltpu.roll(chunks, shift, axis=1).reshape(seq_len, LRU_WIDTH)

        for d in (1, 2, 4):
            if reverse:
                inside = in_chunk < SUBLANES - d
                a_n = jnp.where(inside, chunk_roll(a, SUBLANES - d), 1.0)
                b_n = jnp.where(inside, chunk_roll(b, SUBLANES - d), 0.0)
            else:
                inside = in_chunk >= d
                a_n = jnp.where(inside, chunk_roll(a, d), 1.0)
                b_n = jnp.where(inside, chunk_roll(b, d), 0.0)
            b = a * b_n + b
            a = a * a_n
        a_scr[...] = a
        b_scr[...] = b

        def chunk_step(c, carry, reverse=reverse):
            cc = n_chunks - 1 - c if reverse else c
            off = pl.multiple_of(cc * SUBLANES, SUBLANES)
            hc = a_scr[pl.ds(off, SUBLANES), :] * carry + b_scr[pl.ds(off, SUBLANES), :]
            if reverse:
                h_scr[pl.ds(off, SUBLANES), :] = h_scr[pl.ds(off, SUBLANES), :] + hc
                return hc[0:1, :]
            h_scr[pl.ds(off, SUBLANES), :] = hc
            return hc[SUBLANES - 1:SUBLANES, :]

        lasts.append(lax.fori_loop(0, n_chunks, chunk_step, h0[direction:direction + 1, :]))
    lru_ref[...] = h_scr[...] * jax.nn.gelu(rg_ref[...])
    hlast_ref[0] = jnp.concatenate(lasts, axis=0)


def _mixers(proj, h0, h0_layer, lp, li, seq_len, n_seq, row_block0):
    col0 = 3 * ATT_WIDTH // SGU_WIDTH
    col = lambda c: pl.BlockSpec((seq_len, SGU_WIDTH), lambda b: (row_block0 + b, col0 + c))
    full = lambda shape: _layer_block(li, shape)
    out_rows = pl.BlockSpec((seq_len, SGU_WIDTH), lambda b: (b, 0))
    return pl.pallas_call(
        functools.partial(_mixer_kernel, seq_len=seq_len),
        grid=(n_seq,),
        in_specs=[
            col(0), col(1), col(2), col(3),
            pl.BlockSpec((1, 1, 2, LRU_WIDTH), lambda b: (b, h0_layer, 0, 0)),
            full((1, SGU_WIDTH)), full((1, SGU_WIDTH)),
            full((N_SGU_GROUPS, CHUNK, CHUNK)), full((CHUNK, SGU_WIDTH)),
            full((CONV_W, LRU_WIDTH)), full((1, LRU_WIDTH)),
            full((LRU_WIDTH, 4 * LRU_WIDTH)), full((1, 4 * LRU_WIDTH)),
            full((2, LRU_WIDTH)),
        ],
        out_specs=[out_rows, out_rows, pl.BlockSpec((1, 2, LRU_WIDTH), lambda b: (b, 0, 0))],
        out_shape=[
            jax.ShapeDtypeStruct((n_seq * seq_len, SGU_WIDTH), F32),
            jax.ShapeDtypeStruct((n_seq * seq_len, LRU_WIDTH), F32),
            jax.ShapeDtypeStruct((n_seq, 2, LRU_WIDTH), F32),
        ],
        scratch_shapes=[pltpu.VMEM((seq_len, LRU_WIDTH), F32)] * 3,
        compiler_params=_cparams(1),
        name="mixers_%d" % seq_len,
    )(proj, proj, proj, proj, h0, lp["sgu_ln_g"], lp["sgu_ln_b"], lp["w_spatial"], lp["b_spatial_full"],
      lp["conv_w"], lp["conv_b"], lp["w_rg_full"], lp["b_rg_full"], lp["lru_log"])


def _outproj_kernel(xc_ref, xd_ref, ac_ref, ad_ref, sc_ref, sd_ref, lc_ref, ld_ref,
                    mod_ref, wout_ref, lng_ref, lnb_ref, wr_ref, br_ref,
                    x1_ref, h2_ref, ri_ref, rf_ref, cnt_ref, wbf_ref):
    i = pl.program_id(0)

    @pl.when(i == 0)
    def _():
        wbf_ref[...] = wout_ref[...].astype(BF16)

    m = mod_ref[pl.ds(_mod_row(i), 1), :]
    g1 = m[:, 2 * D_MODEL:3 * D_MODEL]
    sh2 = m[:, 3 * D_MODEL:4 * D_MODEL]
    sc2 = m[:, 4 * D_MODEL:5 * D_MODEL]
    a0, a1 = ATT_WIDTH, ATT_WIDTH + SGU_WIDTH
    x = _pair_read(i, CTX_TILES, xc_ref, xd_ref)
    att = _pair_read(i, CTX_TILES, ac_ref, ad_ref).astype(BF16)
    sgu = _pair_read(i, CTX_TILES, sc_ref, sd_ref).astype(BF16)
    lru = _pair_read(i, CTX_TILES, lc_ref, ld_ref).astype(BF16)
    mix = (jnp.dot(att, wbf_ref[0:a0, :], preferred_element_type=F32)
           + jnp.dot(sgu, wbf_ref[a0:a1, :], preferred_element_type=F32)
           + jnp.dot(lru, wbf_ref[a1:, :], preferred_element_type=F32))
    x1 = _layer_norm_rows(DN_ALPHA * x + g1 * mix, lng_ref[...], lnb_ref[...])
    x1_ref[...] = x1
    h2 = x1 * (1.0 + sc2) + sh2
    h2_ref[...] = h2.astype(BF16)

    logits = jnp.dot(h2.astype(BF16), wr_ref[...].astype(BF16), preferred_element_type=F32) + br_ref[...]
    lane = lax.broadcasted_iota(I32, (TM, LANES), 1)
    lane_f = lane.astype(F32)
    neg_inf = jnp.float32(-jnp.inf)
    work = jnp.where(lane < N_EXPERTS, logits, neg_inf)
    vals, idxs = [], []
    for _ in range(TOP_K):
        top = jnp.max(work, axis=-1, keepdims=True)
        idx = jnp.min(jnp.where(work == top, lane_f, float(LANES)), axis=-1, keepdims=True)
        vals.append(top)
        idxs.append(idx)
        work = jnp.where(lane_f == idx, neg_inf, work)
    exps = [jnp.exp(v - vals[0]) for v in vals]
    denom = exps[0] + exps[1] + exps[2] + exps[3]
    onehot = jnp.zeros((TM, LANES), F32)
    for idx in idxs:
        onehot = onehot + (lane_f == idx).astype(F32)
    r_i = lax.broadcasted_iota(I32, (RT, RT), 0)
    c_i = lax.broadcasted_iota(I32, (RT, RT), 1)
    tri = (r_i > c_i).astype(F32).astype(BF16)
    upper = (lax.broadcasted_iota(I32, (LANES, LANES), 0)
             < lax.broadcasted_iota(I32, (LANES, LANES), 1)).astype(F32).astype(BF16)
    packed_pos = []
    for t in range(RT_PER_TM):
        hot = onehot[t * RT:(t + 1) * RT]
        total = jnp.broadcast_to(jnp.sum(hot, axis=0, keepdims=True), (SUBLANES, LANES))
        cnt_ref[t] = total.astype(I32)
        run_start = jnp.dot(total.astype(BF16), upper, preferred_element_type=F32)[0:1, :]
        packed_pos.append(jnp.dot(tri, hot.astype(BF16), preferred_element_type=F32) + run_start)
    packed_pos = jnp.concatenate(packed_pos, axis=0)
    ri = jnp.zeros((TM, LANES), F32)
    rf = jnp.zeros((TM, LANES), F32)
    for k in range(TOP_K):
        pos = jnp.sum(jnp.where(lane_f == idxs[k], packed_pos, 0.0), axis=-1, keepdims=True)
        ri = jnp.where(lane == k, idxs[k], ri)
        ri = jnp.where(lane == TOP_K + k, pos, ri)
        rf = jnp.where(lane == k, exps[k] / denom, rf)
    ri_ref[...] = ri.astype(I32)
    rf_ref[...] = rf


def _output_projection(x_pair, att_pair, sgu_pair, lru_pair, mod, lp, li):
    rows = lambda w: pl.BlockSpec((TM, w), lambda i: (i, 0))
    full = lambda shape: _layer_block(li, shape)
    return pl.pallas_call(
        _outproj_kernel,
        grid=(N_TILES,),
        in_specs=(_pair_specs(TM, D_MODEL, CTX_TILES) + _pair_specs(TM, ATT_WIDTH, CTX_TILES)
                  + _pair_specs(TM, SGU_WIDTH, CTX_TILES) + _pair_specs(TM, LRU_WIDTH, CTX_TILES) + [
            pl.BlockSpec((None, SUBLANES, 6 * D_MODEL), lambda i: (li, 0, 0)),
            pl.BlockSpec((None, D_MODEL, D_MODEL), lambda i: (li, 0, 0), pipeline_mode=pl.Buffered(1)),
            _layer_block(li, (1, D_MODEL), 0), _layer_block(li, (1, D_MODEL), 0),
            full((D_MODEL, LANES)), full((1, LANES)),
        ]),
        out_specs=[
            rows(D_MODEL), rows(D_MODEL), rows(LANES), rows(LANES),
            pl.BlockSpec((RT_PER_TM, SUBLANES, LANES), lambda i: (i, 0, 0)),
        ],
        out_shape=[
            jax.ShapeDtypeStruct((N_TOK, D_MODEL), F32),
            jax.ShapeDtypeStruct((N_TOK, D_MODEL), BF16),
            jax.ShapeDtypeStruct((N_TOK, LANES), I32),
            jax.ShapeDtypeStruct((N_TOK, LANES), F32),
            jax.ShapeDtypeStruct((N_RT, SUBLANES, LANES), I32),
        ],
        scratch_shapes=[pltpu.VMEM((D_MODEL, D_MODEL), BF16)],
        compiler_params=_cparams(1),
        name="output_projection",
    )(*x_pair, *att_pair, *sgu_pair, *lru_pair, mod, lp["w_out_all"], lp["ln_g"], lp["ln_b"],
      lp["w_router_pad"], lp["b_router_pad"])


def _row_ds(row, n_rows):
    return pl.ds(pl.multiple_of(row * ROW_SLABS, ROW_SLABS), n_rows * ROW_SLABS)


def _for_each_run_piece(length_s, bits, fn):
    for bit in bits:
        done_s = length_s & (-2 * bit * ROW_SLABS)
        @pl.when((length_s & (bit * ROW_SLABS)) != 0)
        def _(done_s=done_s, bit=bit):
            fn(done_s, bit)


def _stored_ds(row_s, n_rows):
    return pl.ds(pl.multiple_of(row_s, ROW_SLABS), n_rows * ROW_SLABS)


def _tile_run_copies(tile, start_ref, len_ref, off_ref, sorted_hbm, buf, sem, to_sorted, live=None):
    for e in range(N_EXPERTS):
        t = tile * N_EXPERTS + e
        start_s, length_s, off_s = start_ref[t], len_ref[t], off_ref[t]
        if live is not None:
            length_s = jnp.where(live, length_s, 0)

        def piece(done_s, bit, start_s=start_s, off_s=off_s):
            packed = buf.at[_stored_ds(off_s + done_s, bit), :]
            srt = sorted_hbm.at[_stored_ds(start_s + done_s, bit), :]
            if to_sorted:
                pltpu.make_async_copy(packed, srt, sem).start()
            else:
                pltpu.make_async_copy(srt, packed, sem).start()

        _for_each_run_piece(length_s, RUN_BITS, piece)


def _tile_runs_wait(buf, sem):
    pltpu.make_async_copy(buf, buf, sem).wait()


def _packed_positions(ri):
    return [ri[:, TOP_K + k:TOP_K + k + 1].astype(F32) for k in range(TOP_K)]


def _slab_columns(buf, row0, n_rows):
    return jnp.concatenate(
        [buf[pl.ds(row0 * ROW_SLABS + s, n_rows, stride=ROW_SLABS), :] for s in range(ROW_SLABS)], axis=-1)


def _dispatch_kernel(start_ref, len_ref, off_ref, pstart_ref, plen_ref, nv_ref,
                     h2_ref, ri_ref, xs_hbm, *scratch):
    bufs, (zbuf, sem, zsem) = scratch[:RT_PER_STEP], scratch[RT_PER_STEP:]
    i = pl.program_id(0)
    n_steps = pl.num_programs(0)

    @pl.when(i == 0)
    def _():
        zbuf[...] = jnp.zeros_like(zbuf)

        def zero_fill(wait):
            def go(cp):
                cp.wait() if wait else cp.start()

            def per_expert(e, carry):
                def piece(done_s, bit):
                    go(pltpu.make_async_copy(zbuf.at[pl.ds(0, bit * ROW_SLABS), :],
                                             xs_hbm.at[_stored_ds(pstart_ref[e] + done_s, bit), :], zsem))
                _for_each_run_piece(plen_ref[e], PAD_BITS, piece)
                return carry
            lax.fori_loop(0, N_EXPERTS, per_expert, 0)

            def per_block(b, carry):
                go(pltpu.make_async_copy(zbuf, xs_hbm.at[_row_ds(b * BM, BM), :], zsem))
                return carry
            lax.fori_loop(nv_ref[0], NB, per_block, 0)

        zero_fill(False)
        zero_fill(True)

    def send(tile, t, live):
        _tile_run_copies(tile, start_ref, len_ref, off_ref, xs_hbm, bufs[t], sem.at[t], to_sorted=True, live=live)

    col = lax.broadcasted_iota(I32, (RT, RT_ROWS), 1).astype(F32)
    for t in range(RT_PER_STEP):
        buf = bufs[t]
        rows = slice(t * RT, (t + 1) * RT)
        tile = i * RT_PER_STEP + t

        @pl.when(i >= 1)
        def _(buf=buf, t=t):
            _tile_runs_wait(buf, sem.at[t])

        send(jnp.maximum(tile - 1, 0), (t - 1) % RT_PER_STEP, tile >= 1)
        pos = _packed_positions(ri_ref[rows, :])
        sel = jnp.zeros((RT, RT_ROWS), F32)
        for p in pos:
            sel = sel + (col == p).astype(F32)
        packed = lax.dot_general(sel.astype(BF16), h2_ref[rows, :], _TN, preferred_element_type=F32)
        for s in range(ROW_SLABS):
            buf[pl.ds(s, RT_ROWS, stride=ROW_SLABS), :] = packed[:, s * LANES:(s + 1) * LANES]

    @pl.when(i == n_steps - 1)
    def _():
        send(N_RT - 1, RT_PER_STEP - 1, True)
        for t in range(RT_PER_STEP):
            _tile_runs_wait(bufs[t], sem.at[t])


def _dispatch(tables, h2, route_i):
    grid_spec = pltpu.PrefetchScalarGridSpec(
        num_scalar_prefetch=6,
        grid=(N_RT // RT_PER_STEP,),
        in_specs=[
            pl.BlockSpec((RT_PER_STEP * RT, D_MODEL), lambda i, *_: (i, 0)),
            pl.BlockSpec((RT_PER_STEP * RT, LANES), lambda i, *_: (i, 0)),
        ],
        out_specs=pl.BlockSpec(memory_space=pl.ANY),
        scratch_shapes=[pltpu.VMEM((RT_ROWS * ROW_SLABS, LANES), F32)] * RT_PER_STEP + [
            pltpu.VMEM((BM * ROW_SLABS, LANES), F32),
            pltpu.SemaphoreType.DMA((RT_PER_STEP,)),
            pltpu.SemaphoreType.DMA(()),
        ],
    )
    return pl.pallas_call(
        _dispatch_kernel,
        grid_spec=grid_spec,
        out_shape=jax.ShapeDtypeStruct((N_SLOTS * ROW_SLABS, LANES), F32),
        compiler_params=_cparams(1),
        name="moe_dispatch",
    )(tables["start"], tables["len"], tables["off"], tables["pad_start"], tables["pad_len"], tables["n_valid"],
      h2, route_i)


def _moe_kernel(be_ref, eo_ref, nv_ref, xs_hbm, wgu_hbm, bgu_ref, wdn_hbm, bdn_ref, y_hbm,
                xbuf, ybuf, gu_stage, dn_stage, wgu_bf, wdn_bf, wsem, xsem, ysem, *, li):
    n_valid = nv_ref[0]
    rows_per_piece = D_MODEL // WEIGHT_PIECES

    def weight_copies(e, stage):
        cps = []
        for p in range(WEIGHT_PIECES):
            band = pl.ds(p * rows_per_piece, rows_per_piece)
            cps.append(pltpu.make_async_copy(wgu_hbm.at[li, e, band, :], gu_stage.at[stage, band, :], wsem.at[stage]))
            cps.append(pltpu.make_async_copy(wdn_hbm.at[li, e, band, :], dn_stage.at[stage, band, :], wsem.at[stage]))
        return cps

    def fetch(k):
        @pl.when(eo_ref[k] >= 0)
        def _():
            for cp in weight_copies(eo_ref[k], k % 2):
                cp.start(priority=1)

    def x_copy(b, slot):
        return pltpu.make_async_copy(xs_hbm.at[_row_ds(b * BM, BM), :], xbuf.at[slot], xsem.at[slot])

    def y_copy(b, slot):
        return pltpu.make_async_copy(ybuf.at[slot], y_hbm.at[_row_ds(b * BM, BM), :], ysem.at[slot])

    fetch(0)
    fetch(1)
    for ahead in range(X_BUFFERS - 1):
        @pl.when(ahead < n_valid)
        def _(ahead=ahead):
            x_copy(ahead, ahead).start()

    def next_slot(s):
        return jnp.where(s == X_BUFFERS - 1, 0, s + 1)

    def block(b, carry):
        k, xslot = carry
        slot = b % 2
        e = be_ref[b]
        new_expert = jnp.logical_or(b == 0, e != be_ref[jnp.maximum(b - 1, 0)])

        @pl.when(b + X_BUFFERS - 1 < n_valid)
        def _():
            ahead_slot = xslot
            for _ in range(X_BUFFERS - 1):
                ahead_slot = next_slot(ahead_slot)
            x_copy(b + X_BUFFERS - 1, ahead_slot).start()

        @pl.when(new_expert)
        def _():
            stage = k % 2
            for cp in weight_copies(e, stage):
                cp.wait()
            for st in range(2):
                @pl.when(stage == st)
                def _(st=st):
                    wgu_bf[...] = gu_stage[st].astype(BF16)
                    wdn_bf[...] = dn_stage[st].astype(BF16)
            fetch(k + 2)

        x_copy(b, xslot).wait()

        @pl.when(b >= 2)
        def _():
            y_copy(b - 2, slot).wait()

        x = _slab_columns(xbuf.at[xslot], 0, BM).astype(BF16)
        bgu = bgu_ref[e]
        g = jnp.dot(x, wgu_bf[:, :D_EXPERT], preferred_element_type=F32) + bgu[:, :D_EXPERT]
        u = jnp.dot(x, wgu_bf[:, D_EXPERT:], preferred_element_type=F32) + bgu[:, D_EXPERT:]
        g = jnp.minimum(g, SWIGLU_LIMIT)
        u = jnp.clip(u, -SWIGLU_LIMIT, SWIGLU_LIMIT)
        act = ((u + 1.0) * (0.5 * g * (1.0 + jnp.tanh((0.5 * SWIGLU_ALPHA) * g)))).astype(BF16)
        y = jnp.dot(act, wdn_bf[...], preferred_element_type=F32) + bdn_ref[e]
        out = ybuf.at[slot]
        for s in range(ROW_SLABS):
            out[pl.ds(s, BM, stride=ROW_SLABS), :] = y[:, s * LANES:(s + 1) * LANES]
        y_copy(b, slot).start()
        return k + new_expert.astype(I32), next_slot(xslot)

    lax.fori_loop(0, n_valid, block, (jnp.int32(0), jnp.int32(0)))

    @pl.when(n_valid >= 2)
    def _():
        y_copy(n_valid - 2, n_valid % 2).wait()
    y_copy(n_valid - 1, (n_valid - 1) % 2).wait()

    ybuf[0] = jnp.zeros((BM * ROW_SLABS, LANES), F32)

    def zero_blocks(wait):
        def one(b, carry):
            cp = y_copy(b, 0)
            cp.wait() if wait else cp.start()
            return carry
        lax.fori_loop(n_valid, NB, one, 0)

    zero_blocks(False)
    zero_blocks(True)


def _moe_blocks(tables, xs, w_gu, b_gu, w_down, b_down, li):
    grid_spec = pltpu.PrefetchScalarGridSpec(
        num_scalar_prefetch=3,
        grid=(1,),
        in_specs=[
            pl.BlockSpec(memory_space=pl.ANY),
            pl.BlockSpec(memory_space=pl.ANY),
            pl.BlockSpec((None, N_EXPERTS, 1, 2 * D_EXPERT), lambda i, *_: (li, 0, 0, 0)),
            pl.BlockSpec(memory_space=pl.ANY),
            pl.BlockSpec((None, N_EXPERTS, 1, D_MODEL), lambda i, *_: (li, 0, 0, 0)),
        ],
        out_specs=pl.BlockSpec(memory_space=pl.ANY),
        scratch_shapes=[
            pltpu.VMEM((X_BUFFERS, BM * ROW_SLABS, LANES), F32),
            pltpu.VMEM((2, BM * ROW_SLABS, LANES), F32),
            pltpu.VMEM((2, D_MODEL, 2 * D_EXPERT), F32),
            pltpu.VMEM((2, D_EXPERT, D_MODEL), F32),
            pltpu.VMEM((D_MODEL, 2 * D_EXPERT), BF16),
            pltpu.VMEM((D_EXPERT, D_MODEL), BF16),
            pltpu.SemaphoreType.DMA((2,)),
            pltpu.SemaphoreType.DMA((X_BUFFERS,)),
            pltpu.SemaphoreType.DMA((2,)),
        ],
    )
    return pl.pallas_call(
        functools.partial(_moe_kernel, li=li),
        grid_spec=grid_spec,
        out_shape=jax.ShapeDtypeStruct((N_SLOTS * ROW_SLABS, LANES), F32),
        compiler_params=_cparams(1),
        name="moe_experts",
    )(tables["block_expert"], tables["expert_order"], tables["n_valid"], xs, w_gu,
      b_gu.reshape(DEPTH, N_EXPERTS, 1, 2 * D_EXPERT), w_down, b_down.reshape(DEPTH, N_EXPERTS, 1, D_MODEL))


def _combine_kernel(start_ref, len_ref, off_ref, y_hbm, x1_ref, ri_ref, rf_ref, mod_ref,
                    lng_ref, lnb_ref, oc_ref, od_ref, *scratch):
    bufs, sem = scratch[:RT_PER_STEP], scratch[RT_PER_STEP]
    i = pl.program_id(0)
    n_steps = pl.num_programs(0)

    def fetch(tile, t):
        _tile_run_copies(tile, start_ref, len_ref, off_ref, y_hbm, bufs[t], sem.at[t], to_sorted=False)

    @pl.when(i == 0)
    def _():
        for t in range(FETCH_AHEAD):
            fetch(t, t)

    m = mod_ref[pl.ds(_mod_row(i * RT_PER_STEP // RT_PER_TM), 1), :]
    g2 = m[:, 5 * D_MODEL:6 * D_MODEL]
    col = lax.broadcasted_iota(I32, (RT, RT_ROWS), 1).astype(F32)
    outs = []
    for t in range(RT_PER_STEP):
        rows = slice(t * RT, (t + 1) * RT)
        fetch(jnp.minimum(i * RT_PER_STEP + t + FETCH_AHEAD, N_RT - 1), (t + FETCH_AHEAD) % RT_PER_STEP)
        _tile_runs_wait(bufs[t], sem.at[t])
        pos = _packed_positions(ri_ref[rows, :])
        gates = rf_ref[rows, :]
        mix = jnp.zeros((RT, RT_ROWS), F32)
        for k in range(TOP_K):
            mix = mix + jnp.where(col == pos[k], gates[:, k:k + 1], 0.0)
        ffn = jnp.dot(mix.astype(BF16), _slab_columns(bufs[t], 0, RT_ROWS).astype(BF16),
                      preferred_element_type=F32)
        outs.append(_layer_norm_rows(DN_ALPHA * x1_ref[rows, :] + g2 * ffn, lng_ref[...], lnb_ref[...]))
    out = jnp.concatenate(outs, axis=0)
    n_ctx_steps = N_CTX // (RT * RT_PER_STEP)

    @pl.when(i < n_ctx_steps)
    def _():
        oc_ref[...] = out

    @pl.when(i >= n_ctx_steps)
    def _():
        od_ref[...] = out

    @pl.when(i == n_steps - 1)
    def _():
        for t in range(FETCH_AHEAD):
            _tile_runs_wait(bufs[t], sem.at[t])


def _combine(tables, y, x1, route_i, route_f, mod, ln_g, ln_b, li):
    rows = RT * RT_PER_STEP
    grid_spec = pltpu.PrefetchScalarGridSpec(
        num_scalar_prefetch=3,
        grid=(N_RT // RT_PER_STEP,),
        in_specs=[
            pl.BlockSpec(memory_space=pl.ANY),
            pl.BlockSpec((rows, D_MODEL), lambda i, *_: (i, 0)),
            pl.BlockSpec((rows, LANES), lambda i, *_: (i, 0)),
            pl.BlockSpec((rows, LANES), lambda i, *_: (i, 0)),
            pl.BlockSpec((None, SUBLANES, 6 * D_MODEL), lambda i, *_: (li, 0, 0)),
            _layer_block(li, (1, D_MODEL), 1),
            _layer_block(li, (1, D_MODEL), 1),
        ],
        out_specs=_pair_specs(rows, D_MODEL, N_CTX // rows),
        scratch_shapes=[pltpu.VMEM((RT_ROWS * ROW_SLABS, LANES), F32)] * RT_PER_STEP + [
            pltpu.SemaphoreType.DMA((RT_PER_STEP,)),
        ],
    )
    return pl.pallas_call(
        _combine_kernel,
        grid_spec=grid_spec,
        out_shape=[jax.ShapeDtypeStruct((N_CTX, D_MODEL), F32), jax.ShapeDtypeStruct((N_DEC, D_MODEL), F32)],
        compiler_params=_cparams(1),
        name="moe_combine",
    )(tables["start"], tables["len"], tables["off"], y, x1, route_i, route_f, mod, ln_g, ln_b)


def _routing_tables(tile_cnt):
    cnt = tile_cnt[:, 0, :N_EXPERTS]
    totals = jnp.sum(cnt, axis=0)
    padded = (totals + BM - 1) // BM * BM
    pends = jnp.cumsum(padded)
    pstarts = pends - padded
    start = pstarts[None, :] + jnp.cumsum(cnt, axis=0) - cnt
    off = jnp.cumsum(cnt, axis=1) - cnt
    n_valid = (pends[-1] // BM).astype(I32)
    block_start = jnp.arange(NB, dtype=I32) * BM
    block_e = jnp.minimum(jnp.sum(block_start[:, None] >= pends[None, :], axis=1), N_EXPERTS - 1)
    last_e = block_e[jnp.maximum(n_valid - 1, 0)]
    block_e = jnp.where(jnp.arange(NB) < n_valid, block_e, last_e)
    ids = jnp.arange(N_EXPERTS, dtype=I32)
    order = jnp.sort(jnp.where(totals > 0, ids, N_EXPERTS))
    order = jnp.concatenate([jnp.where(order < N_EXPERTS, order, -1), jnp.full((2,), -1, I32)])
    return {
        "start": (start.reshape(-1) * ROW_SLABS).astype(I32),
        "len": (cnt.reshape(-1) * ROW_SLABS).astype(I32),
        "off": (off.reshape(-1) * ROW_SLABS).astype(I32),
        "pad_start": ((pstarts + totals) * ROW_SLABS).astype(I32),
        "pad_len": ((padded - totals) * ROW_SLABS).astype(I32),
        "n_valid": n_valid.reshape(1),
        "block_expert": block_e.astype(I32),
        "expert_order": order.astype(I32),
    }


def _stacked_params(p):
    eye = jnp.eye(N_LRU_BLOCKS, dtype=F32)
    pad = LANES - N_EXPERTS
    return {
        "lam": p["lam"],
        "subln_col": p["subln_g"].reshape(DEPTH, HEAD_DIM, 1),
        "sgu_ln_g": p["sgu_ln_g"].reshape(DEPTH, 1, SGU_WIDTH),
        "sgu_ln_b": p["sgu_ln_b"].reshape(DEPTH, 1, SGU_WIDTH),
        "w_spatial": p["w_spatial"],
        "b_spatial_full": jnp.repeat(jnp.swapaxes(p["b_spatial"], 1, 2), SGU_GROUP, axis=2),
        "conv_w": p["conv_w"],
        "conv_b": p["conv_b"].reshape(DEPTH, 1, LRU_WIDTH),
        "w_rg_full": jnp.einsum("ldkgio,gh->lgidkho", p["w_rg"], eye).reshape(DEPTH, LRU_WIDTH, 4 * LRU_WIDTH),
        "b_rg_full": p["b_rg"].reshape(DEPTH, 1, 4 * LRU_WIDTH),
        "lru_log": p["lru_log"],
        "w_out_all": p["w_out"],
        "ln_g": p["ln_g"].reshape(DEPTH, 2, 1, D_MODEL),
        "ln_b": p["ln_b"].reshape(DEPTH, 2, 1, D_MODEL),
        "w_router_pad": jnp.pad(p["w_router"], ((0, 0), (0, 0), (0, pad))),
        "b_router_pad": jnp.pad(p["b_router"], ((0, 0), (0, pad))).reshape(DEPTH, 1, LANES),
    }


def kernel(x_prompt, x_sample, cache_k, cache_v, state_lru, c, c_ctx, w_mod, b_mod, w_in, lam, subln_g, sgu_ln_g, sgu_ln_b, w_spatial, b_spatial, conv_w, conv_b, w_rg, b_rg, lru_log, w_out, ln_g, ln_b, w_router, b_router, w_gu, b_gu, w_down, b_down):
    lp = _stacked_params(dict(
        lam=lam, subln_g=subln_g, sgu_ln_g=sgu_ln_g, sgu_ln_b=sgu_ln_b, w_spatial=w_spatial, b_spatial=b_spatial,
        conv_w=conv_w, conv_b=conv_b, w_rg=w_rg, b_rg=b_rg, lru_log=lru_log, w_out=w_out, ln_g=ln_g, ln_b=ln_b,
        w_router=w_router, b_router=b_router))
    cvec8 = jnp.concatenate([c_ctx[None, :], c, jnp.zeros((SUBLANES - 1 - DEC_BATCH, D_MODEL), F32)], axis=0)
    mod = _modulation(cvec8, w_mod, b_mod)
    x_pair = (x_prompt.reshape(N_CTX, D_MODEL), x_sample.reshape(N_DEC, D_MODEL))
    tables = _rope_tables()
    zero_state = jnp.zeros((BATCH, 1, 2, LRU_WIDTH), F32)
    prev_kv = None
    new_s = []
    for li in range(DEPTH):
        lam_init = 0.8 - 0.6 * math.exp(-0.3 * li)
        proj = _input_projection(*x_pair, mod, w_in, li)
        att_ctx, kc, vc = _context_attention(proj, lp["lam"], lp["subln_col"], li, lam_init, prev_kv)
        prev_kv = (kc, vc)
        att_dec = _denoise_attention(proj, cache_k, cache_v, lp["lam"], lp["subln_col"], li, lam_init, tables)
        sgu_ctx, lru_ctx, h_ctx = _mixers(proj, zero_state, 0, lp, li, SEQ, BATCH, 0)
        sgu_dec, lru_dec, _ = _mixers(proj, state_lru, li, lp, li, DEC_SEQ, DEC_BATCH, N_CTX // DEC_SEQ)
        x1, h2, route_i, route_f, tile_cnt = _output_projection(
            x_pair, (att_ctx, att_dec), (sgu_ctx, sgu_dec), (lru_ctx, lru_dec), mod, lp, li)
        rt = _routing_tables(tile_cnt)
        xs = _dispatch(rt, h2, route_i)
        y = _moe_blocks(rt, xs, w_gu, b_gu, w_down, b_down, li)
        x_pair = _combine(rt, y, x1, route_i, route_f, mod, lp["ln_g"], lp["ln_b"], li)
        new_s.append(h_ctx)
    y_prompt = x_pair[0].reshape(BATCH, SEQ, D_MODEL)
    y_sample = x_pair[1].reshape(DEC_BATCH, DEC_SEQ, D_MODEL)
    return (y_prompt, y_sample, prev_kv[0], prev_kv[1], jnp.stack(new_s, axis=1))
```

```python
import functools
import math

import numpy as np
import jax
import jax.numpy as jnp
from jax import lax
from jax.experimental import pallas as pl
from jax.experimental.pallas import tpu as pltpu

F32 = jnp.float32
BF16 = jnp.bfloat16
I32 = jnp.int32

D_MODEL = 1024
BATCH = 32
SEQ = 256
DEPTH = 2
DEC_BATCH = 2
DEC_SEQ = 1024
PAST_LEN = 256
GRID_W = 64
HEAD_DIM = 64
ATT_WIDTH = D_MODEL // 2
SGU_WIDTH = D_MODEL // 4
LRU_WIDTH = D_MODEL // 4
N_ATT_HEADS = ATT_WIDTH // HEAD_DIM
ATT_HALF = HEAD_DIM // 2
ROPE_FREQS = ATT_HALF // 4
ROPE_THETA = 10000.0
CHUNK = 128
N_SGU_GROUPS = 4
SGU_GROUP = SGU_WIDTH // N_SGU_GROUPS
N_LRU_BLOCKS = 4
LRU_BLOCK = LRU_WIDTH // N_LRU_BLOCKS
CONV_W = 4
LRU_C = 8.0
IN_COLS = 3 * ATT_WIDTH + 2 * SGU_WIDTH + 2 * LRU_WIDTH
N_EXPERTS = 32
TOP_K = 4
D_EXPERT = D_MODEL
SWIGLU_LIMIT = 7.0
SWIGLU_ALPHA = 1.702
DN_ALPHA = (2 * DEPTH) ** 0.25
EPS = 1e-5

N_CTX = BATCH * SEQ
N_DEC = DEC_BATCH * DEC_SEQ
N_TOK = N_CTX + N_DEC

LANES = 128
SUBLANES = 8
ROW_SLABS = D_MODEL // LANES

TM = 512
CTX_TILES = N_CTX // TM
TILES_PER_DEC = DEC_SEQ // TM
N_TILES = N_TOK // TM
MOD_TN = 1024
RT = 256
N_RT = N_TOK // RT
RT_PER_TM = TM // RT
RT_ROWS = RT * TOP_K
RT_PER_STEP = 4
FETCH_AHEAD = 2
BM = 256
NB = N_TOK * TOP_K // BM + N_EXPERTS
WEIGHT_PIECES = 8
X_BUFFERS = 4
N_SLOTS = NB * BM
RUN_BITS = tuple(1 << b for b in range(RT.bit_length() - 1, -1, -1))
PAD_BITS = tuple(1 << b for b in range(BM.bit_length() - 2, -1, -1))
VMEM_LIMIT = 56 * 1024 * 1024


def _cparams(n_axes):
    return pltpu.CompilerParams(
        dimension_semantics=("arbitrary",) * n_axes,
        vmem_limit_bytes=VMEM_LIMIT)


def _mod_row(i):
    return jnp.where(i < CTX_TILES, 0, 1 + (i - CTX_TILES) // TILES_PER_DEC)


def _layer_norm_rows(z, g, b):
    mu = jnp.mean(z, axis=-1, keepdims=True)
    zc = z - mu
    var = jnp.mean(zc * zc, axis=-1, keepdims=True)
    return zc * lax.rsqrt(var + EPS) * g + b


def _pair_specs(tile, width, n_ctx_tiles):
    ctx = pl.BlockSpec((tile, width), lambda i, *_: (jnp.minimum(i, n_ctx_tiles - 1), 0))
    dec = pl.BlockSpec((tile, width), lambda i, *_: (jnp.maximum(i - n_ctx_tiles, 0), 0))
    return [ctx, dec]


def _pair_read(i, n_ctx_tiles, ctx_ref, dec_ref):
    return jnp.where(i < n_ctx_tiles, ctx_ref[...], dec_ref[...])


def _layer_block(li, shape, *lead):
    index = (li,) + tuple(lead) + (0,) * len(shape)
    return pl.BlockSpec((None,) * (1 + len(lead)) + tuple(shape), lambda *_: index)


def _mod_kernel(cvec_ref, w_ref, b_ref, o_ref):
    cv = cvec_ref[...]
    s = cv * jax.nn.sigmoid(cv)
    s_t = s.T
    w = w_ref[0]
    rows = [jnp.sum(s_t[:, r:r + 1] * w, axis=0, keepdims=True) for r in range(1 + DEC_BATCH)]
    rows.append(jnp.zeros((SUBLANES - 1 - DEC_BATCH, MOD_TN), F32))
    o_ref[0] = jnp.concatenate(rows, axis=0) + b_ref[0]


def _modulation(cvec8, w_mod, b_mod):
    n_out = w_mod.shape[-1]
    return pl.pallas_call(
        _mod_kernel,
        grid=(DEPTH, n_out // MOD_TN),
        in_specs=[
            pl.BlockSpec((SUBLANES, D_MODEL), lambda l, j: (0, 0)),
            pl.BlockSpec((1, D_MODEL, MOD_TN), lambda l, j: (l, 0, j)),
            pl.BlockSpec((1, 1, MOD_TN), lambda l, j: (l, 0, j)),
        ],
        out_specs=pl.BlockSpec((1, SUBLANES, MOD_TN), lambda l, j: (l, 0, j)),
        out_shape=jax.ShapeDtypeStruct((DEPTH, SUBLANES, n_out), F32),
        compiler_params=_cparams(2),
        name="modulation",
    )(cvec8, w_mod, b_mod.reshape(DEPTH, 1, n_out))


def _inproj_kernel(xc_ref, xd_ref, mod_ref, w_ref, o_ref, wbf_ref):
    i = pl.program_id(0)

    @pl.when(i == 0)
    def _():
        wbf_ref[...] = w_ref[...].astype(BF16)

    m = mod_ref[pl.ds(_mod_row(i), 1), :]
    sh1 = m[:, 0:D_MODEL]
    sc1 = m[:, D_MODEL:2 * D_MODEL]
    x = _pair_read(i, CTX_TILES, xc_ref, xd_ref)
    h = (x * (1.0 + sc1) + sh1).astype(BF16)
    o_ref[...] = jnp.dot(h, wbf_ref[...], preferred_element_type=F32)


def _input_projection(x_ctx, x_dec, mod, w_in, li):
    return pl.pallas_call(
        _inproj_kernel,
        grid=(N_TILES,),
        in_specs=_pair_specs(TM, D_MODEL, CTX_TILES) + [
            pl.BlockSpec((None, SUBLANES, 6 * D_MODEL), lambda i: (li, 0, 0)),
            pl.BlockSpec((None, D_MODEL, IN_COLS), lambda i: (li, 0, 0), pipeline_mode=pl.Buffered(1)),
        ],
        out_specs=pl.BlockSpec((TM, IN_COLS), lambda i: (i, 0)),
        out_shape=jax.ShapeDtypeStruct((N_TOK, IN_COLS), F32),
        scratch_shapes=[pltpu.VMEM((D_MODEL, IN_COLS), BF16)],
        compiler_params=_cparams(1),
        name="input_projection",
    )(x_ctx, x_dec, mod, w_in)


def _lambda(lam_ref, lam_init):
    lm = lam_ref[...]
    a = jnp.sum(lm[0:1] * lm[1:2], axis=-1, keepdims=True)
    b = jnp.sum(lm[2:3] * lm[3:4], axis=-1, keepdims=True)
    return jnp.exp(a) - jnp.exp(b) + lam_init


_NT = (((1,), (1,)), ((), ()))
_TN = (((0,), (0,)), ((), ()))
_ATT_SCALE = ATT_HALF ** -0.5
_Q_SCALE = _ATT_SCALE * math.log2(math.e)


def _attention_heads(pairs, values, e_scr, lam, g_col, lam_init):
    n_heads = len(pairs)
    n_q = e_scr.shape[2]

    def scores(keys, query):
        return lax.dot_general(keys, query, _NT, preferred_element_type=F32)

    tops = []
    for h in range(n_heads):
        for c in range(2):
            seg_tops = [jnp.max(scores(k, q).reshape(k.shape[0] // SUBLANES, SUBLANES, n_q), axis=0)
                        for k, q in pairs[h][c]]
            tops.append(jnp.max(functools.reduce(jnp.maximum, seg_tops), axis=0, keepdims=True))
    for h in range(n_heads):
        for c in range(2):
            n = 2 * h + c
            row = 0
            for k, q in pairs[h][c]:
                e_scr[n, row:row + k.shape[0], :] = jnp.exp2(scores(k, q) - tops[n]).astype(BF16)
                row += k.shape[0]
    outs = []
    for h in range(n_heads):
        v_aug = jnp.concatenate([values[h], jnp.ones_like(values[h])], axis=-1)
        normed = []
        for c in range(2):
            acc = lax.dot_general(v_aug, e_scr[2 * h + c], _TN, preferred_element_type=F32)
            normed.append(acc[:HEAD_DIM] * (1.0 / acc[HEAD_DIM:HEAD_DIM + 1]))
        o_t = normed[0] - lam * normed[1]
        ms = jnp.mean(o_t * o_t, axis=0, keepdims=True)
        outs.append(o_t * lax.rsqrt(ms + EPS) * g_col * (1.0 - lam_init))
    return outs


def _component_query(q_window, c, lo):
    lane = lax.broadcasted_iota(I32, q_window.shape, 1)
    first = lo + c * ATT_HALF
    inside = jnp.logical_and(lane >= first, lane < first + ATT_HALF)
    return jnp.where(inside, q_window, jnp.zeros_like(q_window))


def _lane_tile(x, h):
    heads_per_tile = LANES // HEAD_DIM
    t = h // heads_per_tile
    return x[:, t * LANES:(t + 1) * LANES], (h % heads_per_tile) * HEAD_DIM


def _ctx_attn_kernel(lam_ref, g_ref, q_ref, k_ref, v_ref, *rest, lam_init, stacked):
    *rest, e_scr = rest
    if stacked:
        pk_ref, pv_ref, att_ref, ck_ref, cv_ref = rest
        ck_ref[0, 0] = pk_ref[0]
        cv_ref[0, 0] = pv_ref[0]
        put_k = lambda h, val: ck_ref.__setitem__((0, 1, h), val)
        put_v = lambda h, val: cv_ref.__setitem__((0, 1, h), val)
    else:
        att_ref, ck_ref, cv_ref = rest
        put_k = lambda h, val: ck_ref.__setitem__((0, h), val)
        put_v = lambda h, val: cv_ref.__setitem__((0, h), val)
    lam = _lambda(lam_ref, lam_init)
    k = k_ref[...]
    v = v_ref[...]
    qb = (q_ref[...] * _Q_SCALE).astype(BF16)
    kb = k.astype(BF16)
    vb = v.astype(BF16)
    heads = range(N_ATT_HEADS)
    pairs = []
    for h in heads:
        (k_tile, lo), (q_tile, _) = _lane_tile(kb, h), _lane_tile(qb, h)
        pairs.append([[(k_tile, _component_query(q_tile, c, lo))] for c in range(2)])
    outs = _attention_heads(pairs, [vb[:, h * HEAD_DIM:(h + 1) * HEAD_DIM] for h in heads],
                            e_scr, lam, g_ref[...], lam_init)
    for h in heads:
        put_k(h, k[:, h * HEAD_DIM:(h + 1) * HEAD_DIM])
        put_v(h, v[:, h * HEAD_DIM:(h + 1) * HEAD_DIM])
    att_ref[...] = jnp.concatenate(outs, axis=0).T


def _context_attention(proj, lam, subln_col, li, lam_init, prev_kv=None):
    stacked = prev_kv is not None
    per_layer = pl.BlockSpec((1, N_ATT_HEADS, SEQ, HEAD_DIM), lambda b: (b, 0, 0, 0))
    if stacked:
        kv_shape = jax.ShapeDtypeStruct((BATCH, DEPTH, N_ATT_HEADS, SEQ, HEAD_DIM), F32)
        kv_spec = pl.BlockSpec((1, DEPTH, N_ATT_HEADS, SEQ, HEAD_DIM), lambda b: (b, 0, 0, 0, 0))
    else:
        kv_shape = jax.ShapeDtypeStruct((BATCH, N_ATT_HEADS, SEQ, HEAD_DIM), F32)
        kv_spec = per_layer
    return pl.pallas_call(
        functools.partial(_ctx_attn_kernel, lam_init=lam_init, stacked=stacked),
        grid=(BATCH,),
        in_specs=[
            _layer_block(li, (4, ATT_HALF)),
            _layer_block(li, (HEAD_DIM, 1)),
            pl.BlockSpec((SEQ, ATT_WIDTH), lambda b: (b, 0)),
            pl.BlockSpec((SEQ, ATT_WIDTH), lambda b: (b, 1)),
            pl.BlockSpec((SEQ, ATT_WIDTH), lambda b: (b, 2)),
        ] + ([per_layer, per_layer] if stacked else []),
        out_specs=[pl.BlockSpec((SEQ, ATT_WIDTH), lambda b: (b, 0)), kv_spec, kv_spec],
        out_shape=[jax.ShapeDtypeStruct((N_CTX, ATT_WIDTH), F32), kv_shape, kv_shape],
        scratch_shapes=[pltpu.VMEM((2 * N_ATT_HEADS, SEQ, SEQ), BF16)],
        compiler_params=_cparams(1),
        name="context_attention",
    )(lam, subln_col, proj, proj, proj, *(prev_kv if stacked else ()))


def _rope_tables():
    t = np.arange(DEC_SEQ)
    pos = np.stack([t // GRID_W, t % GRID_W], axis=1).astype(np.float32)
    inv = (np.float32(ROPE_THETA) ** (-np.arange(ROPE_FREQS, dtype=np.float32) / np.float32(ROPE_FREQS)))
    j = np.arange(HEAD_DIM)
    d = j % ATT_HALF
    axis = d // (2 * ROPE_FREQS)
    u = d % (2 * ROPE_FREQS)
    ang = pos[:, axis] * inv[u % ROPE_FREQS][None, :].astype(np.float32)
    cos = np.cos(ang).astype(np.float32)
    sin = np.sin(ang).astype(np.float32)
    first = (u < ROPE_FREQS)[None, :]
    s_next = np.where(first, -sin, 0.0).astype(np.float32)
    s_prev = np.where(first, 0.0, sin).astype(np.float32)
    tile = lambda a: jnp.asarray(np.tile(a, (1, N_ATT_HEADS)))
    return tile(cos), tile(s_next), tile(s_prev)


def _rotate(x, cos, s_next, s_prev):
    width = x.shape[-1]
    return (x * cos + pltpu.roll(x, width - ROPE_FREQS, axis=1) * s_next
            + pltpu.roll(x, ROPE_FREQS, axis=1) * s_prev)


QB = 256
Q_STEPS = DEC_SEQ // QB
DEC_HEAD_GROUP = 2


def _dec_attn_kernel(lam_ref, g_ref, q_ref, k_ref, v_ref, ck_ref, cv_ref,
                     cq_ref, snq_ref, spq_ref, ck_tab, snk_tab, spk_tab,
                     att_ref, krot_ref, e_scr, *, lam_init):
    j = pl.program_id(1)

    @pl.when(j == 0)
    def _():
        krot_ref[...] = _rotate(k_ref[...], ck_tab[...], snk_tab[...], spk_tab[...]).astype(BF16)

    lam = _lambda(lam_ref, lam_init)
    qb = (_rotate(q_ref[...], cq_ref[...], snq_ref[...], spq_ref[...]) * _Q_SCALE).astype(BF16)
    kb = krot_ref[...]
    vb = v_ref[...].astype(BF16)
    outs = []
    for h0 in range(0, N_ATT_HEADS, DEC_HEAD_GROUP):
        heads = range(h0, h0 + DEC_HEAD_GROUP)
        pairs = []
        for h in heads:
            past_k = ck_ref[0, 0, h].astype(BF16)
            q_head = qb[:, h * HEAD_DIM:(h + 1) * HEAD_DIM]
            (k_tile, lo), (q_tile, _) = _lane_tile(kb, h), _lane_tile(qb, h)
            pairs.append([[(past_k, _component_query(q_head, c, 0)), (k_tile, _component_query(q_tile, c, lo))]
                          for c in range(2)])
        values = [jnp.concatenate([cv_ref[0, 0, h].astype(BF16), vb[:, h * HEAD_DIM:(h + 1) * HEAD_DIM]], axis=0)
                  for h in heads]
        outs += _attention_heads(pairs, values, e_scr, lam, g_ref[...], lam_init)
    att_ref[...] = jnp.concatenate(outs, axis=0).T


def _denoise_attention(proj, cache_k, cache_v, lam, subln_col, li, lam_init, tables):
    cos, s_next, s_prev = tables
    row0 = N_CTX // QB
    seq0 = N_CTX // DEC_SEQ
    q_tab = pl.BlockSpec((QB, ATT_WIDTH), lambda b, j: (j, 0))
    k_tab = pl.BlockSpec((DEC_SEQ, ATT_WIDTH), lambda b, j: (0, 0))
    cache_spec = pl.BlockSpec((1, 1, N_ATT_HEADS, PAST_LEN, HEAD_DIM), lambda b, j: (b, li, 0, 0, 0))
    return pl.pallas_call(
        functools.partial(_dec_attn_kernel, lam_init=lam_init),
        grid=(DEC_BATCH, Q_STEPS),
        in_specs=[
            _layer_block(li, (4, ATT_HALF)),
            _layer_block(li, (HEAD_DIM, 1)),
            pl.BlockSpec((QB, ATT_WIDTH), lambda b, j: (row0 + b * Q_STEPS + j, 0)),
            pl.BlockSpec((DEC_SEQ, ATT_WIDTH), lambda b, j: (seq0 + b, 1)),
            pl.BlockSpec((DEC_SEQ, ATT_WIDTH), lambda b, j: (seq0 + b, 2)),
            cache_spec, cache_spec,
            q_tab, q_tab, q_tab, k_tab, k_tab, k_tab,
        ],
        out_specs=pl.BlockSpec((QB, ATT_WIDTH), lambda b, j: (b * Q_STEPS + j, 0)),
        out_shape=jax.ShapeDtypeStruct((N_DEC, ATT_WIDTH), F32),
        scratch_shapes=[pltpu.VMEM((DEC_SEQ, ATT_WIDTH), BF16),
                        pltpu.VMEM((2 * DEC_HEAD_GROUP, PAST_LEN + DEC_SEQ, QB), BF16)],
        compiler_params=_cparams(2),
        name="denoise_attention",
    )(lam, subln_col, proj, proj, proj, cache_k, cache_v,
      cos, s_next, s_prev, cos, s_next, s_prev)


def _softplus(z):
    return jnp.maximum(z, 0.0) + jnp.log1p(jnp.exp(-jnp.abs(z)))


def _mixer_kernel(su_ref, sv_ref, rx_ref, rg_ref, h0_ref, lng_ref, lnb_ref, ws_ref, bs_ref,
                  cw_ref, cb_ref, wrg_ref, brg_ref, lrulog_ref,
                  sgu_ref, lru_ref, hlast_ref, a_scr, b_scr, h_scr, *, seq_len):
    vn = _layer_norm_rows(sv_ref[...], lng_ref[...], lnb_ref[...])
    lane_group = lax.broadcasted_iota(I32, (CHUNK, SGU_WIDTH), 1) // SGU_GROUP
    for n in range(seq_len // CHUNK):
        rows = slice(n * CHUNK, (n + 1) * CHUNK)
        vc = vn[rows].astype(BF16)
        s = jnp.zeros((CHUNK, SGU_WIDTH), F32)
        for g in range(N_SGU_GROUPS):
            sg = jnp.dot(ws_ref[g].astype(BF16), vc, preferred_element_type=F32)
            s = jnp.where(lane_group == g, sg, s)
        sgu_ref[rows, :] = su_ref[rows, :] * (s + bs_ref[...])

    x = rx_ref[...]
    row = lax.broadcasted_iota(I32, (seq_len, LRU_WIDTH), 0)

    def shifted(val, d, fill):
        rolled = pltpu.roll(val, d % seq_len, axis=0)
        inside = (row >= d) if d > 0 else (row < seq_len + d)
        return jnp.where(inside, rolled, fill)

    left = CONV_W // 2
    xc = cb_ref[...] + x * cw_ref[left:left + 1, :]
    for tap in range(CONV_W):
        if tap != left:
            xc = xc + shifted(x, left - tap, 0.0) * cw_ref[tap:tap + 1, :]
    pre = jnp.dot(xc.astype(BF16), wrg_ref[...].astype(BF16), preferred_element_type=F32) + brg_ref[...]
    gates = 0.5 + 0.5 * jnp.tanh(0.5 * pre)
    in_chunk = row % SUBLANES
    n_chunks = seq_len // SUBLANES
    h0 = h0_ref[0, 0]
    lasts = []
    for direction in range(2):
        reverse = direction == 1
        base = direction * 2 * LRU_WIDTH
        r = gates[:, base:base + LRU_WIDTH]
        gi = gates[:, base + LRU_WIDTH:base + 2 * LRU_WIDTH]
        log_a = -LRU_C * r * _softplus(-lrulog_ref[direction:direction + 1, :])
        a = jnp.exp(log_a)
        b = jnp.sqrt(-jnp.tanh(log_a) * (a * a + 1.0)) * gi * xc
        def chunk_roll(val, shift):
            chunks = val.reshape(n_chunks, SUBLANES, LRU_WIDTH)
            return pltpu.roll(chunks, shift, axis=1).reshape(seq_len, LRU_WIDTH)

        for d in (1, 2, 4):
            if reverse:
                inside = in_chunk < SUBLANES - d
                a_n = jnp.where(inside, chunk_roll(a, SUBLANES - d), 1.0)
                b_n = jnp.where(inside, chunk_roll(b, SUBLANES - d), 0.0)
            else:
                inside = in_chunk >= d
                a_n = jnp.where(inside, chunk_roll(a, d), 1.0)
                b_n = jnp.where(inside, chunk_roll(b, d), 0.0)
            b = a * b_n + b
            a = a * a_n
        a_scr[...] = a
        b_scr[...] = b

        def chunk_step(c, carry, reverse=reverse):
            cc = n_chunks - 1 - c if reverse else c
            off = pl.multiple_of(cc * SUBLANES, SUBLANES)
            hc = a_scr[pl.ds(off, SUBLANES), :] * carry + b_scr[pl.ds(off, SUBLANES), :]
            if reverse:
                h_scr[pl.ds(off, SUBLANES), :] = h_scr[pl.ds(off, SUBLANES), :] + hc
                return hc[0:1, :]
            h_scr[pl.ds(off, SUBLANES), :] = hc
            return hc[SUBLANES - 1:SUBLANES, :]

        lasts.append(lax.fori_loop(0, n_chunks, chunk_step, h0[direction:direction + 1, :]))
    lru_ref[...] = h_scr[...] * jax.nn.gelu(rg_ref[...])
    hlast_ref[0] = jnp.concatenate(lasts, axis=0)


def _mixers(proj, h0, h0_layer, lp, li, seq_len, n_seq, row_block0):
    col0 = 3 * ATT_WIDTH // SGU_WIDTH
    col = lambda c: pl.BlockSpec((seq_len, SGU_WIDTH), lambda b: (row_block0 + b, col0 + c))
    full = lambda shape: _layer_block(li, shape)
    out_rows = pl.BlockSpec((seq_len, SGU_WIDTH), lambda b: (b, 0))
    return pl.pallas_call(
        functools.partial(_mixer_kernel, seq_len=seq_len),
        grid=(n_seq,),
        in_specs=[
            col(0), col(1), col(2), col(3),
            pl.BlockSpec((1, 1, 2, LRU_WIDTH), lambda b: (b, h0_layer, 0, 0)),
            full((1, SGU_WIDTH)), full((1, SGU_WIDTH)),
            full((N_SGU_GROUPS, CHUNK, CHUNK)), full((CHUNK, SGU_WIDTH)),
            full((CONV_W, LRU_WIDTH)), full((1, LRU_WIDTH)),
            full((LRU_WIDTH, 4 * LRU_WIDTH)), full((1, 4 * LRU_WIDTH)),
            full((2, LRU_WIDTH)),
        ],
        out_specs=[out_rows, out_rows, pl.BlockSpec((1, 2, LRU_WIDTH), lambda b: (b, 0, 0))],
        out_shape=[
            jax.ShapeDtypeStruct((n_seq * seq_len, SGU_WIDTH), F32),
            jax.ShapeDtypeStruct((n_seq * seq_len, LRU_WIDTH), F32),
            jax.ShapeDtypeStruct((n_seq, 2, LRU_WIDTH), F32),
        ],
        scratch_shapes=[pltpu.VMEM((seq_len, LRU_WIDTH), F32)] * 3,
        compiler_params=_cparams(1),
        name="mixers_%d" % seq_len,
    )(proj, proj, proj, proj, h0, lp["sgu_ln_g"], lp["sgu_ln_b"], lp["w_spatial"], lp["b_spatial_full"],
      lp["conv_w"], lp["conv_b"], lp["w_rg_full"], lp["b_rg_full"], lp["lru_log"])


def _outproj_kernel(xc_ref, xd_ref, ac_ref, ad_ref, sc_ref, sd_ref, lc_ref, ld_ref,
                    mod_ref, wout_ref, lng_ref, lnb_ref, wr_ref, br_ref,
                    x1_ref, h2_ref, ri_ref, rf_ref, cnt_ref, wbf_ref):
    i = pl.program_id(0)

    @pl.when(i == 0)
    def _():
        wbf_ref[...] = wout_ref[...].astype(BF16)

    m = mod_ref[pl.ds(_mod_row(i), 1), :]
    g1 = m[:, 2 * D_MODEL:3 * D_MODEL]
    sh2 = m[:, 3 * D_MODEL:4 * D_MODEL]
    sc2 = m[:, 4 * D_MODEL:5 * D_MODEL]
    a0, a1 = ATT_WIDTH, ATT_WIDTH + SGU_WIDTH
    x = _pair_read(i, CTX_TILES, xc_ref, xd_ref)
    att = _pair_read(i, CTX_TILES, ac_ref, ad_ref).astype(BF16)
    sgu = _pair_read(i, CTX_TILES, sc_ref, sd_ref).astype(BF16)
    lru = _pair_read(i, CTX_TILES, lc_ref, ld_ref).astype(BF16)
    mix = (jnp.dot(att, wbf_ref[0:a0, :], preferred_element_type=F32)
           + jnp.dot(sgu, wbf_ref[a0:a1, :], preferred_element_type=F32)
           + jnp.dot(lru, wbf_ref[a1:, :], preferred_element_type=F32))
    x1 = _layer_norm_rows(DN_ALPHA * x + g1 * mix, lng_ref[...], lnb_ref[...])
    x1_ref[...] = x1
    h2 = x1 * (1.0 + sc2) + sh2
    h2_ref[...] = h2.astype(BF16)

    logits = jnp.dot(h2.astype(BF16), wr_ref[...].astype(BF16), preferred_element_type=F32) + br_ref[...]
    lane = lax.broadcasted_iota(I32, (TM, LANES), 1)
    lane_f = lane.astype(F32)
    neg_inf = jnp.float32(-jnp.inf)
    work = jnp.where(lane < N_EXPERTS, logits, neg_inf)
    vals, idxs = [], []
    for _ in range(TOP_K):
        top = jnp.max(work, axis=-1, keepdims=True)
        idx = jnp.min(jnp.where(work == top, lane_f, float(LANES)), axis=-1, keepdims=True)
        vals.append(top)
        idxs.append(idx)
        work = jnp.where(lane_f == idx, neg_inf, work)
    exps = [jnp.exp(v - vals[0]) for v in vals]
    denom = exps[0] + exps[1] + exps[2] + exps[3]
    onehot = jnp.zeros((TM, LANES), F32)
    for idx in idxs:
        onehot = onehot + (lane_f == idx).astype(F32)
    r_i = lax.broadcasted_iota(I32, (RT, RT), 0)
    c_i = lax.broadcasted_iota(I32, (RT, RT), 1)
    tri = (r_i > c_i).astype(F32).astype(BF16)
    upper = (lax.broadcasted_iota(I32, (LANES, LANES), 0)
             < lax.broadcasted_iota(I32, (LANES, LANES), 1)).astype(F32).astype(BF16)
    packed_pos = []
    for t in range(RT_PER_TM):
        hot = onehot[t * RT:(t + 1) * RT]
        total = jnp.broadcast_to(jnp.sum(hot, axis=0, keepdims=True), (SUBLANES, LANES))
        cnt_ref[t] = total.astype(I32)
        run_start = jnp.dot(total.astype(BF16), upper, preferred_element_type=F32)[0:1, :]
        packed_pos.append(jnp.dot(tri, hot.astype(BF16), preferred_element_type=F32) + run_start)
    packed_pos = jnp.concatenate(packed_pos, axis=0)
    ri = jnp.zeros((TM, LANES), F32)
    rf = jnp.zeros((TM, LANES), F32)
    for k in range(TOP_K):
        pos = jnp.sum(jnp.where(lane_f == idxs[k], packed_pos, 0.0), axis=-1, keepdims=True)
        ri = jnp.where(lane == k, idxs[k], ri)
        ri = jnp.where(lane == TOP_K + k, pos, ri)
        rf = jnp.where(lane == k, exps[k] / denom, rf)
    ri_ref[...] = ri.astype(I32)
    rf_ref[...] = rf


def _output_projection(x_pair, att_pair, sgu_pair, lru_pair, mod, lp, li):
    rows = lambda w: pl.BlockSpec((TM, w), lambda i: (i, 0))
    full = lambda shape: _layer_block(li, shape)
    return pl.pallas_call(
        _outproj_kernel,
        grid=(N_TILES,),
        in_specs=(_pair_specs(TM, D_MODEL, CTX_TILES) + _pair_specs(TM, ATT_WIDTH, CTX_TILES)
                  + _pair_specs(TM, SGU_WIDTH, CTX_TILES) + _pair_specs(TM, LRU_WIDTH, CTX_TILES) + [
            pl.BlockSpec((None, SUBLANES, 6 * D_MODEL), lambda i: (li, 0, 0)),
            pl.BlockSpec((None, D_MODEL, D_MODEL), lambda i: (li, 0, 0), pipeline_mode=pl.Buffered(1)),
            _layer_block(li, (1, D_MODEL), 0), _layer_block(li, (1, D_MODEL), 0),
            full((D_MODEL, LANES)), full((1, LANES)),
        ]),
        out_specs=[
            rows(D_MODEL), rows(D_MODEL), rows(LANES), rows(LANES),
            pl.BlockSpec((RT_PER_TM, SUBLANES, LANES), lambda i: (i, 0, 0)),
        ],
        out_shape=[
            jax.ShapeDtypeStruct((N_TOK, D_MODEL), F32),
            jax.ShapeDtypeStruct((N_TOK, D_MODEL), BF16),
            jax.ShapeDtypeStruct((N_TOK, LANES), I32),
            jax.ShapeDtypeStruct((N_TOK, LANES), F32),
            jax.ShapeDtypeStruct((N_RT, SUBLANES, LANES), I32),
        ],
        scratch_shapes=[pltpu.VMEM((D_MODEL, D_MODEL), BF16)],
        compiler_params=_cparams(1),
        name="output_projection",
    )(*x_pair, *att_pair, *sgu_pair, *lru_pair, mod, lp["w_out_all"], lp["ln_g"], lp["ln_b"],
      lp["w_router_pad"], lp["b_router_pad"])


def _row_ds(row, n_rows):
    return pl.ds(pl.multiple_of(row * ROW_SLABS, ROW_SLABS), n_rows * ROW_SLABS)


def _for_each_run_piece(length_s, bits, fn):
    for bit in bits:
        done_s = length_s & (-2 * bit * ROW_SLABS)
        @pl.when((length_s & (bit * ROW_SLABS)) != 0)
        def _(done_s=done_s, bit=bit):
            fn(done_s, bit)


def _stored_ds(row_s, n_rows):
    return pl.ds(pl.multiple_of(row_s, ROW_SLABS), n_rows * ROW_SLABS)


def _tile_run_copies(tile, start_ref, len_ref, off_ref, sorted_hbm, buf, sem, to_sorted, live=None):
    for e in range(N_EXPERTS):
        t = tile * N_EXPERTS + e
        start_s, length_s, off_s = start_ref[t], len_ref[t], off_ref[t]
        if live is not None:
            length_s = jnp.where(live, length_s, 0)

        def piece(done_s, bit, start_s=start_s, off_s=off_s):
            packed = buf.at[_stored_ds(off_s + done_s, bit), :]
            srt = sorted_hbm.at[_stored_ds(start_s + done_s, bit), :]
            if to_sorted:
                pltpu.make_async_copy(packed, srt, sem).start()
            else:
                pltpu.make_async_copy(srt, packed, sem).start()

        _for_each_run_piece(length_s, RUN_BITS, piece)


def _tile_runs_wait(buf, sem):
    pltpu.make_async_copy(buf, buf, sem).wait()


def _packed_positions(ri):
    return [ri[:, TOP_K + k:TOP_K + k + 1].astype(F32) for k in range(TOP_K)]


def _slab_columns(buf, row0, n_rows):
    return jnp.concatenate(
        [buf[pl.ds(row0 * ROW_SLABS + s, n_rows, stride=ROW_SLABS), :] for s in range(ROW_SLABS)], axis=-1)


def _dispatch_kernel(start_ref, len_ref, off_ref, pstart_ref, plen_ref, nv_ref,
                     h2_ref, ri_ref, xs_hbm, *scratch):
    bufs, (zbuf, sem, zsem) = scratch[:RT_PER_STEP], scratch[RT_PER_STEP:]
    i = pl.program_id(0)
    n_steps = pl.num_programs(0)

    @pl.when(i == 0)
    def _():
        zbuf[...] = jnp.zeros_like(zbuf)

        def zero_fill(wait):
            def go(cp):
                cp.wait() if wait else cp.start()

            def per_expert(e, carry):
                def piece(done_s, bit):
                    go(pltpu.make_async_copy(zbuf.at[pl.ds(0, bit * ROW_SLABS), :],
                                             xs_hbm.at[_stored_ds(pstart_ref[e] + done_s, bit), :], zsem))
                _for_each_run_piece(plen_ref[e], PAD_BITS, piece)
                return carry
            lax.fori_loop(0, N_EXPERTS, per_expert, 0)

            def per_block(b, carry):
                go(pltpu.make_async_copy(zbuf, xs_hbm.at[_row_ds(b * BM, BM), :], zsem))
                return carry
            lax.fori_loop(nv_ref[0], NB, per_block, 0)

        zero_fill(False)
        zero_fill(True)

    def send(tile, t, live):
        _tile_run_copies(tile, start_ref, len_ref, off_ref, xs_hbm, bufs[t], sem.at[t], to_sorted=True, live=live)

    col = lax.broadcasted_iota(I32, (RT, RT_ROWS), 1).astype(F32)
    for t in range(RT_PER_STEP):
        buf = bufs[t]
        rows = slice(t * RT, (t + 1) * RT)
        tile = i * RT_PER_STEP + t

        @pl.when(i >= 1)
        def _(buf=buf, t=t):
            _tile_runs_wait(buf, sem.at[t])

        send(jnp.maximum(tile - 1, 0), (t - 1) % RT_PER_STEP, tile >= 1)
        pos = _packed_positions(ri_ref[rows, :])
        sel = jnp.zeros((RT, RT_ROWS), F32)
        for p in pos:
            sel = sel + (col == p).astype(F32)
        packed = lax.dot_general(sel.astype(BF16), h2_ref[rows, :], _TN, preferred_element_type=F32)
        for s in range(ROW_SLABS):
            buf[pl.ds(s, RT_ROWS, stride=ROW_SLABS), :] = packed[:, s * LANES:(s + 1) * LANES]

    @pl.when(i == n_steps - 1)
    def _():
        send(N_RT - 1, RT_PER_STEP - 1, True)
        for t in range(RT_PER_STEP):
            _tile_runs_wait(bufs[t], sem.at[t])


def _dispatch(tables, h2, route_i):
    grid_spec = pltpu.PrefetchScalarGridSpec(
        num_scalar_prefetch=6,
        grid=(N_RT // RT_PER_STEP,),
        in_specs=[
            pl.BlockSpec((RT_PER_STEP * RT, D_MODEL), lambda i, *_: (i, 0)),
            pl.BlockSpec((RT_PER_STEP * RT, LANES), lambda i, *_: (i, 0)),
        ],
        out_specs=pl.BlockSpec(memory_space=pl.ANY),
        scratch_shapes=[pltpu.VMEM((RT_ROWS * ROW_SLABS, LANES), F32)] * RT_PER_STEP + [
            pltpu.VMEM((BM * ROW_SLABS, LANES), F32),
            pltpu.SemaphoreType.DMA((RT_PER_STEP,)),
            pltpu.SemaphoreType.DMA(()),
        ],
    )
    return pl.pallas_call(
        _dispatch_kernel,
        grid_spec=grid_spec,
        out_shape=jax.ShapeDtypeStruct((N_SLOTS * ROW_SLABS, LANES), F32),
        compiler_params=_cparams(1),
        name="moe_dispatch",
    )(tables["start"], tables["len"], tables["off"], tables["pad_start"], tables["pad_len"], tables["n_valid"],
      h2, route_i)


def _moe_kernel(be_ref, eo_ref, nv_ref, xs_hbm, wgu_hbm, bgu_ref, wdn_hbm, bdn_ref, y_hbm,
                xbuf, ybuf, gu_stage, dn_stage, wgu_bf, wdn_bf, wsem, xsem, ysem, *, li):
    n_valid = nv_ref[0]
    rows_per_piece = D_MODEL // WEIGHT_PIECES

    def weight_copies(e, stage):
        cps = []
        for p in range(WEIGHT_PIECES):
            band = pl.ds(p * rows_per_piece, rows_per_piece)
            cps.append(pltpu.make_async_copy(wgu_hbm.at[li, e, band, :], gu_stage.at[stage, band, :], wsem.at[stage]))
            cps.append(pltpu.make_async_copy(wdn_hbm.at[li, e, band, :], dn_stage.at[stage, band, :], wsem.at[stage]))
        return cps

    def fetch(k):
        @pl.when(eo_ref[k] >= 0)
        def _():
            for cp in weight_copies(eo_ref[k], k % 2):
                cp.start(priority=1)

    def x_copy(b, slot):
        return pltpu.make_async_copy(xs_hbm.at[_row_ds(b * BM, BM), :], xbuf.at[slot], xsem.at[slot])

    def y_copy(b, slot):
        return pltpu.make_async_copy(ybuf.at[slot], y_hbm.at[_row_ds(b * BM, BM), :], ysem.at[slot])

    fetch(0)
    fetch(1)
    for ahead in range(X_BUFFERS - 1):
        @pl.when(ahead < n_valid)
        def _(ahead=ahead):
            x_copy(ahead, ahead).start()

    def next_slot(s):
        return jnp.where(s == X_BUFFERS - 1, 0, s + 1)

    def block(b, carry):
        k, xslot = carry
        slot = b % 2
        e = be_ref[b]
        new_expert = jnp.logical_or(b == 0, e != be_ref[jnp.maximum(b - 1, 0)])

        @pl.when(b + X_BUFFERS - 1 < n_valid)
        def _():
            ahead_slot = xslot
            for _ in range(X_BUFFERS - 1):
                ahead_slot = next_slot(ahead_slot)
            x_copy(b + X_BUFFERS - 1, ahead_slot).start()

        @pl.when(new_expert)
        def _():
            stage = k % 2
            for cp in weight_copies(e, stage):
                cp.wait()
            for st in range(2):
                @pl.when(stage == st)
                def _(st=st):
                    wgu_bf[...] = gu_stage[st].astype(BF16)
                    wdn_bf[...] = dn_stage[st].astype(BF16)
            fetch(k + 2)

        x_copy(b, xslot).wait()

        @pl.when(b >= 2)
        def _():
            y_copy(b - 2, slot).wait()

        x = _slab_columns(xbuf.at[xslot], 0, BM).astype(BF16)
        bgu = bgu_ref[e]
        g = jnp.dot(x, wgu_bf[:, :D_EXPERT], preferred_element_type=F32) + bgu[:, :D_EXPERT]
        u = jnp.dot(x, wgu_bf[:, D_EXPERT:], preferred_element_type=F32) + bgu[:, D_EXPERT:]
        g = jnp.minimum(g, SWIGLU_LIMIT)
        u = jnp.clip(u, -SWIGLU_LIMIT, SWIGLU_LIMIT)
        act = ((u + 1.0) * (0.5 * g * (1.0 + jnp.tanh((0.5 * SWIGLU_ALPHA) * g)))).astype(BF16)
        y = jnp.dot(act, wdn_bf[...], preferred_element_type=F32) + bdn_ref[e]
        out = ybuf.at[slot]
        for s in range(ROW_SLABS):
            out[pl.ds(s, BM, stride=ROW_SLABS), :] = y[:, s * LANES:(s + 1) * LANES]
        y_copy(b, slot).start()
        return k + new_expert.astype(I32), next_slot(xslot)

    lax.fori_loop(0, n_valid, block, (jnp.int32(0), jnp.int32(0)))

    @pl.when(n_valid >= 2)
    def _():
        y_copy(n_valid - 2, n_valid % 2).wait()
    y_copy(n_valid - 1, (n_valid - 1) % 2).wait()

    ybuf[0] = jnp.zeros((BM * ROW_SLABS, LANES), F32)

    def zero_blocks(wait):
        def one(b, carry):
            cp = y_copy(b, 0)
            cp.wait() if wait else cp.start()
            return carry
        lax.fori_loop(n_valid, NB, one, 0)

    zero_blocks(False)
    zero_blocks(True)


def _moe_blocks(tables, xs, w_gu, b_gu, w_down, b_down, li):
    grid_spec = pltpu.PrefetchScalarGridSpec(
        num_scalar_prefetch=3,
        grid=(1,),
        in_specs=[
            pl.BlockSpec(memory_space=pl.ANY),
            pl.BlockSpec(memory_space=pl.ANY),
            pl.BlockSpec((None, N_EXPERTS, 1, 2 * D_EXPERT), lambda i, *_: (li, 0, 0, 0)),
            pl.BlockSpec(memory_space=pl.ANY),
            pl.BlockSpec((None, N_EXPERTS, 1, D_MODEL), lambda i, *_: (li, 0, 0, 0)),
        ],
        out_specs=pl.BlockSpec(memory_space=pl.ANY),
        scratch_shapes=[
            pltpu.VMEM((X_BUFFERS, BM * ROW_SLABS, LANES), F32),
            pltpu.VMEM((2, BM * ROW_SLABS, LANES), F32),
            pltpu.VMEM((2, D_MODEL, 2 * D_EXPERT), F32),
            pltpu.VMEM((2, D_EXPERT, D_MODEL), F32),
            pltpu.VMEM((D_MODEL, 2 * D_EXPERT), BF16),
            pltpu.VMEM((D_EXPERT, D_MODEL), BF16),
            pltpu.SemaphoreType.DMA((2,)),
            pltpu.SemaphoreType.DMA((X_BUFFERS,)),
            pltpu.SemaphoreType.DMA((2,)),
        ],
    )
    return pl.pallas_call(
        functools.partial(_moe_kernel, li=li),
        grid_spec=grid_spec,
        out_shape=jax.ShapeDtypeStruct((N_SLOTS * ROW_SLABS, LANES), F32),
        compiler_params=_cparams(1),
        name="moe_experts",
    )(tables["block_expert"], tables["expert_order"], tables["n_valid"], xs, w_gu,
      b_gu.reshape(DEPTH, N_EXPERTS, 1, 2 * D_EXPERT), w_down, b_down.reshape(DEPTH, N_EXPERTS, 1, D_MODEL))


def _combine_kernel(start_ref, len_ref, off_ref, y_hbm, x1_ref, ri_ref, rf_ref, mod_ref,
                    lng_ref, lnb_ref, oc_ref, od_ref, *scratch):
    bufs, sem = scratch[:RT_PER_STEP], scratch[RT_PER_STEP]
    i = pl.program_id(0)
    n_steps = pl.num_programs(0)

    def fetch(tile, t):
        _tile_run_copies(tile, start_ref, len_ref, off_ref, y_hbm, bufs[t], sem.at[t], to_sorted=False)

    @pl.when(i == 0)
    def _():
        for t in range(FETCH_AHEAD):
            fetch(t, t)

    m = mod_ref[pl.ds(_mod_row(i * RT_PER_STEP // RT_PER_TM), 1), :]
    g2 = m[:, 5 * D_MODEL:6 * D_MODEL]
    col = lax.broadcasted_iota(I32, (RT, RT_ROWS), 1).astype(F32)
    outs = []
    for t in range(RT_PER_STEP):
        rows = slice(t * RT, (t + 1) * RT)
        fetch(jnp.minimum(i * RT_PER_STEP + t + FETCH_AHEAD, N_RT - 1), (t + FETCH_AHEAD) % RT_PER_STEP)
        _tile_runs_wait(bufs[t], sem.at[t])
        pos = _packed_positions(ri_ref[rows, :])
        gates = rf_ref[rows, :]
        mix = jnp.zeros((RT, RT_ROWS), F32)
        for k in range(TOP_K):
            mix = mix + jnp.where(col == pos[k], gates[:, k:k + 1], 0.0)
        ffn = jnp.dot(mix.astype(BF16), _slab_columns(bufs[t], 0, RT_ROWS).astype(BF16),
                      preferred_element_type=F32)
        outs.append(_layer_norm_rows(DN_ALPHA * x1_ref[rows, :] + g2 * ffn, lng_ref[...], lnb_ref[...]))
    out = jnp.concatenate(outs, axis=0)
    n_ctx_steps = N_CTX // (RT * RT_PER_STEP)

    @pl.when(i < n_ctx_steps)
    def _():
        oc_ref[...] = out

    @pl.when(i >= n_ctx_steps)
    def _():
        od_ref[...] = out

    @pl.when(i == n_steps - 1)
    def _():
        for t in range(FETCH_AHEAD):
            _tile_runs_wait(bufs[t], sem.at[t])


def _combine(tables, y, x1, route_i, route_f, mod, ln_g, ln_b, li):
    rows = RT * RT_PER_STEP
    grid_spec = pltpu.PrefetchScalarGridSpec(
        num_scalar_prefetch=3,
        grid=(N_RT // RT_PER_STEP,),
        in_specs=[
            pl.BlockSpec(memory_space=pl.ANY),
            pl.BlockSpec((rows, D_MODEL), lambda i, *_: (i, 0)),
            pl.BlockSpec((rows, LANES), lambda i, *_: (i, 0)),
            pl.BlockSpec((rows, LANES), lambda i, *_: (i, 0)),
            pl.BlockSpec((None, SUBLANES, 6 * D_MODEL), lambda i, *_: (li, 0, 0)),
            _layer_block(li, (1, D_MODEL), 1),
            _layer_block(li, (1, D_MODEL), 1),
        ],
        out_specs=_pair_specs(rows, D_MODEL, N_CTX // rows),
        scratch_shapes=[pltpu.VMEM((RT_ROWS * ROW_SLABS, LANES), F32)] * RT_PER_STEP + [
            pltpu.SemaphoreType.DMA((RT_PER_STEP,)),
        ],
    )
    return pl.pallas_call(
        _combine_kernel,
        grid_spec=grid_spec,
        out_shape=[jax.ShapeDtypeStruct((N_CTX, D_MODEL), F32), jax.ShapeDtypeStruct((N_DEC, D_MODEL), F32)],
        compiler_params=_cparams(1),
        name="moe_combine",
    )(tables["start"], tables["len"], tables["off"], y, x1, route_i, route_f, mod, ln_g, ln_b)


def _routing_tables(tile_cnt):
    cnt = tile_cnt[:, 0, :N_EXPERTS]
    totals = jnp.sum(cnt, axis=0)
    padded = (totals + BM - 1) // BM * BM
    pends = jnp.cumsum(padded)
    pstarts = pends - padded
    start = pstarts[None, :] + jnp.cumsum(cnt, axis=0) - cnt
    off = jnp.cumsum(cnt, axis=1) - cnt
    n_valid = (pends[-1] // BM).astype(I32)
    block_start = jnp.arange(NB, dtype=I32) * BM
    block_e = jnp.minimum(jnp.sum(block_start[:, None] >= pends[None, :], axis=1), N_EXPERTS - 1)
    last_e = block_e[jnp.maximum(n_valid - 1, 0)]
    block_e = jnp.where(jnp.arange(NB) < n_valid, block_e, last_e)
    ids = jnp.arange(N_EXPERTS, dtype=I32)
    order = jnp.sort(jnp.where(totals > 0, ids, N_EXPERTS))
    order = jnp.concatenate([jnp.where(order < N_EXPERTS, order, -1), jnp.full((2,), -1, I32)])
    return {
        "start": (start.reshape(-1) * ROW_SLABS).astype(I32),
        "len": (cnt.reshape(-1) * ROW_SLABS).astype(I32),
        "off": (off.reshape(-1) * ROW_SLABS).astype(I32),
        "pad_start": ((pstarts + totals) * ROW_SLABS).astype(I32),
        "pad_len": ((padded - totals) * ROW_SLABS).astype(I32),
        "n_valid": n_valid.reshape(1),
        "block_expert": block_e.astype(I32),
        "expert_order": order.astype(I32),
    }


def _stacked_params(p):
    eye = jnp.eye(N_LRU_BLOCKS, dtype=F32)
    pad = LANES - N_EXPERTS
    return {
        "lam": p["lam"],
        "subln_col": p["subln_g"].reshape(DEPTH, HEAD_DIM, 1),
        "sgu_ln_g": p["sgu_ln_g"].reshape(DEPTH, 1, SGU_WIDTH),
        "sgu_ln_b": p["sgu_ln_b"].reshape(DEPTH, 1, SGU_WIDTH),
        "w_spatial": p["w_spatial"],
        "b_spatial_full": jnp.repeat(jnp.swapaxes(p["b_spatial"], 1, 2), SGU_GROUP, axis=2),
        "conv_w": p["conv_w"],
        "conv_b": p["conv_b"].reshape(DEPTH, 1, LRU_WIDTH),
        "w_rg_full": jnp.einsum("ldkgio,gh->lgidkho", p["w_rg"], eye).reshape(DEPTH, LRU_WIDTH, 4 * LRU_WIDTH),
        "b_rg_full": p["b_rg"].reshape(DEPTH, 1, 4 * LRU_WIDTH),
        "lru_log": p["lru_log"],
        "w_out_all": p["w_out"],
        "ln_g": p["ln_g"].reshape(DEPTH, 2, 1, D_MODEL),
        "ln_b": p["ln_b"].reshape(DEPTH, 2, 1, D_MODEL),
        "w_router_pad": jnp.pad(p["w_router"], ((0, 0), (0, 0), (0, pad))),
        "b_router_pad": jnp.pad(p["b_router"], ((0, 0), (0, pad))).reshape(DEPTH, 1, LANES),
    }


def kernel(x_prompt, x_sample, cache_k, cache_v, state_lru, c, c_ctx, w_mod, b_mod, w_in, lam, subln_g, sgu_ln_g, sgu_ln_b, w_spatial, b_spatial, conv_w, conv_b, w_rg, b_rg, lru_log, w_out, ln_g, ln_b, w_router, b_router, w_gu, b_gu, w_down, b_down):
    lp = _stacked_params(dict(
        lam=lam, subln_g=subln_g, sgu_ln_g=sgu_ln_g, sgu_ln_b=sgu_ln_b, w_spatial=w_spatial, b_spatial=b_spatial,
        conv_w=conv_w, conv_b=conv_b, w_rg=w_rg, b_rg=b_rg, lru_log=lru_log, w_out=w_out, ln_g=ln_g, ln_b=ln_b,
        w_router=w_router, b_router=b_router))
    cvec8 = jnp.concatenate([c_ctx[None, :], c, jnp.zeros((SUBLANES - 1 - DEC_BATCH, D_MODEL), F32)], axis=0)
    mod = _modulation(cvec8, w_mod, b_mod)
    x_pair = (x_prompt.reshape(N_CTX, D_MODEL), x_sample.reshape(N_DEC, D_MODEL))
    tables = _rope_tables()
    zero_state = jnp.zeros((BATCH, 1, 2, LRU_WIDTH), F32)
    prev_kv = None
    new_s = []
    for li in range(DEPTH):
        lam_init = 0.8 - 0.6 * math.exp(-0.3 * li)
        proj = _input_projection(*x_pair, mod, w_in, li)
        att_ctx, kc, vc = _context_attention(proj, lp["lam"], lp["subln_col"], li, lam_init, prev_kv)
        prev_kv = (kc, vc)
        att_dec = _denoise_attention(proj, cache_k, cache_v, lp["lam"], lp["subln_col"], li, lam_init, tables)
        sgu_ctx, lru_ctx, h_ctx = _mixers(proj, zero_state, 0, lp, li, SEQ, BATCH, 0)
        sgu_dec, lru_dec, _ = _mixers(proj, state_lru, li, lp, li, DEC_SEQ, DEC_BATCH, N_CTX // DEC_SEQ)
        x1, h2, route_i, route_f, tile_cnt = _output_projection(
            x_pair, (att_ctx, att_dec), (sgu_ctx, sgu_dec), (lru_ctx, lru_dec), mod, lp, li)
        rt = _routing_tables(tile_cnt)
        xs = _dispatch(rt, h2, route_i)
        y = _moe_blocks(rt, xs, w_gu, b_gu, w_down, b_down, li)
        x_pair = _combine(rt, y, x1, route_i, route_f, mod, lp["ln_g"], lp["ln_b"], li)
        new_s.append(h_ctx)
    y_prompt = x_pair[0].reshape(BATCH, SEQ, D_MODEL)
    y_sample = x_pair[1].reshape(DEC_BATCH, DEC_SEQ, D_MODEL)
    return (y_prompt, y_sample, prev_kv[0], prev_kv[1], jnp.stack(new_s, axis=1))
```

```python
import functools
import math

import numpy as np
import jax
import jax.numpy as jnp
from jax import lax
from jax.experimental import pallas as pl
from jax.experimental.pallas import tpu as pltpu

F32 = jnp.float32
BF16 = jnp.bfloat16
I32 = jnp.int32

D_MODEL = 1024
BATCH = 32
SEQ = 256
DEPTH = 2
DEC_BATCH = 2
DEC_SEQ = 1024
PAST_LEN = 256
GRID_W = 64
HEAD_DIM = 64
ATT_WIDTH = D_MODEL // 2
SGU_WIDTH = D_MODEL // 4
LRU_WIDTH = D_MODEL // 4
N_ATT_HEADS = ATT_WIDTH // HEAD_DIM
ATT_HALF = HEAD_DIM // 2
ROPE_FREQS = ATT_HALF // 4
ROPE_THETA = 10000.0
CHUNK = 128
N_SGU_GROUPS = 4
SGU_GROUP = SGU_WIDTH // N_SGU_GROUPS
N_LRU_BLOCKS = 4
LRU_BLOCK = LRU_WIDTH // N_LRU_BLOCKS
CONV_W = 4
LRU_C = 8.0
IN_COLS = 3 * ATT_WIDTH + 2 * SGU_WIDTH + 2 * LRU_WIDTH
N_EXPERTS = 32
TOP_K = 4
D_EXPERT = D_MODEL
SWIGLU_LIMIT = 7.0
SWIGLU_ALPHA = 1.702
DN_ALPHA = (2 * DEPTH) ** 0.25
EPS = 1e-5

N_CTX = BATCH * SEQ
N_DEC = DEC_BATCH * DEC_SEQ
N_TOK = N_CTX + N_DEC

LANES = 128
SUBLANES = 8
ROW_SLABS = D_MODEL // LANES

TM = 512
CTX_TILES = N_CTX // TM
TILES_PER_DEC = DEC_SEQ // TM
N_TILES = N_TOK // TM
MOD_TN = 1024
RT = 256
N_RT = N_TOK // RT
RT_PER_TM = TM // RT
RT_ROWS = RT * TOP_K
RT_PER_STEP = 4
FETCH_AHEAD = 2
BM = 256
NB = N_TOK * TOP_K // BM + N_EXPERTS
WEIGHT_PIECES = 8
X_BUFFERS = 4
N_SLOTS = NB * BM
RUN_BITS = tuple(1 << b for b in range(RT.bit_length() - 1, -1, -1))
PAD_BITS = tuple(1 << b for b in range(BM.bit_length() - 2, -1, -1))
VMEM_LIMIT = 56 * 1024 * 1024


def _cparams(n_axes):
    return pltpu.CompilerParams(
        dimension_semantics=("arbitrary",) * n_axes,
        vmem_limit_bytes=VMEM_LIMIT)


def _mod_row(i):
    return jnp.where(i < CTX_TILES, 0, 1 + (i - CTX_TILES) // TILES_PER_DEC)


def _layer_norm_rows(z, g, b):
    mu = jnp.mean(z, axis=-1, keepdims=True)
    zc = z - mu
    var = jnp.mean(zc * zc, axis=-1, keepdims=True)
    return zc * lax.rsqrt(var + EPS) * g + b


def _pair_specs(tile, width, n_ctx_tiles):
    ctx = pl.BlockSpec((tile, width), lambda i, *_: (jnp.minimum(i, n_ctx_tiles - 1), 0))
    dec = pl.BlockSpec((tile, width), lambda i, *_: (jnp.maximum(i - n_ctx_tiles, 0), 0))
    return [ctx, dec]


def _pair_read(i, n_ctx_tiles, ctx_ref, dec_ref):
    return jnp.where(i < n_ctx_tiles, ctx_ref[...], dec_ref[...])


def _layer_block(li, shape, *lead):
    index = (li,) + tuple(lead) + (0,) * len(shape)
    return pl.BlockSpec((None,) * (1 + len(lead)) + tuple(shape), lambda *_: index)


def _mod_kernel(cvec_ref, w_ref, b_ref, o_ref):
    cv = cvec_ref[...]
    s = cv * jax.nn.sigmoid(cv)
    s_t = s.T
    w = w_ref[0]
    rows = [jnp.sum(s_t[:, r:r + 1] * w, axis=0, keepdims=True) for r in range(1 + DEC_BATCH)]
    rows.append(jnp.zeros((SUBLANES - 1 - DEC_BATCH, MOD_TN), F32))
    o_ref[0] = jnp.concatenate(rows, axis=0) + b_ref[0]


def _modulation(cvec8, w_mod, b_mod):
    n_out = w_mod.shape[-1]
    return pl.pallas_call(
        _mod_kernel,
        grid=(DEPTH, n_out // MOD_TN),
        in_specs=[
            pl.BlockSpec((SUBLANES, D_MODEL), lambda l, j: (0, 0)),
            pl.BlockSpec((1, D_MODEL, MOD_TN), lambda l, j: (l, 0, j)),
            pl.BlockSpec((1, 1, MOD_TN), lambda l, j: (l, 0, j)),
        ],
        out_specs=pl.BlockSpec((1, SUBLANES, MOD_TN), lambda l, j: (l, 0, j)),
        out_shape=jax.ShapeDtypeStruct((DEPTH, SUBLANES, n_out), F32),
        compiler_params=_cparams(2),
        name="modulation",
    )(cvec8, w_mod, b_mod.reshape(DEPTH, 1, n_out))


def _inproj_kernel(xc_ref, xd_ref, mod_ref, w_ref, o_ref, wbf_ref):
    i = pl.program_id(0)

    @pl.when(i == 0)
    def _():
        wbf_ref[...] = w_ref[...].astype(BF16)

    m = mod_ref[pl.ds(_mod_row(i), 1), :]
    sh1 = m[:, 0:D_MODEL]
    sc1 = m[:, D_MODEL:2 * D_MODEL]
    x = _pair_read(i, CTX_TILES, xc_ref, xd_ref)
    h = (x * (1.0 + sc1) + sh1).astype(BF16)
    o_ref[...] = jnp.dot(h, wbf_ref[...], preferred_element_type=F32)


def _input_projection(x_ctx, x_dec, mod, w_in, li):
    return pl.pallas_call(
        _inproj_kernel,
        grid=(N_TILES,),
        in_specs=_pair_specs(TM, D_MODEL, CTX_TILES) + [
            pl.BlockSpec((None, SUBLANES, 6 * D_MODEL), lambda i: (li, 0, 0)),
            pl.BlockSpec((None, D_MODEL, IN_COLS), lambda i: (li, 0, 0), pipeline_mode=pl.Buffered(1)),
        ],
        out_specs=pl.BlockSpec((TM, IN_COLS), lambda i: (i, 0)),
        out_shape=jax.ShapeDtypeStruct((N_TOK, IN_COLS), F32),
        scratch_shapes=[pltpu.VMEM((D_MODEL, IN_COLS), BF16)],
        compiler_params=_cparams(1),
        name="input_projection",
    )(x_ctx, x_dec, mod, w_in)


def _lambda(lam_ref, lam_init):
    lm = lam_ref[...]
    a = jnp.sum(lm[0:1] * lm[1:2], axis=-1, keepdims=True)
    b = jnp.sum(lm[2:3] * lm[3:4], axis=-1, keepdims=True)
    return jnp.exp(a) - jnp.exp(b) + lam_init


_NT = (((1,), (1,)), ((), ()))
_TN = (((0,), (0,)), ((), ()))
_ATT_SCALE = ATT_HALF ** -0.5
_Q_SCALE = _ATT_SCALE * math.log2(math.e)


def _attention_heads(pairs, values, e_scr, lam, g_col, lam_init):
    n_heads = len(pairs)
    n_q = e_scr.shape[2]

    def scores(keys, query):
        return lax.dot_general(keys, query, _NT, preferred_element_type=F32)

    tops = []
    for h in range(n_heads):
        for c in range(2):
            seg_tops = [jnp.max(scores(k, q).reshape(k.shape[0] // SUBLANES, SUBLANES, n_q), axis=0)
                        for k, q in pairs[h][c]]
            tops.append(jnp.max(functools.reduce(jnp.maximum, seg_tops), axis=0, keepdims=True))
    for h in range(n_heads):
        for c in range(2):
            n = 2 * h + c
            row = 0
            for k, q in pairs[h][c]:
                e_scr[n, row:row + k.shape[0], :] = jnp.exp2(scores(k, q) - tops[n]).astype(BF16)
                row += k.shape[0]
    outs = []
    for h in range(n_heads):
        v_aug = jnp.concatenate([values[h], jnp.ones_like(values[h])], axis=-1)
        normed = []
        for c in range(2):
            acc = lax.dot_general(v_aug, e_scr[2 * h + c], _TN, preferred_element_type=F32)
            normed.append(acc[:HEAD_DIM] * (1.0 / acc[HEAD_DIM:HEAD_DIM + 1]))
        o_t = normed[0] - lam * normed[1]
        ms = jnp.mean(o_t * o_t, axis=0, keepdims=True)
        outs.append(o_t * lax.rsqrt(ms + EPS) * g_col * (1.0 - lam_init))
    return outs


def _component_query(q_window, c, lo):
    lane = lax.broadcasted_iota(I32, q_window.shape, 1)
    first = lo + c * ATT_HALF
    inside = jnp.logical_and(lane >= first, lane < first + ATT_HALF)
    return jnp.where(inside, q_window, jnp.zeros_like(q_window))


def _lane_tile(x, h):
    heads_per_tile = LANES // HEAD_DIM
    t = h // heads_per_tile
    return x[:, t * LANES:(t + 1) * LANES], (h % heads_per_tile) * HEAD_DIM


def _ctx_attn_kernel(lam_ref, g_ref, q_ref, k_ref, v_ref, *rest, lam_init, stacked):
    *rest, e_scr = rest
    if stacked:
        pk_ref, pv_ref, att_ref, ck_ref, cv_ref = rest
        ck_ref[0, 0] = pk_ref[0]
        cv_ref[0, 0] = pv_ref[0]
        put_k = lambda h, val: ck_ref.__setitem__((0, 1, h), val)
        put_v = lambda h, val: cv_ref.__setitem__((0, 1, h), val)
    else:
        att_ref, ck_ref, cv_ref = rest
        put_k = lambda h, val: ck_ref.__setitem__((0, h), val)
        put_v = lambda h, val: cv_ref.__setitem__((0, h), val)
    lam = _lambda(lam_ref, lam_init)
    k = k_ref[...]
    v = v_ref[...]
    qb = (q_ref[...] * _Q_SCALE).astype(BF16)
    kb = k.astype(BF16)
    vb = v.astype(BF16)
    heads = range(N_ATT_HEADS)
    pairs = []
    for h in heads:
        (k_tile, lo), (q_tile, _) = _lane_tile(kb, h), _lane_tile(qb, h)
        pairs.append([[(k_tile, _component_query(q_tile, c, lo))] for c in range(2)])
    outs = _attention_heads(pairs, [vb[:, h * HEAD_DIM:(h + 1) * HEAD_DIM] for h in heads],
                            e_scr, lam, g_ref[...], lam_init)
    for h in heads:
        put_k(h, k[:, h * HEAD_DIM:(h + 1) * HEAD_DIM])
        put_v(h, v[:, h * HEAD_DIM:(h + 1) * HEAD_DIM])
    att_ref[...] = jnp.concatenate(outs, axis=0).T


def _context_attention(proj, lam, subln_col, li, lam_init, prev_kv=None):
    stacked = prev_kv is not None
    per_layer = pl.BlockSpec((1, N_ATT_HEADS, SEQ, HEAD_DIM), lambda b: (b, 0, 0, 0))
    if stacked:
        kv_shape = jax.ShapeDtypeStruct((BATCH, DEPTH, N_ATT_HEADS, SEQ, HEAD_DIM), F32)
        kv_spec = pl.BlockSpec((1, DEPTH, N_ATT_HEADS, SEQ, HEAD_DIM), lambda b: (b, 0, 0, 0, 0))
    else:
        kv_shape = jax.ShapeDtypeStruct((BATCH, N_ATT_HEADS, SEQ, HEAD_DIM), F32)
        kv_spec = per_layer
    return pl.pallas_call(
        functools.partial(_ctx_attn_kernel, lam_init=lam_init, stacked=stacked),
        grid=(BATCH,),
        in_specs=[
            _layer_block(li, (4, ATT_HALF)),
            _layer_block(li, (HEAD_DIM, 1)),
            pl.BlockSpec((SEQ, ATT_WIDTH), lambda b: (b, 0)),
            pl.BlockSpec((SEQ, ATT_WIDTH), lambda b: (b, 1)),
            pl.BlockSpec((SEQ, ATT_WIDTH), lambda b: (b, 2)),
        ] + ([per_layer, per_layer] if stacked else []),
        out_specs=[pl.BlockSpec((SEQ, ATT_WIDTH), lambda b: (b, 0)), kv_spec, kv_spec],
        out_shape=[jax.ShapeDtypeStruct((N_CTX, ATT_WIDTH), F32), kv_shape, kv_shape],
        scratch_shapes=[pltpu.VMEM((2 * N_ATT_HEADS, SEQ, SEQ), BF16)],
        compiler_params=_cparams(1),
        name="context_attention",
    )(lam, subln_col, proj, proj, proj, *(prev_kv if stacked else ()))


def _rope_tables():
    t = np.arange(DEC_SEQ)
    pos = np.stack([t // GRID_W, t % GRID_W], axis=1).astype(np.float32)
    inv = (np.float32(ROPE_THETA) ** (-np.arange(ROPE_FREQS, dtype=np.float32) / np.float32(ROPE_FREQS)))
    j = np.arange(HEAD_DIM)
    d = j % ATT_HALF
    axis = d // (2 * ROPE_FREQS)
    u = d % (2 * ROPE_FREQS)
    ang = pos[:, axis] * inv[u % ROPE_FREQS][None, :].astype(np.float32)
    cos = np.cos(ang).astype(np.float32)
    sin = np.sin(ang).astype(np.float32)
    first = (u < ROPE_FREQS)[None, :]
    s_next = np.where(first, -sin, 0.0).astype(np.float32)
    s_prev = np.where(first, 0.0, sin).astype(np.float32)
    tile = lambda a: jnp.asarray(np.tile(a, (1, N_ATT_HEADS)))
    return tile(cos), tile(s_next), tile(s_prev)


def _rotate(x, cos, s_next, s_prev):
    width = x.shape[-1]
    return (x * cos + pltpu.roll(x, width - ROPE_FREQS, axis=1) * s_next
            + pltpu.roll(x, ROPE_FREQS, axis=1) * s_prev)


QB = 256
Q_STEPS = DEC_SEQ // QB
DEC_HEAD_GROUP = 4


def _dec_attn_kernel(lam_ref, g_ref, q_ref, k_ref, v_ref, ck_ref, cv_ref,
                     cq_ref, snq_ref, spq_ref, ck_tab, snk_tab, spk_tab,
                     att_ref, krot_ref, e_scr, *, lam_init):
    j = pl.program_id(1)

    @pl.when(j == 0)
    def _():
        krot_ref[...] = _rotate(k_ref[...], ck_tab[...], snk_tab[...], spk_tab[...]).astype(BF16)

    lam = _lambda(lam_ref, lam_init)
    qb = (_rotate(q_ref[...], cq_ref[...], snq_ref[...], spq_ref[...]) * _Q_SCALE).astype(BF16)
    kb = krot_ref[...]
    vb = v_ref[...].astype(BF16)
    outs = []
    for h0 in range(0, N_ATT_HEADS, DEC_HEAD_GROUP):
        heads = range(h0, h0 + DEC_HEAD_GROUP)
        pairs = []
        for h in heads:
            past_k = ck_ref[0, 0, h].astype(BF16)
            q_head = qb[:, h * HEAD_DIM:(h + 1) * HEAD_DIM]
            (k_tile, lo), (q_tile, _) = _lane_tile(kb, h), _lane_tile(qb, h)
            pairs.append([[(past_k, _component_query(q_head, c, 0)), (k_tile, _component_query(q_tile, c, lo))]
                          for c in range(2)])
        values = [jnp.concatenate([cv_ref[0, 0, h].astype(BF16), vb[:, h * HEAD_DIM:(h + 1) * HEAD_DIM]], axis=0)
                  for h in heads]
        outs += _attention_heads(pairs, values, e_scr, lam, g_ref[...], lam_init)
    att_ref[...] = jnp.concatenate(outs, axis=0).T


def _denoise_attention(proj, cache_k, cache_v, lam, subln_col, li, lam_init, tables):
    cos, s_next, s_prev = tables
    row0 = N_CTX // QB
    seq0 = N_CTX // DEC_SEQ
    q_tab = pl.BlockSpec((QB, ATT_WIDTH), lambda b, j: (j, 0))
    k_tab = pl.BlockSpec((DEC_SEQ, ATT_WIDTH), lambda b, j: (0, 0))
    cache_spec = pl.BlockSpec((1, 1, N_ATT_HEADS, PAST_LEN, HEAD_DIM), lambda b, j: (b, li, 0, 0, 0))
    return pl.pallas_call(
        functools.partial(_dec_attn_kernel, lam_init=lam_init),
        grid=(DEC_BATCH, Q_STEPS),
        in_specs=[
            _layer_block(li, (4, ATT_HALF)),
            _layer_block(li, (HEAD_DIM, 1)),
            pl.BlockSpec((QB, ATT_WIDTH), lambda b, j: (row0 + b * Q_STEPS + j, 0)),
            pl.BlockSpec((DEC_SEQ, ATT_WIDTH), lambda b, j: (seq0 + b, 1)),
            pl.BlockSpec((DEC_SEQ, ATT_WIDTH), lambda b, j: (seq0 + b, 2)),
            cache_spec, cache_spec,
            q_tab, q_tab, q_tab, k_tab, k_tab, k_tab,
        ],
        out_specs=pl.BlockSpec((QB, ATT_WIDTH), lambda b, j: (b * Q_STEPS + j, 0)),
        out_shape=jax.ShapeDtypeStruct((N_DEC, ATT_WIDTH), F32),
        scratch_shapes=[pltpu.VMEM((DEC_SEQ, ATT_WIDTH), BF16),
                        pltpu.VMEM((2 * DEC_HEAD_GROUP, PAST_LEN + DEC_SEQ, QB), BF16)],
        compiler_params=_cparams(2),
        name="denoise_attention",
    )(lam, subln_col, proj, proj, proj, cache_k, cache_v,
      cos, s_next, s_prev, cos, s_next, s_prev)


def _softplus(z):
    return jnp.maximum(z, 0.0) + jnp.log1p(jnp.exp(-jnp.abs(z)))


def _mixer_kernel(su_ref, sv_ref, rx_ref, rg_ref, h0_ref, lng_ref, lnb_ref, ws_ref, bs_ref,
                  cw_ref, cb_ref, wrg_ref, brg_ref, lrulog_ref,
                  sgu_ref, lru_ref, hlast_ref, a_scr, b_scr, h_scr, *, seq_len):
    vn = _layer_norm_rows(sv_ref[...], lng_ref[...], lnb_ref[...])
    lane_group = lax.broadcasted_iota(I32, (CHUNK, SGU_WIDTH), 1) // SGU_GROUP
    for n in range(seq_len // CHUNK):
        rows = slice(n * CHUNK, (n + 1) * CHUNK)
        vc = vn[rows].astype(BF16)
        s = jnp.zeros((CHUNK, SGU_WIDTH), F32)
        for g in range(N_SGU_GROUPS):
            sg = jnp.dot(ws_ref[g].astype(BF16), vc, preferred_element_type=F32)
            s = jnp.where(lane_group == g, sg, s)
        sgu_ref[rows, :] = su_ref[rows, :] * (s + bs_ref[...])

    x = rx_ref[...]
    row = lax.broadcasted_iota(I32, (seq_len, LRU_WIDTH), 0)

    def shifted(val, d, fill):
        rolled = pltpu.roll(val, d % seq_len, axis=0)
        inside = (row >= d) if d > 0 else (row < seq_len + d)
        return jnp.where(inside, rolled, fill)

    left = CONV_W // 2
    xc = cb_ref[...] + x * cw_ref[left:left + 1, :]
    for tap in range(CONV_W):
        if tap != left:
            xc = xc + shifted(x, left - tap, 0.0) * cw_ref[tap:tap + 1, :]
    pre = jnp.dot(xc.astype(BF16), wrg_ref[...].astype(BF16), preferred_element_type=F32) + brg_ref[...]
    gates = 0.5 + 0.5 * jnp.tanh(0.5 * pre)
    in_chunk = row % SUBLANES
    n_chunks = seq_len // SUBLANES
    h0 = h0_ref[0, 0]
    lasts = []
    for direction in range(2):
        reverse = direction == 1
        base = direction * 2 * LRU_WIDTH
        r = gates[:, base:base + LRU_WIDTH]
        gi = gates[:, base + LRU_WIDTH:base + 2 * LRU_WIDTH]
        log_a = -LRU_C * r * _softplus(-lrulog_ref[direction:direction + 1, :])
        a = jnp.exp(log_a)
        b = jnp.sqrt(-jnp.tanh(log_a) * (a * a + 1.0)) * gi * xc
        def chunk_roll(val, shift):
            chunks = val.reshape(n_chunks, SUBLANES, LRU_WIDTH)
            return pltpu.roll(chunks, shift, axis=1).reshape(seq_len, LRU_WIDTH)

        for d in (1, 2, 4):
            if reverse:
                inside = in_chunk < SUBLANES - d
                a_n = jnp.where(inside, chunk_roll(a, SUBLANES - d), 1.0)
                b_n = jnp.where(inside, chunk_roll(b, SUBLANES - d), 0.0)
            else:
                inside = in_chunk >= d
                a_n = jnp.where(inside, chunk_roll(a, d), 1.0)
                b_n = jnp.where(inside, chunk_roll(b, d), 0.0)
            b = a * b_n + b
            a = a * a_n
        a_scr[...] = a
        b_scr[...] = b

        def chunk_step(c, carry, reverse=reverse):
            cc = n_chunks - 1 - c if reverse else c
            off = pl.multiple_of(cc * SUBLANES, SUBLANES)
            hc = a_scr[pl.ds(off, SUBLANES), :] * carry + b_scr[pl.ds(off, SUBLANES), :]
            if reverse:
                h_scr[pl.ds(off, SUBLANES), :] = h_scr[pl.ds(off, SUBLANES), :] + hc
                return hc[0:1, :]
            h_scr[pl.ds(off, SUBLANES), :] = hc
            return hc[SUBLANES - 1:SUBLANES, :]

        lasts.append(lax.fori_loop(0, n_chunks, chunk_step, h0[direction:direction + 1, :]))
    lru_ref[...] = h_scr[...] * jax.nn.gelu(rg_ref[...])
    hlast_ref[0] = jnp.concatenate(lasts, axis=0)


def _mixers(proj, h0, h0_layer, lp, li, seq_len, n_seq, row_block0):
    col0 = 3 * ATT_WIDTH // SGU_WIDTH
    col = lambda c: pl.BlockSpec((seq_len, SGU_WIDTH), lambda b: (row_block0 + b, col0 + c))
    full = lambda shape: _layer_block(li, shape)
    out_rows = pl.BlockSpec((seq_len, SGU_WIDTH), lambda b: (b, 0))
    return pl.pallas_call(
        functools.partial(_mixer_kernel, seq_len=seq_len),
        grid=(n_seq,),
        in_specs=[
            col(0), col(1), col(2), col(3),
            pl.BlockSpec((1, 1, 2, LRU_WIDTH), lambda b: (b, h0_layer, 0, 0)),
            full((1, SGU_WIDTH)), full((1, SGU_WIDTH)),
            full((N_SGU_GROUPS, CHUNK, CHUNK)), full((CHUNK, SGU_WIDTH)),
            full((CONV_W, LRU_WIDTH)), full((1, LRU_WIDTH)),
            full((LRU_WIDTH, 4 * LRU_WIDTH)), full((1, 4 * LRU_WIDTH)),
            full((2, LRU_WIDTH)),
        ],
        out_specs=[out_rows, out_rows, pl.BlockSpec((1, 2, LRU_WIDTH), lambda b: (b, 0, 0))],
        out_shape=[
            jax.ShapeDtypeStruct((n_seq * seq_len, SGU_WIDTH), F32),
            jax.ShapeDtypeStruct((n_seq * seq_len, LRU_WIDTH), F32),
            jax.ShapeDtypeStruct((n_seq, 2, LRU_WIDTH), F32),
        ],
        scratch_shapes=[pltpu.VMEM((seq_len, LRU_WIDTH), F32)] * 3,
        compiler_params=_cparams(1),
        name="mixers_%d" % seq_len,
    )(proj, proj, proj, proj, h0, lp["sgu_ln_g"], lp["sgu_ln_b"], lp["w_spatial"], lp["b_spatial_full"],
      lp["conv_w"], lp["conv_b"], lp["w_rg_full"], lp["b_rg_full"], lp["lru_log"])


def _outproj_kernel(xc_ref, xd_ref, ac_ref, ad_ref, sc_ref, sd_ref, lc_ref, ld_ref,
                    mod_ref, wout_ref, lng_ref, lnb_ref, wr_ref, br_ref,
                    x1_ref, h2_ref, ri_ref, rf_ref, cnt_ref, wbf_ref):
    i = pl.program_id(0)

    @pl.when(i == 0)
    def _():
        wbf_ref[...] = wout_ref[...].astype(BF16)

    m = mod_ref[pl.ds(_mod_row(i), 1), :]
    g1 = m[:, 2 * D_MODEL:3 * D_MODEL]
    sh2 = m[:, 3 * D_MODEL:4 * D_MODEL]
    sc2 = m[:, 4 * D_MODEL:5 * D_MODEL]
    a0, a1 = ATT_WIDTH, ATT_WIDTH + SGU_WIDTH
    x = _pair_read(i, CTX_TILES, xc_ref, xd_ref)
    att = _pair_read(i, CTX_TILES, ac_ref, ad_ref).astype(BF16)
    sgu = _pair_read(i, CTX_TILES, sc_ref, sd_ref).astype(BF16)
    lru = _pair_read(i, CTX_TILES, lc_ref, ld_ref).astype(BF16)
    mix = (jnp.dot(att, wbf_ref[0:a0, :], preferred_element_type=F32)
           + jnp.dot(sgu, wbf_ref[a0:a1, :], preferred_element_type=F32)
           + jnp.dot(lru, wbf_ref[a1:, :], preferred_element_type=F32))
    x1 = _layer_norm_rows(DN_ALPHA * x + g1 * mix, lng_ref[...], lnb_ref[...])
    x1_ref[...] = x1
    h2 = x1 * (1.0 + sc2) + sh2
    h2_ref[...] = h2.astype(BF16)

    logits = jnp.dot(h2.astype(BF16), wr_ref[...].astype(BF16), preferred_element_type=F32) + br_ref[...]
    lane = lax.broadcasted_iota(I32, (TM, LANES), 1)
    lane_f = lane.astype(F32)
    neg_inf = jnp.float32(-jnp.inf)
    work = jnp.where(lane < N_EXPERTS, logits, neg_inf)
    vals, idxs = [], []
    for _ in range(TOP_K):
        top = jnp.max(work, axis=-1, keepdims=True)
        idx = jnp.min(jnp.where(work == top, lane_f, float(LANES)), axis=-1, keepdims=True)
        vals.append(top)
        idxs.append(idx)
        work = jnp.where(lane_f == idx, neg_inf, work)
    exps = [jnp.exp(v - vals[0]) for v in vals]
    denom = exps[0] + exps[1] + exps[2] + exps[3]
    onehot = jnp.zeros((TM, LANES), F32)
    for idx in idxs:
        onehot = onehot + (lane_f == idx).astype(F32)
    r_i = lax.broadcasted_iota(I32, (RT, RT), 0)
    c_i = lax.broadcasted_iota(I32, (RT, RT), 1)
    tri = (r_i > c_i).astype(F32).astype(BF16)
    upper = (lax.broadcasted_iota(I32, (LANES, LANES), 0)
             < lax.broadcasted_iota(I32, (LANES, LANES), 1)).astype(F32).astype(BF16)
    packed_pos = []
    for t in range(RT_PER_TM):
        hot = onehot[t * RT:(t + 1) * RT]
        total = jnp.broadcast_to(jnp.sum(hot, axis=0, keepdims=True), (SUBLANES, LANES))
        cnt_ref[t] = total.astype(I32)
        run_start = jnp.dot(total.astype(BF16), upper, preferred_element_type=F32)[0:1, :]
        packed_pos.append(jnp.dot(tri, hot.astype(BF16), preferred_element_type=F32) + run_start)
    packed_pos = jnp.concatenate(packed_pos, axis=0)
    ri = jnp.zeros((TM, LANES), F32)
    rf = jnp.zeros((TM, LANES), F32)
    for k in range(TOP_K):
        pos = jnp.sum(jnp.where(lane_f == idxs[k], packed_pos, 0.0), axis=-1, keepdims=True)
        ri = jnp.where(lane == k, idxs[k], ri)
        ri = jnp.where(lane == TOP_K + k, pos, ri)
        rf = jnp.where(lane == k, exps[k] / denom, rf)
    ri_ref[...] = ri.astype(I32)
    rf_ref[...] = rf


def _output_projection(x_pair, att_pair, sgu_pair, lru_pair, mod, lp, li):
    rows = lambda w: pl.BlockSpec((TM, w), lambda i: (i, 0))
    full = lambda shape: _layer_block(li, shape)
    return pl.pallas_call(
        _outproj_kernel,
        grid=(N_TILES,),
        in_specs=(_pair_specs(TM, D_MODEL, CTX_TILES) + _pair_specs(TM, ATT_WIDTH, CTX_TILES)
                  + _pair_specs(TM, SGU_WIDTH, CTX_TILES) + _pair_specs(TM, LRU_WIDTH, CTX_TILES) + [
            pl.BlockSpec((None, SUBLANES, 6 * D_MODEL), lambda i: (li, 0, 0)),
            pl.BlockSpec((None, D_MODEL, D_MODEL), lambda i: (li, 0, 0), pipeline_mode=pl.Buffered(1)),
            _layer_block(li, (1, D_MODEL), 0), _layer_block(li, (1, D_MODEL), 0),
            full((D_MODEL, LANES)), full((1, LANES)),
        ]),
        out_specs=[
            rows(D_MODEL), rows(D_MODEL), rows(LANES), rows(LANES),
            pl.BlockSpec((RT_PER_TM, SUBLANES, LANES), lambda i: (i, 0, 0)),
        ],
        out_shape=[
            jax.ShapeDtypeStruct((N_TOK, D_MODEL), F32),
            jax.ShapeDtypeStruct((N_TOK, D_MODEL), BF16),
            jax.ShapeDtypeStruct((N_TOK, LANES), I32),
            jax.ShapeDtypeStruct((N_TOK, LANES), F32),
            jax.ShapeDtypeStruct((N_RT, SUBLANES, LANES), I32),
        ],
        scratch_shapes=[pltpu.VMEM((D_MODEL, D_MODEL), BF16)],
        compiler_params=_cparams(1),
        name="output_projection",
    )(*x_pair, *att_pair, *sgu_pair, *lru_pair, mod, lp["w_out_all"], lp["ln_g"], lp["ln_b"],
      lp["w_router_pad"], lp["b_router_pad"])


def _row_ds(row, n_rows):
    return pl.ds(pl.multiple_of(row * ROW_SLABS, ROW_SLABS), n_rows * ROW_SLABS)


def _for_each_run_piece(length_s, bits, fn):
    for bit in bits:
        done_s = length_s & (-2 * bit * ROW_SLABS)
        @pl.when((length_s & (bit * ROW_SLABS)) != 0)
        def _(done_s=done_s, bit=bit):
            fn(done_s, bit)


def _stored_ds(row_s, n_rows):
    return pl.ds(pl.multiple_of(row_s, ROW_SLABS), n_rows * ROW_SLABS)


def _tile_run_copies(tile, start_ref, len_ref, off_ref, sorted_hbm, buf, sem, to_sorted, live=None):
    for e in range(N_EXPERTS):
        t = tile * N_EXPERTS + e
        start_s, length_s, off_s = start_ref[t], len_ref[t], off_ref[t]
        if live is not None:
            length_s = jnp.where(live, length_s, 0)

        def piece(done_s, bit, start_s=start_s, off_s=off_s):
            packed = buf.at[_stored_ds(off_s + done_s, bit), :]
            srt = sorted_hbm.at[_stored_ds(start_s + done_s, bit), :]
            if to_sorted:
                pltpu.make_async_copy(packed, srt, sem).start()
            else:
                pltpu.make_async_copy(srt, packed, sem).start()

        _for_each_run_piece(length_s, RUN_BITS, piece)


def _tile_runs_wait(buf, sem):
    pltpu.make_async_copy(buf, buf, sem).wait()


def _packed_positions(ri):
    return [ri[:, TOP_K + k:TOP_K + k + 1].astype(F32) for k in range(TOP_K)]


def _slab_columns(buf, row0, n_rows):
    return jnp.concatenate(
        [buf[pl.ds(row0 * ROW_SLABS + s, n_rows, stride=ROW_SLABS), :] for s in range(ROW_SLABS)], axis=-1)


def _dispatch_kernel(start_ref, len_ref, off_ref, pstart_ref, plen_ref, nv_ref,
                     h2_ref, ri_ref, xs_hbm, *scratch):
    bufs, (zbuf, sem, zsem) = scratch[:RT_PER_STEP], scratch[RT_PER_STEP:]
    i = pl.program_id(0)
    n_steps = pl.num_programs(0)

    @pl.when(i == 0)
    def _():
        zbuf[...] = jnp.zeros_like(zbuf)

        def zero_fill(wait):
            def go(cp):
                cp.wait() if wait else cp.start()

            def per_expert(e, carry):
                def piece(done_s, bit):
                    go(pltpu.make_async_copy(zbuf.at[pl.ds(0, bit * ROW_SLABS), :],
                                             xs_hbm.at[_stored_ds(pstart_ref[e] + done_s, bit), :], zsem))
                _for_each_run_piece(plen_ref[e], PAD_BITS, piece)
                return carry
            lax.fori_loop(0, N_EXPERTS, per_expert, 0)

            def per_block(b, carry):
                go(pltpu.make_async_copy(zbuf, xs_hbm.at[_row_ds(b * BM, BM), :], zsem))
                return carry
            lax.fori_loop(nv_ref[0], NB, per_block, 0)

        zero_fill(False)
        zero_fill(True)

    def send(tile, t, live):
        _tile_run_copies(tile, start_ref, len_ref, off_ref, xs_hbm, bufs[t], sem.at[t], to_sorted=True, live=live)

    col = lax.broadcasted_iota(I32, (RT, RT_ROWS), 1).astype(F32)
    for t in range(RT_PER_STEP):
        buf = bufs[t]
        rows = slice(t * RT, (t + 1) * RT)
        tile = i * RT_PER_STEP + t

        @pl.when(i >= 1)
        def _(buf=buf, t=t):
            _tile_runs_wait(buf, sem.at[t])

        send(jnp.maximum(tile - 1, 0), (t - 1) % RT_PER_STEP, tile >= 1)
        pos = _packed_positions(ri_ref[rows, :])
        sel = jnp.zeros((RT, RT_ROWS), F32)
        for p in pos:
            sel = sel + (col == p).astype(F32)
        packed = lax.dot_general(sel.astype(BF16), h2_ref[rows, :], _TN, preferred_element_type=F32)
        for s in range(ROW_SLABS):
            buf[pl.ds(s, RT_ROWS, stride=ROW_SLABS), :] = packed[:, s * LANES:(s + 1) * LANES]

    @pl.when(i == n_steps - 1)
    def _():
        send(N_RT - 1, RT_PER_STEP - 1, True)
        for t in range(RT_PER_STEP):
            _tile_runs_wait(bufs[t], sem.at[t])


def _dispatch(tables, h2, route_i):
    grid_spec = pltpu.PrefetchScalarGridSpec(
        num_scalar_prefetch=6,
        grid=(N_RT // RT_PER_STEP,),
        in_specs=[
            pl.BlockSpec((RT_PER_STEP * RT, D_MODEL), lambda i, *_: (i, 0)),
            pl.BlockSpec((RT_PER_STEP * RT, LANES), lambda i, *_: (i, 0)),
        ],
        out_specs=pl.BlockSpec(memory_space=pl.ANY),
        scratch_shapes=[pltpu.VMEM((RT_ROWS * ROW_SLABS, LANES), F32)] * RT_PER_STEP + [
            pltpu.VMEM((BM * ROW_SLABS, LANES), F32),
            pltpu.SemaphoreType.DMA((RT_PER_STEP,)),
            pltpu.SemaphoreType.DMA(()),
        ],
    )
    return pl.pallas_call(
        _dispatch_kernel,
        grid_spec=grid_spec,
        out_shape=jax.ShapeDtypeStruct((N_SLOTS * ROW_SLABS, LANES), F32),
        compiler_params=_cparams(1),
        name="moe_dispatch",
    )(tables["start"], tables["len"], tables["off"], tables["pad_start"], tables["pad_len"], tables["n_valid"],
      h2, route_i)


def _moe_kernel(be_ref, eo_ref, nv_ref, xs_hbm, wgu_hbm, bgu_ref, wdn_hbm, bdn_ref, y_hbm,
                xbuf, ybuf, gu_stage, dn_stage, wgu_bf, wdn_bf, wsem, xsem, ysem, *, li):
    n_valid = nv_ref[0]
    rows_per_piece = D_MODEL // WEIGHT_PIECES

    def weight_copies(e, stage):
        cps = []
        for p in range(WEIGHT_PIECES):
            band = pl.ds(p * rows_per_piece, rows_per_piece)
            cps.append(pltpu.make_async_copy(wgu_hbm.at[li, e, band, :], gu_stage.at[stage, band, :], wsem.at[stage]))
            cps.append(pltpu.make_async_copy(wdn_hbm.at[li, e, band, :], dn_stage.at[stage, band, :], wsem.at[stage]))
        return cps

    def fetch(k):
        @pl.when(eo_ref[k] >= 0)
        def _():
            for cp in weight_copies(eo_ref[k], k % 2):
                cp.start(priority=1)

    def x_copy(b, slot):
        return pltpu.make_async_copy(xs_hbm.at[_row_ds(b * BM, BM), :], xbuf.at[slot], xsem.at[slot])

    def y_copy(b, slot):
        return pltpu.make_async_copy(ybuf.at[slot], y_hbm.at[_row_ds(b * BM, BM), :], ysem.at[slot])

    fetch(0)
    fetch(1)
    for ahead in range(X_BUFFERS - 1):
        @pl.when(ahead < n_valid)
        def _(ahead=ahead):
            x_copy(ahead, ahead).start()

    def next_slot(s):
        return jnp.where(s == X_BUFFERS - 1, 0, s + 1)

    def block(b, carry):
        k, xslot = carry
        slot = b % 2
        e = be_ref[b]
        new_expert = jnp.logical_or(b == 0, e != be_ref[jnp.maximum(b - 1, 0)])

        @pl.when(b + X_BUFFERS - 1 < n_valid)
        def _():
            ahead_slot = xslot
            for _ in range(X_BUFFERS - 1):
                ahead_slot = next_slot(ahead_slot)
            x_copy(b + X_BUFFERS - 1, ahead_slot).start()

        @pl.when(new_expert)
        def _():
            stage = k % 2
            for cp in weight_copies(e, stage):
                cp.wait()
            for st in range(2):
                @pl.when(stage == st)
                def _(st=st):
                    wgu_bf[...] = gu_stage[st].astype(BF16)
                    wdn_bf[...] = dn_stage[st].astype(BF16)
            fetch(k + 2)

        x_copy(b, xslot).wait()

        @pl.when(b >= 2)
        def _():
            y_copy(b - 2, slot).wait()

        x = _slab_columns(xbuf.at[xslot], 0, BM).astype(BF16)
        bgu = bgu_ref[e]
        g = jnp.dot(x, wgu_bf[:, :D_EXPERT], preferred_element_type=F32) + bgu[:, :D_EXPERT]
        u = jnp.dot(x, wgu_bf[:, D_EXPERT:], preferred_element_type=F32) + bgu[:, D_EXPERT:]
        g = jnp.minimum(g, SWIGLU_LIMIT)
        u = jnp.clip(u, -SWIGLU_LIMIT, SWIGLU_LIMIT)
        act = ((u + 1.0) * (0.5 * g * (1.0 + jnp.tanh((0.5 * SWIGLU_ALPHA) * g)))).astype(BF16)
        y = jnp.dot(act, wdn_bf[...], preferred_element_type=F32) + bdn_ref[e]
        out = ybuf.at[slot]
        for s in range(ROW_SLABS):
            out[pl.ds(s, BM, stride=ROW_SLABS), :] = y[:, s * LANES:(s + 1) * LANES]
        y_copy(b, slot).start()
        return k + new_expert.astype(I32), next_slot(xslot)

    lax.fori_loop(0, n_valid, block, (jnp.int32(0), jnp.int32(0)))

    @pl.when(n_valid >= 2)
    def _():
        y_copy(n_valid - 2, n_valid % 2).wait()
    y_copy(n_valid - 1, (n_valid - 1) % 2).wait()

    ybuf[0] = jnp.zeros((BM * ROW_SLABS, LANES), F32)

    def zero_blocks(wait):
        def one(b, carry):
            cp = y_copy(b, 0)
            cp.wait() if wait else cp.start()
            return carry
        lax.fori_loop(n_valid, NB, one, 0)

    zero_blocks(False)
    zero_blocks(True)


def _moe_blocks(tables, xs, w_gu, b_gu, w_down, b_down, li):
    grid_spec = pltpu.PrefetchScalarGridSpec(
        num_scalar_prefetch=3,
        grid=(1,),
        in_specs=[
            pl.BlockSpec(memory_space=pl.ANY),
            pl.BlockSpec(memory_space=pl.ANY),
            pl.BlockSpec((None, N_EXPERTS, 1, 2 * D_EXPERT), lambda i, *_: (li, 0, 0, 0)),
            pl.BlockSpec(memory_space=pl.ANY),
            pl.BlockSpec((None, N_EXPERTS, 1, D_MODEL), lambda i, *_: (li, 0, 0, 0)),
        ],
        out_specs=pl.BlockSpec(memory_space=pl.ANY),
        scratch_shapes=[
            pltpu.VMEM((X_BUFFERS, BM * ROW_SLABS, LANES), F32),
            pltpu.VMEM((2, BM * ROW_SLABS, LANES), F32),
            pltpu.VMEM((2, D_MODEL, 2 * D_EXPERT), F32),
            pltpu.VMEM((2, D_EXPERT, D_MODEL), F32),
            pltpu.VMEM((D_MODEL, 2 * D_EXPERT), BF16),
            pltpu.VMEM((D_EXPERT, D_MODEL), BF16),
            pltpu.SemaphoreType.DMA((2,)),
            pltpu.SemaphoreType.DMA((X_BUFFERS,)),
            pltpu.SemaphoreType.DMA((2,)),
        ],
    )
    return pl.pallas_call(
        functools.partial(_moe_kernel, li=li),
        grid_spec=grid_spec,
        out_shape=jax.ShapeDtypeStruct((N_SLOTS * ROW_SLABS, LANES), F32),
        compiler_params=_cparams(1),
        name="moe_experts",
    )(tables["block_expert"], tables["expert_order"], tables["n_valid"], xs, w_gu,
      b_gu.reshape(DEPTH, N_EXPERTS, 1, 2 * D_EXPERT), w_down, b_down.reshape(DEPTH, N_EXPERTS, 1, D_MODEL))


def _combine_kernel(start_ref, len_ref, off_ref, y_hbm, x1_ref, ri_ref, rf_ref, mod_ref,
                    lng_ref, lnb_ref, oc_ref, od_ref, *scratch):
    bufs, sem = scratch[:RT_PER_STEP], scratch[RT_PER_STEP]
    i = pl.program_id(0)
    n_steps = pl.num_programs(0)

    def fetch(tile, t):
        _tile_run_copies(tile, start_ref, len_ref, off_ref, y_hbm, bufs[t], sem.at[t], to_sorted=False)

    @pl.when(i == 0)
    def _():
        for t in range(FETCH_AHEAD):
            fetch(t, t)

    m = mod_ref[pl.ds(_mod_row(i * RT_PER_STEP // RT_PER_TM), 1), :]
    g2 = m[:, 5 * D_MODEL:6 * D_MODEL]
    col = lax.broadcasted_iota(I32, (RT, RT_ROWS), 1).astype(F32)
    outs = []
    for t in range(RT_PER_STEP):
        rows = slice(t * RT, (t + 1) * RT)
        fetch(jnp.minimum(i * RT_PER_STEP + t + FETCH_AHEAD, N_RT - 1), (t + FETCH_AHEAD) % RT_PER_STEP)
        _tile_runs_wait(bufs[t], sem.at[t])
        pos = _packed_positions(ri_ref[rows, :])
        gates = rf_ref[rows, :]
        mix = jnp.zeros((RT, RT_ROWS), F32)
        for k in range(TOP_K):
            mix = mix + jnp.where(col == pos[k], gates[:, k:k + 1], 0.0)
        ffn = jnp.dot(mix.astype(BF16), _slab_columns(bufs[t], 0, RT_ROWS).astype(BF16),
                      preferred_element_type=F32)
        outs.append(_layer_norm_rows(DN_ALPHA * x1_ref[rows, :] + g2 * ffn, lng_ref[...], lnb_ref[...]))
    out = jnp.concatenate(outs, axis=0)
    n_ctx_steps = N_CTX // (RT * RT_PER_STEP)

    @pl.when(i < n_ctx_steps)
    def _():
        oc_ref[...] = out

    @pl.when(i >= n_ctx_steps)
    def _():
        od_ref[...] = out

    @pl.when(i == n_steps - 1)
    def _():
        for t in range(FETCH_AHEAD):
            _tile_runs_wait(bufs[t], sem.at[t])


def _combine(tables, y, x1, route_i, route_f, mod, ln_g, ln_b, li):
    rows = RT * RT_PER_STEP
    grid_spec = pltpu.PrefetchScalarGridSpec(
        num_scalar_prefetch=3,
        grid=(N_RT // RT_PER_STEP,),
        in_specs=[
            pl.BlockSpec(memory_space=pl.ANY),
            pl.BlockSpec((rows, D_MODEL), lambda i, *_: (i, 0)),
            pl.BlockSpec((rows, LANES), lambda i, *_: (i, 0)),
            pl.BlockSpec((rows, LANES), lambda i, *_: (i, 0)),
            pl.BlockSpec((None, SUBLANES, 6 * D_MODEL), lambda i, *_: (li, 0, 0)),
            _layer_block(li, (1, D_MODEL), 1),
            _layer_block(li, (1, D_MODEL), 1),
        ],
        out_specs=_pair_specs(rows, D_MODEL, N_CTX // rows),
        scratch_shapes=[pltpu.VMEM((RT_ROWS * ROW_SLABS, LANES), F32)] * RT_PER_STEP + [
            pltpu.SemaphoreType.DMA((RT_PER_STEP,)),
        ],
    )
    return pl.pallas_call(
        _combine_kernel,
        grid_spec=grid_spec,
        out_shape=[jax.ShapeDtypeStruct((N_CTX, D_MODEL), F32), jax.ShapeDtypeStruct((N_DEC, D_MODEL), F32)],
        compiler_params=_cparams(1),
        name="moe_combine",
    )(tables["start"], tables["len"], tables["off"], y, x1, route_i, route_f, mod, ln_g, ln_b)


def _routing_tables(tile_cnt):
    cnt = tile_cnt[:, 0, :N_EXPERTS]
    totals = jnp.sum(cnt, axis=0)
    padded = (totals + BM - 1) // BM * BM
    pends = jnp.cumsum(padded)
    pstarts = pends - padded
    start = pstarts[None, :] + jnp.cumsum(cnt, axis=0) - cnt
    off = jnp.cumsum(cnt, axis=1) - cnt
    n_valid = (pends[-1] // BM).astype(I32)
    block_start = jnp.arange(NB, dtype=I32) * BM
    block_e = jnp.minimum(jnp.sum(block_start[:, None] >= pends[None, :], axis=1), N_EXPERTS - 1)
    last_e = block_e[jnp.maximum(n_valid - 1, 0)]
    block_e = jnp.where(jnp.arange(NB) < n_valid, block_e, last_e)
    ids = jnp.arange(N_EXPERTS, dtype=I32)
    order = jnp.sort(jnp.where(totals > 0, ids, N_EXPERTS))
    order = jnp.concatenate([jnp.where(order < N_EXPERTS, order, -1), jnp.full((2,), -1, I32)])
    return {
        "start": (start.reshape(-1) * ROW_SLABS).astype(I32),
        "len": (cnt.reshape(-1) * ROW_SLABS).astype(I32),
        "off": (off.reshape(-1) * ROW_SLABS).astype(I32),
        "pad_start": ((pstarts + totals) * ROW_SLABS).astype(I32),
        "pad_len": ((padded - totals) * ROW_SLABS).astype(I32),
        "n_valid": n_valid.reshape(1),
        "block_expert": block_e.astype(I32),
        "expert_order": order.astype(I32),
    }


def _stacked_params(p):
    eye = jnp.eye(N_LRU_BLOCKS, dtype=F32)
    pad = LANES - N_EXPERTS
    return {
        "lam": p["lam"],
        "subln_col": p["subln_g"].reshape(DEPTH, HEAD_DIM, 1),
        "sgu_ln_g": p["sgu_ln_g"].reshape(DEPTH, 1, SGU_WIDTH),
        "sgu_ln_b": p["sgu_ln_b"].reshape(DEPTH, 1, SGU_WIDTH),
        "w_spatial": p["w_spatial"],
        "b_spatial_full": jnp.repeat(jnp.swapaxes(p["b_spatial"], 1, 2), SGU_GROUP, axis=2),
        "conv_w": p["conv_w"],
        "conv_b": p["conv_b"].reshape(DEPTH, 1, LRU_WIDTH),
        "w_rg_full": jnp.einsum("ldkgio,gh->lgidkho", p["w_rg"], eye).reshape(DEPTH, LRU_WIDTH, 4 * LRU_WIDTH),
        "b_rg_full": p["b_rg"].reshape(DEPTH, 1, 4 * LRU_WIDTH),
        "lru_log": p["lru_log"],
        "w_out_all": p["w_out"],
        "ln_g": p["ln_g"].reshape(DEPTH, 2, 1, D_MODEL),
        "ln_b": p["ln_b"].reshape(DEPTH, 2, 1, D_MODEL),
        "w_router_pad": jnp.pad(p["w_router"], ((0, 0), (0, 0), (0, pad))),
        "b_router_pad": jnp.pad(p["b_router"], ((0, 0), (0, pad))).reshape(DEPTH, 1, LANES),
    }


def kernel(x_prompt, x_sample, cache_k, cache_v, state_lru, c, c_ctx, w_mod, b_mod, w_in, lam, subln_g, sgu_ln_g, sgu_ln_b, w_spatial, b_spatial, conv_w, conv_b, w_rg, b_rg, lru_log, w_out, ln_g, ln_b, w_router, b_router, w_gu, b_gu, w_down, b_down):
    lp = _stacked_params(dict(
        lam=lam, subln_g=subln_g, sgu_ln_g=sgu_ln_g, sgu_ln_b=sgu_ln_b, w_spatial=w_spatial, b_spatial=b_spatial,
        conv_w=conv_w, conv_b=conv_b, w_rg=w_rg, b_rg=b_rg, lru_log=lru_log, w_out=w_out, ln_g=ln_g, ln_b=ln_b,
        w_router=w_router, b_router=b_router))
    cvec8 = jnp.concatenate([c_ctx[None, :], c, jnp.zeros((SUBLANES - 1 - DEC_BATCH, D_MODEL), F32)], axis=0)
    mod = _modulation(cvec8, w_mod, b_mod)
    x_pair = (x_prompt.reshape(N_CTX, D_MODEL), x_sample.reshape(N_DEC, D_MODEL))
    tables = _rope_tables()
    zero_state = jnp.zeros((BATCH, 1, 2, LRU_WIDTH), F32)
    prev_kv = None
    new_s = []
    for li in range(DEPTH):
        lam_init = 0.8 - 0.6 * math.exp(-0.3 * li)
        proj = _input_projection(*x_pair, mod, w_in, li)
        att_ctx, kc, vc = _context_attention(proj, lp["lam"], lp["subln_col"], li, lam_init, prev_kv)
        prev_kv = (kc, vc)
        att_dec = _denoise_attention(proj, cache_k, cache_v, lp["lam"], lp["subln_col"], li, lam_init, tables)
        sgu_ctx, lru_ctx, h_ctx = _mixers(proj, zero_state, 0, lp, li, SEQ, BATCH, 0)
        sgu_dec, lru_dec, _ = _mixers(proj, state_lru, li, lp, li, DEC_SEQ, DEC_BATCH, N_CTX // DEC_SEQ)
        x1, h2, route_i, route_f, tile_cnt = _output_projection(
            x_pair, (att_ctx, att_dec), (sgu_ctx, sgu_dec), (lru_ctx, lru_dec), mod, lp, li)
        rt = _routing_tables(tile_cnt)
        xs = _dispatch(rt, h2, route_i)
        y = _moe_blocks(rt, xs, w_gu, b_gu, w_down, b_down, li)
        x_pair = _combine(rt, y, x1, route_i, route_f, mod, lp["ln_g"], lp["ln_b"], li)
        new_s.append(h_ctx)
    y_prompt = x_pair[0].reshape(BATCH, SEQ, D_MODEL)
    y_sample = x_pair[1].reshape(DEC_BATCH, DEC_SEQ, D_MODEL)
    return (y_prompt, y_sample, prev_kv[0], prev_kv[1], jnp.stack(new_s, axis=1))
```

```python
import functools
import math

import numpy as np
import jax
import jax.numpy as jnp
from jax import lax
from jax.experimental import pallas as pl
from jax.experimental.pallas import tpu as pltpu

F32 = jnp.float32
BF16 = jnp.bfloat16
I32 = jnp.int32

D_MODEL = 1024
BATCH = 32
SEQ = 256
DEPTH = 2
DEC_BATCH = 2
DEC_SEQ = 1024
PAST_LEN = 256
GRID_W = 64
HEAD_DIM = 64
ATT_WIDTH = D_MODEL // 2
SGU_WIDTH = D_MODEL // 4
LRU_WIDTH = D_MODEL // 4
N_ATT_HEADS = ATT_WIDTH // HEAD_DIM
ATT_HALF = HEAD_DIM // 2
ROPE_FREQS = ATT_HALF // 4
ROPE_THETA = 10000.0
CHUNK = 128
N_SGU_GROUPS = 4
SGU_GROUP = SGU_WIDTH // N_SGU_GROUPS
N_LRU_BLOCKS = 4
LRU_BLOCK = LRU_WIDTH // N_LRU_BLOCKS
CONV_W = 4
LRU_C = 8.0
IN_COLS = 3 * ATT_WIDTH + 2 * SGU_WIDTH + 2 * LRU_WIDTH
N_EXPERTS = 32
TOP_K = 4
D_EXPERT = D_MODEL
SWIGLU_LIMIT = 7.0
SWIGLU_ALPHA = 1.702
DN_ALPHA = (2 * DEPTH) ** 0.25
EPS = 1e-5

N_CTX = BATCH * SEQ
N_DEC = DEC_BATCH * DEC_SEQ
N_TOK = N_CTX + N_DEC

LANES = 128
SUBLANES = 8
ROW_SLABS = D_MODEL // LANES

TM = 512
CTX_TILES = N_CTX // TM
TILES_PER_DEC = DEC_SEQ // TM
N_TILES = N_TOK // TM
MOD_TN = 1024
RT = 256
N_RT = N_TOK // RT
RT_PER_TM = TM // RT
RT_ROWS = RT * TOP_K
RT_PER_STEP = 4
FETCH_AHEAD = 2
BM = 256
NB = N_TOK * TOP_K // BM + N_EXPERTS
WEIGHT_PIECES = 8
X_BUFFERS = 4
N_SLOTS = NB * BM
RUN_BITS = tuple(1 << b for b in range(RT.bit_length() - 1, -1, -1))
PAD_BITS = tuple(1 << b for b in range(BM.bit_length() - 2, -1, -1))
VMEM_LIMIT = 56 * 1024 * 1024


def _cparams(n_axes):
    return pltpu.CompilerParams(
        dimension_semantics=("arbitrary",) * n_axes,
        vmem_limit_bytes=VMEM_LIMIT)


def _mod_row(i):
    return jnp.where(i < CTX_TILES, 0, 1 + (i - CTX_TILES) // TILES_PER_DEC)


def _layer_norm_rows(z, g, b):
    mu = jnp.mean(z, axis=-1, keepdims=True)
    zc = z - mu
    var = jnp.mean(zc * zc, axis=-1, keepdims=True)
    return zc * lax.rsqrt(var + EPS) * g + b


def _pair_specs(tile, width, n_ctx_tiles):
    ctx = pl.BlockSpec((tile, width), lambda i, *_: (jnp.minimum(i, n_ctx_tiles - 1), 0))
    dec = pl.BlockSpec((tile, width), lambda i, *_: (jnp.maximum(i - n_ctx_tiles, 0), 0))
    return [ctx, dec]


def _pair_read(i, n_ctx_tiles, ctx_ref, dec_ref):
    return jnp.where(i < n_ctx_tiles, ctx_ref[...], dec_ref[...])


def _layer_block(li, shape, *lead):
    index = (li,) + tuple(lead) + (0,) * len(shape)
    return pl.BlockSpec((None,) * (1 + len(lead)) + tuple(shape), lambda *_: index)


def _mod_kernel(cvec_ref, w_ref, b_ref, o_ref):
    cv = cvec_ref[...]
    s = cv * jax.nn.sigmoid(cv)
    s_t = s.T
    w = w_ref[0]
    rows = [jnp.sum(s_t[:, r:r + 1] * w, axis=0, keepdims=True) for r in range(1 + DEC_BATCH)]
    rows.append(jnp.zeros((SUBLANES - 1 - DEC_BATCH, MOD_TN), F32))
    o_ref[0] = jnp.concatenate(rows, axis=0) + b_ref[0]


def _modulation(cvec8, w_mod, b_mod):
    n_out = w_mod.shape[-1]
    return pl.pallas_call(
        _mod_kernel,
        grid=(DEPTH, n_out // MOD_TN),
        in_specs=[
            pl.BlockSpec((SUBLANES, D_MODEL), lambda l, j: (0, 0)),
            pl.BlockSpec((1, D_MODEL, MOD_TN), lambda l, j: (l, 0, j)),
            pl.BlockSpec((1, 1, MOD_TN), lambda l, j: (l, 0, j)),
        ],
        out_specs=pl.BlockSpec((1, SUBLANES, MOD_TN), lambda l, j: (l, 0, j)),
        out_shape=jax.ShapeDtypeStruct((DEPTH, SUBLANES, n_out), F32),
        compiler_params=_cparams(2),
        name="modulation",
    )(cvec8, w_mod, b_mod.reshape(DEPTH, 1, n_out))


def _inproj_kernel(xc_ref, xd_ref, mod_ref, w_ref, o_ref, wbf_ref):
    i = pl.program_id(0)

    @pl.when(i == 0)
    def _():
        wbf_ref[...] = w_ref[...].astype(BF16)

    m = mod_ref[pl.ds(_mod_row(i), 1), :]
    sh1 = m[:, 0:D_MODEL]
    sc1 = m[:, D_MODEL:2 * D_MODEL]
    x = _pair_read(i, CTX_TILES, xc_ref, xd_ref)
    h = (x * (1.0 + sc1) + sh1).astype(BF16)
    o_ref[...] = jnp.dot(h, wbf_ref[...], preferred_element_type=F32)


def _input_projection(x_ctx, x_dec, mod, w_in, li):
    return pl.pallas_call(
        _inproj_kernel,
        grid=(N_TILES,),
        in_specs=_pair_specs(TM, D_MODEL, CTX_TILES) + [
            pl.BlockSpec((None, SUBLANES, 6 * D_MODEL), lambda i: (li, 0, 0)),
            pl.BlockSpec((None, D_MODEL, IN_COLS), lambda i: (li, 0, 0), pipeline_mode=pl.Buffered(1)),
        ],
        out_specs=pl.BlockSpec((TM, IN_COLS), lambda i: (i, 0)),
        out_shape=jax.ShapeDtypeStruct((N_TOK, IN_COLS), F32),
        scratch_shapes=[pltpu.VMEM((D_MODEL, IN_COLS), BF16)],
        compiler_params=_cparams(1),
        name="input_projection",
    )(x_ctx, x_dec, mod, w_in)


def _lambda(lam_ref, lam_init):
    lm = lam_ref[...]
    a = jnp.sum(lm[0:1] * lm[1:2], axis=-1, keepdims=True)
    b = jnp.sum(lm[2:3] * lm[3:4], axis=-1, keepdims=True)
    return jnp.exp(a) - jnp.exp(b) + lam_init


_NT = (((1,), (1,)), ((), ()))
_TN = (((0,), (0,)), ((), ()))
_ATT_SCALE = ATT_HALF ** -0.5
_Q_SCALE = _ATT_SCALE * math.log2(math.e)


def _attention_heads(pairs, values, e_scr, lam, g_col, lam_init):
    n_heads = len(pairs)
    n_q = e_scr.shape[2]

    def scores(keys, query):
        return lax.dot_general(keys, query, _NT, preferred_element_type=F32)

    tops = []
    for h in range(n_heads):
        for c in range(2):
            seg_tops = [jnp.max(scores(k, q).reshape(k.shape[0] // SUBLANES, SUBLANES, n_q), axis=0)
                        for k, q in pairs[h][c]]
            tops.append(jnp.max(functools.reduce(jnp.maximum, seg_tops), axis=0, keepdims=True))
    for h in range(n_heads):
        for c in range(2):
            n = 2 * h + c
            row = 0
            for k, q in pairs[h][c]:
                e_scr[n, row:row + k.shape[0], :] = jnp.exp2(scores(k, q) - tops[n]).astype(BF16)
                row += k.shape[0]
    outs = []
    for h in range(n_heads):
        v_aug = jnp.concatenate([values[h], jnp.ones_like(values[h])], axis=-1)
        normed = []
        for c in range(2):
            acc = lax.dot_general(v_aug, e_scr[2 * h + c], _TN, preferred_element_type=F32)
            normed.append(acc[:HEAD_DIM] * (1.0 / acc[HEAD_DIM:HEAD_DIM + 1]))
        o_t = normed[0] - lam * normed[1]
        ms = jnp.mean(o_t * o_t, axis=0, keepdims=True)
        outs.append(o_t * lax.rsqrt(ms + EPS) * g_col * (1.0 - lam_init))
    return outs


def _component_query(q_window, c, lo):
    lane = lax.broadcasted_iota(I32, q_window.shape, 1)
    first = lo + c * ATT_HALF
    inside = jnp.logical_and(lane >= first, lane < first + ATT_HALF)
    return jnp.where(inside, q_window, jnp.zeros_like(q_window))


def _lane_tile(x, h):
    heads_per_tile = LANES // HEAD_DIM
    t = h // heads_per_tile
    return x[:, t * LANES:(t + 1) * LANES], (h % heads_per_tile) * HEAD_DIM


def _ctx_attn_kernel(lam_ref, g_ref, q_ref, k_ref, v_ref, *rest, lam_init, stacked):
    *rest, e_scr = rest
    if stacked:
        pk_ref, pv_ref, att_ref, ck_ref, cv_ref = rest
        ck_ref[0, 0] = pk_ref[0]
        cv_ref[0, 0] = pv_ref[0]
        put_k = lambda h, val: ck_ref.__setitem__((0, 1, h), val)
        put_v = lambda h, val: cv_ref.__setitem__((0, 1, h), val)
    else:
        att_ref, ck_ref, cv_ref = rest
        put_k = lambda h, val: ck_ref.__setitem__((0, h), val)
        put_v = lambda h, val: cv_ref.__setitem__((0, h), val)
    lam = _lambda(lam_ref, lam_init)
    k = k_ref[...]
    v = v_ref[...]
    qb = (q_ref[...] * _Q_SCALE).astype(BF16)
    kb = k.astype(BF16)
    vb = v.astype(BF16)
    heads = range(N_ATT_HEADS)
    pairs = []
    for h in heads:
        (k_tile, lo), (q_tile, _) = _lane_tile(kb, h), _lane_tile(qb, h)
        pairs.append([[(k_tile, _component_query(q_tile, c, lo))] for c in range(2)])
    outs = _attention_heads(pairs, [vb[:, h * HEAD_DIM:(h + 1) * HEAD_DIM] for h in heads],
                            e_scr, lam, g_ref[...], lam_init)
    for h in heads:
        put_k(h, k[:, h * HEAD_DIM:(h + 1) * HEAD_DIM])
        put_v(h, v[:, h * HEAD_DIM:(h + 1) * HEAD_DIM])
    att_ref[...] = jnp.concatenate(outs, axis=0).T


def _context_attention(proj, lam, subln_col, li, lam_init, prev_kv=None):
    stacked = prev_kv is not None
    per_layer = pl.BlockSpec((1, N_ATT_HEADS, SEQ, HEAD_DIM), lambda b: (b, 0, 0, 0))
    if stacked:
        kv_shape = jax.ShapeDtypeStruct((BATCH, DEPTH, N_ATT_HEADS, SEQ, HEAD_DIM), F32)
        kv_spec = pl.BlockSpec((1, DEPTH, N_ATT_HEADS, SEQ, HEAD_DIM), lambda b: (b, 0, 0, 0, 0))
    else:
        kv_shape = jax.ShapeDtypeStruct((BATCH, N_ATT_HEADS, SEQ, HEAD_DIM), F32)
        kv_spec = per_layer
    return pl.pallas_call(
        functools.partial(_ctx_attn_kernel, lam_init=lam_init, stacked=stacked),
        grid=(BATCH,),
        in_specs=[
            _layer_block(li, (4, ATT_HALF)),
            _layer_block(li, (HEAD_DIM, 1)),
            pl.BlockSpec((SEQ, ATT_WIDTH), lambda b: (b, 0)),
            pl.BlockSpec((SEQ, ATT_WIDTH), lambda b: (b, 1)),
            pl.BlockSpec((SEQ, ATT_WIDTH), lambda b: (b, 2)),
        ] + ([per_layer, per_layer] if stacked else []),
        out_specs=[pl.BlockSpec((SEQ, ATT_WIDTH), lambda b: (b, 0)), kv_spec, kv_spec],
        out_shape=[jax.ShapeDtypeStruct((N_CTX, ATT_WIDTH), F32), kv_shape, kv_shape],
        scratch_shapes=[pltpu.VMEM((2 * N_ATT_HEADS, SEQ, SEQ), BF16)],
        compiler_params=_cparams(1),
        name="context_attention",
    )(lam, subln_col, proj, proj, proj, *(prev_kv if stacked else ()))


def _rope_tables():
    t = np.arange(DEC_SEQ)
    pos = np.stack([t // GRID_W, t % GRID_W], axis=1).astype(np.float32)
    inv = (np.float32(ROPE_THETA) ** (-np.arange(ROPE_FREQS, dtype=np.float32) / np.float32(ROPE_FREQS)))
    j = np.arange(HEAD_DIM)
    d = j % ATT_HALF
    axis = d // (2 * ROPE_FREQS)
    u = d % (2 * ROPE_FREQS)
    ang = pos[:, axis] * inv[u % ROPE_FREQS][None, :].astype(np.float32)
    cos = np.cos(ang).astype(np.float32)
    sin = np.sin(ang).astype(np.float32)
    first = (u < ROPE_FREQS)[None, :]
    s_next = np.where(first, -sin, 0.0).astype(np.float32)
    s_prev = np.where(first, 0.0, sin).astype(np.float32)
    tile = lambda a: jnp.asarray(np.tile(a, (1, N_ATT_HEADS)))
    return tile(cos), tile(s_next), tile(s_prev)


def _rotate(x, cos, s_next, s_prev):
    width = x.shape[-1]
    return (x * cos + pltpu.roll(x, width - ROPE_FREQS, axis=1) * s_next
            + pltpu.roll(x, ROPE_FREQS, axis=1) * s_prev)


QB = 512
Q_STEPS = DEC_SEQ // QB
DEC_HEAD_GROUP = 4


def _dec_attn_kernel(lam_ref, g_ref, q_ref, k_ref, v_ref, ck_ref, cv_ref,
                     cq_ref, snq_ref, spq_ref, ck_tab, snk_tab, spk_tab,
                     att_ref, krot_ref, e_scr, *, lam_init):
    j = pl.program_id(1)

    @pl.when(j == 0)
    def _():
        krot_ref[...] = _rotate(k_ref[...], ck_tab[...], snk_tab[...], spk_tab[...]).astype(BF16)

    lam = _lambda(lam_ref, lam_init)
    qb = (_rotate(q_ref[...], cq_ref[...], snq_ref[...], spq_ref[...]) * _Q_SCALE).astype(BF16)
    kb = krot_ref[...]
    vb = v_ref[...].astype(BF16)
    outs = []
    for h0 in range(0, N_ATT_HEADS, DEC_HEAD_GROUP):
        heads = range(h0, h0 + DEC_HEAD_GROUP)
        pairs = []
        for h in heads:
            past_k = ck_ref[0, 0, h].astype(BF16)
            q_head = qb[:, h * HEAD_DIM:(h + 1) * HEAD_DIM]
            (k_tile, lo), (q_tile, _) = _lane_tile(kb, h), _lane_tile(qb, h)
            pairs.append([[(past_k, _component_query(q_head, c, 0)), (k_tile, _component_query(q_tile, c, lo))]
                          for c in range(2)])
        values = [jnp.concatenate([cv_ref[0, 0, h].astype(BF16), vb[:, h * HEAD_DIM:(h + 1) * HEAD_DIM]], axis=0)
                  for h in heads]
        outs += _attention_heads(pairs, values, e_scr, lam, g_ref[...], lam_init)
    att_ref[...] = jnp.concatenate(outs, axis=0).T


def _denoise_attention(proj, cache_k, cache_v, lam, subln_col, li, lam_init, tables):
    cos, s_next, s_prev = tables
    row0 = N_CTX // QB
    seq0 = N_CTX // DEC_SEQ
    q_tab = pl.BlockSpec((QB, ATT_WIDTH), lambda b, j: (j, 0))
    k_tab = pl.BlockSpec((DEC_SEQ, ATT_WIDTH), lambda b, j: (0, 0))
    cache_spec = pl.BlockSpec((1, 1, N_ATT_HEADS, PAST_LEN, HEAD_DIM), lambda b, j: (b, li, 0, 0, 0))
    return pl.pallas_call(
        functools.partial(_dec_attn_kernel, lam_init=lam_init),
        grid=(DEC_BATCH, Q_STEPS),
        in_specs=[
            _layer_block(li, (4, ATT_HALF)),
            _layer_block(li, (HEAD_DIM, 1)),
            pl.BlockSpec((QB, ATT_WIDTH), lambda b, j: (row0 + b * Q_STEPS + j, 0)),
            pl.BlockSpec((DEC_SEQ, ATT_WIDTH), lambda b, j: (seq0 + b, 1)),
            pl.BlockSpec((DEC_SEQ, ATT_WIDTH), lambda b, j: (seq0 + b, 2)),
            cache_spec, cache_spec,
            q_tab, q_tab, q_tab, k_tab, k_tab, k_tab,
        ],
        out_specs=pl.BlockSpec((QB, ATT_WIDTH), lambda b, j: (b * Q_STEPS + j, 0)),
        out_shape=jax.ShapeDtypeStruct((N_DEC, ATT_WIDTH), F32),
        scratch_shapes=[pltpu.VMEM((DEC_SEQ, ATT_WIDTH), BF16),
                        pltpu.VMEM((2 * DEC_HEAD_GROUP, PAST_LEN + DEC_SEQ, QB), BF16)],
        compiler_params=_cparams(2),
        name="denoise_attention",
    )(lam, subln_col, proj, proj, proj, cache_k, cache_v,
      cos, s_next, s_prev, cos, s_next, s_prev)


def _softplus(z):
    return jnp.maximum(z, 0.0) + jnp.log1p(jnp.exp(-jnp.abs(z)))


def _mixer_kernel(su_ref, sv_ref, rx_ref, rg_ref, h0_ref, lng_ref, lnb_ref, ws_ref, bs_ref,
                  cw_ref, cb_ref, wrg_ref, brg_ref, lrulog_ref,
                  sgu_ref, lru_ref, hlast_ref, a_scr, b_scr, h_scr, *, seq_len):
    vn = _layer_norm_rows(sv_ref[...], lng_ref[...], lnb_ref[...])
    lane_group = lax.broadcasted_iota(I32, (CHUNK, SGU_WIDTH), 1) // SGU_GROUP
    for n in range(seq_len // CHUNK):
        rows = slice(n * CHUNK, (n + 1) * CHUNK)
        vc = vn[rows].astype(BF16)
        s = jnp.zeros((CHUNK, SGU_WIDTH), F32)
        for g in range(N_SGU_GROUPS):
            sg = jnp.dot(ws_ref[g].astype(BF16), vc, preferred_element_type=F32)
            s = jnp.where(lane_group == g, sg, s)
        sgu_ref[rows, :] = su_ref[rows, :] * (s + bs_ref[...])

    x = rx_ref[...]
    row = lax.broadcasted_iota(I32, (seq_len, LRU_WIDTH), 0)

    def shifted(val, d, fill):
        rolled = pltpu.roll(val, d % seq_len, axis=0)
        inside = (row >= d) if d > 0 else (row < seq_len + d)
        return jnp.where(inside, rolled, fill)

    left = CONV_W // 2
    xc = cb_ref[...] + x * cw_ref[left:left + 1, :]
    for tap in range(CONV_W):
        if tap != left:
            xc = xc + shifted(x, left - tap, 0.0) * cw_ref[tap:tap + 1, :]
    pre = jnp.dot(xc.astype(BF16), wrg_ref[...].astype(BF16), preferred_element_type=F32) + brg_ref[...]
    gates = 0.5 + 0.5 * jnp.tanh(0.5 * pre)
    in_chunk = row % SUBLANES
    n_chunks = seq_len // SUBLANES
    h0 = h0_ref[0, 0]
    lasts = []
    for direction in range(2):
        reverse = direction == 1
        base = direction * 2 * LRU_WIDTH
        r = gates[:, base:base + LRU_WIDTH]
        gi = gates[:, base + LRU_WIDTH:base + 2 * LRU_WIDTH]
        log_a = -LRU_C * r * _softplus(-lrulog_ref[direction:direction + 1, :])
        a = jnp.exp(log_a)
        b = jnp.sqrt(-jnp.tanh(log_a) * (a * a + 1.0)) * gi * xc
        def chunk_roll(val, shift):
            chunks = val.reshape(n_chunks, SUBLANES, LRU_WIDTH)
            return pltpu.roll(chunks, shift, axis=1).reshape(seq_len, LRU_WIDTH)

        for d in (1, 2, 4):
            if reverse:
                inside = in_chunk < SUBLANES - d
                a_n = jnp.where(inside, chunk_roll(a, SUBLANES - d), 1.0)
                b_n = jnp.where(inside, chunk_roll(b, SUBLANES - d), 0.0)
            else:
                inside = in_chunk >= d
                a_n = jnp.where(inside, chunk_roll(a, d), 1.0)
                b_n = jnp.where(inside, chunk_roll(b, d), 0.0)
            b = a * b_n + b
            a = a * a_n
        a_scr[...] = a
        b_scr[...] = b

        def chunk_step(c, carry, reverse=reverse):
            cc = n_chunks - 1 - c if reverse else c
            off = pl.multiple_of(cc * SUBLANES, SUBLANES)
            hc = a_scr[pl.ds(off, SUBLANES), :] * carry + b_scr[pl.ds(off, SUBLANES), :]
            if reverse:
                h_scr[pl.ds(off, SUBLANES), :] = h_scr[pl.ds(off, SUBLANES), :] + hc
                return hc[0:1, :]
            h_scr[pl.ds(off, SUBLANES), :] = hc
            return hc[SUBLANES - 1:SUBLANES, :]

        lasts.append(lax.fori_loop(0, n_chunks, chunk_step, h0[direction:direction + 1, :]))
    lru_ref[...] = h_scr[...] * jax.nn.gelu(rg_ref[...])
    hlast_ref[0] = jnp.concatenate(lasts, axis=0)


def _mixers(proj, h0, h0_layer, lp, li, seq_len, n_seq, row_block0):
    col0 = 3 * ATT_WIDTH // SGU_WIDTH
    col = lambda c: pl.BlockSpec((seq_len, SGU_WIDTH), lambda b: (row_block0 + b, col0 + c))
    full = lambda shape: _layer_block(li, shape)
    out_rows = pl.BlockSpec((seq_len, SGU_WIDTH), lambda b: (b, 0))
    return pl.pallas_call(
        functools.partial(_mixer_kernel, seq_len=seq_len),
        grid=(n_seq,),
        in_specs=[
            col(0), col(1), col(2), col(3),
            pl.BlockSpec((1, 1, 2, LRU_WIDTH), lambda b: (b, h0_layer, 0, 0)),
            full((1, SGU_WIDTH)), full((1, SGU_WIDTH)),
            full((N_SGU_GROUPS, CHUNK, CHUNK)), full((CHUNK, SGU_WIDTH)),
            full((CONV_W, LRU_WIDTH)), full((1, LRU_WIDTH)),
            full((LRU_WIDTH, 4 * LRU_WIDTH)), full((1, 4 * LRU_WIDTH)),
            full((2, LRU_WIDTH)),
        ],
        out_specs=[out_rows, out_rows, pl.BlockSpec((1, 2, LRU_WIDTH), lambda b: (b, 0, 0))],
        out_shape=[
            jax.ShapeDtypeStruct((n_seq * seq_len, SGU_WIDTH), F32),
            jax.ShapeDtypeStruct((n_seq * seq_len, LRU_WIDTH), F32),
            jax.ShapeDtypeStruct((n_seq, 2, LRU_WIDTH), F32),
        ],
        scratch_shapes=[pltpu.VMEM((seq_len, LRU_WIDTH), F32)] * 3,
        compiler_params=_cparams(1),
        name="mixers_%d" % seq_len,
    )(proj, proj, proj, proj, h0, lp["sgu_ln_g"], lp["sgu_ln_b"], lp["w_spatial"], lp["b_spatial_full"],
      lp["conv_w"], lp["conv_b"], lp["w_rg_full"], lp["b_rg_full"], lp["lru_log"])


def _outproj_kernel(xc_ref, xd_ref, ac_ref, ad_ref, sc_ref, sd_ref, lc_ref, ld_ref,
                    mod_ref, wout_ref, lng_ref, lnb_ref, wr_ref, br_ref,
                    x1_ref, h2_ref, ri_ref, rf_ref, cnt_ref, wbf_ref):
    i = pl.program_id(0)

    @pl.when(i == 0)
    def _():
        wbf_ref[...] = wout_ref[...].astype(BF16)

    m = mod_ref[pl.ds(_mod_row(i), 1), :]
    g1 = m[:, 2 * D_MODEL:3 * D_MODEL]
    sh2 = m[:, 3 * D_MODEL:4 * D_MODEL]
    sc2 = m[:, 4 * D_MODEL:5 * D_MODEL]
    a0, a1 = ATT_WIDTH, ATT_WIDTH + SGU_WIDTH
    x = _pair_read(i, CTX_TILES, xc_ref, xd_ref)
    att = _pair_read(i, CTX_TILES, ac_ref, ad_ref).astype(BF16)
    sgu = _pair_read(i, CTX_TILES, sc_ref, sd_ref).astype(BF16)
    lru = _pair_read(i, CTX_TILES, lc_ref, ld_ref).astype(BF16)
    mix = (jnp.dot(att, wbf_ref[0:a0, :], preferred_element_type=F32)
           + jnp.dot(sgu, wbf_ref[a0:a1, :], preferred_element_type=F32)
           + jnp.dot(lru, wbf_ref[a1:, :], preferred_element_type=F32))
    x1 = _layer_norm_rows(DN_ALPHA * x + g1 * mix, lng_ref[...], lnb_ref[...])
    x1_ref[...] = x1
    h2 = x1 * (1.0 + sc2) + sh2
    h2_ref[...] = h2.astype(BF16)

    logits = jnp.dot(h2.astype(BF16), wr_ref[...].astype(BF16), preferred_element_type=F32) + br_ref[...]
    lane = lax.broadcasted_iota(I32, (TM, LANES), 1)
    lane_f = lane.astype(F32)
    neg_inf = jnp.float32(-jnp.inf)
    work = jnp.where(lane < N_EXPERTS, logits, neg_inf)
    vals, idxs = [], []
    for _ in range(TOP_K):
        top = jnp.max(work, axis=-1, keepdims=True)
        idx = jnp.min(jnp.where(work == top, lane_f, float(LANES)), axis=-1, keepdims=True)
        vals.append(top)
        idxs.append(idx)
        work = jnp.where(lane_f == idx, neg_inf, work)
    exps = [jnp.exp(v - vals[0]) for v in vals]
    denom = exps[0] + exps[1] + exps[2] + exps[3]
    onehot = jnp.zeros((TM, LANES), F32)
    for idx in idxs:
        onehot = onehot + (lane_f == idx).astype(F32)
    r_i = lax.broadcasted_iota(I32, (RT, RT), 0)
    c_i = lax.broadcasted_iota(I32, (RT, RT), 1)
    tri = (r_i > c_i).astype(F32).astype(BF16)
    upper = (lax.broadcasted_iota(I32, (LANES, LANES), 0)
             < lax.broadcasted_iota(I32, (LANES, LANES), 1)).astype(F32).astype(BF16)
    packed_pos = []
    for t in range(RT_PER_TM):
        hot = onehot[t * RT:(t + 1) * RT]
        total = jnp.broadcast_to(jnp.sum(hot, axis=0, keepdims=True), (SUBLANES, LANES))
        cnt_ref[t] = total.astype(I32)
        run_start = jnp.dot(total.astype(BF16), upper, preferred_element_type=F32)[0:1, :]
        packed_pos.append(jnp.dot(tri, hot.astype(BF16), preferred_element_type=F32) + run_start)
    packed_pos = jnp.concatenate(packed_pos, axis=0)
    ri = jnp.zeros((TM, LANES), F32)
    rf = jnp.zeros((TM, LANES), F32)
    for k in range(TOP_K):
        pos = jnp.sum(jnp.where(lane_f == idxs[k], packed_pos, 0.0), axis=-1, keepdims=True)
        ri = jnp.where(lane == k, idxs[k], ri)
        ri = jnp.where(lane == TOP_K + k, pos, ri)
        rf = jnp.where(lane == k, exps[k] / denom, rf)
    ri_ref[...] = ri.astype(I32)
    rf_ref[...] = rf


def _output_projection(x_pair, att_pair, sgu_pair, lru_pair, mod, lp, li):
    rows = lambda w: pl.BlockSpec((TM, w), lambda i: (i, 0))
    full = lambda shape: _layer_block(li, shape)
    return pl.pallas_call(
        _outproj_kernel,
        grid=(N_TILES,),
        in_specs=(_pair_specs(TM, D_MODEL, CTX_TILES) + _pair_specs(TM, ATT_WIDTH, CTX_TILES)
                  + _pair_specs(TM, SGU_WIDTH, CTX_TILES) + _pair_specs(TM, LRU_WIDTH, CTX_TILES) + [
            pl.BlockSpec((None, SUBLANES, 6 * D_MODEL), lambda i: (li, 0, 0)),
            pl.BlockSpec((None, D_MODEL, D_MODEL), lambda i: (li, 0, 0), pipeline_mode=pl.Buffered(1)),
            _layer_block(li, (1, D_MODEL), 0), _layer_block(li, (1, D_MODEL), 0),
            full((D_MODEL, LANES)), full((1, LANES)),
        ]),
        out_specs=[
            rows(D_MODEL), rows(D_MODEL), rows(LANES), rows(LANES),
            pl.BlockSpec((RT_PER_TM, SUBLANES, LANES), lambda i: (i, 0, 0)),
        ],
        out_shape=[
            jax.ShapeDtypeStruct((N_TOK, D_MODEL), F32),
            jax.ShapeDtypeStruct((N_TOK, D_MODEL), BF16),
            jax.ShapeDtypeStruct((N_TOK, LANES), I32),
            jax.ShapeDtypeStruct((N_TOK, LANES), F32),
            jax.ShapeDtypeStruct((N_RT, SUBLANES, LANES), I32),
        ],
        scratch_shapes=[pltpu.VMEM((D_MODEL, D_MODEL), BF16)],
        compiler_params=_cparams(1),
        name="output_projection",
    )(*x_pair, *att_pair, *sgu_pair, *lru_pair, mod, lp["w_out_all"], lp["ln_g"], lp["ln_b"],
      lp["w_router_pad"], lp["b_router_pad"])


def _row_ds(row, n_rows):
    return pl.ds(pl.multiple_of(row * ROW_SLABS, ROW_SLABS), n_rows * ROW_SLABS)


def _for_each_run_piece(length_s, bits, fn):
    for bit in bits:
        done_s = length_s & (-2 * bit * ROW_SLABS)
        @pl.when((length_s & (bit * ROW_SLABS)) != 0)
        def _(done_s=done_s, bit=bit):
            fn(done_s, bit)


def _stored_ds(row_s, n_rows):
    return pl.ds(pl.multiple_of(row_s, ROW_SLABS), n_rows * ROW_SLABS)


def _tile_run_copies(tile, start_ref, len_ref, off_ref, sorted_hbm, buf, sem, to_sorted, live=None):
    for e in range(N_EXPERTS):
        t = tile * N_EXPERTS + e
        start_s, length_s, off_s = start_ref[t], len_ref[t], off_ref[t]
        if live is not None:
            length_s = jnp.where(live, length_s, 0)

        def piece(done_s, bit, start_s=start_s, off_s=off_s):
            packed = buf.at[_stored_ds(off_s + done_s, bit), :]
            srt = sorted_hbm.at[_stored_ds(start_s + done_s, bit), :]
            if to_sorted:
                pltpu.make_async_copy(packed, srt, sem).start()
            else:
                pltpu.make_async_copy(srt, packed, sem).start()

        _for_each_run_piece(length_s, RUN_BITS, piece)


def _tile_runs_wait(buf, sem):
    pltpu.make_async_copy(buf, buf, sem).wait()


def _packed_positions(ri):
    return [ri[:, TOP_K + k:TOP_K + k + 1].astype(F32) for k in range(TOP_K)]


def _slab_columns(buf, row0, n_rows):
    return jnp.concatenate(
        [buf[pl.ds(row0 * ROW_SLABS + s, n_rows, stride=ROW_SLABS), :] for s in range(ROW_SLABS)], axis=-1)


def _dispatch_kernel(start_ref, len_ref, off_ref, pstart_ref, plen_ref, nv_ref,
                     h2_ref, ri_ref, xs_hbm, *scratch):
    bufs, (zbuf, sem, zsem) = scratch[:RT_PER_STEP], scratch[RT_PER_STEP:]
    i = pl.program_id(0)
    n_steps = pl.num_programs(0)

    @pl.when(i == 0)
    def _():
        zbuf[...] = jnp.zeros_like(zbuf)

        def zero_fill(wait):
            def go(cp):
                cp.wait() if wait else cp.start()

            def per_expert(e, carry):
                def piece(done_s, bit):
                    go(pltpu.make_async_copy(zbuf.at[pl.ds(0, bit * ROW_SLABS), :],
                                             xs_hbm.at[_stored_ds(pstart_ref[e] + done_s, bit), :], zsem))
                _for_each_run_piece(plen_ref[e], PAD_BITS, piece)
                return carry
            lax.fori_loop(0, N_EXPERTS, per_expert, 0)

            def per_block(b, carry):
                go(pltpu.make_async_copy(zbuf, xs_hbm.at[_row_ds(b * BM, BM), :], zsem))
                return carry
            lax.fori_loop(nv_ref[0], NB, per_block, 0)

        zero_fill(False)
        zero_fill(True)

    def send(tile, t, live):
        _tile_run_copies(tile, start_ref, len_ref, off_ref, xs_hbm, bufs[t], sem.at[t], to_sorted=True, live=live)

    col = lax.broadcasted_iota(I32, (RT, RT_ROWS), 1).astype(F32)
    for t in range(RT_PER_STEP):
        buf = bufs[t]
        rows = slice(t * RT, (t + 1) * RT)
        tile = i * RT_PER_STEP + t

        @pl.when(i >= 1)
        def _(buf=buf, t=t):
            _tile_runs_wait(buf, sem.at[t])

        send(jnp.maximum(tile - 1, 0), (t - 1) % RT_PER_STEP, tile >= 1)
        pos = _packed_positions(ri_ref[rows, :])
        sel = jnp.zeros((RT, RT_ROWS), F32)
        for p in pos:
            sel = sel + (col == p).astype(F32)
        packed = lax.dot_general(sel.astype(BF16), h2_ref[rows, :], _TN, preferred_element_type=F32)
        for s in range(ROW_SLABS):
            buf[pl.ds(s, RT_ROWS, stride=ROW_SLABS), :] = packed[:, s * LANES:(s + 1) * LANES]

    @pl.when(i == n_steps - 1)
    def _():
        send(N_RT - 1, RT_PER_STEP - 1, True)
        for t in range(RT_PER_STEP):
            _tile_runs_wait(bufs[t], sem.at[t])


def _dispatch(tables, h2, route_i):
    grid_spec = pltpu.PrefetchScalarGridSpec(
        num_scalar_prefetch=6,
        grid=(N_RT // RT_PER_STEP,),
        in_specs=[
            pl.BlockSpec((RT_PER_STEP * RT, D_MODEL), lambda i, *_: (i, 0)),
            pl.BlockSpec((RT_PER_STEP * RT, LANES), lambda i, *_: (i, 0)),
        ],
        out_specs=pl.BlockSpec(memory_space=pl.ANY),
        scratch_shapes=[pltpu.VMEM((RT_ROWS * ROW_SLABS, LANES), F32)] * RT_PER_STEP + [
            pltpu.VMEM((BM * ROW_SLABS, LANES), F32),
            pltpu.SemaphoreType.DMA((RT_PER_STEP,)),
            pltpu.SemaphoreType.DMA(()),
        ],
    )
    return pl.pallas_call(
        _dispatch_kernel,
        grid_spec=grid_spec,
        out_shape=jax.ShapeDtypeStruct((N_SLOTS * ROW_SLABS, LANES), F32),
        compiler_params=_cparams(1),
        name="moe_dispatch",
    )(tables["start"], tables["len"], tables["off"], tables["pad_start"], tables["pad_len"], tables["n_valid"],
      h2, route_i)


def _moe_kernel(be_ref, eo_ref, nv_ref, xs_hbm, wgu_hbm, bgu_ref, wdn_hbm, bdn_ref, y_hbm,
                xbuf, ybuf, gu_stage, dn_stage, wgu_bf, wdn_bf, wsem, xsem, ysem, *, li):
    n_valid = nv_ref[0]
    rows_per_piece = D_MODEL // WEIGHT_PIECES

    def weight_copies(e, stage):
        cps = []
        for p in range(WEIGHT_PIECES):
            band = pl.ds(p * rows_per_piece, rows_per_piece)
            cps.append(pltpu.make_async_copy(wgu_hbm.at[li, e, band, :], gu_stage.at[stage, band, :], wsem.at[stage]))
            cps.append(pltpu.make_async_copy(wdn_hbm.at[li, e, band, :], dn_stage.at[stage, band, :], wsem.at[stage]))
        return cps

    def fetch(k):
        @pl.when(eo_ref[k] >= 0)
        def _():
            for cp in weight_copies(eo_ref[k], k % 2):
                cp.start(priority=1)

    def x_copy(b, slot):
        return pltpu.make_async_copy(xs_hbm.at[_row_ds(b * BM, BM), :], xbuf.at[slot], xsem.at[slot])

    def y_copy(b, slot):
        return pltpu.make_async_copy(ybuf.at[slot], y_hbm.at[_row_ds(b * BM, BM), :], ysem.at[slot])

    fetch(0)
    fetch(1)
    for ahead in range(X_BUFFERS - 1):
        @pl.when(ahead < n_valid)
        def _(ahead=ahead):
            x_copy(ahead, ahead).start()

    def next_slot(s):
        return jnp.where(s == X_BUFFERS - 1, 0, s + 1)

    def block(b, carry):
        k, xslot = carry
        slot = b % 2
        e = be_ref[b]
        new_expert = jnp.logical_or(b == 0, e != be_ref[jnp.maximum(b - 1, 0)])

        @pl.when(b + X_BUFFERS - 1 < n_valid)
        def _():
            ahead_slot = xslot
            for _ in range(X_BUFFERS - 1):
                ahead_slot = next_slot(ahead_slot)
            x_copy(b + X_BUFFERS - 1, ahead_slot).start()

        @pl.when(new_expert)
        def _():
            stage = k % 2
            for cp in weight_copies(e, stage):
                cp.wait()
            for st in range(2):
                @pl.when(stage == st)
                def _(st=st):
                    wgu_bf[...] = gu_stage[st].astype(BF16)
                    wdn_bf[...] = dn_stage[st].astype(BF16)
            fetch(k + 2)

        x_copy(b, xslot).wait()

        @pl.when(b >= 2)
        def _():
            y_copy(b - 2, slot).wait()

        x = _slab_columns(xbuf.at[xslot], 0, BM).astype(BF16)
        bgu = bgu_ref[e]
        g = jnp.dot(x, wgu_bf[:, :D_EXPERT], preferred_element_type=F32) + bgu[:, :D_EXPERT]
        u = jnp.dot(x, wgu_bf[:, D_EXPERT:], preferred_element_type=F32) + bgu[:, D_EXPERT:]
        g = jnp.minimum(g, SWIGLU_LIMIT)
        u = jnp.clip(u, -SWIGLU_LIMIT, SWIGLU_LIMIT)
        act = ((u + 1.0) * (0.5 * g * (1.0 + jnp.tanh((0.5 * SWIGLU_ALPHA) * g)))).astype(BF16)
        y = jnp.dot(act, wdn_bf[...], preferred_element_type=F32) + bdn_ref[e]
        out = ybuf.at[slot]
        for s in range(ROW_SLABS):
            out[pl.ds(s, BM, stride=ROW_SLABS), :] = y[:, s * LANES:(s + 1) * LANES]
        y_copy(b, slot).start()
        return k + new_expert.astype(I32), next_slot(xslot)

    lax.fori_loop(0, n_valid, block, (jnp.int32(0), jnp.int32(0)))

    @pl.when(n_valid >= 2)
    def _():
        y_copy(n_valid - 2, n_valid % 2).wait()
    y_copy(n_valid - 1, (n_valid - 1) % 2).wait()

    ybuf[0] = jnp.zeros((BM * ROW_SLABS, LANES), F32)

    def zero_blocks(wait):
        def one(b, carry):
            cp = y_copy(b, 0)
            cp.wait() if wait else cp.start()
            return carry
        lax.fori_loop(n_valid, NB, one, 0)

    zero_blocks(False)
    zero_blocks(True)


def _moe_blocks(tables, xs, w_gu, b_gu, w_down, b_down, li):
    grid_spec = pltpu.PrefetchScalarGridSpec(
        num_scalar_prefetch=3,
        grid=(1,),
        in_specs=[
            pl.BlockSpec(memory_space=pl.ANY),
            pl.BlockSpec(memory_space=pl.ANY),
            pl.BlockSpec((None, N_EXPERTS, 1, 2 * D_EXPERT), lambda i, *_: (li, 0, 0, 0)),
            pl.BlockSpec(memory_space=pl.ANY),
            pl.BlockSpec((None, N_EXPERTS, 1, D_MODEL), lambda i, *_: (li, 0, 0, 0)),
        ],
        out_specs=pl.BlockSpec(memory_space=pl.ANY),
        scratch_shapes=[
            pltpu.VMEM((X_BUFFERS, BM * ROW_SLABS, LANES), F32),
            pltpu.VMEM((2, BM * ROW_SLABS, LANES), F32),
            pltpu.VMEM((2, D_MODEL, 2 * D_EXPERT), F32),
            pltpu.VMEM((2, D_EXPERT, D_MODEL), F32),
            pltpu.VMEM((D_MODEL, 2 * D_EXPERT), BF16),
            pltpu.VMEM((D_EXPERT, D_MODEL), BF16),
            pltpu.SemaphoreType.DMA((2,)),
            pltpu.SemaphoreType.DMA((X_BUFFERS,)),
            pltpu.SemaphoreType.DMA((2,)),
        ],
    )
    return pl.pallas_call(
        functools.partial(_moe_kernel, li=li),
        grid_spec=grid_spec,
        out_shape=jax.ShapeDtypeStruct((N_SLOTS * ROW_SLABS, LANES), F32),
        compiler_params=_cparams(1),
        name="moe_experts",
    )(tables["block_expert"], tables["expert_order"], tables["n_valid"], xs, w_gu,
      b_gu.reshape(DEPTH, N_EXPERTS, 1, 2 * D_EXPERT), w_down, b_down.reshape(DEPTH, N_EXPERTS, 1, D_MODEL))


def _combine_kernel(start_ref, len_ref, off_ref, y_hbm, x1_ref, ri_ref, rf_ref, mod_ref,
                    lng_ref, lnb_ref, oc_ref, od_ref, *scratch):
    bufs, sem = scratch[:RT_PER_STEP], scratch[RT_PER_STEP]
    i = pl.program_id(0)
    n_steps = pl.num_programs(0)

    def fetch(tile, t):
        _tile_run_copies(tile, start_ref, len_ref, off_ref, y_hbm, bufs[t], sem.at[t], to_sorted=False)

    @pl.when(i == 0)
    def _():
        for t in range(FETCH_AHEAD):
            fetch(t, t)

    m = mod_ref[pl.ds(_mod_row(i * RT_PER_STEP // RT_PER_TM), 1), :]
    g2 = m[:, 5 * D_MODEL:6 * D_MODEL]
    col = lax.broadcasted_iota(I32, (RT, RT_ROWS), 1).astype(F32)
    outs = []
    for t in range(RT_PER_STEP):
        rows = slice(t * RT, (t + 1) * RT)
        fetch(jnp.minimum(i * RT_PER_STEP + t + FETCH_AHEAD, N_RT - 1), (t + FETCH_AHEAD) % RT_PER_STEP)
        _tile_runs_wait(bufs[t], sem.at[t])
        pos = _packed_positions(ri_ref[rows, :])
        gates = rf_ref[rows, :]
        mix = jnp.zeros((RT, RT_ROWS), F32)
        for k in range(TOP_K):
            mix = mix + jnp.where(col == pos[k], gates[:, k:k + 1], 0.0)
        ffn = jnp.dot(mix.astype(BF16), _slab_columns(bufs[t], 0, RT_ROWS).astype(BF16),
                      preferred_element_type=F32)
        outs.append(_layer_norm_rows(DN_ALPHA * x1_ref[rows, :] + g2 * ffn, lng_ref[...], lnb_ref[...]))
    out = jnp.concatenate(outs, axis=0)
    n_ctx_steps = N_CTX // (RT * RT_PER_STEP)

    @pl.when(i < n_ctx_steps)
    def _():
        oc_ref[...] = out

    @pl.when(i >= n_ctx_steps)
    def _():
        od_ref[...] = out

    @pl.when(i == n_steps - 1)
    def _():
        for t in range(FETCH_AHEAD):
            _tile_runs_wait(bufs[t], sem.at[t])


def _combine(tables, y, x1, route_i, route_f, mod, ln_g, ln_b, li):
    rows = RT * RT_PER_STEP
    grid_spec = pltpu.PrefetchScalarGridSpec(
        num_scalar_prefetch=3,
        grid=(N_RT // RT_PER_STEP,),
        in_specs=[
            pl.BlockSpec(memory_space=pl.ANY),
            pl.BlockSpec((rows, D_MODEL), lambda i, *_: (i, 0)),
            pl.BlockSpec((rows, LANES), lambda i, *_: (i, 0)),
            pl.BlockSpec((rows, LANES), lambda i, *_: (i, 0)),
            pl.BlockSpec((None, SUBLANES, 6 * D_MODEL), lambda i, *_: (li, 0, 0)),
            _layer_block(li, (1, D_MODEL), 1),
            _layer_block(li, (1, D_MODEL), 1),
        ],
        out_specs=_pair_specs(rows, D_MODEL, N_CTX // rows),
        scratch_shapes=[pltpu.VMEM((RT_ROWS * ROW_SLABS, LANES), F32)] * RT_PER_STEP + [
            pltpu.SemaphoreType.DMA((RT_PER_STEP,)),
        ],
    )
    return pl.pallas_call(
        _combine_kernel,
        grid_spec=grid_spec,
        out_shape=[jax.ShapeDtypeStruct((N_CTX, D_MODEL), F32), jax.ShapeDtypeStruct((N_DEC, D_MODEL), F32)],
        compiler_params=_cparams(1),
        name="moe_combine",
    )(tables["start"], tables["len"], tables["off"], y, x1, route_i, route_f, mod, ln_g, ln_b)


def _routing_tables(tile_cnt):
    cnt = tile_cnt[:, 0, :N_EXPERTS]
    totals = jnp.sum(cnt, axis=0)
    padded = (totals + BM - 1) // BM * BM
    pends = jnp.cumsum(padded)
    pstarts = pends - padded
    start = pstarts[None, :] + jnp.cumsum(cnt, axis=0) - cnt
    off = jnp.cumsum(cnt, axis=1) - cnt
    n_valid = (pends[-1] // BM).astype(I32)
    block_start = jnp.arange(NB, dtype=I32) * BM
    block_e = jnp.minimum(jnp.sum(block_start[:, None] >= pends[None, :], axis=1), N_EXPERTS - 1)
    last_e = block_e[jnp.maximum(n_valid - 1, 0)]
    block_e = jnp.where(jnp.arange(NB) < n_valid, block_e, last_e)
    ids = jnp.arange(N_EXPERTS, dtype=I32)
    order = jnp.sort(jnp.where(totals > 0, ids, N_EXPERTS))
    order = jnp.concatenate([jnp.where(order < N_EXPERTS, order, -1), jnp.full((2,), -1, I32)])
    return {
        "start": (start.reshape(-1) * ROW_SLABS).astype(I32),
        "len": (cnt.reshape(-1) * ROW_SLABS).astype(I32),
        "off": (off.reshape(-1) * ROW_SLABS).astype(I32),
        "pad_start": ((pstarts + totals) * ROW_SLABS).astype(I32),
        "pad_len": ((padded - totals) * ROW_SLABS).astype(I32),
        "n_valid": n_valid.reshape(1),
        "block_expert": block_e.astype(I32),
        "expert_order": order.astype(I32),
    }


def _stacked_params(p):
    eye = jnp.eye(N_LRU_BLOCKS, dtype=F32)
    pad = LANES - N_EXPERTS
    return {
        "lam": p["lam"],
        "subln_col": p["subln_g"].reshape(DEPTH, HEAD_DIM, 1),
        "sgu_ln_g": p["sgu_ln_g"].reshape(DEPTH, 1, SGU_WIDTH),
        "sgu_ln_b": p["sgu_ln_b"].reshape(DEPTH, 1, SGU_WIDTH),
        "w_spatial": p["w_spatial"],
        "b_spatial_full": jnp.repeat(jnp.swapaxes(p["b_spatial"], 1, 2), SGU_GROUP, axis=2),
        "conv_w": p["conv_w"],
        "conv_b": p["conv_b"].reshape(DEPTH, 1, LRU_WIDTH),
        "w_rg_full": jnp.einsum("ldkgio,gh->lgidkho", p["w_rg"], eye).reshape(DEPTH, LRU_WIDTH, 4 * LRU_WIDTH),
        "b_rg_full": p["b_rg"].reshape(DEPTH, 1, 4 * LRU_WIDTH),
        "lru_log": p["lru_log"],
        "w_out_all": p["w_out"],
        "ln_g": p["ln_g"].reshape(DEPTH, 2, 1, D_MODEL),
        "ln_b": p["ln_b"].reshape(DEPTH, 2, 1, D_MODEL),
        "w_router_pad": jnp.pad(p["w_router"], ((0, 0), (0, 0), (0, pad))),
        "b_router_pad": jnp.pad(p["b_router"], ((0, 0), (0, pad))).reshape(DEPTH, 1, LANES),
    }


def kernel(x_prompt, x_sample, cache_k, cache_v, state_lru, c, c_ctx, w_mod, b_mod, w_in, lam, subln_g, sgu_ln_g, sgu_ln_b, w_spatial, b_spatial, conv_w, conv_b, w_rg, b_rg, lru_log, w_out, ln_g, ln_b, w_router, b_router, w_gu, b_gu, w_down, b_down):
    lp = _stacked_params(dict(
        lam=lam, subln_g=subln_g, sgu_ln_g=sgu_ln_g, sgu_ln_b=sgu_ln_b, w_spatial=w_spatial, b_spatial=b_spatial,
        conv_w=conv_w, conv_b=conv_b, w_rg=w_rg, b_rg=b_rg, lru_log=lru_log, w_out=w_out, ln_g=ln_g, ln_b=ln_b,
        w_router=w_router, b_router=b_router))
    cvec8 = jnp.concatenate([c_ctx[None, :], c, jnp.zeros((SUBLANES - 1 - DEC_BATCH, D_MODEL), F32)], axis=0)
    mod = _modulation(cvec8, w_mod, b_mod)
    x_pair = (x_prompt.reshape(N_CTX, D_MODEL), x_sample.reshape(N_DEC, D_MODEL))
    tables = _rope_tables()
    zero_state = jnp.zeros((BATCH, 1, 2, LRU_WIDTH), F32)
    prev_kv = None
    new_s = []
    for li in range(DEPTH):
        lam_init = 0.8 - 0.6 * math.exp(-0.3 * li)
        proj = _input_projection(*x_pair, mod, w_in, li)
        att_ctx, kc, vc = _context_attention(proj, lp["lam"], lp["subln_col"], li, lam_init, prev_kv)
        prev_kv = (kc, vc)
        att_dec = _denoise_attention(proj, cache_k, cache_v, lp["lam"], lp["subln_col"], li, lam_init, tables)
        sgu_ctx, lru_ctx, h_ctx = _mixers(proj, zero_state, 0, lp, li, SEQ, BATCH, 0)
        sgu_dec, lru_dec, _ = _mixers(proj, state_lru, li, lp, li, DEC_SEQ, DEC_BATCH, N_CTX // DEC_SEQ)
        x1, h2, route_i, route_f, tile_cnt = _output_projection(
            x_pair, (att_ctx, att_dec), (sgu_ctx, sgu_dec), (lru_ctx, lru_dec), mod, lp, li)
        rt = _routing_tables(tile_cnt)
        xs = _dispatch(rt, h2, route_i)
        y = _moe_blocks(rt, xs, w_gu, b_gu, w_down, b_down, li)
        x_pair = _combine(rt, y, x1, route_i, route_f, mod, lp["ln_g"], lp["ln_b"], li)
        new_s.append(h_ctx)
    y_prompt = x_pair[0].reshape(BATCH, SEQ, D_MODEL)
    y_sample = x_pair[1].reshape(DEC_BATCH, DEC_SEQ, D_MODEL)
    return (y_prompt, y_sample, prev_kv[0], prev_kv[1], jnp.stack(new_s, axis=1))
```

```python
import functools
import math

import numpy as np
import jax
import jax.numpy as jnp
from jax import lax
from jax.experimental import pallas as pl
from jax.experimental.pallas import tpu as pltpu

F32 = jnp.float32
BF16 = jnp.bfloat16
I32 = jnp.int32

D_MODEL = 1024
BATCH = 32
SEQ = 256
DEPTH = 2
DEC_BATCH = 2
DEC_SEQ = 1024
PAST_LEN = 256
GRID_W = 64
HEAD_DIM = 64
ATT_WIDTH = D_MODEL // 2
SGU_WIDTH = D_MODEL // 4
LRU_WIDTH = D_MODEL // 4
N_ATT_HEADS = ATT_WIDTH // HEAD_DIM
ATT_HALF = HEAD_DIM // 2
ROPE_FREQS = ATT_HALF // 4
ROPE_THETA = 10000.0
CHUNK = 128
N_SGU_GROUPS = 4
SGU_GROUP = SGU_WIDTH // N_SGU_GROUPS
N_LRU_BLOCKS = 4
LRU_BLOCK = LRU_WIDTH // N_LRU_BLOCKS
CONV_W = 4
LRU_C = 8.0
IN_COLS = 3 * ATT_WIDTH + 2 * SGU_WIDTH + 2 * LRU_WIDTH
N_EXPERTS = 32
TOP_K = 4
D_EXPERT = D_MODEL
SWIGLU_LIMIT = 7.0
SWIGLU_ALPHA = 1.702
DN_ALPHA = (2 * DEPTH) ** 0.25
EPS = 1e-5

N_CTX = BATCH * SEQ
N_DEC = DEC_BATCH * DEC_SEQ
N_TOK = N_CTX + N_DEC

LANES = 128
SUBLANES = 8
ROW_SLABS = D_MODEL // LANES

TM = 512
CTX_TILES = N_CTX // TM
TILES_PER_DEC = DEC_SEQ // TM
N_TILES = N_TOK // TM
MOD_TN = 1024
RT = 256
N_RT = N_TOK // RT
RT_PER_TM = TM // RT
RT_ROWS = RT * TOP_K
RT_PER_STEP = 4
FETCH_AHEAD = 2
BM = 256
NB = N_TOK * TOP_K // BM + N_EXPERTS
WEIGHT_PIECES = 8
X_BUFFERS = 4
N_SLOTS = NB * BM
RUN_BITS = tuple(1 << b for b in range(RT.bit_length() - 1, -1, -1))
PAD_BITS = tuple(1 << b for b in range(BM.bit_length() - 2, -1, -1))
VMEM_LIMIT = 56 * 1024 * 1024


def _cparams(n_axes):
    return pltpu.CompilerParams(
        dimension_semantics=("arbitrary",) * n_axes,
        vmem_limit_bytes=VMEM_LIMIT)


def _mod_row(i):
    return jnp.where(i < CTX_TILES, 0, 1 + (i - CTX_TILES) // TILES_PER_DEC)


def _layer_norm_rows(z, g, b):
    mu = jnp.mean(z, axis=-1, keepdims=True)
    zc = z - mu
    var = jnp.mean(zc * zc, axis=-1, keepdims=True)
    return zc * lax.rsqrt(var + EPS) * g + b


def _pair_specs(tile, width, n_ctx_tiles):
    ctx = pl.BlockSpec((tile, width), lambda i, *_: (jnp.minimum(i, n_ctx_tiles - 1), 0))
    dec = pl.BlockSpec((tile, width), lambda i, *_: (jnp.maximum(i - n_ctx_tiles, 0), 0))
    return [ctx, dec]


def _pair_read(i, n_ctx_tiles, ctx_ref, dec_ref):
    return jnp.where(i < n_ctx_tiles, ctx_ref[...], dec_ref[...])


def _layer_block(li, shape, *lead):
    index = (li,) + tuple(lead) + (0,) * len(shape)
    return pl.BlockSpec((None,) * (1 + len(lead)) + tuple(shape), lambda *_: index)


def _mod_kernel(cvec_ref, w_ref, b_ref, o_ref):
    cv = cvec_ref[...]
    s = cv * jax.nn.sigmoid(cv)
    s_t = s.T
    w = w_ref[0]
    rows = [jnp.sum(s_t[:, r:r + 1] * w, axis=0, keepdims=True) for r in range(1 + DEC_BATCH)]
    rows.append(jnp.zeros((SUBLANES - 1 - DEC_BATCH, MOD_TN), F32))
    o_ref[0] = jnp.concatenate(rows, axis=0) + b_ref[0]


def _modulation(cvec8, w_mod, b_mod):
    n_out = w_mod.shape[-1]
    return pl.pallas_call(
        _mod_kernel,
        grid=(DEPTH, n_out // MOD_TN),
        in_specs=[
            pl.BlockSpec((SUBLANES, D_MODEL), lambda l, j: (0, 0)),
            pl.BlockSpec((1, D_MODEL, MOD_TN), lambda l, j: (l, 0, j)),
            pl.BlockSpec((1, 1, MOD_TN), lambda l, j: (l, 0, j)),
        ],
        out_specs=pl.BlockSpec((1, SUBLANES, MOD_TN), lambda l, j: (l, 0, j)),
        out_shape=jax.ShapeDtypeStruct((DEPTH, SUBLANES, n_out), F32),
        compiler_params=_cparams(2),
        name="modulation",
    )(cvec8, w_mod, b_mod.reshape(DEPTH, 1, n_out))


def _inproj_kernel(xc_ref, xd_ref, mod_ref, w_ref, o_ref, wbf_ref):
    i = pl.program_id(0)

    @pl.when(i == 0)
    def _():
        wbf_ref[...] = w_ref[...].astype(BF16)

    m = mod_ref[pl.ds(_mod_row(i), 1), :]
    sh1 = m[:, 0:D_MODEL]
    sc1 = m[:, D_MODEL:2 * D_MODEL]
    x = _pair_read(i, CTX_TILES, xc_ref, xd_ref)
    h = (x * (1.0 + sc1) + sh1).astype(BF16)
    o_ref[...] = jnp.dot(h, wbf_ref[...], preferred_element_type=F32)


def _input_projection(x_ctx, x_dec, mod, w_in, li):
    return pl.pallas_call(
        _inproj_kernel,
        grid=(N_TILES,),
        in_specs=_pair_specs(TM, D_MODEL, CTX_TILES) + [
            pl.BlockSpec((None, SUBLANES, 6 * D_MODEL), lambda i: (li, 0, 0)),
            pl.BlockSpec((None, D_MODEL, IN_COLS), lambda i: (li, 0, 0), pipeline_mode=pl.Buffered(1)),
        ],
        out_specs=pl.BlockSpec((TM, IN_COLS), lambda i: (i, 0)),
        out_shape=jax.ShapeDtypeStruct((N_TOK, IN_COLS), F32),
        scratch_shapes=[pltpu.VMEM((D_MODEL, IN_COLS), BF16)],
        compiler_params=_cparams(1),
        name="input_projection",
    )(x_ctx, x_dec, mod, w_in)


def _lambda(lam_ref, lam_init):
    lm = lam_ref[...]
    a = jnp.sum(lm[0:1] * lm[1:2], axis=-1, keepdims=True)
    b = jnp.sum(lm[2:3] * lm[3:4], axis=-1, keepdims=True)
    return jnp.exp(a) - jnp.exp(b) + lam_init


_NT = (((1,), (1,)), ((), ()))
_TN = (((0,), (0,)), ((), ()))
_ATT_SCALE = ATT_HALF ** -0.5
_Q_SCALE = _ATT_SCALE * math.log2(math.e)


def _attention_heads(pairs, values, e_scr, lam, g_col, lam_init):
    n_heads = len(pairs)
    n_q = e_scr.shape[2]

    def scores(keys, query):
        return lax.dot_general(keys, query, _NT, preferred_element_type=F32)

    tops = []
    for h in range(n_heads):
        for c in range(2):
            seg_tops = [jnp.max(scores(k, q).reshape(k.shape[0] // SUBLANES, SUBLANES, n_q), axis=0)
                        for k, q in pairs[h][c]]
            tops.append(jnp.max(functools.reduce(jnp.maximum, seg_tops), axis=0, keepdims=True))
    for h in range(n_heads):
        for c in range(2):
            n = 2 * h + c
            row = 0
            for k, q in pairs[h][c]:
                e_scr[n, row:row + k.shape[0], :] = jnp.exp2(scores(k, q) - tops[n]).astype(BF16)
                row += k.shape[0]
    outs = []
    for h in range(n_heads):
        v_aug = jnp.concatenate([values[h], jnp.ones_like(values[h])], axis=-1)
        normed = []
        for c in range(2):
            acc = lax.dot_general(v_aug, e_scr[2 * h + c], _TN, preferred_element_type=F32)
            normed.append(acc[:HEAD_DIM] * (1.0 / acc[HEAD_DIM:HEAD_DIM + 1]))
        o_t = normed[0] - lam * normed[1]
        ms = jnp.mean(o_t * o_t, axis=0, keepdims=True)
        outs.append(o_t * lax.rsqrt(ms + EPS) * g_col * (1.0 - lam_init))
    return outs


def _component_query(q_window, c, lo):
    lane = lax.broadcasted_iota(I32, q_window.shape, 1)
    first = lo + c * ATT_HALF
    inside = jnp.logical_and(lane >= first, lane < first + ATT_HALF)
    return jnp.where(inside, q_window, jnp.zeros_like(q_window))


def _lane_tile(x, h):
    heads_per_tile = LANES // HEAD_DIM
    t = h // heads_per_tile
    return x[:, t * LANES:(t + 1) * LANES], (h % heads_per_tile) * HEAD_DIM


def _ctx_attn_kernel(lam_ref, g_ref, q_ref, k_ref, v_ref, *rest, lam_init, stacked):
    *rest, e_scr = rest
    if stacked:
        _, _, att_ref, ck_ref, cv_ref = rest
    else:
        att_ref, ck_ref, cv_ref = rest
        ck_ref[0, 1] = jnp.zeros(ck_ref.shape[2:], F32)
        cv_ref[0, 1] = jnp.zeros(cv_ref.shape[2:], F32)
    put_k = lambda h, val: ck_ref.__setitem__((0, 0, h), val)
    put_v = lambda h, val: cv_ref.__setitem__((0, 0, h), val)
    lam = _lambda(lam_ref, lam_init)
    k = k_ref[...]
    v = v_ref[...]
    qb = (q_ref[...] * _Q_SCALE).astype(BF16)
    kb = k.astype(BF16)
    vb = v.astype(BF16)
    heads = range(N_ATT_HEADS)
    pairs = []
    for h in heads:
        (k_tile, lo), (q_tile, _) = _lane_tile(kb, h), _lane_tile(qb, h)
        pairs.append([[(k_tile, _component_query(q_tile, c, lo))] for c in range(2)])
    outs = _attention_heads(pairs, [vb[:, h * HEAD_DIM:(h + 1) * HEAD_DIM] for h in heads],
                            e_scr, lam, g_ref[...], lam_init)
    for h in heads:
        put_k(h, k[:, h * HEAD_DIM:(h + 1) * HEAD_DIM])
        put_v(h, v[:, h * HEAD_DIM:(h + 1) * HEAD_DIM])
    att_ref[...] = jnp.concatenate(outs, axis=0).T


def _context_attention(proj, lam, subln_col, li, lam_init, prev_kv=None):
    stacked = prev_kv is not None
    kv_shape = jax.ShapeDtypeStruct((BATCH, DEPTH, N_ATT_HEADS, SEQ, HEAD_DIM), F32)
    if stacked:
        kv_spec = pl.BlockSpec((1, 1, N_ATT_HEADS, SEQ, HEAD_DIM), lambda b: (b, 1, 0, 0, 0))
    else:
        kv_spec = pl.BlockSpec((1, DEPTH, N_ATT_HEADS, SEQ, HEAD_DIM), lambda b: (b, 0, 0, 0, 0))
    n_fixed_inputs = 5
    return pl.pallas_call(
        functools.partial(_ctx_attn_kernel, lam_init=lam_init, stacked=stacked),
        grid=(BATCH,),
        input_output_aliases={n_fixed_inputs: 1, n_fixed_inputs + 1: 2} if stacked else {},
        in_specs=[
            _layer_block(li, (4, ATT_HALF)),
            _layer_block(li, (HEAD_DIM, 1)),
            pl.BlockSpec((SEQ, ATT_WIDTH), lambda b: (b, 0)),
            pl.BlockSpec((SEQ, ATT_WIDTH), lambda b: (b, 1)),
            pl.BlockSpec((SEQ, ATT_WIDTH), lambda b: (b, 2)),
        ] + ([pl.BlockSpec(memory_space=pl.ANY)] * 2 if stacked else []),
        out_specs=[pl.BlockSpec((SEQ, ATT_WIDTH), lambda b: (b, 0)), kv_spec, kv_spec],
        out_shape=[jax.ShapeDtypeStruct((N_CTX, ATT_WIDTH), F32), kv_shape, kv_shape],
        scratch_shapes=[pltpu.VMEM((2 * N_ATT_HEADS, SEQ, SEQ), BF16)],
        compiler_params=_cparams(1),
        name="context_attention",
    )(lam, subln_col, proj, proj, proj, *(prev_kv if stacked else ()))


def _rope_tables():
    t = np.arange(DEC_SEQ)
    pos = np.stack([t // GRID_W, t % GRID_W], axis=1).astype(np.float32)
    inv = (np.float32(ROPE_THETA) ** (-np.arange(ROPE_FREQS, dtype=np.float32) / np.float32(ROPE_FREQS)))
    j = np.arange(HEAD_DIM)
    d = j % ATT_HALF
    axis = d // (2 * ROPE_FREQS)
    u = d % (2 * ROPE_FREQS)
    ang = pos[:, axis] * inv[u % ROPE_FREQS][None, :].astype(np.float32)
    cos = np.cos(ang).astype(np.float32)
    sin = np.sin(ang).astype(np.float32)
    first = (u < ROPE_FREQS)[None, :]
    s_next = np.where(first, -sin, 0.0).astype(np.float32)
    s_prev = np.where(first, 0.0, sin).astype(np.float32)
    tile = lambda a: jnp.asarray(np.tile(a, (1, N_ATT_HEADS)))
    return tile(cos), tile(s_next), tile(s_prev)


def _rotate(x, cos, s_next, s_prev):
    width = x.shape[-1]
    return (x * cos + pltpu.roll(x, width - ROPE_FREQS, axis=1) * s_next
            + pltpu.roll(x, ROPE_FREQS, axis=1) * s_prev)


QB = 256
Q_STEPS = DEC_SEQ // QB
DEC_HEAD_GROUP = 4


def _dec_attn_kernel(lam_ref, g_ref, q_ref, k_ref, v_ref, ck_ref, cv_ref,
                     cq_ref, snq_ref, spq_ref, ck_tab, snk_tab, spk_tab,
                     att_ref, krot_ref, e_scr, *, lam_init):
    j = pl.program_id(1)

    @pl.when(j == 0)
    def _():
        krot_ref[...] = _rotate(k_ref[...], ck_tab[...], snk_tab[...], spk_tab[...]).astype(BF16)

    lam = _lambda(lam_ref, lam_init)
    qb = (_rotate(q_ref[...], cq_ref[...], snq_ref[...], spq_ref[...]) * _Q_SCALE).astype(BF16)
    kb = krot_ref[...]
    vb = v_ref[...].astype(BF16)
    outs = []
    for h0 in range(0, N_ATT_HEADS, DEC_HEAD_GROUP):
        heads = range(h0, h0 + DEC_HEAD_GROUP)
        pairs = []
        for h in heads:
            past_k = ck_ref[0, 0, h].astype(BF16)
            q_head = qb[:, h * HEAD_DIM:(h + 1) * HEAD_DIM]
            (k_tile, lo), (q_tile, _) = _lane_tile(kb, h), _lane_tile(qb, h)
            pairs.append([[(past_k, _component_query(q_head, c, 0)), (k_tile, _component_query(q_tile, c, lo))]
                          for c in range(2)])
        values = [jnp.concatenate([cv_ref[0, 0, h].astype(BF16), vb[:, h * HEAD_DIM:(h + 1) * HEAD_DIM]], axis=0)
                  for h in heads]
        outs += _attention_heads(pairs, values, e_scr, lam, g_ref[...], lam_init)
    att_ref[...] = jnp.concatenate(outs, axis=0).T


def _denoise_attention(proj, cache_k, cache_v, lam, subln_col, li, lam_init, tables):
    cos, s_next, s_prev = tables
    row0 = N_CTX // QB
    seq0 = N_CTX // DEC_SEQ
    q_tab = pl.BlockSpec((QB, ATT_WIDTH), lambda b, j: (j, 0))
    k_tab = pl.BlockSpec((DEC_SEQ, ATT_WIDTH), lambda b, j: (0, 0))
    cache_spec = pl.BlockSpec((1, 1, N_ATT_HEADS, PAST_LEN, HEAD_DIM), lambda b, j: (b, li, 0, 0, 0))
    return pl.pallas_call(
        functools.partial(_dec_attn_kernel, lam_init=lam_init),
        grid=(DEC_BATCH, Q_STEPS),
        in_specs=[
            _layer_block(li, (4, ATT_HALF)),
            _layer_block(li, (HEAD_DIM, 1)),
            pl.BlockSpec((QB, ATT_WIDTH), lambda b, j: (row0 + b * Q_STEPS + j, 0)),
            pl.BlockSpec((DEC_SEQ, ATT_WIDTH), lambda b, j: (seq0 + b, 1)),
            pl.BlockSpec((DEC_SEQ, ATT_WIDTH), lambda b, j: (seq0 + b, 2)),
            cache_spec, cache_spec,
            q_tab, q_tab, q_tab, k_tab, k_tab, k_tab,
        ],
        out_specs=pl.BlockSpec((QB, ATT_WIDTH), lambda b, j: (b * Q_STEPS + j, 0)),
        out_shape=jax.ShapeDtypeStruct((N_DEC, ATT_WIDTH), F32),
        scratch_shapes=[pltpu.VMEM((DEC_SEQ, ATT_WIDTH), BF16),
                        pltpu.VMEM((2 * DEC_HEAD_GROUP, PAST_LEN + DEC_SEQ, QB), BF16)],
        compiler_params=_cparams(2),
        name="denoise_attention",
    )(lam, subln_col, proj, proj, proj, cache_k, cache_v,
      cos, s_next, s_prev, cos, s_next, s_prev)


def _softplus(z):
    return jnp.maximum(z, 0.0) + jnp.log1p(jnp.exp(-jnp.abs(z)))


def _mixer_kernel(su_ref, sv_ref, rx_ref, rg_ref, h0_ref, lng_ref, lnb_ref, ws_ref, bs_ref,
                  cw_ref, cb_ref, wrg_ref, brg_ref, lrulog_ref,
                  sgu_ref, lru_ref, hlast_ref, a_scr, b_scr, h_scr, *, seq_len):
    vn = _layer_norm_rows(sv_ref[...], lng_ref[...], lnb_ref[...])
    lane_group = lax.broadcasted_iota(I32, (CHUNK, SGU_WIDTH), 1) // SGU_GROUP
    for n in range(seq_len // CHUNK):
        rows = slice(n * CHUNK, (n + 1) * CHUNK)
        vc = vn[rows].astype(BF16)
        s = jnp.zeros((CHUNK, SGU_WIDTH), F32)
        for g in range(N_SGU_GROUPS):
            sg = jnp.dot(ws_ref[g].astype(BF16), vc, preferred_element_type=F32)
            s = jnp.where(lane_group == g, sg, s)
        sgu_ref[rows, :] = su_ref[rows, :] * (s + bs_ref[...])

    x = rx_ref[...]
    row = lax.broadcasted_iota(I32, (seq_len, LRU_WIDTH), 0)

    def shifted(val, d, fill):
        rolled = pltpu.roll(val, d % seq_len, axis=0)
        inside = (row >= d) if d > 0 else (row < seq_len + d)
        return jnp.where(inside, rolled, fill)

    left = CONV_W // 2
    xc = cb_ref[...] + x * cw_ref[left:left + 1, :]
    for tap in range(CONV_W):
        if tap != left:
            xc = xc + shifted(x, left - tap, 0.0) * cw_ref[tap:tap + 1, :]
    pre = jnp.dot(xc.astype(BF16), wrg_ref[...].astype(BF16), preferred_element_type=F32) + brg_ref[...]
    gates = 0.5 + 0.5 * jnp.tanh(0.5 * pre)
    in_chunk = row % SUBLANES
    n_chunks = seq_len // SUBLANES
    h0 = h0_ref[0, 0]
    lasts = []
    for direction in range(2):
        reverse = direction == 1
        base = direction * 2 * LRU_WIDTH
        r = gates[:, base:base + LRU_WIDTH]
        gi = gates[:, base + LRU_WIDTH:base + 2 * LRU_WIDTH]
        log_a = -LRU_C * r * _softplus(-lrulog_ref[direction:direction + 1, :])
        a = jnp.exp(log_a)
        b = jnp.sqrt(-jnp.tanh(log_a) * (a * a + 1.0)) * gi * xc
        def chunk_roll(val, shift):
            chunks = val.reshape(n_chunks, SUBLANES, LRU_WIDTH)
            return pltpu.roll(chunks, shift, axis=1).reshape(seq_len, LRU_WIDTH)

        for d in (1, 2, 4):
            if reverse:
                inside = in_chunk < SUBLANES - d
                a_n = jnp.where(inside, chunk_roll(a, SUBLANES - d), 1.0)
                b_n = jnp.where(inside, chunk_roll(b, SUBLANES - d), 0.0)
            else:
                inside = in_chunk >= d
                a_n = jnp.where(inside, chunk_roll(a, d), 1.0)
                b_n = jnp.where(inside, chunk_roll(b, d), 0.0)
            b = a * b_n + b
            a = a * a_n
        a_scr[...] = a
        b_scr[...] = b

        def chunk_step(c, carry, reverse=reverse):
            cc = n_chunks - 1 - c if reverse else c
            off = pl.multiple_of(cc * SUBLANES, SUBLANES)
            hc = a_scr[pl.ds(off, SUBLANES), :] * carry + b_scr[pl.ds(off, SUBLANES), :]
            if reverse:
                h_scr[pl.ds(off, SUBLANES), :] = h_scr[pl.ds(off, SUBLANES), :] + hc
                return hc[0:1, :]
            h_scr[pl.ds(off, SUBLANES), :] = hc
            return hc[SUBLANES - 1:SUBLANES, :]

        lasts.append(lax.fori_loop(0, n_chunks, chunk_step, h0[direction:direction + 1, :]))
    lru_ref[...] = h_scr[...] * jax.nn.gelu(rg_ref[...])
    hlast_ref[0] = jnp.concatenate(lasts, axis=0)


def _mixers(proj, h0, h0_layer, lp, li, seq_len, n_seq, row_block0):
    col0 = 3 * ATT_WIDTH // SGU_WIDTH
    col = lambda c: pl.BlockSpec((seq_len, SGU_WIDTH), lambda b: (row_block0 + b, col0 + c))
    full = lambda shape: _layer_block(li, shape)
    out_rows = pl.BlockSpec((seq_len, SGU_WIDTH), lambda b: (b, 0))
    return pl.pallas_call(
        functools.partial(_mixer_kernel, seq_len=seq_len),
        grid=(n_seq,),
        in_specs=[
            col(0), col(1), col(2), col(3),
            pl.BlockSpec((1, 1, 2, LRU_WIDTH), lambda b: (b, h0_layer, 0, 0)),
            full((1, SGU_WIDTH)), full((1, SGU_WIDTH)),
            full((N_SGU_GROUPS, CHUNK, CHUNK)), full((CHUNK, SGU_WIDTH)),
            full((CONV_W, LRU_WIDTH)), full((1, LRU_WIDTH)),
            full((LRU_WIDTH, 4 * LRU_WIDTH)), full((1, 4 * LRU_WIDTH)),
            full((2, LRU_WIDTH)),
        ],
        out_specs=[out_rows, out_rows, pl.BlockSpec((1, 2, LRU_WIDTH), lambda b: (b, 0, 0))],
        out_shape=[
            jax.ShapeDtypeStruct((n_seq * seq_len, SGU_WIDTH), F32),
            jax.ShapeDtypeStruct((n_seq * seq_len, LRU_WIDTH), F32),
            jax.ShapeDtypeStruct((n_seq, 2, LRU_WIDTH), F32),
        ],
        scratch_shapes=[pltpu.VMEM((seq_len, LRU_WIDTH), F32)] * 3,
        compiler_params=_cparams(1),
        name="mixers_%d" % seq_len,
    )(proj, proj, proj, proj, h0, lp["sgu_ln_g"], lp["sgu_ln_b"], lp["w_spatial"], lp["b_spatial_full"],
      lp["conv_w"], lp["conv_b"], lp["w_rg_full"], lp["b_rg_full"], lp["lru_log"])


def _outproj_kernel(xc_ref, xd_ref, ac_ref, ad_ref, sc_ref, sd_ref, lc_ref, ld_ref,
                    mod_ref, wout_ref, lng_ref, lnb_ref, wr_ref, br_ref,
                    x1_ref, h2_ref, ri_ref, rf_ref, cnt_ref, wbf_ref):
    i = pl.program_id(0)

    @pl.when(i == 0)
    def _():
        wbf_ref[...] = wout_ref[...].astype(BF16)

    m = mod_ref[pl.ds(_mod_row(i), 1), :]
    g1 = m[:, 2 * D_MODEL:3 * D_MODEL]
    sh2 = m[:, 3 * D_MODEL:4 * D_MODEL]
    sc2 = m[:, 4 * D_MODEL:5 * D_MODEL]
    a0, a1 = ATT_WIDTH, ATT_WIDTH + SGU_WIDTH
    x = _pair_read(i, CTX_TILES, xc_ref, xd_ref)
    att = _pair_read(i, CTX_TILES, ac_ref, ad_ref).astype(BF16)
    sgu = _pair_read(i, CTX_TILES, sc_ref, sd_ref).astype(BF16)
    lru = _pair_read(i, CTX_TILES, lc_ref, ld_ref).astype(BF16)
    mix = (jnp.dot(att, wbf_ref[0:a0, :], preferred_element_type=F32)
           + jnp.dot(sgu, wbf_ref[a0:a1, :], preferred_element_type=F32)
           + jnp.dot(lru, wbf_ref[a1:, :], preferred_element_type=F32))
    x1 = _layer_norm_rows(DN_ALPHA * x + g1 * mix, lng_ref[...], lnb_ref[...])
    x1_ref[...] = x1
    h2 = x1 * (1.0 + sc2) + sh2
    h2_ref[...] = h2.astype(BF16)

    logits = jnp.dot(h2.astype(BF16), wr_ref[...].astype(BF16), preferred_element_type=F32) + br_ref[...]
    lane = lax.broadcasted_iota(I32, (TM, LANES), 1)
    lane_f = lane.astype(F32)
    neg_inf = jnp.float32(-jnp.inf)
    work = jnp.where(lane < N_EXPERTS, logits, neg_inf)
    vals, idxs = [], []
    for _ in range(TOP_K):
        top = jnp.max(work, axis=-1, keepdims=True)
        idx = jnp.min(jnp.where(work == top, lane_f, float(LANES)), axis=-1, keepdims=True)
        vals.append(top)
        idxs.append(idx)
        work = jnp.where(lane_f == idx, neg_inf, work)
    exps = [jnp.exp(v - vals[0]) for v in vals]
    denom = exps[0] + exps[1] + exps[2] + exps[3]
    onehot = jnp.zeros((TM, LANES), F32)
    for idx in idxs:
        onehot = onehot + (lane_f == idx).astype(F32)
    r_i = lax.broadcasted_iota(I32, (RT, RT), 0)
    c_i = lax.broadcasted_iota(I32, (RT, RT), 1)
    tri = (r_i > c_i).astype(F32).astype(BF16)
    upper = (lax.broadcasted_iota(I32, (LANES, LANES), 0)
             < lax.broadcasted_iota(I32, (LANES, LANES), 1)).astype(F32).astype(BF16)
    packed_pos = []
    for t in range(RT_PER_TM):
        hot = onehot[t * RT:(t + 1) * RT]
        total = jnp.broadcast_to(jnp.sum(hot, axis=0, keepdims=True), (SUBLANES, LANES))
        cnt_ref[t] = total.astype(I32)
        run_start = jnp.dot(total.astype(BF16), upper, preferred_element_type=F32)[0:1, :]
        packed_pos.append(jnp.dot(tri, hot.astype(BF16), preferred_element_type=F32) + run_start)
    packed_pos = jnp.concatenate(packed_pos, axis=0)
    ri = jnp.zeros((TM, LANES), F32)
    rf = jnp.zeros((TM, LANES), F32)
    for k in range(TOP_K):
        pos = jnp.sum(jnp.where(lane_f == idxs[k], packed_pos, 0.0), axis=-1, keepdims=True)
        ri = jnp.where(lane == k, idxs[k], ri)
        ri = jnp.where(lane == TOP_K + k, pos, ri)
        rf = jnp.where(lane == k, exps[k] / denom, rf)
    ri_ref[...] = ri.astype(I32)
    rf_ref[...] = rf


def _output_projection(x_pair, att_pair, sgu_pair, lru_pair, mod, lp, li):
    rows = lambda w: pl.BlockSpec((TM, w), lambda i: (i, 0))
    full = lambda shape: _layer_block(li, shape)
    return pl.pallas_call(
        _outproj_kernel,
        grid=(N_TILES,),
        in_specs=(_pair_specs(TM, D_MODEL, CTX_TILES) + _pair_specs(TM, ATT_WIDTH, CTX_TILES)
                  + _pair_specs(TM, SGU_WIDTH, CTX_TILES) + _pair_specs(TM, LRU_WIDTH, CTX_TILES) + [
            pl.BlockSpec((None, SUBLANES, 6 * D_MODEL), lambda i: (li, 0, 0)),
            pl.BlockSpec((None, D_MODEL, D_MODEL), lambda i: (li, 0, 0), pipeline_mode=pl.Buffered(1)),
            _layer_block(li, (1, D_MODEL), 0), _layer_block(li, (1, D_MODEL), 0),
            full((D_MODEL, LANES)), full((1, LANES)),
        ]),
        out_specs=[
            rows(D_MODEL), rows(D_MODEL), rows(LANES), rows(LANES),
            pl.BlockSpec((RT_PER_TM, SUBLANES, LANES), lambda i: (i, 0, 0)),
        ],
        out_shape=[
            jax.ShapeDtypeStruct((N_TOK, D_MODEL), F32),
            jax.ShapeDtypeStruct((N_TOK, D_MODEL), BF16),
            jax.ShapeDtypeStruct((N_TOK, LANES), I32),
            jax.ShapeDtypeStruct((N_TOK, LANES), F32),
            jax.ShapeDtypeStruct((N_RT, SUBLANES, LANES), I32),
        ],
        scratch_shapes=[pltpu.VMEM((D_MODEL, D_MODEL), BF16)],
        compiler_params=_cparams(1),
        name="output_projection",
    )(*x_pair, *att_pair, *sgu_pair, *lru_pair, mod, lp["w_out_all"], lp["ln_g"], lp["ln_b"],
      lp["w_router_pad"], lp["b_router_pad"])


def _row_ds(row, n_rows):
    return pl.ds(pl.multiple_of(row * ROW_SLABS, ROW_SLABS), n_rows * ROW_SLABS)


def _for_each_run_piece(length_s, bits, fn):
    for bit in bits:
        done_s = length_s & (-2 * bit * ROW_SLABS)
        @pl.when((length_s & (bit * ROW_SLABS)) != 0)
        def _(done_s=done_s, bit=bit):
            fn(done_s, bit)


def _stored_ds(row_s, n_rows):
    return pl.ds(pl.multiple_of(row_s, ROW_SLABS), n_rows * ROW_SLABS)


def _tile_run_copies(tile, start_ref, len_ref, off_ref, sorted_hbm, buf, sem, to_sorted, live=None):
    for e in range(N_EXPERTS):
        t = tile * N_EXPERTS + e
        start_s, length_s, off_s = start_ref[t], len_ref[t], off_ref[t]
        if live is not None:
            length_s = jnp.where(live, length_s, 0)

        def piece(done_s, bit, start_s=start_s, off_s=off_s):
            packed = buf.at[_stored_ds(off_s + done_s, bit), :]
            srt = sorted_hbm.at[_stored_ds(start_s + done_s, bit), :]
            if to_sorted:
                pltpu.make_async_copy(packed, srt, sem).start()
            else:
                pltpu.make_async_copy(srt, packed, sem).start()

        _for_each_run_piece(length_s, RUN_BITS, piece)


def _tile_runs_wait(buf, sem):
    pltpu.make_async_copy(buf, buf, sem).wait()


def _packed_positions(ri):
    return [ri[:, TOP_K + k:TOP_K + k + 1].astype(F32) for k in range(TOP_K)]


def _slab_columns(buf, row0, n_rows):
    return jnp.concatenate(
        [buf[pl.ds(row0 * ROW_SLABS + s, n_rows, stride=ROW_SLABS), :] for s in range(ROW_SLABS)], axis=-1)


def _dispatch_kernel(start_ref, len_ref, off_ref, pstart_ref, plen_ref, nv_ref,
                     h2_ref, ri_ref, xs_hbm, *scratch):
    bufs, (zbuf, sem, zsem) = scratch[:RT_PER_STEP], scratch[RT_PER_STEP:]
    i = pl.program_id(0)
    n_steps = pl.num_programs(0)

    @pl.when(i == 0)
    def _():
        zbuf[...] = jnp.zeros_like(zbuf)

        def zero_fill(wait):
            def go(cp):
                cp.wait() if wait else cp.start()

            def per_expert(e, carry):
                def piece(done_s, bit):
                    go(pltpu.make_async_copy(zbuf.at[pl.ds(0, bit * ROW_SLABS), :],
                                             xs_hbm.at[_stored_ds(pstart_ref[e] + done_s, bit), :], zsem))
                _for_each_run_piece(plen_ref[e], PAD_BITS, piece)
                return carry
            lax.fori_loop(0, N_EXPERTS, per_expert, 0)

            def per_block(b, carry):
                go(pltpu.make_async_copy(zbuf, xs_hbm.at[_row_ds(b * BM, BM), :], zsem))
                return carry
            lax.fori_loop(nv_ref[0], NB, per_block, 0)

        zero_fill(False)
        zero_fill(True)

    def send(tile, t, live):
        _tile_run_copies(tile, start_ref, len_ref, off_ref, xs_hbm, bufs[t], sem.at[t], to_sorted=True, live=live)

    col = lax.broadcasted_iota(I32, (RT, RT_ROWS), 1).astype(F32)
    for t in range(RT_PER_STEP):
        buf = bufs[t]
        rows = slice(t * RT, (t + 1) * RT)
        tile = i * RT_PER_STEP + t

        @pl.when(i >= 1)
        def _(buf=buf, t=t):
            _tile_runs_wait(buf, sem.at[t])

        send(jnp.maximum(tile - 1, 0), (t - 1) % RT_PER_STEP, tile >= 1)
        pos = _packed_positions(ri_ref[rows, :])
        sel = jnp.zeros((RT, RT_ROWS), F32)
        for p in pos:
            sel = sel + (col == p).astype(F32)
        packed = lax.dot_general(sel.astype(BF16), h2_ref[rows, :], _TN, preferred_element_type=F32)
        for s in range(ROW_SLABS):
            buf[pl.ds(s, RT_ROWS, stride=ROW_SLABS), :] = packed[:, s * LANES:(s + 1) * LANES]

    @pl.when(i == n_steps - 1)
    def _():
        send(N_RT - 1, RT_PER_STEP - 1, True)
        for t in range(RT_PER_STEP):
            _tile_runs_wait(bufs[t], sem.at[t])


def _dispatch(tables, h2, route_i):
    grid_spec = pltpu.PrefetchScalarGridSpec(
        num_scalar_prefetch=6,
        grid=(N_RT // RT_PER_STEP,),
        in_specs=[
            pl.BlockSpec((RT_PER_STEP * RT, D_MODEL), lambda i, *_: (i, 0)),
            pl.BlockSpec((RT_PER_STEP * RT, LANES), lambda i, *_: (i, 0)),
        ],
        out_specs=pl.BlockSpec(memory_space=pl.ANY),
        scratch_shapes=[pltpu.VMEM((RT_ROWS * ROW_SLABS, LANES), F32)] * RT_PER_STEP + [
            pltpu.VMEM((BM * ROW_SLABS, LANES), F32),
            pltpu.SemaphoreType.DMA((RT_PER_STEP,)),
            pltpu.SemaphoreType.DMA(()),
        ],
    )
    return pl.pallas_call(
        _dispatch_kernel,
        grid_spec=grid_spec,
        out_shape=jax.ShapeDtypeStruct((N_SLOTS * ROW_SLABS, LANES), F32),
        compiler_params=_cparams(1),
        name="moe_dispatch",
    )(tables["start"], tables["len"], tables["off"], tables["pad_start"], tables["pad_len"], tables["n_valid"],
      h2, route_i)


def _moe_kernel(be_ref, eo_ref, nv_ref, xs_hbm, wgu_hbm, bgu_ref, wdn_hbm, bdn_ref, y_hbm,
                xbuf, ybuf, gu_stage, dn_stage, wgu_bf, wdn_bf, wsem, xsem, ysem, *, li):
    n_valid = nv_ref[0]
    rows_per_piece = D_MODEL // WEIGHT_PIECES

    def weight_copies(e, stage):
        cps = []
        for p in range(WEIGHT_PIECES):
            band = pl.ds(p * rows_per_piece, rows_per_piece)
            cps.append(pltpu.make_async_copy(wgu_hbm.at[li, e, band, :], gu_stage.at[stage, band, :], wsem.at[stage]))
            cps.append(pltpu.make_async_copy(wdn_hbm.at[li, e, band, :], dn_stage.at[stage, band, :], wsem.at[stage]))
        return cps

    def fetch(k):
        @pl.when(eo_ref[k] >= 0)
        def _():
            for cp in weight_copies(eo_ref[k], k % 2):
                cp.start(priority=1)

    def x_copy(b, slot):
        return pltpu.make_async_copy(xs_hbm.at[_row_ds(b * BM, BM), :], xbuf.at[slot], xsem.at[slot])

    def y_copy(b, slot):
        return pltpu.make_async_copy(ybuf.at[slot], y_hbm.at[_row_ds(b * BM, BM), :], ysem.at[slot])

    fetch(0)
    fetch(1)
    for ahead in range(X_BUFFERS - 1):
        @pl.when(ahead < n_valid)
        def _(ahead=ahead):
            x_copy(ahead, ahead).start()

    def next_slot(s):
        return jnp.where(s == X_BUFFERS - 1, 0, s + 1)

    def block(b, carry):
        k, xslot = carry
        slot = b % 2
        e = be_ref[b]
        new_expert = jnp.logical_or(b == 0, e != be_ref[jnp.maximum(b - 1, 0)])

        @pl.when(b + X_BUFFERS - 1 < n_valid)
        def _():
            ahead_slot = xslot
            for _ in range(X_BUFFERS - 1):
                ahead_slot = next_slot(ahead_slot)
            x_copy(b + X_BUFFERS - 1, ahead_slot).start()

        @pl.when(new_expert)
        def _():
            stage = k % 2
            for cp in weight_copies(e, stage):
                cp.wait()
            for st in range(2):
                @pl.when(stage == st)
                def _(st=st):
                    wgu_bf[...] = gu_stage[st].astype(BF16)
                    wdn_bf[...] = dn_stage[st].astype(BF16)
            fetch(k + 2)

        x_copy(b, xslot).wait()

        @pl.when(b >= 2)
        def _():
            y_copy(b - 2, slot).wait()

        x = _slab_columns(xbuf.at[xslot], 0, BM).astype(BF16)
        bgu = bgu_ref[e]
        g = jnp.dot(x, wgu_bf[:, :D_EXPERT], preferred_element_type=F32) + bgu[:, :D_EXPERT]
        u = jnp.dot(x, wgu_bf[:, D_EXPERT:], preferred_element_type=F32) + bgu[:, D_EXPERT:]
        g = jnp.minimum(g, SWIGLU_LIMIT)
        u = jnp.clip(u, -SWIGLU_LIMIT, SWIGLU_LIMIT)
        act = ((u + 1.0) * (0.5 * g * (1.0 + jnp.tanh((0.5 * SWIGLU_ALPHA) * g)))).astype(BF16)
        y = jnp.dot(act, wdn_bf[...], preferred_element_type=F32) + bdn_ref[e]
        out = ybuf.at[slot]
        for s in range(ROW_SLABS):
            out[pl.ds(s, BM, stride=ROW_SLABS), :] = y[:, s * LANES:(s + 1) * LANES]
        y_copy(b, slot).start()
        return k + new_expert.astype(I32), next_slot(xslot)

    lax.fori_loop(0, n_valid, block, (jnp.int32(0), jnp.int32(0)))

    @pl.when(n_valid >= 2)
    def _():
        y_copy(n_valid - 2, n_valid % 2).wait()
    y_copy(n_valid - 1, (n_valid - 1) % 2).wait()

    ybuf[0] = jnp.zeros((BM * ROW_SLABS, LANES), F32)

    def zero_blocks(wait):
        def one(b, carry):
            cp = y_copy(b, 0)
            cp.wait() if wait else cp.start()
            return carry
        lax.fori_loop(n_valid, NB, one, 0)

    zero_blocks(False)
    zero_blocks(True)


def _moe_blocks(tables, xs, w_gu, b_gu, w_down, b_down, li):
    grid_spec = pltpu.PrefetchScalarGridSpec(
        num_scalar_prefetch=3,
        grid=(1,),
        in_specs=[
            pl.BlockSpec(memory_space=pl.ANY),
            pl.BlockSpec(memory_space=pl.ANY),
            pl.BlockSpec((None, N_EXPERTS, 1, 2 * D_EXPERT), lambda i, *_: (li, 0, 0, 0)),
            pl.BlockSpec(memory_space=pl.ANY),
            pl.BlockSpec((None, N_EXPERTS, 1, D_MODEL), lambda i, *_: (li, 0, 0, 0)),
        ],
        out_specs=pl.BlockSpec(memory_space=pl.ANY),
        scratch_shapes=[
            pltpu.VMEM((X_BUFFERS, BM * ROW_SLABS, LANES), F32),
            pltpu.VMEM((2, BM * ROW_SLABS, LANES), F32),
            pltpu.VMEM((2, D_MODEL, 2 * D_EXPERT), F32),
            pltpu.VMEM((2, D_EXPERT, D_MODEL), F32),
            pltpu.VMEM((D_MODEL, 2 * D_EXPERT), BF16),
            pltpu.VMEM((D_EXPERT, D_MODEL), BF16),
            pltpu.SemaphoreType.DMA((2,)),
            pltpu.SemaphoreType.DMA((X_BUFFERS,)),
            pltpu.SemaphoreType.DMA((2,)),
        ],
    )
    return pl.pallas_call(
        functools.partial(_moe_kernel, li=li),
        grid_spec=grid_spec,
        out_shape=jax.ShapeDtypeStruct((N_SLOTS * ROW_SLABS, LANES), F32),
        compiler_params=_cparams(1),
        name="moe_experts",
    )(tables["block_expert"], tables["expert_order"], tables["n_valid"], xs, w_gu,
      b_gu.reshape(DEPTH, N_EXPERTS, 1, 2 * D_EXPERT), w_down, b_down.reshape(DEPTH, N_EXPERTS, 1, D_MODEL))


def _combine_kernel(start_ref, len_ref, off_ref, y_hbm, x1_ref, ri_ref, rf_ref, mod_ref,
                    lng_ref, lnb_ref, oc_ref, od_ref, *scratch):
    bufs, sem = scratch[:RT_PER_STEP], scratch[RT_PER_STEP]
    i = pl.program_id(0)
    n_steps = pl.num_programs(0)

    def fetch(tile, t):
        _tile_run_copies(tile, start_ref, len_ref, off_ref, y_hbm, bufs[t], sem.at[t], to_sorted=False)

    @pl.when(i == 0)
    def _():
        for t in range(FETCH_AHEAD):
            fetch(t, t)

    m = mod_ref[pl.ds(_mod_row(i * RT_PER_STEP // RT_PER_TM), 1), :]
    g2 = m[:, 5 * D_MODEL:6 * D_MODEL]
    col = lax.broadcasted_iota(I32, (RT, RT_ROWS), 1).astype(F32)
    outs = []
    for t in range(RT_PER_STEP):
        rows = slice(t * RT, (t + 1) * RT)
        fetch(jnp.minimum(i * RT_PER_STEP + t + FETCH_AHEAD, N_RT - 1), (t + FETCH_AHEAD) % RT_PER_STEP)
        _tile_runs_wait(bufs[t], sem.at[t])
        pos = _packed_positions(ri_ref[rows, :])
        gates = rf_ref[rows, :]
        mix = jnp.zeros((RT, RT_ROWS), F32)
        for k in range(TOP_K):
            mix = mix + jnp.where(col == pos[k], gates[:, k:k + 1], 0.0)
        ffn = jnp.dot(mix.astype(BF16), _slab_columns(bufs[t], 0, RT_ROWS).astype(BF16),
                      preferred_element_type=F32)
        outs.append(_layer_norm_rows(DN_ALPHA * x1_ref[rows, :] + g2 * ffn, lng_ref[...], lnb_ref[...]))
    out = jnp.concatenate(outs, axis=0)
    n_ctx_steps = N_CTX // (RT * RT_PER_STEP)

    @pl.when(i < n_ctx_steps)
    def _():
        oc_ref[...] = out

    @pl.when(i >= n_ctx_steps)
    def _():
        od_ref[...] = out

    @pl.when(i == n_steps - 1)
    def _():
        for t in range(FETCH_AHEAD):
            _tile_runs_wait(bufs[t], sem.at[t])


def _combine(tables, y, x1, route_i, route_f, mod, ln_g, ln_b, li):
    rows = RT * RT_PER_STEP
    grid_spec = pltpu.PrefetchScalarGridSpec(
        num_scalar_prefetch=3,
        grid=(N_RT // RT_PER_STEP,),
        in_specs=[
            pl.BlockSpec(memory_space=pl.ANY),
            pl.BlockSpec((rows, D_MODEL), lambda i, *_: (i, 0)),
            pl.BlockSpec((rows, LANES), lambda i, *_: (i, 0)),
            pl.BlockSpec((rows, LANES), lambda i, *_: (i, 0)),
            pl.BlockSpec((None, SUBLANES, 6 * D_MODEL), lambda i, *_: (li, 0, 0)),
            _layer_block(li, (1, D_MODEL), 1),
            _layer_block(li, (1, D_MODEL), 1),
        ],
        out_specs=_pair_specs(rows, D_MODEL, N_CTX // rows),
        scratch_shapes=[pltpu.VMEM((RT_ROWS * ROW_SLABS, LANES), F32)] * RT_PER_STEP + [
            pltpu.SemaphoreType.DMA((RT_PER_STEP,)),
        ],
    )
    return pl.pallas_call(
        _combine_kernel,
        grid_spec=grid_spec,
        out_shape=[jax.ShapeDtypeStruct((N_CTX, D_MODEL), F32), jax.ShapeDtypeStruct((N_DEC, D_MODEL), F32)],
        compiler_params=_cparams(1),
        name="moe_combine",
    )(tables["start"], tables["len"], tables["off"], y, x1, route_i, route_f, mod, ln_g, ln_b)


def _routing_tables(tile_cnt):
    cnt = tile_cnt[:, 0, :N_EXPERTS]
    totals = jnp.sum(cnt, axis=0)
    padded = (totals + BM - 1) // BM * BM
    pends = jnp.cumsum(padded)
    pstarts = pends - padded
    start = pstarts[None, :] + jnp.cumsum(cnt, axis=0) - cnt
    off = jnp.cumsum(cnt, axis=1) - cnt
    n_valid = (pends[-1] // BM).astype(I32)
    block_start = jnp.arange(NB, dtype=I32) * BM
    block_e = jnp.minimum(jnp.sum(block_start[:, None] >= pends[None, :], axis=1), N_EXPERTS - 1)
    last_e = block_e[jnp.maximum(n_valid - 1, 0)]
    block_e = jnp.where(jnp.arange(NB) < n_valid, block_e, last_e)
    ids = jnp.arange(N_EXPERTS, dtype=I32)
    order = jnp.sort(jnp.where(totals > 0, ids, N_EXPERTS))
    order = jnp.concatenate([jnp.where(order < N_EXPERTS, order, -1), jnp.full((2,), -1, I32)])
    return {
        "start": (start.reshape(-1) * ROW_SLABS).astype(I32),
        "len": (cnt.reshape(-1) * ROW_SLABS).astype(I32),
        "off": (off.reshape(-1) * ROW_SLABS).astype(I32),
        "pad_start": ((pstarts + totals) * ROW_SLABS).astype(I32),
        "pad_len": ((padded - totals) * ROW_SLABS).astype(I32),
        "n_valid": n_valid.reshape(1),
        "block_expert": block_e.astype(I32),
        "expert_order": order.astype(I32),
    }


def _stacked_params(p):
    eye = jnp.eye(N_LRU_BLOCKS, dtype=F32)
    pad = LANES - N_EXPERTS
    return {
        "lam": p["lam"],
        "subln_col": p["subln_g"].reshape(DEPTH, HEAD_DIM, 1),
        "sgu_ln_g": p["sgu_ln_g"].reshape(DEPTH, 1, SGU_WIDTH),
        "sgu_ln_b": p["sgu_ln_b"].reshape(DEPTH, 1, SGU_WIDTH),
        "w_spatial": p["w_spatial"],
        "b_spatial_full": jnp.repeat(jnp.swapaxes(p["b_spatial"], 1, 2), SGU_GROUP, axis=2),
        "conv_w": p["conv_w"],
        "conv_b": p["conv_b"].reshape(DEPTH, 1, LRU_WIDTH),
        "w_rg_full": jnp.einsum("ldkgio,gh->lgidkho", p["w_rg"], eye).reshape(DEPTH, LRU_WIDTH, 4 * LRU_WIDTH),
        "b_rg_full": p["b_rg"].reshape(DEPTH, 1, 4 * LRU_WIDTH),
        "lru_log": p["lru_log"],
        "w_out_all": p["w_out"],
        "ln_g": p["ln_g"].reshape(DEPTH, 2, 1, D_MODEL),
        "ln_b": p["ln_b"].reshape(DEPTH, 2, 1, D_MODEL),
        "w_router_pad": jnp.pad(p["w_router"], ((0, 0), (0, 0), (0, pad))),
        "b_router_pad": jnp.pad(p["b_router"], ((0, 0), (0, pad))).reshape(DEPTH, 1, LANES),
    }


def kernel(x_prompt, x_sample, cache_k, cache_v, state_lru, c, c_ctx, w_mod, b_mod, w_in, lam, subln_g, sgu_ln_g, sgu_ln_b, w_spatial, b_spatial, conv_w, conv_b, w_rg, b_rg, lru_log, w_out, ln_g, ln_b, w_router, b_router, w_gu, b_gu, w_down, b_down):
    lp = _stacked_params(dict(
        lam=lam, subln_g=subln_g, sgu_ln_g=sgu_ln_g, sgu_ln_b=sgu_ln_b, w_spatial=w_spatial, b_spatial=b_spatial,
        conv_w=conv_w, conv_b=conv_b, w_rg=w_rg, b_rg=b_rg, lru_log=lru_log, w_out=w_out, ln_g=ln_g, ln_b=ln_b,
        w_router=w_router, b_router=b_router))
    cvec8 = jnp.concatenate([c_ctx[None, :], c, jnp.zeros((SUBLANES - 1 - DEC_BATCH, D_MODEL), F32)], axis=0)
    mod = _modulation(cvec8, w_mod, b_mod)
    x_pair = (x_prompt.reshape(N_CTX, D_MODEL), x_sample.reshape(N_DEC, D_MODEL))
    tables = _rope_tables()
    zero_state = jnp.zeros((BATCH, 1, 2, LRU_WIDTH), F32)
    prev_kv = None
    new_s = []
    for li in range(DEPTH):
        lam_init = 0.8 - 0.6 * math.exp(-0.3 * li)
        proj = _input_projection(*x_pair, mod, w_in, li)
        att_ctx, kc, vc = _context_attention(proj, lp["lam"], lp["subln_col"], li, lam_init, prev_kv)
        prev_kv = (kc, vc)
        att_dec = _denoise_attention(proj, cache_k, cache_v, lp["lam"], lp["subln_col"], li, lam_init, tables)
        sgu_ctx, lru_ctx, h_ctx = _mixers(proj, zero_state, 0, lp, li, SEQ, BATCH, 0)
        sgu_dec, lru_dec, _ = _mixers(proj, state_lru, li, lp, li, DEC_SEQ, DEC_BATCH, N_CTX // DEC_SEQ)
        x1, h2, route_i, route_f, tile_cnt = _output_projection(
            x_pair, (att_ctx, att_dec), (sgu_ctx, sgu_dec), (lru_ctx, lru_dec), mod, lp, li)
        rt = _routing_tables(tile_cnt)
        xs = _dispatch(rt, h2, route_i)
        y = _moe_blocks(rt, xs, w_gu, b_gu, w_down, b_down, li)
        x_pair = _combine(rt, y, x1, route_i, route_f, mod, lp["ln_g"], lp["ln_b"], li)
        new_s.append(h_ctx)
    y_prompt = x_pair[0].reshape(BATCH, SEQ, D_MODEL)
    y_sample = x_pair[1].reshape(DEC_BATCH, DEC_SEQ, D_MODEL)
    return (y_prompt, y_sample, prev_kv[0], prev_kv[1], jnp.stack(new_s, axis=1))
```
